```python
import numpy as np
import jax
import jax.numpy as jnp
from jax import lax

D_MODEL = 2048
BATCH = 8
SEQ = 2048
DEPTH = 1

D_MIX = D_MODEL
NSA_HEADS = 8
NSA_KV_HEADS = 2
NSA_HEAD_DIM = D_MIX // 2 // NSA_HEADS
CMP_LEN = 32
CMP_STRIDE = 16
CMP_HIDDEN = NSA_HEAD_DIM
SEL_BLOCK = 64
SEL_TOPK = 8
WINDOW = 512
ATT_QBLK = 128
SEL_QBLK = 64
ROPE_THETA = 500000.0
ROPE_DIV = 4
MLSTM_HEADS = 4
MLSTM_HEAD_DIM = D_MIX // 2 // MLSTM_HEADS
MLSTM_CHUNK = 64
CONV_WIDTH = 4
MOE_GROUPS = 8
EXPERTS_PER_GROUP = 8
N_EXPERTS = MOE_GROUPS * EXPERTS_PER_GROUP
EXPERT_TOPK = 2
D_EXPERT = D_MODEL // 4
MOE_BLOCK = 128
NORM_EPS = 1e-6
NEG_INF = -1e30
FORCE_SCORE = 1000.0

NSA_Q = NSA_HEADS * NSA_HEAD_DIM
NSA_KV = NSA_KV_HEADS * NSA_HEAD_DIM
MLSTM_W = MLSTM_HEADS * MLSTM_HEAD_DIM
D_IN_PROJ = NSA_Q + 6 * NSA_KV + 3 * NSA_HEADS + 4 * MLSTM_W + 2 * MLSTM_HEADS

kernel_name = 'nsa_mlstm_hier_moe_hybrid'


def rms_norm(x, g):
    xf = x.astype(jnp.float32)
    y = xf * lax.rsqrt(jnp.mean(xf * xf, axis=-1, keepdims=True) + NORM_EPS)
    return (y * g.astype(jnp.float32)).astype(x.dtype)


def masked_softmax(s, mask):
    s = jnp.where(mask, s.astype(jnp.float32), NEG_INF)
    return jax.nn.softmax(s, axis=-1) * mask


def partial_rope(x, positions):
    rd = x.shape[-1] // ROPE_DIV
    half = rd // 2
    inv_freq = jnp.power(jnp.float32(ROPE_THETA), -jnp.arange(half, dtype=jnp.float32) * 2.0 / rd)
    ang = positions.astype(jnp.float32)[:, :, None] * inv_freq
    cos = jnp.cos(ang)[:, :, None, :]
    sin = jnp.sin(ang)[:, :, None, :]
    xr = x[..., :rd].astype(jnp.float32)
    x1, x2 = xr[..., :half], xr[..., half:]
    rot = jnp.concatenate([x1 * cos - x2 * sin, x2 * cos + x1 * sin], axis=-1).astype(x.dtype)
    return jnp.concatenate([rot, x[..., rd:]], axis=-1)


def compress_blocks(kv, pe, w1, w2):
    B, S, G, hd = kv.shape
    n_cmp = (S - CMP_LEN) // CMP_STRIDE + 1
    idx = np.arange(n_cmp)[:, None] * CMP_STRIDE + np.arange(CMP_LEN)[None, :]
    blocks = kv[:, idx] + pe[:, None, :]
    blocks = blocks.transpose(0, 1, 3, 2, 4).reshape(B, n_cmp, G, CMP_LEN * hd)
    return jax.nn.silu(blocks @ w1) @ w2


def selection_attention(qg, k, v, sel_idx, scale):
    B, S, G, J, hd = qg.shape
    n_sel = S // SEL_BLOCK
    k_top = sel_idx.shape[-1]
    kb = k.reshape(B, n_sel, SEL_BLOCK, G, hd).transpose(0, 3, 1, 2, 4)
    vb = v.reshape(B, n_sel, SEL_BLOCK, G, hd).transpose(0, 3, 1, 2, 4)
    nq = S // SEL_QBLK
    q_chunks = qg.reshape(B, nq, SEL_QBLK, G, J, hd).transpose(1, 0, 2, 3, 4, 5)
    i_chunks = sel_idx.reshape(B, G, nq, SEL_QBLK, k_top).transpose(2, 0, 1, 3, 4)
    t_chunks = jnp.arange(S).reshape(nq, SEL_QBLK)
    bi = jnp.arange(B)[:, None, None, None]
    gi = jnp.arange(G)[None, :, None, None]
    offs = jnp.arange(SEL_BLOCK)

    def one_chunk(args):
        qc, ic, tc = args
        kg = kb[bi, gi, ic]
        vg = vb[bi, gi, ic]
        s = jnp.einsum('bqgjd,bgqkld->bgjqkl', qc, kg) * scale
        pos = ic[..., None] * SEL_BLOCK + offs
        mask = (pos <= tc[None, None, :, None, None])[:, :, None]
        p = masked_softmax(s.reshape(B, G, J, SEL_QBLK, k_top * SEL_BLOCK),
                           mask.reshape(B, G, 1, SEL_QBLK, k_top * SEL_BLOCK))
        p = p.reshape(B, G, J, SEL_QBLK, k_top, SEL_BLOCK).astype(vg.dtype)
        return jnp.einsum('bgjqkl,bgqkld->bqgjd', p, vg)

    o = lax.map(one_chunk, (q_chunks, i_chunks, t_chunks))
    return o.transpose(1, 0, 2, 3, 4, 5).reshape(B, S, G, J, hd)


def window_attention(qg, k, v, scale):
    B, S, G, J, hd = qg.shape
    nc = S // ATT_QBLK
    nb = WINDOW // ATT_QBLK

    def band(a):
        ab = a.reshape(B, nc, ATT_QBLK, G, hd)
        ap = jnp.pad(ab, ((0, 0), (nb, 0), (0, 0), (0, 0), (0, 0)))
        return jnp.concatenate([ap[:, i:i + nc] for i in range(nb + 1)], axis=2)

    kb, vb = band(k), band(v)
    qpos = jnp.arange(S).reshape(nc, ATT_QBLK)
    kpos = (jnp.arange(nc)[:, None] - nb) * ATT_QBLK + jnp.arange((nb + 1) * ATT_QBLK)[None, :]
    diff = qpos[:, :, None] - kpos[:, None, :]
    mask = (diff >= 0) & (diff < WINDOW) & (kpos[:, None, :] >= 0)
    s = jnp.einsum('bcqgjd,bckgd->bcgjqk', qg.reshape(B, nc, ATT_QBLK, G, J, hd), kb) * scale
    p = masked_softmax(s, mask[None, :, None, None]).astype(v.dtype)
    o = jnp.einsum('bcgjqk,bckgd->bcqgjd', p, vb)
    return o.reshape(B, S, G, J, hd)


def nsa_mixer(q, k_cmp, v_cmp, k_sel, v_sel, k_win, v_win, gates,
              cmp_pe_k, cmp_pe_v, cmp_wk1, cmp_wk2, cmp_wv1, cmp_wv2):
    B, S, H, hd = q.shape
    G = k_cmp.shape[2]
    J = H // G
    scale = hd ** -0.5
    qg = q.reshape(B, S, G, J, hd)
    t = jnp.arange(S)
    kc = compress_blocks(k_cmp, cmp_pe_k, cmp_wk1, cmp_wk2)
    vc = compress_blocks(v_cmp, cmp_pe_v, cmp_wv1, cmp_wv2)
    n_cmp = kc.shape[1]
    s = jnp.einsum('bsgjd,bngd->bgjsn', qg, kc) * scale
    ends = jnp.arange(n_cmp) * CMP_STRIDE + (CMP_LEN - 1)
    p_cmp = masked_softmax(s, ends[None, :] <= t[:, None])
    o_cmp = jnp.einsum('bgjsn,bngd->bsgjd', p_cmp.astype(q.dtype), vc)
    n_sel = S // SEL_BLOCK
    cs = np.arange(n_cmp) * CMP_STRIDE
    ss = np.arange(n_sel) * SEL_BLOCK
    shared = np.minimum(cs[:, None] + CMP_LEN, ss[None, :] + SEL_BLOCK) - np.maximum(cs[:, None], ss[None, :])
    cover = jnp.asarray(np.clip(shared, 0, None) / CMP_LEN, dtype=jnp.float32)
    imp = jnp.einsum('bgjsn,nm->bgsm', p_cmp, cover)
    jt = t // SEL_BLOCK
    jb = jnp.arange(n_sel)
    valid = jb[None, :] <= jt[:, None]
    forced = (jb[None, :] == 0) | (jb[None, :] == jt[:, None]) | (jb[None, :] == jt[:, None] - 1)
    score = jnp.where(valid, imp + FORCE_SCORE * forced, -1.0)
    _, sel_idx = lax.top_k(score, min(SEL_TOPK, n_sel))
    o_sel = selection_attention(qg, k_sel, v_sel, sel_idx, scale)
    o_win = window_attention(qg, k_win, v_win, scale)
    g = jax.nn.sigmoid(gates.astype(jnp.float32)).reshape(B, S, G, J, 3).astype(q.dtype)
    o = g[..., 0:1] * o_cmp + g[..., 1:2] * o_sel + g[..., 2:3] * o_win
    return o.reshape(B, S, H * hd)


def causal_conv(x, w, b):
    C = x.shape[-1]
    y = lax.conv_general_dilated(x, w[:, None, :], window_strides=(1,), padding=[(w.shape[0] - 1, 0)],
                                 dimension_numbers=('NWC', 'WIO', 'NWC'), feature_group_count=C)
    return y + b


def mlstm_chunkwise(q, k, v, i_pre, f_pre):
    B, H, S, d = q.shape
    L = MLSTM_CHUNK
    nc = S // L
    qf = q.astype(jnp.float32) * d ** -0.5
    kf = k.astype(jnp.float32)
    vf = v.astype(jnp.float32)
    ig = i_pre.astype(jnp.float32)
    log_f = jax.nn.log_sigmoid(f_pre.astype(jnp.float32))

    def to_chunks(a):
        return jnp.moveaxis(a.reshape((B, H, nc, L) + a.shape[3:]), 2, 0)

    causal = jnp.tril(jnp.ones((L, L), dtype=bool))

    def step(carry, xs):
        C, n, m = carry
        qc, kc, vc, ic, fc = xs
        b = jnp.cumsum(fc, axis=-1)
        d_log = jnp.where(causal, b[..., :, None] - b[..., None, :] + ic[..., None, :], NEG_INF)
        inter = b + m[..., None]
        m_t = jnp.maximum(inter, jnp.max(d_log, axis=-1))
        w_intra = jnp.exp(d_log - m_t[..., None])
        w_inter = jnp.exp(inter - m_t)
        qk = jnp.einsum('bhtd,bhsd->bhts', qc, kc) * w_intra
        num = jnp.einsum('bhts,bhsv->bhtv', qk, vc) + w_inter[..., None] * jnp.einsum('bhvk,bhtk->bhtv', C, qc)
        den = jnp.sum(qk, axis=-1) + w_inter * jnp.einsum('bhk,bhtk->bht', n, qc)
        h = num / jnp.maximum(jnp.abs(den), jnp.exp(-m_t))[..., None]
        b_last = b[..., -1]
        w_log = b_last[..., None] - b + ic
        m_new = jnp.maximum(b_last + m, jnp.max(w_log, axis=-1))
        w_state = jnp.exp(w_log - m_new[..., None])
        decay = jnp.exp(b_last + m - m_new)
        C = decay[..., None, None] * C + jnp.einsum('bhs,bhsv,bhsk->bhvk', w_state, vc, kc)
        n = decay[..., None] * n + jnp.einsum('bhs,bhsk->bhk', w_state, kc)
        return (C, n, m_new), h

    init = (jnp.zeros((B, H, d, d), jnp.float32), jnp.zeros((B, H, d), jnp.float32), jnp.zeros((B, H), jnp.float32))
    _, hs = lax.scan(step, init, (to_chunks(qf), to_chunks(kf), to_chunks(vf), to_chunks(ig), to_chunks(log_f)))
    return jnp.moveaxis(hs, 0, 2).reshape(B, H, S, d)


def hierarchical_moe(x, w_group, b_group, w_router, b_router, w_gate, w_up, w_down):
    B, S, D = x.shape
    T = B * S
    xt = x.reshape(T, D)
    g_prob = jax.nn.softmax((xt @ w_group + b_group).astype(jnp.float32), axis=-1)
    g_w, g_idx = lax.top_k(g_prob, 1)
    e_logits = (xt @ w_router + b_router).astype(jnp.float32).reshape(T, MOE_GROUPS, EXPERTS_PER_GROUP)
    e_logits = jnp.take_along_axis(e_logits, g_idx[:, :, None], axis=1)[:, 0]
    e_w, e_local = lax.top_k(jax.nn.softmax(e_logits, axis=-1), EXPERT_TOPK)
    weight = g_w * e_w / jnp.sum(e_w, axis=-1, keepdims=True)
    expert = g_idx * EXPERTS_PER_GROUP + e_local
    A = T * EXPERT_TOPK
    e_flat = expert.reshape(A)
    w_flat = weight.reshape(A)
    tok_flat = jnp.repeat(jnp.arange(T, dtype=jnp.int32), EXPERT_TOPK)
    order = jnp.argsort(e_flat)
    se, stok, sw = e_flat[order], tok_flat[order], w_flat[order]
    counts = jnp.bincount(e_flat, length=N_EXPERTS)
    starts = jnp.cumsum(counts) - counts
    pcounts = (counts + MOE_BLOCK - 1) // MOE_BLOCK * MOE_BLOCK
    pends = jnp.cumsum(pcounts)
    pstarts = pends - pcounts
    dest = pstarts[se] + (jnp.arange(A) - starts[se])
    P = (A + MOE_BLOCK - 1) // MOE_BLOCK * MOE_BLOCK + N_EXPERTS * MOE_BLOCK
    NB = P // MOE_BLOCK
    row_tok = jnp.zeros((P,), jnp.int32).at[dest].set(stok)
    row_w = jnp.zeros((P,), jnp.float32).at[dest].set(sw)
    block_expert = jnp.minimum(jnp.searchsorted(pends, jnp.arange(NB) * MOE_BLOCK, side='right'),
                               N_EXPERTS - 1).astype(jnp.int32)

    def expert_block(args):
        xb, e = args
        hb = jax.nn.silu(xb @ w_gate[e]) * (xb @ w_up[e])
        return hb @ w_down[e]

    y = lax.map(expert_block, (xt[row_tok].reshape(NB, MOE_BLOCK, D), block_expert))
    y = y.reshape(P, D) * row_w[:, None].astype(y.dtype)
    out = jnp.zeros_like(xt).at[row_tok].add(y)
    return out.reshape(B, S, D)


def hybrid_layer(x, positions, norm1_g, w_in, cmp_pe_k, cmp_pe_v, cmp_wk1, cmp_wk2, cmp_wv1, cmp_wv2,
                 nsa_norm_g, conv_w, conv_b, b_igate, b_fgate, mlstm_norm_g, w_out, norm2_g,
                 w_group, b_group, w_router, b_router, w_exp_gate, w_exp_up, w_exp_down):
    B, S, _ = x.shape
    hd = NSA_HEAD_DIM
    h = rms_norm(x, norm1_g)
    proj = h @ w_in
    sizes = (NSA_Q, 6 * NSA_KV, 3 * NSA_HEADS, 2 * MLSTM_W, MLSTM_W, MLSTM_W, MLSTM_HEADS, MLSTM_HEADS)
    offs = np.cumsum(sizes)[:-1].tolist()
    q_a, kv_a, g_a, qk_m, v_m, o_m, i_m, f_m = jnp.split(proj, offs, axis=-1)
    q_a = partial_rope(q_a.reshape(B, S, NSA_HEADS, hd), positions)
    kv_a = kv_a.reshape(B, S, 6, NSA_KV_HEADS, hd)
    k_cmp = partial_rope(kv_a[:, :, 0], positions)
    k_sel = partial_rope(kv_a[:, :, 2], positions)
    k_win = partial_rope(kv_a[:, :, 4], positions)
    o_nsa = nsa_mixer(q_a, k_cmp, kv_a[:, :, 1], k_sel, kv_a[:, :, 3], k_win, kv_a[:, :, 5], g_a,
                      cmp_pe_k, cmp_pe_v, cmp_wk1, cmp_wk2, cmp_wv1, cmp_wv2)
    qk = jax.nn.silu(causal_conv(qk_m, conv_w, conv_b))
    q_m, k_m = jnp.split(qk, 2, axis=-1)

    def heads(a):
        return a.reshape(B, S, MLSTM_HEADS, MLSTM_HEAD_DIM).transpose(0, 2, 1, 3)

    hm = mlstm_chunkwise(heads(q_m), heads(k_m), heads(v_m),
                         (i_m + b_igate).transpose(0, 2, 1), (f_m + b_fgate).transpose(0, 2, 1))
    hm = hm.transpose(0, 2, 1, 3)
    hm = hm * lax.rsqrt(jnp.mean(hm * hm, axis=-1, keepdims=True) + NORM_EPS)
    hm = hm * mlstm_norm_g.astype(jnp.float32).reshape(MLSTM_HEADS, MLSTM_HEAD_DIM)
    o_gate = jax.nn.sigmoid(o_m.astype(jnp.float32)).reshape(B, S, MLSTM_HEADS, MLSTM_HEAD_DIM)
    o_mlstm = (hm * o_gate).reshape(B, S, MLSTM_W).astype(x.dtype)
    mixed = jnp.concatenate([rms_norm(o_nsa, nsa_norm_g), o_mlstm], axis=-1) @ w_out
    x = x + mixed
    x = x + hierarchical_moe(rms_norm(x, norm2_g), w_group, b_group, w_router, b_router,
                             w_exp_gate, w_exp_up, w_exp_down)
    return x


def setup_inputs(seed: int = 0) -> dict:
    key = jax.random.key(seed)
    ks = jax.random.split(key, 26)
    L = DEPTH
    D = D_MODEL
    hd = NSA_HEAD_DIM

    def nrm(k, shape, scale):
        return jax.random.normal(k, shape, jnp.float32) * scale

    def gain(k, shape):
        return 1.0 + nrm(k, shape, 0.02)

    return {
        'x': nrm(ks[0], (BATCH, SEQ, D), 1.0),
        'positions': (jax.random.randint(ks[1], (BATCH, 1), 0, 4096) + jnp.arange(SEQ)[None, :]).astype(jnp.int32),
        'norm1_g': gain(ks[2], (L, D)),
        'w_in': nrm(ks[3], (L, D, D_IN_PROJ), D ** -0.5),
        'cmp_pe_k': nrm(ks[4], (L, CMP_LEN, hd), 0.02),
        'cmp_pe_v': nrm(ks[5], (L, CMP_LEN, hd), 0.02),
        'cmp_wk1': nrm(ks[6], (L, CMP_LEN * hd, CMP_HIDDEN), (CMP_LEN * hd) ** -0.5),
        'cmp_wk2': nrm(ks[7], (L, CMP_HIDDEN, hd), CMP_HIDDEN ** -0.5),
        'cmp_wv1': nrm(ks[8], (L, CMP_LEN * hd, CMP_HIDDEN), (CMP_LEN * hd) ** -0.5),
        'cmp_wv2': nrm(ks[9], (L, CMP_HIDDEN, hd), CMP_HIDDEN ** -0.5),
        'nsa_norm_g': gain(ks[10], (L, NSA_Q)),
        'conv_w': nrm(ks[11], (L, CONV_WIDTH, 2 * MLSTM_W), CONV_WIDTH ** -0.5),
        'conv_b': nrm(ks[12], (L, 2 * MLSTM_W), 0.02),
        'b_igate': nrm(ks[13], (L, MLSTM_HEADS), 0.1),
        'b_fgate': jnp.linspace(3.0, 6.0, MLSTM_HEADS, dtype=jnp.float32)[None, :] + nrm(ks[14], (L, MLSTM_HEADS), 0.1),
        'mlstm_norm_g': gain(ks[15], (L, MLSTM_W)),
        'w_out': nrm(ks[16], (L, D_MIX, D), D_MIX ** -0.5),
        'norm2_g': gain(ks[17], (L, D)),
        'w_group': nrm(ks[18], (L, D, MOE_GROUPS), D ** -0.5),
        'b_group': nrm(ks[19], (L, MOE_GROUPS), 0.01),
        'w_router': nrm(ks[20], (L, D, N_EXPERTS), D ** -0.5),
        'b_router': nrm(ks[21], (L, N_EXPERTS), 0.01),
        'w_exp_gate': nrm(ks[22], (L, N_EXPERTS, D, D_EXPERT), D ** -0.5),
        'w_exp_up': nrm(ks[23], (L, N_EXPERTS, D, D_EXPERT), D ** -0.5),
        'w_exp_down': nrm(ks[24], (L, N_EXPERTS, D_EXPERT, D), D_EXPERT ** -0.5),
        'final_norm_g': gain(ks[25], (D,)),
    }


def reference(x, positions, norm1_g, w_in, cmp_pe_k, cmp_pe_v, cmp_wk1, cmp_wk2, cmp_wv1, cmp_wv2,
              nsa_norm_g, conv_w, conv_b, b_igate, b_fgate, mlstm_norm_g, w_out, norm2_g,
              w_group, b_group, w_router, b_router, w_exp_gate, w_exp_up, w_exp_down, final_norm_g):
    for l in range(DEPTH):
        x = hybrid_layer(x, positions, norm1_g[l], w_in[l], cmp_pe_k[l], cmp_pe_v[l], cmp_wk1[l], cmp_wk2[l],
                         cmp_wv1[l], cmp_wv2[l], nsa_norm_g[l], conv_w[l], conv_b[l], b_igate[l], b_fgate[l],
                         mlstm_norm_g[l], w_out[l], norm2_g[l], w_group[l], b_group[l], w_router[l], b_router[l],
                         w_exp_gate[l], w_exp_up[l], w_exp_down[l])
    return rms_norm(x, final_norm_g)
```

```python
import functools

import numpy as np
import jax
import jax.numpy as jnp
from jax import lax
from jax.experimental import pallas as pl
from jax.experimental.pallas import tpu as pltpu

F32 = jnp.float32
BF16 = jnp.bfloat16
U32 = jnp.uint32

D_MODEL = 2048
NSA_HEADS = 8
NSA_GROUPS = 2
NSA_J = NSA_HEADS // NSA_GROUPS
HD = 128
CMP_LEN = 32
CMP_STRIDE = 16
SEL_BLOCK = 64
SEL_TOPK = 8
WINDOW = 512
ROPE_THETA = 500000.0
ROPE_DIM = 32
ROPE_HALF = 16
ML_HEADS = 4
ML_HD = 256
CONV_W = 4
MOE_GROUPS = 8
EPG = 8
N_EXPERTS = 64
D_EXPERT = 512
SEL_BLOCK_LOG2 = 6
EPG_LOG2 = 3
assert (1 << SEL_BLOCK_LOG2) == SEL_BLOCK and (1 << EPG_LOG2) == EPG
NORM_EPS = 1e-6
NEG_INF = -1e30
FORCE_SCORE = 1000.0

COL_Q = 0
COL_KV = 1024
COL_QKM = 2560
COL_VM = 4608
COL_OM = 5632
N_MAIN = 6656

LANE = 128
VMEM_LIMIT = 56 * 1024 * 1024

TM_IN = 1024
TN_IN = 512
TS_ROPE = 256
TQ = 128
KV_SEL = 256
WIN_KEYS = WINDOW + TQ
ML_CHUNK = 256
TM_OUT = 512
BM = 256
TC = 256

NT_DIMS = (((1,), (1,)), ((), ()))
TN_DIMS = (((0,), (0,)), ((), ()))


def _cparams(sem):
    return pltpu.CompilerParams(dimension_semantics=sem, vmem_limit_bytes=VMEM_LIMIT)


def _sigmoid(x):
    return 1.0 / (1.0 + jnp.exp(-x))


def _inproj_kernel(x_ref, g_ref, w_ref, ws_ref, o_ref, os_ref, h_ref):
    @pl.when(pl.program_id(1) == 0)
    def _():
        x = x_ref[...]
        ms = jnp.mean(x * x, axis=-1, keepdims=True)
        h_ref[...] = (x * lax.rsqrt(ms + NORM_EPS) * g_ref[...]).astype(BF16)
        os_ref[...] = jnp.dot(h_ref[...], ws_ref[...], preferred_element_type=F32)

    o_ref[...] = jnp.dot(h_ref[...], w_ref[...], preferred_element_type=F32).astype(o_ref.dtype)


def _in_proj(x2, g1, w_main, w_small):
    T = x2.shape[0]
    return pl.pallas_call(
        _inproj_kernel,
        grid=(T // TM_IN, N_MAIN // TN_IN),
        in_specs=[
            pl.BlockSpec((TM_IN, D_MODEL), lambda m, n: (m, 0)),
            pl.BlockSpec((1, D_MODEL), lambda m, n: (0, 0)),
            pl.BlockSpec((D_MODEL, TN_IN), lambda m, n: (0, n)),
            pl.BlockSpec((D_MODEL, LANE), lambda m, n: (0, 0)),
        ],
        out_specs=[
            pl.BlockSpec((TM_IN, TN_IN), lambda m, n: (m, n)),
            pl.BlockSpec((TM_IN, LANE), lambda m, n: (m, 0)),
        ],
        out_shape=[
            jax.ShapeDtypeStruct((T, N_MAIN), BF16),
            jax.ShapeDtypeStruct((T, LANE), F32),
        ],
        scratch_shapes=[pltpu.VMEM((TM_IN, D_MODEL), BF16)],
        compiler_params=_cparams(("parallel", "arbitrary")),
        name="in_proj",
    )(x2, g1, w_main, w_small)


def _rope_kernel(pos_ref, invf_ref, q_ref, kc_ref, ks_ref, kw_ref, qo_ref, ko_ref):
    ang = pos_ref[...] * invf_ref[...]
    c = jnp.cos(ang)
    s = jnp.sin(ang)
    lane = lax.broadcasted_iota(jnp.int32, ang.shape, 1)
    sa = jnp.where(lane < ROPE_HALF, -s, 0.0)
    sb = jnp.where(lane < ROPE_HALF, 0.0, s)

    def rope(x):
        return x * c + pltpu.roll(x, LANE - ROPE_HALF, 1) * sa + pltpu.roll(x, ROPE_HALF, 1) * sb

    scale = HD ** -0.5
    for h in range(NSA_HEADS):
        sl = slice(h * HD, (h + 1) * HD)
        qo_ref[:, sl] = (rope(q_ref[:, sl].astype(F32)) * scale).astype(BF16)
    for i, r in enumerate((kc_ref, ks_ref, kw_ref)):
        for g in range(NSA_GROUPS):
            sl = slice(g * HD, (g + 1) * HD)
            so = slice(i * 2 * HD + g * HD, i * 2 * HD + (g + 1) * HD)
            ko_ref[:, so] = rope(r[:, sl].astype(F32)).astype(BF16)


def _rope(posb, invf, proj):
    T = proj.shape[0]
    kvb = COL_KV // 256
    return pl.pallas_call(
        _rope_kernel,
        grid=(T // TS_ROPE,),
        in_specs=[
            pl.BlockSpec((TS_ROPE, LANE), lambda i: (i, 0)),
            pl.BlockSpec((1, LANE), lambda i: (0, 0)),
            pl.BlockSpec((TS_ROPE, 1024), lambda i: (i, 0)),
            pl.BlockSpec((TS_ROPE, 256), lambda i: (i, kvb + 0)),
            pl.BlockSpec((TS_ROPE, 256), lambda i: (i, kvb + 2)),
            pl.BlockSpec((TS_ROPE, 256), lambda i: (i, kvb + 4)),
        ],
        out_specs=[
            pl.BlockSpec((TS_ROPE, 1024), lambda i: (i, 0)),
            pl.BlockSpec((TS_ROPE, 768), lambda i: (i, 0)),
        ],
        out_shape=[
            jax.ShapeDtypeStruct((T, 1024), BF16),
            jax.ShapeDtypeStruct((T, 768), BF16),
        ],
        compiler_params=_cparams(("parallel",)),
        name="rope",
    )(posb, invf, proj, proj, proj, proj)


def _compress_kernel(k_ref, v_ref, pek_ref, pev_ref, w1k_ref, w2k_ref, w1v_ref, w2v_ref,
                     kc_ref, vc_ref, xs_ref):
    S = k_ref.shape[0]
    n_blk = S // CMP_STRIDE
    for src, pe, w1, w2, dst in ((k_ref, pek_ref, w1k_ref, w2k_ref, kc_ref),
                                 (v_ref, pev_ref, w1v_ref, w2v_ref, vc_ref)):
        xs_ref[0:S, :] = src[...].astype(F32)
        xs_ref[S:S + CMP_LEN, :] = jnp.zeros((CMP_LEN, HD), F32)
        acc = jnp.zeros((n_blk, HD), F32)
        for l in range(CMP_LEN):
            a = xs_ref[pl.ds(l, n_blk, stride=CMP_STRIDE), :] + pe[l:l + 1, :]
            acc = acc + jnp.dot(a.astype(BF16), w1[l * HD:(l + 1) * HD, :],
                                preferred_element_type=F32)
        hid = acc * _sigmoid(acc)
        out = jnp.dot(hid.astype(BF16), w2[...], preferred_element_type=F32)
        dst[0, 0] = out.astype(BF16)


def _compress(k_r, proj, B, S, pek, pev, w1k, w2k, w1v, w2v):
    n_blk = S // CMP_STRIDE
    vcol = (COL_KV + 256) // HD
    full = lambda shape: pl.BlockSpec(shape, lambda b, g: tuple(0 for _ in shape))
    return pl.pallas_call(
        _compress_kernel,
        grid=(B, NSA_GROUPS),
        in_specs=[
            pl.BlockSpec((S, HD), lambda b, g: (b, g)),
            pl.BlockSpec((S, HD), lambda b, g: (b, vcol + g)),
            full((CMP_LEN, HD)), full((CMP_LEN, HD)),
            full((CMP_LEN * HD, HD)), full((HD, HD)),
            full((CMP_LEN * HD, HD)), full((HD, HD)),
        ],
        out_specs=[
            pl.BlockSpec((1, 1, n_blk, HD), lambda b, g: (b, g, 0, 0)),
            pl.BlockSpec((1, 1, n_blk, HD), lambda b, g: (b, g, 0, 0)),
        ],
        out_shape=[
            jax.ShapeDtypeStruct((B, NSA_GROUPS, n_blk, HD), BF16),
            jax.ShapeDtypeStruct((B, NSA_GROUPS, n_blk, HD), BF16),
        ],
        scratch_shapes=[pltpu.VMEM((S + CMP_LEN, HD), F32)],
        compiler_params=_cparams(("parallel", "parallel")),
        name="compress",
    )(k_r, proj, pek, pev, w1k, w2k, w1v, w2v)


def _nsa_kernel(q_ref, kc_ref, vc_ref, ks_ref, vs_ref, kw_ref, vw_ref, sm_ref, cover_ref, e_ref,
                ng_ref, o_ref, obuf_ref):
    qi = pl.program_id(1)
    q0 = qi * TQ
    R = NSA_J * TQ
    n_cmp = kc_ref.shape[2]

    def row_t(shape):
        r = lax.broadcasted_iota(jnp.int32, shape, 0)
        return q0 + (r & (TQ - 1))

    gates = _sigmoid(sm_ref[...])
    ssq = jnp.zeros((TQ, 1), F32)

    for g in range(NSA_GROUPS):
        qg = jnp.concatenate([q_ref[:, (g * NSA_J + j) * HD:(g * NSA_J + j + 1) * HD]
                              for j in range(NSA_J)], axis=0)

        s = lax.dot_general(qg, kc_ref[0, g], NT_DIMS, preferred_element_type=F32)
        n_lane = lax.broadcasted_iota(jnp.int32, (R, n_cmp), 1)
        cmask = (n_lane * CMP_STRIDE + (CMP_LEN - 1)) <= row_t((R, n_cmp))
        s = jnp.where(cmask, s, NEG_INF)
        m = jnp.max(s, axis=1, keepdims=True)
        e = jnp.where(cmask, jnp.exp(s - m), 0.0)
        l = jnp.sum(e, axis=1, keepdims=True)
        p = (e / jnp.where(l > 0.0, l, 1.0)).astype(BF16)
        o_cmp = jnp.dot(p, vc_ref[0, g], preferred_element_type=F32)
        impr = jnp.dot(p, cover_ref[...], preferred_element_type=F32)
        imp = impr[0:TQ]
        for j in range(1, NSA_J):
            imp = imp + impr[j * TQ:(j + 1) * TQ]

        m_lane = lax.broadcasted_iota(jnp.int32, (TQ, LANE), 1)
        jt = (q0 + lax.broadcasted_iota(jnp.int32, (TQ, LANE), 0)) >> SEL_BLOCK_LOG2
        forced = jnp.where(m_lane == 0, FORCE_SCORE,
                           jnp.where(m_lane == jt, FORCE_SCORE,
                                     jnp.where(m_lane == jt - 1, FORCE_SCORE, 0.0)))
        score = jnp.where(m_lane <= jt, imp + forced, -1.0)
        n_sel = ks_ref.shape[0] // SEL_BLOCK
        score = jnp.where(m_lane < n_sel, score, -2.0)
        rank = jnp.zeros((TQ, LANE), F32)
        for mp in range(n_sel):
            col = score[:, mp:mp + 1]
            ge = jnp.where(col >= score, 1.0, 0.0)
            gt = jnp.where(col > score, 1.0, 0.0)
            rank = rank + jnp.where(m_lane > mp, ge, gt)
        sel = jnp.where(rank < float(min(SEL_TOPK, n_sel)), 1.0, 0.0)
        sel = jnp.where(m_lane < n_sel, sel, 0.0).astype(BF16)
        sel4 = jnp.concatenate([sel] * NSA_J, axis=0)

        trow_s = row_t((R, KV_SEL))
        klane_s = lax.broadcasted_iota(jnp.int32, (R, KV_SEL), 1)

        def sel_body(c, carry):
            m_i, l_i, acc = carry
            k0 = pl.multiple_of(c * KV_SEL, KV_SEL)
            kt = ks_ref[pl.ds(k0, KV_SEL), g * HD:(g + 1) * HD]
            vt = vs_ref[pl.ds(k0, KV_SEL), g * HD:(g + 1) * HD]
            sc = lax.dot_general(qg, kt, NT_DIMS, preferred_element_type=F32)
            selm = jnp.dot(sel4, e_ref[c], preferred_element_type=F32)
            keep = jnp.where(klane_s + k0 <= trow_s, selm, 0.0) > 0.5
            sc = jnp.where(keep, sc, NEG_INF)
            m_new = jnp.maximum(m_i, jnp.max(sc, axis=1, keepdims=True))
            alpha = jnp.exp(m_i - m_new)
            pp = jnp.where(keep, jnp.exp(sc - m_new), 0.0)
            l_new = alpha * l_i + jnp.sum(pp, axis=1, keepdims=True)
            acc = alpha * acc + jnp.dot(pp.astype(BF16), vt, preferred_element_type=F32)
            return m_new, l_new, acc

        n_kv = (q0 + TQ + KV_SEL - 1) // KV_SEL
        m_i, l_i, acc = lax.fori_loop(
            0, n_kv, sel_body,
            (jnp.full((R, 1), NEG_INF, F32), jnp.zeros((R, 1), F32), jnp.zeros((R, HD), F32)))
        o_sel = acc / jnp.where(l_i > 0.0, l_i, 1.0)

        w0 = pl.multiple_of(jnp.maximum(q0 - WINDOW, 0), TQ)
        kt = kw_ref[pl.ds(w0, WIN_KEYS), g * HD:(g + 1) * HD]
        vt = vw_ref[pl.ds(w0, WIN_KEYS), g * HD:(g + 1) * HD]
        sc = lax.dot_general(qg, kt, NT_DIMS, preferred_element_type=F32)
        diff = row_t((R, WIN_KEYS)) - (w0 + lax.broadcasted_iota(jnp.int32, (R, WIN_KEYS), 1))
        keep = jnp.where(diff >= 0, diff, WINDOW) < WINDOW
        sc = jnp.where(keep, sc, NEG_INF)
        mw = jnp.max(sc, axis=1, keepdims=True)
        pw = jnp.where(keep, jnp.exp(sc - mw), 0.0)
        lw = jnp.sum(pw, axis=1, keepdims=True)
        o_win = jnp.dot(pw.astype(BF16), vt, preferred_element_type=F32) / jnp.where(lw > 0.0, lw, 1.0)

        for j in range(NSA_J):
            h = g * NSA_J + j
            rs = slice(j * TQ, (j + 1) * TQ)
            o = (gates[:, 3 * h:3 * h + 1] * o_cmp[rs] + gates[:, 3 * h + 1:3 * h + 2] * o_sel[rs]
                 + gates[:, 3 * h + 2:3 * h + 3] * o_win[rs])
            ssq = ssq + jnp.sum(o * o, axis=1, keepdims=True)
            obuf_ref[:, h * HD:(h + 1) * HD] = o

    inv = lax.rsqrt(ssq / float(NSA_HEADS * HD) + NORM_EPS)
    o_ref[...] = (obuf_ref[...] * inv * ng_ref[...]).astype(BF16)


def _nsa(q_r, k_r, proj, small, kc, vc, cover, emat, ng, B, S):
    T = B * S
    nq = S // TQ
    n_blk = S // CMP_STRIDE
    kvb = COL_KV // 256
    return pl.pallas_call(
        _nsa_kernel,
        grid=(B, nq),
        in_specs=[
            pl.BlockSpec((TQ, 1024), lambda b, i: (b * nq + i, 0)),
            pl.BlockSpec((1, NSA_GROUPS, n_blk, HD), lambda b, i: (b, 0, 0, 0)),
            pl.BlockSpec((1, NSA_GROUPS, n_blk, HD), lambda b, i: (b, 0, 0, 0)),
            pl.BlockSpec((S, 256), lambda b, i: (b, 1)),
            pl.BlockSpec((S, 256), lambda b, i: (b, kvb + 3)),
            pl.BlockSpec((S, 256), lambda b, i: (b, 2)),
            pl.BlockSpec((S, 256), lambda b, i: (b, kvb + 5)),
            pl.BlockSpec((TQ, LANE), lambda b, i: (b * nq + i, 0)),
            pl.BlockSpec((n_blk, LANE), lambda b, i: (0, 0)),
            pl.BlockSpec((S // KV_SEL, LANE, KV_SEL), lambda b, i: (0, 0, 0)),
            pl.BlockSpec((1, 1024), lambda b, i: (0, 0)),
        ],
        out_specs=pl.BlockSpec((TQ, 1024), lambda b, i: (b * nq + i, 0)),
        out_shape=jax.ShapeDtypeStruct((T, 1024), BF16),
        scratch_shapes=[pltpu.VMEM((TQ, 1024), F32)],
        compiler_params=_cparams(("parallel", "parallel")),
        name="nsa",
    )(q_r, kc, vc, k_r, proj, k_r, proj, small, cover, emat, ng)


def _mlstm_kernel(bias_ref, q_ref, k_ref, v_ref, om_ref, if_ref, cwq_ref, cwk_ref, cbq_ref, cbk_ref,
                  ng_ref, o_ref, c_ref, n_ref, m_ref, tq_ref, tk_ref):
    h = pl.program_id(1)
    L = ML_CHUNK

    @pl.when(pl.program_id(2) == 0)
    def _():
        c_ref[...] = jnp.zeros_like(c_ref)
        n_ref[...] = jnp.zeros_like(n_ref)
        m_ref[...] = jnp.zeros_like(m_ref)
        tq_ref[...] = jnp.zeros_like(tq_ref)
        tk_ref[...] = jnp.zeros_like(tk_ref)

    row8 = lax.broadcasted_iota(jnp.int32, (8, ML_HD), 0)

    def conv_silu(x, tail_ref, w_ref, b_ref):
        tail = tail_ref[...]
        y = x * w_ref[CONV_W - 1:CONV_W, :] + b_ref[...]
        for k in range(1, CONV_W):
            xs = pltpu.roll(x, k, 0)
            fix = jnp.where(row8 < k, pltpu.roll(tail, k, 0), xs[0:8])
            xs = jnp.concatenate([fix, xs[8:]], axis=0)
            y = y + xs * w_ref[CONV_W - 1 - k:CONV_W - k, :]
        tail_ref[...] = x[L - 8:L]
        return y * _sigmoid(y)

    qf = conv_silu(q_ref[...].astype(F32), tq_ref, cwq_ref, cbq_ref) * (ML_HD ** -0.5)
    kf = conv_silu(k_ref[...].astype(F32), tk_ref, cwk_ref, cbk_ref)
    qb = qf.astype(BF16)
    kb = kf.astype(BF16)
    vb = v_ref[...]
    vf = vb.astype(F32)

    ic = if_ref[0, 0, 0:1, :] + bias_ref[0, h]
    fp = if_ref[0, 0, 1:2, :] + bias_ref[1, h]
    fc = jnp.minimum(fp, 0.0) - jnp.log(1.0 + jnp.exp(-jnp.abs(fp)))

    r = lax.broadcasted_iota(jnp.int32, (L, L), 0)
    cidx = lax.broadcasted_iota(jnp.int32, (L, L), 1)
    tril = cidx <= r
    eye = cidx == r
    fc_b = jnp.broadcast_to(fc, (L, L))
    ic_b = jnp.broadcast_to(ic, (L, L))
    b_col = jnp.sum(jnp.where(tril, fc_b, 0.0), axis=1, keepdims=True)
    fc_col = jnp.sum(jnp.where(eye, fc_b, 0.0), axis=1, keepdims=True)
    ic_col = jnp.sum(jnp.where(eye, ic_b, 0.0), axis=1, keepdims=True)
    b_row = jnp.sum(jnp.where(r <= cidx, jnp.broadcast_to(fc_col, (L, L)), 0.0),
                    axis=0, keepdims=True)
    b_last = b_col[L - 1:L, :]
    m_prev = m_ref[...]

    d_log = jnp.where(tril, b_col - b_row + ic, NEG_INF)
    inter = b_col + m_prev
    m_t = jnp.maximum(inter, jnp.max(d_log, axis=1, keepdims=True))
    w_intra = jnp.exp(d_log - m_t)
    w_inter = jnp.exp(inter - m_t)
    qk = lax.dot_general(qb, kb, NT_DIMS, preferred_element_type=F32) * w_intra
    num = (jnp.dot(qk.astype(BF16), vb, preferred_element_type=F32)
           + w_inter * lax.dot_general(qb, c_ref[...].astype(BF16), NT_DIMS, preferred_element_type=F32))
    den = jnp.sum(qk, axis=1, keepdims=True) + w_inter * jnp.sum(qf * n_ref[...], axis=1, keepdims=True)
    hm = num / jnp.maximum(jnp.abs(den), jnp.exp(-m_t))

    w_log = b_last - b_col + ic_col
    m_new = jnp.maximum(b_last + m_prev, jnp.max(w_log, axis=0, keepdims=True))
    w_state = jnp.exp(w_log - m_new)
    decay = jnp.exp(b_last + m_prev - m_new)
    c_ref[...] = decay * c_ref[...] + lax.dot_general((w_state * vf).astype(BF16), kb, TN_DIMS,
                                                      preferred_element_type=F32)
    n_ref[...] = decay * n_ref[...] + jnp.sum(w_state * kf, axis=0, keepdims=True)
    m_ref[...] = m_new

    hn = hm * lax.rsqrt(jnp.mean(hm * hm, axis=1, keepdims=True) + NORM_EPS) * ng_ref[...]
    o_ref[...] = (hn * _sigmoid(om_ref[...].astype(F32))).astype(BF16)


def _mlstm(proj, if_arr, gate_bias, conv_w, conv_b, ng, B, S):
    T = B * S
    nc = S // ML_CHUNK
    cq, ck, cv, co = COL_QKM // ML_HD, COL_QKM // ML_HD + ML_HEADS, COL_VM // ML_HD, COL_OM // ML_HD
    rows = lambda col0: pl.BlockSpec((ML_CHUNK, ML_HD), lambda b, h, c: (b * nc + c, col0 + h))
    return pl.pallas_call(
        _mlstm_kernel,
        grid=(B, ML_HEADS, nc),
        in_specs=[
            pl.BlockSpec(memory_space=pltpu.SMEM),
            rows(cq), rows(ck), rows(cv), rows(co),
            pl.BlockSpec((1, 1, 2, ML_CHUNK), lambda b, h, c: (b, h, 0, c)),
            pl.BlockSpec((CONV_W, ML_HD), lambda b, h, c: (0, h)),
            pl.BlockSpec((CONV_W, ML_HD), lambda b, h, c: (0, ML_HEADS + h)),
            pl.BlockSpec((1, ML_HD), lambda b, h, c: (0, h)),
            pl.BlockSpec((1, ML_HD), lambda b, h, c: (0, ML_HEADS + h)),
            pl.BlockSpec((1, ML_HD), lambda b, h, c: (0, h)),
        ],
        out_specs=pl.BlockSpec((ML_CHUNK, ML_HD), lambda b, h, c: (b * nc + c, h)),
        out_shape=jax.ShapeDtypeStruct((T, ML_HEADS * ML_HD), BF16),
        scratch_shapes=[
            pltpu.VMEM((ML_HD, ML_HD), F32), pltpu.VMEM((1, ML_HD), F32), pltpu.VMEM((1, 1), F32),
            pltpu.VMEM((8, ML_HD), F32), pltpu.VMEM((8, ML_HD), F32),
        ],
        compiler_params=_cparams(("parallel", "parallel", "arbitrary")),
        name="mlstm",
    )(gate_bias, proj, proj, proj, proj, if_arr, conv_w, conv_w, conv_b, conv_b, ng)


def _pack_bf16_pair(lo, hi):
    lo_b = pltpu.bitcast(lo.astype(BF16).astype(F32), U32)
    hi_b = pltpu.bitcast(hi.astype(BF16).astype(F32), U32)
    return (lo_b >> 16) | hi_b


def _unpack_bf16_pair(p):
    lo = pltpu.bitcast(p << 16, F32)
    hi = pltpu.bitcast(p & jnp.uint32(0xFFFF0000), F32)
    return lo, hi


def _outproj_kernel(nsa_ref, ml_ref, w_ref, x_ref, g2_ref, wr_ref, br_ref, tril_ref,
                    x1_ref, xp_ref, rt_ref, cnt_ref, carry_ref):
    @pl.when(pl.program_id(0) == 0)
    def _():
        carry_ref[...] = jnp.zeros_like(carry_ref)

    half = D_MODEL // 2
    acc = jnp.dot(nsa_ref[...], w_ref[0:half, :], preferred_element_type=F32)
    acc = acc + jnp.dot(ml_ref[...], w_ref[half:, :], preferred_element_type=F32)
    x1 = x_ref[...] + acc
    x1_ref[...] = x1
    xn = x1 * lax.rsqrt(jnp.mean(x1 * x1, axis=-1, keepdims=True) + NORM_EPS) * g2_ref[...]
    xp_ref[...] = _pack_bf16_pair(xn[:, :half], xn[:, half:])
    logits = jnp.dot(xn.astype(BF16), wr_ref[...], preferred_element_type=F32) + br_ref[...]

    tm = logits.shape[0]
    lane = lax.broadcasted_iota(jnp.int32, (tm, LANE), 1)
    lane_f = lane.astype(F32)
    big = float(LANE)
    gmask = lane < MOE_GROUPS
    gmax = jnp.max(jnp.where(gmask, logits, NEG_INF), axis=1, keepdims=True)
    ge = jnp.where(gmask, jnp.exp(logits - gmax), 0.0)
    gp = ge / jnp.sum(ge, axis=1, keepdims=True)
    g_w = jnp.max(gp, axis=1, keepdims=True)
    g_idx = jnp.min(jnp.where(gmask, jnp.where(gp == g_w, lane_f, big), big), axis=1, keepdims=True)
    grp_of_lane = ((lane - MOE_GROUPS) >> EPG_LOG2).astype(F32)
    emask = jnp.where(lane >= MOE_GROUPS, grp_of_lane, -1.0) == g_idx
    emax = jnp.max(jnp.where(emask, logits, NEG_INF), axis=1, keepdims=True)
    ee = jnp.where(emask, jnp.exp(logits - emax), 0.0)
    ep = jnp.where(emask, ee / jnp.sum(ee, axis=1, keepdims=True), -1.0)
    v1 = jnp.max(ep, axis=1, keepdims=True)
    i1 = jnp.min(jnp.where(ep == v1, lane_f, big), axis=1, keepdims=True)
    ep2 = jnp.where(lane_f == i1, -1.0, ep)
    v2 = jnp.max(ep2, axis=1, keepdims=True)
    i2 = jnp.min(jnp.where(ep2 == v2, lane_f, big), axis=1, keepdims=True)
    w0 = g_w * v1 / (v1 + v2)
    w1 = g_w * v2 / (v1 + v2)
    e0 = i1 - float(MOE_GROUPS)
    e1 = i2 - float(MOE_GROUPS)

    oh0 = jnp.where(lane_f == e0, 1.0, 0.0)
    oh1 = jnp.where(lane_f == e1, 1.0, 0.0)
    pre0 = jnp.dot(tril_ref[...], oh0.astype(BF16), preferred_element_type=F32)
    pre1 = jnp.dot(tril_ref[...], oh1.astype(BF16), preferred_element_type=F32)
    carry = carry_ref[...]
    tot0 = pre0[tm - 1:tm, :]
    tot1 = pre1[tm - 1:tm, :]
    rank0 = jnp.sum(oh0 * (pre0 - 1.0 + carry), axis=1, keepdims=True)
    rank1 = jnp.sum(oh1 * (pre1 - 1.0 + carry + tot0), axis=1, keepdims=True)
    new_carry = carry + tot0 + tot1
    carry_ref[...] = new_carry
    cnt_ref[...] = jnp.broadcast_to(new_carry, cnt_ref.shape)

    rt = jnp.where(lane == 0, e0, jnp.where(lane == 1, e1, jnp.where(lane == 2, w0, jnp.where(
        lane == 3, w1, jnp.where(lane == 4, rank0, jnp.where(lane == 5, rank1, 0.0))))))
    rt_ref[...] = rt


def _out_proj(nsa_o, ml_o, w_out, x2, g2, w_r, b_r, tril):
    T = x2.shape[0]
    half = D_MODEL // 2
    return pl.pallas_call(
        _outproj_kernel,
        grid=(T // TM_OUT,),
        in_specs=[
            pl.BlockSpec((TM_OUT, half), lambda i: (i, 0)),
            pl.BlockSpec((TM_OUT, half), lambda i: (i, 0)),
            pl.BlockSpec((D_MODEL, D_MODEL), lambda i: (0, 0)),
            pl.BlockSpec((TM_OUT, D_MODEL), lambda i: (i, 0)),
            pl.BlockSpec((1, D_MODEL), lambda i: (0, 0)),
            pl.BlockSpec((D_MODEL, LANE), lambda i: (0, 0)),
            pl.BlockSpec((1, LANE), lambda i: (0, 0)),
            pl.BlockSpec((TM_OUT, TM_OUT), lambda i: (0, 0)),
        ],
        out_specs=[
            pl.BlockSpec((TM_OUT, D_MODEL), lambda i: (i, 0)),
            pl.BlockSpec((TM_OUT, half), lambda i: (i, 0)),
            pl.BlockSpec((TM_OUT, LANE), lambda i: (i, 0)),
            pl.BlockSpec((8, LANE), lambda i: (0, 0)),
        ],
        out_shape=[
            jax.ShapeDtypeStruct((T, D_MODEL), F32),
            jax.ShapeDtypeStruct((T, half), U32),
            jax.ShapeDtypeStruct((T, LANE), F32),
            jax.ShapeDtypeStruct((8, LANE), F32),
        ],
        scratch_shapes=[pltpu.VMEM((1, LANE), F32)],
        compiler_params=_cparams(("arbitrary",)),
        name="out_proj",
    )(nsa_o, ml_o, w_out, x2, g2, w_r, b_r, tril)


def _expert_kernel(be_ref, tok_ref, xp_hbm, wg_ref, wu_ref, wd_ref, y_ref,
                   xbuf_ref, sem_ref, wgb_ref, wub_ref, wdb_ref):
    i = pl.program_id(0)
    nb = pl.num_programs(0)
    half = D_MODEL // 2

    def row_copy(blk, slot, r):
        tok = tok_ref[blk * BM + r]
        return pltpu.make_async_copy(xp_hbm.at[pl.ds(tok, 1), :], xbuf_ref.at[slot, pl.ds(r, 1), :],
                                     sem_ref.at[slot])

    def start_gather(blk, slot):
        def body(r, _):
            row_copy(blk, slot, r).start()
            return 0
        lax.fori_loop(0, BM, body, 0, unroll=8)

    @pl.when(i == 0)
    def _():
        start_gather(0, 0)

    @pl.when(i + 1 < nb)
    def _():
        start_gather(i + 1, (i + 1) % 2)

    slot = i % 2
    pltpu.make_async_copy(xp_hbm.at[pl.ds(0, BM), :], xbuf_ref.at[slot], sem_ref.at[slot]).wait()

    prev = be_ref[jnp.maximum(i - 1, 0)]

    @pl.when((i == 0) | (be_ref[i] != prev))
    def _():
        wgb_ref[...] = wg_ref[0].astype(BF16)
        wub_ref[...] = wu_ref[0].astype(BF16)
        wdb_ref[...] = wd_ref[0].astype(BF16)

    lo, hi = _unpack_bf16_pair(xbuf_ref[slot])
    xl = lo.astype(BF16)
    xh = hi.astype(BF16)
    gt = (jnp.dot(xl, wgb_ref[0:half, :], preferred_element_type=F32)
          + jnp.dot(xh, wgb_ref[half:, :], preferred_element_type=F32))
    up = (jnp.dot(xl, wub_ref[0:half, :], preferred_element_type=F32)
          + jnp.dot(xh, wub_ref[half:, :], preferred_element_type=F32))
    hb = (gt * _sigmoid(gt) * up).astype(BF16)
    y = jnp.dot(hb, wdb_ref[...], preferred_element_type=F32)
    y_ref[...] = _pack_bf16_pair(y[:, :half], y[:, half:])


def _experts(block_expert, row_tok, xp, w_gate, w_up, w_down, n_rows):
    half = D_MODEL // 2
    nb = n_rows // BM
    grid_spec = pltpu.PrefetchScalarGridSpec(
        num_scalar_prefetch=2,
        grid=(nb,),
        in_specs=[
            pl.BlockSpec(memory_space=pl.ANY),
            pl.BlockSpec((1, D_MODEL, D_EXPERT), lambda i, be, tok: (be[i], 0, 0)),
            pl.BlockSpec((1, D_MODEL, D_EXPERT), lambda i, be, tok: (be[i], 0, 0)),
            pl.BlockSpec((1, D_EXPERT, D_MODEL), lambda i, be, tok: (be[i], 0, 0)),
        ],
        out_specs=pl.BlockSpec((BM, half), lambda i, be, tok: (i, 0)),
        scratch_shapes=[
            pltpu.VMEM((2, BM, half), U32),
            pltpu.SemaphoreType.DMA((2,)),
            pltpu.VMEM((D_MODEL, D_EXPERT), BF16),
            pltpu.VMEM((D_MODEL, D_EXPERT), BF16),
            pltpu.VMEM((D_EXPERT, D_MODEL), BF16),
        ],
    )
    return pl.pallas_call(
        _expert_kernel,
        grid_spec=grid_spec,
        out_shape=jax.ShapeDtypeStruct((n_rows, half), U32),
        compiler_params=_cparams(("arbitrary",)),
        name="experts",
    )(block_expert, row_tok, xp, w_gate, w_up, w_down)


def _combine_kernel(dest_ref, y_hbm, x1_ref, rt_ref, fg_ref, o_ref, ybuf_ref, sem_ref):
    i = pl.program_id(0)
    nt = pl.num_programs(0)
    half = D_MODEL // 2

    def start_gather(tile, slot):
        def body(r, _):
            for k in range(2):
                d = dest_ref[(tile * TC + r) * 2 + k]
                pltpu.make_async_copy(y_hbm.at[pl.ds(d, 1), :], ybuf_ref.at[slot, k, pl.ds(r, 1), :],
                                      sem_ref.at[slot]).start()
            return 0
        lax.fori_loop(0, TC, body, 0, unroll=8)

    @pl.when(i == 0)
    def _():
        start_gather(0, 0)

    @pl.when(i + 1 < nt)
    def _():
        start_gather(i + 1, (i + 1) % 2)

    slot = i % 2
    for k in range(2):
        pltpu.make_async_copy(y_hbm.at[pl.ds(0, TC), :], ybuf_ref.at[slot, k], sem_ref.at[slot]).wait()

    rt = rt_ref[...]
    w0 = rt[:, 2:3]
    w1 = rt[:, 3:4]
    lo0, hi0 = _unpack_bf16_pair(ybuf_ref[slot, 0])
    lo1, hi1 = _unpack_bf16_pair(ybuf_ref[slot, 1])
    xl = x1_ref[:, :half] + (w0 * lo0 + w1 * lo1)
    xh = x1_ref[:, half:] + (w0 * hi0 + w1 * hi1)
    ms = (jnp.sum(xl * xl, axis=1, keepdims=True) + jnp.sum(xh * xh, axis=1, keepdims=True)) / float(D_MODEL)
    inv = lax.rsqrt(ms + NORM_EPS)
    o_ref[:, :half] = xl * inv * fg_ref[:, :half]
    o_ref[:, half:] = xh * inv * fg_ref[:, half:]


def _combine(dest, ys, x1, route, fg):
    T = x1.shape[0]
    half = D_MODEL // 2
    grid_spec = pltpu.PrefetchScalarGridSpec(
        num_scalar_prefetch=1,
        grid=(T // TC,),
        in_specs=[
            pl.BlockSpec(memory_space=pl.ANY),
            pl.BlockSpec((TC, D_MODEL), lambda i, d: (i, 0)),
            pl.BlockSpec((TC, LANE), lambda i, d: (i, 0)),
            pl.BlockSpec((1, D_MODEL), lambda i, d: (0, 0)),
        ],
        out_specs=pl.BlockSpec((TC, D_MODEL), lambda i, d: (i, 0)),
        scratch_shapes=[
            pltpu.VMEM((2, 2, TC, half), U32),
            pltpu.SemaphoreType.DMA((2,)),
        ],
    )
    return pl.pallas_call(
        _combine_kernel,
        grid_spec=grid_spec,
        out_shape=jax.ShapeDtypeStruct((T, D_MODEL), F32),
        compiler_params=_cparams(("arbitrary",)),
        name="combine",
    )(dest, ys, x1, route, fg)


def _cover_matrix(S):
    n_blk = S // CMP_STRIDE
    n_sel = S // SEL_BLOCK
    cs = np.arange(n_blk) * CMP_STRIDE
    ss = np.arange(n_sel) * SEL_BLOCK
    shared = np.minimum(cs[:, None] + CMP_LEN, ss[None, :] + SEL_BLOCK) - np.maximum(cs[:, None], ss[None, :])
    cover = np.zeros((n_blk, LANE), np.float32)
    cover[:, :n_sel] = np.clip(shared, 0, None) / CMP_LEN
    return cover


def _expand_matrix(S):
    n_sel = S // SEL_BLOCK
    e = np.zeros((S // KV_SEL, LANE, KV_SEL), np.float32)
    for c in range(S // KV_SEL):
        kpos = c * KV_SEL + np.arange(KV_SEL)
        e[c, kpos // SEL_BLOCK, np.arange(KV_SEL)] = 1.0
    assert n_sel <= LANE
    return e


def _inv_freq_row():
    inv = np.power(np.float32(ROPE_THETA), -np.arange(ROPE_HALF, dtype=np.float32) * 2.0 / ROPE_DIM)
    row = np.zeros((1, LANE), np.float32)
    row[0, :ROPE_HALF] = inv
    row[0, ROPE_HALF:ROPE_DIM] = inv
    return row


def _layer(x2, positions, B, S, norm1_g, w_in, cmp_pe_k, cmp_pe_v, cmp_wk1, cmp_wk2, cmp_wv1, cmp_wv2,
           nsa_norm_g, conv_w, conv_b, b_igate, b_fgate, mlstm_norm_g, w_out, norm2_g,
           w_group, b_group, w_router, b_router, w_exp_gate, w_exp_up, w_exp_down, out_norm_g):
    T = B * S
    o_q, o_kv, o_g, o_qk, o_v, o_o, o_i, o_f = 0, 1024, 2560, 2584, 4632, 5656, 6680, 6684
    w_main = jnp.concatenate([w_in[:, o_q:o_g], w_in[:, o_qk:o_i]], axis=1).astype(BF16)
    w_small = jnp.concatenate([w_in[:, o_g:o_qk], w_in[:, o_i:], jnp.zeros((D_MODEL, LANE - 32), F32)],
                              axis=1).astype(BF16)
    posb = jnp.broadcast_to(positions.reshape(T, 1).astype(F32), (T, LANE))
    invf = jnp.asarray(_inv_freq_row())

    proj, small = _in_proj(x2, norm1_g.reshape(1, -1), w_main, w_small)
    q_r, k_r = _rope(posb, invf, proj)
    kc, vc = _compress(k_r, proj, B, S, cmp_pe_k, cmp_pe_v, cmp_wk1.astype(BF16), cmp_wk2.astype(BF16),
                       cmp_wv1.astype(BF16), cmp_wv2.astype(BF16))
    nsa_o = _nsa(q_r, k_r, proj, small, kc, vc, jnp.asarray(_cover_matrix(S), BF16),
                 jnp.asarray(_expand_matrix(S), BF16), nsa_norm_g.reshape(1, -1), B, S)

    if_arr = small[:, 24:32].reshape(B, S, 2, ML_HEADS).transpose(0, 3, 2, 1)
    gate_bias = jnp.stack([b_igate, b_fgate]).astype(F32)
    ml_o = _mlstm(proj, if_arr, gate_bias, conv_w, conv_b.reshape(1, -1), mlstm_norm_g.reshape(1, -1), B, S)

    w_r = jnp.concatenate([w_group, w_router, jnp.zeros((D_MODEL, LANE - MOE_GROUPS - N_EXPERTS), F32)],
                          axis=1).astype(BF16)
    b_r = jnp.concatenate([b_group, b_router, jnp.zeros((LANE - MOE_GROUPS - N_EXPERTS,), F32)]).reshape(1, LANE)
    tril = jnp.asarray(np.tril(np.ones((TM_OUT, TM_OUT), np.float32)), BF16)
    x1, xp, route, cnt = _out_proj(nsa_o, ml_o, w_out.astype(BF16), x2, norm2_g.reshape(1, -1), w_r, b_r, tril)

    n_rows = T * 2 + N_EXPERTS * BM
    counts = cnt[0, :N_EXPERTS].astype(jnp.int32)
    pcounts = (counts + BM - 1) // BM * BM
    pends = jnp.cumsum(pcounts)
    pstarts = pends - pcounts
    eid = route[:, 0:2].astype(jnp.int32)
    dest = (pstarts[eid] + route[:, 4:6].astype(jnp.int32)).reshape(T * 2)
    tok = jnp.repeat(jnp.arange(T, dtype=jnp.int32), 2)
    row_tok = jnp.zeros((n_rows,), jnp.int32).at[dest].set(tok)
    block_expert = jnp.minimum(jnp.searchsorted(pends, jnp.arange(n_rows // BM) * BM, side='right'),
                               N_EXPERTS - 1).astype(jnp.int32)

    ys = _experts(block_expert, row_tok, xp, w_exp_gate, w_exp_up, w_exp_down, n_rows)
    return _combine(dest, ys, x1, route, out_norm_g.reshape(1, -1))


def kernel(x, positions, norm1_g, w_in, cmp_pe_k, cmp_pe_v, cmp_wk1, cmp_wk2, cmp_wv1, cmp_wv2, nsa_norm_g,
           conv_w, conv_b, b_igate, b_fgate, mlstm_norm_g, w_out, norm2_g, w_group, b_group, w_router,
           b_router, w_exp_gate, w_exp_up, w_exp_down, final_norm_g):
    B, S, D = x.shape
    assert D == D_MODEL and norm1_g.shape[0] == 1, "single-layer, D_MODEL-wide configuration only"
    assert S % ML_CHUNK == 0 and S % KV_SEL == 0 and S >= WIN_KEYS and (B * S) % TM_IN == 0
    out = _layer(x.reshape(B * S, D), positions, B, S, norm1_g[0], w_in[0], cmp_pe_k[0], cmp_pe_v[0],
                 cmp_wk1[0], cmp_wk2[0], cmp_wv1[0], cmp_wv2[0], nsa_norm_g[0], conv_w[0], conv_b[0],
                 b_igate[0], b_fgate[0], mlstm_norm_g[0], w_out[0], norm2_g[0], w_group[0], b_group[0],
                 w_router[0], b_router[0], w_exp_gate[0], w_exp_up[0], w_exp_down[0], final_norm_g)
    return out.reshape(B, S, D)
```

```python
import functools

import numpy as np
import jax
import jax.numpy as jnp
from jax import lax
from jax.experimental import pallas as pl
from jax.experimental.pallas import tpu as pltpu

F32 = jnp.float32
BF16 = jnp.bfloat16
U32 = jnp.uint32

D_MODEL = 2048
NSA_HEADS = 8
NSA_GROUPS = 2
NSA_J = NSA_HEADS // NSA_GROUPS
HD = 128
CMP_LEN = 32
CMP_STRIDE = 16
SEL_BLOCK = 64
SEL_TOPK = 8
WINDOW = 512
ROPE_THETA = 500000.0
ROPE_DIM = 32
ROPE_HALF = 16
ML_HEADS = 4
ML_HD = 256
CONV_W = 4
MOE_GROUPS = 8
EPG = 8
N_EXPERTS = 64
D_EXPERT = 512
SEL_BLOCK_LOG2 = 6
EPG_LOG2 = 3
assert (1 << SEL_BLOCK_LOG2) == SEL_BLOCK and (1 << EPG_LOG2) == EPG
NORM_EPS = 1e-6
NEG_INF = -1e30
FORCE_SCORE = 1000.0

COL_Q = 0
COL_KV = 1024
COL_QKM = 2560
COL_VM = 4608
COL_OM = 5632
N_MAIN = 6656

LANE = 128
VMEM_LIMIT = 56 * 1024 * 1024

TM_IN = 1024
TN_IN = 512
TS_ROPE = 256
TQ = 128
KV_SEL = 256
WIN_KEYS = WINDOW + TQ
ML_CHUNK = 256
TM_OUT = 512
BM = 256
TC = 256

NT_DIMS = (((1,), (1,)), ((), ()))
TN_DIMS = (((0,), (0,)), ((), ()))


def _cparams(sem):
    return pltpu.CompilerParams(dimension_semantics=sem, vmem_limit_bytes=VMEM_LIMIT)


def _sigmoid(x):
    return 1.0 / (1.0 + jnp.exp(-x))


def _inproj_kernel(x_ref, g_ref, w_ref, ws_ref, o_ref, os_ref, h_ref):
    @pl.when(pl.program_id(1) == 0)
    def _():
        x = x_ref[...]
        ms = jnp.mean(x * x, axis=-1, keepdims=True)
        h_ref[...] = (x * lax.rsqrt(ms + NORM_EPS) * g_ref[...]).astype(BF16)
        os_ref[...] = jnp.dot(h_ref[...], ws_ref[...], preferred_element_type=F32)

    o_ref[...] = jnp.dot(h_ref[...], w_ref[...], preferred_element_type=F32).astype(o_ref.dtype)


def _in_proj(x2, g1, w_main, w_small):
    T = x2.shape[0]
    return pl.pallas_call(
        _inproj_kernel,
        grid=(T // TM_IN, N_MAIN // TN_IN),
        in_specs=[
            pl.BlockSpec((TM_IN, D_MODEL), lambda m, n: (m, 0)),
            pl.BlockSpec((1, D_MODEL), lambda m, n: (0, 0)),
            pl.BlockSpec((D_MODEL, TN_IN), lambda m, n: (0, n)),
            pl.BlockSpec((D_MODEL, LANE), lambda m, n: (0, 0)),
        ],
        out_specs=[
            pl.BlockSpec((TM_IN, TN_IN), lambda m, n: (m, n)),
            pl.BlockSpec((TM_IN, LANE), lambda m, n: (m, 0)),
        ],
        out_shape=[
            jax.ShapeDtypeStruct((T, N_MAIN), BF16),
            jax.ShapeDtypeStruct((T, LANE), F32),
        ],
        scratch_shapes=[pltpu.VMEM((TM_IN, D_MODEL), BF16)],
        compiler_params=_cparams(("parallel", "arbitrary")),
        name="in_proj",
    )(x2, g1, w_main, w_small)


def _rope_kernel(pos_ref, invf_ref, q_ref, kc_ref, ks_ref, kw_ref, qo_ref, ko_ref):
    ang = pos_ref[...] * invf_ref[...]
    c = jnp.cos(ang)
    s = jnp.sin(ang)
    lane = lax.broadcasted_iota(jnp.int32, ang.shape, 1)
    sa = jnp.where(lane < ROPE_HALF, -s, 0.0)
    sb = jnp.where(lane < ROPE_HALF, 0.0, s)

    def rope(x):
        return x * c + pltpu.roll(x, LANE - ROPE_HALF, 1) * sa + pltpu.roll(x, ROPE_HALF, 1) * sb

    scale = HD ** -0.5
    for h in range(NSA_HEADS):
        sl = slice(h * HD, (h + 1) * HD)
        qo_ref[:, sl] = (rope(q_ref[:, sl].astype(F32)) * scale).astype(BF16)
    for i, r in enumerate((kc_ref, ks_ref, kw_ref)):
        for g in range(NSA_GROUPS):
            sl = slice(g * HD, (g + 1) * HD)
            so = slice(i * 2 * HD + g * HD, i * 2 * HD + (g + 1) * HD)
            ko_ref[:, so] = rope(r[:, sl].astype(F32)).astype(BF16)


def _rope(posb, invf, proj):
    T = proj.shape[0]
    kvb = COL_KV // 256
    return pl.pallas_call(
        _rope_kernel,
        grid=(T // TS_ROPE,),
        in_specs=[
            pl.BlockSpec((TS_ROPE, LANE), lambda i: (i, 0)),
            pl.BlockSpec((1, LANE), lambda i: (0, 0)),
            pl.BlockSpec((TS_ROPE, 1024), lambda i: (i, 0)),
            pl.BlockSpec((TS_ROPE, 256), lambda i: (i, kvb + 0)),
            pl.BlockSpec((TS_ROPE, 256), lambda i: (i, kvb + 2)),
            pl.BlockSpec((TS_ROPE, 256), lambda i: (i, kvb + 4)),
        ],
        out_specs=[
            pl.BlockSpec((TS_ROPE, 1024), lambda i: (i, 0)),
            pl.BlockSpec((TS_ROPE, 768), lambda i: (i, 0)),
        ],
        out_shape=[
            jax.ShapeDtypeStruct((T, 1024), BF16),
            jax.ShapeDtypeStruct((T, 768), BF16),
        ],
        compiler_params=_cparams(("parallel",)),
        name="rope",
    )(posb, invf, proj, proj, proj, proj)


def _compress_kernel(k_ref, v_ref, pek_ref, pev_ref, w1k_ref, w2k_ref, w1v_ref, w2v_ref,
                     kc_ref, vc_ref, xs_ref):
    S = k_ref.shape[0]
    n_blk = S // CMP_STRIDE
    for src, pe, w1, w2, dst in ((k_ref, pek_ref, w1k_ref, w2k_ref, kc_ref),
                                 (v_ref, pev_ref, w1v_ref, w2v_ref, vc_ref)):
        xs_ref[0:S, :] = src[...].astype(F32)
        xs_ref[S:S + CMP_LEN, :] = jnp.zeros((CMP_LEN, HD), F32)
        acc = jnp.zeros((n_blk, HD), F32)
        for l in range(CMP_LEN):
            a = xs_ref[pl.ds(l, n_blk, stride=CMP_STRIDE), :] + pe[l:l + 1, :]
            acc = acc + jnp.dot(a.astype(BF16), w1[l * HD:(l + 1) * HD, :],
                                preferred_element_type=F32)
        hid = acc * _sigmoid(acc)
        out = jnp.dot(hid.astype(BF16), w2[...], preferred_element_type=F32)
        dst[0, 0] = out.astype(BF16)


def _compress(k_r, proj, B, S, pek, pev, w1k, w2k, w1v, w2v):
    n_blk = S // CMP_STRIDE
    vcol = (COL_KV + 256) // HD
    full = lambda shape: pl.BlockSpec(shape, lambda b, g: tuple(0 for _ in shape))
    return pl.pallas_call(
        _compress_kernel,
        grid=(B, NSA_GROUPS),
        in_specs=[
            pl.BlockSpec((S, HD), lambda b, g: (b, g)),
            pl.BlockSpec((S, HD), lambda b, g: (b, vcol + g)),
            full((CMP_LEN, HD)), full((CMP_LEN, HD)),
            full((CMP_LEN * HD, HD)), full((HD, HD)),
            full((CMP_LEN * HD, HD)), full((HD, HD)),
        ],
        out_specs=[
            pl.BlockSpec((1, 1, n_blk, HD), lambda b, g: (b, g, 0, 0)),
            pl.BlockSpec((1, 1, n_blk, HD), lambda b, g: (b, g, 0, 0)),
        ],
        out_shape=[
            jax.ShapeDtypeStruct((B, NSA_GROUPS, n_blk, HD), BF16),
            jax.ShapeDtypeStruct((B, NSA_GROUPS, n_blk, HD), BF16),
        ],
        scratch_shapes=[pltpu.VMEM((S + CMP_LEN, HD), F32)],
        compiler_params=_cparams(("parallel", "parallel")),
        name="compress",
    )(k_r, proj, pek, pev, w1k, w2k, w1v, w2v)


def _nsa_kernel(q_ref, kc_ref, vc_ref, ks_ref, vs_ref, kw_ref, vw_ref, sm_ref, cover_ref, e_ref,
                ng_ref, o_ref, obuf_ref):
    qi = pl.program_id(1)
    q0 = qi * TQ
    R = NSA_J * TQ
    n_cmp = kc_ref.shape[2]

    def row_t(shape):
        r = lax.broadcasted_iota(jnp.int32, shape, 0)
        return q0 + (r & (TQ - 1))

    gates = _sigmoid(sm_ref[...])
    ssq = jnp.zeros((TQ, 1), F32)

    for g in range(NSA_GROUPS):
        qg = jnp.concatenate([q_ref[:, (g * NSA_J + j) * HD:(g * NSA_J + j + 1) * HD]
                              for j in range(NSA_J)], axis=0)

        s = lax.dot_general(qg, kc_ref[0, g], NT_DIMS, preferred_element_type=F32)
        n_lane = lax.broadcasted_iota(jnp.int32, (R, n_cmp), 1)
        cmask = (n_lane * CMP_STRIDE + (CMP_LEN - 1)) <= row_t((R, n_cmp))
        s = jnp.where(cmask, s, NEG_INF)
        m = jnp.max(s, axis=1, keepdims=True)
        e = jnp.where(cmask, jnp.exp(s - m), 0.0)
        l = jnp.sum(e, axis=1, keepdims=True)
        p = (e / jnp.where(l > 0.0, l, 1.0)).astype(BF16)
        o_cmp = jnp.dot(p, vc_ref[0, g], preferred_element_type=F32)
        impr = jnp.dot(p, cover_ref[...], preferred_element_type=F32)
        imp = impr[0:TQ]
        for j in range(1, NSA_J):
            imp = imp + impr[j * TQ:(j + 1) * TQ]

        m_lane = lax.broadcasted_iota(jnp.int32, (TQ, LANE), 1)
        jt = (q0 + lax.broadcasted_iota(jnp.int32, (TQ, LANE), 0)) >> SEL_BLOCK_LOG2
        forced = jnp.where(m_lane == 0, FORCE_SCORE,
                           jnp.where(m_lane == jt, FORCE_SCORE,
                                     jnp.where(m_lane == jt - 1, FORCE_SCORE, 0.0)))
        score = jnp.where(m_lane <= jt, imp + forced, -1.0)
        n_sel = ks_ref.shape[0] // SEL_BLOCK
        score = jnp.where(m_lane < n_sel, score, -2.0)
        rank = jnp.zeros((TQ, LANE), F32)
        for mp in range(n_sel):
            col = score[:, mp:mp + 1]
            ge = jnp.where(col >= score, 1.0, 0.0)
            gt = jnp.where(col > score, 1.0, 0.0)
            rank = rank + jnp.where(m_lane > mp, ge, gt)
        sel = jnp.where(rank < float(min(SEL_TOPK, n_sel)), 1.0, 0.0)
        sel = jnp.where(m_lane < n_sel, sel, 0.0).astype(BF16)
        sel4 = jnp.concatenate([sel] * NSA_J, axis=0)

        trow_s = row_t((R, KV_SEL))
        klane_s = lax.broadcasted_iota(jnp.int32, (R, KV_SEL), 1)

        def sel_body(c, carry):
            m_i, l_i, acc = carry
            k0 = pl.multiple_of(c * KV_SEL, KV_SEL)
            kt = ks_ref[pl.ds(k0, KV_SEL), g * HD:(g + 1) * HD]
            vt = vs_ref[pl.ds(k0, KV_SEL), g * HD:(g + 1) * HD]
            sc = lax.dot_general(qg, kt, NT_DIMS, preferred_element_type=F32)
            selm = jnp.dot(sel4, e_ref[c], preferred_element_type=F32)
            keep = jnp.where(klane_s + k0 <= trow_s, selm, 0.0) > 0.5
            sc = jnp.where(keep, sc, NEG_INF)
            m_new = jnp.maximum(m_i, jnp.max(sc, axis=1, keepdims=True))
            alpha = jnp.exp(m_i - m_new)
            pp = jnp.where(keep, jnp.exp(sc - m_new), 0.0)
            l_new = alpha * l_i + jnp.sum(pp, axis=1, keepdims=True)
            acc = alpha * acc + jnp.dot(pp.astype(BF16), vt, preferred_element_type=F32)
            return m_new, l_new, acc

        n_kv = (q0 + TQ + KV_SEL - 1) // KV_SEL
        m_i, l_i, acc = lax.fori_loop(
            0, n_kv, sel_body,
            (jnp.full((R, 1), NEG_INF, F32), jnp.zeros((R, 1), F32), jnp.zeros((R, HD), F32)))
        o_sel = acc / jnp.where(l_i > 0.0, l_i, 1.0)

        w0 = pl.multiple_of(jnp.maximum(q0 - WINDOW, 0), TQ)
        kt = kw_ref[pl.ds(w0, WIN_KEYS), g * HD:(g + 1) * HD]
        vt = vw_ref[pl.ds(w0, WIN_KEYS), g * HD:(g + 1) * HD]
        sc = lax.dot_general(qg, kt, NT_DIMS, preferred_element_type=F32)
        diff = row_t((R, WIN_KEYS)) - (w0 + lax.broadcasted_iota(jnp.int32, (R, WIN_KEYS), 1))
        keep = jnp.where(diff >= 0, diff, WINDOW) < WINDOW
        sc = jnp.where(keep, sc, NEG_INF)
        mw = jnp.max(sc, axis=1, keepdims=True)
        pw = jnp.where(keep, jnp.exp(sc - mw), 0.0)
        lw = jnp.sum(pw, axis=1, keepdims=True)
        o_win = jnp.dot(pw.astype(BF16), vt, preferred_element_type=F32) / jnp.where(lw > 0.0, lw, 1.0)

        for j in range(NSA_J):
            h = g * NSA_J + j
            rs = slice(j * TQ, (j + 1) * TQ)
            o = (gates[:, 3 * h:3 * h + 1] * o_cmp[rs] + gates[:, 3 * h + 1:3 * h + 2] * o_sel[rs]
                 + gates[:, 3 * h + 2:3 * h + 3] * o_win[rs])
            ssq = ssq + jnp.sum(o * o, axis=1, keepdims=True)
            obuf_ref[:, h * HD:(h + 1) * HD] = o

    inv = lax.rsqrt(ssq / float(NSA_HEADS * HD) + NORM_EPS)
    o_ref[...] = (obuf_ref[...] * inv * ng_ref[...]).astype(BF16)


def _nsa(q_r, k_r, proj, small, kc, vc, cover, emat, ng, B, S):
    T = B * S
    nq = S // TQ
    n_blk = S // CMP_STRIDE
    kvb = COL_KV // 256
    return pl.pallas_call(
        _nsa_kernel,
        grid=(B, nq),
        in_specs=[
            pl.BlockSpec((TQ, 1024), lambda b, i: (b * nq + i, 0)),
            pl.BlockSpec((1, NSA_GROUPS, n_blk, HD), lambda b, i: (b, 0, 0, 0)),
            pl.BlockSpec((1, NSA_GROUPS, n_blk, HD), lambda b, i: (b, 0, 0, 0)),
            pl.BlockSpec((S, 256), lambda b, i: (b, 1)),
            pl.BlockSpec((S, 256), lambda b, i: (b, kvb + 3)),
            pl.BlockSpec((S, 256), lambda b, i: (b, 2)),
            pl.BlockSpec((S, 256), lambda b, i: (b, kvb + 5)),
            pl.BlockSpec((TQ, LANE), lambda b, i: (b * nq + i, 0)),
            pl.BlockSpec((n_blk, LANE), lambda b, i: (0, 0)),
            pl.BlockSpec((S // KV_SEL, LANE, KV_SEL), lambda b, i: (0, 0, 0)),
            pl.BlockSpec((1, 1024), lambda b, i: (0, 0)),
        ],
        out_specs=pl.BlockSpec((TQ, 1024), lambda b, i: (b * nq + i, 0)),
        out_shape=jax.ShapeDtypeStruct((T, 1024), BF16),
        scratch_shapes=[pltpu.VMEM((TQ, 1024), F32)],
        compiler_params=_cparams(("parallel", "parallel")),
        name="nsa",
    )(q_r, kc, vc, k_r, proj, k_r, proj, small, cover, emat, ng)


def _mlstm_kernel(bias_ref, q_ref, k_ref, v_ref, om_ref, if_ref, cwq_ref, cwk_ref, cbq_ref, cbk_ref,
                  ng_ref, o_ref, c_ref, n_ref, m_ref, tq_ref, tk_ref):
    h = pl.program_id(1)
    L = ML_CHUNK

    @pl.when(pl.program_id(2) == 0)
    def _():
        c_ref[...] = jnp.zeros_like(c_ref)
        n_ref[...] = jnp.zeros_like(n_ref)
        m_ref[...] = jnp.zeros_like(m_ref)
        tq_ref[...] = jnp.zeros_like(tq_ref)
        tk_ref[...] = jnp.zeros_like(tk_ref)

    row8 = lax.broadcasted_iota(jnp.int32, (8, ML_HD), 0)

    def conv_silu(x, tail_ref, w_ref, b_ref):
        tail = tail_ref[...]
        y = x * w_ref[CONV_W - 1:CONV_W, :] + b_ref[...]
        for k in range(1, CONV_W):
            xs = pltpu.roll(x, k, 0)
            fix = jnp.where(row8 < k, pltpu.roll(tail, k, 0), xs[0:8])
            xs = jnp.concatenate([fix, xs[8:]], axis=0)
            y = y + xs * w_ref[CONV_W - 1 - k:CONV_W - k, :]
        tail_ref[...] = x[L - 8:L]
        return y * _sigmoid(y)

    qf = conv_silu(q_ref[...].astype(F32), tq_ref, cwq_ref, cbq_ref) * (ML_HD ** -0.5)
    kf = conv_silu(k_ref[...].astype(F32), tk_ref, cwk_ref, cbk_ref)
    qb = qf.astype(BF16)
    kb = kf.astype(BF16)
    vb = v_ref[...]
    vf = vb.astype(F32)

    ic = if_ref[0, 0, 0:1, :] + bias_ref[0, h]
    fp = if_ref[0, 0, 1:2, :] + bias_ref[1, h]
    fc = jnp.minimum(fp, 0.0) - jnp.log(1.0 + jnp.exp(-jnp.abs(fp)))

    r = lax.broadcasted_iota(jnp.int32, (L, L), 0)
    cidx = lax.broadcasted_iota(jnp.int32, (L, L), 1)
    tril = cidx <= r
    eye = cidx == r
    fc_b = jnp.broadcast_to(fc, (L, L))
    ic_b = jnp.broadcast_to(ic, (L, L))
    b_col = jnp.sum(jnp.where(tril, fc_b, 0.0), axis=1, keepdims=True)
    fc_col = jnp.sum(jnp.where(eye, fc_b, 0.0), axis=1, keepdims=True)
    ic_col = jnp.sum(jnp.where(eye, ic_b, 0.0), axis=1, keepdims=True)
    b_row = jnp.sum(jnp.where(r <= cidx, jnp.broadcast_to(fc_col, (L, L)), 0.0),
                    axis=0, keepdims=True)
    b_last = b_col[L - 1:L, :]
    m_prev = m_ref[...]

    d_log = jnp.where(tril, b_col - b_row + ic, NEG_INF)
    inter = b_col + m_prev
    m_t = jnp.maximum(inter, jnp.max(d_log, axis=1, keepdims=True))
    w_intra = jnp.exp(d_log - m_t)
    w_inter = jnp.exp(inter - m_t)
    qk = lax.dot_general(qb, kb, NT_DIMS, preferred_element_type=F32) * w_intra
    num = (jnp.dot(qk.astype(BF16), vb, preferred_element_type=F32)
           + w_inter * lax.dot_general(qb, c_ref[...].astype(BF16), NT_DIMS, preferred_element_type=F32))
    den = jnp.sum(qk, axis=1, keepdims=True) + w_inter * jnp.sum(qf * n_ref[...], axis=1, keepdims=True)
    hm = num / jnp.maximum(jnp.abs(den), jnp.exp(-m_t))

    w_log = b_last - b_col + ic_col
    m_new = jnp.maximum(b_last + m_prev, jnp.max(w_log, axis=0, keepdims=True))
    w_state = jnp.exp(w_log - m_new)
    decay = jnp.exp(b_last + m_prev - m_new)
    c_ref[...] = decay * c_ref[...] + lax.dot_general((w_state * vf).astype(BF16), kb, TN_DIMS,
                                                      preferred_element_type=F32)
    n_ref[...] = decay * n_ref[...] + jnp.sum(w_state * kf, axis=0, keepdims=True)
    m_ref[...] = m_new

    hn = hm * lax.rsqrt(jnp.mean(hm * hm, axis=1, keepdims=True) + NORM_EPS) * ng_ref[...]
    o_ref[...] = (hn * _sigmoid(om_ref[...].astype(F32))).astype(BF16)


def _mlstm(proj, if_arr, gate_bias, conv_w, conv_b, ng, B, S):
    T = B * S
    nc = S // ML_CHUNK
    cq, ck, cv, co = COL_QKM // ML_HD, COL_QKM // ML_HD + ML_HEADS, COL_VM // ML_HD, COL_OM // ML_HD
    rows = lambda col0: pl.BlockSpec((ML_CHUNK, ML_HD), lambda b, h, c: (b * nc + c, col0 + h))
    return pl.pallas_call(
        _mlstm_kernel,
        grid=(B, ML_HEADS, nc),
        in_specs=[
            pl.BlockSpec(memory_space=pltpu.SMEM),
            rows(cq), rows(ck), rows(cv), rows(co),
            pl.BlockSpec((1, 1, 2, ML_CHUNK), lambda b, h, c: (b, h, 0, c)),
            pl.BlockSpec((CONV_W, ML_HD), lambda b, h, c: (0, h)),
            pl.BlockSpec((CONV_W, ML_HD), lambda b, h, c: (0, ML_HEADS + h)),
            pl.BlockSpec((1, ML_HD), lambda b, h, c: (0, h)),
            pl.BlockSpec((1, ML_HD), lambda b, h, c: (0, ML_HEADS + h)),
            pl.BlockSpec((1, ML_HD), lambda b, h, c: (0, h)),
        ],
        out_specs=pl.BlockSpec((ML_CHUNK, ML_HD), lambda b, h, c: (b * nc + c, h)),
        out_shape=jax.ShapeDtypeStruct((T, ML_HEADS * ML_HD), BF16),
        scratch_shapes=[
            pltpu.VMEM((ML_HD, ML_HD), F32), pltpu.VMEM((1, ML_HD), F32), pltpu.VMEM((1, 1), F32),
            pltpu.VMEM((8, ML_HD), F32), pltpu.VMEM((8, ML_HD), F32),
        ],
        compiler_params=_cparams(("parallel", "parallel", "arbitrary")),
        name="mlstm",
    )(gate_bias, proj, proj, proj, proj, if_arr, conv_w, conv_w, conv_b, conv_b, ng)


def _pack_bf16_pair(lo, hi):
    lo_b = pltpu.bitcast(lo.astype(BF16).astype(F32), U32)
    hi_b = pltpu.bitcast(hi.astype(BF16).astype(F32), U32)
    return (lo_b >> 16) | hi_b


def _unpack_bf16_pair(p):
    lo = pltpu.bitcast(p << 16, F32)
    hi = pltpu.bitcast(p & jnp.uint32(0xFFFF0000), F32)
    return lo, hi


def _outproj_kernel(nsa_ref, ml_ref, w_ref, x_ref, g2_ref, wr_ref, br_ref, tril_ref,
                    x1_ref, xp_ref, rt_ref, cnt_ref, carry_ref):
    @pl.when(pl.program_id(0) == 0)
    def _():
        carry_ref[...] = jnp.zeros_like(carry_ref)

    half = D_MODEL // 2
    acc = jnp.dot(nsa_ref[...], w_ref[0:half, :], preferred_element_type=F32)
    acc = acc + jnp.dot(ml_ref[...], w_ref[half:, :], preferred_element_type=F32)
    x1 = x_ref[...] + acc
    x1_ref[...] = x1
    xn = x1 * lax.rsqrt(jnp.mean(x1 * x1, axis=-1, keepdims=True) + NORM_EPS) * g2_ref[...]
    xp_ref[...] = _pack_bf16_pair(xn[:, :half], xn[:, half:])
    logits = jnp.dot(xn.astype(BF16), wr_ref[...], preferred_element_type=F32) + br_ref[...]

    tm = logits.shape[0]
    lane = lax.broadcasted_iota(jnp.int32, (tm, LANE), 1)
    lane_f = lane.astype(F32)
    big = float(LANE)
    gmask = lane < MOE_GROUPS
    gmax = jnp.max(jnp.where(gmask, logits, NEG_INF), axis=1, keepdims=True)
    ge = jnp.where(gmask, jnp.exp(logits - gmax), 0.0)
    gp = ge / jnp.sum(ge, axis=1, keepdims=True)
    g_w = jnp.max(gp, axis=1, keepdims=True)
    g_idx = jnp.min(jnp.where(gmask, jnp.where(gp == g_w, lane_f, big), big), axis=1, keepdims=True)
    grp_of_lane = ((lane - MOE_GROUPS) >> EPG_LOG2).astype(F32)
    emask = jnp.where(lane >= MOE_GROUPS, grp_of_lane, -1.0) == g_idx
    emax = jnp.max(jnp.where(emask, logits, NEG_INF), axis=1, keepdims=True)
    ee = jnp.where(emask, jnp.exp(logits - emax), 0.0)
    ep = jnp.where(emask, ee / jnp.sum(ee, axis=1, keepdims=True), -1.0)
    v1 = jnp.max(ep, axis=1, keepdims=True)
    i1 = jnp.min(jnp.where(ep == v1, lane_f, big), axis=1, keepdims=True)
    ep2 = jnp.where(lane_f == i1, -1.0, ep)
    v2 = jnp.max(ep2, axis=1, keepdims=True)
    i2 = jnp.min(jnp.where(ep2 == v2, lane_f, big), axis=1, keepdims=True)
    w0 = g_w * v1 / (v1 + v2)
    w1 = g_w * v2 / (v1 + v2)
    e0 = i1 - float(MOE_GROUPS)
    e1 = i2 - float(MOE_GROUPS)

    oh0 = jnp.where(lane_f == e0, 1.0, 0.0)
    oh1 = jnp.where(lane_f == e1, 1.0, 0.0)
    pre0 = jnp.dot(tril_ref[...], oh0.astype(BF16), preferred_element_type=F32)
    pre1 = jnp.dot(tril_ref[...], oh1.astype(BF16), preferred_element_type=F32)
    carry = carry_ref[...]
    tot0 = pre0[tm - 1:tm, :]
    tot1 = pre1[tm - 1:tm, :]
    rank0 = jnp.sum(oh0 * (pre0 - 1.0 + carry), axis=1, keepdims=True)
    rank1 = jnp.sum(oh1 * (pre1 - 1.0 + carry + tot0), axis=1, keepdims=True)
    new_carry = carry + tot0 + tot1
    carry_ref[...] = new_carry
    cnt_ref[...] = jnp.broadcast_to(new_carry, cnt_ref.shape)

    rt = jnp.where(lane == 0, e0, jnp.where(lane == 1, e1, jnp.where(lane == 2, w0, jnp.where(
        lane == 3, w1, jnp.where(lane == 4, rank0, jnp.where(lane == 5, rank1, 0.0))))))
    rt_ref[...] = rt


def _out_proj(nsa_o, ml_o, w_out, x2, g2, w_r, b_r, tril):
    T = x2.shape[0]
    half = D_MODEL // 2
    return pl.pallas_call(
        _outproj_kernel,
        grid=(T // TM_OUT,),
        in_specs=[
            pl.BlockSpec((TM_OUT, half), lambda i: (i, 0)),
            pl.BlockSpec((TM_OUT, half), lambda i: (i, 0)),
            pl.BlockSpec((D_MODEL, D_MODEL), lambda i: (0, 0)),
            pl.BlockSpec((TM_OUT, D_MODEL), lambda i: (i, 0)),
            pl.BlockSpec((1, D_MODEL), lambda i: (0, 0)),
            pl.BlockSpec((D_MODEL, LANE), lambda i: (0, 0)),
            pl.BlockSpec((1, LANE), lambda i: (0, 0)),
            pl.BlockSpec((TM_OUT, TM_OUT), lambda i: (0, 0)),
        ],
        out_specs=[
            pl.BlockSpec((TM_OUT, D_MODEL), lambda i: (i, 0)),
            pl.BlockSpec((TM_OUT, half), lambda i: (i, 0)),
            pl.BlockSpec((TM_OUT, LANE), lambda i: (i, 0)),
            pl.BlockSpec((8, LANE), lambda i: (0, 0)),
        ],
        out_shape=[
            jax.ShapeDtypeStruct((T, D_MODEL), F32),
            jax.ShapeDtypeStruct((T, half), U32),
            jax.ShapeDtypeStruct((T, LANE), F32),
            jax.ShapeDtypeStruct((8, LANE), F32),
        ],
        scratch_shapes=[pltpu.VMEM((1, LANE), F32)],
        compiler_params=_cparams(("arbitrary",)),
        name="out_proj",
    )(nsa_o, ml_o, w_out, x2, g2, w_r, b_r, tril)


def _expert_kernel(be_ref, tok_ref, nxt_ref, meta_ref, xp_hbm, wg_hbm, wu_hbm, wd_hbm, y_ref,
                   xa_ref, xb_ref, gsem_ref, wsg_ref, wsu_ref, wsd_ref, wsem_ref, wgb_ref, wub_ref, wdb_ref):
    i = pl.program_id(0)
    n_used = meta_ref[0]
    half = D_MODEL // 2
    bufs = (xa_ref, xb_ref)

    def row_copy(blk, slot, r):
        tok = tok_ref[blk * BM + r]
        return pltpu.make_async_copy(xp_hbm.at[pl.ds(tok, 1), :], bufs[slot].at[pl.ds(r, 1), :],
                                     gsem_ref.at[slot])

    def wait_rows(slot):
        pltpu.make_async_copy(xp_hbm.at[pl.ds(0, BM), :], bufs[slot], gsem_ref.at[slot]).wait()

    def weight_copies(e):
        return (pltpu.make_async_copy(wg_hbm.at[e], wsg_ref, wsem_ref.at[0]),
                pltpu.make_async_copy(wu_hbm.at[e], wsu_ref, wsem_ref.at[1]),
                pltpu.make_async_copy(wd_hbm.at[e], wsd_ref, wsem_ref.at[2]))

    def step(slot):
        e = be_ref[i]
        wait_rows(slot)

        @pl.when((i == 0) | (e != be_ref[jnp.maximum(i - 1, 0)]))
        def _():
            for c in weight_copies(e):
                c.wait()
            wgb_ref[...] = wsg_ref[...].astype(BF16)
            wub_ref[...] = wsu_ref[...].astype(BF16)
            wdb_ref[...] = wsd_ref[...].astype(BF16)
            nx = nxt_ref[e]

            @pl.when(nx >= 0)
            def _():
                for c in weight_copies(nx):
                    c.start()

        for r in range(BM):
            row_copy(i + 1, 1 - slot, r).start()

        lo, hi = _unpack_bf16_pair(bufs[slot][...])
        xl = lo.astype(BF16)
        xh = hi.astype(BF16)
        gt = (jnp.dot(xl, wgb_ref[0:half, :], preferred_element_type=F32)
              + jnp.dot(xh, wgb_ref[half:, :], preferred_element_type=F32))
        up = (jnp.dot(xl, wub_ref[0:half, :], preferred_element_type=F32)
              + jnp.dot(xh, wub_ref[half:, :], preferred_element_type=F32))
        hb = (gt * _sigmoid(gt) * up).astype(BF16)
        y = jnp.dot(hb, wdb_ref[...], preferred_element_type=F32)
        y_ref[...] = _pack_bf16_pair(y[:, :half], y[:, half:])

        @pl.when(i == n_used - 1)
        def _():
            wait_rows(1 - slot)

    @pl.when(i == 0)
    def _():
        for c in weight_copies(be_ref[0]):
            c.start()

        def body(r, _):
            row_copy(0, 0, r).start()
            return 0
        lax.fori_loop(0, BM, body, 0, unroll=8)

    @pl.when((i < n_used) & (i % 2 == 0))
    def _():
        step(0)

    @pl.when((i < n_used) & (i % 2 == 1))
    def _():
        step(1)

    @pl.when(i >= n_used)
    def _():
        y_ref[...] = jnp.zeros_like(y_ref)


def _experts(block_expert, row_tok, next_expert, meta, xp, w_gate, w_up, w_down, n_rows):
    half = D_MODEL // 2
    nb = n_rows // BM
    grid_spec = pltpu.PrefetchScalarGridSpec(
        num_scalar_prefetch=4,
        grid=(nb,),
        in_specs=[pl.BlockSpec(memory_space=pl.ANY)] * 4,
        out_specs=pl.BlockSpec((BM, half), lambda i, be, tok, nxt, meta: (i, 0)),
        scratch_shapes=[
            pltpu.VMEM((BM, half), U32),
            pltpu.VMEM((BM, half), U32),
            pltpu.SemaphoreType.DMA((2,)),
            pltpu.VMEM((D_MODEL, D_EXPERT), F32),
            pltpu.VMEM((D_MODEL, D_EXPERT), F32),
            pltpu.VMEM((D_EXPERT, D_MODEL), F32),
            pltpu.SemaphoreType.DMA((3,)),
            pltpu.VMEM((D_MODEL, D_EXPERT), BF16),
            pltpu.VMEM((D_MODEL, D_EXPERT), BF16),
            pltpu.VMEM((D_EXPERT, D_MODEL), BF16),
        ],
    )
    return pl.pallas_call(
        _expert_kernel,
        grid_spec=grid_spec,
        out_shape=jax.ShapeDtypeStruct((n_rows, half), U32),
        compiler_params=_cparams(("arbitrary",)),
        name="experts",
    )(block_expert, row_tok, next_expert, meta, xp, w_gate, w_up, w_down)


def _combine_kernel(dest_ref, y_hbm, x1_ref, rt_ref, fg_ref, o_ref, ybuf_ref, sem_ref):
    i = pl.program_id(0)
    nt = pl.num_programs(0)
    half = D_MODEL // 2

    def start_gather(tile, slot):
        def body(r, _):
            for k in range(2):
                d = dest_ref[(tile * TC + r) * 2 + k]
                pltpu.make_async_copy(y_hbm.at[pl.ds(d, 1), :], ybuf_ref.at[slot, k, pl.ds(r, 1), :],
                                      sem_ref.at[slot]).start()
            return 0
        lax.fori_loop(0, TC, body, 0, unroll=8)

    @pl.when(i == 0)
    def _():
        start_gather(0, 0)

    @pl.when(i + 1 < nt)
    def _():
        start_gather(i + 1, (i + 1) % 2)

    slot = i % 2
    for k in range(2):
        pltpu.make_async_copy(y_hbm.at[pl.ds(0, TC), :], ybuf_ref.at[slot, k], sem_ref.at[slot]).wait()

    rt = rt_ref[...]
    w0 = rt[:, 2:3]
    w1 = rt[:, 3:4]
    lo0, hi0 = _unpack_bf16_pair(ybuf_ref[slot, 0])
    lo1, hi1 = _unpack_bf16_pair(ybuf_ref[slot, 1])
    xl = x1_ref[:, :half] + (w0 * lo0 + w1 * lo1)
    xh = x1_ref[:, half:] + (w0 * hi0 + w1 * hi1)
    ms = (jnp.sum(xl * xl, axis=1, keepdims=True) + jnp.sum(xh * xh, axis=1, keepdims=True)) / float(D_MODEL)
    inv = lax.rsqrt(ms + NORM_EPS)
    o_ref[:, :half] = xl * inv * fg_ref[:, :half]
    o_ref[:, half:] = xh * inv * fg_ref[:, half:]


def _combine(dest, ys, x1, route, fg):
    T = x1.shape[0]
    half = D_MODEL // 2
    grid_spec = pltpu.PrefetchScalarGridSpec(
        num_scalar_prefetch=1,
        grid=(T // TC,),
        in_specs=[
            pl.BlockSpec(memory_space=pl.ANY),
            pl.BlockSpec((TC, D_MODEL), lambda i, d: (i, 0)),
            pl.BlockSpec((TC, LANE), lambda i, d: (i, 0)),
            pl.BlockSpec((1, D_MODEL), lambda i, d: (0, 0)),
        ],
        out_specs=pl.BlockSpec((TC, D_MODEL), lambda i, d: (i, 0)),
        scratch_shapes=[
            pltpu.VMEM((2, 2, TC, half), U32),
            pltpu.SemaphoreType.DMA((2,)),
        ],
    )
    return pl.pallas_call(
        _combine_kernel,
        grid_spec=grid_spec,
        out_shape=jax.ShapeDtypeStruct((T, D_MODEL), F32),
        compiler_params=_cparams(("arbitrary",)),
        name="combine",
    )(dest, ys, x1, route, fg)


def _cover_matrix(S):
    n_blk = S // CMP_STRIDE
    n_sel = S // SEL_BLOCK
    cs = np.arange(n_blk) * CMP_STRIDE
    ss = np.arange(n_sel) * SEL_BLOCK
    shared = np.minimum(cs[:, None] + CMP_LEN, ss[None, :] + SEL_BLOCK) - np.maximum(cs[:, None], ss[None, :])
    cover = np.zeros((n_blk, LANE), np.float32)
    cover[:, :n_sel] = np.clip(shared, 0, None) / CMP_LEN
    return cover


def _expand_matrix(S):
    n_sel = S // SEL_BLOCK
    e = np.zeros((S // KV_SEL, LANE, KV_SEL), np.float32)
    for c in range(S // KV_SEL):
        kpos = c * KV_SEL + np.arange(KV_SEL)
        e[c, kpos // SEL_BLOCK, np.arange(KV_SEL)] = 1.0
    assert n_sel <= LANE
    return e


def _inv_freq_row():
    inv = np.power(np.float32(ROPE_THETA), -np.arange(ROPE_HALF, dtype=np.float32) * 2.0 / ROPE_DIM)
    row = np.zeros((1, LANE), np.float32)
    row[0, :ROPE_HALF] = inv
    row[0, ROPE_HALF:ROPE_DIM] = inv
    return row


def _layer(x2, positions, B, S, norm1_g, w_in, cmp_pe_k, cmp_pe_v, cmp_wk1, cmp_wk2, cmp_wv1, cmp_wv2,
           nsa_norm_g, conv_w, conv_b, b_igate, b_fgate, mlstm_norm_g, w_out, norm2_g,
           w_group, b_group, w_router, b_router, w_exp_gate, w_exp_up, w_exp_down, out_norm_g):
    T = B * S
    o_q, o_kv, o_g, o_qk, o_v, o_o, o_i, o_f = 0, 1024, 2560, 2584, 4632, 5656, 6680, 6684
    w_main = jnp.concatenate([w_in[:, o_q:o_g], w_in[:, o_qk:o_i]], axis=1).astype(BF16)
    w_small = jnp.concatenate([w_in[:, o_g:o_qk], w_in[:, o_i:], jnp.zeros((D_MODEL, LANE - 32), F32)],
                              axis=1).astype(BF16)
    posb = jnp.broadcast_to(positions.reshape(T, 1).astype(F32), (T, LANE))
    invf = jnp.asarray(_inv_freq_row())

    proj, small = _in_proj(x2, norm1_g.reshape(1, -1), w_main, w_small)
    q_r, k_r = _rope(posb, invf, proj)
    kc, vc = _compress(k_r, proj, B, S, cmp_pe_k, cmp_pe_v, cmp_wk1.astype(BF16), cmp_wk2.astype(BF16),
                       cmp_wv1.astype(BF16), cmp_wv2.astype(BF16))
    nsa_o = _nsa(q_r, k_r, proj, small, kc, vc, jnp.asarray(_cover_matrix(S), BF16),
                 jnp.asarray(_expand_matrix(S), BF16), nsa_norm_g.reshape(1, -1), B, S)

    if_arr = small[:, 24:32].reshape(B, S, 2, ML_HEADS).transpose(0, 3, 2, 1)
    gate_bias = jnp.stack([b_igate, b_fgate]).astype(F32)
    ml_o = _mlstm(proj, if_arr, gate_bias, conv_w, conv_b.reshape(1, -1), mlstm_norm_g.reshape(1, -1), B, S)

    w_r = jnp.concatenate([w_group, w_router, jnp.zeros((D_MODEL, LANE - MOE_GROUPS - N_EXPERTS), F32)],
                          axis=1).astype(BF16)
    b_r = jnp.concatenate([b_group, b_router, jnp.zeros((LANE - MOE_GROUPS - N_EXPERTS,), F32)]).reshape(1, LANE)
    tril = jnp.asarray(np.tril(np.ones((TM_OUT, TM_OUT), np.float32)), BF16)
    x1, xp, route, cnt = _out_proj(nsa_o, ml_o, w_out.astype(BF16), x2, norm2_g.reshape(1, -1), w_r, b_r, tril)

    n_rows = T * 2 + N_EXPERTS * BM
    counts = cnt[0, :N_EXPERTS].astype(jnp.int32)
    pcounts = (counts + BM - 1) // BM * BM
    pends = jnp.cumsum(pcounts)
    pstarts = pends - pcounts
    eid = route[:, 0:2].astype(jnp.int32)
    dest = (pstarts[eid] + route[:, 4:6].astype(jnp.int32)).reshape(T * 2)
    tok = jnp.repeat(jnp.arange(T, dtype=jnp.int32), 2)
    row_tok = jnp.zeros((n_rows + BM,), jnp.int32).at[dest].set(tok)
    block_expert = jnp.minimum(jnp.searchsorted(pends, jnp.arange(n_rows // BM) * BM, side='right'),
                               N_EXPERTS - 1).astype(jnp.int32)
    present = jnp.where(counts > 0, jnp.arange(N_EXPERTS, dtype=jnp.int32), N_EXPERTS)
    later = jnp.concatenate([lax.cummin(present[::-1])[::-1][1:], jnp.full((1,), N_EXPERTS, jnp.int32)])
    next_expert = jnp.where(later < N_EXPERTS, later, -1).astype(jnp.int32)
    meta = (pends[-1:] // BM).astype(jnp.int32)

    ys = _experts(block_expert, row_tok, next_expert, meta, xp, w_exp_gate, w_exp_up, w_exp_down, n_rows)
    return _combine(dest, ys, x1, route, out_norm_g.reshape(1, -1))


def kernel(x, positions, norm1_g, w_in, cmp_pe_k, cmp_pe_v, cmp_wk1, cmp_wk2, cmp_wv1, cmp_wv2, nsa_norm_g,
           conv_w, conv_b, b_igate, b_fgate, mlstm_norm_g, w_out, norm2_g, w_group, b_group, w_router,
           b_router, w_exp_gate, w_exp_up, w_exp_down, final_norm_g):
    B, S, D = x.shape
    assert D == D_MODEL and norm1_g.shape[0] == 1, "single-layer, D_MODEL-wide configuration only"
    assert S % ML_CHUNK == 0 and S % KV_SEL == 0 and S >= WIN_KEYS and (B * S) % TM_IN == 0
    out = _layer(x.reshape(B * S, D), positions, B, S, norm1_g[0], w_in[0], cmp_pe_k[0], cmp_pe_v[0],
                 cmp_wk1[0], cmp_wk2[0], cmp_wv1[0], cmp_wv2[0], nsa_norm_g[0], conv_w[0], conv_b[0],
                 b_igate[0], b_fgate[0], mlstm_norm_g[0], w_out[0], norm2_g[0], w_group[0], b_group[0],
                 w_router[0], b_router[0], w_exp_gate[0], w_exp_up[0], w_exp_down[0], final_norm_g)
    return out.reshape(B, S, D)
```

```python
import functools

import numpy as np
import jax
import jax.numpy as jnp
from jax import lax
from jax.experimental import pallas as pl
from jax.experimental.pallas import tpu as pltpu

F32 = jnp.float32
BF16 = jnp.bfloat16
U32 = jnp.uint32

D_MODEL = 2048
NSA_HEADS = 8
NSA_GROUPS = 2
NSA_J = NSA_HEADS // NSA_GROUPS
HD = 128
CMP_LEN = 32
CMP_STRIDE = 16
SEL_BLOCK = 64
SEL_TOPK = 8
WINDOW = 512
ROPE_THETA = 500000.0
ROPE_DIM = 32
ROPE_HALF = 16
ML_HEADS = 4
ML_HD = 256
CONV_W = 4
MOE_GROUPS = 8
EPG = 8
N_EXPERTS = 64
D_EXPERT = 512
SEL_BLOCK_LOG2 = 6
EPG_LOG2 = 3
assert (1 << SEL_BLOCK_LOG2) == SEL_BLOCK and (1 << EPG_LOG2) == EPG
NORM_EPS = 1e-6
NEG_INF = -1e30
FORCE_SCORE = 1000.0

COL_Q = 0
COL_KV = 1024
COL_QKM = 2560
COL_VM = 4608
COL_OM = 5632
N_MAIN = 6656

LANE = 128
VMEM_LIMIT = 56 * 1024 * 1024

TM_IN = 1024
TN_IN = 512
TS_ROPE = 256
TQ = 128
SEL_BUCKET = 512
WIN_KEYS = WINDOW + TQ
ML_CHUNK = 256
TM_OUT = 512
BM = 256
TC = 256

NT_DIMS = (((1,), (1,)), ((), ()))
TN_DIMS = (((0,), (0,)), ((), ()))


def _cparams(sem):
    return pltpu.CompilerParams(dimension_semantics=sem, vmem_limit_bytes=VMEM_LIMIT)


def _sigmoid(x):
    return 1.0 / (1.0 + jnp.exp(-x))


def _inproj_kernel(x_ref, g_ref, w_ref, ws_ref, o_ref, os_ref, h_ref):
    @pl.when(pl.program_id(1) == 0)
    def _():
        x = x_ref[...]
        ms = jnp.mean(x * x, axis=-1, keepdims=True)
        h_ref[...] = (x * lax.rsqrt(ms + NORM_EPS) * g_ref[...]).astype(BF16)
        os_ref[...] = jnp.dot(h_ref[...], ws_ref[...], preferred_element_type=F32)

    o_ref[...] = jnp.dot(h_ref[...], w_ref[...], preferred_element_type=F32).astype(o_ref.dtype)


def _in_proj(x2, g1, w_main, w_small):
    T = x2.shape[0]
    return pl.pallas_call(
        _inproj_kernel,
        grid=(T // TM_IN, N_MAIN // TN_IN),
        in_specs=[
            pl.BlockSpec((TM_IN, D_MODEL), lambda m, n: (m, 0)),
            pl.BlockSpec((1, D_MODEL), lambda m, n: (0, 0)),
            pl.BlockSpec((D_MODEL, TN_IN), lambda m, n: (0, n)),
            pl.BlockSpec((D_MODEL, LANE), lambda m, n: (0, 0)),
        ],
        out_specs=[
            pl.BlockSpec((TM_IN, TN_IN), lambda m, n: (m, n)),
            pl.BlockSpec((TM_IN, LANE), lambda m, n: (m, 0)),
        ],
        out_shape=[
            jax.ShapeDtypeStruct((T, N_MAIN), BF16),
            jax.ShapeDtypeStruct((T, LANE), F32),
        ],
        scratch_shapes=[pltpu.VMEM((TM_IN, D_MODEL), BF16)],
        compiler_params=_cparams(("parallel", "arbitrary")),
        name="in_proj",
    )(x2, g1, w_main, w_small)


def _rope_kernel(pos_ref, invf_ref, q_ref, kc_ref, ks_ref, kw_ref, qo_ref, ko_ref):
    ang = pos_ref[...] * invf_ref[...]
    c = jnp.cos(ang)
    s = jnp.sin(ang)
    lane = lax.broadcasted_iota(jnp.int32, ang.shape, 1)
    sa = jnp.where(lane < ROPE_HALF, -s, 0.0)
    sb = jnp.where(lane < ROPE_HALF, 0.0, s)

    def rope(x):
        return x * c + pltpu.roll(x, LANE - ROPE_HALF, 1) * sa + pltpu.roll(x, ROPE_HALF, 1) * sb

    scale = HD ** -0.5
    for h in range(NSA_HEADS):
        sl = slice(h * HD, (h + 1) * HD)
        qo_ref[:, sl] = (rope(q_ref[:, sl].astype(F32)) * scale).astype(BF16)
    for i, r in enumerate((kc_ref, ks_ref, kw_ref)):
        for g in range(NSA_GROUPS):
            sl = slice(g * HD, (g + 1) * HD)
            so = slice(i * 2 * HD + g * HD, i * 2 * HD + (g + 1) * HD)
            ko_ref[:, so] = rope(r[:, sl].astype(F32)).astype(BF16)


def _rope(posb, invf, proj):
    T = proj.shape[0]
    kvb = COL_KV // 256
    return pl.pallas_call(
        _rope_kernel,
        grid=(T // TS_ROPE,),
        in_specs=[
            pl.BlockSpec((TS_ROPE, LANE), lambda i: (i, 0)),
            pl.BlockSpec((1, LANE), lambda i: (0, 0)),
            pl.BlockSpec((TS_ROPE, 1024), lambda i: (i, 0)),
            pl.BlockSpec((TS_ROPE, 256), lambda i: (i, kvb + 0)),
            pl.BlockSpec((TS_ROPE, 256), lambda i: (i, kvb + 2)),
            pl.BlockSpec((TS_ROPE, 256), lambda i: (i, kvb + 4)),
        ],
        out_specs=[
            pl.BlockSpec((TS_ROPE, 1024), lambda i: (i, 0)),
            pl.BlockSpec((TS_ROPE, 768), lambda i: (i, 0)),
        ],
        out_shape=[
            jax.ShapeDtypeStruct((T, 1024), BF16),
            jax.ShapeDtypeStruct((T, 768), BF16),
        ],
        compiler_params=_cparams(("parallel",)),
        name="rope",
    )(posb, invf, proj, proj, proj, proj)


def _compress_kernel(k_ref, v_ref, pek_ref, pev_ref, w1k_ref, w2k_ref, w1v_ref, w2v_ref,
                     kc_ref, vc_ref, xs_ref):
    S = k_ref.shape[0]
    n_blk = S // CMP_STRIDE
    for src, pe, w1, w2, dst in ((k_ref, pek_ref, w1k_ref, w2k_ref, kc_ref),
                                 (v_ref, pev_ref, w1v_ref, w2v_ref, vc_ref)):
        xs_ref[0:S, :] = src[...].astype(F32)
        xs_ref[S:S + CMP_LEN, :] = jnp.zeros((CMP_LEN, HD), F32)
        acc = jnp.zeros((n_blk, HD), F32)
        for l in range(CMP_LEN):
            a = xs_ref[pl.ds(l, n_blk, stride=CMP_STRIDE), :] + pe[l:l + 1, :]
            acc = acc + jnp.dot(a.astype(BF16), w1[l * HD:(l + 1) * HD, :],
                                preferred_element_type=F32)
        hid = acc * _sigmoid(acc)
        out = jnp.dot(hid.astype(BF16), w2[...], preferred_element_type=F32)
        dst[0, 0] = out.astype(BF16)


def _compress(k_r, proj, B, S, pek, pev, w1k, w2k, w1v, w2v):
    n_blk = S // CMP_STRIDE
    vcol = (COL_KV + 256) // HD
    full = lambda shape: pl.BlockSpec(shape, lambda b, g: tuple(0 for _ in shape))
    return pl.pallas_call(
        _compress_kernel,
        grid=(B, NSA_GROUPS),
        in_specs=[
            pl.BlockSpec((S, HD), lambda b, g: (b, g)),
            pl.BlockSpec((S, HD), lambda b, g: (b, vcol + g)),
            full((CMP_LEN, HD)), full((CMP_LEN, HD)),
            full((CMP_LEN * HD, HD)), full((HD, HD)),
            full((CMP_LEN * HD, HD)), full((HD, HD)),
        ],
        out_specs=[
            pl.BlockSpec((1, 1, n_blk, HD), lambda b, g: (b, g, 0, 0)),
            pl.BlockSpec((1, 1, n_blk, HD), lambda b, g: (b, g, 0, 0)),
        ],
        out_shape=[
            jax.ShapeDtypeStruct((B, NSA_GROUPS, n_blk, HD), BF16),
            jax.ShapeDtypeStruct((B, NSA_GROUPS, n_blk, HD), BF16),
        ],
        scratch_shapes=[pltpu.VMEM((S + CMP_LEN, HD), F32)],
        compiler_params=_cparams(("parallel", "parallel")),
        name="compress",
    )(k_r, proj, pek, pev, w1k, w2k, w1v, w2v)


def _nsa_kernel(q_ref, kc_ref, vc_ref, ks_ref, vs_ref, kw_ref, vw_ref, sm_ref, covt_ref, eneg_ref, eye_ref,
                ng_ref, o_ref, obuf_ref, owin_ref, osel_ref):
    qi = pl.program_id(1)
    q0 = qi * TQ
    R = NSA_J * TQ
    n_cmp = kc_ref.shape[2]
    n_sel = ks_ref.shape[0] // SEL_BLOCK
    n_win = WIN_KEYS // TQ

    def row_t(shape):
        r = lax.broadcasted_iota(jnp.int32, shape, 0)
        return q0 + (r & (TQ - 1))

    qgs = [jnp.concatenate([q_ref[:, (g * NSA_J + j) * HD:(g * NSA_J + j + 1) * HD]
                            for j in range(NSA_J)], axis=0) for g in range(NSA_GROUPS)]

    def with_ones(v):
        lane = lax.broadcasted_iota(jnp.int32, v.shape, 1)
        return jnp.concatenate([v, jnp.where(lane == 0, 1.0, 0.0).astype(BF16)], axis=1)

    def window(masks):
        w0 = pl.multiple_of(jnp.maximum(q0 - WINDOW, 0), TQ)
        for g in range(NSA_GROUPS):
            kt = kw_ref[pl.ds(w0, WIN_KEYS), g * HD:(g + 1) * HD]
            vt = with_ones(vw_ref[pl.ds(w0, WIN_KEYS), g * HD:(g + 1) * HD])
            sc = masks(lax.dot_general(qgs[g], kt, NT_DIMS, preferred_element_type=F32), w0)
            mw = jnp.max(sc, axis=1, keepdims=True)
            pw = jnp.exp((sc - mw).astype(BF16))
            acc = jnp.dot(pw, vt, preferred_element_type=F32)
            owin_ref[g] = acc[:, 0:HD] / acc[:, HD:HD + 1]

    def band_masks(sc, w0):
        d = (lax.broadcasted_iota(jnp.int32, (R, TQ), 1)
             - (lax.broadcasted_iota(jnp.int32, (R, TQ), 0) & (TQ - 1)))
        first = jnp.where(d > 0, sc[:, 0:TQ], NEG_INF)
        last = jnp.where(d <= 0, sc[:, (n_win - 1) * TQ:], NEG_INF)
        return jnp.concatenate([first, sc[:, TQ:(n_win - 1) * TQ], last], axis=1)

    def general_masks(sc, w0):
        diff = row_t((R, WIN_KEYS)) - (w0 + lax.broadcasted_iota(jnp.int32, (R, WIN_KEYS), 1))
        keep = jnp.where(diff >= 0, diff, WINDOW) < WINDOW
        return jnp.where(keep, sc, NEG_INF)

    @pl.when(q0 >= WINDOW)
    def _():
        window(band_masks)

    @pl.when(q0 < WINDOW)
    def _():
        window(general_masks)

    gates = _sigmoid(sm_ref[...])
    o_cmps = []
    qps = []
    for g in range(NSA_GROUPS):
        qg = qgs[g]

        s = lax.dot_general(qg, kc_ref[0, g], NT_DIMS, preferred_element_type=F32)
        n_lane = lax.broadcasted_iota(jnp.int32, (R, n_cmp), 1)
        cmask = (n_lane * CMP_STRIDE + (CMP_LEN - 1)) <= row_t((R, n_cmp))
        s = jnp.where(cmask, s, NEG_INF)
        m = jnp.max(s, axis=1, keepdims=True)
        e = jnp.where(cmask, jnp.exp(s - m), 0.0)
        l = jnp.sum(e, axis=1, keepdims=True)
        p = (e / jnp.where(l > 0.0, l, 1.0)).astype(BF16)
        o_cmps.append(jnp.dot(p, vc_ref[0, g], preferred_element_type=F32))
        impr = lax.dot_general(covt_ref[...], p, NT_DIMS, preferred_element_type=F32)
        imp = impr[:, 0:TQ]
        for j in range(1, NSA_J):
            imp = imp + impr[:, j * TQ:(j + 1) * TQ]

        m_sub = lax.broadcasted_iota(jnp.int32, (n_sel, TQ), 0)
        jt = (q0 + lax.broadcasted_iota(jnp.int32, (n_sel, TQ), 1)) >> SEL_BLOCK_LOG2
        forced = jnp.where(m_sub == 0, FORCE_SCORE,
                           jnp.where(m_sub == jt, FORCE_SCORE,
                                     jnp.where(m_sub == jt - 1, FORCE_SCORE, 0.0)))
        score = jnp.where(m_sub <= jt, imp + forced, -1.0)
        rank = jnp.zeros((n_sel, TQ), F32)
        for mp in range(n_sel):
            row = score[mp:mp + 1, :]
            ge = jnp.where(row >= score, 1.0, 0.0)
            gt = jnp.where(row > score, 1.0, 0.0)
            rank = rank + jnp.where(m_sub > mp, ge, gt)
        notsel_t = jnp.where(rank < float(min(SEL_TOPK, n_sel)), 0.0, 1.0).astype(BF16)
        notsel = lax.dot_general(notsel_t, eye_ref[...], TN_DIMS,
                                 preferred_element_type=F32).astype(BF16)
        qps.append(jnp.concatenate([qg, jnp.concatenate([notsel] * NSA_J, axis=0)], axis=1))

    def sel_oneshot(nk):
        for g in range(NSA_GROUPS):
            kt = jnp.concatenate([ks_ref[0:nk, g * HD:(g + 1) * HD], eneg_ref[0:nk, :]], axis=1)
            vt = with_ones(vs_ref[0:nk, g * HD:(g + 1) * HD])
            sc = lax.dot_general(qps[g], kt, NT_DIMS, preferred_element_type=F32)
            klane = lax.broadcasted_iota(jnp.int32, (R, SEL_BUCKET), 1) + (nk - SEL_BUCKET)
            tail = jnp.where(klane <= row_t((R, SEL_BUCKET)), sc[:, nk - SEL_BUCKET:], NEG_INF)
            sc = tail if nk == SEL_BUCKET else jnp.concatenate([sc[:, 0:nk - SEL_BUCKET], tail], axis=1)
            ms = jnp.max(sc, axis=1, keepdims=True)
            acc = jnp.dot(jnp.exp((sc - ms).astype(BF16)), vt, preferred_element_type=F32)
            osel_ref[g] = acc[:, 0:HD] / acc[:, HD:HD + 1]

    for b in range(ks_ref.shape[0] // SEL_BUCKET):
        @pl.when(q0 // SEL_BUCKET == b)
        def _(b=b):
            sel_oneshot((b + 1) * SEL_BUCKET)

    ssq = jnp.zeros((TQ, 1), F32)
    for g in range(NSA_GROUPS):
        o_sel = osel_ref[g]
        o_cmp = o_cmps[g]
        o_win = owin_ref[g]
        for j in range(NSA_J):
            h = g * NSA_J + j
            rs = slice(j * TQ, (j + 1) * TQ)
            o = (gates[:, 3 * h:3 * h + 1] * o_cmp[rs] + gates[:, 3 * h + 1:3 * h + 2] * o_sel[rs]
                 + gates[:, 3 * h + 2:3 * h + 3] * o_win[rs])
            ssq = ssq + jnp.sum(o * o, axis=1, keepdims=True)
            obuf_ref[:, h * HD:(h + 1) * HD] = o

    inv = lax.rsqrt(ssq / float(NSA_HEADS * HD) + NORM_EPS)
    o_ref[...] = (obuf_ref[...] * inv * ng_ref[...]).astype(BF16)


def _nsa(q_r, k_r, proj, small, kc, vc, covt, eneg, eye, ng, B, S):
    T = B * S
    nq = S // TQ
    n_blk = S // CMP_STRIDE
    n_sel = S // SEL_BLOCK
    kvb = COL_KV // 256
    return pl.pallas_call(
        _nsa_kernel,
        grid=(B, nq),
        in_specs=[
            pl.BlockSpec((TQ, 1024), lambda b, i: (b * nq + i, 0)),
            pl.BlockSpec((1, NSA_GROUPS, n_blk, HD), lambda b, i: (b, 0, 0, 0)),
            pl.BlockSpec((1, NSA_GROUPS, n_blk, HD), lambda b, i: (b, 0, 0, 0)),
            pl.BlockSpec((S, 256), lambda b, i: (b, 1)),
            pl.BlockSpec((S, 256), lambda b, i: (b, kvb + 3)),
            pl.BlockSpec((S, 256), lambda b, i: (b, 2)),
            pl.BlockSpec((S, 256), lambda b, i: (b, kvb + 5)),
            pl.BlockSpec((TQ, LANE), lambda b, i: (b * nq + i, 0)),
            pl.BlockSpec((n_sel, n_blk), lambda b, i: (0, 0)),
            pl.BlockSpec((S, LANE), lambda b, i: (0, 0)),
            pl.BlockSpec((n_sel, LANE), lambda b, i: (0, 0)),
            pl.BlockSpec((1, 1024), lambda b, i: (0, 0)),
        ],
        out_specs=pl.BlockSpec((TQ, 1024), lambda b, i: (b * nq + i, 0)),
        out_shape=jax.ShapeDtypeStruct((T, 1024), BF16),
        scratch_shapes=[pltpu.VMEM((TQ, 1024), F32), pltpu.VMEM((NSA_GROUPS, NSA_J * TQ, HD), F32),
                        pltpu.VMEM((NSA_GROUPS, NSA_J * TQ, HD), F32)],
        compiler_params=_cparams(("parallel", "parallel")),
        name="nsa",
    )(q_r, kc, vc, k_r, proj, k_r, proj, small, covt, eneg, eye, ng)


def _mlstm_kernel(bias_ref, q_ref, k_ref, v_ref, om_ref, if_ref, cwq_ref, cwk_ref, cbq_ref, cbk_ref,
                  ng_ref, o_ref, c_ref, n_ref, m_ref, tq_ref, tk_ref):
    h = pl.program_id(1)
    L = ML_CHUNK

    @pl.when(pl.program_id(2) == 0)
    def _():
        c_ref[...] = jnp.zeros_like(c_ref)
        n_ref[...] = jnp.zeros_like(n_ref)
        m_ref[...] = jnp.zeros_like(m_ref)
        tq_ref[...] = jnp.zeros_like(tq_ref)
        tk_ref[...] = jnp.zeros_like(tk_ref)

    row8 = lax.broadcasted_iota(jnp.int32, (8, ML_HD), 0)

    def conv_silu(x, tail_ref, w_ref, b_ref):
        tail = tail_ref[...]
        y = x * w_ref[CONV_W - 1:CONV_W, :] + b_ref[...]
        for k in range(1, CONV_W):
            xs = pltpu.roll(x, k, 0)
            fix = jnp.where(row8 < k, pltpu.roll(tail, k, 0), xs[0:8])
            xs = jnp.concatenate([fix, xs[8:]], axis=0)
            y = y + xs * w_ref[CONV_W - 1 - k:CONV_W - k, :]
        tail_ref[...] = x[L - 8:L]
        return y * _sigmoid(y)

    qf = conv_silu(q_ref[...].astype(F32), tq_ref, cwq_ref, cbq_ref) * (ML_HD ** -0.5)
    kf = conv_silu(k_ref[...].astype(F32), tk_ref, cwk_ref, cbk_ref)
    qb = qf.astype(BF16)
    kb = kf.astype(BF16)
    vb = v_ref[...]
    vf = vb.astype(F32)

    ic = if_ref[0, 0, 0:1, :] + bias_ref[0, h]
    fp = if_ref[0, 0, 1:2, :] + bias_ref[1, h]
    fc = jnp.minimum(fp, 0.0) - jnp.log(1.0 + jnp.exp(-jnp.abs(fp)))

    r = lax.broadcasted_iota(jnp.int32, (L, L), 0)
    cidx = lax.broadcasted_iota(jnp.int32, (L, L), 1)
    tril = cidx <= r
    eye = cidx == r
    fc_b = jnp.broadcast_to(fc, (L, L))
    ic_b = jnp.broadcast_to(ic, (L, L))
    b_col = jnp.sum(jnp.where(tril, fc_b, 0.0), axis=1, keepdims=True)
    fc_col = jnp.sum(jnp.where(eye, fc_b, 0.0), axis=1, keepdims=True)
    ic_col = jnp.sum(jnp.where(eye, ic_b, 0.0), axis=1, keepdims=True)
    b_row = jnp.sum(jnp.where(r <= cidx, jnp.broadcast_to(fc_col, (L, L)), 0.0),
                    axis=0, keepdims=True)
    b_last = b_col[L - 1:L, :]
    m_prev = m_ref[...]

    d_log = jnp.where(tril, b_col - b_row + ic, NEG_INF)
    inter = b_col + m_prev
    m_t = jnp.maximum(inter, jnp.max(d_log, axis=1, keepdims=True))
    w_intra = jnp.exp(d_log - m_t)
    w_inter = jnp.exp(inter - m_t)
    qk = lax.dot_general(qb, kb, NT_DIMS, preferred_element_type=F32) * w_intra
    num = (jnp.dot(qk.astype(BF16), vb, preferred_element_type=F32)
           + w_inter * lax.dot_general(qb, c_ref[...].astype(BF16), NT_DIMS, preferred_element_type=F32))
    den = jnp.sum(qk, axis=1, keepdims=True) + w_inter * jnp.sum(qf * n_ref[...], axis=1, keepdims=True)
    hm = num / jnp.maximum(jnp.abs(den), jnp.exp(-m_t))

    w_log = b_last - b_col + ic_col
    m_new = jnp.maximum(b_last + m_prev, jnp.max(w_log, axis=0, keepdims=True))
    w_state = jnp.exp(w_log - m_new)
    decay = jnp.exp(b_last + m_prev - m_new)
    c_ref[...] = decay * c_ref[...] + lax.dot_general((w_state * vf).astype(BF16), kb, TN_DIMS,
                                                      preferred_element_type=F32)
    n_ref[...] = decay * n_ref[...] + jnp.sum(w_state * kf, axis=0, keepdims=True)
    m_ref[...] = m_new

    hn = hm * lax.rsqrt(jnp.mean(hm * hm, axis=1, keepdims=True) + NORM_EPS) * ng_ref[...]
    o_ref[...] = (hn * _sigmoid(om_ref[...].astype(F32))).astype(BF16)


def _mlstm(proj, if_arr, gate_bias, conv_w, conv_b, ng, B, S):
    T = B * S
    nc = S // ML_CHUNK
    cq, ck, cv, co = COL_QKM // ML_HD, COL_QKM // ML_HD + ML_HEADS, COL_VM // ML_HD, COL_OM // ML_HD
    rows = lambda col0: pl.BlockSpec((ML_CHUNK, ML_HD), lambda b, h, c: (b * nc + c, col0 + h))
    return pl.pallas_call(
        _mlstm_kernel,
        grid=(B, ML_HEADS, nc),
        in_specs=[
            pl.BlockSpec(memory_space=pltpu.SMEM),
            rows(cq), rows(ck), rows(cv), rows(co),
            pl.BlockSpec((1, 1, 2, ML_CHUNK), lambda b, h, c: (b, h, 0, c)),
            pl.BlockSpec((CONV_W, ML_HD), lambda b, h, c: (0, h)),
            pl.BlockSpec((CONV_W, ML_HD), lambda b, h, c: (0, ML_HEADS + h)),
            pl.BlockSpec((1, ML_HD), lambda b, h, c: (0, h)),
            pl.BlockSpec((1, ML_HD), lambda b, h, c: (0, ML_HEADS + h)),
            pl.BlockSpec((1, ML_HD), lambda b, h, c: (0, h)),
        ],
        out_specs=pl.BlockSpec((ML_CHUNK, ML_HD), lambda b, h, c: (b * nc + c, h)),
        out_shape=jax.ShapeDtypeStruct((T, ML_HEADS * ML_HD), BF16),
        scratch_shapes=[
            pltpu.VMEM((ML_HD, ML_HD), F32), pltpu.VMEM((1, ML_HD), F32), pltpu.VMEM((1, 1), F32),
            pltpu.VMEM((8, ML_HD), F32), pltpu.VMEM((8, ML_HD), F32),
        ],
        compiler_params=_cparams(("parallel", "parallel", "arbitrary")),
        name="mlstm",
    )(gate_bias, proj, proj, proj, proj, if_arr, conv_w, conv_w, conv_b, conv_b, ng)


def _pack_bf16_pair(lo, hi):
    lo_b = pltpu.bitcast(lo.astype(BF16).astype(F32), U32)
    hi_b = pltpu.bitcast(hi.astype(BF16).astype(F32), U32)
    return (lo_b >> 16) | hi_b


def _unpack_bf16_pair(p):
    lo = pltpu.bitcast(p << 16, F32)
    hi = pltpu.bitcast(p & jnp.uint32(0xFFFF0000), F32)
    return lo, hi


def _outproj_kernel(nsa_ref, ml_ref, w_ref, x_ref, g2_ref, wr_ref, br_ref, tril_ref,
                    x1_ref, xp_ref, rt_ref, cnt_ref, carry_ref):
    @pl.when(pl.program_id(0) == 0)
    def _():
        carry_ref[...] = jnp.zeros_like(carry_ref)

    half = D_MODEL // 2
    acc = jnp.dot(nsa_ref[...], w_ref[0:half, :], preferred_element_type=F32)
    acc = acc + jnp.dot(ml_ref[...], w_ref[half:, :], preferred_element_type=F32)
    x1 = x_ref[...] + acc
    x1_ref[...] = x1
    xn = x1 * lax.rsqrt(jnp.mean(x1 * x1, axis=-1, keepdims=True) + NORM_EPS) * g2_ref[...]
    xp_ref[...] = _pack_bf16_pair(xn[:, :half], xn[:, half:])
    logits = jnp.dot(xn.astype(BF16), wr_ref[...], preferred_element_type=F32) + br_ref[...]

    tm = logits.shape[0]
    lane = lax.broadcasted_iota(jnp.int32, (tm, LANE), 1)
    lane_f = lane.astype(F32)
    big = float(LANE)
    gmask = lane < MOE_GROUPS
    gmax = jnp.max(jnp.where(gmask, logits, NEG_INF), axis=1, keepdims=True)
    ge = jnp.where(gmask, jnp.exp(logits - gmax), 0.0)
    gp = ge / jnp.sum(ge, axis=1, keepdims=True)
    g_w = jnp.max(gp, axis=1, keepdims=True)
    g_idx = jnp.min(jnp.where(gmask, jnp.where(gp == g_w, lane_f, big), big), axis=1, keepdims=True)
    grp_of_lane = ((lane - MOE_GROUPS) >> EPG_LOG2).astype(F32)
    emask = jnp.where(lane >= MOE_GROUPS, grp_of_lane, -1.0) == g_idx
    emax = jnp.max(jnp.where(emask, logits, NEG_INF), axis=1, keepdims=True)
    ee = jnp.where(emask, jnp.exp(logits - emax), 0.0)
    ep = jnp.where(emask, ee / jnp.sum(ee, axis=1, keepdims=True), -1.0)
    v1 = jnp.max(ep, axis=1, keepdims=True)
    i1 = jnp.min(jnp.where(ep == v1, lane_f, big), axis=1, keepdims=True)
    ep2 = jnp.where(lane_f == i1, -1.0, ep)
    v2 = jnp.max(ep2, axis=1, keepdims=True)
    i2 = jnp.min(jnp.where(ep2 == v2, lane_f, big), axis=1, keepdims=True)
    w0 = g_w * v1 / (v1 + v2)
    w1 = g_w * v2 / (v1 + v2)
    e0 = i1 - float(MOE_GROUPS)
    e1 = i2 - float(MOE_GROUPS)

    oh0 = jnp.where(lane_f == e0, 1.0, 0.0)
    oh1 = jnp.where(lane_f == e1, 1.0, 0.0)
    pre0 = jnp.dot(tril_ref[...], oh0.astype(BF16), preferred_element_type=F32)
    pre1 = jnp.dot(tril_ref[...], oh1.astype(BF16), preferred_element_type=F32)
    carry = carry_ref[...]
    tot0 = pre0[tm - 1:tm, :]
    tot1 = pre1[tm - 1:tm, :]
    rank0 = jnp.sum(oh0 * (pre0 - 1.0 + carry), axis=1, keepdims=True)
    rank1 = jnp.sum(oh1 * (pre1 - 1.0 + carry + tot0), axis=1, keepdims=True)
    new_carry = carry + tot0 + tot1
    carry_ref[...] = new_carry
    cnt_ref[...] = jnp.broadcast_to(new_carry, cnt_ref.shape)

    rt = jnp.where(lane == 0, e0, jnp.where(lane == 1, e1, jnp.where(lane == 2, w0, jnp.where(
        lane == 3, w1, jnp.where(lane == 4, rank0, jnp.where(lane == 5, rank1, 0.0))))))
    rt_ref[...] = rt


def _out_proj(nsa_o, ml_o, w_out, x2, g2, w_r, b_r, tril):
    T = x2.shape[0]
    half = D_MODEL // 2
    return pl.pallas_call(
        _outproj_kernel,
        grid=(T // TM_OUT,),
        in_specs=[
            pl.BlockSpec((TM_OUT, half), lambda i: (i, 0)),
            pl.BlockSpec((TM_OUT, half), lambda i: (i, 0)),
            pl.BlockSpec((D_MODEL, D_MODEL), lambda i: (0, 0)),
            pl.BlockSpec((TM_OUT, D_MODEL), lambda i: (i, 0)),
            pl.BlockSpec((1, D_MODEL), lambda i: (0, 0)),
            pl.BlockSpec((D_MODEL, LANE), lambda i: (0, 0)),
            pl.BlockSpec((1, LANE), lambda i: (0, 0)),
            pl.BlockSpec((TM_OUT, TM_OUT), lambda i: (0, 0)),
        ],
        out_specs=[
            pl.BlockSpec((TM_OUT, D_MODEL), lambda i: (i, 0)),
            pl.BlockSpec((TM_OUT, half), lambda i: (i, 0)),
            pl.BlockSpec((TM_OUT, LANE), lambda i: (i, 0)),
            pl.BlockSpec((8, LANE), lambda i: (0, 0)),
        ],
        out_shape=[
            jax.ShapeDtypeStruct((T, D_MODEL), F32),
            jax.ShapeDtypeStruct((T, half), U32),
            jax.ShapeDtypeStruct((T, LANE), F32),
            jax.ShapeDtypeStruct((8, LANE), F32),
        ],
        scratch_shapes=[pltpu.VMEM((1, LANE), F32)],
        compiler_params=_cparams(("arbitrary",)),
        name="out_proj",
    )(nsa_o, ml_o, w_out, x2, g2, w_r, b_r, tril)


def _expert_kernel(be_ref, tok_ref, nxt_ref, meta_ref, xp_hbm, wg_hbm, wu_hbm, wd_hbm, y_ref,
                   xa_ref, xb_ref, gsem_ref, wsg_ref, wsu_ref, wsd_ref, wsem_ref, wgb_ref, wub_ref, wdb_ref):
    i = pl.program_id(0)
    n_used = meta_ref[0]
    half = D_MODEL // 2
    bufs = (xa_ref, xb_ref)

    def row_copy(blk, slot, r):
        tok = tok_ref[blk * BM + r]
        return pltpu.make_async_copy(xp_hbm.at[pl.ds(tok, 1), :], bufs[slot].at[pl.ds(r, 1), :],
                                     gsem_ref.at[slot])

    def wait_rows(slot):
        pltpu.make_async_copy(xp_hbm.at[pl.ds(0, BM), :], bufs[slot], gsem_ref.at[slot]).wait()

    def weight_copies(e):
        return (pltpu.make_async_copy(wg_hbm.at[e], wsg_ref, wsem_ref.at[0]),
                pltpu.make_async_copy(wu_hbm.at[e], wsu_ref, wsem_ref.at[1]),
                pltpu.make_async_copy(wd_hbm.at[e], wsd_ref, wsem_ref.at[2]))

    def step(slot):
        e = be_ref[i]
        wait_rows(slot)

        @pl.when((i == 0) | (e != be_ref[jnp.maximum(i - 1, 0)]))
        def _():
            for c in weight_copies(e):
                c.wait()
            wgb_ref[...] = wsg_ref[...].astype(BF16)
            wub_ref[...] = wsu_ref[...].astype(BF16)
            wdb_ref[...] = wsd_ref[...].astype(BF16)
            nx = nxt_ref[e]

            @pl.when(nx >= 0)
            def _():
                for c in weight_copies(nx):
                    c.start()

        for r in range(BM):
            row_copy(i + 1, 1 - slot, r).start()

        lo, hi = _unpack_bf16_pair(bufs[slot][...])
        xl = lo.astype(BF16)
        xh = hi.astype(BF16)
        gt = (jnp.dot(xl, wgb_ref[0:half, :], preferred_element_type=F32)
              + jnp.dot(xh, wgb_ref[half:, :], preferred_element_type=F32))
        up = (jnp.dot(xl, wub_ref[0:half, :], preferred_element_type=F32)
              + jnp.dot(xh, wub_ref[half:, :], preferred_element_type=F32))
        hb = (gt * _sigmoid(gt) * up).astype(BF16)
        y = jnp.dot(hb, wdb_ref[...], preferred_element_type=F32)
        y_ref[...] = _pack_bf16_pair(y[:, :half], y[:, half:])

        @pl.when(i == n_used - 1)
        def _():
            wait_rows(1 - slot)

    @pl.when(i == 0)
    def _():
        for c in weight_copies(be_ref[0]):
            c.start()

        def body(r, _):
            row_copy(0, 0, r).start()
            return 0
        lax.fori_loop(0, BM, body, 0, unroll=8)

    @pl.when((i < n_used) & (i % 2 == 0))
    def _():
        step(0)

    @pl.when((i < n_used) & (i % 2 == 1))
    def _():
        step(1)

    @pl.when(i >= n_used)
    def _():
        y_ref[...] = jnp.zeros_like(y_ref)


def _experts(block_expert, row_tok, next_expert, meta, xp, w_gate, w_up, w_down, n_rows):
    half = D_MODEL // 2
    nb = n_rows // BM
    grid_spec = pltpu.PrefetchScalarGridSpec(
        num_scalar_prefetch=4,
        grid=(nb,),
        in_specs=[pl.BlockSpec(memory_space=pl.ANY)] * 4,
        out_specs=pl.BlockSpec((BM, half), lambda i, be, tok, nxt, meta: (i, 0)),
        scratch_shapes=[
            pltpu.VMEM((BM, half), U32),
            pltpu.VMEM((BM, half), U32),
            pltpu.SemaphoreType.DMA((2,)),
            pltpu.VMEM((D_MODEL, D_EXPERT), F32),
            pltpu.VMEM((D_MODEL, D_EXPERT), F32),
            pltpu.VMEM((D_EXPERT, D_MODEL), F32),
            pltpu.SemaphoreType.DMA((3,)),
            pltpu.VMEM((D_MODEL, D_EXPERT), BF16),
            pltpu.VMEM((D_MODEL, D_EXPERT), BF16),
            pltpu.VMEM((D_EXPERT, D_MODEL), BF16),
        ],
    )
    return pl.pallas_call(
        _expert_kernel,
        grid_spec=grid_spec,
        out_shape=jax.ShapeDtypeStruct((n_rows, half), U32),
        compiler_params=_cparams(("arbitrary",)),
        name="experts",
    )(block_expert, row_tok, next_expert, meta, xp, w_gate, w_up, w_down)


def _combine_kernel(dest_ref, y_hbm, x1_ref, rt_ref, fg_ref, o_ref, ybuf_ref, sem_ref):
    i = pl.program_id(0)
    nt = pl.num_programs(0)
    half = D_MODEL // 2

    def start_gather(tile, slot):
        def body(r, _):
            for k in range(2):
                d = dest_ref[(tile * TC + r) * 2 + k]
                pltpu.make_async_copy(y_hbm.at[pl.ds(d, 1), :], ybuf_ref.at[slot, k, pl.ds(r, 1), :],
                                      sem_ref.at[slot]).start()
            return 0
        lax.fori_loop(0, TC, body, 0, unroll=8)

    @pl.when(i == 0)
    def _():
        start_gather(0, 0)

    @pl.when(i + 1 < nt)
    def _():
        start_gather(i + 1, (i + 1) % 2)

    slot = i % 2
    for k in range(2):
        pltpu.make_async_copy(y_hbm.at[pl.ds(0, TC), :], ybuf_ref.at[slot, k], sem_ref.at[slot]).wait()

    rt = rt_ref[...]
    w0 = rt[:, 2:3]
    w1 = rt[:, 3:4]
    lo0, hi0 = _unpack_bf16_pair(ybuf_ref[slot, 0])
    lo1, hi1 = _unpack_bf16_pair(ybuf_ref[slot, 1])
    xl = x1_ref[:, :half] + (w0 * lo0 + w1 * lo1)
    xh = x1_ref[:, half:] + (w0 * hi0 + w1 * hi1)
    ms = (jnp.sum(xl * xl, axis=1, keepdims=True) + jnp.sum(xh * xh, axis=1, keepdims=True)) / float(D_MODEL)
    inv = lax.rsqrt(ms + NORM_EPS)
    o_ref[:, :half] = xl * inv * fg_ref[:, :half]
    o_ref[:, half:] = xh * inv * fg_ref[:, half:]


def _combine(dest, ys, x1, route, fg):
    T = x1.shape[0]
    half = D_MODEL // 2
    grid_spec = pltpu.PrefetchScalarGridSpec(
        num_scalar_prefetch=1,
        grid=(T // TC,),
        in_specs=[
            pl.BlockSpec(memory_space=pl.ANY),
            pl.BlockSpec((TC, D_MODEL), lambda i, d: (i, 0)),
            pl.BlockSpec((TC, LANE), lambda i, d: (i, 0)),
            pl.BlockSpec((1, D_MODEL), lambda i, d: (0, 0)),
        ],
        out_specs=pl.BlockSpec((TC, D_MODEL), lambda i, d: (i, 0)),
        scratch_shapes=[
            pltpu.VMEM((2, 2, TC, half), U32),
            pltpu.SemaphoreType.DMA((2,)),
        ],
    )
    return pl.pallas_call(
        _combine_kernel,
        grid_spec=grid_spec,
        out_shape=jax.ShapeDtypeStruct((T, D_MODEL), F32),
        compiler_params=_cparams(("arbitrary",)),
        name="combine",
    )(dest, ys, x1, route, fg)


def _cover_matrix(S):
    n_blk = S // CMP_STRIDE
    n_sel = S // SEL_BLOCK
    cs = np.arange(n_blk) * CMP_STRIDE
    ss = np.arange(n_sel) * SEL_BLOCK
    shared = np.minimum(cs[:, None] + CMP_LEN, ss[None, :] + SEL_BLOCK) - np.maximum(cs[:, None], ss[None, :])
    return (np.clip(shared, 0, None) / CMP_LEN).T.astype(np.float32)


def _block_mask_matrix(S):
    n_sel = S // SEL_BLOCK
    assert n_sel <= LANE
    e = np.zeros((S, LANE), np.float32)
    e[np.arange(S), np.arange(S) // SEL_BLOCK] = NEG_INF
    return e


def _inv_freq_row():
    inv = np.power(np.float32(ROPE_THETA), -np.arange(ROPE_HALF, dtype=np.float32) * 2.0 / ROPE_DIM)
    row = np.zeros((1, LANE), np.float32)
    row[0, :ROPE_HALF] = inv
    row[0, ROPE_HALF:ROPE_DIM] = inv
    return row


def _layer(x2, positions, B, S, norm1_g, w_in, cmp_pe_k, cmp_pe_v, cmp_wk1, cmp_wk2, cmp_wv1, cmp_wv2,
           nsa_norm_g, conv_w, conv_b, b_igate, b_fgate, mlstm_norm_g, w_out, norm2_g,
           w_group, b_group, w_router, b_router, w_exp_gate, w_exp_up, w_exp_down, out_norm_g):
    T = B * S
    o_q, o_kv, o_g, o_qk, o_v, o_o, o_i, o_f = 0, 1024, 2560, 2584, 4632, 5656, 6680, 6684
    w_main = jnp.concatenate([w_in[:, o_q:o_g], w_in[:, o_qk:o_i]], axis=1).astype(BF16)
    w_small = jnp.concatenate([w_in[:, o_g:o_qk], w_in[:, o_i:], jnp.zeros((D_MODEL, LANE - 32), F32)],
                              axis=1).astype(BF16)
    posb = jnp.broadcast_to(positions.reshape(T, 1).astype(F32), (T, LANE))
    invf = jnp.asarray(_inv_freq_row())

    proj, small = _in_proj(x2, norm1_g.reshape(1, -1), w_main, w_small)
    q_r, k_r = _rope(posb, invf, proj)
    kc, vc = _compress(k_r, proj, B, S, cmp_pe_k, cmp_pe_v, cmp_wk1.astype(BF16), cmp_wk2.astype(BF16),
                       cmp_wv1.astype(BF16), cmp_wv2.astype(BF16))
    eye = np.eye(S // SEL_BLOCK, LANE, dtype=np.float32)
    nsa_o = _nsa(q_r, k_r, proj, small, kc, vc, jnp.asarray(_cover_matrix(S), BF16),
                 jnp.asarray(_block_mask_matrix(S), BF16), jnp.asarray(eye, BF16),
                 nsa_norm_g.reshape(1, -1), B, S)

    if_arr = small[:, 24:32].reshape(B, S, 2, ML_HEADS).transpose(0, 3, 2, 1)
    gate_bias = jnp.stack([b_igate, b_fgate]).astype(F32)
    ml_o = _mlstm(proj, if_arr, gate_bias, conv_w, conv_b.reshape(1, -1), mlstm_norm_g.reshape(1, -1), B, S)

    w_r = jnp.concatenate([w_group, w_router, jnp.zeros((D_MODEL, LANE - MOE_GROUPS - N_EXPERTS), F32)],
                          axis=1).astype(BF16)
    b_r = jnp.concatenate([b_group, b_router, jnp.zeros((LANE - MOE_GROUPS - N_EXPERTS,), F32)]).reshape(1, LANE)
    tril = jnp.asarray(np.tril(np.ones((TM_OUT, TM_OUT), np.float32)), BF16)
    x1, xp, route, cnt = _out_proj(nsa_o, ml_o, w_out.astype(BF16), x2, norm2_g.reshape(1, -1), w_r, b_r, tril)

    n_rows = T * 2 + N_EXPERTS * BM
    counts = cnt[0, :N_EXPERTS].astype(jnp.int32)
    pcounts = (counts + BM - 1) // BM * BM
    pends = jnp.cumsum(pcounts)
    pstarts = pends - pcounts
    eid = route[:, 0:2].astype(jnp.int32)
    dest = (pstarts[eid] + route[:, 4:6].astype(jnp.int32)).reshape(T * 2)
    tok = jnp.repeat(jnp.arange(T, dtype=jnp.int32), 2)
    row_tok = jnp.zeros((n_rows + BM,), jnp.int32).at[dest].set(tok)
    block_expert = jnp.minimum(jnp.searchsorted(pends, jnp.arange(n_rows // BM) * BM, side='right'),
                               N_EXPERTS - 1).astype(jnp.int32)
    present = jnp.where(counts > 0, jnp.arange(N_EXPERTS, dtype=jnp.int32), N_EXPERTS)
    later = jnp.concatenate([lax.cummin(present[::-1])[::-1][1:], jnp.full((1,), N_EXPERTS, jnp.int32)])
    next_expert = jnp.where(later < N_EXPERTS, later, -1).astype(jnp.int32)
    meta = (pends[-1:] // BM).astype(jnp.int32)

    ys = _experts(block_expert, row_tok, next_expert, meta, xp, w_exp_gate, w_exp_up, w_exp_down, n_rows)
    return _combine(dest, ys, x1, route, out_norm_g.reshape(1, -1))


def kernel(x, positions, norm1_g, w_in, cmp_pe_k, cmp_pe_v, cmp_wk1, cmp_wk2, cmp_wv1, cmp_wv2, nsa_norm_g,
           conv_w, conv_b, b_igate, b_fgate, mlstm_norm_g, w_out, norm2_g, w_group, b_group, w_router,
           b_router, w_exp_gate, w_exp_up, w_exp_down, final_norm_g):
    B, S, D = x.shape
    assert D == D_MODEL and norm1_g.shape[0] == 1, "single-layer, D_MODEL-wide configuration only"
    assert S % ML_CHUNK == 0 and S % SEL_BUCKET == 0 and S >= WIN_KEYS and (B * S) % TM_IN == 0
    out = _layer(x.reshape(B * S, D), positions, B, S, norm1_g[0], w_in[0], cmp_pe_k[0], cmp_pe_v[0],
                 cmp_wk1[0], cmp_wk2[0], cmp_wv1[0], cmp_wv2[0], nsa_norm_g[0], conv_w[0], conv_b[0],
                 b_igate[0], b_fgate[0], mlstm_norm_g[0], w_out[0], norm2_g[0], w_group[0], b_group[0],
                 w_router[0], b_router[0], w_exp_gate[0], w_exp_up[0], w_exp_down[0], final_norm_g)
    return out.reshape(B, S, D)
```

```python
import functools

import numpy as np
import jax
import jax.numpy as jnp
from jax import lax
from jax.experimental import pallas as pl
from jax.experimental.pallas import tpu as pltpu

F32 = jnp.float32
BF16 = jnp.bfloat16
U32 = jnp.uint32

D_MODEL = 2048
NSA_HEADS = 8
NSA_GROUPS = 2
NSA_J = NSA_HEADS // NSA_GROUPS
HD = 128
CMP_LEN = 32
CMP_STRIDE = 16
SEL_BLOCK = 64
SEL_TOPK = 8
WINDOW = 512
ROPE_THETA = 500000.0
ROPE_DIM = 32
ROPE_HALF = 16
ML_HEADS = 4
ML_HD = 256
CONV_W = 4
MOE_GROUPS = 8
EPG = 8
N_EXPERTS = 64
D_EXPERT = 512
SEL_BLOCK_LOG2 = 6
EPG_LOG2 = 3
assert (1 << SEL_BLOCK_LOG2) == SEL_BLOCK and (1 << EPG_LOG2) == EPG
NORM_EPS = 1e-6
NEG_INF = -1e30
FORCE_SCORE = 1000.0

COL_Q = 0
COL_KV = 1024
COL_QKM = 2560
COL_VM = 4608
COL_OM = 5632
N_MAIN = 6656

LANE = 128
VMEM_LIMIT = 56 * 1024 * 1024

TM_IN = 1024
TN_IN = 512
TS_ROPE = 256
TQ = 128
SEL_BUCKET = 512
WIN_KEYS = WINDOW + TQ
ML_CHUNK = 256
TM_OUT = 512
BM = 256
TC = 256

NT_DIMS = (((1,), (1,)), ((), ()))
TN_DIMS = (((0,), (0,)), ((), ()))


def _cparams(sem):
    return pltpu.CompilerParams(dimension_semantics=sem, vmem_limit_bytes=VMEM_LIMIT)


def _sigmoid(x):
    return 1.0 / (1.0 + jnp.exp(-x))


def _inproj_kernel(x_ref, g_ref, w_ref, ws_ref, o_ref, os_ref, h_ref):
    @pl.when(pl.program_id(1) == 0)
    def _():
        x = x_ref[...]
        ms = jnp.mean(x * x, axis=-1, keepdims=True)
        h_ref[...] = (x * lax.rsqrt(ms + NORM_EPS) * g_ref[...]).astype(BF16)
        os_ref[...] = jnp.dot(h_ref[...], ws_ref[...], preferred_element_type=F32)

    o_ref[...] = jnp.dot(h_ref[...], w_ref[...], preferred_element_type=F32).astype(o_ref.dtype)


def _in_proj(x2, g1, w_main, w_small):
    T = x2.shape[0]
    return pl.pallas_call(
        _inproj_kernel,
        grid=(T // TM_IN, N_MAIN // TN_IN),
        in_specs=[
            pl.BlockSpec((TM_IN, D_MODEL), lambda m, n: (m, 0)),
            pl.BlockSpec((1, D_MODEL), lambda m, n: (0, 0)),
            pl.BlockSpec((D_MODEL, TN_IN), lambda m, n: (0, n)),
            pl.BlockSpec((D_MODEL, LANE), lambda m, n: (0, 0)),
        ],
        out_specs=[
            pl.BlockSpec((TM_IN, TN_IN), lambda m, n: (m, n)),
            pl.BlockSpec((TM_IN, LANE), lambda m, n: (m, 0)),
        ],
        out_shape=[
            jax.ShapeDtypeStruct((T, N_MAIN), BF16),
            jax.ShapeDtypeStruct((T, LANE), F32),
        ],
        scratch_shapes=[pltpu.VMEM((TM_IN, D_MODEL), BF16)],
        compiler_params=_cparams(("parallel", "arbitrary")),
        name="in_proj",
    )(x2, g1, w_main, w_small)


def _rope_kernel(pos_ref, invf_ref, q_ref, kc_ref, ks_ref, kw_ref, qo_ref, ko_ref):
    ang = pos_ref[...] * invf_ref[...]
    c = jnp.cos(ang)
    s = jnp.sin(ang)
    lane = lax.broadcasted_iota(jnp.int32, ang.shape, 1)
    sa = jnp.where(lane < ROPE_HALF, -s, 0.0)
    sb = jnp.where(lane < ROPE_HALF, 0.0, s)

    def rope(x):
        return x * c + pltpu.roll(x, LANE - ROPE_HALF, 1) * sa + pltpu.roll(x, ROPE_HALF, 1) * sb

    scale = HD ** -0.5
    for h in range(NSA_HEADS):
        sl = slice(h * HD, (h + 1) * HD)
        qo_ref[:, sl] = (rope(q_ref[:, sl].astype(F32)) * scale).astype(BF16)
    for i, r in enumerate((kc_ref, ks_ref, kw_ref)):
        for g in range(NSA_GROUPS):
            sl = slice(g * HD, (g + 1) * HD)
            so = slice(i * 2 * HD + g * HD, i * 2 * HD + (g + 1) * HD)
            ko_ref[:, so] = rope(r[:, sl].astype(F32)).astype(BF16)


def _rope(posb, invf, proj):
    T = proj.shape[0]
    kvb = COL_KV // 256
    return pl.pallas_call(
        _rope_kernel,
        grid=(T // TS_ROPE,),
        in_specs=[
            pl.BlockSpec((TS_ROPE, LANE), lambda i: (i, 0)),
            pl.BlockSpec((1, LANE), lambda i: (0, 0)),
            pl.BlockSpec((TS_ROPE, 1024), lambda i: (i, 0)),
            pl.BlockSpec((TS_ROPE, 256), lambda i: (i, kvb + 0)),
            pl.BlockSpec((TS_ROPE, 256), lambda i: (i, kvb + 2)),
            pl.BlockSpec((TS_ROPE, 256), lambda i: (i, kvb + 4)),
        ],
        out_specs=[
            pl.BlockSpec((TS_ROPE, 1024), lambda i: (i, 0)),
            pl.BlockSpec((TS_ROPE, 768), lambda i: (i, 0)),
        ],
        out_shape=[
            jax.ShapeDtypeStruct((T, 1024), BF16),
            jax.ShapeDtypeStruct((T, 768), BF16),
        ],
        compiler_params=_cparams(("parallel",)),
        name="rope",
    )(posb, invf, proj, proj, proj, proj)


def _compress_kernel(k_ref, v_ref, pek_ref, pev_ref, w1k_ref, w2k_ref, w1v_ref, w2v_ref,
                     kc_ref, vc_ref, xs_ref):
    S = k_ref.shape[0]
    n_blk = S // CMP_STRIDE
    for src, pe, w1, w2, dst in ((k_ref, pek_ref, w1k_ref, w2k_ref, kc_ref),
                                 (v_ref, pev_ref, w1v_ref, w2v_ref, vc_ref)):
        xs_ref[0:S, :] = src[...].astype(F32)
        xs_ref[S:S + CMP_LEN, :] = jnp.zeros((CMP_LEN, HD), F32)
        acc = jnp.zeros((n_blk, HD), F32)
        for l in range(CMP_LEN):
            a = xs_ref[pl.ds(l, n_blk, stride=CMP_STRIDE), :] + pe[l:l + 1, :]
            acc = acc + jnp.dot(a.astype(BF16), w1[l * HD:(l + 1) * HD, :],
                                preferred_element_type=F32)
        hid = acc * _sigmoid(acc)
        out = jnp.dot(hid.astype(BF16), w2[...], preferred_element_type=F32)
        dst[0, 0] = out.astype(BF16)


def _compress(k_r, proj, B, S, pek, pev, w1k, w2k, w1v, w2v):
    n_blk = S // CMP_STRIDE
    vcol = (COL_KV + 256) // HD
    full = lambda shape: pl.BlockSpec(shape, lambda b, g: tuple(0 for _ in shape))
    return pl.pallas_call(
        _compress_kernel,
        grid=(B, NSA_GROUPS),
        in_specs=[
            pl.BlockSpec((S, HD), lambda b, g: (b, g)),
            pl.BlockSpec((S, HD), lambda b, g: (b, vcol + g)),
            full((CMP_LEN, HD)), full((CMP_LEN, HD)),
            full((CMP_LEN * HD, HD)), full((HD, HD)),
            full((CMP_LEN * HD, HD)), full((HD, HD)),
        ],
        out_specs=[
            pl.BlockSpec((1, 1, n_blk, HD), lambda b, g: (b, g, 0, 0)),
            pl.BlockSpec((1, 1, n_blk, HD), lambda b, g: (b, g, 0, 0)),
        ],
        out_shape=[
            jax.ShapeDtypeStruct((B, NSA_GROUPS, n_blk, HD), BF16),
            jax.ShapeDtypeStruct((B, NSA_GROUPS, n_blk, HD), BF16),
        ],
        scratch_shapes=[pltpu.VMEM((S + CMP_LEN, HD), F32)],
        compiler_params=_cparams(("parallel", "parallel")),
        name="compress",
    )(k_r, proj, pek, pev, w1k, w2k, w1v, w2v)


def _nsa_kernel(q_ref, kc_ref, vc_ref, ks_ref, vs_ref, kw_ref, vw_ref, sm_ref, covt_ref, eneg_ref, eye_ref,
                ng_ref, o_ref, obuf_ref, owin_ref, osel_ref):
    qi = pl.program_id(1)
    q0 = qi * TQ
    R = NSA_J * TQ
    n_cmp = kc_ref.shape[2]
    n_sel = ks_ref.shape[0] // SEL_BLOCK
    n_win = WIN_KEYS // TQ

    def row_t(shape):
        r = lax.broadcasted_iota(jnp.int32, shape, 0)
        return q0 + (r & (TQ - 1))

    qgs = [jnp.concatenate([q_ref[:, (g * NSA_J + j) * HD:(g * NSA_J + j + 1) * HD]
                            for j in range(NSA_J)], axis=0) for g in range(NSA_GROUPS)]

    def with_ones(v):
        lane = lax.broadcasted_iota(jnp.int32, v.shape, 1)
        return jnp.concatenate([v, jnp.where(lane == 0, 1.0, 0.0).astype(BF16)], axis=1)

    def window(masks):
        w0 = pl.multiple_of(jnp.maximum(q0 - WINDOW, 0), TQ)
        for g in range(NSA_GROUPS):
            kt = kw_ref[pl.ds(w0, WIN_KEYS), g * HD:(g + 1) * HD]
            vt = with_ones(vw_ref[pl.ds(w0, WIN_KEYS), g * HD:(g + 1) * HD])
            sc = masks(lax.dot_general(qgs[g], kt, NT_DIMS, preferred_element_type=F32), w0)
            mw = jnp.max(sc, axis=1, keepdims=True)
            pw = jnp.exp((sc - mw).astype(BF16))
            acc = jnp.dot(pw, vt, preferred_element_type=F32)
            owin_ref[g] = acc[:, 0:HD] / acc[:, HD:HD + 1]

    def band_masks(sc, w0):
        d = (lax.broadcasted_iota(jnp.int32, (R, TQ), 1)
             - (lax.broadcasted_iota(jnp.int32, (R, TQ), 0) & (TQ - 1)))
        first = jnp.where(d > 0, sc[:, 0:TQ], NEG_INF)
        last = jnp.where(d <= 0, sc[:, (n_win - 1) * TQ:], NEG_INF)
        return jnp.concatenate([first, sc[:, TQ:(n_win - 1) * TQ], last], axis=1)

    def general_masks(sc, w0):
        diff = row_t((R, WIN_KEYS)) - (w0 + lax.broadcasted_iota(jnp.int32, (R, WIN_KEYS), 1))
        keep = jnp.where(diff >= 0, diff, WINDOW) < WINDOW
        return jnp.where(keep, sc, NEG_INF)

    @pl.when(q0 >= WINDOW)
    def _():
        window(band_masks)

    @pl.when(q0 < WINDOW)
    def _():
        window(general_masks)

    gates = _sigmoid(sm_ref[...])
    o_cmps = []
    qps = []
    for g in range(NSA_GROUPS):
        qg = qgs[g]

        s = lax.dot_general(qg, kc_ref[0, g], NT_DIMS, preferred_element_type=F32)
        n_lane = lax.broadcasted_iota(jnp.int32, (R, n_cmp), 1)
        cmask = (n_lane * CMP_STRIDE + (CMP_LEN - 1)) <= row_t((R, n_cmp))
        s = jnp.where(cmask, s, NEG_INF)
        m = jnp.max(s, axis=1, keepdims=True)
        e = jnp.where(cmask, jnp.exp(s - m), 0.0)
        l = jnp.sum(e, axis=1, keepdims=True)
        p = (e / jnp.where(l > 0.0, l, 1.0)).astype(BF16)
        o_cmps.append(jnp.dot(p, vc_ref[0, g], preferred_element_type=F32))
        impr = lax.dot_general(covt_ref[...], p, NT_DIMS, preferred_element_type=F32)
        imp = impr[:, 0:TQ]
        for j in range(1, NSA_J):
            imp = imp + impr[:, j * TQ:(j + 1) * TQ]

        m_sub = lax.broadcasted_iota(jnp.int32, (n_sel, TQ), 0)
        jt = (q0 + lax.broadcasted_iota(jnp.int32, (n_sel, TQ), 1)) >> SEL_BLOCK_LOG2
        forced = jnp.where(m_sub == 0, FORCE_SCORE,
                           jnp.where(m_sub == jt, FORCE_SCORE,
                                     jnp.where(m_sub == jt - 1, FORCE_SCORE, 0.0)))
        score = jnp.where(m_sub <= jt, imp + forced, -1.0)
        rank = jnp.zeros((n_sel, TQ), F32)
        for mp in range(n_sel):
            row = score[mp:mp + 1, :]
            ge = jnp.where(row >= score, 1.0, 0.0)
            gt = jnp.where(row > score, 1.0, 0.0)
            rank = rank + jnp.where(m_sub > mp, ge, gt)
        notsel_t = jnp.where(rank < float(min(SEL_TOPK, n_sel)), 0.0, 1.0).astype(BF16)
        notsel = lax.dot_general(notsel_t, eye_ref[...], TN_DIMS,
                                 preferred_element_type=F32).astype(BF16)
        qps.append(jnp.concatenate([qg, jnp.concatenate([notsel] * NSA_J, axis=0)], axis=1))

    def sel_oneshot(nk):
        for g in range(NSA_GROUPS):
            kt = jnp.concatenate([ks_ref[0:nk, g * HD:(g + 1) * HD], eneg_ref[0:nk, :]], axis=1)
            vt = with_ones(vs_ref[0:nk, g * HD:(g + 1) * HD])
            sc = lax.dot_general(qps[g], kt, NT_DIMS, preferred_element_type=F32)
            klane = lax.broadcasted_iota(jnp.int32, (R, SEL_BUCKET), 1) + (nk - SEL_BUCKET)
            tail = jnp.where(klane <= row_t((R, SEL_BUCKET)), sc[:, nk - SEL_BUCKET:], NEG_INF)
            sc = tail if nk == SEL_BUCKET else jnp.concatenate([sc[:, 0:nk - SEL_BUCKET], tail], axis=1)
            ms = jnp.max(sc, axis=1, keepdims=True)
            acc = jnp.dot(jnp.exp((sc - ms).astype(BF16)), vt, preferred_element_type=F32)
            osel_ref[g] = acc[:, 0:HD] / acc[:, HD:HD + 1]

    for b in range(ks_ref.shape[0] // SEL_BUCKET):
        @pl.when(q0 // SEL_BUCKET == b)
        def _(b=b):
            sel_oneshot((b + 1) * SEL_BUCKET)

    ssq = jnp.zeros((TQ, 1), F32)
    for g in range(NSA_GROUPS):
        o_sel = osel_ref[g]
        o_cmp = o_cmps[g]
        o_win = owin_ref[g]
        for j in range(NSA_J):
            h = g * NSA_J + j
            rs = slice(j * TQ, (j + 1) * TQ)
            o = (gates[:, 3 * h:3 * h + 1] * o_cmp[rs] + gates[:, 3 * h + 1:3 * h + 2] * o_sel[rs]
                 + gates[:, 3 * h + 2:3 * h + 3] * o_win[rs])
            ssq = ssq + jnp.sum(o * o, axis=1, keepdims=True)
            obuf_ref[:, h * HD:(h + 1) * HD] = o

    inv = lax.rsqrt(ssq / float(NSA_HEADS * HD) + NORM_EPS)
    o_ref[...] = (obuf_ref[...] * inv * ng_ref[...]).astype(BF16)


def _nsa(q_r, k_r, proj, small, kc, vc, covt, eneg, eye, ng, B, S):
    T = B * S
    nq = S // TQ
    n_blk = S // CMP_STRIDE
    n_sel = S // SEL_BLOCK
    kvb = COL_KV // 256
    return pl.pallas_call(
        _nsa_kernel,
        grid=(B, nq),
        in_specs=[
            pl.BlockSpec((TQ, 1024), lambda b, i: (b * nq + i, 0)),
            pl.BlockSpec((1, NSA_GROUPS, n_blk, HD), lambda b, i: (b, 0, 0, 0)),
            pl.BlockSpec((1, NSA_GROUPS, n_blk, HD), lambda b, i: (b, 0, 0, 0)),
            pl.BlockSpec((S, 256), lambda b, i: (b, 1)),
            pl.BlockSpec((S, 256), lambda b, i: (b, kvb + 3)),
            pl.BlockSpec((S, 256), lambda b, i: (b, 2)),
            pl.BlockSpec((S, 256), lambda b, i: (b, kvb + 5)),
            pl.BlockSpec((TQ, LANE), lambda b, i: (b * nq + i, 0)),
            pl.BlockSpec((n_sel, n_blk), lambda b, i: (0, 0)),
            pl.BlockSpec((S, LANE), lambda b, i: (0, 0)),
            pl.BlockSpec((n_sel, LANE), lambda b, i: (0, 0)),
            pl.BlockSpec((1, 1024), lambda b, i: (0, 0)),
        ],
        out_specs=pl.BlockSpec((TQ, 1024), lambda b, i: (b * nq + i, 0)),
        out_shape=jax.ShapeDtypeStruct((T, 1024), BF16),
        scratch_shapes=[pltpu.VMEM((TQ, 1024), F32), pltpu.VMEM((NSA_GROUPS, NSA_J * TQ, HD), F32),
                        pltpu.VMEM((NSA_GROUPS, NSA_J * TQ, HD), F32)],
        compiler_params=_cparams(("parallel", "parallel")),
        name="nsa",
    )(q_r, kc, vc, k_r, proj, k_r, proj, small, covt, eneg, eye, ng)


def _mlstm_kernel(bias_ref, q_ref, k_ref, v_ref, om_ref, if_ref, cwq_ref, cwk_ref, cbq_ref, cbk_ref,
                  ng_ref, o_ref, c_ref, n_ref, m_ref, tq_ref, tk_ref):
    h = pl.program_id(1)
    L = ML_CHUNK

    @pl.when(pl.program_id(2) == 0)
    def _():
        c_ref[...] = jnp.zeros_like(c_ref)
        n_ref[...] = jnp.zeros_like(n_ref)
        m_ref[...] = jnp.zeros_like(m_ref)
        tq_ref[...] = jnp.zeros_like(tq_ref)
        tk_ref[...] = jnp.zeros_like(tk_ref)

    row8 = lax.broadcasted_iota(jnp.int32, (8, ML_HD), 0)

    def conv_silu(x, tail_ref, w_ref, b_ref):
        tail = tail_ref[...]
        y = x * w_ref[CONV_W - 1:CONV_W, :] + b_ref[...]
        for k in range(1, CONV_W):
            xs = pltpu.roll(x, k, 0)
            fix = jnp.where(row8 < k, pltpu.roll(tail, k, 0), xs[0:8])
            xs = jnp.concatenate([fix, xs[8:]], axis=0)
            y = y + xs * w_ref[CONV_W - 1 - k:CONV_W - k, :]
        tail_ref[...] = x[L - 8:L]
        return y * _sigmoid(y)

    qf = conv_silu(q_ref[...].astype(F32), tq_ref, cwq_ref, cbq_ref) * (ML_HD ** -0.5)
    kf = conv_silu(k_ref[...].astype(F32), tk_ref, cwk_ref, cbk_ref)
    qb = qf.astype(BF16)
    kb = kf.astype(BF16)
    vb = v_ref[...]
    vf = vb.astype(F32)

    ic = if_ref[0, 0, 0:1, :] + bias_ref[0, h]
    fp = if_ref[0, 0, 1:2, :] + bias_ref[1, h]
    fc = jnp.minimum(fp, 0.0) - jnp.log(1.0 + jnp.exp(-jnp.abs(fp)))

    r = lax.broadcasted_iota(jnp.int32, (L, L), 0)
    cidx = lax.broadcasted_iota(jnp.int32, (L, L), 1)
    tril = cidx <= r
    eye = cidx == r
    fc_b = jnp.broadcast_to(fc, (L, L))
    ic_b = jnp.broadcast_to(ic, (L, L))
    b_col = jnp.sum(jnp.where(tril, fc_b, 0.0), axis=1, keepdims=True)
    fc_col = jnp.sum(jnp.where(eye, fc_b, 0.0), axis=1, keepdims=True)
    ic_col = jnp.sum(jnp.where(eye, ic_b, 0.0), axis=1, keepdims=True)
    b_row = jnp.sum(jnp.where(r <= cidx, jnp.broadcast_to(fc_col, (L, L)), 0.0),
                    axis=0, keepdims=True)
    b_last = b_col[L - 1:L, :]
    m_prev = m_ref[...]

    d_log = jnp.where(tril, b_col - b_row + ic, NEG_INF)
    inter = b_col + m_prev
    m_t = jnp.maximum(inter, jnp.max(d_log, axis=1, keepdims=True))
    w_intra = jnp.exp(d_log - m_t)
    w_inter = jnp.exp(inter - m_t)
    qk = lax.dot_general(qb, kb, NT_DIMS, preferred_element_type=F32) * w_intra
    num = (jnp.dot(qk.astype(BF16), vb, preferred_element_type=F32)
           + w_inter * lax.dot_general(qb, c_ref[...].astype(BF16), NT_DIMS, preferred_element_type=F32))
    den = jnp.sum(qk, axis=1, keepdims=True) + w_inter * jnp.sum(qf * n_ref[...], axis=1, keepdims=True)
    hm = num / jnp.maximum(jnp.abs(den), jnp.exp(-m_t))

    w_log = b_last - b_col + ic_col
    m_new = jnp.maximum(b_last + m_prev, jnp.max(w_log, axis=0, keepdims=True))
    w_state = jnp.exp(w_log - m_new)
    decay = jnp.exp(b_last + m_prev - m_new)
    c_ref[...] = decay * c_ref[...] + lax.dot_general((w_state * vf).astype(BF16), kb, TN_DIMS,
                                                      preferred_element_type=F32)
    n_ref[...] = decay * n_ref[...] + jnp.sum(w_state * kf, axis=0, keepdims=True)
    m_ref[...] = m_new

    hn = hm * lax.rsqrt(jnp.mean(hm * hm, axis=1, keepdims=True) + NORM_EPS) * ng_ref[...]
    o_ref[...] = (hn * _sigmoid(om_ref[...].astype(F32))).astype(BF16)


def _mlstm(proj, if_arr, gate_bias, conv_w, conv_b, ng, B, S):
    T = B * S
    nc = S // ML_CHUNK
    cq, ck, cv, co = COL_QKM // ML_HD, COL_QKM // ML_HD + ML_HEADS, COL_VM // ML_HD, COL_OM // ML_HD
    rows = lambda col0: pl.BlockSpec((ML_CHUNK, ML_HD), lambda b, h, c: (b * nc + c, col0 + h))
    return pl.pallas_call(
        _mlstm_kernel,
        grid=(B, ML_HEADS, nc),
        in_specs=[
            pl.BlockSpec(memory_space=pltpu.SMEM),
            rows(cq), rows(ck), rows(cv), rows(co),
            pl.BlockSpec((1, 1, 2, ML_CHUNK), lambda b, h, c: (b, h, 0, c)),
            pl.BlockSpec((CONV_W, ML_HD), lambda b, h, c: (0, h)),
            pl.BlockSpec((CONV_W, ML_HD), lambda b, h, c: (0, ML_HEADS + h)),
            pl.BlockSpec((1, ML_HD), lambda b, h, c: (0, h)),
            pl.BlockSpec((1, ML_HD), lambda b, h, c: (0, ML_HEADS + h)),
            pl.BlockSpec((1, ML_HD), lambda b, h, c: (0, h)),
        ],
        out_specs=pl.BlockSpec((ML_CHUNK, ML_HD), lambda b, h, c: (b * nc + c, h)),
        out_shape=jax.ShapeDtypeStruct((T, ML_HEADS * ML_HD), BF16),
        scratch_shapes=[
            pltpu.VMEM((ML_HD, ML_HD), F32), pltpu.VMEM((1, ML_HD), F32), pltpu.VMEM((1, 1), F32),
            pltpu.VMEM((8, ML_HD), F32), pltpu.VMEM((8, ML_HD), F32),
        ],
        compiler_params=_cparams(("parallel", "parallel", "arbitrary")),
        name="mlstm",
    )(gate_bias, proj, proj, proj, proj, if_arr, conv_w, conv_w, conv_b, conv_b, ng)


def _pack_bf16_pair(lo, hi):
    lo_b = pltpu.bitcast(lo.astype(BF16).astype(F32), U32)
    hi_b = pltpu.bitcast(hi.astype(BF16).astype(F32), U32)
    return (lo_b >> 16) | hi_b


def _unpack_bf16_pair(p):
    lo = pltpu.bitcast(p << 16, F32)
    hi = pltpu.bitcast(p & jnp.uint32(0xFFFF0000), F32)
    return lo, hi


ROW_TILES = D_MODEL // 2 // LANE


def _gather_dst(buf, r_tile, s):
    return buf.at[r_tile, :, s, :]


def _gathered_to_dense(ref):
    n = ref.shape[0] * 8
    return jnp.concatenate([ref[:, c].reshape(n, LANE) for c in range(ROW_TILES)], axis=1)


def _dense_to_rows(ref, val):
    for k in range(ROW_TILES):
        ref[:, k, :] = val[:, k * LANE:(k + 1) * LANE]


def _outproj_kernel(nsa_ref, ml_ref, w_ref, x_ref, g2_ref, wr_ref, br_ref, tril_ref,
                    x1_ref, xp_ref, rt_ref, cnt_ref, carry_ref):
    @pl.when(pl.program_id(0) == 0)
    def _():
        carry_ref[...] = jnp.zeros_like(carry_ref)

    half = D_MODEL // 2
    acc = jnp.dot(nsa_ref[...], w_ref[0:half, :], preferred_element_type=F32)
    acc = acc + jnp.dot(ml_ref[...], w_ref[half:, :], preferred_element_type=F32)
    x1 = x_ref[...] + acc
    x1_ref[...] = x1
    xn = x1 * lax.rsqrt(jnp.mean(x1 * x1, axis=-1, keepdims=True) + NORM_EPS) * g2_ref[...]
    _dense_to_rows(xp_ref, _pack_bf16_pair(xn[:, :half], xn[:, half:]))
    logits = jnp.dot(xn.astype(BF16), wr_ref[...], preferred_element_type=F32) + br_ref[...]

    tm = logits.shape[0]
    lane = lax.broadcasted_iota(jnp.int32, (tm, LANE), 1)
    lane_f = lane.astype(F32)
    big = float(LANE)
    gmask = lane < MOE_GROUPS
    gmax = jnp.max(jnp.where(gmask, logits, NEG_INF), axis=1, keepdims=True)
    ge = jnp.where(gmask, jnp.exp(logits - gmax), 0.0)
    gp = ge / jnp.sum(ge, axis=1, keepdims=True)
    g_w = jnp.max(gp, axis=1, keepdims=True)
    g_idx = jnp.min(jnp.where(gmask, jnp.where(gp == g_w, lane_f, big), big), axis=1, keepdims=True)
    grp_of_lane = ((lane - MOE_GROUPS) >> EPG_LOG2).astype(F32)
    emask = jnp.where(lane >= MOE_GROUPS, grp_of_lane, -1.0) == g_idx
    emax = jnp.max(jnp.where(emask, logits, NEG_INF), axis=1, keepdims=True)
    ee = jnp.where(emask, jnp.exp(logits - emax), 0.0)
    ep = jnp.where(emask, ee / jnp.sum(ee, axis=1, keepdims=True), -1.0)
    v1 = jnp.max(ep, axis=1, keepdims=True)
    i1 = jnp.min(jnp.where(ep == v1, lane_f, big), axis=1, keepdims=True)
    ep2 = jnp.where(lane_f == i1, -1.0, ep)
    v2 = jnp.max(ep2, axis=1, keepdims=True)
    i2 = jnp.min(jnp.where(ep2 == v2, lane_f, big), axis=1, keepdims=True)
    w0 = g_w * v1 / (v1 + v2)
    w1 = g_w * v2 / (v1 + v2)
    e0 = i1 - float(MOE_GROUPS)
    e1 = i2 - float(MOE_GROUPS)

    oh0 = jnp.where(lane_f == e0, 1.0, 0.0)
    oh1 = jnp.where(lane_f == e1, 1.0, 0.0)
    pre0 = jnp.dot(tril_ref[...], oh0.astype(BF16), preferred_element_type=F32)
    pre1 = jnp.dot(tril_ref[...], oh1.astype(BF16), preferred_element_type=F32)
    carry = carry_ref[...]
    tot0 = pre0[tm - 1:tm, :]
    tot1 = pre1[tm - 1:tm, :]
    rank0 = jnp.sum(oh0 * (pre0 - 1.0 + carry), axis=1, keepdims=True)
    rank1 = jnp.sum(oh1 * (pre1 - 1.0 + carry + tot0), axis=1, keepdims=True)
    new_carry = carry + tot0 + tot1
    carry_ref[...] = new_carry
    cnt_ref[...] = jnp.broadcast_to(new_carry, cnt_ref.shape)

    rt = jnp.where(lane == 0, e0, jnp.where(lane == 1, e1, jnp.where(lane == 2, w0, jnp.where(
        lane == 3, w1, jnp.where(lane == 4, rank0, jnp.where(lane == 5, rank1, 0.0))))))
    rt_ref[...] = rt


def _out_proj(nsa_o, ml_o, w_out, x2, g2, w_r, b_r, tril):
    T = x2.shape[0]
    half = D_MODEL // 2
    return pl.pallas_call(
        _outproj_kernel,
        grid=(T // TM_OUT,),
        in_specs=[
            pl.BlockSpec((TM_OUT, half), lambda i: (i, 0)),
            pl.BlockSpec((TM_OUT, half), lambda i: (i, 0)),
            pl.BlockSpec((D_MODEL, D_MODEL), lambda i: (0, 0)),
            pl.BlockSpec((TM_OUT, D_MODEL), lambda i: (i, 0)),
            pl.BlockSpec((1, D_MODEL), lambda i: (0, 0)),
            pl.BlockSpec((D_MODEL, LANE), lambda i: (0, 0)),
            pl.BlockSpec((1, LANE), lambda i: (0, 0)),
            pl.BlockSpec((TM_OUT, TM_OUT), lambda i: (0, 0)),
        ],
        out_specs=[
            pl.BlockSpec((TM_OUT, D_MODEL), lambda i: (i, 0)),
            pl.BlockSpec((TM_OUT, ROW_TILES, LANE), lambda i: (i, 0, 0)),
            pl.BlockSpec((TM_OUT, LANE), lambda i: (i, 0)),
            pl.BlockSpec((8, LANE), lambda i: (0, 0)),
        ],
        out_shape=[
            jax.ShapeDtypeStruct((T, D_MODEL), F32),
            jax.ShapeDtypeStruct((T, ROW_TILES, LANE), U32),
            jax.ShapeDtypeStruct((T, LANE), F32),
            jax.ShapeDtypeStruct((8, LANE), F32),
        ],
        scratch_shapes=[pltpu.VMEM((1, LANE), F32)],
        compiler_params=_cparams(("arbitrary",)),
        name="out_proj",
    )(nsa_o, ml_o, w_out, x2, g2, w_r, b_r, tril)


def _expert_kernel(be_ref, tok_ref, nxt_ref, meta_ref, xp_hbm, wg_hbm, wu_hbm, wd_hbm, y_ref,
                   xa_ref, xb_ref, gsem_ref, wsg_ref, wsu_ref, wsd_ref, wsem_ref, wgb_ref, wub_ref, wdb_ref):
    i = pl.program_id(0)
    n_used = meta_ref[0]
    half = D_MODEL // 2
    bufs = (xa_ref, xb_ref)

    def row_copy(blk, slot, r):
        tok = tok_ref[blk * BM + r]
        return pltpu.make_async_copy(xp_hbm.at[tok], _gather_dst(bufs[slot], r // 8, r % 8), gsem_ref.at[slot])

    def wait_rows(slot):
        pltpu.make_async_copy(bufs[slot], bufs[slot], gsem_ref.at[slot]).wait()

    def weight_copies(e):
        return (pltpu.make_async_copy(wg_hbm.at[e], wsg_ref, wsem_ref.at[0]),
                pltpu.make_async_copy(wu_hbm.at[e], wsu_ref, wsem_ref.at[1]),
                pltpu.make_async_copy(wd_hbm.at[e], wsd_ref, wsem_ref.at[2]))

    def step(slot):
        e = be_ref[i]
        wait_rows(slot)

        @pl.when((i == 0) | (e != be_ref[jnp.maximum(i - 1, 0)]))
        def _():
            for c in weight_copies(e):
                c.wait()
            wgb_ref[...] = wsg_ref[...].astype(BF16)
            wub_ref[...] = wsu_ref[...].astype(BF16)
            wdb_ref[...] = wsd_ref[...].astype(BF16)
            nx = nxt_ref[e]

            @pl.when(nx >= 0)
            def _():
                for c in weight_copies(nx):
                    c.start()

        for r in range(BM):
            row_copy(i + 1, 1 - slot, r).start()

        lo, hi = _unpack_bf16_pair(_gathered_to_dense(bufs[slot]))
        xl = lo.astype(BF16)
        xh = hi.astype(BF16)
        gt = (jnp.dot(xl, wgb_ref[0:half, :], preferred_element_type=F32)
              + jnp.dot(xh, wgb_ref[half:, :], preferred_element_type=F32))
        up = (jnp.dot(xl, wub_ref[0:half, :], preferred_element_type=F32)
              + jnp.dot(xh, wub_ref[half:, :], preferred_element_type=F32))
        hb = (gt * _sigmoid(gt) * up).astype(BF16)
        y = jnp.dot(hb, wdb_ref[...], preferred_element_type=F32)
        _dense_to_rows(y_ref, _pack_bf16_pair(y[:, :half], y[:, half:]))

        @pl.when(i == n_used - 1)
        def _():
            wait_rows(1 - slot)

    @pl.when(i == 0)
    def _():
        for c in weight_copies(be_ref[0]):
            c.start()

        def body(rt, _):
            for s in range(8):
                pltpu.make_async_copy(xp_hbm.at[tok_ref[rt * 8 + s]], _gather_dst(xa_ref, rt, s),
                                      gsem_ref.at[0]).start()
            return 0
        lax.fori_loop(0, BM // 8, body, 0)

    @pl.when((i < n_used) & (i % 2 == 0))
    def _():
        step(0)

    @pl.when((i < n_used) & (i % 2 == 1))
    def _():
        step(1)

    @pl.when(i >= n_used)
    def _():
        y_ref[...] = jnp.zeros_like(y_ref)


def _experts(block_expert, row_tok, next_expert, meta, xp, w_gate, w_up, w_down, n_rows):
    half = D_MODEL // 2
    nb = n_rows // BM
    grid_spec = pltpu.PrefetchScalarGridSpec(
        num_scalar_prefetch=4,
        grid=(nb,),
        in_specs=[pl.BlockSpec(memory_space=pl.ANY)] * 4,
        out_specs=pl.BlockSpec((BM, ROW_TILES, LANE), lambda i, be, tok, nxt, meta: (i, 0, 0)),
        scratch_shapes=[
            pltpu.VMEM((BM // 8, ROW_TILES, 8, LANE), U32),
            pltpu.VMEM((BM // 8, ROW_TILES, 8, LANE), U32),
            pltpu.SemaphoreType.DMA((2,)),
            pltpu.VMEM((D_MODEL, D_EXPERT), F32),
            pltpu.VMEM((D_MODEL, D_EXPERT), F32),
            pltpu.VMEM((D_EXPERT, D_MODEL), F32),
            pltpu.SemaphoreType.DMA((3,)),
            pltpu.VMEM((D_MODEL, D_EXPERT), BF16),
            pltpu.VMEM((D_MODEL, D_EXPERT), BF16),
            pltpu.VMEM((D_EXPERT, D_MODEL), BF16),
        ],
    )
    return pl.pallas_call(
        _expert_kernel,
        grid_spec=grid_spec,
        out_shape=jax.ShapeDtypeStruct((n_rows, ROW_TILES, LANE), U32),
        compiler_params=_cparams(("arbitrary",)),
        name="experts",
    )(block_expert, row_tok, next_expert, meta, xp, w_gate, w_up, w_down)


def _combine_kernel(dest_ref, y_hbm, x1_ref, rt_ref, fg_ref, o_ref, ya_ref, yb_ref, sem_ref):
    i = pl.program_id(0)
    nt = pl.num_programs(0)
    half = D_MODEL // 2
    bufs = (ya_ref, yb_ref)

    def row_copy(tile, slot, r_tile, s, k):
        d = dest_ref[(tile * TC + r_tile * 8 + s) * 2 + k]
        return pltpu.make_async_copy(y_hbm.at[d], _gather_dst(bufs[slot].at[k], r_tile, s), sem_ref.at[slot])

    def wait_rows(slot):
        pltpu.make_async_copy(bufs[slot], bufs[slot], sem_ref.at[slot]).wait()

    def step(slot):
        wait_rows(slot)
        for r in range(TC):
            for k in range(2):
                row_copy(i + 1, 1 - slot, r // 8, r % 8, k).start()
        rt = rt_ref[...]
        w0 = rt[:, 2:3]
        w1 = rt[:, 3:4]
        lo0, hi0 = _unpack_bf16_pair(_gathered_to_dense(bufs[slot].at[0]))
        lo1, hi1 = _unpack_bf16_pair(_gathered_to_dense(bufs[slot].at[1]))
        xl = x1_ref[:, :half] + (w0 * lo0 + w1 * lo1)
        xh = x1_ref[:, half:] + (w0 * hi0 + w1 * hi1)
        ms = (jnp.sum(xl * xl, axis=1, keepdims=True) + jnp.sum(xh * xh, axis=1, keepdims=True)) / float(D_MODEL)
        inv = lax.rsqrt(ms + NORM_EPS)
        o_ref[:, :half] = xl * inv * fg_ref[:, :half]
        o_ref[:, half:] = xh * inv * fg_ref[:, half:]

        @pl.when(i == nt - 1)
        def _():
            wait_rows(1 - slot)

    @pl.when(i == 0)
    def _():
        def body(rt, _):
            for s in range(8):
                for k in range(2):
                    row_copy(0, 0, rt, s, k).start()
            return 0
        lax.fori_loop(0, TC // 8, body, 0)

    @pl.when(i % 2 == 0)
    def _():
        step(0)

    @pl.when(i % 2 == 1)
    def _():
        step(1)


def _combine(dest, ys, x1, route, fg):
    T = x1.shape[0]
    half = D_MODEL // 2
    grid_spec = pltpu.PrefetchScalarGridSpec(
        num_scalar_prefetch=1,
        grid=(T // TC,),
        in_specs=[
            pl.BlockSpec(memory_space=pl.ANY),
            pl.BlockSpec((TC, D_MODEL), lambda i, d: (i, 0)),
            pl.BlockSpec((TC, LANE), lambda i, d: (i, 0)),
            pl.BlockSpec((1, D_MODEL), lambda i, d: (0, 0)),
        ],
        out_specs=pl.BlockSpec((TC, D_MODEL), lambda i, d: (i, 0)),
        scratch_shapes=[
            pltpu.VMEM((2, TC // 8, ROW_TILES, 8, LANE), U32),
            pltpu.VMEM((2, TC // 8, ROW_TILES, 8, LANE), U32),
            pltpu.SemaphoreType.DMA((2,)),
        ],
    )
    return pl.pallas_call(
        _combine_kernel,
        grid_spec=grid_spec,
        out_shape=jax.ShapeDtypeStruct((T, D_MODEL), F32),
        compiler_params=_cparams(("arbitrary",)),
        name="combine",
    )(dest, ys, x1, route, fg)


def _cover_matrix(S):
    n_blk = S // CMP_STRIDE
    n_sel = S // SEL_BLOCK
    cs = np.arange(n_blk) * CMP_STRIDE
    ss = np.arange(n_sel) * SEL_BLOCK
    shared = np.minimum(cs[:, None] + CMP_LEN, ss[None, :] + SEL_BLOCK) - np.maximum(cs[:, None], ss[None, :])
    return (np.clip(shared, 0, None) / CMP_LEN).T.astype(np.float32)


def _block_mask_matrix(S):
    n_sel = S // SEL_BLOCK
    assert n_sel <= LANE
    e = np.zeros((S, LANE), np.float32)
    e[np.arange(S), np.arange(S) // SEL_BLOCK] = NEG_INF
    return e


def _inv_freq_row():
    inv = np.power(np.float32(ROPE_THETA), -np.arange(ROPE_HALF, dtype=np.float32) * 2.0 / ROPE_DIM)
    row = np.zeros((1, LANE), np.float32)
    row[0, :ROPE_HALF] = inv
    row[0, ROPE_HALF:ROPE_DIM] = inv
    return row


def _layer(x2, positions, B, S, norm1_g, w_in, cmp_pe_k, cmp_pe_v, cmp_wk1, cmp_wk2, cmp_wv1, cmp_wv2,
           nsa_norm_g, conv_w, conv_b, b_igate, b_fgate, mlstm_norm_g, w_out, norm2_g,
           w_group, b_group, w_router, b_router, w_exp_gate, w_exp_up, w_exp_down, out_norm_g):
    T = B * S
    o_q, o_kv, o_g, o_qk, o_v, o_o, o_i, o_f = 0, 1024, 2560, 2584, 4632, 5656, 6680, 6684
    w_main = jnp.concatenate([w_in[:, o_q:o_g], w_in[:, o_qk:o_i]], axis=1).astype(BF16)
    w_small = jnp.concatenate([w_in[:, o_g:o_qk], w_in[:, o_i:], jnp.zeros((D_MODEL, LANE - 32), F32)],
                              axis=1).astype(BF16)
    posb = jnp.broadcast_to(positions.reshape(T, 1).astype(F32), (T, LANE))
    invf = jnp.asarray(_inv_freq_row())

    proj, small = _in_proj(x2, norm1_g.reshape(1, -1), w_main, w_small)
    q_r, k_r = _rope(posb, invf, proj)
    kc, vc = _compress(k_r, proj, B, S, cmp_pe_k, cmp_pe_v, cmp_wk1.astype(BF16), cmp_wk2.astype(BF16),
                       cmp_wv1.astype(BF16), cmp_wv2.astype(BF16))
    eye = np.eye(S // SEL_BLOCK, LANE, dtype=np.float32)
    nsa_o = _nsa(q_r, k_r, proj, small, kc, vc, jnp.asarray(_cover_matrix(S), BF16),
                 jnp.asarray(_block_mask_matrix(S), BF16), jnp.asarray(eye, BF16),
                 nsa_norm_g.reshape(1, -1), B, S)

    if_arr = small[:, 24:32].reshape(B, S, 2, ML_HEADS).transpose(0, 3, 2, 1)
    gate_bias = jnp.stack([b_igate, b_fgate]).astype(F32)
    ml_o = _mlstm(proj, if_arr, gate_bias, conv_w, conv_b.reshape(1, -1), mlstm_norm_g.reshape(1, -1), B, S)

    w_r = jnp.concatenate([w_group, w_router, jnp.zeros((D_MODEL, LANE - MOE_GROUPS - N_EXPERTS), F32)],
                          axis=1).astype(BF16)
    b_r = jnp.concatenate([b_group, b_router, jnp.zeros((LANE - MOE_GROUPS - N_EXPERTS,), F32)]).reshape(1, LANE)
    tril = jnp.asarray(np.tril(np.ones((TM_OUT, TM_OUT), np.float32)), BF16)
    x1, xp, route, cnt = _out_proj(nsa_o, ml_o, w_out.astype(BF16), x2, norm2_g.reshape(1, -1), w_r, b_r, tril)

    n_rows = T * 2 + N_EXPERTS * BM
    counts = cnt[0, :N_EXPERTS].astype(jnp.int32)
    pcounts = (counts + BM - 1) // BM * BM
    pends = jnp.cumsum(pcounts)
    pstarts = pends - pcounts
    eid = route[:, 0:2].astype(jnp.int32)
    dest = (pstarts[eid] + route[:, 4:6].astype(jnp.int32)).reshape(T * 2)
    tok = jnp.repeat(jnp.arange(T, dtype=jnp.int32), 2)
    row_tok = jnp.zeros((n_rows + BM,), jnp.int32).at[dest].set(tok)
    block_expert = jnp.minimum(jnp.searchsorted(pends, jnp.arange(n_rows // BM) * BM, side='right'),
                               N_EXPERTS - 1).astype(jnp.int32)
    present = jnp.where(counts > 0, jnp.arange(N_EXPERTS, dtype=jnp.int32), N_EXPERTS)
    later = jnp.concatenate([lax.cummin(present[::-1])[::-1][1:], jnp.full((1,), N_EXPERTS, jnp.int32)])
    next_expert = jnp.where(later < N_EXPERTS, later, -1).astype(jnp.int32)
    meta = (pends[-1:] // BM).astype(jnp.int32)

    ys = _experts(block_expert, row_tok, next_expert, meta, xp, w_exp_gate, w_exp_up, w_exp_down, n_rows)
    dest_pad = jnp.concatenate([dest, jnp.zeros((2 * TC,), jnp.int32)])
    return _combine(dest_pad, ys, x1, route, out_norm_g.reshape(1, -1))


def kernel(x, positions, norm1_g, w_in, cmp_pe_k, cmp_pe_v, cmp_wk1, cmp_wk2, cmp_wv1, cmp_wv2, nsa_norm_g,
           conv_w, conv_b, b_igate, b_fgate, mlstm_norm_g, w_out, norm2_g, w_group, b_group, w_router,
           b_router, w_exp_gate, w_exp_up, w_exp_down, final_norm_g):
    B, S, D = x.shape
    assert D == D_MODEL and norm1_g.shape[0] == 1, "single-layer, D_MODEL-wide configuration only"
    assert S % ML_CHUNK == 0 and S % SEL_BUCKET == 0 and S >= WIN_KEYS and (B * S) % TM_IN == 0
    out = _layer(x.reshape(B * S, D), positions, B, S, norm1_g[0], w_in[0], cmp_pe_k[0], cmp_pe_v[0],
                 cmp_wk1[0], cmp_wk2[0], cmp_wv1[0], cmp_wv2[0], nsa_norm_g[0], conv_w[0], conv_b[0],
                 b_igate[0], b_fgate[0], mlstm_norm_g[0], w_out[0], norm2_g[0], w_group[0], b_group[0],
                 w_router[0], b_router[0], w_exp_gate[0], w_exp_up[0], w_exp_down[0], final_norm_g)
    return out.reshape(B, S, D)
```

```python
import functools

import numpy as np
import jax
import jax.numpy as jnp
from jax import lax
from jax.experimental import pallas as pl
from jax.experimental.pallas import tpu as pltpu

F32 = jnp.float32
BF16 = jnp.bfloat16
U32 = jnp.uint32

D_MODEL = 2048
NSA_HEADS = 8
NSA_GROUPS = 2
NSA_J = NSA_HEADS // NSA_GROUPS
HD = 128
CMP_LEN = 32
CMP_STRIDE = 16
SEL_BLOCK = 64
SEL_TOPK = 8
WINDOW = 512
ROPE_THETA = 500000.0
ROPE_DIM = 32
ROPE_HALF = 16
ML_HEADS = 4
ML_HD = 256
CONV_W = 4
MOE_GROUPS = 8
EPG = 8
N_EXPERTS = 64
D_EXPERT = 512
SEL_BLOCK_LOG2 = 6
EPG_LOG2 = 3
assert (1 << SEL_BLOCK_LOG2) == SEL_BLOCK and (1 << EPG_LOG2) == EPG
NORM_EPS = 1e-6
NEG_INF = -1e30
FORCE_SCORE = 1000.0

COL_Q = 0
COL_KV = 1024
COL_QKM = 2560
COL_VM = 4608
COL_OM = 5632
N_MAIN = 6656

LANE = 128
VMEM_LIMIT = 56 * 1024 * 1024

TM_IN = 1024
TN_IN = 512
TS_ROPE = 256
TQ = 128
SEL_BUCKET = 512
WIN_KEYS = WINDOW + TQ
ML_CHUNK = 256
TM_OUT = 512
BM = 256
TC = 256

NT_DIMS = (((1,), (1,)), ((), ()))
TN_DIMS = (((0,), (0,)), ((), ()))


def _cparams(sem):
    return pltpu.CompilerParams(dimension_semantics=sem, vmem_limit_bytes=VMEM_LIMIT)


def _sigmoid(x):
    return 1.0 / (1.0 + jnp.exp(-x))


def _inproj_kernel(x_ref, g_ref, w_ref, ws_ref, o_ref, os_ref, h_ref):
    @pl.when(pl.program_id(1) == 0)
    def _():
        x = x_ref[...]
        ms = jnp.mean(x * x, axis=-1, keepdims=True)
        h_ref[...] = (x * lax.rsqrt(ms + NORM_EPS) * g_ref[...]).astype(BF16)
        os_ref[...] = jnp.dot(h_ref[...], ws_ref[...], preferred_element_type=F32)

    o_ref[...] = jnp.dot(h_ref[...], w_ref[...], preferred_element_type=F32).astype(o_ref.dtype)


def _in_proj(x2, g1, w_main, w_small):
    T = x2.shape[0]
    return pl.pallas_call(
        _inproj_kernel,
        grid=(T // TM_IN, N_MAIN // TN_IN),
        in_specs=[
            pl.BlockSpec((TM_IN, D_MODEL), lambda m, n: (m, 0)),
            pl.BlockSpec((1, D_MODEL), lambda m, n: (0, 0)),
            pl.BlockSpec((D_MODEL, TN_IN), lambda m, n: (0, n)),
            pl.BlockSpec((D_MODEL, LANE), lambda m, n: (0, 0)),
        ],
        out_specs=[
            pl.BlockSpec((TM_IN, TN_IN), lambda m, n: (m, n)),
            pl.BlockSpec((TM_IN, LANE), lambda m, n: (m, 0)),
        ],
        out_shape=[
            jax.ShapeDtypeStruct((T, N_MAIN), BF16),
            jax.ShapeDtypeStruct((T, LANE), F32),
        ],
        scratch_shapes=[pltpu.VMEM((TM_IN, D_MODEL), BF16)],
        compiler_params=_cparams(("parallel", "arbitrary")),
        name="in_proj",
    )(x2, g1, w_main, w_small)


def _rope_kernel(pos_ref, invf_ref, q_ref, kc_ref, ks_ref, kw_ref, qo_ref, ko_ref):
    ang = pos_ref[...] * invf_ref[...]
    c = jnp.cos(ang)
    s = jnp.sin(ang)
    lane = lax.broadcasted_iota(jnp.int32, ang.shape, 1)
    sa = jnp.where(lane < ROPE_HALF, -s, 0.0)
    sb = jnp.where(lane < ROPE_HALF, 0.0, s)

    def rope(x):
        return x * c + pltpu.roll(x, LANE - ROPE_HALF, 1) * sa + pltpu.roll(x, ROPE_HALF, 1) * sb

    scale = HD ** -0.5
    for h in range(NSA_HEADS):
        sl = slice(h * HD, (h + 1) * HD)
        qo_ref[:, sl] = (rope(q_ref[:, sl].astype(F32)) * scale).astype(BF16)
    for i, r in enumerate((kc_ref, ks_ref, kw_ref)):
        for g in range(NSA_GROUPS):
            sl = slice(g * HD, (g + 1) * HD)
            so = slice(i * 2 * HD + g * HD, i * 2 * HD + (g + 1) * HD)
            ko_ref[:, so] = rope(r[:, sl].astype(F32)).astype(BF16)


def _rope(posb, invf, proj):
    T = proj.shape[0]
    kvb = COL_KV // 256
    return pl.pallas_call(
        _rope_kernel,
        grid=(T // TS_ROPE,),
        in_specs=[
            pl.BlockSpec((TS_ROPE, LANE), lambda i: (i, 0)),
            pl.BlockSpec((1, LANE), lambda i: (0, 0)),
            pl.BlockSpec((TS_ROPE, 1024), lambda i: (i, 0)),
            pl.BlockSpec((TS_ROPE, 256), lambda i: (i, kvb + 0)),
            pl.BlockSpec((TS_ROPE, 256), lambda i: (i, kvb + 2)),
            pl.BlockSpec((TS_ROPE, 256), lambda i: (i, kvb + 4)),
        ],
        out_specs=[
            pl.BlockSpec((TS_ROPE, 1024), lambda i: (i, 0)),
            pl.BlockSpec((TS_ROPE, 768), lambda i: (i, 0)),
        ],
        out_shape=[
            jax.ShapeDtypeStruct((T, 1024), BF16),
            jax.ShapeDtypeStruct((T, 768), BF16),
        ],
        compiler_params=_cparams(("parallel",)),
        name="rope",
    )(posb, invf, proj, proj, proj, proj)


def _compress_kernel(k_ref, v_ref, pek_ref, pev_ref, w1k_ref, w2k_ref, w1v_ref, w2v_ref,
                     kc_ref, vc_ref, xs_ref):
    S = k_ref.shape[0]
    n_blk = S // CMP_STRIDE
    for src, pe, w1, w2, dst in ((k_ref, pek_ref, w1k_ref, w2k_ref, kc_ref),
                                 (v_ref, pev_ref, w1v_ref, w2v_ref, vc_ref)):
        xs_ref[0:S, :] = src[...].astype(F32)
        xs_ref[S:S + CMP_LEN, :] = jnp.zeros((CMP_LEN, HD), F32)
        acc = jnp.zeros((n_blk, HD), F32)
        for l in range(CMP_LEN):
            a = xs_ref[pl.ds(l, n_blk, stride=CMP_STRIDE), :] + pe[l:l + 1, :]
            acc = acc + jnp.dot(a.astype(BF16), w1[l * HD:(l + 1) * HD, :],
                                preferred_element_type=F32)
        hid = acc * _sigmoid(acc)
        out = jnp.dot(hid.astype(BF16), w2[...], preferred_element_type=F32)
        dst[0, 0] = out.astype(BF16)


def _compress(k_r, proj, B, S, pek, pev, w1k, w2k, w1v, w2v):
    n_blk = S // CMP_STRIDE
    vcol = (COL_KV + 256) // HD
    full = lambda shape: pl.BlockSpec(shape, lambda b, g: tuple(0 for _ in shape))
    return pl.pallas_call(
        _compress_kernel,
        grid=(B, NSA_GROUPS),
        in_specs=[
            pl.BlockSpec((S, HD), lambda b, g: (b, g)),
            pl.BlockSpec((S, HD), lambda b, g: (b, vcol + g)),
            full((CMP_LEN, HD)), full((CMP_LEN, HD)),
            full((CMP_LEN * HD, HD)), full((HD, HD)),
            full((CMP_LEN * HD, HD)), full((HD, HD)),
        ],
        out_specs=[
            pl.BlockSpec((1, 1, n_blk, HD), lambda b, g: (b, g, 0, 0)),
            pl.BlockSpec((1, 1, n_blk, HD), lambda b, g: (b, g, 0, 0)),
        ],
        out_shape=[
            jax.ShapeDtypeStruct((B, NSA_GROUPS, n_blk, HD), BF16),
            jax.ShapeDtypeStruct((B, NSA_GROUPS, n_blk, HD), BF16),
        ],
        scratch_shapes=[pltpu.VMEM((S + CMP_LEN, HD), F32)],
        compiler_params=_cparams(("parallel", "parallel")),
        name="compress",
    )(k_r, proj, pek, pev, w1k, w2k, w1v, w2v)


def _nsa_kernel(q_ref, kc_ref, vc_ref, ks_ref, vs_ref, kw_ref, vw_ref, sm_ref, covt_ref, eneg_ref, eye_ref,
                ng_ref, o_ref, obuf_ref, owin_ref, osel_ref):
    qi = pl.program_id(1)
    q0 = qi * TQ
    R = NSA_J * TQ
    n_cmp = kc_ref.shape[2]
    n_sel = ks_ref.shape[0] // SEL_BLOCK
    n_win = WIN_KEYS // TQ

    def row_t(shape):
        r = lax.broadcasted_iota(jnp.int32, shape, 0)
        return q0 + (r & (TQ - 1))

    qgs = [jnp.concatenate([q_ref[:, (g * NSA_J + j) * HD:(g * NSA_J + j + 1) * HD]
                            for j in range(NSA_J)], axis=0) for g in range(NSA_GROUPS)]

    def with_ones(v):
        lane = lax.broadcasted_iota(jnp.int32, v.shape, 1)
        return jnp.concatenate([v, jnp.where(lane == 0, 1.0, 0.0).astype(BF16)], axis=1)

    def window(masks):
        w0 = pl.multiple_of(jnp.maximum(q0 - WINDOW, 0), TQ)
        for g in range(NSA_GROUPS):
            kt = kw_ref[pl.ds(w0, WIN_KEYS), g * HD:(g + 1) * HD]
            vt = with_ones(vw_ref[pl.ds(w0, WIN_KEYS), g * HD:(g + 1) * HD])
            sc = masks(lax.dot_general(qgs[g], kt, NT_DIMS, preferred_element_type=F32), w0)
            mw = jnp.max(sc, axis=1, keepdims=True)
            pw = jnp.exp((sc - mw).astype(BF16))
            acc = jnp.dot(pw, vt, preferred_element_type=F32)
            owin_ref[g] = acc[:, 0:HD] / acc[:, HD:HD + 1]

    def band_masks(sc, w0):
        d = (lax.broadcasted_iota(jnp.int32, (R, TQ), 1)
             - (lax.broadcasted_iota(jnp.int32, (R, TQ), 0) & (TQ - 1)))
        first = jnp.where(d > 0, sc[:, 0:TQ], NEG_INF)
        last = jnp.where(d <= 0, sc[:, (n_win - 1) * TQ:], NEG_INF)
        return jnp.concatenate([first, sc[:, TQ:(n_win - 1) * TQ], last], axis=1)

    def general_masks(sc, w0):
        diff = row_t((R, WIN_KEYS)) - (w0 + lax.broadcasted_iota(jnp.int32, (R, WIN_KEYS), 1))
        keep = jnp.where(diff >= 0, diff, WINDOW) < WINDOW
        return jnp.where(keep, sc, NEG_INF)

    @pl.when(q0 >= WINDOW)
    def _():
        window(band_masks)

    @pl.when(q0 < WINDOW)
    def _():
        window(general_masks)

    gates = _sigmoid(sm_ref[...])
    o_cmps = []
    qps = []
    for g in range(NSA_GROUPS):
        qg = qgs[g]

        s = lax.dot_general(qg, kc_ref[0, g], NT_DIMS, preferred_element_type=F32)
        n_lane = lax.broadcasted_iota(jnp.int32, (R, n_cmp), 1)
        cmask = (n_lane * CMP_STRIDE + (CMP_LEN - 1)) <= row_t((R, n_cmp))
        s = jnp.where(cmask, s, NEG_INF)
        m = jnp.max(s, axis=1, keepdims=True)
        e = jnp.where(cmask, jnp.exp(s - m), 0.0)
        l = jnp.sum(e, axis=1, keepdims=True)
        p = (e / jnp.where(l > 0.0, l, 1.0)).astype(BF16)
        o_cmps.append(jnp.dot(p, vc_ref[0, g], preferred_element_type=F32))
        impr = lax.dot_general(covt_ref[...], p, NT_DIMS, preferred_element_type=F32)
        imp = impr[:, 0:TQ]
        for j in range(1, NSA_J):
            imp = imp + impr[:, j * TQ:(j + 1) * TQ]

        m_sub = lax.broadcasted_iota(jnp.int32, (n_sel, TQ), 0)
        jt = (q0 + lax.broadcasted_iota(jnp.int32, (n_sel, TQ), 1)) >> SEL_BLOCK_LOG2
        forced = jnp.where(m_sub == 0, FORCE_SCORE,
                           jnp.where(m_sub == jt, FORCE_SCORE,
                                     jnp.where(m_sub == jt - 1, FORCE_SCORE, 0.0)))
        score = jnp.where(m_sub <= jt, imp + forced, -1.0)
        rank = jnp.zeros((n_sel, TQ), F32)
        for mp in range(n_sel):
            row = score[mp:mp + 1, :]
            ge = jnp.where(row >= score, 1.0, 0.0)
            gt = jnp.where(row > score, 1.0, 0.0)
            rank = rank + jnp.where(m_sub > mp, ge, gt)
        notsel_t = jnp.where(rank < float(min(SEL_TOPK, n_sel)), 0.0, 1.0).astype(BF16)
        notsel = lax.dot_general(notsel_t, eye_ref[...], TN_DIMS,
                                 preferred_element_type=F32).astype(BF16)
        qps.append(jnp.concatenate([qg, jnp.concatenate([notsel] * NSA_J, axis=0)], axis=1))

    def sel_oneshot(nk):
        for g in range(NSA_GROUPS):
            kt = jnp.concatenate([ks_ref[0:nk, g * HD:(g + 1) * HD], eneg_ref[0:nk, :]], axis=1)
            vt = with_ones(vs_ref[0:nk, g * HD:(g + 1) * HD])
            sc = lax.dot_general(qps[g], kt, NT_DIMS, preferred_element_type=F32)
            klane = lax.broadcasted_iota(jnp.int32, (R, SEL_BUCKET), 1) + (nk - SEL_BUCKET)
            tail = jnp.where(klane <= row_t((R, SEL_BUCKET)), sc[:, nk - SEL_BUCKET:], NEG_INF)
            sc = tail if nk == SEL_BUCKET else jnp.concatenate([sc[:, 0:nk - SEL_BUCKET], tail], axis=1)
            ms = jnp.max(sc, axis=1, keepdims=True)
            acc = jnp.dot(jnp.exp((sc - ms).astype(BF16)), vt, preferred_element_type=F32)
            osel_ref[g] = acc[:, 0:HD] / acc[:, HD:HD + 1]

    for b in range(ks_ref.shape[0] // SEL_BUCKET):
        @pl.when(q0 // SEL_BUCKET == b)
        def _(b=b):
            sel_oneshot((b + 1) * SEL_BUCKET)

    ssq = jnp.zeros((TQ, 1), F32)
    for g in range(NSA_GROUPS):
        o_sel = osel_ref[g]
        o_cmp = o_cmps[g]
        o_win = owin_ref[g]
        for j in range(NSA_J):
            h = g * NSA_J + j
            rs = slice(j * TQ, (j + 1) * TQ)
            o = (gates[:, 3 * h:3 * h + 1] * o_cmp[rs] + gates[:, 3 * h + 1:3 * h + 2] * o_sel[rs]
                 + gates[:, 3 * h + 2:3 * h + 3] * o_win[rs])
            ssq = ssq + jnp.sum(o * o, axis=1, keepdims=True)
            obuf_ref[:, h * HD:(h + 1) * HD] = o

    inv = lax.rsqrt(ssq / float(NSA_HEADS * HD) + NORM_EPS)
    o_ref[...] = (obuf_ref[...] * inv * ng_ref[...]).astype(BF16)


def _nsa(q_r, k_r, proj, small, kc, vc, covt, eneg, eye, ng, B, S):
    T = B * S
    nq = S // TQ
    n_blk = S // CMP_STRIDE
    n_sel = S // SEL_BLOCK
    kvb = COL_KV // 256
    return pl.pallas_call(
        _nsa_kernel,
        grid=(B, nq),
        in_specs=[
            pl.BlockSpec((TQ, 1024), lambda b, i: (b * nq + i, 0)),
            pl.BlockSpec((1, NSA_GROUPS, n_blk, HD), lambda b, i: (b, 0, 0, 0)),
            pl.BlockSpec((1, NSA_GROUPS, n_blk, HD), lambda b, i: (b, 0, 0, 0)),
            pl.BlockSpec((S, 256), lambda b, i: (b, 1)),
            pl.BlockSpec((S, 256), lambda b, i: (b, kvb + 3)),
            pl.BlockSpec((S, 256), lambda b, i: (b, 2)),
            pl.BlockSpec((S, 256), lambda b, i: (b, kvb + 5)),
            pl.BlockSpec((TQ, LANE), lambda b, i: (b * nq + i, 0)),
            pl.BlockSpec((n_sel, n_blk), lambda b, i: (0, 0)),
            pl.BlockSpec((S, LANE), lambda b, i: (0, 0)),
            pl.BlockSpec((n_sel, LANE), lambda b, i: (0, 0)),
            pl.BlockSpec((1, 1024), lambda b, i: (0, 0)),
        ],
        out_specs=pl.BlockSpec((TQ, 1024), lambda b, i: (b * nq + i, 0)),
        out_shape=jax.ShapeDtypeStruct((T, 1024), BF16),
        scratch_shapes=[pltpu.VMEM((TQ, 1024), F32), pltpu.VMEM((NSA_GROUPS, NSA_J * TQ, HD), F32),
                        pltpu.VMEM((NSA_GROUPS, NSA_J * TQ, HD), F32)],
        compiler_params=_cparams(("parallel", "parallel")),
        name="nsa",
    )(q_r, kc, vc, k_r, proj, k_r, proj, small, covt, eneg, eye, ng)


def _mlstm_kernel(bias_ref, q_ref, k_ref, v_ref, om_ref, if_ref, cwq_ref, cwk_ref, cbq_ref, cbk_ref,
                  ng_ref, o_ref, c_ref, n_ref, m_ref, tq_ref, tk_ref):
    h = pl.program_id(1)
    L = ML_CHUNK

    @pl.when(pl.program_id(2) == 0)
    def _():
        c_ref[...] = jnp.zeros_like(c_ref)
        n_ref[...] = jnp.zeros_like(n_ref)
        m_ref[...] = jnp.zeros_like(m_ref)
        tq_ref[...] = jnp.zeros_like(tq_ref)
        tk_ref[...] = jnp.zeros_like(tk_ref)

    row8 = lax.broadcasted_iota(jnp.int32, (8, ML_HD), 0)

    def conv_silu(x, tail_ref, w_ref, b_ref):
        tail = tail_ref[...]
        y = x * w_ref[CONV_W - 1:CONV_W, :] + b_ref[...]
        for k in range(1, CONV_W):
            xs = pltpu.roll(x, k, 0)
            fix = jnp.where(row8 < k, pltpu.roll(tail, k, 0), xs[0:8])
            xs = jnp.concatenate([fix, xs[8:]], axis=0)
            y = y + xs * w_ref[CONV_W - 1 - k:CONV_W - k, :]
        tail_ref[...] = x[L - 8:L]
        return y * _sigmoid(y)

    qf = conv_silu(q_ref[...].astype(F32), tq_ref, cwq_ref, cbq_ref) * (ML_HD ** -0.5)
    kf = conv_silu(k_ref[...].astype(F32), tk_ref, cwk_ref, cbk_ref)
    qb = qf.astype(BF16)
    kb = kf.astype(BF16)
    vb = v_ref[...]
    vf = vb.astype(F32)

    ic = if_ref[0, 0, 0:1, :] + bias_ref[0, h]
    fp = if_ref[0, 0, 1:2, :] + bias_ref[1, h]
    fc = jnp.minimum(fp, 0.0) - jnp.log(1.0 + jnp.exp(-jnp.abs(fp)))

    r = lax.broadcasted_iota(jnp.int32, (L, L), 0)
    cidx = lax.broadcasted_iota(jnp.int32, (L, L), 1)
    tril = cidx <= r
    eye = cidx == r
    fc_b = jnp.broadcast_to(fc, (L, L))
    ic_b = jnp.broadcast_to(ic, (L, L))
    b_col = jnp.sum(jnp.where(tril, fc_b, 0.0), axis=1, keepdims=True)
    fc_col = jnp.sum(jnp.where(eye, fc_b, 0.0), axis=1, keepdims=True)
    ic_col = jnp.sum(jnp.where(eye, ic_b, 0.0), axis=1, keepdims=True)
    b_row = jnp.sum(jnp.where(r <= cidx, jnp.broadcast_to(fc_col, (L, L)), 0.0),
                    axis=0, keepdims=True)
    b_last = b_col[L - 1:L, :]
    m_prev = m_ref[...]

    d_log = jnp.where(tril, b_col - b_row + ic, NEG_INF)
    inter = b_col + m_prev
    m_t = jnp.maximum(inter, jnp.max(d_log, axis=1, keepdims=True))
    w_intra = jnp.exp(d_log - m_t)
    w_inter = jnp.exp(inter - m_t)
    qk = lax.dot_general(qb, kb, NT_DIMS, preferred_element_type=F32) * w_intra
    num = (jnp.dot(qk.astype(BF16), vb, preferred_element_type=F32)
           + w_inter * lax.dot_general(qb, c_ref[...].astype(BF16), NT_DIMS, preferred_element_type=F32))
    den = jnp.sum(qk, axis=1, keepdims=True) + w_inter * jnp.sum(qf * n_ref[...], axis=1, keepdims=True)
    hm = num / jnp.maximum(jnp.abs(den), jnp.exp(-m_t))

    w_log = b_last - b_col + ic_col
    m_new = jnp.maximum(b_last + m_prev, jnp.max(w_log, axis=0, keepdims=True))
    w_state = jnp.exp(w_log - m_new)
    decay = jnp.exp(b_last + m_prev - m_new)
    c_ref[...] = decay * c_ref[...] + lax.dot_general((w_state * vf).astype(BF16), kb, TN_DIMS,
                                                      preferred_element_type=F32)
    n_ref[...] = decay * n_ref[...] + jnp.sum(w_state * kf, axis=0, keepdims=True)
    m_ref[...] = m_new

    hn = hm * lax.rsqrt(jnp.mean(hm * hm, axis=1, keepdims=True) + NORM_EPS) * ng_ref[...]
    o_ref[...] = (hn * _sigmoid(om_ref[...].astype(F32))).astype(BF16)


def _mlstm(proj, if_arr, gate_bias, conv_w, conv_b, ng, B, S):
    T = B * S
    nc = S // ML_CHUNK
    cq, ck, cv, co = COL_QKM // ML_HD, COL_QKM // ML_HD + ML_HEADS, COL_VM // ML_HD, COL_OM // ML_HD
    rows = lambda col0: pl.BlockSpec((ML_CHUNK, ML_HD), lambda b, h, c: (b * nc + c, col0 + h))
    return pl.pallas_call(
        _mlstm_kernel,
        grid=(B, ML_HEADS, nc),
        in_specs=[
            pl.BlockSpec(memory_space=pltpu.SMEM),
            rows(cq), rows(ck), rows(cv), rows(co),
            pl.BlockSpec((1, 1, 2, ML_CHUNK), lambda b, h, c: (b, h, 0, c)),
            pl.BlockSpec((CONV_W, ML_HD), lambda b, h, c: (0, h)),
            pl.BlockSpec((CONV_W, ML_HD), lambda b, h, c: (0, ML_HEADS + h)),
            pl.BlockSpec((1, ML_HD), lambda b, h, c: (0, h)),
            pl.BlockSpec((1, ML_HD), lambda b, h, c: (0, ML_HEADS + h)),
            pl.BlockSpec((1, ML_HD), lambda b, h, c: (0, h)),
        ],
        out_specs=pl.BlockSpec((ML_CHUNK, ML_HD), lambda b, h, c: (b * nc + c, h)),
        out_shape=jax.ShapeDtypeStruct((T, ML_HEADS * ML_HD), BF16),
        scratch_shapes=[
            pltpu.VMEM((ML_HD, ML_HD), F32), pltpu.VMEM((1, ML_HD), F32), pltpu.VMEM((1, 1), F32),
            pltpu.VMEM((8, ML_HD), F32), pltpu.VMEM((8, ML_HD), F32),
        ],
        compiler_params=_cparams(("parallel", "parallel", "arbitrary")),
        name="mlstm",
    )(gate_bias, proj, proj, proj, proj, if_arr, conv_w, conv_w, conv_b, conv_b, ng)


def _pack_bf16_pair(lo, hi):
    lo_b = pltpu.bitcast(lo.astype(BF16).astype(F32), U32)
    hi_b = pltpu.bitcast(hi.astype(BF16).astype(F32), U32)
    return (lo_b >> 16) | hi_b


def _unpack_bf16_pair(p):
    lo = pltpu.bitcast(p << 16, F32)
    hi = pltpu.bitcast(p & jnp.uint32(0xFFFF0000), F32)
    return lo, hi


def _row(ref, r):
    return ref.at[pl.ds(r, 1), :]


def _outproj_kernel(nsa_ref, ml_ref, w_ref, x_ref, g2_ref, wr_ref, br_ref, tril_ref,
                    x1_ref, xp_ref, rt_ref, cnt_ref, carry_ref):
    @pl.when(pl.program_id(0) == 0)
    def _():
        carry_ref[...] = jnp.zeros_like(carry_ref)

    half = D_MODEL // 2
    acc = jnp.dot(nsa_ref[...], w_ref[0:half, :], preferred_element_type=F32)
    acc = acc + jnp.dot(ml_ref[...], w_ref[half:, :], preferred_element_type=F32)
    x1 = x_ref[...] + acc
    x1_ref[...] = x1
    xn = x1 * lax.rsqrt(jnp.mean(x1 * x1, axis=-1, keepdims=True) + NORM_EPS) * g2_ref[...]
    xp_ref[...] = _pack_bf16_pair(xn[:, :half], xn[:, half:])
    logits = jnp.dot(xn.astype(BF16), wr_ref[...], preferred_element_type=F32) + br_ref[...]

    tm = logits.shape[0]
    lane = lax.broadcasted_iota(jnp.int32, (tm, LANE), 1)
    lane_f = lane.astype(F32)
    big = float(LANE)
    gmask = lane < MOE_GROUPS
    gmax = jnp.max(jnp.where(gmask, logits, NEG_INF), axis=1, keepdims=True)
    ge = jnp.where(gmask, jnp.exp(logits - gmax), 0.0)
    gp = ge / jnp.sum(ge, axis=1, keepdims=True)
    g_w = jnp.max(gp, axis=1, keepdims=True)
    g_idx = jnp.min(jnp.where(gmask, jnp.where(gp == g_w, lane_f, big), big), axis=1, keepdims=True)
    grp_of_lane = ((lane - MOE_GROUPS) >> EPG_LOG2).astype(F32)
    emask = jnp.where(lane >= MOE_GROUPS, grp_of_lane, -1.0) == g_idx
    emax = jnp.max(jnp.where(emask, logits, NEG_INF), axis=1, keepdims=True)
    ee = jnp.where(emask, jnp.exp(logits - emax), 0.0)
    ep = jnp.where(emask, ee / jnp.sum(ee, axis=1, keepdims=True), -1.0)
    v1 = jnp.max(ep, axis=1, keepdims=True)
    i1 = jnp.min(jnp.where(ep == v1, lane_f, big), axis=1, keepdims=True)
    ep2 = jnp.where(lane_f == i1, -1.0, ep)
    v2 = jnp.max(ep2, axis=1, keepdims=True)
    i2 = jnp.min(jnp.where(ep2 == v2, lane_f, big), axis=1, keepdims=True)
    w0 = g_w * v1 / (v1 + v2)
    w1 = g_w * v2 / (v1 + v2)
    e0 = i1 - float(MOE_GROUPS)
    e1 = i2 - float(MOE_GROUPS)

    oh0 = jnp.where(lane_f == e0, 1.0, 0.0)
    oh1 = jnp.where(lane_f == e1, 1.0, 0.0)
    pre0 = jnp.dot(tril_ref[...], oh0.astype(BF16), preferred_element_type=F32)
    pre1 = jnp.dot(tril_ref[...], oh1.astype(BF16), preferred_element_type=F32)
    carry = carry_ref[...]
    tot0 = pre0[tm - 1:tm, :]
    tot1 = pre1[tm - 1:tm, :]
    rank0 = jnp.sum(oh0 * (pre0 - 1.0 + carry), axis=1, keepdims=True)
    rank1 = jnp.sum(oh1 * (pre1 - 1.0 + carry + tot0), axis=1, keepdims=True)
    new_carry = carry + tot0 + tot1
    carry_ref[...] = new_carry
    cnt_ref[...] = jnp.broadcast_to(new_carry, cnt_ref.shape)

    rt = jnp.where(lane == 0, e0, jnp.where(lane == 1, e1, jnp.where(lane == 2, w0, jnp.where(
        lane == 3, w1, jnp.where(lane == 4, rank0, jnp.where(lane == 5, rank1, 0.0))))))
    rt_ref[...] = rt


def _out_proj(nsa_o, ml_o, w_out, x2, g2, w_r, b_r, tril):
    T = x2.shape[0]
    half = D_MODEL // 2
    return pl.pallas_call(
        _outproj_kernel,
        grid=(T // TM_OUT,),
        in_specs=[
            pl.BlockSpec((TM_OUT, half), lambda i: (i, 0)),
            pl.BlockSpec((TM_OUT, half), lambda i: (i, 0)),
            pl.BlockSpec((D_MODEL, D_MODEL), lambda i: (0, 0)),
            pl.BlockSpec((TM_OUT, D_MODEL), lambda i: (i, 0)),
            pl.BlockSpec((1, D_MODEL), lambda i: (0, 0)),
            pl.BlockSpec((D_MODEL, LANE), lambda i: (0, 0)),
            pl.BlockSpec((1, LANE), lambda i: (0, 0)),
            pl.BlockSpec((TM_OUT, TM_OUT), lambda i: (0, 0)),
        ],
        out_specs=[
            pl.BlockSpec((TM_OUT, D_MODEL), lambda i: (i, 0)),
            pl.BlockSpec((TM_OUT, half), lambda i: (i, 0)),
            pl.BlockSpec((TM_OUT, LANE), lambda i: (i, 0)),
            pl.BlockSpec((8, LANE), lambda i: (0, 0)),
        ],
        out_shape=[
            jax.ShapeDtypeStruct((T, D_MODEL), F32),
            jax.ShapeDtypeStruct((T, half), U32),
            jax.ShapeDtypeStruct((T, LANE), F32),
            jax.ShapeDtypeStruct((8, LANE), F32),
        ],
        scratch_shapes=[pltpu.VMEM((1, LANE), F32)],
        compiler_params=_cparams(("arbitrary",)),
        name="out_proj",
    )(nsa_o, ml_o, w_out, x2, g2, w_r, b_r, tril)


DISPATCH_CHUNK = 1024
DISPATCH_UNROLL = 8


def _dispatch_kernel(dest_ref, zblk_ref, meta_ref, xp_hbm, xs_hbm, zbuf_ref, zsem_ref, rsem_ref):
    n_assign = dest_ref.shape[0]
    nb = xs_hbm.shape[0] // BM
    n_used = meta_ref[0]
    zbuf_ref[...] = jnp.zeros_like(zbuf_ref)

    def zero_block(blk):
        return pltpu.make_async_copy(zbuf_ref, xs_hbm.at[pl.ds(blk * BM, BM), :], zsem_ref.at[0])

    def for_zero_blocks(fn):
        for e in range(N_EXPERTS):
            @pl.when(zblk_ref[e] >= 0)
            def _(e=e):
                fn(zero_block(zblk_ref[e]))

        def body(blk, _):
            fn(zero_block(blk))
            return 0
        lax.fori_loop(n_used, nb, body, 0)

    for_zero_blocks(lambda c: c.start())
    for_zero_blocks(lambda c: c.wait())

    def row_copy(a, u):
        return pltpu.make_async_copy(_row(xp_hbm, a >> 1), _row(xs_hbm, dest_ref[a]), rsem_ref.at[u % 2])

    def wait_chunk():
        for p in range(2):
            pltpu.make_async_copy(xp_hbm.at[pl.ds(0, DISPATCH_CHUNK // 2), :],
                                  xs_hbm.at[pl.ds(0, DISPATCH_CHUNK // 2), :], rsem_ref.at[p]).wait()

    def chunk(c, _):
        def body(j, _):
            for u in range(DISPATCH_UNROLL):
                row_copy(c * DISPATCH_CHUNK + j * DISPATCH_UNROLL + u, u).start(priority=u % 2)
            return 0
        lax.fori_loop(0, DISPATCH_CHUNK // DISPATCH_UNROLL, body, 0)

        @pl.when(c > 0)
        def _():
            wait_chunk()
        return 0

    lax.fori_loop(0, n_assign // DISPATCH_CHUNK, chunk, 0)
    wait_chunk()


def _dispatch(dest, zblk, meta, xp, n_rows):
    half = D_MODEL // 2
    assert dest.shape[0] % DISPATCH_CHUNK == 0
    grid_spec = pltpu.PrefetchScalarGridSpec(
        num_scalar_prefetch=3,
        grid=(1,),
        in_specs=[pl.BlockSpec(memory_space=pl.ANY)],
        out_specs=pl.BlockSpec(memory_space=pl.ANY),
        scratch_shapes=[
            pltpu.VMEM((BM, half), U32),
            pltpu.SemaphoreType.DMA((1,)),
            pltpu.SemaphoreType.DMA((2,)),
        ],
    )
    return pl.pallas_call(
        _dispatch_kernel,
        grid_spec=grid_spec,
        out_shape=jax.ShapeDtypeStruct((n_rows, half), U32),
        compiler_params=_cparams(("arbitrary",)),
        name="dispatch",
    )(dest, zblk, meta, xp)


def _expert_kernel(be_ref, nxt_ref, meta_ref, x_ref, wg_hbm, wu_hbm, wd_hbm, y_ref,
                   wsg_ref, wsu_ref, wsd_ref, wsem_ref, wgb_ref, wub_ref, wdb_ref):
    i = pl.program_id(0)
    n_used = meta_ref[0]
    half = D_MODEL // 2

    def weight_copies(e):
        return (pltpu.make_async_copy(wg_hbm.at[e], wsg_ref, wsem_ref.at[0]),
                pltpu.make_async_copy(wu_hbm.at[e], wsu_ref, wsem_ref.at[1]),
                pltpu.make_async_copy(wd_hbm.at[e], wsd_ref, wsem_ref.at[2]))

    @pl.when(i == 0)
    def _():
        for c in weight_copies(be_ref[0]):
            c.start()

    @pl.when(i < n_used)
    def _():
        e = be_ref[i]

        @pl.when((i == 0) | (e != be_ref[jnp.maximum(i - 1, 0)]))
        def _():
            for c in weight_copies(e):
                c.wait()
            wgb_ref[...] = wsg_ref[...].astype(BF16)
            wub_ref[...] = wsu_ref[...].astype(BF16)
            wdb_ref[...] = wsd_ref[...].astype(BF16)
            nx = nxt_ref[e]

            @pl.when(nx >= 0)
            def _():
                for c in weight_copies(nx):
                    c.start()

        lo, hi = _unpack_bf16_pair(x_ref[...])
        xl = lo.astype(BF16)
        xh = hi.astype(BF16)
        gt = (jnp.dot(xl, wgb_ref[0:half, :], preferred_element_type=F32)
              + jnp.dot(xh, wgb_ref[half:, :], preferred_element_type=F32))
        up = (jnp.dot(xl, wub_ref[0:half, :], preferred_element_type=F32)
              + jnp.dot(xh, wub_ref[half:, :], preferred_element_type=F32))
        hb = (gt * _sigmoid(gt) * up).astype(BF16)
        y = jnp.dot(hb, wdb_ref[...], preferred_element_type=F32)
        y_ref[...] = _pack_bf16_pair(y[:, :half], y[:, half:])

    @pl.when(i >= n_used)
    def _():
        y_ref[...] = jnp.zeros_like(y_ref)


def _experts(block_expert, next_expert, meta, xs, w_gate, w_up, w_down):
    half = D_MODEL // 2
    n_rows = xs.shape[0]
    nb = n_rows // BM
    grid_spec = pltpu.PrefetchScalarGridSpec(
        num_scalar_prefetch=3,
        grid=(nb,),
        in_specs=[
            pl.BlockSpec((BM, half), lambda i, be, nxt, meta: (jnp.minimum(i, meta[0] - 1), 0)),
            pl.BlockSpec(memory_space=pl.ANY), pl.BlockSpec(memory_space=pl.ANY),
            pl.BlockSpec(memory_space=pl.ANY),
        ],
        out_specs=pl.BlockSpec((BM, half), lambda i, be, nxt, meta: (i, 0)),
        scratch_shapes=[
            pltpu.VMEM((D_MODEL, D_EXPERT), F32),
            pltpu.VMEM((D_MODEL, D_EXPERT), F32),
            pltpu.VMEM((D_EXPERT, D_MODEL), F32),
            pltpu.SemaphoreType.DMA((3,)),
            pltpu.VMEM((D_MODEL, D_EXPERT), BF16),
            pltpu.VMEM((D_MODEL, D_EXPERT), BF16),
            pltpu.VMEM((D_EXPERT, D_MODEL), BF16),
        ],
    )
    return pl.pallas_call(
        _expert_kernel,
        grid_spec=grid_spec,
        out_shape=jax.ShapeDtypeStruct((n_rows, half), U32),
        compiler_params=_cparams(("arbitrary",)),
        name="experts",
    )(block_expert, next_expert, meta, xs, w_gate, w_up, w_down)


def _combine_kernel(dest_ref, y_hbm, x1_ref, rt_ref, fg_ref, o_ref, ya_ref, yb_ref, sem_ref):
    i = pl.program_id(0)
    nt = pl.num_programs(0)
    half = D_MODEL // 2
    bufs = (ya_ref, yb_ref)

    def row_copy(tile, slot, r_tile, s, k):
        r = r_tile * 8 + s
        d = dest_ref[(tile * TC + r) * 2 + k]
        return pltpu.make_async_copy(_row(y_hbm, d), _row(bufs[slot].at[k], r), sem_ref.at[slot])

    def wait_rows(slot):
        pltpu.make_async_copy(bufs[slot], bufs[slot], sem_ref.at[slot]).wait()

    def step(slot):
        wait_rows(slot)
        for r in range(TC):
            for k in range(2):
                row_copy(i + 1, 1 - slot, r // 8, r % 8, k).start(priority=k)
        rt = rt_ref[...]
        w0 = rt[:, 2:3]
        w1 = rt[:, 3:4]
        lo0, hi0 = _unpack_bf16_pair(bufs[slot][0])
        lo1, hi1 = _unpack_bf16_pair(bufs[slot][1])
        xl = x1_ref[:, :half] + (w0 * lo0 + w1 * lo1)
        xh = x1_ref[:, half:] + (w0 * hi0 + w1 * hi1)
        ms = (jnp.sum(xl * xl, axis=1, keepdims=True) + jnp.sum(xh * xh, axis=1, keepdims=True)) / float(D_MODEL)
        inv = lax.rsqrt(ms + NORM_EPS)
        o_ref[:, :half] = xl * inv * fg_ref[:, :half]
        o_ref[:, half:] = xh * inv * fg_ref[:, half:]

        @pl.when(i == nt - 1)
        def _():
            wait_rows(1 - slot)

    @pl.when(i == 0)
    def _():
        def body(rt, _):
            for s in range(8):
                for k in range(2):
                    row_copy(0, 0, rt, s, k).start(priority=k)
            return 0
        lax.fori_loop(0, TC // 8, body, 0)

    @pl.when(i % 2 == 0)
    def _():
        step(0)

    @pl.when(i % 2 == 1)
    def _():
        step(1)


def _combine(dest, ys, x1, route, fg):
    T = x1.shape[0]
    half = D_MODEL // 2
    grid_spec = pltpu.PrefetchScalarGridSpec(
        num_scalar_prefetch=1,
        grid=(T // TC,),
        in_specs=[
            pl.BlockSpec(memory_space=pl.ANY),
            pl.BlockSpec((TC, D_MODEL), lambda i, d: (i, 0)),
            pl.BlockSpec((TC, LANE), lambda i, d: (i, 0)),
            pl.BlockSpec((1, D_MODEL), lambda i, d: (0, 0)),
        ],
        out_specs=pl.BlockSpec((TC, D_MODEL), lambda i, d: (i, 0)),
        scratch_shapes=[
            pltpu.VMEM((2, TC, half), U32),
            pltpu.VMEM((2, TC, half), U32),
            pltpu.SemaphoreType.DMA((2,)),
        ],
    )
    return pl.pallas_call(
        _combine_kernel,
        grid_spec=grid_spec,
        out_shape=jax.ShapeDtypeStruct((T, D_MODEL), F32),
        compiler_params=_cparams(("arbitrary",)),
        name="combine",
    )(dest, ys, x1, route, fg)


def _cover_matrix(S):
    n_blk = S // CMP_STRIDE
    n_sel = S // SEL_BLOCK
    cs = np.arange(n_blk) * CMP_STRIDE
    ss = np.arange(n_sel) * SEL_BLOCK
    shared = np.minimum(cs[:, None] + CMP_LEN, ss[None, :] + SEL_BLOCK) - np.maximum(cs[:, None], ss[None, :])
    return (np.clip(shared, 0, None) / CMP_LEN).T.astype(np.float32)


def _block_mask_matrix(S):
    n_sel = S // SEL_BLOCK
    assert n_sel <= LANE
    e = np.zeros((S, LANE), np.float32)
    e[np.arange(S), np.arange(S) // SEL_BLOCK] = NEG_INF
    return e


def _inv_freq_row():
    inv = np.power(np.float32(ROPE_THETA), -np.arange(ROPE_HALF, dtype=np.float32) * 2.0 / ROPE_DIM)
    row = np.zeros((1, LANE), np.float32)
    row[0, :ROPE_HALF] = inv
    row[0, ROPE_HALF:ROPE_DIM] = inv
    return row


def _layer(x2, positions, B, S, norm1_g, w_in, cmp_pe_k, cmp_pe_v, cmp_wk1, cmp_wk2, cmp_wv1, cmp_wv2,
           nsa_norm_g, conv_w, conv_b, b_igate, b_fgate, mlstm_norm_g, w_out, norm2_g,
           w_group, b_group, w_router, b_router, w_exp_gate, w_exp_up, w_exp_down, out_norm_g):
    T = B * S
    o_q, o_kv, o_g, o_qk, o_v, o_o, o_i, o_f = 0, 1024, 2560, 2584, 4632, 5656, 6680, 6684
    w_main = jnp.concatenate([w_in[:, o_q:o_g], w_in[:, o_qk:o_i]], axis=1).astype(BF16)
    w_small = jnp.concatenate([w_in[:, o_g:o_qk], w_in[:, o_i:], jnp.zeros((D_MODEL, LANE - 32), F32)],
                              axis=1).astype(BF16)
    posb = jnp.broadcast_to(positions.reshape(T, 1).astype(F32), (T, LANE))
    invf = jnp.asarray(_inv_freq_row())

    proj, small = _in_proj(x2, norm1_g.reshape(1, -1), w_main, w_small)
    q_r, k_r = _rope(posb, invf, proj)
    kc, vc = _compress(k_r, proj, B, S, cmp_pe_k, cmp_pe_v, cmp_wk1.astype(BF16), cmp_wk2.astype(BF16),
                       cmp_wv1.astype(BF16), cmp_wv2.astype(BF16))
    eye = np.eye(S // SEL_BLOCK, LANE, dtype=np.float32)
    nsa_o = _nsa(q_r, k_r, proj, small, kc, vc, jnp.asarray(_cover_matrix(S), BF16),
                 jnp.asarray(_block_mask_matrix(S), BF16), jnp.asarray(eye, BF16),
                 nsa_norm_g.reshape(1, -1), B, S)

    if_arr = small[:, 24:32].reshape(B, S, 2, ML_HEADS).transpose(0, 3, 2, 1)
    gate_bias = jnp.stack([b_igate, b_fgate]).astype(F32)
    ml_o = _mlstm(proj, if_arr, gate_bias, conv_w, conv_b.reshape(1, -1), mlstm_norm_g.reshape(1, -1), B, S)

    w_r = jnp.concatenate([w_group, w_router, jnp.zeros((D_MODEL, LANE - MOE_GROUPS - N_EXPERTS), F32)],
                          axis=1).astype(BF16)
    b_r = jnp.concatenate([b_group, b_router, jnp.zeros((LANE - MOE_GROUPS - N_EXPERTS,), F32)]).reshape(1, LANE)
    tril = jnp.asarray(np.tril(np.ones((TM_OUT, TM_OUT), np.float32)), BF16)
    x1, xp, route, cnt = _out_proj(nsa_o, ml_o, w_out.astype(BF16), x2, norm2_g.reshape(1, -1), w_r, b_r, tril)

    n_rows = T * 2 + N_EXPERTS * BM
    counts = cnt[0, :N_EXPERTS].astype(jnp.int32)
    pcounts = (counts + BM - 1) // BM * BM
    pends = jnp.cumsum(pcounts)
    pstarts = pends - pcounts
    eid = route[:, 0:2].astype(jnp.int32)
    dest = (pstarts[eid] + route[:, 4:6].astype(jnp.int32)).reshape(T * 2)
    block_expert = jnp.minimum(jnp.searchsorted(pends, jnp.arange(n_rows // BM) * BM, side='right'),
                               N_EXPERTS - 1).astype(jnp.int32)
    present = jnp.where(counts > 0, jnp.arange(N_EXPERTS, dtype=jnp.int32), N_EXPERTS)
    later = jnp.concatenate([lax.cummin(present[::-1])[::-1][1:], jnp.full((1,), N_EXPERTS, jnp.int32)])
    next_expert = jnp.where(later < N_EXPERTS, later, -1).astype(jnp.int32)
    meta = (pends[-1:] // BM).astype(jnp.int32)

    last_block = jnp.where(counts > 0, pends // BM - 1, -1).astype(jnp.int32)

    xs = _dispatch(dest, last_block, meta, xp, n_rows)
    ys = _experts(block_expert, next_expert, meta, xs, w_exp_gate, w_exp_up, w_exp_down)
    dest_pad = jnp.concatenate([dest, jnp.zeros((2 * TC,), jnp.int32)])
    return _combine(dest_pad, ys, x1, route, out_norm_g.reshape(1, -1))


def kernel(x, positions, norm1_g, w_in, cmp_pe_k, cmp_pe_v, cmp_wk1, cmp_wk2, cmp_wv1, cmp_wv2, nsa_norm_g,
           conv_w, conv_b, b_igate, b_fgate, mlstm_norm_g, w_out, norm2_g, w_group, b_group, w_router,
           b_router, w_exp_gate, w_exp_up, w_exp_down, final_norm_g):
    B, S, D = x.shape
    assert D == D_MODEL and norm1_g.shape[0] == 1, "single-layer, D_MODEL-wide configuration only"
    assert S % ML_CHUNK == 0 and S % SEL_BUCKET == 0 and S >= WIN_KEYS and (B * S) % TM_IN == 0
    out = _layer(x.reshape(B * S, D), positions, B, S, norm1_g[0], w_in[0], cmp_pe_k[0], cmp_pe_v[0],
                 cmp_wk1[0], cmp_wk2[0], cmp_wv1[0], cmp_wv2[0], nsa_norm_g[0], conv_w[0], conv_b[0],
                 b_igate[0], b_fgate[0], mlstm_norm_g[0], w_out[0], norm2_g[0], w_group[0], b_group[0],
                 w_router[0], b_router[0], w_exp_gate[0], w_exp_up[0], w_exp_down[0], final_norm_g)
    return out.reshape(B, S, D)
```

```python
import functools

import numpy as np
import jax
import jax.numpy as jnp
from jax import lax
from jax.experimental import pallas as pl
from jax.experimental.pallas import tpu as pltpu

F32 = jnp.float32
BF16 = jnp.bfloat16
U32 = jnp.uint32

D_MODEL = 2048
NSA_HEADS = 8
NSA_GROUPS = 2
NSA_J = NSA_HEADS // NSA_GROUPS
HD = 128
CMP_LEN = 32
CMP_STRIDE = 16
SEL_BLOCK = 64
SEL_TOPK = 8
WINDOW = 512
ROPE_THETA = 500000.0
ROPE_DIM = 32
ROPE_HALF = 16
ML_HEADS = 4
ML_HD = 256
CONV_W = 4
MOE_GROUPS = 8
EPG = 8
N_EXPERTS = 64
D_EXPERT = 512
SEL_BLOCK_LOG2 = 6
EPG_LOG2 = 3
assert (1 << SEL_BLOCK_LOG2) == SEL_BLOCK and (1 << EPG_LOG2) == EPG
NORM_EPS = 1e-6
NEG_INF = -1e30
FORCE_SCORE = 1000.0

COL_Q = 0
COL_KV = 1024
COL_QKM = 2560
COL_VM = 4608
COL_OM = 5632
N_MAIN = 6656

LANE = 128
VMEM_LIMIT = 56 * 1024 * 1024

TM_IN = 1024
TN_IN = 512
TS_ROPE = 256
TQ = 128
SEL_BUCKET = 512
WIN_KEYS = WINDOW + TQ
ML_CHUNK = 256
TM_OUT = 512
BM = 256
TC = 256

NT_DIMS = (((1,), (1,)), ((), ()))
TN_DIMS = (((0,), (0,)), ((), ()))


def _cparams(sem):
    return pltpu.CompilerParams(dimension_semantics=sem, vmem_limit_bytes=VMEM_LIMIT)


def _sigmoid(x):
    return 1.0 / (1.0 + jnp.exp(-x))


def _inproj_kernel(x_ref, g_ref, w_ref, ws_ref, o_ref, os_ref, h_ref):
    @pl.when(pl.program_id(1) == 0)
    def _():
        x = x_ref[...]
        ms = jnp.mean(x * x, axis=-1, keepdims=True)
        h_ref[...] = (x * lax.rsqrt(ms + NORM_EPS) * g_ref[...]).astype(BF16)
        os_ref[...] = jnp.dot(h_ref[...], ws_ref[...], preferred_element_type=F32)

    o_ref[...] = jnp.dot(h_ref[...], w_ref[...], preferred_element_type=F32).astype(o_ref.dtype)


def _in_proj(x2, g1, w_main, w_small):
    T = x2.shape[0]
    return pl.pallas_call(
        _inproj_kernel,
        grid=(T // TM_IN, N_MAIN // TN_IN),
        in_specs=[
            pl.BlockSpec((TM_IN, D_MODEL), lambda m, n: (m, 0)),
            pl.BlockSpec((1, D_MODEL), lambda m, n: (0, 0)),
            pl.BlockSpec((D_MODEL, TN_IN), lambda m, n: (0, n)),
            pl.BlockSpec((D_MODEL, LANE), lambda m, n: (0, 0)),
        ],
        out_specs=[
            pl.BlockSpec((TM_IN, TN_IN), lambda m, n: (m, n)),
            pl.BlockSpec((TM_IN, LANE), lambda m, n: (m, 0)),
        ],
        out_shape=[
            jax.ShapeDtypeStruct((T, N_MAIN), BF16),
            jax.ShapeDtypeStruct((T, LANE), F32),
        ],
        scratch_shapes=[pltpu.VMEM((TM_IN, D_MODEL), BF16)],
        compiler_params=_cparams(("parallel", "arbitrary")),
        name="in_proj",
    )(x2, g1, w_main, w_small)


def _rope_kernel(pos_ref, invf_ref, q_ref, kc_ref, ks_ref, kw_ref, qo_ref, ko_ref):
    ang = pos_ref[...] * invf_ref[...]
    c = jnp.cos(ang)
    s = jnp.sin(ang)
    lane = lax.broadcasted_iota(jnp.int32, ang.shape, 1)
    sa = jnp.where(lane < ROPE_HALF, -s, 0.0)
    sb = jnp.where(lane < ROPE_HALF, 0.0, s)

    def rope(x):
        return x * c + pltpu.roll(x, LANE - ROPE_HALF, 1) * sa + pltpu.roll(x, ROPE_HALF, 1) * sb

    scale = HD ** -0.5
    for h in range(NSA_HEADS):
        sl = slice(h * HD, (h + 1) * HD)
        qo_ref[:, sl] = (rope(q_ref[:, sl].astype(F32)) * scale).astype(BF16)
    for i, r in enumerate((kc_ref, ks_ref, kw_ref)):
        for g in range(NSA_GROUPS):
            sl = slice(g * HD, (g + 1) * HD)
            so = slice(i * 2 * HD + g * HD, i * 2 * HD + (g + 1) * HD)
            ko_ref[:, so] = rope(r[:, sl].astype(F32)).astype(BF16)


def _rope(posb, invf, proj):
    T = proj.shape[0]
    kvb = COL_KV // 256
    return pl.pallas_call(
        _rope_kernel,
        grid=(T // TS_ROPE,),
        in_specs=[
            pl.BlockSpec((TS_ROPE, LANE), lambda i: (i, 0)),
            pl.BlockSpec((1, LANE), lambda i: (0, 0)),
            pl.BlockSpec((TS_ROPE, 1024), lambda i: (i, 0)),
            pl.BlockSpec((TS_ROPE, 256), lambda i: (i, kvb + 0)),
            pl.BlockSpec((TS_ROPE, 256), lambda i: (i, kvb + 2)),
            pl.BlockSpec((TS_ROPE, 256), lambda i: (i, kvb + 4)),
        ],
        out_specs=[
            pl.BlockSpec((TS_ROPE, 1024), lambda i: (i, 0)),
            pl.BlockSpec((TS_ROPE, 768), lambda i: (i, 0)),
        ],
        out_shape=[
            jax.ShapeDtypeStruct((T, 1024), BF16),
            jax.ShapeDtypeStruct((T, 768), BF16),
        ],
        compiler_params=_cparams(("parallel",)),
        name="rope",
    )(posb, invf, proj, proj, proj, proj)


def _compress_kernel(k_ref, v_ref, pek_ref, pev_ref, w1k_ref, w2k_ref, w1v_ref, w2v_ref,
                     kc_ref, vc_ref, xs_ref):
    S = k_ref.shape[0]
    n_blk = S // CMP_STRIDE
    for src, pe, w1, w2, dst in ((k_ref, pek_ref, w1k_ref, w2k_ref, kc_ref),
                                 (v_ref, pev_ref, w1v_ref, w2v_ref, vc_ref)):
        xs_ref[0:S, :] = src[...].astype(F32)
        xs_ref[S:S + CMP_LEN, :] = jnp.zeros((CMP_LEN, HD), F32)
        acc = jnp.zeros((n_blk, HD), F32)
        for l in range(CMP_LEN):
            a = xs_ref[pl.ds(l, n_blk, stride=CMP_STRIDE), :] + pe[l:l + 1, :]
            acc = acc + jnp.dot(a.astype(BF16), w1[l * HD:(l + 1) * HD, :],
                                preferred_element_type=F32)
        hid = acc * _sigmoid(acc)
        out = jnp.dot(hid.astype(BF16), w2[...], preferred_element_type=F32)
        dst[0, 0] = out.astype(BF16)


def _compress(k_r, proj, B, S, pek, pev, w1k, w2k, w1v, w2v):
    n_blk = S // CMP_STRIDE
    vcol = (COL_KV + 256) // HD
    full = lambda shape: pl.BlockSpec(shape, lambda b, g: tuple(0 for _ in shape))
    return pl.pallas_call(
        _compress_kernel,
        grid=(B, NSA_GROUPS),
        in_specs=[
            pl.BlockSpec((S, HD), lambda b, g: (b, g)),
            pl.BlockSpec((S, HD), lambda b, g: (b, vcol + g)),
            full((CMP_LEN, HD)), full((CMP_LEN, HD)),
            full((CMP_LEN * HD, HD)), full((HD, HD)),
            full((CMP_LEN * HD, HD)), full((HD, HD)),
        ],
        out_specs=[
            pl.BlockSpec((1, 1, n_blk, HD), lambda b, g: (b, g, 0, 0)),
            pl.BlockSpec((1, 1, n_blk, HD), lambda b, g: (b, g, 0, 0)),
        ],
        out_shape=[
            jax.ShapeDtypeStruct((B, NSA_GROUPS, n_blk, HD), BF16),
            jax.ShapeDtypeStruct((B, NSA_GROUPS, n_blk, HD), BF16),
        ],
        scratch_shapes=[pltpu.VMEM((S + CMP_LEN, HD), F32)],
        compiler_params=_cparams(("parallel", "parallel")),
        name="compress",
    )(k_r, proj, pek, pev, w1k, w2k, w1v, w2v)


def _nsa_kernel(q_ref, kc_ref, vc_ref, ks_ref, vs_ref, kw_ref, vw_ref, sm_ref, covt_ref, eneg_ref, eye_ref,
                ng_ref, o_ref, obuf_ref, owin_ref, osel_ref):
    qi = pl.program_id(1)
    q0 = qi * TQ
    R = NSA_J * TQ
    n_cmp = kc_ref.shape[2]
    n_sel = ks_ref.shape[0] // SEL_BLOCK
    n_win = WIN_KEYS // TQ

    def row_t(shape):
        r = lax.broadcasted_iota(jnp.int32, shape, 0)
        return q0 + (r & (TQ - 1))

    qgs = [jnp.concatenate([q_ref[:, (g * NSA_J + j) * HD:(g * NSA_J + j + 1) * HD]
                            for j in range(NSA_J)], axis=0) for g in range(NSA_GROUPS)]

    def with_ones(v):
        lane = lax.broadcasted_iota(jnp.int32, v.shape, 1)
        return jnp.concatenate([v, jnp.where(lane == 0, 1.0, 0.0).astype(BF16)], axis=1)

    def window(masks):
        w0 = pl.multiple_of(jnp.maximum(q0 - WINDOW, 0), TQ)
        for g in range(NSA_GROUPS):
            kt = kw_ref[pl.ds(w0, WIN_KEYS), g * HD:(g + 1) * HD]
            vt = with_ones(vw_ref[pl.ds(w0, WIN_KEYS), g * HD:(g + 1) * HD])
            sc = masks(lax.dot_general(qgs[g], kt, NT_DIMS, preferred_element_type=F32), w0)
            mw = jnp.max(sc, axis=1, keepdims=True)
            pw = jnp.exp((sc - mw).astype(BF16))
            acc = jnp.dot(pw, vt, preferred_element_type=F32)
            owin_ref[g] = acc[:, 0:HD] / acc[:, HD:HD + 1]

    def band_masks(sc, w0):
        d = (lax.broadcasted_iota(jnp.int32, (R, TQ), 1)
             - (lax.broadcasted_iota(jnp.int32, (R, TQ), 0) & (TQ - 1)))
        first = jnp.where(d > 0, sc[:, 0:TQ], NEG_INF)
        last = jnp.where(d <= 0, sc[:, (n_win - 1) * TQ:], NEG_INF)
        return jnp.concatenate([first, sc[:, TQ:(n_win - 1) * TQ], last], axis=1)

    def general_masks(sc, w0):
        diff = row_t((R, WIN_KEYS)) - (w0 + lax.broadcasted_iota(jnp.int32, (R, WIN_KEYS), 1))
        keep = jnp.where(diff >= 0, diff, WINDOW) < WINDOW
        return jnp.where(keep, sc, NEG_INF)

    @pl.when(q0 >= WINDOW)
    def _():
        window(band_masks)

    @pl.when(q0 < WINDOW)
    def _():
        window(general_masks)

    gates = _sigmoid(sm_ref[...])
    o_cmps = []
    qps = []
    for g in range(NSA_GROUPS):
        qg = qgs[g]

        s = lax.dot_general(qg, kc_ref[0, g], NT_DIMS, preferred_element_type=F32)
        n_lane = lax.broadcasted_iota(jnp.int32, (R, n_cmp), 1)
        cmask = (n_lane * CMP_STRIDE + (CMP_LEN - 1)) <= row_t((R, n_cmp))
        s = jnp.where(cmask, s, NEG_INF)
        m = jnp.max(s, axis=1, keepdims=True)
        e = jnp.where(cmask, jnp.exp(s - m), 0.0)
        l = jnp.sum(e, axis=1, keepdims=True)
        p = (e / jnp.where(l > 0.0, l, 1.0)).astype(BF16)
        o_cmps.append(jnp.dot(p, vc_ref[0, g], preferred_element_type=F32))
        impr = lax.dot_general(covt_ref[...], p, NT_DIMS, preferred_element_type=F32)
        imp = impr[:, 0:TQ]
        for j in range(1, NSA_J):
            imp = imp + impr[:, j * TQ:(j + 1) * TQ]

        m_sub = lax.broadcasted_iota(jnp.int32, (n_sel, TQ), 0)
        jt = (q0 + lax.broadcasted_iota(jnp.int32, (n_sel, TQ), 1)) >> SEL_BLOCK_LOG2
        forced = jnp.where(m_sub == 0, FORCE_SCORE,
                           jnp.where(m_sub == jt, FORCE_SCORE,
                                     jnp.where(m_sub == jt - 1, FORCE_SCORE, 0.0)))
        score = jnp.where(m_sub <= jt, imp + forced, -1.0)
        rank = jnp.zeros((n_sel, TQ), F32)
        for mp in range(n_sel):
            row = score[mp:mp + 1, :]
            ge = jnp.where(row >= score, 1.0, 0.0)
            gt = jnp.where(row > score, 1.0, 0.0)
            rank = rank + jnp.where(m_sub > mp, ge, gt)
        notsel_t = jnp.where(rank < float(min(SEL_TOPK, n_sel)), 0.0, 1.0).astype(BF16)
        notsel = lax.dot_general(notsel_t, eye_ref[...], TN_DIMS,
                                 preferred_element_type=F32).astype(BF16)
        qps.append(jnp.concatenate([qg, jnp.concatenate([notsel] * NSA_J, axis=0)], axis=1))

    def sel_oneshot(nk):
        for g in range(NSA_GROUPS):
            kt = jnp.concatenate([ks_ref[0:nk, g * HD:(g + 1) * HD], eneg_ref[0:nk, :]], axis=1)
            vt = with_ones(vs_ref[0:nk, g * HD:(g + 1) * HD])
            sc = lax.dot_general(qps[g], kt, NT_DIMS, preferred_element_type=F32)
            klane = lax.broadcasted_iota(jnp.int32, (R, SEL_BUCKET), 1) + (nk - SEL_BUCKET)
            tail = jnp.where(klane <= row_t((R, SEL_BUCKET)), sc[:, nk - SEL_BUCKET:], NEG_INF)
            sc = tail if nk == SEL_BUCKET else jnp.concatenate([sc[:, 0:nk - SEL_BUCKET], tail], axis=1)
            ms = jnp.max(sc, axis=1, keepdims=True)
            acc = jnp.dot(jnp.exp((sc - ms).astype(BF16)), vt, preferred_element_type=F32)
            osel_ref[g] = acc[:, 0:HD] / acc[:, HD:HD + 1]

    for b in range(ks_ref.shape[0] // SEL_BUCKET):
        @pl.when(q0 // SEL_BUCKET == b)
        def _(b=b):
            sel_oneshot((b + 1) * SEL_BUCKET)

    ssq = jnp.zeros((TQ, 1), F32)
    for g in range(NSA_GROUPS):
        o_sel = osel_ref[g]
        o_cmp = o_cmps[g]
        o_win = owin_ref[g]
        for j in range(NSA_J):
            h = g * NSA_J + j
            rs = slice(j * TQ, (j + 1) * TQ)
            o = (gates[:, 3 * h:3 * h + 1] * o_cmp[rs] + gates[:, 3 * h + 1:3 * h + 2] * o_sel[rs]
                 + gates[:, 3 * h + 2:3 * h + 3] * o_win[rs])
            ssq = ssq + jnp.sum(o * o, axis=1, keepdims=True)
            obuf_ref[:, h * HD:(h + 1) * HD] = o

    inv = lax.rsqrt(ssq / float(NSA_HEADS * HD) + NORM_EPS)
    o_ref[...] = (obuf_ref[...] * inv * ng_ref[...]).astype(BF16)


def _nsa(q_r, k_r, proj, small, kc, vc, covt, eneg, eye, ng, B, S):
    T = B * S
    nq = S // TQ
    n_blk = S // CMP_STRIDE
    n_sel = S // SEL_BLOCK
    kvb = COL_KV // 256
    return pl.pallas_call(
        _nsa_kernel,
        grid=(B, nq),
        in_specs=[
            pl.BlockSpec((TQ, 1024), lambda b, i: (b * nq + i, 0)),
            pl.BlockSpec((1, NSA_GROUPS, n_blk, HD), lambda b, i: (b, 0, 0, 0)),
            pl.BlockSpec((1, NSA_GROUPS, n_blk, HD), lambda b, i: (b, 0, 0, 0)),
            pl.BlockSpec((S, 256), lambda b, i: (b, 1)),
            pl.BlockSpec((S, 256), lambda b, i: (b, kvb + 3)),
            pl.BlockSpec((S, 256), lambda b, i: (b, 2)),
            pl.BlockSpec((S, 256), lambda b, i: (b, kvb + 5)),
            pl.BlockSpec((TQ, LANE), lambda b, i: (b * nq + i, 0)),
            pl.BlockSpec((n_sel, n_blk), lambda b, i: (0, 0)),
            pl.BlockSpec((S, LANE), lambda b, i: (0, 0)),
            pl.BlockSpec((n_sel, LANE), lambda b, i: (0, 0)),
            pl.BlockSpec((1, 1024), lambda b, i: (0, 0)),
        ],
        out_specs=pl.BlockSpec((TQ, 1024), lambda b, i: (b * nq + i, 0)),
        out_shape=jax.ShapeDtypeStruct((T, 1024), BF16),
        scratch_shapes=[pltpu.VMEM((TQ, 1024), F32), pltpu.VMEM((NSA_GROUPS, NSA_J * TQ, HD), F32),
                        pltpu.VMEM((NSA_GROUPS, NSA_J * TQ, HD), F32)],
        compiler_params=_cparams(("parallel", "parallel")),
        name="nsa",
    )(q_r, kc, vc, k_r, proj, k_r, proj, small, covt, eneg, eye, ng)


def _mlstm_kernel(bias_ref, q_ref, k_ref, v_ref, om_ref, if_ref, cwq_ref, cwk_ref, cbq_ref, cbk_ref,
                  ng_ref, o_ref, c_ref, n_ref, m_ref, tq_ref, tk_ref):
    h = pl.program_id(1)
    L = ML_CHUNK

    @pl.when(pl.program_id(2) == 0)
    def _():
        c_ref[...] = jnp.zeros_like(c_ref)
        n_ref[...] = jnp.zeros_like(n_ref)
        m_ref[...] = jnp.zeros_like(m_ref)
        tq_ref[...] = jnp.zeros_like(tq_ref)
        tk_ref[...] = jnp.zeros_like(tk_ref)

    row8 = lax.broadcasted_iota(jnp.int32, (8, ML_HD), 0)

    def conv_silu(x, tail_ref, w_ref, b_ref):
        tail = tail_ref[...]
        y = x * w_ref[CONV_W - 1:CONV_W, :] + b_ref[...]
        for k in range(1, CONV_W):
            xs = pltpu.roll(x, k, 0)
            fix = jnp.where(row8 < k, pltpu.roll(tail, k, 0), xs[0:8])
            xs = jnp.concatenate([fix, xs[8:]], axis=0)
            y = y + xs * w_ref[CONV_W - 1 - k:CONV_W - k, :]
        tail_ref[...] = x[L - 8:L]
        return y * _sigmoid(y)

    qf = conv_silu(q_ref[...].astype(F32), tq_ref, cwq_ref, cbq_ref) * (ML_HD ** -0.5)
    kf = conv_silu(k_ref[...].astype(F32), tk_ref, cwk_ref, cbk_ref)
    qb = qf.astype(BF16)
    kb = kf.astype(BF16)
    vb = v_ref[...]
    vf = vb.astype(F32)

    ic = if_ref[0, 0, 0:1, :] + bias_ref[0, h]
    fp = if_ref[0, 0, 1:2, :] + bias_ref[1, h]
    fc = jnp.minimum(fp, 0.0) - jnp.log(1.0 + jnp.exp(-jnp.abs(fp)))

    r = lax.broadcasted_iota(jnp.int32, (L, L), 0)
    cidx = lax.broadcasted_iota(jnp.int32, (L, L), 1)
    tril = cidx <= r
    eye = cidx == r
    fc_b = jnp.broadcast_to(fc, (L, L))
    ic_b = jnp.broadcast_to(ic, (L, L))
    b_col = jnp.sum(jnp.where(tril, fc_b, 0.0), axis=1, keepdims=True)
    fc_col = jnp.sum(jnp.where(eye, fc_b, 0.0), axis=1, keepdims=True)
    ic_col = jnp.sum(jnp.where(eye, ic_b, 0.0), axis=1, keepdims=True)
    b_row = jnp.sum(jnp.where(r <= cidx, jnp.broadcast_to(fc_col, (L, L)), 0.0),
                    axis=0, keepdims=True)
    b_last = b_col[L - 1:L, :]
    m_prev = m_ref[...]

    d_log = jnp.where(tril, b_col - b_row + ic, NEG_INF)
    inter = b_col + m_prev
    m_t = jnp.maximum(inter, jnp.max(d_log, axis=1, keepdims=True))
    w_intra = jnp.exp(d_log - m_t)
    w_inter = jnp.exp(inter - m_t)
    qk = lax.dot_general(qb, kb, NT_DIMS, preferred_element_type=F32) * w_intra
    num = (jnp.dot(qk.astype(BF16), vb, preferred_element_type=F32)
           + w_inter * lax.dot_general(qb, c_ref[...].astype(BF16), NT_DIMS, preferred_element_type=F32))
    den = jnp.sum(qk, axis=1, keepdims=True) + w_inter * jnp.sum(qf * n_ref[...], axis=1, keepdims=True)
    hm = num / jnp.maximum(jnp.abs(den), jnp.exp(-m_t))

    w_log = b_last - b_col + ic_col
    m_new = jnp.maximum(b_last + m_prev, jnp.max(w_log, axis=0, keepdims=True))
    w_state = jnp.exp(w_log - m_new)
    decay = jnp.exp(b_last + m_prev - m_new)
    c_ref[...] = decay * c_ref[...] + lax.dot_general((w_state * vf).astype(BF16), kb, TN_DIMS,
                                                      preferred_element_type=F32)
    n_ref[...] = decay * n_ref[...] + jnp.sum(w_state * kf, axis=0, keepdims=True)
    m_ref[...] = m_new

    hn = hm * lax.rsqrt(jnp.mean(hm * hm, axis=1, keepdims=True) + NORM_EPS) * ng_ref[...]
    o_ref[...] = (hn * _sigmoid(om_ref[...].astype(F32))).astype(BF16)


def _mlstm(proj, if_arr, gate_bias, conv_w, conv_b, ng, B, S):
    T = B * S
    nc = S // ML_CHUNK
    cq, ck, cv, co = COL_QKM // ML_HD, COL_QKM // ML_HD + ML_HEADS, COL_VM // ML_HD, COL_OM // ML_HD
    rows = lambda col0: pl.BlockSpec((ML_CHUNK, ML_HD), lambda b, h, c: (b * nc + c, col0 + h))
    return pl.pallas_call(
        _mlstm_kernel,
        grid=(B, ML_HEADS, nc),
        in_specs=[
            pl.BlockSpec(memory_space=pltpu.SMEM),
            rows(cq), rows(ck), rows(cv), rows(co),
            pl.BlockSpec((1, 1, 2, ML_CHUNK), lambda b, h, c: (b, h, 0, c)),
            pl.BlockSpec((CONV_W, ML_HD), lambda b, h, c: (0, h)),
            pl.BlockSpec((CONV_W, ML_HD), lambda b, h, c: (0, ML_HEADS + h)),
            pl.BlockSpec((1, ML_HD), lambda b, h, c: (0, h)),
            pl.BlockSpec((1, ML_HD), lambda b, h, c: (0, ML_HEADS + h)),
            pl.BlockSpec((1, ML_HD), lambda b, h, c: (0, h)),
        ],
        out_specs=pl.BlockSpec((ML_CHUNK, ML_HD), lambda b, h, c: (b * nc + c, h)),
        out_shape=jax.ShapeDtypeStruct((T, ML_HEADS * ML_HD), BF16),
        scratch_shapes=[
            pltpu.VMEM((ML_HD, ML_HD), F32), pltpu.VMEM((1, ML_HD), F32), pltpu.VMEM((1, 1), F32),
            pltpu.VMEM((8, ML_HD), F32), pltpu.VMEM((8, ML_HD), F32),
        ],
        compiler_params=_cparams(("parallel", "parallel", "arbitrary")),
        name="mlstm",
    )(gate_bias, proj, proj, proj, proj, if_arr, conv_w, conv_w, conv_b, conv_b, ng)


def _pack_bf16_pair(lo, hi):
    lo_b = pltpu.bitcast(lo.astype(BF16).astype(F32), U32)
    hi_b = pltpu.bitcast(hi.astype(BF16).astype(F32), U32)
    return (lo_b >> 16) | hi_b


def _unpack_bf16_pair(p):
    lo = pltpu.bitcast(p << 16, F32)
    hi = pltpu.bitcast(p & jnp.uint32(0xFFFF0000), F32)
    return lo, hi


def _row(ref, r):
    return ref.at[pl.ds(r, 1), :]


def _outproj_kernel(nsa_ref, ml_ref, w_ref, x_ref, g2_ref, wr_ref, br_ref, tril_ref,
                    x1_ref, xp_ref, rt_ref, cnt_ref, carry_ref):
    @pl.when(pl.program_id(0) == 0)
    def _():
        carry_ref[...] = jnp.zeros_like(carry_ref)

    half = D_MODEL // 2
    acc = jnp.dot(nsa_ref[...], w_ref[0:half, :], preferred_element_type=F32)
    acc = acc + jnp.dot(ml_ref[...], w_ref[half:, :], preferred_element_type=F32)
    x1 = x_ref[...] + acc
    x1_ref[...] = x1
    xn = x1 * lax.rsqrt(jnp.mean(x1 * x1, axis=-1, keepdims=True) + NORM_EPS) * g2_ref[...]
    xp_ref[...] = _pack_bf16_pair(xn[:, :half], xn[:, half:])
    logits = jnp.dot(xn.astype(BF16), wr_ref[...], preferred_element_type=F32) + br_ref[...]

    tm = logits.shape[0]
    lane = lax.broadcasted_iota(jnp.int32, (tm, LANE), 1)
    lane_f = lane.astype(F32)
    big = float(LANE)
    gmask = lane < MOE_GROUPS
    gmax = jnp.max(jnp.where(gmask, logits, NEG_INF), axis=1, keepdims=True)
    ge = jnp.where(gmask, jnp.exp(logits - gmax), 0.0)
    gp = ge / jnp.sum(ge, axis=1, keepdims=True)
    g_w = jnp.max(gp, axis=1, keepdims=True)
    g_idx = jnp.min(jnp.where(gmask, jnp.where(gp == g_w, lane_f, big), big), axis=1, keepdims=True)
    grp_of_lane = ((lane - MOE_GROUPS) >> EPG_LOG2).astype(F32)
    emask = jnp.where(lane >= MOE_GROUPS, grp_of_lane, -1.0) == g_idx
    emax = jnp.max(jnp.where(emask, logits, NEG_INF), axis=1, keepdims=True)
    ee = jnp.where(emask, jnp.exp(logits - emax), 0.0)
    ep = jnp.where(emask, ee / jnp.sum(ee, axis=1, keepdims=True), -1.0)
    v1 = jnp.max(ep, axis=1, keepdims=True)
    i1 = jnp.min(jnp.where(ep == v1, lane_f, big), axis=1, keepdims=True)
    ep2 = jnp.where(lane_f == i1, -1.0, ep)
    v2 = jnp.max(ep2, axis=1, keepdims=True)
    i2 = jnp.min(jnp.where(ep2 == v2, lane_f, big), axis=1, keepdims=True)
    w0 = g_w * v1 / (v1 + v2)
    w1 = g_w * v2 / (v1 + v2)
    e0 = i1 - float(MOE_GROUPS)
    e1 = i2 - float(MOE_GROUPS)

    oh0 = jnp.where(lane_f == e0, 1.0, 0.0)
    oh1 = jnp.where(lane_f == e1, 1.0, 0.0)
    pre0 = jnp.dot(tril_ref[...], oh0.astype(BF16), preferred_element_type=F32)
    pre1 = jnp.dot(tril_ref[...], oh1.astype(BF16), preferred_element_type=F32)
    carry = carry_ref[...]
    tot0 = pre0[tm - 1:tm, :]
    tot1 = pre1[tm - 1:tm, :]
    rank0 = jnp.sum(oh0 * (pre0 - 1.0 + carry), axis=1, keepdims=True)
    rank1 = jnp.sum(oh1 * (pre1 - 1.0 + carry + tot0), axis=1, keepdims=True)
    new_carry = carry + tot0 + tot1
    carry_ref[...] = new_carry
    cnt_ref[...] = jnp.broadcast_to(new_carry, cnt_ref.shape)

    rt = jnp.where(lane == 0, e0, jnp.where(lane == 1, e1, jnp.where(lane == 2, w0, jnp.where(
        lane == 3, w1, jnp.where(lane == 4, rank0, jnp.where(lane == 5, rank1, 0.0))))))
    rt_ref[...] = rt


def _out_proj(nsa_o, ml_o, w_out, x2, g2, w_r, b_r, tril):
    T = x2.shape[0]
    half = D_MODEL // 2
    return pl.pallas_call(
        _outproj_kernel,
        grid=(T // TM_OUT,),
        in_specs=[
            pl.BlockSpec((TM_OUT, half), lambda i: (i, 0)),
            pl.BlockSpec((TM_OUT, half), lambda i: (i, 0)),
            pl.BlockSpec((D_MODEL, D_MODEL), lambda i: (0, 0)),
            pl.BlockSpec((TM_OUT, D_MODEL), lambda i: (i, 0)),
            pl.BlockSpec((1, D_MODEL), lambda i: (0, 0)),
            pl.BlockSpec((D_MODEL, LANE), lambda i: (0, 0)),
            pl.BlockSpec((1, LANE), lambda i: (0, 0)),
            pl.BlockSpec((TM_OUT, TM_OUT), lambda i: (0, 0)),
        ],
        out_specs=[
            pl.BlockSpec((TM_OUT, D_MODEL), lambda i: (i, 0)),
            pl.BlockSpec((TM_OUT, half), lambda i: (i, 0)),
            pl.BlockSpec((TM_OUT, LANE), lambda i: (i, 0)),
            pl.BlockSpec((8, LANE), lambda i: (0, 0)),
        ],
        out_shape=[
            jax.ShapeDtypeStruct((T, D_MODEL), F32),
            jax.ShapeDtypeStruct((T, half), U32),
            jax.ShapeDtypeStruct((T, LANE), F32),
            jax.ShapeDtypeStruct((8, LANE), F32),
        ],
        scratch_shapes=[pltpu.VMEM((1, LANE), F32)],
        compiler_params=_cparams(("arbitrary",)),
        name="out_proj",
    )(nsa_o, ml_o, w_out, x2, g2, w_r, b_r, tril)


TD = 256
DISPATCH_BUFS = 3
DISPATCH_UNROLL = 8


def _dispatch_kernel(dest_ref, zblk_ref, meta_ref, xp_hbm, xs_hbm, zbuf_ref, zsem_ref, tbuf_ref, lsem_ref,
                     rsem_ref):
    n_assign = dest_ref.shape[0]
    nb = xs_hbm.shape[0] // BM
    n_used = meta_ref[0]
    zbuf_ref[...] = jnp.zeros_like(zbuf_ref)

    def zero_block(blk):
        return pltpu.make_async_copy(zbuf_ref, xs_hbm.at[pl.ds(blk * BM, BM), :], zsem_ref.at[0])

    def for_zero_blocks(fn):
        for e in range(N_EXPERTS):
            @pl.when(zblk_ref[e] >= 0)
            def _(e=e):
                fn(zero_block(zblk_ref[e]))

        def body(blk, _):
            fn(zero_block(blk))
            return 0
        lax.fori_loop(n_used, nb, body, 0)

    for_zero_blocks(lambda c: c.start())
    for_zero_blocks(lambda c: c.wait())

    n_tiles = n_assign // (2 * TD)

    def load(tile, slot):
        return pltpu.make_async_copy(xp_hbm.at[pl.ds(tile * TD, TD), :], tbuf_ref.at[slot], lsem_ref.at[slot])

    def wait_scatter(slot):
        for _ in range(2):
            pltpu.make_async_copy(tbuf_ref.at[slot], tbuf_ref.at[slot], rsem_ref.at[slot]).wait()

    load(0, 0).start()

    def tile_step(i, _):
        slot = i % DISPATCH_BUFS
        nslot = (i + 1) % DISPATCH_BUFS

        @pl.when(i + 1 < n_tiles)
        def _():
            @pl.when(i + 1 >= DISPATCH_BUFS)
            def _():
                wait_scatter(nslot)
            load(i + 1, nslot).start()

        load(i, slot).wait()

        def body(j, _):
            for u in range(DISPATCH_UNROLL):
                r = j * DISPATCH_UNROLL + u
                for k in range(2):
                    d = dest_ref[(i * TD + r) * 2 + k]
                    pltpu.make_async_copy(_row(tbuf_ref.at[slot], r), _row(xs_hbm, d),
                                          rsem_ref.at[slot]).start(priority=k)
            return 0
        lax.fori_loop(0, TD // DISPATCH_UNROLL, body, 0)
        return 0

    lax.fori_loop(0, n_tiles, tile_step, 0)
    for back in range(min(DISPATCH_BUFS, n_tiles)):
        wait_scatter((n_tiles - 1 - back) % DISPATCH_BUFS)


def _dispatch(dest, zblk, meta, xp, n_rows):
    half = D_MODEL // 2
    assert dest.shape[0] % (2 * TD) == 0 and dest.shape[0] // (2 * TD) >= DISPATCH_BUFS
    grid_spec = pltpu.PrefetchScalarGridSpec(
        num_scalar_prefetch=3,
        grid=(1,),
        in_specs=[pl.BlockSpec(memory_space=pl.ANY)],
        out_specs=pl.BlockSpec(memory_space=pl.ANY),
        scratch_shapes=[
            pltpu.VMEM((BM, half), U32),
            pltpu.SemaphoreType.DMA((1,)),
            pltpu.VMEM((DISPATCH_BUFS, TD, half), U32),
            pltpu.SemaphoreType.DMA((DISPATCH_BUFS,)),
            pltpu.SemaphoreType.DMA((DISPATCH_BUFS,)),
        ],
    )
    return pl.pallas_call(
        _dispatch_kernel,
        grid_spec=grid_spec,
        out_shape=jax.ShapeDtypeStruct((n_rows, half), U32),
        compiler_params=_cparams(("arbitrary",)),
        name="dispatch",
    )(dest, zblk, meta, xp)


def _expert_kernel(be_ref, nxt_ref, meta_ref, x_ref, wg_hbm, wu_hbm, wd_hbm, y_ref,
                   wsg_ref, wsu_ref, wsd_ref, wsem_ref, wgb_ref, wub_ref, wdb_ref):
    i = pl.program_id(0)
    n_used = meta_ref[0]
    half = D_MODEL // 2

    def weight_copies(e):
        return (pltpu.make_async_copy(wg_hbm.at[e], wsg_ref, wsem_ref.at[0]),
                pltpu.make_async_copy(wu_hbm.at[e], wsu_ref, wsem_ref.at[1]),
                pltpu.make_async_copy(wd_hbm.at[e], wsd_ref, wsem_ref.at[2]))

    @pl.when(i == 0)
    def _():
        for c in weight_copies(be_ref[0]):
            c.start()

    @pl.when(i < n_used)
    def _():
        e = be_ref[i]

        @pl.when((i == 0) | (e != be_ref[jnp.maximum(i - 1, 0)]))
        def _():
            for c in weight_copies(e):
                c.wait()
            wgb_ref[...] = wsg_ref[...].astype(BF16)
            wub_ref[...] = wsu_ref[...].astype(BF16)
            wdb_ref[...] = wsd_ref[...].astype(BF16)
            nx = nxt_ref[e]

            @pl.when(nx >= 0)
            def _():
                for c in weight_copies(nx):
                    c.start()

        lo, hi = _unpack_bf16_pair(x_ref[...])
        xl = lo.astype(BF16)
        xh = hi.astype(BF16)
        gt = (jnp.dot(xl, wgb_ref[0:half, :], preferred_element_type=F32)
              + jnp.dot(xh, wgb_ref[half:, :], preferred_element_type=F32))
        up = (jnp.dot(xl, wub_ref[0:half, :], preferred_element_type=F32)
              + jnp.dot(xh, wub_ref[half:, :], preferred_element_type=F32))
        hb = (gt * _sigmoid(gt) * up).astype(BF16)
        y = jnp.dot(hb, wdb_ref[...], preferred_element_type=F32)
        y_ref[...] = _pack_bf16_pair(y[:, :half], y[:, half:])

    @pl.when(i >= n_used)
    def _():
        y_ref[...] = jnp.zeros_like(y_ref)


def _experts(block_expert, next_expert, meta, xs, w_gate, w_up, w_down):
    half = D_MODEL // 2
    n_rows = xs.shape[0]
    nb = n_rows // BM
    grid_spec = pltpu.PrefetchScalarGridSpec(
        num_scalar_prefetch=3,
        grid=(nb,),
        in_specs=[
            pl.BlockSpec((BM, half), lambda i, be, nxt, meta: (jnp.minimum(i, meta[0] - 1), 0)),
            pl.BlockSpec(memory_space=pl.ANY), pl.BlockSpec(memory_space=pl.ANY),
            pl.BlockSpec(memory_space=pl.ANY),
        ],
        out_specs=pl.BlockSpec((BM, half), lambda i, be, nxt, meta: (i, 0)),
        scratch_shapes=[
            pltpu.VMEM((D_MODEL, D_EXPERT), F32),
            pltpu.VMEM((D_MODEL, D_EXPERT), F32),
            pltpu.VMEM((D_EXPERT, D_MODEL), F32),
            pltpu.SemaphoreType.DMA((3,)),
            pltpu.VMEM((D_MODEL, D_EXPERT), BF16),
            pltpu.VMEM((D_MODEL, D_EXPERT), BF16),
            pltpu.VMEM((D_EXPERT, D_MODEL), BF16),
        ],
    )
    return pl.pallas_call(
        _expert_kernel,
        grid_spec=grid_spec,
        out_shape=jax.ShapeDtypeStruct((n_rows, half), U32),
        compiler_params=_cparams(("arbitrary",)),
        name="experts",
    )(block_expert, next_expert, meta, xs, w_gate, w_up, w_down)


def _combine_kernel(dest_ref, y_hbm, x1_ref, rt_ref, fg_ref, o_ref, ya_ref, yb_ref, sem_ref):
    i = pl.program_id(0)
    nt = pl.num_programs(0)
    half = D_MODEL // 2
    bufs = (ya_ref, yb_ref)

    def row_copy(tile, slot, r_tile, s, k):
        r = r_tile * 8 + s
        d = dest_ref[(tile * TC + r) * 2 + k]
        return pltpu.make_async_copy(_row(y_hbm, d), _row(bufs[slot].at[k], r), sem_ref.at[slot])

    def wait_rows(slot):
        pltpu.make_async_copy(bufs[slot], bufs[slot], sem_ref.at[slot]).wait()

    def step(slot):
        wait_rows(slot)
        for r in range(TC):
            for k in range(2):
                row_copy(i + 1, 1 - slot, r // 8, r % 8, k).start(priority=k)
        rt = rt_ref[...]
        w0 = rt[:, 2:3]
        w1 = rt[:, 3:4]
        lo0, hi0 = _unpack_bf16_pair(bufs[slot][0])
        lo1, hi1 = _unpack_bf16_pair(bufs[slot][1])
        xl = x1_ref[:, :half] + (w0 * lo0 + w1 * lo1)
        xh = x1_ref[:, half:] + (w0 * hi0 + w1 * hi1)
        ms = (jnp.sum(xl * xl, axis=1, keepdims=True) + jnp.sum(xh * xh, axis=1, keepdims=True)) / float(D_MODEL)
        inv = lax.rsqrt(ms + NORM_EPS)
        o_ref[:, :half] = xl * inv * fg_ref[:, :half]
        o_ref[:, half:] = xh * inv * fg_ref[:, half:]

        @pl.when(i == nt - 1)
        def _():
            wait_rows(1 - slot)

    @pl.when(i == 0)
    def _():
        def body(rt, _):
            for s in range(8):
                for k in range(2):
                    row_copy(0, 0, rt, s, k).start(priority=k)
            return 0
        lax.fori_loop(0, TC // 8, body, 0)

    @pl.when(i % 2 == 0)
    def _():
        step(0)

    @pl.when(i % 2 == 1)
    def _():
        step(1)


def _combine(dest, ys, x1, route, fg):
    T = x1.shape[0]
    half = D_MODEL // 2
    grid_spec = pltpu.PrefetchScalarGridSpec(
        num_scalar_prefetch=1,
        grid=(T // TC,),
        in_specs=[
            pl.BlockSpec(memory_space=pl.ANY),
            pl.BlockSpec((TC, D_MODEL), lambda i, d: (i, 0)),
            pl.BlockSpec((TC, LANE), lambda i, d: (i, 0)),
            pl.BlockSpec((1, D_MODEL), lambda i, d: (0, 0)),
        ],
        out_specs=pl.BlockSpec((TC, D_MODEL), lambda i, d: (i, 0)),
        scratch_shapes=[
            pltpu.VMEM((2, TC, half), U32),
            pltpu.VMEM((2, TC, half), U32),
            pltpu.SemaphoreType.DMA((2,)),
        ],
    )
    return pl.pallas_call(
        _combine_kernel,
        grid_spec=grid_spec,
        out_shape=jax.ShapeDtypeStruct((T, D_MODEL), F32),
        compiler_params=_cparams(("arbitrary",)),
        name="combine",
    )(dest, ys, x1, route, fg)


def _cover_matrix(S):
    n_blk = S // CMP_STRIDE
    n_sel = S // SEL_BLOCK
    cs = np.arange(n_blk) * CMP_STRIDE
    ss = np.arange(n_sel) * SEL_BLOCK
    shared = np.minimum(cs[:, None] + CMP_LEN, ss[None, :] + SEL_BLOCK) - np.maximum(cs[:, None], ss[None, :])
    return (np.clip(shared, 0, None) / CMP_LEN).T.astype(np.float32)


def _block_mask_matrix(S):
    n_sel = S // SEL_BLOCK
    assert n_sel <= LANE
    e = np.zeros((S, LANE), np.float32)
    e[np.arange(S), np.arange(S) // SEL_BLOCK] = NEG_INF
    return e


def _inv_freq_row():
    inv = np.power(np.float32(ROPE_THETA), -np.arange(ROPE_HALF, dtype=np.float32) * 2.0 / ROPE_DIM)
    row = np.zeros((1, LANE), np.float32)
    row[0, :ROPE_HALF] = inv
    row[0, ROPE_HALF:ROPE_DIM] = inv
    return row


def _layer(x2, positions, B, S, norm1_g, w_in, cmp_pe_k, cmp_pe_v, cmp_wk1, cmp_wk2, cmp_wv1, cmp_wv2,
           nsa_norm_g, conv_w, conv_b, b_igate, b_fgate, mlstm_norm_g, w_out, norm2_g,
           w_group, b_group, w_router, b_router, w_exp_gate, w_exp_up, w_exp_down, out_norm_g):
    T = B * S
    o_q, o_kv, o_g, o_qk, o_v, o_o, o_i, o_f = 0, 1024, 2560, 2584, 4632, 5656, 6680, 6684
    w_main = jnp.concatenate([w_in[:, o_q:o_g], w_in[:, o_qk:o_i]], axis=1).astype(BF16)
    w_small = jnp.concatenate([w_in[:, o_g:o_qk], w_in[:, o_i:], jnp.zeros((D_MODEL, LANE - 32), F32)],
                              axis=1).astype(BF16)
    posb = jnp.broadcast_to(positions.reshape(T, 1).astype(F32), (T, LANE))
    invf = jnp.asarray(_inv_freq_row())

    proj, small = _in_proj(x2, norm1_g.reshape(1, -1), w_main, w_small)
    q_r, k_r = _rope(posb, invf, proj)
    kc, vc = _compress(k_r, proj, B, S, cmp_pe_k, cmp_pe_v, cmp_wk1.astype(BF16), cmp_wk2.astype(BF16),
                       cmp_wv1.astype(BF16), cmp_wv2.astype(BF16))
    eye = np.eye(S // SEL_BLOCK, LANE, dtype=np.float32)
    nsa_o = _nsa(q_r, k_r, proj, small, kc, vc, jnp.asarray(_cover_matrix(S), BF16),
                 jnp.asarray(_block_mask_matrix(S), BF16), jnp.asarray(eye, BF16),
                 nsa_norm_g.reshape(1, -1), B, S)

    if_arr = small[:, 24:32].reshape(B, S, 2, ML_HEADS).transpose(0, 3, 2, 1)
    gate_bias = jnp.stack([b_igate, b_fgate]).astype(F32)
    ml_o = _mlstm(proj, if_arr, gate_bias, conv_w, conv_b.reshape(1, -1), mlstm_norm_g.reshape(1, -1), B, S)

    w_r = jnp.concatenate([w_group, w_router, jnp.zeros((D_MODEL, LANE - MOE_GROUPS - N_EXPERTS), F32)],
                          axis=1).astype(BF16)
    b_r = jnp.concatenate([b_group, b_router, jnp.zeros((LANE - MOE_GROUPS - N_EXPERTS,), F32)]).reshape(1, LANE)
    tril = jnp.asarray(np.tril(np.ones((TM_OUT, TM_OUT), np.float32)), BF16)
    x1, xp, route, cnt = _out_proj(nsa_o, ml_o, w_out.astype(BF16), x2, norm2_g.reshape(1, -1), w_r, b_r, tril)

    n_rows = T * 2 + N_EXPERTS * BM
    counts = cnt[0, :N_EXPERTS].astype(jnp.int32)
    pcounts = (counts + BM - 1) // BM * BM
    pends = jnp.cumsum(pcounts)
    pstarts = pends - pcounts
    eid = route[:, 0:2].astype(jnp.int32)
    dest = (pstarts[eid] + route[:, 4:6].astype(jnp.int32)).reshape(T * 2)
    block_expert = jnp.minimum(jnp.searchsorted(pends, jnp.arange(n_rows // BM) * BM, side='right'),
                               N_EXPERTS - 1).astype(jnp.int32)
    present = jnp.where(counts > 0, jnp.arange(N_EXPERTS, dtype=jnp.int32), N_EXPERTS)
    later = jnp.concatenate([lax.cummin(present[::-1])[::-1][1:], jnp.full((1,), N_EXPERTS, jnp.int32)])
    next_expert = jnp.where(later < N_EXPERTS, later, -1).astype(jnp.int32)
    meta = (pends[-1:] // BM).astype(jnp.int32)

    last_block = jnp.where(counts > 0, pends // BM - 1, -1).astype(jnp.int32)

    xs = _dispatch(dest, last_block, meta, xp, n_rows)
    ys = _experts(block_expert, next_expert, meta, xs, w_exp_gate, w_exp_up, w_exp_down)
    dest_pad = jnp.concatenate([dest, jnp.zeros((2 * TC,), jnp.int32)])
    return _combine(dest_pad, ys, x1, route, out_norm_g.reshape(1, -1))


def kernel(x, positions, norm1_g, w_in, cmp_pe_k, cmp_pe_v, cmp_wk1, cmp_wk2, cmp_wv1, cmp_wv2, nsa_norm_g,
           conv_w, conv_b, b_igate, b_fgate, mlstm_norm_g, w_out, norm2_g, w_group, b_group, w_router,
           b_router, w_exp_gate, w_exp_up, w_exp_down, final_norm_g):
    B, S, D = x.shape
    assert D == D_MODEL and norm1_g.shape[0] == 1, "single-layer, D_MODEL-wide configuration only"
    assert S % ML_CHUNK == 0 and S % SEL_BUCKET == 0 and S >= WIN_KEYS and (B * S) % TM_IN == 0
    out = _layer(x.reshape(B * S, D), positions, B, S, norm1_g[0], w_in[0], cmp_pe_k[0], cmp_pe_v[0],
                 cmp_wk1[0], cmp_wk2[0], cmp_wv1[0], cmp_wv2[0], nsa_norm_g[0], conv_w[0], conv_b[0],
                 b_igate[0], b_fgate[0], mlstm_norm_g[0], w_out[0], norm2_g[0], w_group[0], b_group[0],
                 w_router[0], b_router[0], w_exp_gate[0], w_exp_up[0], w_exp_down[0], final_norm_g)
    return out.reshape(B, S, D)
```

```python
import functools

import numpy as np
import jax
import jax.numpy as jnp
from jax import lax
from jax.experimental import pallas as pl
from jax.experimental.pallas import tpu as pltpu

F32 = jnp.float32
BF16 = jnp.bfloat16
U32 = jnp.uint32

D_MODEL = 2048
NSA_HEADS = 8
NSA_GROUPS = 2
NSA_J = NSA_HEADS // NSA_GROUPS
HD = 128
CMP_LEN = 32
CMP_STRIDE = 16
SEL_BLOCK = 64
SEL_TOPK = 8
WINDOW = 512
ROPE_THETA = 500000.0
ROPE_DIM = 32
ROPE_HALF = 16
ML_HEADS = 4
ML_HD = 256
CONV_W = 4
MOE_GROUPS = 8
EPG = 8
N_EXPERTS = 64
D_EXPERT = 512
SEL_BLOCK_LOG2 = 6
EPG_LOG2 = 3
assert (1 << SEL_BLOCK_LOG2) == SEL_BLOCK and (1 << EPG_LOG2) == EPG
NORM_EPS = 1e-6
NEG_INF = -1e30
FORCE_SCORE = 1000.0

COL_Q = 0
COL_KV = 1024
COL_QKM = 2560
COL_VM = 4608
COL_OM = 5632
N_MAIN = 6656

LANE = 128
VMEM_LIMIT = 56 * 1024 * 1024

TM_IN = 1024
TN_IN = 1664
TS_ROPE = 256
TQ = 128
SEL_BUCKET = 512
WIN_KEYS = WINDOW + TQ
ML_CHUNK = 256
TM_OUT = 512
BM = 256
TC = 256

NT_DIMS = (((1,), (1,)), ((), ()))
TN_DIMS = (((0,), (0,)), ((), ()))


def _cparams(sem):
    return pltpu.CompilerParams(dimension_semantics=sem, vmem_limit_bytes=VMEM_LIMIT)


def _sigmoid(x):
    return 0.5 * jnp.tanh(0.5 * x) + 0.5


def _silu(x):
    h = 0.5 * x
    return h + h * jnp.tanh(h)


def _inproj_kernel(x_ref, g_ref, w_ref, ws_ref, o_ref, os_ref, h_ref):
    @pl.when(pl.program_id(1) == 0)
    def _():
        x = x_ref[...]
        ms = jnp.mean(x * x, axis=-1, keepdims=True)
        h_ref[...] = (x * lax.rsqrt(ms + NORM_EPS) * g_ref[...]).astype(BF16)
        os_ref[...] = jnp.dot(h_ref[...], ws_ref[...], preferred_element_type=F32)

    o_ref[...] = jnp.dot(h_ref[...], w_ref[...], preferred_element_type=F32).astype(o_ref.dtype)


def _in_proj(x2, g1, w_main, w_small):
    T = x2.shape[0]
    return pl.pallas_call(
        _inproj_kernel,
        grid=(T // TM_IN, N_MAIN // TN_IN),
        in_specs=[
            pl.BlockSpec((TM_IN, D_MODEL), lambda m, n: (m, 0)),
            pl.BlockSpec((1, D_MODEL), lambda m, n: (0, 0)),
            pl.BlockSpec((D_MODEL, TN_IN), lambda m, n: (0, n)),
            pl.BlockSpec((D_MODEL, LANE), lambda m, n: (0, 0)),
        ],
        out_specs=[
            pl.BlockSpec((TM_IN, TN_IN), lambda m, n: (m, n)),
            pl.BlockSpec((TM_IN, LANE), lambda m, n: (m, 0)),
        ],
        out_shape=[
            jax.ShapeDtypeStruct((T, N_MAIN), BF16),
            jax.ShapeDtypeStruct((T, LANE), F32),
        ],
        scratch_shapes=[pltpu.VMEM((TM_IN, D_MODEL), BF16)],
        compiler_params=_cparams(("parallel", "arbitrary")),
        name="in_proj",
    )(x2, g1, w_main, w_small)


def _rope_kernel(pos_ref, invf_ref, q_ref, kc_ref, ks_ref, kw_ref, qo_ref, ko_ref):
    ang = pos_ref[...] * invf_ref[...]
    c = jnp.cos(ang)
    s = jnp.sin(ang)
    lane = lax.broadcasted_iota(jnp.int32, ang.shape, 1)
    sa = jnp.where(lane < ROPE_HALF, -s, 0.0)
    sb = jnp.where(lane < ROPE_HALF, 0.0, s)

    def rope(x):
        return x * c + pltpu.roll(x, LANE - ROPE_HALF, 1) * sa + pltpu.roll(x, ROPE_HALF, 1) * sb

    scale = HD ** -0.5
    for h in range(NSA_HEADS):
        sl = slice(h * HD, (h + 1) * HD)
        qo_ref[:, sl] = (rope(q_ref[:, sl].astype(F32)) * scale).astype(BF16)
    for i, r in enumerate((kc_ref, ks_ref, kw_ref)):
        for g in range(NSA_GROUPS):
            sl = slice(g * HD, (g + 1) * HD)
            so = slice(i * 2 * HD + g * HD, i * 2 * HD + (g + 1) * HD)
            ko_ref[:, so] = rope(r[:, sl].astype(F32)).astype(BF16)


def _rope(posb, invf, proj):
    T = proj.shape[0]
    kvb = COL_KV // 256
    return pl.pallas_call(
        _rope_kernel,
        grid=(T // TS_ROPE,),
        in_specs=[
            pl.BlockSpec((TS_ROPE, LANE), lambda i: (i, 0)),
            pl.BlockSpec((1, LANE), lambda i: (0, 0)),
            pl.BlockSpec((TS_ROPE, 1024), lambda i: (i, 0)),
            pl.BlockSpec((TS_ROPE, 256), lambda i: (i, kvb + 0)),
            pl.BlockSpec((TS_ROPE, 256), lambda i: (i, kvb + 2)),
            pl.BlockSpec((TS_ROPE, 256), lambda i: (i, kvb + 4)),
        ],
        out_specs=[
            pl.BlockSpec((TS_ROPE, 1024), lambda i: (i, 0)),
            pl.BlockSpec((TS_ROPE, 768), lambda i: (i, 0)),
        ],
        out_shape=[
            jax.ShapeDtypeStruct((T, 1024), BF16),
            jax.ShapeDtypeStruct((T, 768), BF16),
        ],
        compiler_params=_cparams(("parallel",)),
        name="rope",
    )(posb, invf, proj, proj, proj, proj)


def _compress_kernel(k_ref, v_ref, pek_ref, pev_ref, w1k_ref, w2k_ref, w1v_ref, w2v_ref,
                     kc_ref, vc_ref, xs_ref):
    S = k_ref.shape[0]
    n_blk = S // CMP_STRIDE
    for src, pe, w1, w2, dst in ((k_ref, pek_ref, w1k_ref, w2k_ref, kc_ref),
                                 (v_ref, pev_ref, w1v_ref, w2v_ref, vc_ref)):
        xs_ref[0:S, :] = src[...].astype(F32)
        xs_ref[S:S + CMP_LEN, :] = jnp.zeros((CMP_LEN, HD), F32)
        acc = jnp.zeros((n_blk, HD), F32)
        for l in range(CMP_LEN):
            a = xs_ref[pl.ds(l, n_blk, stride=CMP_STRIDE), :] + pe[l:l + 1, :]
            acc = acc + jnp.dot(a.astype(BF16), w1[l * HD:(l + 1) * HD, :],
                                preferred_element_type=F32)
        hid = _silu(acc)
        out = jnp.dot(hid.astype(BF16), w2[...], preferred_element_type=F32)
        dst[0, 0] = out.astype(BF16)


def _compress(k_r, proj, B, S, pek, pev, w1k, w2k, w1v, w2v):
    n_blk = S // CMP_STRIDE
    vcol = (COL_KV + 256) // HD
    full = lambda shape: pl.BlockSpec(shape, lambda b, g: tuple(0 for _ in shape))
    return pl.pallas_call(
        _compress_kernel,
        grid=(B, NSA_GROUPS),
        in_specs=[
            pl.BlockSpec((S, HD), lambda b, g: (b, g)),
            pl.BlockSpec((S, HD), lambda b, g: (b, vcol + g)),
            full((CMP_LEN, HD)), full((CMP_LEN, HD)),
            full((CMP_LEN * HD, HD)), full((HD, HD)),
            full((CMP_LEN * HD, HD)), full((HD, HD)),
        ],
        out_specs=[
            pl.BlockSpec((1, 1, n_blk, HD), lambda b, g: (b, g, 0, 0)),
            pl.BlockSpec((1, 1, n_blk, HD), lambda b, g: (b, g, 0, 0)),
        ],
        out_shape=[
            jax.ShapeDtypeStruct((B, NSA_GROUPS, n_blk, HD), BF16),
            jax.ShapeDtypeStruct((B, NSA_GROUPS, n_blk, HD), BF16),
        ],
        scratch_shapes=[pltpu.VMEM((S + CMP_LEN, HD), F32)],
        compiler_params=_cparams(("parallel", "parallel")),
        name="compress",
    )(k_r, proj, pek, pev, w1k, w2k, w1v, w2v)


def _nsa_kernel(q_ref, kc_ref, vc_ref, ks_ref, vs_ref, kw_ref, vw_ref, sm_ref, covt_ref, eneg_ref, eye_ref,
                ng_ref, o_ref, obuf_ref, owin_ref, osel_ref):
    qi = pl.program_id(1)
    q0 = qi * TQ
    R = NSA_J * TQ
    n_cmp = kc_ref.shape[2]
    n_sel = ks_ref.shape[0] // SEL_BLOCK
    n_win = WIN_KEYS // TQ

    def row_t(shape):
        r = lax.broadcasted_iota(jnp.int32, shape, 0)
        return q0 + (r & (TQ - 1))

    qgs = [jnp.concatenate([q_ref[:, (g * NSA_J + j) * HD:(g * NSA_J + j + 1) * HD]
                            for j in range(NSA_J)], axis=0) for g in range(NSA_GROUPS)]

    def with_ones(v):
        lane = lax.broadcasted_iota(jnp.int32, v.shape, 1)
        return jnp.concatenate([v, jnp.where(lane == 0, 1.0, 0.0).astype(BF16)], axis=1)

    def window(masks):
        w0 = pl.multiple_of(jnp.maximum(q0 - WINDOW, 0), TQ)
        for g in range(NSA_GROUPS):
            kt = kw_ref[pl.ds(w0, WIN_KEYS), g * HD:(g + 1) * HD]
            vt = with_ones(vw_ref[pl.ds(w0, WIN_KEYS), g * HD:(g + 1) * HD])
            sc = masks(lax.dot_general(qgs[g], kt, NT_DIMS, preferred_element_type=F32), w0)
            mw = jnp.max(sc, axis=1, keepdims=True)
            pw = jnp.exp((sc - mw).astype(BF16))
            acc = jnp.dot(pw, vt, preferred_element_type=F32)
            owin_ref[g] = acc[:, 0:HD] / acc[:, HD:HD + 1]

    def band_masks(sc, w0):
        d = (lax.broadcasted_iota(jnp.int32, (R, TQ), 1)
             - (lax.broadcasted_iota(jnp.int32, (R, TQ), 0) & (TQ - 1)))
        first = jnp.where(d > 0, sc[:, 0:TQ], NEG_INF)
        last = jnp.where(d <= 0, sc[:, (n_win - 1) * TQ:], NEG_INF)
        return jnp.concatenate([first, sc[:, TQ:(n_win - 1) * TQ], last], axis=1)

    def general_masks(sc, w0):
        diff = row_t((R, WIN_KEYS)) - (w0 + lax.broadcasted_iota(jnp.int32, (R, WIN_KEYS), 1))
        keep = jnp.where(diff >= 0, diff, WINDOW) < WINDOW
        return jnp.where(keep, sc, NEG_INF)

    @pl.when(q0 >= WINDOW)
    def _():
        window(band_masks)

    @pl.when(q0 < WINDOW)
    def _():
        window(general_masks)

    gates = _sigmoid(sm_ref[...])
    o_cmps = []
    qps = []
    for g in range(NSA_GROUPS):
        qg = qgs[g]

        s = lax.dot_general(qg, kc_ref[0, g], NT_DIMS, preferred_element_type=F32)
        n_lane = lax.broadcasted_iota(jnp.int32, (R, n_cmp), 1)
        cmask = (n_lane * CMP_STRIDE + (CMP_LEN - 1)) <= row_t((R, n_cmp))
        s = jnp.where(cmask, s, NEG_INF)
        m = jnp.max(s, axis=1, keepdims=True)
        e = jnp.where(cmask, jnp.exp(s - m), 0.0)
        l = jnp.sum(e, axis=1, keepdims=True)
        p = (e / jnp.where(l > 0.0, l, 1.0)).astype(BF16)
        o_cmps.append(jnp.dot(p, vc_ref[0, g], preferred_element_type=F32))
        impr = lax.dot_general(covt_ref[...], p, NT_DIMS, preferred_element_type=F32)
        imp = impr[:, 0:TQ]
        for j in range(1, NSA_J):
            imp = imp + impr[:, j * TQ:(j + 1) * TQ]

        m_sub = lax.broadcasted_iota(jnp.int32, (n_sel, TQ), 0)
        jt = (q0 + lax.broadcasted_iota(jnp.int32, (n_sel, TQ), 1)) >> SEL_BLOCK_LOG2
        forced = jnp.where(m_sub == 0, FORCE_SCORE,
                           jnp.where(m_sub == jt, FORCE_SCORE,
                                     jnp.where(m_sub == jt - 1, FORCE_SCORE, 0.0)))
        score = jnp.where(m_sub <= jt, imp + forced, -1.0)
        rank = jnp.zeros((n_sel, TQ), F32)
        for mp in range(n_sel):
            row = score[mp:mp + 1, :]
            ge = jnp.where(row >= score, 1.0, 0.0)
            gt = jnp.where(row > score, 1.0, 0.0)
            rank = rank + jnp.where(m_sub > mp, ge, gt)
        notsel_t = jnp.where(rank < float(min(SEL_TOPK, n_sel)), 0.0, 1.0).astype(BF16)
        notsel = lax.dot_general(notsel_t, eye_ref[...], TN_DIMS,
                                 preferred_element_type=F32).astype(BF16)
        qps.append(jnp.concatenate([qg, jnp.concatenate([notsel] * NSA_J, axis=0)], axis=1))

    def sel_oneshot(nk):
        for g in range(NSA_GROUPS):
            kt = jnp.concatenate([ks_ref[0:nk, g * HD:(g + 1) * HD], eneg_ref[0:nk, :]], axis=1)
            vt = with_ones(vs_ref[0:nk, g * HD:(g + 1) * HD])
            sc = lax.dot_general(qps[g], kt, NT_DIMS, preferred_element_type=F32)
            klane = lax.broadcasted_iota(jnp.int32, (R, SEL_BUCKET), 1) + (nk - SEL_BUCKET)
            tail = jnp.where(klane <= row_t((R, SEL_BUCKET)), sc[:, nk - SEL_BUCKET:], NEG_INF)
            sc = tail if nk == SEL_BUCKET else jnp.concatenate([sc[:, 0:nk - SEL_BUCKET], tail], axis=1)
            ms = jnp.max(sc, axis=1, keepdims=True)
            acc = jnp.dot(jnp.exp((sc - ms).astype(BF16)), vt, preferred_element_type=F32)
            osel_ref[g] = acc[:, 0:HD] / acc[:, HD:HD + 1]

    for b in range(ks_ref.shape[0] // SEL_BUCKET):
        @pl.when(q0 // SEL_BUCKET == b)
        def _(b=b):
            sel_oneshot((b + 1) * SEL_BUCKET)

    ssq = jnp.zeros((TQ, 1), F32)
    for g in range(NSA_GROUPS):
        o_sel = osel_ref[g]
        o_cmp = o_cmps[g]
        o_win = owin_ref[g]
        for j in range(NSA_J):
            h = g * NSA_J + j
            rs = slice(j * TQ, (j + 1) * TQ)
            o = (gates[:, 3 * h:3 * h + 1] * o_cmp[rs] + gates[:, 3 * h + 1:3 * h + 2] * o_sel[rs]
                 + gates[:, 3 * h + 2:3 * h + 3] * o_win[rs])
            ssq = ssq + jnp.sum(o * o, axis=1, keepdims=True)
            obuf_ref[:, h * HD:(h + 1) * HD] = o

    inv = lax.rsqrt(ssq / float(NSA_HEADS * HD) + NORM_EPS)
    o_ref[...] = (obuf_ref[...] * inv * ng_ref[...]).astype(BF16)


def _nsa(q_r, k_r, proj, small, kc, vc, covt, eneg, eye, ng, B, S):
    T = B * S
    nq = S // TQ
    n_blk = S // CMP_STRIDE
    n_sel = S // SEL_BLOCK
    kvb = COL_KV // 256
    return pl.pallas_call(
        _nsa_kernel,
        grid=(B, nq),
        in_specs=[
            pl.BlockSpec((TQ, 1024), lambda b, i: (b * nq + i, 0)),
            pl.BlockSpec((1, NSA_GROUPS, n_blk, HD), lambda b, i: (b, 0, 0, 0)),
            pl.BlockSpec((1, NSA_GROUPS, n_blk, HD), lambda b, i: (b, 0, 0, 0)),
            pl.BlockSpec((S, 256), lambda b, i: (b, 1)),
            pl.BlockSpec((S, 256), lambda b, i: (b, kvb + 3)),
            pl.BlockSpec((S, 256), lambda b, i: (b, 2)),
            pl.BlockSpec((S, 256), lambda b, i: (b, kvb + 5)),
            pl.BlockSpec((TQ, LANE), lambda b, i: (b * nq + i, 0)),
            pl.BlockSpec((n_sel, n_blk), lambda b, i: (0, 0)),
            pl.BlockSpec((S, LANE), lambda b, i: (0, 0)),
            pl.BlockSpec((n_sel, LANE), lambda b, i: (0, 0)),
            pl.BlockSpec((1, 1024), lambda b, i: (0, 0)),
        ],
        out_specs=pl.BlockSpec((TQ, 1024), lambda b, i: (b * nq + i, 0)),
        out_shape=jax.ShapeDtypeStruct((T, 1024), BF16),
        scratch_shapes=[pltpu.VMEM((TQ, 1024), F32), pltpu.VMEM((NSA_GROUPS, NSA_J * TQ, HD), F32),
                        pltpu.VMEM((NSA_GROUPS, NSA_J * TQ, HD), F32)],
        compiler_params=_cparams(("parallel", "parallel")),
        name="nsa",
    )(q_r, kc, vc, k_r, proj, k_r, proj, small, covt, eneg, eye, ng)


ML_HPS = 2


def _mlstm_kernel(bias_ref, q_ref, k_ref, v_ref, om_ref, if_ref, cwq_ref, cwk_ref, cbq_ref, cbk_ref,
                  ng_ref, o_ref, c_ref, n_ref, m_ref, xq_ref, xk_ref):
    hp = pl.program_id(1)
    L = ML_CHUNK

    @pl.when(pl.program_id(2) == 0)
    def _():
        c_ref[...] = jnp.zeros_like(c_ref)
        n_ref[...] = jnp.zeros_like(n_ref)
        m_ref[...] = jnp.zeros_like(m_ref)
        xq_ref[:, 0:8, :] = jnp.zeros((ML_HPS, 8, ML_HD), F32)
        xk_ref[:, 0:8, :] = jnp.zeros((ML_HPS, 8, ML_HD), F32)

    def conv_silu(x_ref, hh, xb_ref, w_ref, b_ref):
        cs = slice(hh * ML_HD, (hh + 1) * ML_HD)
        xb_ref[hh, 8:8 + L, :] = x_ref[:, cs].astype(F32)
        y = b_ref[:, cs] + xb_ref[hh, 8:8 + L, :] * w_ref[CONV_W - 1:CONV_W, cs]
        for k in range(1, CONV_W):
            y = y + xb_ref[hh, 8 - k:8 - k + L, :] * w_ref[CONV_W - 1 - k:CONV_W - k, cs]
        xb_ref[hh, 0:8, :] = xb_ref[hh, L:L + 8, :]
        return _silu(y)

    r = lax.broadcasted_iota(jnp.int32, (L, L), 0)
    cidx = lax.broadcasted_iota(jnp.int32, (L, L), 1)
    tril = cidx <= r
    eye = cidx == r

    for hh in range(ML_HPS):
        h = hp * ML_HPS + hh
        cs = slice(hh * ML_HD, (hh + 1) * ML_HD)
        qf = conv_silu(q_ref, hh, xq_ref, cwq_ref, cbq_ref) * (ML_HD ** -0.5)
        kf = conv_silu(k_ref, hh, xk_ref, cwk_ref, cbk_ref)
        qb = qf.astype(BF16)
        kb = kf.astype(BF16)
        vb = v_ref[:, cs]
        vf = vb.astype(F32)

        ic = if_ref[0, hh, 0:1, :] + bias_ref[0, h]
        fp = if_ref[0, hh, 1:2, :] + bias_ref[1, h]
        fc = jnp.minimum(fp, 0.0) - jnp.log(1.0 + jnp.exp(-jnp.abs(fp)))

        fc_b = jnp.broadcast_to(fc, (L, L))
        ic_b = jnp.broadcast_to(ic, (L, L))
        b_col = jnp.sum(jnp.where(tril, fc_b, 0.0), axis=1, keepdims=True)
        fc_col = jnp.sum(jnp.where(eye, fc_b, 0.0), axis=1, keepdims=True)
        ic_col = jnp.sum(jnp.where(eye, ic_b, 0.0), axis=1, keepdims=True)
        b_row = jnp.sum(jnp.where(r <= cidx, jnp.broadcast_to(fc_col, (L, L)), 0.0),
                        axis=0, keepdims=True)
        b_last = b_col[L - 1:L, :]
        m_prev = m_ref[hh]

        d_log = jnp.where(tril, b_col - b_row + ic, NEG_INF)
        inter = b_col + m_prev
        m_t = jnp.maximum(inter, jnp.max(d_log, axis=1, keepdims=True))
        w_intra = jnp.exp(d_log - m_t)
        w_inter = jnp.exp(inter - m_t)
        qk = lax.dot_general(qb, kb, NT_DIMS, preferred_element_type=F32) * w_intra
        num = (jnp.dot(qk.astype(BF16), vb, preferred_element_type=F32)
               + w_inter * lax.dot_general(qb, c_ref[hh].astype(BF16), NT_DIMS, preferred_element_type=F32))
        den = jnp.sum(qk, axis=1, keepdims=True) + w_inter * jnp.sum(qf * n_ref[hh], axis=1, keepdims=True)
        hm = num / jnp.maximum(jnp.abs(den), jnp.exp(-m_t))

        w_log = b_last - b_col + ic_col
        m_new = jnp.maximum(b_last + m_prev, jnp.max(w_log, axis=0, keepdims=True))
        w_state = jnp.exp(w_log - m_new)
        decay = jnp.exp(b_last + m_prev - m_new)
        c_ref[hh] = decay * c_ref[hh] + lax.dot_general((w_state * vf).astype(BF16), kb, TN_DIMS,
                                                        preferred_element_type=F32)
        n_ref[hh] = decay * n_ref[hh] + jnp.sum(w_state * kf, axis=0, keepdims=True)
        m_ref[hh] = m_new

        hn = hm * lax.rsqrt(jnp.mean(hm * hm, axis=1, keepdims=True) + NORM_EPS) * ng_ref[:, cs]
        o_ref[:, cs] = (hn * _sigmoid(om_ref[:, cs].astype(F32))).astype(BF16)


def _mlstm(proj, if_arr, gate_bias, conv_w, conv_b, ng, B, S):
    T = B * S
    nc = S // ML_CHUNK
    W = ML_HPS * ML_HD
    n_hp = ML_HEADS // ML_HPS
    assert COL_QKM % W == 0 and COL_VM % W == 0 and COL_OM % W == 0 and ML_HEADS % ML_HPS == 0
    cq, ck, cv, co = COL_QKM // W, COL_QKM // W + n_hp, COL_VM // W, COL_OM // W
    rows = lambda col0: pl.BlockSpec((ML_CHUNK, W), lambda b, h, c: (b * nc + c, col0 + h))
    return pl.pallas_call(
        _mlstm_kernel,
        grid=(B, n_hp, nc),
        in_specs=[
            pl.BlockSpec(memory_space=pltpu.SMEM),
            rows(cq), rows(ck), rows(cv), rows(co),
            pl.BlockSpec((1, ML_HPS, 2, ML_CHUNK), lambda b, h, c: (b, h, 0, c)),
            pl.BlockSpec((CONV_W, W), lambda b, h, c: (0, h)),
            pl.BlockSpec((CONV_W, W), lambda b, h, c: (0, n_hp + h)),
            pl.BlockSpec((1, W), lambda b, h, c: (0, h)),
            pl.BlockSpec((1, W), lambda b, h, c: (0, n_hp + h)),
            pl.BlockSpec((1, W), lambda b, h, c: (0, h)),
        ],
        out_specs=pl.BlockSpec((ML_CHUNK, W), lambda b, h, c: (b * nc + c, h)),
        out_shape=jax.ShapeDtypeStruct((T, ML_HEADS * ML_HD), BF16),
        scratch_shapes=[
            pltpu.VMEM((ML_HPS, ML_HD, ML_HD), F32), pltpu.VMEM((ML_HPS, 1, ML_HD), F32),
            pltpu.VMEM((ML_HPS, 1, 1), F32),
            pltpu.VMEM((ML_HPS, ML_CHUNK + 8, ML_HD), F32), pltpu.VMEM((ML_HPS, ML_CHUNK + 8, ML_HD), F32),
        ],
        compiler_params=_cparams(("parallel", "parallel", "arbitrary")),
        name="mlstm",
    )(gate_bias, proj, proj, proj, proj, if_arr, conv_w, conv_w, conv_b, conv_b, ng)


def _pack_bf16_pair(lo, hi):
    lo_b = pltpu.bitcast(lo.astype(BF16).astype(F32), U32)
    hi_b = pltpu.bitcast(hi.astype(BF16).astype(F32), U32)
    return (lo_b >> 16) | hi_b


def _unpack_bf16_pair(p):
    lo = pltpu.bitcast(p << 16, F32)
    hi = pltpu.bitcast(p & jnp.uint32(0xFFFF0000), F32)
    return lo, hi


def _row(ref, r):
    return ref.at[pl.ds(r, 1), :]


def _outproj_kernel(nsa_ref, ml_ref, w_ref, x_ref, g2_ref, wr_ref, br_ref, tril_ref,
                    x1_ref, xp_ref, rt_ref, cnt_ref, carry_ref):
    @pl.when(pl.program_id(0) == 0)
    def _():
        carry_ref[...] = jnp.zeros_like(carry_ref)

    half = D_MODEL // 2
    acc = jnp.dot(nsa_ref[...], w_ref[0:half, :], preferred_element_type=F32)
    acc = acc + jnp.dot(ml_ref[...], w_ref[half:, :], preferred_element_type=F32)
    x1 = x_ref[...] + acc
    x1_ref[...] = x1
    xn = x1 * lax.rsqrt(jnp.mean(x1 * x1, axis=-1, keepdims=True) + NORM_EPS) * g2_ref[...]
    xp_ref[...] = _pack_bf16_pair(xn[:, :half], xn[:, half:])
    logits = jnp.dot(xn.astype(BF16), wr_ref[...], preferred_element_type=F32) + br_ref[...]

    tm = logits.shape[0]
    lane = lax.broadcasted_iota(jnp.int32, (tm, LANE), 1)
    lane_f = lane.astype(F32)
    big = float(LANE)
    gmask = lane < MOE_GROUPS
    gmax = jnp.max(jnp.where(gmask, logits, NEG_INF), axis=1, keepdims=True)
    ge = jnp.where(gmask, jnp.exp(logits - gmax), 0.0)
    gp = ge / jnp.sum(ge, axis=1, keepdims=True)
    g_w = jnp.max(gp, axis=1, keepdims=True)
    g_idx = jnp.min(jnp.where(gmask, jnp.where(gp == g_w, lane_f, big), big), axis=1, keepdims=True)
    grp_of_lane = ((lane - MOE_GROUPS) >> EPG_LOG2).astype(F32)
    emask = jnp.where(lane >= MOE_GROUPS, grp_of_lane, -1.0) == g_idx
    emax = jnp.max(jnp.where(emask, logits, NEG_INF), axis=1, keepdims=True)
    ee = jnp.where(emask, jnp.exp(logits - emax), 0.0)
    ep = jnp.where(emask, ee / jnp.sum(ee, axis=1, keepdims=True), -1.0)
    v1 = jnp.max(ep, axis=1, keepdims=True)
    i1 = jnp.min(jnp.where(ep == v1, lane_f, big), axis=1, keepdims=True)
    ep2 = jnp.where(lane_f == i1, -1.0, ep)
    v2 = jnp.max(ep2, axis=1, keepdims=True)
    i2 = jnp.min(jnp.where(ep2 == v2, lane_f, big), axis=1, keepdims=True)
    w0 = g_w * v1 / (v1 + v2)
    w1 = g_w * v2 / (v1 + v2)
    e0 = i1 - float(MOE_GROUPS)
    e1 = i2 - float(MOE_GROUPS)

    oh0 = jnp.where(lane_f == e0, 1.0, 0.0)
    oh1 = jnp.where(lane_f == e1, 1.0, 0.0)
    pre0 = jnp.dot(tril_ref[...], oh0.astype(BF16), preferred_element_type=F32)
    pre1 = jnp.dot(tril_ref[...], oh1.astype(BF16), preferred_element_type=F32)
    carry = carry_ref[...]
    tot0 = pre0[tm - 1:tm, :]
    tot1 = pre1[tm - 1:tm, :]
    rank0 = jnp.sum(oh0 * (pre0 - 1.0 + carry), axis=1, keepdims=True)
    rank1 = jnp.sum(oh1 * (pre1 - 1.0 + carry + tot0), axis=1, keepdims=True)
    new_carry = carry + tot0 + tot1
    carry_ref[...] = new_carry
    cnt_ref[...] = jnp.broadcast_to(new_carry, cnt_ref.shape)

    rt = jnp.where(lane == 0, e0, jnp.where(lane == 1, e1, jnp.where(lane == 2, w0, jnp.where(
        lane == 3, w1, jnp.where(lane == 4, rank0, jnp.where(lane == 5, rank1, 0.0))))))
    rt_ref[...] = rt


def _out_proj(nsa_o, ml_o, w_out, x2, g2, w_r, b_r, tril):
    T = x2.shape[0]
    half = D_MODEL // 2
    return pl.pallas_call(
        _outproj_kernel,
        grid=(T // TM_OUT,),
        in_specs=[
            pl.BlockSpec((TM_OUT, half), lambda i: (i, 0)),
            pl.BlockSpec((TM_OUT, half), lambda i: (i, 0)),
            pl.BlockSpec((D_MODEL, D_MODEL), lambda i: (0, 0)),
            pl.BlockSpec((TM_OUT, D_MODEL), lambda i: (i, 0)),
            pl.BlockSpec((1, D_MODEL), lambda i: (0, 0)),
            pl.BlockSpec((D_MODEL, LANE), lambda i: (0, 0)),
            pl.BlockSpec((1, LANE), lambda i: (0, 0)),
            pl.BlockSpec((TM_OUT, TM_OUT), lambda i: (0, 0)),
        ],
        out_specs=[
            pl.BlockSpec((TM_OUT, D_MODEL), lambda i: (i, 0)),
            pl.BlockSpec((TM_OUT, half), lambda i: (i, 0)),
            pl.BlockSpec((TM_OUT, LANE), lambda i: (i, 0)),
            pl.BlockSpec((8, LANE), lambda i: (0, 0)),
        ],
        out_shape=[
            jax.ShapeDtypeStruct((T, D_MODEL), F32),
            jax.ShapeDtypeStruct((T, half), U32),
            jax.ShapeDtypeStruct((T, LANE), F32),
            jax.ShapeDtypeStruct((8, LANE), F32),
        ],
        scratch_shapes=[pltpu.VMEM((1, LANE), F32)],
        compiler_params=_cparams(("arbitrary",)),
        name="out_proj",
    )(nsa_o, ml_o, w_out, x2, g2, w_r, b_r, tril)


TD = 256
DISPATCH_BUFS = 3
DISPATCH_UNROLL = 8


def _dispatch_kernel(dest_ref, zblk_ref, meta_ref, xp_hbm, xs_hbm, zbuf_ref, zsem_ref, tbuf_ref, lsem_ref,
                     rsem_ref):
    n_assign = dest_ref.shape[0]
    nb = xs_hbm.shape[0] // BM
    n_used = meta_ref[0]
    zbuf_ref[...] = jnp.zeros_like(zbuf_ref)

    def zero_block(blk):
        return pltpu.make_async_copy(zbuf_ref, xs_hbm.at[pl.ds(blk * BM, BM), :], zsem_ref.at[0])

    def for_zero_blocks(fn):
        for e in range(N_EXPERTS):
            @pl.when(zblk_ref[e] >= 0)
            def _(e=e):
                fn(zero_block(zblk_ref[e]))

        def body(blk, _):
            fn(zero_block(blk))
            return 0
        lax.fori_loop(n_used, nb, body, 0)

    for_zero_blocks(lambda c: c.start())
    for_zero_blocks(lambda c: c.wait())

    n_tiles = n_assign // (2 * TD)

    def load(tile, slot):
        return pltpu.make_async_copy(xp_hbm.at[pl.ds(tile * TD, TD), :], tbuf_ref.at[slot], lsem_ref.at[slot])

    def wait_scatter(slot):
        for _ in range(2):
            pltpu.make_async_copy(tbuf_ref.at[slot], tbuf_ref.at[slot], rsem_ref.at[slot]).wait()

    load(0, 0).start()

    def tile_step(i, _):
        slot = i % DISPATCH_BUFS
        nslot = (i + 1) % DISPATCH_BUFS

        @pl.when(i + 1 < n_tiles)
        def _():
            @pl.when(i + 1 >= DISPATCH_BUFS)
            def _():
                wait_scatter(nslot)
            load(i + 1, nslot).start()

        load(i, slot).wait()

        def body(j, _):
            for u in range(DISPATCH_UNROLL):
                r = j * DISPATCH_UNROLL + u
                for k in range(2):
                    d = dest_ref[(i * TD + r) * 2 + k]
                    pltpu.make_async_copy(_row(tbuf_ref.at[slot], r), _row(xs_hbm, d),
                                          rsem_ref.at[slot]).start(priority=k)
            return 0
        lax.fori_loop(0, TD // DISPATCH_UNROLL, body, 0)
        return 0

    lax.fori_loop(0, n_tiles, tile_step, 0)
    for back in range(min(DISPATCH_BUFS, n_tiles)):
        wait_scatter((n_tiles - 1 - back) % DISPATCH_BUFS)


def _dispatch(dest, zblk, meta, xp, n_rows):
    half = D_MODEL // 2
    assert dest.shape[0] % (2 * TD) == 0 and dest.shape[0] // (2 * TD) >= DISPATCH_BUFS
    grid_spec = pltpu.PrefetchScalarGridSpec(
        num_scalar_prefetch=3,
        grid=(1,),
        in_specs=[pl.BlockSpec(memory_space=pl.ANY)],
        out_specs=pl.BlockSpec(memory_space=pl.ANY),
        scratch_shapes=[
            pltpu.VMEM((BM, half), U32),
            pltpu.SemaphoreType.DMA((1,)),
            pltpu.VMEM((DISPATCH_BUFS, TD, half), U32),
            pltpu.SemaphoreType.DMA((DISPATCH_BUFS,)),
            pltpu.SemaphoreType.DMA((DISPATCH_BUFS,)),
        ],
    )
    return pl.pallas_call(
        _dispatch_kernel,
        grid_spec=grid_spec,
        out_shape=jax.ShapeDtypeStruct((n_rows, half), U32),
        compiler_params=_cparams(("arbitrary",)),
        name="dispatch",
    )(dest, zblk, meta, xp)


def _expert_kernel(be_ref, nxt_ref, meta_ref, x_ref, wg_hbm, wu_hbm, wd_hbm, y_ref,
                   wsg_ref, wsu_ref, wsd_ref, wsem_ref, wgb_ref, wub_ref, wdb_ref):
    i = pl.program_id(0)
    n_used = meta_ref[0]
    half = D_MODEL // 2

    def weight_copies(e):
        return (pltpu.make_async_copy(wg_hbm.at[e], wsg_ref, wsem_ref.at[0]),
                pltpu.make_async_copy(wu_hbm.at[e], wsu_ref, wsem_ref.at[1]),
                pltpu.make_async_copy(wd_hbm.at[e], wsd_ref, wsem_ref.at[2]))

    @pl.when(i == 0)
    def _():
        for c in weight_copies(be_ref[0]):
            c.start()

    @pl.when(i < n_used)
    def _():
        e = be_ref[i]

        @pl.when((i == 0) | (e != be_ref[jnp.maximum(i - 1, 0)]))
        def _():
            for c in weight_copies(e):
                c.wait()
            wgb_ref[...] = wsg_ref[...].astype(BF16)
            wub_ref[...] = wsu_ref[...].astype(BF16)
            wdb_ref[...] = wsd_ref[...].astype(BF16)
            nx = nxt_ref[e]

            @pl.when(nx >= 0)
            def _():
                for c in weight_copies(nx):
                    c.start()

        lo, hi = _unpack_bf16_pair(x_ref[...])
        xl = lo.astype(BF16)
        xh = hi.astype(BF16)
        gt = (jnp.dot(xl, wgb_ref[0:half, :], preferred_element_type=F32)
              + jnp.dot(xh, wgb_ref[half:, :], preferred_element_type=F32))
        up = (jnp.dot(xl, wub_ref[0:half, :], preferred_element_type=F32)
              + jnp.dot(xh, wub_ref[half:, :], preferred_element_type=F32))
        hb = (_silu(gt) * up).astype(BF16)
        y = jnp.dot(hb, wdb_ref[...], preferred_element_type=F32)
        y_ref[...] = _pack_bf16_pair(y[:, :half], y[:, half:])

    @pl.when(i >= n_used)
    def _():
        y_ref[...] = jnp.zeros_like(y_ref)


def _experts(block_expert, next_expert, meta, xs, w_gate, w_up, w_down):
    half = D_MODEL // 2
    n_rows = xs.shape[0]
    nb = n_rows // BM
    grid_spec = pltpu.PrefetchScalarGridSpec(
        num_scalar_prefetch=3,
        grid=(nb,),
        in_specs=[
            pl.BlockSpec((BM, half), lambda i, be, nxt, meta: (jnp.minimum(i, meta[0] - 1), 0)),
            pl.BlockSpec(memory_space=pl.ANY), pl.BlockSpec(memory_space=pl.ANY),
            pl.BlockSpec(memory_space=pl.ANY),
        ],
        out_specs=pl.BlockSpec((BM, half), lambda i, be, nxt, meta: (i, 0)),
        scratch_shapes=[
            pltpu.VMEM((D_MODEL, D_EXPERT), F32),
            pltpu.VMEM((D_MODEL, D_EXPERT), F32),
            pltpu.VMEM((D_EXPERT, D_MODEL), F32),
            pltpu.SemaphoreType.DMA((3,)),
            pltpu.VMEM((D_MODEL, D_EXPERT), BF16),
            pltpu.VMEM((D_MODEL, D_EXPERT), BF16),
            pltpu.VMEM((D_EXPERT, D_MODEL), BF16),
        ],
    )
    return pl.pallas_call(
        _expert_kernel,
        grid_spec=grid_spec,
        out_shape=jax.ShapeDtypeStruct((n_rows, half), U32),
        compiler_params=_cparams(("arbitrary",)),
        name="experts",
    )(block_expert, next_expert, meta, xs, w_gate, w_up, w_down)


def _combine_kernel(dest_ref, y_hbm, x1_ref, rt_ref, fg_ref, o_ref, ya_ref, yb_ref, sem_ref):
    i = pl.program_id(0)
    nt = pl.num_programs(0)
    half = D_MODEL // 2
    bufs = (ya_ref, yb_ref)

    def row_copy(tile, slot, r_tile, s, k):
        r = r_tile * 8 + s
        d = dest_ref[(tile * TC + r) * 2 + k]
        return pltpu.make_async_copy(_row(y_hbm, d), _row(bufs[slot].at[k], r), sem_ref.at[slot])

    def wait_rows(slot):
        pltpu.make_async_copy(bufs[slot], bufs[slot], sem_ref.at[slot]).wait()

    def step(slot):
        wait_rows(slot)
        for r in range(TC):
            for k in range(2):
                row_copy(i + 1, 1 - slot, r // 8, r % 8, k).start(priority=k)
        rt = rt_ref[...]
        w0 = rt[:, 2:3]
        w1 = rt[:, 3:4]
        lo0, hi0 = _unpack_bf16_pair(bufs[slot][0])
        lo1, hi1 = _unpack_bf16_pair(bufs[slot][1])
        xl = x1_ref[:, :half] + (w0 * lo0 + w1 * lo1)
        xh = x1_ref[:, half:] + (w0 * hi0 + w1 * hi1)
        ms = (jnp.sum(xl * xl, axis=1, keepdims=True) + jnp.sum(xh * xh, axis=1, keepdims=True)) / float(D_MODEL)
        inv = lax.rsqrt(ms + NORM_EPS)
        o_ref[:, :half] = xl * inv * fg_ref[:, :half]
        o_ref[:, half:] = xh * inv * fg_ref[:, half:]

        @pl.when(i == nt - 1)
        def _():
            wait_rows(1 - slot)

    @pl.when(i == 0)
    def _():
        def body(rt, _):
            for s in range(8):
                for k in range(2):
                    row_copy(0, 0, rt, s, k).start(priority=k)
            return 0
        lax.fori_loop(0, TC // 8, body, 0)

    @pl.when(i % 2 == 0)
    def _():
        step(0)

    @pl.when(i % 2 == 1)
    def _():
        step(1)


def _combine(dest, ys, x1, route, fg):
    T = x1.shape[0]
    half = D_MODEL // 2
    grid_spec = pltpu.PrefetchScalarGridSpec(
        num_scalar_prefetch=1,
        grid=(T // TC,),
        in_specs=[
            pl.BlockSpec(memory_space=pl.ANY),
            pl.BlockSpec((TC, D_MODEL), lambda i, d: (i, 0)),
            pl.BlockSpec((TC, LANE), lambda i, d: (i, 0)),
            pl.BlockSpec((1, D_MODEL), lambda i, d: (0, 0)),
        ],
        out_specs=pl.BlockSpec((TC, D_MODEL), lambda i, d: (i, 0)),
        scratch_shapes=[
            pltpu.VMEM((2, TC, half), U32),
            pltpu.VMEM((2, TC, half), U32),
            pltpu.SemaphoreType.DMA((2,)),
        ],
    )
    return pl.pallas_call(
        _combine_kernel,
        grid_spec=grid_spec,
        out_shape=jax.ShapeDtypeStruct((T, D_MODEL), F32),
        compiler_params=_cparams(("arbitrary",)),
        name="combine",
    )(dest, ys, x1, route, fg)


def _cover_matrix(S):
    n_blk = S // CMP_STRIDE
    n_sel = S // SEL_BLOCK
    cs = np.arange(n_blk) * CMP_STRIDE
    ss = np.arange(n_sel) * SEL_BLOCK
    shared = np.minimum(cs[:, None] + CMP_LEN, ss[None, :] + SEL_BLOCK) - np.maximum(cs[:, None], ss[None, :])
    return (np.clip(shared, 0, None) / CMP_LEN).T.astype(np.float32)


def _block_mask_matrix(S):
    n_sel = S // SEL_BLOCK
    assert n_sel <= LANE
    e = np.zeros((S, LANE), np.float32)
    e[np.arange(S), np.arange(S) // SEL_BLOCK] = NEG_INF
    return e


def _inv_freq_row():
    inv = np.power(np.float32(ROPE_THETA), -np.arange(ROPE_HALF, dtype=np.float32) * 2.0 / ROPE_DIM)
    row = np.zeros((1, LANE), np.float32)
    row[0, :ROPE_HALF] = inv
    row[0, ROPE_HALF:ROPE_DIM] = inv
    return row


def _layer(x2, positions, B, S, norm1_g, w_in, cmp_pe_k, cmp_pe_v, cmp_wk1, cmp_wk2, cmp_wv1, cmp_wv2,
           nsa_norm_g, conv_w, conv_b, b_igate, b_fgate, mlstm_norm_g, w_out, norm2_g,
           w_group, b_group, w_router, b_router, w_exp_gate, w_exp_up, w_exp_down, out_norm_g):
    T = B * S
    o_q, o_kv, o_g, o_qk, o_v, o_o, o_i, o_f = 0, 1024, 2560, 2584, 4632, 5656, 6680, 6684
    w_main = jnp.concatenate([w_in[:, o_q:o_g], w_in[:, o_qk:o_i]], axis=1).astype(BF16)
    w_small = jnp.concatenate([w_in[:, o_g:o_qk], w_in[:, o_i:], jnp.zeros((D_MODEL, LANE - 32), F32)],
                              axis=1).astype(BF16)
    posb = jnp.broadcast_to(positions.reshape(T, 1).astype(F32), (T, LANE))
    invf = jnp.asarray(_inv_freq_row())

    proj, small = _in_proj(x2, norm1_g.reshape(1, -1), w_main, w_small)
    q_r, k_r = _rope(posb, invf, proj)
    kc, vc = _compress(k_r, proj, B, S, cmp_pe_k, cmp_pe_v, cmp_wk1.astype(BF16), cmp_wk2.astype(BF16),
                       cmp_wv1.astype(BF16), cmp_wv2.astype(BF16))
    eye = np.eye(S // SEL_BLOCK, LANE, dtype=np.float32)
    nsa_o = _nsa(q_r, k_r, proj, small, kc, vc, jnp.asarray(_cover_matrix(S), BF16),
                 jnp.asarray(_block_mask_matrix(S), BF16), jnp.asarray(eye, BF16),
                 nsa_norm_g.reshape(1, -1), B, S)

    if_arr = small[:, 24:32].reshape(B, S, 2, ML_HEADS).transpose(0, 3, 2, 1)
    gate_bias = jnp.stack([b_igate, b_fgate]).astype(F32)
    ml_o = _mlstm(proj, if_arr, gate_bias, conv_w, conv_b.reshape(1, -1), mlstm_norm_g.reshape(1, -1), B, S)

    w_r = jnp.concatenate([w_group, w_router, jnp.zeros((D_MODEL, LANE - MOE_GROUPS - N_EXPERTS), F32)],
                          axis=1).astype(BF16)
    b_r = jnp.concatenate([b_group, b_router, jnp.zeros((LANE - MOE_GROUPS - N_EXPERTS,), F32)]).reshape(1, LANE)
    tril = jnp.asarray(np.tril(np.ones((TM_OUT, TM_OUT), np.float32)), BF16)
    x1, xp, route, cnt = _out_proj(nsa_o, ml_o, w_out.astype(BF16), x2, norm2_g.reshape(1, -1), w_r, b_r, tril)

    n_rows = T * 2 + N_EXPERTS * BM
    counts = cnt[0, :N_EXPERTS].astype(jnp.int32)
    pcounts = (counts + BM - 1) // BM * BM
    pends = jnp.cumsum(pcounts)
    pstarts = pends - pcounts
    eid = route[:, 0:2].astype(jnp.int32)
    onehot = (eid[..., None] == jnp.arange(N_EXPERTS, dtype=jnp.int32)).astype(F32)
    start_blk = jnp.einsum('tke,e->tk', onehot, (pstarts // BM).astype(F32)).astype(jnp.int32)
    dest = (start_blk * BM + route[:, 4:6].astype(jnp.int32)).reshape(T * 2)
    block_expert = jnp.minimum(jnp.searchsorted(pends, jnp.arange(n_rows // BM) * BM, side='right'),
                               N_EXPERTS - 1).astype(jnp.int32)
    present = jnp.where(counts > 0, jnp.arange(N_EXPERTS, dtype=jnp.int32), N_EXPERTS)
    later = jnp.concatenate([lax.cummin(present[::-1])[::-1][1:], jnp.full((1,), N_EXPERTS, jnp.int32)])
    next_expert = jnp.where(later < N_EXPERTS, later, -1).astype(jnp.int32)
    meta = (pends[-1:] // BM).astype(jnp.int32)

    last_block = jnp.where(counts > 0, pends // BM - 1, -1).astype(jnp.int32)

    xs = _dispatch(dest, last_block, meta, xp, n_rows)
    ys = _experts(block_expert, next_expert, meta, xs, w_exp_gate, w_exp_up, w_exp_down)
    dest_pad = jnp.concatenate([dest, jnp.zeros((2 * TC,), jnp.int32)])
    return _combine(dest_pad, ys, x1, route, out_norm_g.reshape(1, -1))


def kernel(x, positions, norm1_g, w_in, cmp_pe_k, cmp_pe_v, cmp_wk1, cmp_wk2, cmp_wv1, cmp_wv2, nsa_norm_g,
           conv_w, conv_b, b_igate, b_fgate, mlstm_norm_g, w_out, norm2_g, w_group, b_group, w_router,
           b_router, w_exp_gate, w_exp_up, w_exp_down, final_norm_g):
    B, S, D = x.shape
    assert D == D_MODEL and norm1_g.shape[0] == 1, "single-layer, D_MODEL-wide configuration only"
    assert S % ML_CHUNK == 0 and S % SEL_BUCKET == 0 and S >= WIN_KEYS and (B * S) % TM_IN == 0
    out = _layer(x.reshape(B * S, D), positions, B, S, norm1_g[0], w_in[0], cmp_pe_k[0], cmp_pe_v[0],
                 cmp_wk1[0], cmp_wk2[0], cmp_wv1[0], cmp_wv2[0], nsa_norm_g[0], conv_w[0], conv_b[0],
                 b_igate[0], b_fgate[0], mlstm_norm_g[0], w_out[0], norm2_g[0], w_group[0], b_group[0],
                 w_router[0], b_router[0], w_exp_gate[0], w_exp_up[0], w_exp_down[0], final_norm_g)
    return out.reshape(B, S, D)
```

```python
import functools

import numpy as np
import jax
import jax.numpy as jnp
from jax import lax
from jax.experimental import pallas as pl
from jax.experimental.pallas import tpu as pltpu

F32 = jnp.float32
BF16 = jnp.bfloat16
U32 = jnp.uint32

D_MODEL = 2048
NSA_HEADS = 8
NSA_GROUPS = 2
NSA_J = NSA_HEADS // NSA_GROUPS
HD = 128
CMP_LEN = 32
CMP_STRIDE = 16
SEL_BLOCK = 64
SEL_TOPK = 8
WINDOW = 512
ROPE_THETA = 500000.0
ROPE_DIM = 32
ROPE_HALF = 16
ML_HEADS = 4
ML_HD = 256
CONV_W = 4
MOE_GROUPS = 8
EPG = 8
N_EXPERTS = 64
D_EXPERT = 512
SEL_BLOCK_LOG2 = 6
EPG_LOG2 = 3
assert (1 << SEL_BLOCK_LOG2) == SEL_BLOCK and (1 << EPG_LOG2) == EPG
NORM_EPS = 1e-6
NEG_INF = -1e30
FORCE_SCORE = 1000.0

COL_Q = 0
COL_KV = 1024
COL_QKM = 2560
COL_VM = 4608
COL_OM = 5632
N_MAIN = 6656

LANE = 128
VMEM_LIMIT = 56 * 1024 * 1024

TM_IN = 1024
TN_IN = 1664
TS_ROPE = 256
TQ = 128
SEL_BUCKET = 256
WIN_KEYS = WINDOW + TQ
ML_CHUNK = 256
TM_OUT = 512
BM = 256
TC = 256

NT_DIMS = (((1,), (1,)), ((), ()))
TN_DIMS = (((0,), (0,)), ((), ()))


def _cparams(sem):
    return pltpu.CompilerParams(dimension_semantics=sem, vmem_limit_bytes=VMEM_LIMIT)


def _sigmoid(x):
    return 0.5 * jnp.tanh(0.5 * x) + 0.5


def _silu(x):
    h = 0.5 * x
    return h + h * jnp.tanh(h)


def _inproj_kernel(x_ref, g_ref, w_ref, ws_ref, o_ref, os_ref, h_ref):
    @pl.when(pl.program_id(1) == 0)
    def _():
        x = x_ref[...]
        ms = jnp.mean(x * x, axis=-1, keepdims=True)
        h_ref[...] = (x * lax.rsqrt(ms + NORM_EPS) * g_ref[...]).astype(BF16)
        os_ref[...] = jnp.dot(h_ref[...], ws_ref[...], preferred_element_type=F32)

    o_ref[...] = jnp.dot(h_ref[...], w_ref[...], preferred_element_type=F32).astype(o_ref.dtype)


def _in_proj(x2, g1, w_main, w_small):
    T = x2.shape[0]
    return pl.pallas_call(
        _inproj_kernel,
        grid=(T // TM_IN, N_MAIN // TN_IN),
        in_specs=[
            pl.BlockSpec((TM_IN, D_MODEL), lambda m, n: (m, 0)),
            pl.BlockSpec((1, D_MODEL), lambda m, n: (0, 0)),
            pl.BlockSpec((D_MODEL, TN_IN), lambda m, n: (0, n)),
            pl.BlockSpec((D_MODEL, LANE), lambda m, n: (0, 0)),
        ],
        out_specs=[
            pl.BlockSpec((TM_IN, TN_IN), lambda m, n: (m, n)),
            pl.BlockSpec((TM_IN, LANE), lambda m, n: (m, 0)),
        ],
        out_shape=[
            jax.ShapeDtypeStruct((T, N_MAIN), BF16),
            jax.ShapeDtypeStruct((T, LANE), F32),
        ],
        scratch_shapes=[pltpu.VMEM((TM_IN, D_MODEL), BF16)],
        compiler_params=_cparams(("parallel", "arbitrary")),
        name="in_proj",
    )(x2, g1, w_main, w_small)


def _rope_kernel(pos_ref, invf_ref, q_ref, kc_ref, ks_ref, kw_ref, qo_ref, ko_ref):
    ang = pos_ref[...] * invf_ref[...]
    c = jnp.cos(ang)
    s = jnp.sin(ang)
    lane = lax.broadcasted_iota(jnp.int32, ang.shape, 1)
    sa = jnp.where(lane < ROPE_HALF, -s, 0.0)
    sb = jnp.where(lane < ROPE_HALF, 0.0, s)

    def rope(x):
        return x * c + pltpu.roll(x, LANE - ROPE_HALF, 1) * sa + pltpu.roll(x, ROPE_HALF, 1) * sb

    scale = HD ** -0.5
    for h in range(NSA_HEADS):
        sl = slice(h * HD, (h + 1) * HD)
        qo_ref[:, sl] = (rope(q_ref[:, sl].astype(F32)) * scale).astype(BF16)
    for i, r in enumerate((kc_ref, ks_ref, kw_ref)):
        for g in range(NSA_GROUPS):
            sl = slice(g * HD, (g + 1) * HD)
            so = slice(i * 2 * HD + g * HD, i * 2 * HD + (g + 1) * HD)
            ko_ref[:, so] = rope(r[:, sl].astype(F32)).astype(BF16)


def _rope(posb, invf, proj):
    T = proj.shape[0]
    kvb = COL_KV // 256
    return pl.pallas_call(
        _rope_kernel,
        grid=(T // TS_ROPE,),
        in_specs=[
            pl.BlockSpec((TS_ROPE, LANE), lambda i: (i, 0)),
            pl.BlockSpec((1, LANE), lambda i: (0, 0)),
            pl.BlockSpec((TS_ROPE, 1024), lambda i: (i, 0)),
            pl.BlockSpec((TS_ROPE, 256), lambda i: (i, kvb + 0)),
            pl.BlockSpec((TS_ROPE, 256), lambda i: (i, kvb + 2)),
            pl.BlockSpec((TS_ROPE, 256), lambda i: (i, kvb + 4)),
        ],
        out_specs=[
            pl.BlockSpec((TS_ROPE, 1024), lambda i: (i, 0)),
            pl.BlockSpec((TS_ROPE, 768), lambda i: (i, 0)),
        ],
        out_shape=[
            jax.ShapeDtypeStruct((T, 1024), BF16),
            jax.ShapeDtypeStruct((T, 768), BF16),
        ],
        compiler_params=_cparams(("parallel",)),
        name="rope",
    )(posb, invf, proj, proj, proj, proj)


def _compress_kernel(k_ref, v_ref, pek_ref, pev_ref, w1k_ref, w2k_ref, w1v_ref, w2v_ref,
                     kc_ref, vc_ref, xs_ref):
    S = k_ref.shape[0]
    n_blk = S // CMP_STRIDE
    for src, pe, w1, w2, dst in ((k_ref, pek_ref, w1k_ref, w2k_ref, kc_ref),
                                 (v_ref, pev_ref, w1v_ref, w2v_ref, vc_ref)):
        xs_ref[0:S, :] = src[...].astype(F32)
        xs_ref[S:S + CMP_LEN, :] = jnp.zeros((CMP_LEN, HD), F32)
        acc = jnp.zeros((n_blk, HD), F32)
        for l in range(CMP_LEN):
            a = xs_ref[pl.ds(l, n_blk, stride=CMP_STRIDE), :] + pe[l:l + 1, :]
            acc = acc + jnp.dot(a.astype(BF16), w1[l * HD:(l + 1) * HD, :],
                                preferred_element_type=F32)
        hid = _silu(acc)
        out = jnp.dot(hid.astype(BF16), w2[...], preferred_element_type=F32)
        dst[0, 0] = out.astype(BF16)


def _compress(k_r, proj, B, S, pek, pev, w1k, w2k, w1v, w2v):
    n_blk = S // CMP_STRIDE
    vcol = (COL_KV + 256) // HD
    full = lambda shape: pl.BlockSpec(shape, lambda b, g: tuple(0 for _ in shape))
    return pl.pallas_call(
        _compress_kernel,
        grid=(B, NSA_GROUPS),
        in_specs=[
            pl.BlockSpec((S, HD), lambda b, g: (b, g)),
            pl.BlockSpec((S, HD), lambda b, g: (b, vcol + g)),
            full((CMP_LEN, HD)), full((CMP_LEN, HD)),
            full((CMP_LEN * HD, HD)), full((HD, HD)),
            full((CMP_LEN * HD, HD)), full((HD, HD)),
        ],
        out_specs=[
            pl.BlockSpec((1, 1, n_blk, HD), lambda b, g: (b, g, 0, 0)),
            pl.BlockSpec((1, 1, n_blk, HD), lambda b, g: (b, g, 0, 0)),
        ],
        out_shape=[
            jax.ShapeDtypeStruct((B, NSA_GROUPS, n_blk, HD), BF16),
            jax.ShapeDtypeStruct((B, NSA_GROUPS, n_blk, HD), BF16),
        ],
        scratch_shapes=[pltpu.VMEM((S + CMP_LEN, HD), F32)],
        compiler_params=_cparams(("parallel", "parallel")),
        name="compress",
    )(k_r, proj, pek, pev, w1k, w2k, w1v, w2v)


def _nsa_kernel(q_ref, kc_ref, vc_ref, ks_ref, vs_ref, kw_ref, vw_ref, sm_ref, covt_ref, eneg_ref, eye_ref,
                ng_ref, o_ref, obuf_ref, owin_ref, osel_ref):
    qi = pl.program_id(1)
    q0 = qi * TQ
    R = NSA_J * TQ
    n_cmp = kc_ref.shape[2]
    n_sel = ks_ref.shape[0] // SEL_BLOCK
    n_win = WIN_KEYS // TQ

    def row_t(shape):
        r = lax.broadcasted_iota(jnp.int32, shape, 0)
        return q0 + (r & (TQ - 1))

    qgs = [jnp.concatenate([q_ref[:, (g * NSA_J + j) * HD:(g * NSA_J + j + 1) * HD]
                            for j in range(NSA_J)], axis=0) for g in range(NSA_GROUPS)]

    def with_ones(v):
        lane = lax.broadcasted_iota(jnp.int32, v.shape, 1)
        return jnp.concatenate([v, jnp.where(lane == 0, 1.0, 0.0).astype(BF16)], axis=1)

    def window(masks):
        w0 = pl.multiple_of(jnp.maximum(q0 - WINDOW, 0), TQ)
        for g in range(NSA_GROUPS):
            kt = kw_ref[pl.ds(w0, WIN_KEYS), g * HD:(g + 1) * HD]
            vt = with_ones(vw_ref[pl.ds(w0, WIN_KEYS), g * HD:(g + 1) * HD])
            sc = masks(lax.dot_general(qgs[g], kt, NT_DIMS, preferred_element_type=F32), w0)
            mw = jnp.max(sc, axis=1, keepdims=True)
            pw = jnp.exp((sc - mw).astype(BF16))
            acc = jnp.dot(pw, vt, preferred_element_type=F32)
            owin_ref[g] = acc[:, 0:HD] / acc[:, HD:HD + 1]

    def band_masks(sc, w0):
        d = (lax.broadcasted_iota(jnp.int32, (R, TQ), 1)
             - (lax.broadcasted_iota(jnp.int32, (R, TQ), 0) & (TQ - 1)))
        first = jnp.where(d > 0, sc[:, 0:TQ], NEG_INF)
        last = jnp.where(d <= 0, sc[:, (n_win - 1) * TQ:], NEG_INF)
        return jnp.concatenate([first, sc[:, TQ:(n_win - 1) * TQ], last], axis=1)

    def general_masks(sc, w0):
        diff = row_t((R, WIN_KEYS)) - (w0 + lax.broadcasted_iota(jnp.int32, (R, WIN_KEYS), 1))
        keep = jnp.where(diff >= 0, diff, WINDOW) < WINDOW
        return jnp.where(keep, sc, NEG_INF)

    window(general_masks)

    gates = _sigmoid(sm_ref[...])
    o_cmps = []
    qps = []
    for g in range(NSA_GROUPS):
        qg = qgs[g]

        s = lax.dot_general(qg, kc_ref[0, g], NT_DIMS, preferred_element_type=F32)
        n_lane = lax.broadcasted_iota(jnp.int32, (R, n_cmp), 1)
        cmask = (n_lane * CMP_STRIDE + (CMP_LEN - 1)) <= row_t((R, n_cmp))
        s = jnp.where(cmask, s, NEG_INF)
        m = jnp.max(s, axis=1, keepdims=True)
        e = jnp.where(cmask, jnp.exp(s - m), 0.0)
        l = jnp.sum(e, axis=1, keepdims=True)
        p = (e / jnp.where(l > 0.0, l, 1.0)).astype(BF16)
        o_cmps.append(jnp.dot(p, vc_ref[0, g], preferred_element_type=F32))
        impr = lax.dot_general(covt_ref[...], p, NT_DIMS, preferred_element_type=F32)
        imp = impr[:, 0:TQ]
        for j in range(1, NSA_J):
            imp = imp + impr[:, j * TQ:(j + 1) * TQ]

        m_sub = lax.broadcasted_iota(jnp.int32, (n_sel, TQ), 0)
        jt = (q0 + lax.broadcasted_iota(jnp.int32, (n_sel, TQ), 1)) >> SEL_BLOCK_LOG2
        forced = jnp.where(m_sub == 0, FORCE_SCORE,
                           jnp.where(m_sub == jt, FORCE_SCORE,
                                     jnp.where(m_sub == jt - 1, FORCE_SCORE, 0.0)))
        score = jnp.where(m_sub <= jt, imp + forced, -1.0)
        rank = jnp.zeros((n_sel, TQ), F32)
        for mp in range(n_sel):
            row = score[mp:mp + 1, :]
            ge = jnp.where(row >= score, 1.0, 0.0)
            gt = jnp.where(row > score, 1.0, 0.0)
            rank = rank + jnp.where(m_sub > mp, ge, gt)
        notsel_t = jnp.where(rank < float(min(SEL_TOPK, n_sel)), 0.0, 1.0).astype(BF16)
        notsel = lax.dot_general(notsel_t, eye_ref[...], TN_DIMS,
                                 preferred_element_type=F32).astype(BF16)
        qps.append(jnp.concatenate([qg, jnp.concatenate([notsel] * NSA_J, axis=0)], axis=1))

    def sel_oneshot(nk):
        for g in range(NSA_GROUPS):
            kt = jnp.concatenate([ks_ref[0:nk, g * HD:(g + 1) * HD], eneg_ref[0:nk, :]], axis=1)
            vt = with_ones(vs_ref[0:nk, g * HD:(g + 1) * HD])
            sc = lax.dot_general(qps[g], kt, NT_DIMS, preferred_element_type=F32)
            klane = lax.broadcasted_iota(jnp.int32, (R, SEL_BUCKET), 1) + (nk - SEL_BUCKET)
            tail = jnp.where(klane <= row_t((R, SEL_BUCKET)), sc[:, nk - SEL_BUCKET:], NEG_INF)
            sc = tail if nk == SEL_BUCKET else jnp.concatenate([sc[:, 0:nk - SEL_BUCKET], tail], axis=1)
            ms = jnp.max(sc, axis=1, keepdims=True)
            acc = jnp.dot(jnp.exp((sc - ms).astype(BF16)), vt, preferred_element_type=F32)
            osel_ref[g] = acc[:, 0:HD] / acc[:, HD:HD + 1]

    for b in range(ks_ref.shape[0] // SEL_BUCKET):
        @pl.when(q0 // SEL_BUCKET == b)
        def _(b=b):
            sel_oneshot((b + 1) * SEL_BUCKET)

    ssq = jnp.zeros((TQ, 1), F32)
    for g in range(NSA_GROUPS):
        o_sel = osel_ref[g]
        o_cmp = o_cmps[g]
        o_win = owin_ref[g]
        for j in range(NSA_J):
            h = g * NSA_J + j
            rs = slice(j * TQ, (j + 1) * TQ)
            o = (gates[:, 3 * h:3 * h + 1] * o_cmp[rs] + gates[:, 3 * h + 1:3 * h + 2] * o_sel[rs]
                 + gates[:, 3 * h + 2:3 * h + 3] * o_win[rs])
            ssq = ssq + jnp.sum(o * o, axis=1, keepdims=True)
            obuf_ref[:, h * HD:(h + 1) * HD] = o

    inv = lax.rsqrt(ssq / float(NSA_HEADS * HD) + NORM_EPS)
    o_ref[...] = (obuf_ref[...] * inv * ng_ref[...]).astype(BF16)


def _nsa(q_r, k_r, proj, small, kc, vc, covt, eneg, eye, ng, B, S):
    T = B * S
    nq = S // TQ
    n_blk = S // CMP_STRIDE
    n_sel = S // SEL_BLOCK
    kvb = COL_KV // 256
    return pl.pallas_call(
        _nsa_kernel,
        grid=(B, nq),
        in_specs=[
            pl.BlockSpec((TQ, 1024), lambda b, i: (b * nq + i, 0)),
            pl.BlockSpec((1, NSA_GROUPS, n_blk, HD), lambda b, i: (b, 0, 0, 0)),
            pl.BlockSpec((1, NSA_GROUPS, n_blk, HD), lambda b, i: (b, 0, 0, 0)),
            pl.BlockSpec((S, 256), lambda b, i: (b, 1)),
            pl.BlockSpec((S, 256), lambda b, i: (b, kvb + 3)),
            pl.BlockSpec((S, 256), lambda b, i: (b, 2)),
            pl.BlockSpec((S, 256), lambda b, i: (b, kvb + 5)),
            pl.BlockSpec((TQ, LANE), lambda b, i: (b * nq + i, 0)),
            pl.BlockSpec((n_sel, n_blk), lambda b, i: (0, 0)),
            pl.BlockSpec((S, LANE), lambda b, i: (0, 0)),
            pl.BlockSpec((n_sel, LANE), lambda b, i: (0, 0)),
            pl.BlockSpec((1, 1024), lambda b, i: (0, 0)),
        ],
        out_specs=pl.BlockSpec((TQ, 1024), lambda b, i: (b * nq + i, 0)),
        out_shape=jax.ShapeDtypeStruct((T, 1024), BF16),
        scratch_shapes=[pltpu.VMEM((TQ, 1024), F32), pltpu.VMEM((NSA_GROUPS, NSA_J * TQ, HD), F32),
                        pltpu.VMEM((NSA_GROUPS, NSA_J * TQ, HD), F32)],
        compiler_params=_cparams(("parallel", "parallel")),
        name="nsa",
    )(q_r, kc, vc, k_r, proj, k_r, proj, small, covt, eneg, eye, ng)


ML_HPS = 2


def _mlstm_kernel(bias_ref, q_ref, k_ref, v_ref, om_ref, if_ref, cwq_ref, cwk_ref, cbq_ref, cbk_ref,
                  ng_ref, o_ref, c_ref, n_ref, m_ref, xq_ref, xk_ref):
    hp = pl.program_id(1)
    L = ML_CHUNK

    @pl.when(pl.program_id(2) == 0)
    def _():
        c_ref[...] = jnp.zeros_like(c_ref)
        n_ref[...] = jnp.zeros_like(n_ref)
        m_ref[...] = jnp.zeros_like(m_ref)
        xq_ref[:, 0:8, :] = jnp.zeros((ML_HPS, 8, ML_HD), F32)
        xk_ref[:, 0:8, :] = jnp.zeros((ML_HPS, 8, ML_HD), F32)

    def conv_silu(x_ref, hh, xb_ref, w_ref, b_ref):
        cs = slice(hh * ML_HD, (hh + 1) * ML_HD)
        xb_ref[hh, 8:8 + L, :] = x_ref[:, cs].astype(F32)
        y = b_ref[:, cs] + xb_ref[hh, 8:8 + L, :] * w_ref[CONV_W - 1:CONV_W, cs]
        for k in range(1, CONV_W):
            y = y + xb_ref[hh, 8 - k:8 - k + L, :] * w_ref[CONV_W - 1 - k:CONV_W - k, cs]
        xb_ref[hh, 0:8, :] = xb_ref[hh, L:L + 8, :]
        return _silu(y)

    r = lax.broadcasted_iota(jnp.int32, (L, L), 0)
    cidx = lax.broadcasted_iota(jnp.int32, (L, L), 1)
    tril = cidx <= r
    eye = cidx == r

    for hh in range(ML_HPS):
        h = hp * ML_HPS + hh
        cs = slice(hh * ML_HD, (hh + 1) * ML_HD)
        qf = conv_silu(q_ref, hh, xq_ref, cwq_ref, cbq_ref) * (ML_HD ** -0.5)
        kf = conv_silu(k_ref, hh, xk_ref, cwk_ref, cbk_ref)
        qb = qf.astype(BF16)
        kb = kf.astype(BF16)
        vb = v_ref[:, cs]
        vf = vb.astype(F32)

        ic = if_ref[0, hh, 0:1, :] + bias_ref[0, h]
        fp = if_ref[0, hh, 1:2, :] + bias_ref[1, h]
        fc = jnp.minimum(fp, 0.0) - jnp.log(1.0 + jnp.exp(-jnp.abs(fp)))

        fc_b = jnp.broadcast_to(fc, (L, L))
        ic_b = jnp.broadcast_to(ic, (L, L))
        b_col = jnp.sum(jnp.where(tril, fc_b, 0.0), axis=1, keepdims=True)
        fc_col = jnp.sum(jnp.where(eye, fc_b, 0.0), axis=1, keepdims=True)
        ic_col = jnp.sum(jnp.where(eye, ic_b, 0.0), axis=1, keepdims=True)
        b_row = jnp.sum(jnp.where(r <= cidx, jnp.broadcast_to(fc_col, (L, L)), 0.0),
                        axis=0, keepdims=True)
        b_last = b_col[L - 1:L, :]
        m_prev = m_ref[hh]

        d_log = jnp.where(tril, b_col - b_row + ic, NEG_INF)
        inter = b_col + m_prev
        m_t = jnp.maximum(inter, jnp.max(d_log, axis=1, keepdims=True))
        w_intra = jnp.exp(d_log - m_t)
        w_inter = jnp.exp(inter - m_t)
        qk = lax.dot_general(qb, kb, NT_DIMS, preferred_element_type=F32) * w_intra
        num = (jnp.dot(qk.astype(BF16), vb, preferred_element_type=F32)
               + w_inter * lax.dot_general(qb, c_ref[hh].astype(BF16), NT_DIMS, preferred_element_type=F32))
        den = jnp.sum(qk, axis=1, keepdims=True) + w_inter * jnp.sum(qf * n_ref[hh], axis=1, keepdims=True)
        hm = num / jnp.maximum(jnp.abs(den), jnp.exp(-m_t))

        w_log = b_last - b_col + ic_col
        m_new = jnp.maximum(b_last + m_prev, jnp.max(w_log, axis=0, keepdims=True))
        w_state = jnp.exp(w_log - m_new)
        decay = jnp.exp(b_last + m_prev - m_new)
        c_ref[hh] = decay * c_ref[hh] + lax.dot_general((w_state * vf).astype(BF16), kb, TN_DIMS,
                                                        preferred_element_type=F32)
        n_ref[hh] = decay * n_ref[hh] + jnp.sum(w_state * kf, axis=0, keepdims=True)
        m_ref[hh] = m_new

        hn = hm * lax.rsqrt(jnp.mean(hm * hm, axis=1, keepdims=True) + NORM_EPS) * ng_ref[:, cs]
        o_ref[:, cs] = (hn * _sigmoid(om_ref[:, cs].astype(F32))).astype(BF16)


def _mlstm(proj, if_arr, gate_bias, conv_w, conv_b, ng, B, S):
    T = B * S
    nc = S // ML_CHUNK
    W = ML_HPS * ML_HD
    n_hp = ML_HEADS // ML_HPS
    assert COL_QKM % W == 0 and COL_VM % W == 0 and COL_OM % W == 0 and ML_HEADS % ML_HPS == 0
    cq, ck, cv, co = COL_QKM // W, COL_QKM // W + n_hp, COL_VM // W, COL_OM // W
    rows = lambda col0: pl.BlockSpec((ML_CHUNK, W), lambda b, h, c: (b * nc + c, col0 + h))
    return pl.pallas_call(
        _mlstm_kernel,
        grid=(B, n_hp, nc),
        in_specs=[
            pl.BlockSpec(memory_space=pltpu.SMEM),
            rows(cq), rows(ck), rows(cv), rows(co),
            pl.BlockSpec((1, ML_HPS, 2, ML_CHUNK), lambda b, h, c: (b, h, 0, c)),
            pl.BlockSpec((CONV_W, W), lambda b, h, c: (0, h)),
            pl.BlockSpec((CONV_W, W), lambda b, h, c: (0, n_hp + h)),
            pl.BlockSpec((1, W), lambda b, h, c: (0, h)),
            pl.BlockSpec((1, W), lambda b, h, c: (0, n_hp + h)),
            pl.BlockSpec((1, W), lambda b, h, c: (0, h)),
        ],
        out_specs=pl.BlockSpec((ML_CHUNK, W), lambda b, h, c: (b * nc + c, h)),
        out_shape=jax.ShapeDtypeStruct((T, ML_HEADS * ML_HD), BF16),
        scratch_shapes=[
            pltpu.VMEM((ML_HPS, ML_HD, ML_HD), F32), pltpu.VMEM((ML_HPS, 1, ML_HD), F32),
            pltpu.VMEM((ML_HPS, 1, 1), F32),
            pltpu.VMEM((ML_HPS, ML_CHUNK + 8, ML_HD), F32), pltpu.VMEM((ML_HPS, ML_CHUNK + 8, ML_HD), F32),
        ],
        compiler_params=_cparams(("parallel", "parallel", "arbitrary")),
        name="mlstm",
    )(gate_bias, proj, proj, proj, proj, if_arr, conv_w, conv_w, conv_b, conv_b, ng)


def _pack_bf16_pair(lo, hi):
    lo_b = pltpu.bitcast(lo.astype(BF16).astype(F32), U32)
    hi_b = pltpu.bitcast(hi.astype(BF16).astype(F32), U32)
    return (lo_b >> 16) | hi_b


def _unpack_bf16_pair(p):
    lo = pltpu.bitcast(p << 16, F32)
    hi = pltpu.bitcast(p & jnp.uint32(0xFFFF0000), F32)
    return lo, hi


def _row(ref, r):
    return ref.at[pl.ds(r, 1), :]


def _outproj_kernel(nsa_ref, ml_ref, w_ref, x_ref, g2_ref, wr_ref, br_ref, tril_ref,
                    x1_ref, xp_ref, rt_ref, cnt_ref, carry_ref):
    @pl.when(pl.program_id(0) == 0)
    def _():
        carry_ref[...] = jnp.zeros_like(carry_ref)

    half = D_MODEL // 2
    acc = jnp.dot(nsa_ref[...], w_ref[0:half, :], preferred_element_type=F32)
    acc = acc + jnp.dot(ml_ref[...], w_ref[half:, :], preferred_element_type=F32)
    x1 = x_ref[...] + acc
    x1_ref[...] = x1
    xn = x1 * lax.rsqrt(jnp.mean(x1 * x1, axis=-1, keepdims=True) + NORM_EPS) * g2_ref[...]
    xp_ref[...] = _pack_bf16_pair(xn[:, :half], xn[:, half:])
    logits = jnp.dot(xn.astype(BF16), wr_ref[...], preferred_element_type=F32) + br_ref[...]

    tm = logits.shape[0]
    lane = lax.broadcasted_iota(jnp.int32, (tm, LANE), 1)
    lane_f = lane.astype(F32)
    big = float(LANE)
    gmask = lane < MOE_GROUPS
    gmax = jnp.max(jnp.where(gmask, logits, NEG_INF), axis=1, keepdims=True)
    ge = jnp.where(gmask, jnp.exp(logits - gmax), 0.0)
    gp = ge / jnp.sum(ge, axis=1, keepdims=True)
    g_w = jnp.max(gp, axis=1, keepdims=True)
    g_idx = jnp.min(jnp.where(gmask, jnp.where(gp == g_w, lane_f, big), big), axis=1, keepdims=True)
    grp_of_lane = ((lane - MOE_GROUPS) >> EPG_LOG2).astype(F32)
    emask = jnp.where(lane >= MOE_GROUPS, grp_of_lane, -1.0) == g_idx
    emax = jnp.max(jnp.where(emask, logits, NEG_INF), axis=1, keepdims=True)
    ee = jnp.where(emask, jnp.exp(logits - emax), 0.0)
    ep = jnp.where(emask, ee / jnp.sum(ee, axis=1, keepdims=True), -1.0)
    v1 = jnp.max(ep, axis=1, keepdims=True)
    i1 = jnp.min(jnp.where(ep == v1, lane_f, big), axis=1, keepdims=True)
    ep2 = jnp.where(lane_f == i1, -1.0, ep)
    v2 = jnp.max(ep2, axis=1, keepdims=True)
    i2 = jnp.min(jnp.where(ep2 == v2, lane_f, big), axis=1, keepdims=True)
    w0 = g_w * v1 / (v1 + v2)
    w1 = g_w * v2 / (v1 + v2)
    e0 = i1 - float(MOE_GROUPS)
    e1 = i2 - float(MOE_GROUPS)

    oh0 = jnp.where(lane_f == e0, 1.0, 0.0)
    oh1 = jnp.where(lane_f == e1, 1.0, 0.0)
    pre0 = jnp.dot(tril_ref[...], oh0.astype(BF16), preferred_element_type=F32)
    pre1 = jnp.dot(tril_ref[...], oh1.astype(BF16), preferred_element_type=F32)
    carry = carry_ref[...]
    tot0 = pre0[tm - 1:tm, :]
    tot1 = pre1[tm - 1:tm, :]
    rank0 = jnp.sum(oh0 * (pre0 - 1.0 + carry), axis=1, keepdims=True)
    rank1 = jnp.sum(oh1 * (pre1 - 1.0 + carry + tot0), axis=1, keepdims=True)
    new_carry = carry + tot0 + tot1
    carry_ref[...] = new_carry
    cnt_ref[...] = jnp.broadcast_to(new_carry, cnt_ref.shape)

    rt = jnp.where(lane == 0, e0, jnp.where(lane == 1, e1, jnp.where(lane == 2, w0, jnp.where(
        lane == 3, w1, jnp.where(lane == 4, rank0, jnp.where(lane == 5, rank1, 0.0))))))
    rt_ref[...] = rt


def _out_proj(nsa_o, ml_o, w_out, x2, g2, w_r, b_r, tril):
    T = x2.shape[0]
    half = D_MODEL // 2
    return pl.pallas_call(
        _outproj_kernel,
        grid=(T // TM_OUT,),
        in_specs=[
            pl.BlockSpec((TM_OUT, half), lambda i: (i, 0)),
            pl.BlockSpec((TM_OUT, half), lambda i: (i, 0)),
            pl.BlockSpec((D_MODEL, D_MODEL), lambda i: (0, 0)),
            pl.BlockSpec((TM_OUT, D_MODEL), lambda i: (i, 0)),
            pl.BlockSpec((1, D_MODEL), lambda i: (0, 0)),
            pl.BlockSpec((D_MODEL, LANE), lambda i: (0, 0)),
            pl.BlockSpec((1, LANE), lambda i: (0, 0)),
            pl.BlockSpec((TM_OUT, TM_OUT), lambda i: (0, 0)),
        ],
        out_specs=[
            pl.BlockSpec((TM_OUT, D_MODEL), lambda i: (i, 0)),
            pl.BlockSpec((TM_OUT, half), lambda i: (i, 0)),
            pl.BlockSpec((TM_OUT, LANE), lambda i: (i, 0)),
            pl.BlockSpec((8, LANE), lambda i: (0, 0)),
        ],
        out_shape=[
            jax.ShapeDtypeStruct((T, D_MODEL), F32),
            jax.ShapeDtypeStruct((T, half), U32),
            jax.ShapeDtypeStruct((T, LANE), F32),
            jax.ShapeDtypeStruct((8, LANE), F32),
        ],
        scratch_shapes=[pltpu.VMEM((1, LANE), F32)],
        compiler_params=_cparams(("arbitrary",)),
        name="out_proj",
    )(nsa_o, ml_o, w_out, x2, g2, w_r, b_r, tril)


TD = 256
DISPATCH_BUFS = 3
DISPATCH_UNROLL = 8


def _dispatch_kernel(dest_ref, zblk_ref, meta_ref, xp_hbm, xs_hbm, zbuf_ref, zsem_ref, tbuf_ref, lsem_ref,
                     rsem_ref):
    n_assign = dest_ref.shape[0]
    nb = xs_hbm.shape[0] // BM
    n_used = meta_ref[0]
    zbuf_ref[...] = jnp.zeros_like(zbuf_ref)

    def zero_block(blk):
        return pltpu.make_async_copy(zbuf_ref, xs_hbm.at[pl.ds(blk * BM, BM), :], zsem_ref.at[0])

    def for_zero_blocks(fn):
        for e in range(N_EXPERTS):
            @pl.when(zblk_ref[e] >= 0)
            def _(e=e):
                fn(zero_block(zblk_ref[e]))

        def body(blk, _):
            fn(zero_block(blk))
            return 0
        lax.fori_loop(n_used, nb, body, 0)

    for_zero_blocks(lambda c: c.start())
    for_zero_blocks(lambda c: c.wait())

    n_tiles = n_assign // (2 * TD)

    def load(tile, slot):
        return pltpu.make_async_copy(xp_hbm.at[pl.ds(tile * TD, TD), :], tbuf_ref.at[slot], lsem_ref.at[slot])

    def wait_scatter(slot):
        for _ in range(2):
            pltpu.make_async_copy(tbuf_ref.at[slot], tbuf_ref.at[slot], rsem_ref.at[slot]).wait()

    load(0, 0).start()

    def tile_step(i, _):
        slot = i % DISPATCH_BUFS
        nslot = (i + 1) % DISPATCH_BUFS

        @pl.when(i + 1 < n_tiles)
        def _():
            @pl.when(i + 1 >= DISPATCH_BUFS)
            def _():
                wait_scatter(nslot)
            load(i + 1, nslot).start()

        load(i, slot).wait()

        def body(j, _):
            for u in range(DISPATCH_UNROLL):
                r = j * DISPATCH_UNROLL + u
                for k in range(2):
                    d = dest_ref[(i * TD + r) * 2 + k]
                    pltpu.make_async_copy(_row(tbuf_ref.at[slot], r), _row(xs_hbm, d),
                                          rsem_ref.at[slot]).start(priority=k)
            return 0
        lax.fori_loop(0, TD // DISPATCH_UNROLL, body, 0)
        return 0

    lax.fori_loop(0, n_tiles, tile_step, 0)
    for back in range(min(DISPATCH_BUFS, n_tiles)):
        wait_scatter((n_tiles - 1 - back) % DISPATCH_BUFS)


def _dispatch(dest, zblk, meta, xp, n_rows):
    half = D_MODEL // 2
    assert dest.shape[0] % (2 * TD) == 0 and dest.shape[0] // (2 * TD) >= DISPATCH_BUFS
    grid_spec = pltpu.PrefetchScalarGridSpec(
        num_scalar_prefetch=3,
        grid=(1,),
        in_specs=[pl.BlockSpec(memory_space=pl.ANY)],
        out_specs=pl.BlockSpec(memory_space=pl.ANY),
        scratch_shapes=[
            pltpu.VMEM((BM, half), U32),
            pltpu.SemaphoreType.DMA((1,)),
            pltpu.VMEM((DISPATCH_BUFS, TD, half), U32),
            pltpu.SemaphoreType.DMA((DISPATCH_BUFS,)),
            pltpu.SemaphoreType.DMA((DISPATCH_BUFS,)),
        ],
    )
    return pl.pallas_call(
        _dispatch_kernel,
        grid_spec=grid_spec,
        out_shape=jax.ShapeDtypeStruct((n_rows, half), U32),
        compiler_params=_cparams(("arbitrary",)),
        name="dispatch",
    )(dest, zblk, meta, xp)


def _expert_kernel(be_ref, nxt_ref, par_ref, meta_ref, x_ref, wg_hbm, wu_hbm, wd_hbm, y_ref,
                   wsg_ref, wsu_ref, wsd_ref, wsem_ref, wgb_ref, wub_ref, wdb_ref):
    i = pl.program_id(0)
    n_used = meta_ref[0]
    half = D_MODEL // 2

    def weight_copies(e):
        return (pltpu.make_async_copy(wg_hbm.at[e], wsg_ref, wsem_ref.at[0]),
                pltpu.make_async_copy(wu_hbm.at[e], wsu_ref, wsem_ref.at[1]),
                pltpu.make_async_copy(wd_hbm.at[e], wsd_ref, wsem_ref.at[2]))

    def start_weights(e):
        @pl.when(e >= 0)
        def _():
            for c in weight_copies(e):
                c.start()

    def wait_staged():
        for c in weight_copies(0):
            c.wait()

    def cast_staged(p):
        wgb_ref[p] = wsg_ref[...].astype(BF16)
        wub_ref[p] = wsu_ref[...].astype(BF16)
        wdb_ref[p] = wsd_ref[...].astype(BF16)

    @pl.when(i == 0)
    def _():
        start_weights(be_ref[0])
        wait_staged()
        cast_staged(0)
        start_weights(nxt_ref[be_ref[0]])

    def step(p, cast_next):
        if cast_next:
            wait_staged()
        lo, hi = _unpack_bf16_pair(x_ref[...])
        xl = lo.astype(BF16)
        xh = hi.astype(BF16)
        gt = (jnp.dot(xl, wgb_ref[p, 0:half, :], preferred_element_type=F32)
              + jnp.dot(xh, wgb_ref[p, half:, :], preferred_element_type=F32))
        up = (jnp.dot(xl, wub_ref[p, 0:half, :], preferred_element_type=F32)
              + jnp.dot(xh, wub_ref[p, half:, :], preferred_element_type=F32))
        hb = (_silu(gt) * up).astype(BF16)
        y = jnp.dot(hb, wdb_ref[p], preferred_element_type=F32)
        y_ref[...] = _pack_bf16_pair(y[:, :half], y[:, half:])
        if cast_next:
            cast_staged(1 - p)
            start_weights(nxt_ref[nxt_ref[be_ref[i]]])

    e = be_ref[jnp.minimum(i, n_used - 1)]
    ends_expert = (i + 1 < n_used) & (be_ref[jnp.minimum(i + 1, n_used - 1)] != e)
    for p in range(2):
        for cast_next in (False, True):
            @pl.when((i < n_used) & (par_ref[i] == p)
                     & (ends_expert if cast_next else jnp.logical_not(ends_expert)))
            def _(p=p, cast_next=cast_next):
                step(p, cast_next)

    @pl.when(i >= n_used)
    def _():
        y_ref[...] = jnp.zeros_like(y_ref)


def _experts(block_expert, next_expert, block_parity, meta, xs, w_gate, w_up, w_down):
    half = D_MODEL // 2
    n_rows = xs.shape[0]
    nb = n_rows // BM
    grid_spec = pltpu.PrefetchScalarGridSpec(
        num_scalar_prefetch=4,
        grid=(nb,),
        in_specs=[
            pl.BlockSpec((BM, half), lambda i, be, nxt, par, meta: (jnp.minimum(i, meta[0] - 1), 0)),
            pl.BlockSpec(memory_space=pl.ANY), pl.BlockSpec(memory_space=pl.ANY),
            pl.BlockSpec(memory_space=pl.ANY),
        ],
        out_specs=pl.BlockSpec((BM, half), lambda i, be, nxt, par, meta: (i, 0)),
        scratch_shapes=[
            pltpu.VMEM((D_MODEL, D_EXPERT), F32),
            pltpu.VMEM((D_MODEL, D_EXPERT), F32),
            pltpu.VMEM((D_EXPERT, D_MODEL), F32),
            pltpu.SemaphoreType.DMA((3,)),
            pltpu.VMEM((2, D_MODEL, D_EXPERT), BF16),
            pltpu.VMEM((2, D_MODEL, D_EXPERT), BF16),
            pltpu.VMEM((2, D_EXPERT, D_MODEL), BF16),
        ],
    )
    return pl.pallas_call(
        _expert_kernel,
        grid_spec=grid_spec,
        out_shape=jax.ShapeDtypeStruct((n_rows, half), U32),
        compiler_params=_cparams(("arbitrary",)),
        name="experts",
    )(block_expert, next_expert, block_parity, meta, xs, w_gate, w_up, w_down)


def _combine_kernel(dest_ref, y_hbm, x1_ref, rt_ref, fg_ref, o_ref, ya_ref, yb_ref, sem_ref):
    i = pl.program_id(0)
    nt = pl.num_programs(0)
    half = D_MODEL // 2
    bufs = (ya_ref, yb_ref)

    def row_copy(tile, slot, r_tile, s, k):
        r = r_tile * 8 + s
        d = dest_ref[(tile * TC + r) * 2 + k]
        return pltpu.make_async_copy(_row(y_hbm, d), _row(bufs[slot].at[k], r), sem_ref.at[slot])

    def wait_rows(slot):
        pltpu.make_async_copy(bufs[slot], bufs[slot], sem_ref.at[slot]).wait()

    def step(slot):
        wait_rows(slot)
        for r in range(TC):
            for k in range(2):
                row_copy(i + 1, 1 - slot, r // 8, r % 8, k).start(priority=k)
        rt = rt_ref[...]
        w0 = rt[:, 2:3]
        w1 = rt[:, 3:4]
        lo0, hi0 = _unpack_bf16_pair(bufs[slot][0])
        lo1, hi1 = _unpack_bf16_pair(bufs[slot][1])
        xl = x1_ref[:, :half] + (w0 * lo0 + w1 * lo1)
        xh = x1_ref[:, half:] + (w0 * hi0 + w1 * hi1)
        ms = (jnp.sum(xl * xl, axis=1, keepdims=True) + jnp.sum(xh * xh, axis=1, keepdims=True)) / float(D_MODEL)
        inv = lax.rsqrt(ms + NORM_EPS)
        o_ref[:, :half] = xl * inv * fg_ref[:, :half]
        o_ref[:, half:] = xh * inv * fg_ref[:, half:]

        @pl.when(i == nt - 1)
        def _():
            wait_rows(1 - slot)

    @pl.when(i == 0)
    def _():
        def body(rt, _):
            for s in range(8):
                for k in range(2):
                    row_copy(0, 0, rt, s, k).start(priority=k)
            return 0
        lax.fori_loop(0, TC // 8, body, 0)

    @pl.when(i % 2 == 0)
    def _():
        step(0)

    @pl.when(i % 2 == 1)
    def _():
        step(1)


def _combine(dest, ys, x1, route, fg):
    T = x1.shape[0]
    half = D_MODEL // 2
    grid_spec = pltpu.PrefetchScalarGridSpec(
        num_scalar_prefetch=1,
        grid=(T // TC,),
        in_specs=[
            pl.BlockSpec(memory_space=pl.ANY),
            pl.BlockSpec((TC, D_MODEL), lambda i, d: (i, 0)),
            pl.BlockSpec((TC, LANE), lambda i, d: (i, 0)),
            pl.BlockSpec((1, D_MODEL), lambda i, d: (0, 0)),
        ],
        out_specs=pl.BlockSpec((TC, D_MODEL), lambda i, d: (i, 0)),
        scratch_shapes=[
            pltpu.VMEM((2, TC, half), U32),
            pltpu.VMEM((2, TC, half), U32),
            pltpu.SemaphoreType.DMA((2,)),
        ],
    )
    return pl.pallas_call(
        _combine_kernel,
        grid_spec=grid_spec,
        out_shape=jax.ShapeDtypeStruct((T, D_MODEL), F32),
        compiler_params=_cparams(("arbitrary",)),
        name="combine",
    )(dest, ys, x1, route, fg)


def _cover_matrix(S):
    n_blk = S // CMP_STRIDE
    n_sel = S // SEL_BLOCK
    cs = np.arange(n_blk) * CMP_STRIDE
    ss = np.arange(n_sel) * SEL_BLOCK
    shared = np.minimum(cs[:, None] + CMP_LEN, ss[None, :] + SEL_BLOCK) - np.maximum(cs[:, None], ss[None, :])
    return (np.clip(shared, 0, None) / CMP_LEN).T.astype(np.float32)


def _block_mask_matrix(S):
    n_sel = S // SEL_BLOCK
    assert n_sel <= LANE
    e = np.zeros((S, LANE), np.float32)
    e[np.arange(S), np.arange(S) // SEL_BLOCK] = NEG_INF
    return e


def _inv_freq_row():
    inv = np.power(np.float32(ROPE_THETA), -np.arange(ROPE_HALF, dtype=np.float32) * 2.0 / ROPE_DIM)
    row = np.zeros((1, LANE), np.float32)
    row[0, :ROPE_HALF] = inv
    row[0, ROPE_HALF:ROPE_DIM] = inv
    return row


def _layer(x2, positions, B, S, norm1_g, w_in, cmp_pe_k, cmp_pe_v, cmp_wk1, cmp_wk2, cmp_wv1, cmp_wv2,
           nsa_norm_g, conv_w, conv_b, b_igate, b_fgate, mlstm_norm_g, w_out, norm2_g,
           w_group, b_group, w_router, b_router, w_exp_gate, w_exp_up, w_exp_down, out_norm_g):
    T = B * S
    o_q, o_kv, o_g, o_qk, o_v, o_o, o_i, o_f = 0, 1024, 2560, 2584, 4632, 5656, 6680, 6684
    w_main = jnp.concatenate([w_in[:, o_q:o_g], w_in[:, o_qk:o_i]], axis=1).astype(BF16)
    w_small = jnp.concatenate([w_in[:, o_g:o_qk], w_in[:, o_i:], jnp.zeros((D_MODEL, LANE - 32), F32)],
                              axis=1).astype(BF16)
    posb = jnp.broadcast_to(positions.reshape(T, 1).astype(F32), (T, LANE))
    invf = jnp.asarray(_inv_freq_row())

    proj, small = _in_proj(x2, norm1_g.reshape(1, -1), w_main, w_small)
    q_r, k_r = _rope(posb, invf, proj)
    kc, vc = _compress(k_r, proj, B, S, cmp_pe_k, cmp_pe_v, cmp_wk1.astype(BF16), cmp_wk2.astype(BF16),
                       cmp_wv1.astype(BF16), cmp_wv2.astype(BF16))
    eye = np.eye(S // SEL_BLOCK, LANE, dtype=np.float32)
    nsa_o = _nsa(q_r, k_r, proj, small, kc, vc, jnp.asarray(_cover_matrix(S), BF16),
                 jnp.asarray(_block_mask_matrix(S), BF16), jnp.asarray(eye, BF16),
                 nsa_norm_g.reshape(1, -1), B, S)

    if_arr = small[:, 24:32].reshape(B, S, 2, ML_HEADS).transpose(0, 3, 2, 1)
    gate_bias = jnp.stack([b_igate, b_fgate]).astype(F32)
    ml_o = _mlstm(proj, if_arr, gate_bias, conv_w, conv_b.reshape(1, -1), mlstm_norm_g.reshape(1, -1), B, S)

    w_r = jnp.concatenate([w_group, w_router, jnp.zeros((D_MODEL, LANE - MOE_GROUPS - N_EXPERTS), F32)],
                          axis=1).astype(BF16)
    b_r = jnp.concatenate([b_group, b_router, jnp.zeros((LANE - MOE_GROUPS - N_EXPERTS,), F32)]).reshape(1, LANE)
    tril = jnp.asarray(np.tril(np.ones((TM_OUT, TM_OUT), np.float32)), BF16)
    x1, xp, route, cnt = _out_proj(nsa_o, ml_o, w_out.astype(BF16), x2, norm2_g.reshape(1, -1), w_r, b_r, tril)

    n_rows = T * 2 + N_EXPERTS * BM
    counts = cnt[0, :N_EXPERTS].astype(jnp.int32)
    pcounts = (counts + BM - 1) // BM * BM
    pends = jnp.cumsum(pcounts)
    pstarts = pends - pcounts
    eid = route[:, 0:2].astype(jnp.int32)
    onehot = (eid[..., None] == jnp.arange(N_EXPERTS, dtype=jnp.int32)).astype(F32)
    start_blk = jnp.einsum('tke,e->tk', onehot, (pstarts // BM).astype(F32)).astype(jnp.int32)
    dest = (start_blk * BM + route[:, 4:6].astype(jnp.int32)).reshape(T * 2)
    block_expert = jnp.minimum(jnp.searchsorted(pends, jnp.arange(n_rows // BM) * BM, side='right'),
                               N_EXPERTS - 1).astype(jnp.int32)
    present = jnp.where(counts > 0, jnp.arange(N_EXPERTS, dtype=jnp.int32), N_EXPERTS)
    later = jnp.concatenate([lax.cummin(present[::-1])[::-1][1:], jnp.full((1,), N_EXPERTS, jnp.int32)])
    next_expert = jnp.where(later < N_EXPERTS, later, -1).astype(jnp.int32)
    meta = (pends[-1:] // BM).astype(jnp.int32)

    last_block = jnp.where(counts > 0, pends // BM - 1, -1).astype(jnp.int32)

    xs = _dispatch(dest, last_block, meta, xp, n_rows)
    ordinal = jnp.cumsum((counts > 0).astype(jnp.int32)) - 1
    block_parity = (ordinal[block_expert] % 2).astype(jnp.int32)
    ys = _experts(block_expert, next_expert, block_parity, meta, xs, w_exp_gate, w_exp_up, w_exp_down)
    dest_pad = jnp.concatenate([dest, jnp.zeros((2 * TC,), jnp.int32)])
    return _combine(dest_pad, ys, x1, route, out_norm_g.reshape(1, -1))


def kernel(x, positions, norm1_g, w_in, cmp_pe_k, cmp_pe_v, cmp_wk1, cmp_wk2, cmp_wv1, cmp_wv2, nsa_norm_g,
           conv_w, conv_b, b_igate, b_fgate, mlstm_norm_g, w_out, norm2_g, w_group, b_group, w_router,
           b_router, w_exp_gate, w_exp_up, w_exp_down, final_norm_g):
    B, S, D = x.shape
    assert D == D_MODEL and norm1_g.shape[0] == 1, "single-layer, D_MODEL-wide configuration only"
    assert S % ML_CHUNK == 0 and S % SEL_BUCKET == 0 and S >= WIN_KEYS and (B * S) % TM_IN == 0
    out = _layer(x.reshape(B * S, D), positions, B, S, norm1_g[0], w_in[0], cmp_pe_k[0], cmp_pe_v[0],
                 cmp_wk1[0], cmp_wk2[0], cmp_wv1[0], cmp_wv2[0], nsa_norm_g[0], conv_w[0], conv_b[0],
                 b_igate[0], b_fgate[0], mlstm_norm_g[0], w_out[0], norm2_g[0], w_group[0], b_group[0],
                 w_router[0], b_router[0], w_exp_gate[0], w_exp_up[0], w_exp_down[0], final_norm_g)
    return out.reshape(B, S, D)
```

```python
import functools

import numpy as np
import jax
import jax.numpy as jnp
from jax import lax
from jax.experimental import pallas as pl
from jax.experimental.pallas import tpu as pltpu

F32 = jnp.float32
BF16 = jnp.bfloat16
U32 = jnp.uint32

D_MODEL = 2048
NSA_HEADS = 8
NSA_GROUPS = 2
NSA_J = NSA_HEADS // NSA_GROUPS
HD = 128
CMP_LEN = 32
CMP_STRIDE = 16
SEL_BLOCK = 64
SEL_TOPK = 8
WINDOW = 512
ROPE_THETA = 500000.0
ROPE_DIM = 32
ROPE_HALF = 16
ML_HEADS = 4
ML_HD = 256
CONV_W = 4
MOE_GROUPS = 8
EPG = 8
N_EXPERTS = 64
D_EXPERT = 512
SEL_BLOCK_LOG2 = 6
EPG_LOG2 = 3
assert (1 << SEL_BLOCK_LOG2) == SEL_BLOCK and (1 << EPG_LOG2) == EPG
NORM_EPS = 1e-6
NEG_INF = -1e30
FORCE_SCORE = 1000.0

COL_Q = 0
COL_KV = 1024
COL_QKM = 2560
COL_VM = 4608
COL_OM = 5632
N_MAIN = 6656

LANE = 128
VMEM_LIMIT = 56 * 1024 * 1024

TM_IN = 1024
TN_IN = 1664
TS_ROPE = 256
TQ = 128
SEL_BUCKET = 256
WIN_KEYS = WINDOW + TQ
ML_CHUNK = 256
TM_OUT = 512
BM = 256
TC = 256

NT_DIMS = (((1,), (1,)), ((), ()))
TN_DIMS = (((0,), (0,)), ((), ()))


def _cparams(sem):
    return pltpu.CompilerParams(dimension_semantics=sem, vmem_limit_bytes=VMEM_LIMIT)


def _sigmoid(x):
    return 0.5 * jnp.tanh(0.5 * x) + 0.5


def _silu(x):
    h = 0.5 * x
    return h + h * jnp.tanh(h)


W_GATES = 2560
W_QKM = 2584
W_IGATE = 6680
W_END = 6688
TR_PREP = 256


def _wprep_kernel(w_ref, wm_ref, ws_ref):
    wm_ref[:, 0:W_GATES] = w_ref[:, 0:W_GATES].astype(BF16)
    wm_ref[:, W_GATES:N_MAIN] = w_ref[:, W_QKM:W_IGATE].astype(BF16)
    n_small = (W_QKM - W_GATES) + (W_END - W_IGATE)
    ws = jnp.concatenate([w_ref[:, W_GATES:W_QKM], w_ref[:, W_IGATE:W_END],
                          jnp.zeros((w_ref.shape[0], LANE - n_small), F32)], axis=1)
    ws_ref[...] = ws.astype(BF16)


def _weight_prep(w_in):
    assert w_in.shape == (D_MODEL, W_END) and W_IGATE - W_QKM == N_MAIN - W_GATES
    return pl.pallas_call(
        _wprep_kernel,
        grid=(D_MODEL // TR_PREP,),
        in_specs=[pl.BlockSpec((TR_PREP, W_END), lambda i: (i, 0))],
        out_specs=[pl.BlockSpec((TR_PREP, N_MAIN), lambda i: (i, 0)),
                   pl.BlockSpec((TR_PREP, LANE), lambda i: (i, 0))],
        out_shape=[jax.ShapeDtypeStruct((D_MODEL, N_MAIN), BF16),
                   jax.ShapeDtypeStruct((D_MODEL, LANE), BF16)],
        compiler_params=_cparams(("parallel",)),
        name="weight_prep",
    )(w_in)


def _inproj_kernel(x_ref, g_ref, w_ref, ws_ref, o_ref, os_ref, h_ref):
    @pl.when(pl.program_id(1) == 0)
    def _():
        x = x_ref[...]
        ms = jnp.mean(x * x, axis=-1, keepdims=True)
        h_ref[...] = (x * lax.rsqrt(ms + NORM_EPS) * g_ref[...]).astype(BF16)
        os_ref[...] = jnp.dot(h_ref[...], ws_ref[...], preferred_element_type=F32)

    o_ref[...] = jnp.dot(h_ref[...], w_ref[...], preferred_element_type=F32).astype(o_ref.dtype)


def _in_proj(x2, g1, w_main, w_small):
    T = x2.shape[0]
    return pl.pallas_call(
        _inproj_kernel,
        grid=(T // TM_IN, N_MAIN // TN_IN),
        in_specs=[
            pl.BlockSpec((TM_IN, D_MODEL), lambda m, n: (m, 0)),
            pl.BlockSpec((1, D_MODEL), lambda m, n: (0, 0)),
            pl.BlockSpec((D_MODEL, TN_IN), lambda m, n: (0, n)),
            pl.BlockSpec((D_MODEL, LANE), lambda m, n: (0, 0)),
        ],
        out_specs=[
            pl.BlockSpec((TM_IN, TN_IN), lambda m, n: (m, n)),
            pl.BlockSpec((TM_IN, LANE), lambda m, n: (m, 0)),
        ],
        out_shape=[
            jax.ShapeDtypeStruct((T, N_MAIN), BF16),
            jax.ShapeDtypeStruct((T, LANE), F32),
        ],
        scratch_shapes=[pltpu.VMEM((TM_IN, D_MODEL), BF16)],
        compiler_params=_cparams(("parallel", "arbitrary")),
        name="in_proj",
    )(x2, g1, w_main, w_small)


ROPE_SLOTS = LANE // ROPE_HALF


def _rope_kernel(pos_ref, invf_ref, q_ref, kc_ref, ks_ref, kw_ref, qo_ref, ko_ref):
    ang = pos_ref[...] * invf_ref[...]
    c = jnp.cos(ang)
    s = jnp.sin(ang)
    rows = ang.shape[0]
    lane = lax.broadcasted_iota(jnp.int32, ang.shape, 1)
    lo = lane < ROPE_HALF
    mid = lane < ROPE_DIM
    scale = HD ** -0.5

    def lanes_from(x, src):
        shift = (-src) % LANE
        return x if shift == 0 else pltpu.roll(x, shift, 1)

    for slot in range(ROPE_SLOTS):
        src = slot * ROPE_HALF
        cf = jnp.where(lo, lanes_from(c, src), jnp.where(mid, lanes_from(c, src - ROPE_HALF), 1.0))
        sa = jnp.where(lo, -lanes_from(s, src), 0.0)
        sb = jnp.where(lo, 0.0, jnp.where(mid, lanes_from(s, src - ROPE_HALF), 0.0))
        rs = slice(slot * rows, (slot + 1) * rows)

        def rope(x):
            return x * cf + pltpu.roll(x, LANE - ROPE_HALF, 1) * sa + pltpu.roll(x, ROPE_HALF, 1) * sb

        for h in range(NSA_HEADS):
            sl = slice(h * HD, (h + 1) * HD)
            qo_ref[rs, sl] = (rope(q_ref[rs, sl].astype(F32)) * scale).astype(BF16)
        for i, r in enumerate((kc_ref, ks_ref, kw_ref)):
            for g in range(NSA_GROUPS):
                sl = slice(g * HD, (g + 1) * HD)
                so = slice(i * 2 * HD + g * HD, i * 2 * HD + (g + 1) * HD)
                ko_ref[rs, so] = rope(r[rs, sl].astype(F32)).astype(BF16)


def _rope(posb, invf, proj):
    T = proj.shape[0]
    kvb = COL_KV // 256
    return pl.pallas_call(
        _rope_kernel,
        grid=(T // TS_ROPE,),
        in_specs=[
            pl.BlockSpec((TS_ROPE // ROPE_SLOTS, LANE), lambda i: (i, 0)),
            pl.BlockSpec((1, LANE), lambda i: (0, 0)),
            pl.BlockSpec((TS_ROPE, 1024), lambda i: (i, 0)),
            pl.BlockSpec((TS_ROPE, 256), lambda i: (i, kvb + 0)),
            pl.BlockSpec((TS_ROPE, 256), lambda i: (i, kvb + 2)),
            pl.BlockSpec((TS_ROPE, 256), lambda i: (i, kvb + 4)),
        ],
        out_specs=[
            pl.BlockSpec((TS_ROPE, 1024), lambda i: (i, 0)),
            pl.BlockSpec((TS_ROPE, 768), lambda i: (i, 0)),
        ],
        out_shape=[
            jax.ShapeDtypeStruct((T, 1024), BF16),
            jax.ShapeDtypeStruct((T, 768), BF16),
        ],
        compiler_params=_cparams(("parallel",)),
        name="rope",
    )(posb, invf, proj, proj, proj, proj)


def _compress_kernel(k_ref, v_ref, pek_ref, pev_ref, w1k_ref, w2k_ref, w1v_ref, w2v_ref,
                     kc_ref, vc_ref, xs_ref):
    S = k_ref.shape[0]
    n_blk = S // CMP_STRIDE
    for src, pe, w1, w2, dst in ((k_ref, pek_ref, w1k_ref, w2k_ref, kc_ref),
                                 (v_ref, pev_ref, w1v_ref, w2v_ref, vc_ref)):
        xs_ref[0:S, :] = src[...].astype(F32)
        xs_ref[S:S + CMP_LEN, :] = jnp.zeros((CMP_LEN, HD), F32)
        acc = jnp.zeros((n_blk, HD), F32)
        for l in range(CMP_LEN):
            a = xs_ref[pl.ds(l, n_blk, stride=CMP_STRIDE), :] + pe[l:l + 1, :]
            acc = acc + jnp.dot(a.astype(BF16), w1[l * HD:(l + 1) * HD, :],
                                preferred_element_type=F32)
        hid = _silu(acc)
        out = jnp.dot(hid.astype(BF16), w2[...], preferred_element_type=F32)
        dst[0, 0] = out.astype(BF16)


def _compress(k_r, proj, B, S, pek, pev, w1k, w2k, w1v, w2v):
    n_blk = S // CMP_STRIDE
    vcol = (COL_KV + 256) // HD
    full = lambda shape: pl.BlockSpec(shape, lambda b, g: tuple(0 for _ in shape))
    return pl.pallas_call(
        _compress_kernel,
        grid=(B, NSA_GROUPS),
        in_specs=[
            pl.BlockSpec((S, HD), lambda b, g: (b, g)),
            pl.BlockSpec((S, HD), lambda b, g: (b, vcol + g)),
            full((CMP_LEN, HD)), full((CMP_LEN, HD)),
            full((CMP_LEN * HD, HD)), full((HD, HD)),
            full((CMP_LEN * HD, HD)), full((HD, HD)),
        ],
        out_specs=[
            pl.BlockSpec((1, 1, n_blk, HD), lambda b, g: (b, g, 0, 0)),
            pl.BlockSpec((1, 1, n_blk, HD), lambda b, g: (b, g, 0, 0)),
        ],
        out_shape=[
            jax.ShapeDtypeStruct((B, NSA_GROUPS, n_blk, HD), BF16),
            jax.ShapeDtypeStruct((B, NSA_GROUPS, n_blk, HD), BF16),
        ],
        scratch_shapes=[pltpu.VMEM((S + CMP_LEN, HD), F32)],
        compiler_params=_cparams(("parallel", "parallel")),
        name="compress",
    )(k_r, proj, pek, pev, w1k, w2k, w1v, w2v)


def _nsa_kernel(q_ref, kc_ref, vc_ref, ks_ref, vs_ref, kw_ref, vw_ref, sm_ref, covt_ref, eneg_ref, eye_ref,
                ng_ref, o_ref, obuf_ref, owin_ref, osel_ref):
    qi = pl.program_id(1)
    q0 = qi * TQ
    R = NSA_J * TQ
    n_cmp = kc_ref.shape[2]
    n_sel = ks_ref.shape[0] // SEL_BLOCK
    n_win = WIN_KEYS // TQ

    def row_t(shape):
        r = lax.broadcasted_iota(jnp.int32, shape, 0)
        return q0 + (r & (TQ - 1))

    qgs = [jnp.concatenate([q_ref[:, (g * NSA_J + j) * HD:(g * NSA_J + j + 1) * HD]
                            for j in range(NSA_J)], axis=0) for g in range(NSA_GROUPS)]

    def with_ones(v):
        lane = lax.broadcasted_iota(jnp.int32, v.shape, 1)
        return jnp.concatenate([v, jnp.where(lane == 0, 1.0, 0.0).astype(BF16)], axis=1)

    def window(masks):
        w0 = pl.multiple_of(jnp.maximum(q0 - WINDOW, 0), TQ)
        for g in range(NSA_GROUPS):
            kt = kw_ref[pl.ds(w0, WIN_KEYS), g * HD:(g + 1) * HD]
            vt = with_ones(vw_ref[pl.ds(w0, WIN_KEYS), g * HD:(g + 1) * HD])
            sc = masks(lax.dot_general(qgs[g], kt, NT_DIMS, preferred_element_type=F32), w0)
            mw = jnp.max(sc, axis=1, keepdims=True)
            pw = jnp.exp((sc - mw).astype(BF16))
            acc = jnp.dot(pw, vt, preferred_element_type=F32)
            owin_ref[g] = acc[:, 0:HD] / acc[:, HD:HD + 1]

    def band_masks(sc, w0):
        d = (lax.broadcasted_iota(jnp.int32, (R, TQ), 1)
             - (lax.broadcasted_iota(jnp.int32, (R, TQ), 0) & (TQ - 1)))
        first = jnp.where(d > 0, sc[:, 0:TQ], NEG_INF)
        last = jnp.where(d <= 0, sc[:, (n_win - 1) * TQ:], NEG_INF)
        return jnp.concatenate([first, sc[:, TQ:(n_win - 1) * TQ], last], axis=1)

    def general_masks(sc, w0):
        diff = row_t((R, WIN_KEYS)) - (w0 + lax.broadcasted_iota(jnp.int32, (R, WIN_KEYS), 1))
        keep = jnp.where(diff >= 0, diff, WINDOW) < WINDOW
        return jnp.where(keep, sc, NEG_INF)

    window(general_masks)

    gates = _sigmoid(sm_ref[...])
    o_cmps = []
    qps = []
    for g in range(NSA_GROUPS):
        qg = qgs[g]

        s = lax.dot_general(qg, kc_ref[0, g], NT_DIMS, preferred_element_type=F32)
        n_lane = lax.broadcasted_iota(jnp.int32, (R, n_cmp), 1)
        cmask = (n_lane * CMP_STRIDE + (CMP_LEN - 1)) <= row_t((R, n_cmp))
        s = jnp.where(cmask, s, NEG_INF)
        m = jnp.max(s, axis=1, keepdims=True)
        e = jnp.where(cmask, jnp.exp(s - m), 0.0)
        l = jnp.sum(e, axis=1, keepdims=True)
        p = (e / jnp.where(l > 0.0, l, 1.0)).astype(BF16)
        o_cmps.append(jnp.dot(p, vc_ref[0, g], preferred_element_type=F32))
        impr = lax.dot_general(covt_ref[...], p, NT_DIMS, preferred_element_type=F32)
        imp = impr[:, 0:TQ]
        for j in range(1, NSA_J):
            imp = imp + impr[:, j * TQ:(j + 1) * TQ]

        m_sub = lax.broadcasted_iota(jnp.int32, (n_sel, TQ), 0)
        jt = (q0 + lax.broadcasted_iota(jnp.int32, (n_sel, TQ), 1)) >> SEL_BLOCK_LOG2
        forced = jnp.where(m_sub == 0, FORCE_SCORE,
                           jnp.where(m_sub == jt, FORCE_SCORE,
                                     jnp.where(m_sub == jt - 1, FORCE_SCORE, 0.0)))
        score = jnp.where(m_sub <= jt, imp + forced, -1.0)
        rank = jnp.zeros((n_sel, TQ), F32)
        for mp in range(n_sel):
            row = score[mp:mp + 1, :]
            ge = jnp.where(row >= score, 1.0, 0.0)
            gt = jnp.where(row > score, 1.0, 0.0)
            rank = rank + jnp.where(m_sub > mp, ge, gt)
        notsel_t = jnp.where(rank < float(min(SEL_TOPK, n_sel)), 0.0, 1.0).astype(BF16)
        notsel = lax.dot_general(notsel_t, eye_ref[...], TN_DIMS,
                                 preferred_element_type=F32).astype(BF16)
        qps.append(jnp.concatenate([qg, jnp.concatenate([notsel] * NSA_J, axis=0)], axis=1))

    def sel_oneshot(nk):
        for g in range(NSA_GROUPS):
            kt = jnp.concatenate([ks_ref[0:nk, g * HD:(g + 1) * HD], eneg_ref[0:nk, :]], axis=1)
            vt = with_ones(vs_ref[0:nk, g * HD:(g + 1) * HD])
            sc = lax.dot_general(qps[g], kt, NT_DIMS, preferred_element_type=F32)
            klane = lax.broadcasted_iota(jnp.int32, (R, SEL_BUCKET), 1) + (nk - SEL_BUCKET)
            tail = jnp.where(klane <= row_t((R, SEL_BUCKET)), sc[:, nk - SEL_BUCKET:], NEG_INF)
            sc = tail if nk == SEL_BUCKET else jnp.concatenate([sc[:, 0:nk - SEL_BUCKET], tail], axis=1)
            ms = jnp.max(sc, axis=1, keepdims=True)
            acc = jnp.dot(jnp.exp((sc - ms).astype(BF16)), vt, preferred_element_type=F32)
            osel_ref[g] = acc[:, 0:HD] / acc[:, HD:HD + 1]

    for b in range(ks_ref.shape[0] // SEL_BUCKET):
        @pl.when(q0 // SEL_BUCKET == b)
        def _(b=b):
            sel_oneshot((b + 1) * SEL_BUCKET)

    ssq = jnp.zeros((TQ, 1), F32)
    for g in range(NSA_GROUPS):
        o_sel = osel_ref[g]
        o_cmp = o_cmps[g]
        o_win = owin_ref[g]
        for j in range(NSA_J):
            h = g * NSA_J + j
            rs = slice(j * TQ, (j + 1) * TQ)
            o = (gates[:, 3 * h:3 * h + 1] * o_cmp[rs] + gates[:, 3 * h + 1:3 * h + 2] * o_sel[rs]
                 + gates[:, 3 * h + 2:3 * h + 3] * o_win[rs])
            ssq = ssq + jnp.sum(o * o, axis=1, keepdims=True)
            obuf_ref[:, h * HD:(h + 1) * HD] = o

    inv = lax.rsqrt(ssq / float(NSA_HEADS * HD) + NORM_EPS)
    o_ref[...] = (obuf_ref[...] * inv * ng_ref[...]).astype(BF16)


def _nsa(q_r, k_r, proj, small, kc, vc, covt, eneg, eye, ng, B, S):
    T = B * S
    nq = S // TQ
    n_blk = S // CMP_STRIDE
    n_sel = S // SEL_BLOCK
    kvb = COL_KV // 256
    return pl.pallas_call(
        _nsa_kernel,
        grid=(B, nq),
        in_specs=[
            pl.BlockSpec((TQ, 1024), lambda b, i: (b * nq + i, 0)),
            pl.BlockSpec((1, NSA_GROUPS, n_blk, HD), lambda b, i: (b, 0, 0, 0)),
            pl.BlockSpec((1, NSA_GROUPS, n_blk, HD), lambda b, i: (b, 0, 0, 0)),
            pl.BlockSpec((S, 256), lambda b, i: (b, 1)),
            pl.BlockSpec((S, 256), lambda b, i: (b, kvb + 3)),
            pl.BlockSpec((S, 256), lambda b, i: (b, 2)),
            pl.BlockSpec((S, 256), lambda b, i: (b, kvb + 5)),
            pl.BlockSpec((TQ, LANE), lambda b, i: (b * nq + i, 0)),
            pl.BlockSpec((n_sel, n_blk), lambda b, i: (0, 0)),
            pl.BlockSpec((S, LANE), lambda b, i: (0, 0)),
            pl.BlockSpec((n_sel, LANE), lambda b, i: (0, 0)),
            pl.BlockSpec((1, 1024), lambda b, i: (0, 0)),
        ],
        out_specs=pl.BlockSpec((TQ, 1024), lambda b, i: (b * nq + i, 0)),
        out_shape=jax.ShapeDtypeStruct((T, 1024), BF16),
        scratch_shapes=[pltpu.VMEM((TQ, 1024), F32), pltpu.VMEM((NSA_GROUPS, NSA_J * TQ, HD), F32),
                        pltpu.VMEM((NSA_GROUPS, NSA_J * TQ, HD), F32)],
        compiler_params=_cparams(("parallel", "parallel")),
        name="nsa",
    )(q_r, kc, vc, k_r, proj, k_r, proj, small, covt, eneg, eye, ng)


ML_HPS = 2


def _mlstm_kernel(bias_ref, q_ref, k_ref, v_ref, om_ref, if_ref, cwq_ref, cwk_ref, cbq_ref, cbk_ref,
                  ng_ref, o_ref, c_ref, n_ref, m_ref, xq_ref, xk_ref):
    hp = pl.program_id(1)
    L = ML_CHUNK

    @pl.when(pl.program_id(2) == 0)
    def _():
        c_ref[...] = jnp.zeros_like(c_ref)
        n_ref[...] = jnp.zeros_like(n_ref)
        m_ref[...] = jnp.zeros_like(m_ref)
        xq_ref[:, 0:8, :] = jnp.zeros((ML_HPS, 8, ML_HD), F32)
        xk_ref[:, 0:8, :] = jnp.zeros((ML_HPS, 8, ML_HD), F32)

    def conv_silu(x_ref, hh, xb_ref, w_ref, b_ref):
        cs = slice(hh * ML_HD, (hh + 1) * ML_HD)
        xb_ref[hh, 8:8 + L, :] = x_ref[:, cs].astype(F32)
        y = b_ref[:, cs] + xb_ref[hh, 8:8 + L, :] * w_ref[CONV_W - 1:CONV_W, cs]
        for k in range(1, CONV_W):
            y = y + xb_ref[hh, 8 - k:8 - k + L, :] * w_ref[CONV_W - 1 - k:CONV_W - k, cs]
        xb_ref[hh, 0:8, :] = xb_ref[hh, L:L + 8, :]
        return _silu(y)

    r = lax.broadcasted_iota(jnp.int32, (L, L), 0)
    cidx = lax.broadcasted_iota(jnp.int32, (L, L), 1)
    tril = cidx <= r
    eye = cidx == r

    for hh in range(ML_HPS):
        h = hp * ML_HPS + hh
        cs = slice(hh * ML_HD, (hh + 1) * ML_HD)
        qf = conv_silu(q_ref, hh, xq_ref, cwq_ref, cbq_ref) * (ML_HD ** -0.5)
        kf = conv_silu(k_ref, hh, xk_ref, cwk_ref, cbk_ref)
        qb = qf.astype(BF16)
        kb = kf.astype(BF16)
        vb = v_ref[:, cs]
        vf = vb.astype(F32)

        ic = if_ref[0, hh, 0:1, :] + bias_ref[0, h]
        fp = if_ref[0, hh, 1:2, :] + bias_ref[1, h]
        fc = jnp.minimum(fp, 0.0) - jnp.log(1.0 + jnp.exp(-jnp.abs(fp)))

        fc_b = jnp.broadcast_to(fc, (L, L))
        ic_b = jnp.broadcast_to(ic, (L, L))
        b_col = jnp.sum(jnp.where(tril, fc_b, 0.0), axis=1, keepdims=True)
        fc_col = jnp.sum(jnp.where(eye, fc_b, 0.0), axis=1, keepdims=True)
        ic_col = jnp.sum(jnp.where(eye, ic_b, 0.0), axis=1, keepdims=True)
        b_row = jnp.sum(jnp.where(r <= cidx, jnp.broadcast_to(fc_col, (L, L)), 0.0),
                        axis=0, keepdims=True)
        b_last = b_col[L - 1:L, :]
        m_prev = m_ref[hh]

        d_log = jnp.where(tril, b_col - b_row + ic, NEG_INF)
        inter = b_col + m_prev
        m_t = jnp.maximum(inter, jnp.max(d_log, axis=1, keepdims=True))
        w_intra = jnp.exp(d_log - m_t)
        w_inter = jnp.exp(inter - m_t)
        qk = lax.dot_general(qb, kb, NT_DIMS, preferred_element_type=F32) * w_intra
        num = (jnp.dot(qk.astype(BF16), vb, preferred_element_type=F32)
               + w_inter * lax.dot_general(qb, c_ref[hh].astype(BF16), NT_DIMS, preferred_element_type=F32))
        den = jnp.sum(qk, axis=1, keepdims=True) + w_inter * jnp.sum(qf * n_ref[hh], axis=1, keepdims=True)
        hm = num / jnp.maximum(jnp.abs(den), jnp.exp(-m_t))

        w_log = b_last - b_col + ic_col
        m_new = jnp.maximum(b_last + m_prev, jnp.max(w_log, axis=0, keepdims=True))
        w_state = jnp.exp(w_log - m_new)
        decay = jnp.exp(b_last + m_prev - m_new)
        c_ref[hh] = decay * c_ref[hh] + lax.dot_general((w_state * vf).astype(BF16), kb, TN_DIMS,
                                                        preferred_element_type=F32)
        n_ref[hh] = decay * n_ref[hh] + jnp.sum(w_state * kf, axis=0, keepdims=True)
        m_ref[hh] = m_new

        hn = hm * lax.rsqrt(jnp.mean(hm * hm, axis=1, keepdims=True) + NORM_EPS) * ng_ref[:, cs]
        o_ref[:, cs] = (hn * _sigmoid(om_ref[:, cs].astype(F32))).astype(BF16)


def _mlstm(proj, if_arr, gate_bias, conv_w, conv_b, ng, B, S):
    T = B * S
    nc = S // ML_CHUNK
    W = ML_HPS * ML_HD
    n_hp = ML_HEADS // ML_HPS
    assert COL_QKM % W == 0 and COL_VM % W == 0 and COL_OM % W == 0 and ML_HEADS % ML_HPS == 0
    cq, ck, cv, co = COL_QKM // W, COL_QKM // W + n_hp, COL_VM // W, COL_OM // W
    rows = lambda col0: pl.BlockSpec((ML_CHUNK, W), lambda b, h, c: (b * nc + c, col0 + h))
    return pl.pallas_call(
        _mlstm_kernel,
        grid=(B, n_hp, nc),
        in_specs=[
            pl.BlockSpec(memory_space=pltpu.SMEM),
            rows(cq), rows(ck), rows(cv), rows(co),
            pl.BlockSpec((1, ML_HPS, 2, ML_CHUNK), lambda b, h, c: (b, h, 0, c)),
            pl.BlockSpec((CONV_W, W), lambda b, h, c: (0, h)),
            pl.BlockSpec((CONV_W, W), lambda b, h, c: (0, n_hp + h)),
            pl.BlockSpec((1, W), lambda b, h, c: (0, h)),
            pl.BlockSpec((1, W), lambda b, h, c: (0, n_hp + h)),
            pl.BlockSpec((1, W), lambda b, h, c: (0, h)),
        ],
        out_specs=pl.BlockSpec((ML_CHUNK, W), lambda b, h, c: (b * nc + c, h)),
        out_shape=jax.ShapeDtypeStruct((T, ML_HEADS * ML_HD), BF16),
        scratch_shapes=[
            pltpu.VMEM((ML_HPS, ML_HD, ML_HD), F32), pltpu.VMEM((ML_HPS, 1, ML_HD), F32),
            pltpu.VMEM((ML_HPS, 1, 1), F32),
            pltpu.VMEM((ML_HPS, ML_CHUNK + 8, ML_HD), F32), pltpu.VMEM((ML_HPS, ML_CHUNK + 8, ML_HD), F32),
        ],
        compiler_params=_cparams(("parallel", "parallel", "arbitrary")),
        name="mlstm",
    )(gate_bias, proj, proj, proj, proj, if_arr, conv_w, conv_w, conv_b, conv_b, ng)


def _pack_bf16_pair(lo, hi):
    lo_b = pltpu.bitcast(lo.astype(BF16).astype(F32), U32)
    hi_b = pltpu.bitcast(hi.astype(BF16).astype(F32), U32)
    return (lo_b >> 16) | hi_b


def _unpack_bf16_pair(p):
    lo = pltpu.bitcast(p << 16, F32)
    hi = pltpu.bitcast(p & jnp.uint32(0xFFFF0000), F32)
    return lo, hi


def _row(ref, r):
    return ref.at[pl.ds(r, 1), :]


def _outproj_kernel(nsa_ref, ml_ref, w_ref, x_ref, g2_ref, wr_ref, br_ref, tril_ref,
                    x1_ref, xp_ref, rt_ref, cnt_ref, carry_ref):
    @pl.when(pl.program_id(0) == 0)
    def _():
        carry_ref[...] = jnp.zeros_like(carry_ref)

    half = D_MODEL // 2
    acc = jnp.dot(nsa_ref[...], w_ref[0:half, :], preferred_element_type=F32)
    acc = acc + jnp.dot(ml_ref[...], w_ref[half:, :], preferred_element_type=F32)
    x1 = x_ref[...] + acc
    x1_ref[...] = x1
    xn = x1 * lax.rsqrt(jnp.mean(x1 * x1, axis=-1, keepdims=True) + NORM_EPS) * g2_ref[...]
    xp_ref[...] = _pack_bf16_pair(xn[:, :half], xn[:, half:])
    logits = jnp.dot(xn.astype(BF16), wr_ref[...], preferred_element_type=F32) + br_ref[...]

    tm = logits.shape[0]
    lane = lax.broadcasted_iota(jnp.int32, (tm, LANE), 1)
    lane_f = lane.astype(F32)
    big = float(LANE)
    gmask = lane < MOE_GROUPS
    gmax = jnp.max(jnp.where(gmask, logits, NEG_INF), axis=1, keepdims=True)
    ge = jnp.where(gmask, jnp.exp(logits - gmax), 0.0)
    gp = ge / jnp.sum(ge, axis=1, keepdims=True)
    g_w = jnp.max(gp, axis=1, keepdims=True)
    g_idx = jnp.min(jnp.where(gmask, jnp.where(gp == g_w, lane_f, big), big), axis=1, keepdims=True)
    grp_of_lane = ((lane - MOE_GROUPS) >> EPG_LOG2).astype(F32)
    emask = jnp.where(lane >= MOE_GROUPS, grp_of_lane, -1.0) == g_idx
    emax = jnp.max(jnp.where(emask, logits, NEG_INF), axis=1, keepdims=True)
    ee = jnp.where(emask, jnp.exp(logits - emax), 0.0)
    ep = jnp.where(emask, ee / jnp.sum(ee, axis=1, keepdims=True), -1.0)
    v1 = jnp.max(ep, axis=1, keepdims=True)
    i1 = jnp.min(jnp.where(ep == v1, lane_f, big), axis=1, keepdims=True)
    ep2 = jnp.where(lane_f == i1, -1.0, ep)
    v2 = jnp.max(ep2, axis=1, keepdims=True)
    i2 = jnp.min(jnp.where(ep2 == v2, lane_f, big), axis=1, keepdims=True)
    w0 = g_w * v1 / (v1 + v2)
    w1 = g_w * v2 / (v1 + v2)
    e0 = i1 - float(MOE_GROUPS)
    e1 = i2 - float(MOE_GROUPS)

    oh0 = jnp.where(lane_f == e0, 1.0, 0.0)
    oh1 = jnp.where(lane_f == e1, 1.0, 0.0)
    pre0 = jnp.dot(tril_ref[...], oh0.astype(BF16), preferred_element_type=F32)
    pre1 = jnp.dot(tril_ref[...], oh1.astype(BF16), preferred_element_type=F32)
    carry = carry_ref[...]
    tot0 = pre0[tm - 1:tm, :]
    tot1 = pre1[tm - 1:tm, :]
    rank0 = jnp.sum(oh0 * (pre0 - 1.0 + carry), axis=1, keepdims=True)
    rank1 = jnp.sum(oh1 * (pre1 - 1.0 + carry + tot0), axis=1, keepdims=True)
    new_carry = carry + tot0 + tot1
    carry_ref[...] = new_carry
    cnt_ref[...] = jnp.broadcast_to(new_carry, cnt_ref.shape)

    rt = jnp.where(lane == 0, e0, jnp.where(lane == 1, e1, jnp.where(lane == 2, w0, jnp.where(
        lane == 3, w1, jnp.where(lane == 4, rank0, jnp.where(lane == 5, rank1, 0.0))))))
    rt_ref[...] = rt


def _out_proj(nsa_o, ml_o, w_out, x2, g2, w_r, b_r, tril):
    T = x2.shape[0]
    half = D_MODEL // 2
    return pl.pallas_call(
        _outproj_kernel,
        grid=(T // TM_OUT,),
        in_specs=[
            pl.BlockSpec((TM_OUT, half), lambda i: (i, 0)),
            pl.BlockSpec((TM_OUT, half), lambda i: (i, 0)),
            pl.BlockSpec((D_MODEL, D_MODEL), lambda i: (0, 0)),
            pl.BlockSpec((TM_OUT, D_MODEL), lambda i: (i, 0)),
            pl.BlockSpec((1, D_MODEL), lambda i: (0, 0)),
            pl.BlockSpec((D_MODEL, LANE), lambda i: (0, 0)),
            pl.BlockSpec((1, LANE), lambda i: (0, 0)),
            pl.BlockSpec((TM_OUT, TM_OUT), lambda i: (0, 0)),
        ],
        out_specs=[
            pl.BlockSpec((TM_OUT, D_MODEL), lambda i: (i, 0)),
            pl.BlockSpec((TM_OUT, half), lambda i: (i, 0)),
            pl.BlockSpec((TM_OUT, LANE), lambda i: (i, 0)),
            pl.BlockSpec((8, LANE), lambda i: (0, 0)),
        ],
        out_shape=[
            jax.ShapeDtypeStruct((T, D_MODEL), F32),
            jax.ShapeDtypeStruct((T, half), U32),
            jax.ShapeDtypeStruct((T, LANE), F32),
            jax.ShapeDtypeStruct((8, LANE), F32),
        ],
        scratch_shapes=[pltpu.VMEM((1, LANE), F32)],
        compiler_params=_cparams(("arbitrary",)),
        name="out_proj",
    )(nsa_o, ml_o, w_out, x2, g2, w_r, b_r, tril)


TD = 256
DISPATCH_BUFS = 3
DISPATCH_UNROLL = 8


def _dispatch_kernel(dest_ref, zblk_ref, meta_ref, xp_hbm, xs_hbm, zbuf_ref, zsem_ref, tbuf_ref, lsem_ref,
                     rsem_ref):
    n_assign = dest_ref.shape[0]
    nb = xs_hbm.shape[0] // BM
    n_used = meta_ref[0]
    zbuf_ref[...] = jnp.zeros_like(zbuf_ref)

    def zero_block(blk):
        return pltpu.make_async_copy(zbuf_ref, xs_hbm.at[pl.ds(blk * BM, BM), :], zsem_ref.at[0])

    def for_zero_blocks(fn):
        for e in range(N_EXPERTS):
            @pl.when(zblk_ref[e] >= 0)
            def _(e=e):
                fn(zero_block(zblk_ref[e]))

        def body(blk, _):
            fn(zero_block(blk))
            return 0
        lax.fori_loop(n_used, nb, body, 0)

    for_zero_blocks(lambda c: c.start())
    for_zero_blocks(lambda c: c.wait())

    n_tiles = n_assign // (2 * TD)

    def load(tile, slot):
        return pltpu.make_async_copy(xp_hbm.at[pl.ds(tile * TD, TD), :], tbuf_ref.at[slot], lsem_ref.at[slot])

    def wait_scatter(slot):
        for _ in range(2):
            pltpu.make_async_copy(tbuf_ref.at[slot], tbuf_ref.at[slot], rsem_ref.at[slot]).wait()

    load(0, 0).start()

    def tile_step(i, _):
        slot = i % DISPATCH_BUFS
        nslot = (i + 1) % DISPATCH_BUFS

        @pl.when(i + 1 < n_tiles)
        def _():
            @pl.when(i + 1 >= DISPATCH_BUFS)
            def _():
                wait_scatter(nslot)
            load(i + 1, nslot).start()

        load(i, slot).wait()

        def body(j, _):
            for u in range(DISPATCH_UNROLL):
                r = j * DISPATCH_UNROLL + u
                for k in range(2):
                    d = dest_ref[(i * TD + r) * 2 + k]
                    pltpu.make_async_copy(_row(tbuf_ref.at[slot], r), _row(xs_hbm, d),
                                          rsem_ref.at[slot]).start(priority=k)
            return 0
        lax.fori_loop(0, TD // DISPATCH_UNROLL, body, 0)
        return 0

    lax.fori_loop(0, n_tiles, tile_step, 0)
    for back in range(min(DISPATCH_BUFS, n_tiles)):
        wait_scatter((n_tiles - 1 - back) % DISPATCH_BUFS)


def _dispatch(dest, zblk, meta, xp, n_rows):
    half = D_MODEL // 2
    assert dest.shape[0] % (2 * TD) == 0 and dest.shape[0] // (2 * TD) >= DISPATCH_BUFS
    grid_spec = pltpu.PrefetchScalarGridSpec(
        num_scalar_prefetch=3,
        grid=(1,),
        in_specs=[pl.BlockSpec(memory_space=pl.ANY)],
        out_specs=pl.BlockSpec(memory_space=pl.ANY),
        scratch_shapes=[
            pltpu.VMEM((BM, half), U32),
            pltpu.SemaphoreType.DMA((1,)),
            pltpu.VMEM((DISPATCH_BUFS, TD, half), U32),
            pltpu.SemaphoreType.DMA((DISPATCH_BUFS,)),
            pltpu.SemaphoreType.DMA((DISPATCH_BUFS,)),
        ],
    )
    return pl.pallas_call(
        _dispatch_kernel,
        grid_spec=grid_spec,
        out_shape=jax.ShapeDtypeStruct((n_rows, half), U32),
        compiler_params=_cparams(("arbitrary",)),
        name="dispatch",
    )(dest, zblk, meta, xp)


def _expert_kernel(be_ref, nxt_ref, par_ref, meta_ref, x_ref, wg_hbm, wu_hbm, wd_hbm, y_ref,
                   wsg_ref, wsu_ref, wsd_ref, wsem_ref, wgb_ref, wub_ref, wdb_ref):
    i = pl.program_id(0)
    n_used = meta_ref[0]
    half = D_MODEL // 2

    def weight_copies(e, p):
        return (pltpu.make_async_copy(wg_hbm.at[e], wsg_ref.at[p], wsem_ref.at[p, 0]),
                pltpu.make_async_copy(wu_hbm.at[e], wsu_ref.at[p], wsem_ref.at[p, 1]),
                pltpu.make_async_copy(wd_hbm.at[e], wsd_ref.at[p], wsem_ref.at[p, 2]))

    def succ(e):
        return jnp.where(e >= 0, nxt_ref[jnp.maximum(e, 0)], -1)

    def start_weights(e, p):
        @pl.when(e >= 0)
        def _():
            for c in weight_copies(e, p):
                c.start(priority=1)

    def wait_staged(p):
        for c in weight_copies(0, p):
            c.wait()

    def cast_staged(p):
        wgb_ref[p] = wsg_ref[p].astype(BF16)
        wub_ref[p] = wsu_ref[p].astype(BF16)
        wdb_ref[p] = wsd_ref[p].astype(BF16)

    @pl.when(i == 0)
    def _():
        e0 = be_ref[0]
        start_weights(e0, 0)
        start_weights(succ(e0), 1)
        wait_staged(0)
        cast_staged(0)
        start_weights(succ(succ(e0)), 0)

    def step(p, cast_next):
        if cast_next:
            wait_staged(1 - p)
        lo, hi = _unpack_bf16_pair(x_ref[...])
        xl = lo.astype(BF16)
        xh = hi.astype(BF16)
        gt = (jnp.dot(xl, wgb_ref[p, 0:half, :], preferred_element_type=F32)
              + jnp.dot(xh, wgb_ref[p, half:, :], preferred_element_type=F32))
        up = (jnp.dot(xl, wub_ref[p, 0:half, :], preferred_element_type=F32)
              + jnp.dot(xh, wub_ref[p, half:, :], preferred_element_type=F32))
        hb = (_silu(gt) * up).astype(BF16)
        y = jnp.dot(hb, wdb_ref[p], preferred_element_type=F32)
        y_ref[...] = _pack_bf16_pair(y[:, :half], y[:, half:])
        if cast_next:
            cast_staged(1 - p)
            start_weights(succ(succ(succ(be_ref[i]))), 1 - p)

    e = be_ref[jnp.minimum(i, n_used - 1)]
    ends_expert = (i + 1 < n_used) & (be_ref[jnp.minimum(i + 1, n_used - 1)] != e)
    for p in range(2):
        for cast_next in (False, True):
            @pl.when((i < n_used) & (par_ref[i] == p)
                     & (ends_expert if cast_next else jnp.logical_not(ends_expert)))
            def _(p=p, cast_next=cast_next):
                step(p, cast_next)

    @pl.when(i >= n_used)
    def _():
        y_ref[...] = jnp.zeros_like(y_ref)


def _experts(block_expert, next_expert, block_parity, meta, xs, w_gate, w_up, w_down):
    half = D_MODEL // 2
    n_rows = xs.shape[0]
    nb = n_rows // BM
    grid_spec = pltpu.PrefetchScalarGridSpec(
        num_scalar_prefetch=4,
        grid=(nb,),
        in_specs=[
            pl.BlockSpec((BM, half), lambda i, be, nxt, par, meta: (jnp.minimum(i, meta[0] - 1), 0)),
            pl.BlockSpec(memory_space=pl.ANY), pl.BlockSpec(memory_space=pl.ANY),
            pl.BlockSpec(memory_space=pl.ANY),
        ],
        out_specs=pl.BlockSpec((BM, half), lambda i, be, nxt, par, meta: (i, 0)),
        scratch_shapes=[
            pltpu.VMEM((2, D_MODEL, D_EXPERT), F32),
            pltpu.VMEM((2, D_MODEL, D_EXPERT), F32),
            pltpu.VMEM((2, D_EXPERT, D_MODEL), F32),
            pltpu.SemaphoreType.DMA((2, 3)),
            pltpu.VMEM((2, D_MODEL, D_EXPERT), BF16),
            pltpu.VMEM((2, D_MODEL, D_EXPERT), BF16),
            pltpu.VMEM((2, D_EXPERT, D_MODEL), BF16),
        ],
    )
    return pl.pallas_call(
        _expert_kernel,
        grid_spec=grid_spec,
        out_shape=jax.ShapeDtypeStruct((n_rows, half), U32),
        compiler_params=_cparams(("arbitrary",)),
        name="experts",
    )(block_expert, next_expert, block_parity, meta, xs, w_gate, w_up, w_down)


def _combine_kernel(dest_ref, y_hbm, x1_ref, rt_ref, fg_ref, o_ref, ya_ref, yb_ref, sem_ref):
    i = pl.program_id(0)
    nt = pl.num_programs(0)
    half = D_MODEL // 2
    bufs = (ya_ref, yb_ref)

    def row_copy(tile, slot, r_tile, s, k):
        r = r_tile * 8 + s
        d = dest_ref[(tile * TC + r) * 2 + k]
        return pltpu.make_async_copy(_row(y_hbm, d), _row(bufs[slot].at[k], r), sem_ref.at[slot])

    def wait_rows(slot):
        pltpu.make_async_copy(bufs[slot], bufs[slot], sem_ref.at[slot]).wait()

    def step(slot):
        wait_rows(slot)
        for r in range(TC):
            for k in range(2):
                row_copy(i + 1, 1 - slot, r // 8, r % 8, k).start(priority=k)
        rt = rt_ref[...]
        w0 = rt[:, 2:3]
        w1 = rt[:, 3:4]
        lo0, hi0 = _unpack_bf16_pair(bufs[slot][0])
        lo1, hi1 = _unpack_bf16_pair(bufs[slot][1])
        xl = x1_ref[:, :half] + (w0 * lo0 + w1 * lo1)
        xh = x1_ref[:, half:] + (w0 * hi0 + w1 * hi1)
        ms = (jnp.sum(xl * xl, axis=1, keepdims=True) + jnp.sum(xh * xh, axis=1, keepdims=True)) / float(D_MODEL)
        inv = lax.rsqrt(ms + NORM_EPS)
        o_ref[:, :half] = xl * inv * fg_ref[:, :half]
        o_ref[:, half:] = xh * inv * fg_ref[:, half:]

        @pl.when(i == nt - 1)
        def _():
            wait_rows(1 - slot)

    @pl.when(i == 0)
    def _():
        def body(rt, _):
            for s in range(8):
                for k in range(2):
                    row_copy(0, 0, rt, s, k).start(priority=k)
            return 0
        lax.fori_loop(0, TC // 8, body, 0)

    @pl.when(i % 2 == 0)
    def _():
        step(0)

    @pl.when(i % 2 == 1)
    def _():
        step(1)


def _combine(dest, ys, x1, route, fg):
    T = x1.shape[0]
    half = D_MODEL // 2
    grid_spec = pltpu.PrefetchScalarGridSpec(
        num_scalar_prefetch=1,
        grid=(T // TC,),
        in_specs=[
            pl.BlockSpec(memory_space=pl.ANY),
            pl.BlockSpec((TC, D_MODEL), lambda i, d: (i, 0)),
            pl.BlockSpec((TC, LANE), lambda i, d: (i, 0)),
            pl.BlockSpec((1, D_MODEL), lambda i, d: (0, 0)),
        ],
        out_specs=pl.BlockSpec((TC, D_MODEL), lambda i, d: (i, 0)),
        scratch_shapes=[
            pltpu.VMEM((2, TC, half), U32),
            pltpu.VMEM((2, TC, half), U32),
            pltpu.SemaphoreType.DMA((2,)),
        ],
    )
    return pl.pallas_call(
        _combine_kernel,
        grid_spec=grid_spec,
        out_shape=jax.ShapeDtypeStruct((T, D_MODEL), F32),
        compiler_params=_cparams(("arbitrary",)),
        name="combine",
    )(dest, ys, x1, route, fg)


def _cover_matrix(S):
    n_blk = S // CMP_STRIDE
    n_sel = S // SEL_BLOCK
    cs = np.arange(n_blk) * CMP_STRIDE
    ss = np.arange(n_sel) * SEL_BLOCK
    shared = np.minimum(cs[:, None] + CMP_LEN, ss[None, :] + SEL_BLOCK) - np.maximum(cs[:, None], ss[None, :])
    return (np.clip(shared, 0, None) / CMP_LEN).T.astype(np.float32)


def _block_mask_matrix(S):
    n_sel = S // SEL_BLOCK
    assert n_sel <= LANE
    e = np.zeros((S, LANE), np.float32)
    e[np.arange(S), np.arange(S) // SEL_BLOCK] = NEG_INF
    return e


def _inv_freq_row():
    inv = np.power(np.float32(ROPE_THETA), -np.arange(ROPE_HALF, dtype=np.float32) * 2.0 / ROPE_DIM)
    return np.tile(inv, ROPE_SLOTS).reshape(1, LANE).astype(np.float32)


def _packed_positions(positions, T):
    rows = TS_ROPE // ROPE_SLOTS
    p = positions.reshape(T // TS_ROPE, ROPE_SLOTS, rows).transpose(0, 2, 1).astype(F32)
    return jnp.repeat(p, ROPE_HALF, axis=2).reshape(T // ROPE_SLOTS, LANE)


def _layer(x2, positions, B, S, norm1_g, w_in, cmp_pe_k, cmp_pe_v, cmp_wk1, cmp_wk2, cmp_wv1, cmp_wv2,
           nsa_norm_g, conv_w, conv_b, b_igate, b_fgate, mlstm_norm_g, w_out, norm2_g,
           w_group, b_group, w_router, b_router, w_exp_gate, w_exp_up, w_exp_down, out_norm_g):
    T = B * S
    w_main, w_small = _weight_prep(w_in)
    posb = _packed_positions(positions, T)
    invf = jnp.asarray(_inv_freq_row())

    proj, small = _in_proj(x2, norm1_g.reshape(1, -1), w_main, w_small)
    q_r, k_r = _rope(posb, invf, proj)
    kc, vc = _compress(k_r, proj, B, S, cmp_pe_k, cmp_pe_v, cmp_wk1.astype(BF16), cmp_wk2.astype(BF16),
                       cmp_wv1.astype(BF16), cmp_wv2.astype(BF16))
    eye = np.eye(S // SEL_BLOCK, LANE, dtype=np.float32)
    nsa_o = _nsa(q_r, k_r, proj, small, kc, vc, jnp.asarray(_cover_matrix(S), BF16),
                 jnp.asarray(_block_mask_matrix(S), BF16), jnp.asarray(eye, BF16),
                 nsa_norm_g.reshape(1, -1), B, S)

    if_arr = small[:, 24:32].reshape(B, S, 2, ML_HEADS).transpose(0, 3, 2, 1)
    gate_bias = jnp.stack([b_igate, b_fgate]).astype(F32)
    ml_o = _mlstm(proj, if_arr, gate_bias, conv_w, conv_b.reshape(1, -1), mlstm_norm_g.reshape(1, -1), B, S)

    w_r = jnp.concatenate([w_group, w_router, jnp.zeros((D_MODEL, LANE - MOE_GROUPS - N_EXPERTS), F32)],
                          axis=1).astype(BF16)
    b_r = jnp.concatenate([b_group, b_router, jnp.zeros((LANE - MOE_GROUPS - N_EXPERTS,), F32)]).reshape(1, LANE)
    tril = jnp.asarray(np.tril(np.ones((TM_OUT, TM_OUT), np.float32)), BF16)
    x1, xp, route, cnt = _out_proj(nsa_o, ml_o, w_out.astype(BF16), x2, norm2_g.reshape(1, -1), w_r, b_r, tril)

    n_rows = T * 2 + N_EXPERTS * BM
    counts = cnt[0, :N_EXPERTS].astype(jnp.int32)
    pcounts = (counts + BM - 1) // BM * BM
    pends = jnp.cumsum(pcounts)
    pstarts = pends - pcounts
    eid = route[:, 0:2].astype(jnp.int32)
    onehot = (eid[..., None] == jnp.arange(N_EXPERTS, dtype=jnp.int32)).astype(F32)
    start_blk = jnp.einsum('tke,e->tk', onehot, (pstarts // BM).astype(F32)).astype(jnp.int32)
    dest = (start_blk * BM + route[:, 4:6].astype(jnp.int32)).reshape(T * 2)
    block_expert = jnp.minimum(jnp.searchsorted(pends, jnp.arange(n_rows // BM) * BM, side='right'),
                               N_EXPERTS - 1).astype(jnp.int32)
    present = jnp.where(counts > 0, jnp.arange(N_EXPERTS, dtype=jnp.int32), N_EXPERTS)
    later = jnp.concatenate([lax.cummin(present[::-1])[::-1][1:], jnp.full((1,), N_EXPERTS, jnp.int32)])
    next_expert = jnp.where(later < N_EXPERTS, later, -1).astype(jnp.int32)
    meta = (pends[-1:] // BM).astype(jnp.int32)

    last_block = jnp.where(counts > 0, pends // BM - 1, -1).astype(jnp.int32)

    xs = _dispatch(dest, last_block, meta, xp, n_rows)
    ordinal = jnp.cumsum((counts > 0).astype(jnp.int32)) - 1
    block_parity = (ordinal[block_expert] % 2).astype(jnp.int32)
    ys = _experts(block_expert, next_expert, block_parity, meta, xs, w_exp_gate, w_exp_up, w_exp_down)
    dest_pad = jnp.concatenate([dest, jnp.zeros((2 * TC,), jnp.int32)])
    return _combine(dest_pad, ys, x1, route, out_norm_g.reshape(1, -1))


def kernel(x, positions, norm1_g, w_in, cmp_pe_k, cmp_pe_v, cmp_wk1, cmp_wk2, cmp_wv1, cmp_wv2, nsa_norm_g,
           conv_w, conv_b, b_igate, b_fgate, mlstm_norm_g, w_out, norm2_g, w_group, b_group, w_router,
           b_router, w_exp_gate, w_exp_up, w_exp_down, final_norm_g):
    B, S, D = x.shape
    assert D == D_MODEL and norm1_g.shape[0] == 1, "single-layer, D_MODEL-wide configuration only"
    assert S % ML_CHUNK == 0 and S % SEL_BUCKET == 0 and S >= WIN_KEYS and (B * S) % TM_IN == 0
    out = _layer(x.reshape(B * S, D), positions, B, S, norm1_g[0], w_in[0], cmp_pe_k[0], cmp_pe_v[0],
                 cmp_wk1[0], cmp_wk2[0], cmp_wv1[0], cmp_wv2[0], nsa_norm_g[0], conv_w[0], conv_b[0],
                 b_igate[0], b_fgate[0], mlstm_norm_g[0], w_out[0], norm2_g[0], w_group[0], b_group[0],
                 w_router[0], b_router[0], w_exp_gate[0], w_exp_up[0], w_exp_down[0], final_norm_g)
    return out.reshape(B, S, D)
```

```python
import functools

import numpy as np
import jax
import jax.numpy as jnp
from jax import lax
from jax.experimental import pallas as pl
from jax.experimental.pallas import tpu as pltpu

F32 = jnp.float32
BF16 = jnp.bfloat16
U32 = jnp.uint32

D_MODEL = 2048
NSA_HEADS = 8
NSA_GROUPS = 2
NSA_J = NSA_HEADS // NSA_GROUPS
HD = 128
CMP_LEN = 32
CMP_STRIDE = 16
SEL_BLOCK = 64
SEL_TOPK = 8
WINDOW = 512
ROPE_THETA = 500000.0
ROPE_DIM = 32
ROPE_HALF = 16
ML_HEADS = 4
ML_HD = 256
CONV_W = 4
MOE_GROUPS = 8
EPG = 8
N_EXPERTS = 64
D_EXPERT = 512
SEL_BLOCK_LOG2 = 6
EPG_LOG2 = 3
assert (1 << SEL_BLOCK_LOG2) == SEL_BLOCK and (1 << EPG_LOG2) == EPG
NORM_EPS = 1e-6
NEG_INF = -1e30
FORCE_SCORE = 1000.0

COL_Q = 0
COL_KV = 1024
COL_QKM = 2560
COL_VM = 4608
COL_OM = 5632
N_MAIN = 6656

LANE = 128
VMEM_LIMIT = 56 * 1024 * 1024

TM_IN = 1024
TN_IN = 1664
TS_ROPE = 256
TQ = 128
SEL_BUCKET = 256
WIN_KEYS = WINDOW + TQ
ML_CHUNK = 256
TM_OUT = 512
BM = 256
TC = 256

NT_DIMS = (((1,), (1,)), ((), ()))
TN_DIMS = (((0,), (0,)), ((), ()))


def _cparams(sem):
    return pltpu.CompilerParams(dimension_semantics=sem, vmem_limit_bytes=VMEM_LIMIT)


def _sigmoid(x):
    return 0.5 * jnp.tanh(0.5 * x) + 0.5


def _silu(x):
    h = 0.5 * x
    return h + h * jnp.tanh(h)


W_GATES = 2560
W_QKM = 2584
W_IGATE = 6680
W_END = 6688
TR_PREP = 512
assert W_GATES % TR_PREP == 0 and N_MAIN % TR_PREP == 0 and W_IGATE - W_QKM == N_MAIN - W_GATES


def _wprep_kernel(wt_hbm, wm_ref, ws_ref, wbuf_ref, sbuf_ref, sem_ref, ssem_ref):
    i = pl.program_id(0)
    n_gate = W_QKM - W_GATES
    n_if = W_END - W_IGATE

    def load(blk, slot):
        start = pl.multiple_of(blk * TR_PREP + jnp.where(blk * TR_PREP >= W_GATES, n_gate, 0), 8)
        return pltpu.make_async_copy(wt_hbm.at[pl.ds(start, TR_PREP), :], wbuf_ref.at[slot], sem_ref.at[slot])

    def small_loads():
        return (pltpu.make_async_copy(wt_hbm.at[pl.ds(W_GATES, n_gate), :], sbuf_ref.at[pl.ds(0, n_gate), :],
                                      ssem_ref.at[0]),
                pltpu.make_async_copy(wt_hbm.at[pl.ds(W_IGATE, n_if), :], sbuf_ref.at[pl.ds(n_gate, n_if), :],
                                      ssem_ref.at[1]))

    @pl.when(i == 0)
    def _():
        load(0, 0).start()
        for c in small_loads():
            c.start()
        for c in small_loads():
            c.wait()
        ws_ref[...] = jnp.zeros_like(ws_ref)
        ws_ref[0:n_gate + n_if, :] = sbuf_ref[...].astype(BF16)

    @pl.when(i + 1 < pl.num_programs(0))
    def _():
        load(i + 1, (i + 1) % 2).start()

    load(i, i % 2).wait()
    wm_ref[...] = wbuf_ref[i % 2].astype(BF16)


def _weight_prep(wt):
    assert wt.shape == (W_END, D_MODEL)
    n_small = (W_QKM - W_GATES) + (W_END - W_IGATE)
    return pl.pallas_call(
        _wprep_kernel,
        grid=(N_MAIN // TR_PREP,),
        in_specs=[pl.BlockSpec(memory_space=pl.ANY)],
        out_specs=[pl.BlockSpec((TR_PREP, D_MODEL), lambda i: (i, 0)),
                   pl.BlockSpec((LANE, D_MODEL), lambda i: (0, 0))],
        out_shape=[jax.ShapeDtypeStruct((N_MAIN, D_MODEL), BF16),
                   jax.ShapeDtypeStruct((LANE, D_MODEL), BF16)],
        scratch_shapes=[pltpu.VMEM((2, TR_PREP, D_MODEL), F32), pltpu.VMEM((n_small, D_MODEL), F32),
                        pltpu.SemaphoreType.DMA((2,)), pltpu.SemaphoreType.DMA((2,))],
        compiler_params=_cparams(("arbitrary",)),
        name="weight_prep",
    )(wt)


def _inproj_kernel(x_ref, g_ref, w_ref, ws_ref, o_ref, os_ref, h_ref):
    @pl.when(pl.program_id(1) == 0)
    def _():
        x = x_ref[...]
        ms = jnp.mean(x * x, axis=-1, keepdims=True)
        h_ref[...] = (x * lax.rsqrt(ms + NORM_EPS) * g_ref[...]).astype(BF16)
        os_ref[...] = lax.dot_general(h_ref[...], ws_ref[...], NT_DIMS, preferred_element_type=F32)

    o_ref[...] = lax.dot_general(h_ref[...], w_ref[...], NT_DIMS,
                                 preferred_element_type=F32).astype(o_ref.dtype)


def _in_proj(x2, g1, w_main, w_small):
    T = x2.shape[0]
    return pl.pallas_call(
        _inproj_kernel,
        grid=(T // TM_IN, N_MAIN // TN_IN),
        in_specs=[
            pl.BlockSpec((TM_IN, D_MODEL), lambda m, n: (m, 0)),
            pl.BlockSpec((1, D_MODEL), lambda m, n: (0, 0)),
            pl.BlockSpec((TN_IN, D_MODEL), lambda m, n: (n, 0)),
            pl.BlockSpec((LANE, D_MODEL), lambda m, n: (0, 0)),
        ],
        out_specs=[
            pl.BlockSpec((TM_IN, TN_IN), lambda m, n: (m, n)),
            pl.BlockSpec((TM_IN, LANE), lambda m, n: (m, 0)),
        ],
        out_shape=[
            jax.ShapeDtypeStruct((T, N_MAIN), BF16),
            jax.ShapeDtypeStruct((T, LANE), F32),
        ],
        scratch_shapes=[pltpu.VMEM((TM_IN, D_MODEL), BF16)],
        compiler_params=_cparams(("parallel", "arbitrary")),
        name="in_proj",
    )(x2, g1, w_main, w_small)


ROPE_SLOTS = LANE // ROPE_HALF


def _rope_kernel(pos_ref, invf_ref, q_ref, kc_ref, ks_ref, kw_ref, qo_ref, ko_ref):
    ang = pos_ref[...] * invf_ref[...]
    c = jnp.cos(ang)
    s = jnp.sin(ang)
    rows = ang.shape[0]
    lane = lax.broadcasted_iota(jnp.int32, ang.shape, 1)
    lo = lane < ROPE_HALF
    mid = lane < ROPE_DIM
    scale = HD ** -0.5

    def lanes_from(x, src):
        shift = (-src) % LANE
        return x if shift == 0 else pltpu.roll(x, shift, 1)

    for slot in range(ROPE_SLOTS):
        src = slot * ROPE_HALF
        cf = jnp.where(lo, lanes_from(c, src), jnp.where(mid, lanes_from(c, src - ROPE_HALF), 1.0))
        sa = jnp.where(lo, -lanes_from(s, src), 0.0)
        sb = jnp.where(lo, 0.0, jnp.where(mid, lanes_from(s, src - ROPE_HALF), 0.0))
        rs = slice(slot * rows, (slot + 1) * rows)

        def rope(x):
            return x * cf + pltpu.roll(x, LANE - ROPE_HALF, 1) * sa + pltpu.roll(x, ROPE_HALF, 1) * sb

        for h in range(NSA_HEADS):
            sl = slice(h * HD, (h + 1) * HD)
            qo_ref[rs, sl] = (rope(q_ref[rs, sl].astype(F32)) * scale).astype(BF16)
        for i, r in enumerate((kc_ref, ks_ref, kw_ref)):
            for g in range(NSA_GROUPS):
                sl = slice(g * HD, (g + 1) * HD)
                so = slice(i * 2 * HD + g * HD, i * 2 * HD + (g + 1) * HD)
                ko_ref[rs, so] = rope(r[rs, sl].astype(F32)).astype(BF16)


def _rope(posb, invf, proj):
    T = proj.shape[0]
    kvb = COL_KV // 256
    return pl.pallas_call(
        _rope_kernel,
        grid=(T // TS_ROPE,),
        in_specs=[
            pl.BlockSpec((TS_ROPE // ROPE_SLOTS, LANE), lambda i: (i, 0)),
            pl.BlockSpec((1, LANE), lambda i: (0, 0)),
            pl.BlockSpec((TS_ROPE, 1024), lambda i: (i, 0)),
            pl.BlockSpec((TS_ROPE, 256), lambda i: (i, kvb + 0)),
            pl.BlockSpec((TS_ROPE, 256), lambda i: (i, kvb + 2)),
            pl.BlockSpec((TS_ROPE, 256), lambda i: (i, kvb + 4)),
        ],
        out_specs=[
            pl.BlockSpec((TS_ROPE, 1024), lambda i: (i, 0)),
            pl.BlockSpec((TS_ROPE, 768), lambda i: (i, 0)),
        ],
        out_shape=[
            jax.ShapeDtypeStruct((T, 1024), BF16),
            jax.ShapeDtypeStruct((T, 768), BF16),
        ],
        compiler_params=_cparams(("parallel",)),
        name="rope",
    )(posb, invf, proj, proj, proj, proj)


def _compress_kernel(k_ref, v_ref, pek_ref, pev_ref, w1k_ref, w2k_ref, w1v_ref, w2v_ref,
                     kc_ref, vc_ref, xs_ref):
    S = k_ref.shape[0]
    n_blk = S // CMP_STRIDE
    for src, pe, w1, w2, dst in ((k_ref, pek_ref, w1k_ref, w2k_ref, kc_ref),
                                 (v_ref, pev_ref, w1v_ref, w2v_ref, vc_ref)):
        xs_ref[0:S, :] = src[...].astype(F32)
        xs_ref[S:S + CMP_LEN, :] = jnp.zeros((CMP_LEN, HD), F32)
        acc = jnp.zeros((n_blk, HD), F32)
        for l in range(CMP_LEN):
            a = xs_ref[pl.ds(l, n_blk, stride=CMP_STRIDE), :] + pe[l:l + 1, :]
            acc = acc + jnp.dot(a.astype(BF16), w1[l * HD:(l + 1) * HD, :],
                                preferred_element_type=F32)
        hid = _silu(acc)
        out = jnp.dot(hid.astype(BF16), w2[...], preferred_element_type=F32)
        dst[0, 0] = out.astype(BF16)


def _compress(k_r, proj, B, S, pek, pev, w1k, w2k, w1v, w2v):
    n_blk = S // CMP_STRIDE
    vcol = (COL_KV + 256) // HD
    full = lambda shape: pl.BlockSpec(shape, lambda b, g: tuple(0 for _ in shape))
    return pl.pallas_call(
        _compress_kernel,
        grid=(B, NSA_GROUPS),
        in_specs=[
            pl.BlockSpec((S, HD), lambda b, g: (b, g)),
            pl.BlockSpec((S, HD), lambda b, g: (b, vcol + g)),
            full((CMP_LEN, HD)), full((CMP_LEN, HD)),
            full((CMP_LEN * HD, HD)), full((HD, HD)),
            full((CMP_LEN * HD, HD)), full((HD, HD)),
        ],
        out_specs=[
            pl.BlockSpec((1, 1, n_blk, HD), lambda b, g: (b, g, 0, 0)),
            pl.BlockSpec((1, 1, n_blk, HD), lambda b, g: (b, g, 0, 0)),
        ],
        out_shape=[
            jax.ShapeDtypeStruct((B, NSA_GROUPS, n_blk, HD), BF16),
            jax.ShapeDtypeStruct((B, NSA_GROUPS, n_blk, HD), BF16),
        ],
        scratch_shapes=[pltpu.VMEM((S + CMP_LEN, HD), F32)],
        compiler_params=_cparams(("parallel", "parallel")),
        name="compress",
    )(k_r, proj, pek, pev, w1k, w2k, w1v, w2v)


def _nsa_kernel(q_ref, kc_ref, vc_ref, ks_ref, vs_ref, kw_ref, vw_ref, sm_ref, covt_ref, eneg_ref, eye_ref,
                ng_ref, o_ref, obuf_ref, owin_ref, osel_ref):
    qi = pl.program_id(1)
    q0 = qi * TQ
    R = NSA_J * TQ
    n_cmp = kc_ref.shape[2]
    n_sel = ks_ref.shape[0] // SEL_BLOCK
    n_win = WIN_KEYS // TQ

    def row_t(shape):
        r = lax.broadcasted_iota(jnp.int32, shape, 0)
        return q0 + (r & (TQ - 1))

    qgs = [jnp.concatenate([q_ref[:, (g * NSA_J + j) * HD:(g * NSA_J + j + 1) * HD]
                            for j in range(NSA_J)], axis=0) for g in range(NSA_GROUPS)]

    def with_ones(v):
        lane = lax.broadcasted_iota(jnp.int32, v.shape, 1)
        return jnp.concatenate([v, jnp.where(lane == 0, 1.0, 0.0).astype(BF16)], axis=1)

    def window(masks):
        w0 = pl.multiple_of(jnp.maximum(q0 - WINDOW, 0), TQ)
        for g in range(NSA_GROUPS):
            kt = kw_ref[pl.ds(w0, WIN_KEYS), g * HD:(g + 1) * HD]
            vt = with_ones(vw_ref[pl.ds(w0, WIN_KEYS), g * HD:(g + 1) * HD])
            sc = masks(lax.dot_general(qgs[g], kt, NT_DIMS, preferred_element_type=F32), w0)
            mw = jnp.max(sc, axis=1, keepdims=True)
            pw = jnp.exp((sc - mw).astype(BF16))
            acc = jnp.dot(pw, vt, preferred_element_type=F32)
            owin_ref[g] = acc[:, 0:HD] / acc[:, HD:HD + 1]

    def band_masks(sc, w0):
        d = (lax.broadcasted_iota(jnp.int32, (R, TQ), 1)
             - (lax.broadcasted_iota(jnp.int32, (R, TQ), 0) & (TQ - 1)))
        first = jnp.where(d > 0, sc[:, 0:TQ], NEG_INF)
        last = jnp.where(d <= 0, sc[:, (n_win - 1) * TQ:], NEG_INF)
        return jnp.concatenate([first, sc[:, TQ:(n_win - 1) * TQ], last], axis=1)

    def general_masks(sc, w0):
        diff = row_t((R, WIN_KEYS)) - (w0 + lax.broadcasted_iota(jnp.int32, (R, WIN_KEYS), 1))
        keep = jnp.where(diff >= 0, diff, WINDOW) < WINDOW
        return jnp.where(keep, sc, NEG_INF)

    window(general_masks)

    gates = _sigmoid(sm_ref[...])
    o_cmps = []
    qps = []
    for g in range(NSA_GROUPS):
        qg = qgs[g]

        s = lax.dot_general(qg, kc_ref[0, g], NT_DIMS, preferred_element_type=F32)
        n_lane = lax.broadcasted_iota(jnp.int32, (R, n_cmp), 1)
        cmask = (n_lane * CMP_STRIDE + (CMP_LEN - 1)) <= row_t((R, n_cmp))
        s = jnp.where(cmask, s, NEG_INF)
        m = jnp.max(s, axis=1, keepdims=True)
        e = jnp.where(cmask, jnp.exp(s - m), 0.0)
        l = jnp.sum(e, axis=1, keepdims=True)
        p = (e / jnp.where(l > 0.0, l, 1.0)).astype(BF16)
        o_cmps.append(jnp.dot(p, vc_ref[0, g], preferred_element_type=F32))
        impr = lax.dot_general(covt_ref[...], p, NT_DIMS, preferred_element_type=F32)
        imp = impr[:, 0:TQ]
        for j in range(1, NSA_J):
            imp = imp + impr[:, j * TQ:(j + 1) * TQ]

        m_sub = lax.broadcasted_iota(jnp.int32, (n_sel, TQ), 0)
        jt = (q0 + lax.broadcasted_iota(jnp.int32, (n_sel, TQ), 1)) >> SEL_BLOCK_LOG2
        forced = jnp.where(m_sub == 0, FORCE_SCORE,
                           jnp.where(m_sub == jt, FORCE_SCORE,
                                     jnp.where(m_sub == jt - 1, FORCE_SCORE, 0.0)))
        score = jnp.where(m_sub <= jt, imp + forced, -1.0)
        rank = jnp.zeros((n_sel, TQ), F32)
        for mp in range(n_sel):
            row = score[mp:mp + 1, :]
            ge = jnp.where(row >= score, 1.0, 0.0)
            gt = jnp.where(row > score, 1.0, 0.0)
            rank = rank + jnp.where(m_sub > mp, ge, gt)
        notsel_t = jnp.where(rank < float(min(SEL_TOPK, n_sel)), 0.0, 1.0).astype(BF16)
        notsel = lax.dot_general(notsel_t, eye_ref[...], TN_DIMS,
                                 preferred_element_type=F32).astype(BF16)
        qps.append(jnp.concatenate([qg, jnp.concatenate([notsel] * NSA_J, axis=0)], axis=1))

    def sel_oneshot(nk):
        for g in range(NSA_GROUPS):
            kt = jnp.concatenate([ks_ref[0:nk, g * HD:(g + 1) * HD], eneg_ref[0:nk, :]], axis=1)
            vt = with_ones(vs_ref[0:nk, g * HD:(g + 1) * HD])
            sc = lax.dot_general(qps[g], kt, NT_DIMS, preferred_element_type=F32)
            klane = lax.broadcasted_iota(jnp.int32, (R, SEL_BUCKET), 1) + (nk - SEL_BUCKET)
            tail = jnp.where(klane <= row_t((R, SEL_BUCKET)), sc[:, nk - SEL_BUCKET:], NEG_INF)
            sc = tail if nk == SEL_BUCKET else jnp.concatenate([sc[:, 0:nk - SEL_BUCKET], tail], axis=1)
            ms = jnp.max(sc, axis=1, keepdims=True)
            acc = jnp.dot(jnp.exp((sc - ms).astype(BF16)), vt, preferred_element_type=F32)
            osel_ref[g] = acc[:, 0:HD] / acc[:, HD:HD + 1]

    for b in range(ks_ref.shape[0] // SEL_BUCKET):
        @pl.when(q0 // SEL_BUCKET == b)
        def _(b=b):
            sel_oneshot((b + 1) * SEL_BUCKET)

    ssq = jnp.zeros((TQ, 1), F32)
    for g in range(NSA_GROUPS):
        o_sel = osel_ref[g]
        o_cmp = o_cmps[g]
        o_win = owin_ref[g]
        for j in range(NSA_J):
            h = g * NSA_J + j
            rs = slice(j * TQ, (j + 1) * TQ)
            o = (gates[:, 3 * h:3 * h + 1] * o_cmp[rs] + gates[:, 3 * h + 1:3 * h + 2] * o_sel[rs]
                 + gates[:, 3 * h + 2:3 * h + 3] * o_win[rs])
            ssq = ssq + jnp.sum(o * o, axis=1, keepdims=True)
            obuf_ref[:, h * HD:(h + 1) * HD] = o

    inv = lax.rsqrt(ssq / float(NSA_HEADS * HD) + NORM_EPS)
    o_ref[...] = (obuf_ref[...] * inv * ng_ref[...]).astype(BF16)


def _nsa(q_r, k_r, proj, small, kc, vc, covt, eneg, eye, ng, B, S):
    T = B * S
    nq = S // TQ
    n_blk = S // CMP_STRIDE
    n_sel = S // SEL_BLOCK
    kvb = COL_KV // 256
    return pl.pallas_call(
        _nsa_kernel,
        grid=(B, nq),
        in_specs=[
            pl.BlockSpec((TQ, 1024), lambda b, i: (b * nq + i, 0)),
            pl.BlockSpec((1, NSA_GROUPS, n_blk, HD), lambda b, i: (b, 0, 0, 0)),
            pl.BlockSpec((1, NSA_GROUPS, n_blk, HD), lambda b, i: (b, 0, 0, 0)),
            pl.BlockSpec((S, 256), lambda b, i: (b, 1)),
            pl.BlockSpec((S, 256), lambda b, i: (b, kvb + 3)),
            pl.BlockSpec((S, 256), lambda b, i: (b, 2)),
            pl.BlockSpec((S, 256), lambda b, i: (b, kvb + 5)),
            pl.BlockSpec((TQ, LANE), lambda b, i: (b * nq + i, 0)),
            pl.BlockSpec((n_sel, n_blk), lambda b, i: (0, 0)),
            pl.BlockSpec((S, LANE), lambda b, i: (0, 0)),
            pl.BlockSpec((n_sel, LANE), lambda b, i: (0, 0)),
            pl.BlockSpec((1, 1024), lambda b, i: (0, 0)),
        ],
        out_specs=pl.BlockSpec((TQ, 1024), lambda b, i: (b * nq + i, 0)),
        out_shape=jax.ShapeDtypeStruct((T, 1024), BF16),
        scratch_shapes=[pltpu.VMEM((TQ, 1024), F32), pltpu.VMEM((NSA_GROUPS, NSA_J * TQ, HD), F32),
                        pltpu.VMEM((NSA_GROUPS, NSA_J * TQ, HD), F32)],
        compiler_params=_cparams(("parallel", "parallel")),
        name="nsa",
    )(q_r, kc, vc, k_r, proj, k_r, proj, small, covt, eneg, eye, ng)


ML_HPS = 2


def _mlstm_kernel(bias_ref, q_ref, k_ref, v_ref, om_ref, if_ref, cwq_ref, cwk_ref, cbq_ref, cbk_ref,
                  ng_ref, o_ref, c_ref, n_ref, m_ref, xq_ref, xk_ref):
    hp = pl.program_id(1)
    L = ML_CHUNK

    @pl.when(pl.program_id(2) == 0)
    def _():
        c_ref[...] = jnp.zeros_like(c_ref)
        n_ref[...] = jnp.zeros_like(n_ref)
        m_ref[...] = jnp.zeros_like(m_ref)
        xq_ref[:, 0:8, :] = jnp.zeros((ML_HPS, 8, ML_HD), F32)
        xk_ref[:, 0:8, :] = jnp.zeros((ML_HPS, 8, ML_HD), F32)

    def conv_silu(x_ref, hh, xb_ref, w_ref, b_ref):
        cs = slice(hh * ML_HD, (hh + 1) * ML_HD)
        xb_ref[hh, 8:8 + L, :] = x_ref[:, cs].astype(F32)
        y = b_ref[:, cs] + xb_ref[hh, 8:8 + L, :] * w_ref[CONV_W - 1:CONV_W, cs]
        for k in range(1, CONV_W):
            y = y + xb_ref[hh, 8 - k:8 - k + L, :] * w_ref[CONV_W - 1 - k:CONV_W - k, cs]
        xb_ref[hh, 0:8, :] = xb_ref[hh, L:L + 8, :]
        return _silu(y)

    r = lax.broadcasted_iota(jnp.int32, (L, L), 0)
    cidx = lax.broadcasted_iota(jnp.int32, (L, L), 1)
    tril = cidx <= r
    eye = cidx == r

    for hh in range(ML_HPS):
        h = hp * ML_HPS + hh
        cs = slice(hh * ML_HD, (hh + 1) * ML_HD)
        qf = conv_silu(q_ref, hh, xq_ref, cwq_ref, cbq_ref) * (ML_HD ** -0.5)
        kf = conv_silu(k_ref, hh, xk_ref, cwk_ref, cbk_ref)
        qb = qf.astype(BF16)
        kb = kf.astype(BF16)
        vb = v_ref[:, cs]
        vf = vb.astype(F32)

        ic = if_ref[0, hh, 0:1, :] + bias_ref[0, h]
        fp = if_ref[0, hh, 1:2, :] + bias_ref[1, h]
        fc = jnp.minimum(fp, 0.0) - jnp.log(1.0 + jnp.exp(-jnp.abs(fp)))

        fc_b = jnp.broadcast_to(fc, (L, L))
        ic_b = jnp.broadcast_to(ic, (L, L))
        b_col = jnp.sum(jnp.where(tril, fc_b, 0.0), axis=1, keepdims=True)
        fc_col = jnp.sum(jnp.where(eye, fc_b, 0.0), axis=1, keepdims=True)
        ic_col = jnp.sum(jnp.where(eye, ic_b, 0.0), axis=1, keepdims=True)
        b_row = jnp.sum(jnp.where(r <= cidx, jnp.broadcast_to(fc_col, (L, L)), 0.0),
                        axis=0, keepdims=True)
        b_last = b_col[L - 1:L, :]
        m_prev = m_ref[hh]

        d_log = jnp.where(tril, b_col - b_row + ic, NEG_INF)
        inter = b_col + m_prev
        m_t = jnp.maximum(inter, jnp.max(d_log, axis=1, keepdims=True))
        w_intra = jnp.exp(d_log - m_t)
        w_inter = jnp.exp(inter - m_t)
        qk = lax.dot_general(qb, kb, NT_DIMS, preferred_element_type=F32) * w_intra
        num = (jnp.dot(qk.astype(BF16), vb, preferred_element_type=F32)
               + w_inter * lax.dot_general(qb, c_ref[hh].astype(BF16), NT_DIMS, preferred_element_type=F32))
        den = jnp.sum(qk, axis=1, keepdims=True) + w_inter * jnp.sum(qf * n_ref[hh], axis=1, keepdims=True)
        hm = num / jnp.maximum(jnp.abs(den), jnp.exp(-m_t))

        w_log = b_last - b_col + ic_col
        m_new = jnp.maximum(b_last + m_prev, jnp.max(w_log, axis=0, keepdims=True))
        w_state = jnp.exp(w_log - m_new)
        decay = jnp.exp(b_last + m_prev - m_new)
        c_ref[hh] = decay * c_ref[hh] + lax.dot_general((w_state * vf).astype(BF16), kb, TN_DIMS,
                                                        preferred_element_type=F32)
        n_ref[hh] = decay * n_ref[hh] + jnp.sum(w_state * kf, axis=0, keepdims=True)
        m_ref[hh] = m_new

        hn = hm * lax.rsqrt(jnp.mean(hm * hm, axis=1, keepdims=True) + NORM_EPS) * ng_ref[:, cs]
        o_ref[:, cs] = (hn * _sigmoid(om_ref[:, cs].astype(F32))).astype(BF16)


def _mlstm(proj, if_arr, gate_bias, conv_w, conv_b, ng, B, S):
    T = B * S
    nc = S // ML_CHUNK
    W = ML_HPS * ML_HD
    n_hp = ML_HEADS // ML_HPS
    assert COL_QKM % W == 0 and COL_VM % W == 0 and COL_OM % W == 0 and ML_HEADS % ML_HPS == 0
    cq, ck, cv, co = COL_QKM // W, COL_QKM // W + n_hp, COL_VM // W, COL_OM // W
    rows = lambda col0: pl.BlockSpec((ML_CHUNK, W), lambda b, h, c: (b * nc + c, col0 + h))
    return pl.pallas_call(
        _mlstm_kernel,
        grid=(B, n_hp, nc),
        in_specs=[
            pl.BlockSpec(memory_space=pltpu.SMEM),
            rows(cq), rows(ck), rows(cv), rows(co),
            pl.BlockSpec((1, ML_HPS, 2, ML_CHUNK), lambda b, h, c: (b, h, 0, c)),
            pl.BlockSpec((CONV_W, W), lambda b, h, c: (0, h)),
            pl.BlockSpec((CONV_W, W), lambda b, h, c: (0, n_hp + h)),
            pl.BlockSpec((1, W), lambda b, h, c: (0, h)),
            pl.BlockSpec((1, W), lambda b, h, c: (0, n_hp + h)),
            pl.BlockSpec((1, W), lambda b, h, c: (0, h)),
        ],
        out_specs=pl.BlockSpec((ML_CHUNK, W), lambda b, h, c: (b * nc + c, h)),
        out_shape=jax.ShapeDtypeStruct((T, ML_HEADS * ML_HD), BF16),
        scratch_shapes=[
            pltpu.VMEM((ML_HPS, ML_HD, ML_HD), F32), pltpu.VMEM((ML_HPS, 1, ML_HD), F32),
            pltpu.VMEM((ML_HPS, 1, 1), F32),
            pltpu.VMEM((ML_HPS, ML_CHUNK + 8, ML_HD), F32), pltpu.VMEM((ML_HPS, ML_CHUNK + 8, ML_HD), F32),
        ],
        compiler_params=_cparams(("parallel", "parallel", "arbitrary")),
        name="mlstm",
    )(gate_bias, proj, proj, proj, proj, if_arr, conv_w, conv_w, conv_b, conv_b, ng)


def _pack_bf16_pair(lo, hi):
    lo_b = pltpu.bitcast(lo.astype(BF16).astype(F32), U32)
    hi_b = pltpu.bitcast(hi.astype(BF16).astype(F32), U32)
    return (lo_b >> 16) | hi_b


def _unpack_bf16_pair(p):
    lo = pltpu.bitcast(p << 16, F32)
    hi = pltpu.bitcast(p & jnp.uint32(0xFFFF0000), F32)
    return lo, hi


def _row(ref, r):
    return ref.at[pl.ds(r, 1), :]


def _outproj_kernel(nsa_ref, ml_ref, w_ref, x_ref, g2_ref, wr_ref, br_ref, tril_ref,
                    x1_ref, xp_ref, rt_ref, cnt_ref, carry_ref):
    @pl.when(pl.program_id(0) == 0)
    def _():
        carry_ref[...] = jnp.zeros_like(carry_ref)

    half = D_MODEL // 2
    acc = jnp.dot(nsa_ref[...], w_ref[0:half, :], preferred_element_type=F32)
    acc = acc + jnp.dot(ml_ref[...], w_ref[half:, :], preferred_element_type=F32)
    x1 = x_ref[...] + acc
    x1_ref[...] = x1
    xn = x1 * lax.rsqrt(jnp.mean(x1 * x1, axis=-1, keepdims=True) + NORM_EPS) * g2_ref[...]
    xp_ref[...] = _pack_bf16_pair(xn[:, :half], xn[:, half:])
    logits = jnp.dot(xn.astype(BF16), wr_ref[...], preferred_element_type=F32) + br_ref[...]

    tm = logits.shape[0]
    lane = lax.broadcasted_iota(jnp.int32, (tm, LANE), 1)
    lane_f = lane.astype(F32)
    big = float(LANE)
    gmask = lane < MOE_GROUPS
    gmax = jnp.max(jnp.where(gmask, logits, NEG_INF), axis=1, keepdims=True)
    ge = jnp.where(gmask, jnp.exp(logits - gmax), 0.0)
    gp = ge / jnp.sum(ge, axis=1, keepdims=True)
    g_w = jnp.max(gp, axis=1, keepdims=True)
    g_idx = jnp.min(jnp.where(gmask, jnp.where(gp == g_w, lane_f, big), big), axis=1, keepdims=True)
    grp_of_lane = ((lane - MOE_GROUPS) >> EPG_LOG2).astype(F32)
    emask = jnp.where(lane >= MOE_GROUPS, grp_of_lane, -1.0) == g_idx
    emax = jnp.max(jnp.where(emask, logits, NEG_INF), axis=1, keepdims=True)
    ee = jnp.where(emask, jnp.exp(logits - emax), 0.0)
    ep = jnp.where(emask, ee / jnp.sum(ee, axis=1, keepdims=True), -1.0)
    v1 = jnp.max(ep, axis=1, keepdims=True)
    i1 = jnp.min(jnp.where(ep == v1, lane_f, big), axis=1, keepdims=True)
    ep2 = jnp.where(lane_f == i1, -1.0, ep)
    v2 = jnp.max(ep2, axis=1, keepdims=True)
    i2 = jnp.min(jnp.where(ep2 == v2, lane_f, big), axis=1, keepdims=True)
    w0 = g_w * v1 / (v1 + v2)
    w1 = g_w * v2 / (v1 + v2)
    e0 = i1 - float(MOE_GROUPS)
    e1 = i2 - float(MOE_GROUPS)

    oh0 = jnp.where(lane_f == e0, 1.0, 0.0)
    oh1 = jnp.where(lane_f == e1, 1.0, 0.0)
    pre0 = jnp.dot(tril_ref[...], oh0.astype(BF16), preferred_element_type=F32)
    pre1 = jnp.dot(tril_ref[...], oh1.astype(BF16), preferred_element_type=F32)
    carry = carry_ref[...]
    tot0 = pre0[tm - 1:tm, :]
    tot1 = pre1[tm - 1:tm, :]
    rank0 = jnp.sum(oh0 * (pre0 - 1.0 + carry), axis=1, keepdims=True)
    rank1 = jnp.sum(oh1 * (pre1 - 1.0 + carry + tot0), axis=1, keepdims=True)
    new_carry = carry + tot0 + tot1
    carry_ref[...] = new_carry
    cnt_ref[...] = jnp.broadcast_to(new_carry, cnt_ref.shape)

    rt = jnp.where(lane == 0, e0, jnp.where(lane == 1, e1, jnp.where(lane == 2, w0, jnp.where(
        lane == 3, w1, jnp.where(lane == 4, rank0, jnp.where(lane == 5, rank1, 0.0))))))
    rt_ref[...] = rt


def _out_proj(nsa_o, ml_o, w_out, x2, g2, w_r, b_r, tril):
    T = x2.shape[0]
    half = D_MODEL // 2
    return pl.pallas_call(
        _outproj_kernel,
        grid=(T // TM_OUT,),
        in_specs=[
            pl.BlockSpec((TM_OUT, half), lambda i: (i, 0)),
            pl.BlockSpec((TM_OUT, half), lambda i: (i, 0)),
            pl.BlockSpec((D_MODEL, D_MODEL), lambda i: (0, 0)),
            pl.BlockSpec((TM_OUT, D_MODEL), lambda i: (i, 0)),
            pl.BlockSpec((1, D_MODEL), lambda i: (0, 0)),
            pl.BlockSpec((D_MODEL, LANE), lambda i: (0, 0)),
            pl.BlockSpec((1, LANE), lambda i: (0, 0)),
            pl.BlockSpec((TM_OUT, TM_OUT), lambda i: (0, 0)),
        ],
        out_specs=[
            pl.BlockSpec((TM_OUT, D_MODEL), lambda i: (i, 0)),
            pl.BlockSpec((TM_OUT, half), lambda i: (i, 0)),
            pl.BlockSpec((TM_OUT, LANE), lambda i: (i, 0)),
            pl.BlockSpec((8, LANE), lambda i: (0, 0)),
        ],
        out_shape=[
            jax.ShapeDtypeStruct((T, D_MODEL), F32),
            jax.ShapeDtypeStruct((T, half), U32),
            jax.ShapeDtypeStruct((T, LANE), F32),
            jax.ShapeDtypeStruct((8, LANE), F32),
        ],
        scratch_shapes=[pltpu.VMEM((1, LANE), F32)],
        compiler_params=_cparams(("arbitrary",)),
        name="out_proj",
    )(nsa_o, ml_o, w_out, x2, g2, w_r, b_r, tril)


TD = 256
DISPATCH_BUFS = 3
DISPATCH_UNROLL = 8


def _dispatch_kernel(dest_ref, zblk_ref, meta_ref, xp_hbm, xs_hbm, zbuf_ref, zsem_ref, tbuf_ref, lsem_ref,
                     rsem_ref):
    n_assign = dest_ref.shape[0]
    nb = xs_hbm.shape[0] // BM
    n_used = meta_ref[0]
    zbuf_ref[...] = jnp.zeros_like(zbuf_ref)

    def zero_block(blk):
        return pltpu.make_async_copy(zbuf_ref, xs_hbm.at[pl.ds(blk * BM, BM), :], zsem_ref.at[0])

    def for_zero_blocks(fn):
        for e in range(N_EXPERTS):
            @pl.when(zblk_ref[e] >= 0)
            def _(e=e):
                fn(zero_block(zblk_ref[e]))

        def body(blk, _):
            fn(zero_block(blk))
            return 0
        lax.fori_loop(n_used, nb, body, 0)

    for_zero_blocks(lambda c: c.start())
    for_zero_blocks(lambda c: c.wait())

    n_tiles = n_assign // (2 * TD)

    def load(tile, slot):
        return pltpu.make_async_copy(xp_hbm.at[pl.ds(tile * TD, TD), :], tbuf_ref.at[slot], lsem_ref.at[slot])

    def wait_scatter(slot):
        for _ in range(2):
            pltpu.make_async_copy(tbuf_ref.at[slot], tbuf_ref.at[slot], rsem_ref.at[slot]).wait()

    load(0, 0).start()

    def tile_step(i, _):
        slot = i % DISPATCH_BUFS
        nslot = (i + 1) % DISPATCH_BUFS

        @pl.when(i + 1 < n_tiles)
        def _():
            @pl.when(i + 1 >= DISPATCH_BUFS)
            def _():
                wait_scatter(nslot)
            load(i + 1, nslot).start()

        load(i, slot).wait()

        def body(j, _):
            for u in range(DISPATCH_UNROLL):
                r = j * DISPATCH_UNROLL + u
                for k in range(2):
                    d = dest_ref[(i * TD + r) * 2 + k]
                    pltpu.make_async_copy(_row(tbuf_ref.at[slot], r), _row(xs_hbm, d),
                                          rsem_ref.at[slot]).start(priority=k)
            return 0
        lax.fori_loop(0, TD // DISPATCH_UNROLL, body, 0)
        return 0

    lax.fori_loop(0, n_tiles, tile_step, 0)
    for back in range(min(DISPATCH_BUFS, n_tiles)):
        wait_scatter((n_tiles - 1 - back) % DISPATCH_BUFS)


def _dispatch(dest, zblk, meta, xp, n_rows):
    half = D_MODEL // 2
    assert dest.shape[0] % (2 * TD) == 0 and dest.shape[0] // (2 * TD) >= DISPATCH_BUFS
    grid_spec = pltpu.PrefetchScalarGridSpec(
        num_scalar_prefetch=3,
        grid=(1,),
        in_specs=[pl.BlockSpec(memory_space=pl.ANY)],
        out_specs=pl.BlockSpec(memory_space=pl.ANY),
        scratch_shapes=[
            pltpu.VMEM((BM, half), U32),
            pltpu.SemaphoreType.DMA((1,)),
            pltpu.VMEM((DISPATCH_BUFS, TD, half), U32),
            pltpu.SemaphoreType.DMA((DISPATCH_BUFS,)),
            pltpu.SemaphoreType.DMA((DISPATCH_BUFS,)),
        ],
    )
    return pl.pallas_call(
        _dispatch_kernel,
        grid_spec=grid_spec,
        out_shape=jax.ShapeDtypeStruct((n_rows, half), U32),
        compiler_params=_cparams(("arbitrary",)),
        name="dispatch",
    )(dest, zblk, meta, xp)


def _expert_kernel(be_ref, nxt_ref, par_ref, meta_ref, x_ref, wg_hbm, wu_hbm, wd_hbm, y_ref,
                   wsg_ref, wsu_ref, wsd_ref, wsem_ref, wgb_ref, wub_ref, wdb_ref):
    i = pl.program_id(0)
    n_used = meta_ref[0]
    half = D_MODEL // 2

    def weight_copies(e, p):
        return (pltpu.make_async_copy(wg_hbm.at[e], wsg_ref.at[p], wsem_ref.at[p, 0]),
                pltpu.make_async_copy(wu_hbm.at[e], wsu_ref.at[p], wsem_ref.at[p, 1]),
                pltpu.make_async_copy(wd_hbm.at[e], wsd_ref.at[p], wsem_ref.at[p, 2]))

    def succ(e):
        return jnp.where(e >= 0, nxt_ref[jnp.maximum(e, 0)], -1)

    def start_weights(e, p):
        @pl.when(e >= 0)
        def _():
            for c in weight_copies(e, p):
                c.start(priority=1)

    def wait_staged(p):
        for c in weight_copies(0, p):
            c.wait()

    def cast_staged(p):
        wgb_ref[p] = wsg_ref[p].astype(BF16)
        wub_ref[p] = wsu_ref[p].astype(BF16)
        wdb_ref[p] = wsd_ref[p].astype(BF16)

    @pl.when(i == 0)
    def _():
        e0 = be_ref[0]
        start_weights(e0, 0)
        start_weights(succ(e0), 1)
        wait_staged(0)
        cast_staged(0)
        start_weights(succ(succ(e0)), 0)

    def step(p, cast_next):
        if cast_next:
            wait_staged(1 - p)
        lo, hi = _unpack_bf16_pair(x_ref[...])
        xl = lo.astype(BF16)
        xh = hi.astype(BF16)
        gt = (jnp.dot(xl, wgb_ref[p, 0:half, :], preferred_element_type=F32)
              + jnp.dot(xh, wgb_ref[p, half:, :], preferred_element_type=F32))
        up = (jnp.dot(xl, wub_ref[p, 0:half, :], preferred_element_type=F32)
              + jnp.dot(xh, wub_ref[p, half:, :], preferred_element_type=F32))
        hb = (_silu(gt) * up).astype(BF16)
        y = jnp.dot(hb, wdb_ref[p], preferred_element_type=F32)
        y_ref[...] = _pack_bf16_pair(y[:, :half], y[:, half:])
        if cast_next:
            cast_staged(1 - p)
            start_weights(succ(succ(succ(be_ref[i]))), 1 - p)

    e = be_ref[jnp.minimum(i, n_used - 1)]
    ends_expert = (i + 1 < n_used) & (be_ref[jnp.minimum(i + 1, n_used - 1)] != e)
    for p in range(2):
        for cast_next in (False, True):
            @pl.when((i < n_used) & (par_ref[i] == p)
                     & (ends_expert if cast_next else jnp.logical_not(ends_expert)))
            def _(p=p, cast_next=cast_next):
                step(p, cast_next)

    @pl.when(i >= n_used)
    def _():
        y_ref[...] = jnp.zeros_like(y_ref)


def _experts(block_expert, next_expert, block_parity, meta, xs, w_gate, w_up, w_down):
    half = D_MODEL // 2
    n_rows = xs.shape[0]
    nb = n_rows // BM
    grid_spec = pltpu.PrefetchScalarGridSpec(
        num_scalar_prefetch=4,
        grid=(nb,),
        in_specs=[
            pl.BlockSpec((BM, half), lambda i, be, nxt, par, meta: (jnp.minimum(i, meta[0] - 1), 0)),
            pl.BlockSpec(memory_space=pl.ANY), pl.BlockSpec(memory_space=pl.ANY),
            pl.BlockSpec(memory_space=pl.ANY),
        ],
        out_specs=pl.BlockSpec((BM, half), lambda i, be, nxt, par, meta: (i, 0)),
        scratch_shapes=[
            pltpu.VMEM((2, D_MODEL, D_EXPERT), F32),
            pltpu.VMEM((2, D_MODEL, D_EXPERT), F32),
            pltpu.VMEM((2, D_EXPERT, D_MODEL), F32),
            pltpu.SemaphoreType.DMA((2, 3)),
            pltpu.VMEM((2, D_MODEL, D_EXPERT), BF16),
            pltpu.VMEM((2, D_MODEL, D_EXPERT), BF16),
            pltpu.VMEM((2, D_EXPERT, D_MODEL), BF16),
        ],
    )
    return pl.pallas_call(
        _expert_kernel,
        grid_spec=grid_spec,
        out_shape=jax.ShapeDtypeStruct((n_rows, half), U32),
        compiler_params=_cparams(("arbitrary",)),
        name="experts",
    )(block_expert, next_expert, block_parity, meta, xs, w_gate, w_up, w_down)


def _combine_kernel(dest_ref, y_hbm, x1_ref, rt_ref, fg_ref, o_ref, ya_ref, yb_ref, sem_ref):
    i = pl.program_id(0)
    nt = pl.num_programs(0)
    half = D_MODEL // 2
    bufs = (ya_ref, yb_ref)

    def row_copy(tile, slot, r_tile, s, k):
        r = r_tile * 8 + s
        d = dest_ref[(tile * TC + r) * 2 + k]
        return pltpu.make_async_copy(_row(y_hbm, d), _row(bufs[slot].at[k], r), sem_ref.at[slot])

    def wait_rows(slot):
        pltpu.make_async_copy(bufs[slot], bufs[slot], sem_ref.at[slot]).wait()

    def step(slot):
        wait_rows(slot)
        for r in range(TC):
            for k in range(2):
                row_copy(i + 1, 1 - slot, r // 8, r % 8, k).start(priority=k)
        rt = rt_ref[...]
        w0 = rt[:, 2:3]
        w1 = rt[:, 3:4]
        lo0, hi0 = _unpack_bf16_pair(bufs[slot][0])
        lo1, hi1 = _unpack_bf16_pair(bufs[slot][1])
        xl = x1_ref[:, :half] + (w0 * lo0 + w1 * lo1)
        xh = x1_ref[:, half:] + (w0 * hi0 + w1 * hi1)
        ms = (jnp.sum(xl * xl, axis=1, keepdims=True) + jnp.sum(xh * xh, axis=1, keepdims=True)) / float(D_MODEL)
        inv = lax.rsqrt(ms + NORM_EPS)
        o_ref[:, :half] = xl * inv * fg_ref[:, :half]
        o_ref[:, half:] = xh * inv * fg_ref[:, half:]

        @pl.when(i == nt - 1)
        def _():
            wait_rows(1 - slot)

    @pl.when(i == 0)
    def _():
        def body(rt, _):
            for s in range(8):
                for k in range(2):
                    row_copy(0, 0, rt, s, k).start(priority=k)
            return 0
        lax.fori_loop(0, TC // 8, body, 0)

    @pl.when(i % 2 == 0)
    def _():
        step(0)

    @pl.when(i % 2 == 1)
    def _():
        step(1)


def _combine(dest, ys, x1, route, fg):
    T = x1.shape[0]
    half = D_MODEL // 2
    grid_spec = pltpu.PrefetchScalarGridSpec(
        num_scalar_prefetch=1,
        grid=(T // TC,),
        in_specs=[
            pl.BlockSpec(memory_space=pl.ANY),
            pl.BlockSpec((TC, D_MODEL), lambda i, d: (i, 0)),
            pl.BlockSpec((TC, LANE), lambda i, d: (i, 0)),
            pl.BlockSpec((1, D_MODEL), lambda i, d: (0, 0)),
        ],
        out_specs=pl.BlockSpec((TC, D_MODEL), lambda i, d: (i, 0)),
        scratch_shapes=[
            pltpu.VMEM((2, TC, half), U32),
            pltpu.VMEM((2, TC, half), U32),
            pltpu.SemaphoreType.DMA((2,)),
        ],
    )
    return pl.pallas_call(
        _combine_kernel,
        grid_spec=grid_spec,
        out_shape=jax.ShapeDtypeStruct((T, D_MODEL), F32),
        compiler_params=_cparams(("arbitrary",)),
        name="combine",
    )(dest, ys, x1, route, fg)


def _cover_matrix(S):
    n_blk = S // CMP_STRIDE
    n_sel = S // SEL_BLOCK
    cs = np.arange(n_blk) * CMP_STRIDE
    ss = np.arange(n_sel) * SEL_BLOCK
    shared = np.minimum(cs[:, None] + CMP_LEN, ss[None, :] + SEL_BLOCK) - np.maximum(cs[:, None], ss[None, :])
    return (np.clip(shared, 0, None) / CMP_LEN).T.astype(np.float32)


def _block_mask_matrix(S):
    n_sel = S // SEL_BLOCK
    assert n_sel <= LANE
    e = np.zeros((S, LANE), np.float32)
    e[np.arange(S), np.arange(S) // SEL_BLOCK] = NEG_INF
    return e


def _inv_freq_row():
    inv = np.power(np.float32(ROPE_THETA), -np.arange(ROPE_HALF, dtype=np.float32) * 2.0 / ROPE_DIM)
    return np.tile(inv, ROPE_SLOTS).reshape(1, LANE).astype(np.float32)


def _packed_positions(positions, T):
    rows = TS_ROPE // ROPE_SLOTS
    p = positions.reshape(T // TS_ROPE, ROPE_SLOTS, rows).transpose(0, 2, 1).astype(F32)
    return jnp.repeat(p, ROPE_HALF, axis=2).reshape(T // ROPE_SLOTS, LANE)


def _layer(x2, positions, B, S, norm1_g, w_in, cmp_pe_k, cmp_pe_v, cmp_wk1, cmp_wk2, cmp_wv1, cmp_wv2,
           nsa_norm_g, conv_w, conv_b, b_igate, b_fgate, mlstm_norm_g, w_out, norm2_g,
           w_group, b_group, w_router, b_router, w_exp_gate, w_exp_up, w_exp_down, out_norm_g):
    T = B * S
    w_main, w_small = _weight_prep(jnp.transpose(w_in[0]))
    posb = _packed_positions(positions, T)
    invf = jnp.asarray(_inv_freq_row())

    proj, small = _in_proj(x2, norm1_g.reshape(1, -1), w_main, w_small)
    q_r, k_r = _rope(posb, invf, proj)
    kc, vc = _compress(k_r, proj, B, S, cmp_pe_k, cmp_pe_v, cmp_wk1.astype(BF16), cmp_wk2.astype(BF16),
                       cmp_wv1.astype(BF16), cmp_wv2.astype(BF16))
    eye = np.eye(S // SEL_BLOCK, LANE, dtype=np.float32)
    nsa_o = _nsa(q_r, k_r, proj, small, kc, vc, jnp.asarray(_cover_matrix(S), BF16),
                 jnp.asarray(_block_mask_matrix(S), BF16), jnp.asarray(eye, BF16),
                 nsa_norm_g.reshape(1, -1), B, S)

    if_arr = small[:, 24:32].reshape(B, S, 2, ML_HEADS).transpose(0, 3, 2, 1)
    gate_bias = jnp.stack([b_igate, b_fgate]).astype(F32)
    ml_o = _mlstm(proj, if_arr, gate_bias, conv_w, conv_b.reshape(1, -1), mlstm_norm_g.reshape(1, -1), B, S)

    w_r = jnp.concatenate([w_group, w_router, jnp.zeros((D_MODEL, LANE - MOE_GROUPS - N_EXPERTS), F32)],
                          axis=1).astype(BF16)
    b_r = jnp.concatenate([b_group, b_router, jnp.zeros((LANE - MOE_GROUPS - N_EXPERTS,), F32)]).reshape(1, LANE)
    tril = jnp.asarray(np.tril(np.ones((TM_OUT, TM_OUT), np.float32)), BF16)
    x1, xp, route, cnt = _out_proj(nsa_o, ml_o, w_out.astype(BF16), x2, norm2_g.reshape(1, -1), w_r, b_r, tril)

    n_rows = T * 2 + N_EXPERTS * BM
    counts = cnt[0, :N_EXPERTS].astype(jnp.int32)
    pcounts = (counts + BM - 1) // BM * BM
    pends = jnp.cumsum(pcounts)
    pstarts = pends - pcounts
    eid = route[:, 0:2].astype(jnp.int32)
    onehot = (eid[..., None] == jnp.arange(N_EXPERTS, dtype=jnp.int32)).astype(F32)
    start_blk = jnp.einsum('tke,e->tk', onehot, (pstarts // BM).astype(F32)).astype(jnp.int32)
    dest = (start_blk * BM + route[:, 4:6].astype(jnp.int32)).reshape(T * 2)
    block_expert = jnp.minimum(jnp.searchsorted(pends, jnp.arange(n_rows // BM) * BM, side='right'),
                               N_EXPERTS - 1).astype(jnp.int32)
    present = jnp.where(counts > 0, jnp.arange(N_EXPERTS, dtype=jnp.int32), N_EXPERTS)
    later = jnp.concatenate([lax.cummin(present[::-1])[::-1][1:], jnp.full((1,), N_EXPERTS, jnp.int32)])
    next_expert = jnp.where(later < N_EXPERTS, later, -1).astype(jnp.int32)
    meta = (pends[-1:] // BM).astype(jnp.int32)

    last_block = jnp.where(counts > 0, pends // BM - 1, -1).astype(jnp.int32)

    xs = _dispatch(dest, last_block, meta, xp, n_rows)
    ordinal = jnp.cumsum((counts > 0).astype(jnp.int32)) - 1
    block_parity = (ordinal[block_expert] % 2).astype(jnp.int32)
    ys = _experts(block_expert, next_expert, block_parity, meta, xs, w_exp_gate, w_exp_up, w_exp_down)
    dest_pad = jnp.concatenate([dest, jnp.zeros((2 * TC,), jnp.int32)])
    return _combine(dest_pad, ys, x1, route, out_norm_g.reshape(1, -1))


def kernel(x, positions, norm1_g, w_in, cmp_pe_k, cmp_pe_v, cmp_wk1, cmp_wk2, cmp_wv1, cmp_wv2, nsa_norm_g,
           conv_w, conv_b, b_igate, b_fgate, mlstm_norm_g, w_out, norm2_g, w_group, b_group, w_router,
           b_router, w_exp_gate, w_exp_up, w_exp_down, final_norm_g):
    B, S, D = x.shape
    assert D == D_MODEL and norm1_g.shape[0] == 1, "single-layer, D_MODEL-wide configuration only"
    assert S % ML_CHUNK == 0 and S % SEL_BUCKET == 0 and S >= WIN_KEYS and (B * S) % TM_IN == 0
    out = _layer(x.reshape(B * S, D), positions, B, S, norm1_g[0], w_in, cmp_pe_k[0], cmp_pe_v[0],
                 cmp_wk1[0], cmp_wk2[0], cmp_wv1[0], cmp_wv2[0], nsa_norm_g[0], conv_w[0], conv_b[0],
                 b_igate[0], b_fgate[0], mlstm_norm_g[0], w_out[0], norm2_g[0], w_group[0], b_group[0],
                 w_router[0], b_router[0], w_exp_gate[0], w_exp_up[0], w_exp_down[0], final_norm_g)
    return out.reshape(B, S, D)
```

```python
import functools

import numpy as np
import jax
import jax.numpy as jnp
from jax import lax
from jax.experimental import pallas as pl
from jax.experimental.pallas import tpu as pltpu

F32 = jnp.float32
BF16 = jnp.bfloat16
U32 = jnp.uint32

D_MODEL = 2048
NSA_HEADS = 8
NSA_GROUPS = 2
NSA_J = NSA_HEADS // NSA_GROUPS
HD = 128
CMP_LEN = 32
CMP_STRIDE = 16
SEL_BLOCK = 64
SEL_TOPK = 8
WINDOW = 512
ROPE_THETA = 500000.0
ROPE_DIM = 32
ROPE_HALF = 16
ML_HEADS = 4
ML_HD = 256
CONV_W = 4
MOE_GROUPS = 8
EPG = 8
N_EXPERTS = 64
D_EXPERT = 512
SEL_BLOCK_LOG2 = 6
EPG_LOG2 = 3
assert (1 << SEL_BLOCK_LOG2) == SEL_BLOCK and (1 << EPG_LOG2) == EPG
NORM_EPS = 1e-6
NEG_INF = -1e30
FORCE_SCORE = 1000.0

COL_Q = 0
COL_KV = 1024
COL_QKM = 2560
COL_VM = 4608
COL_OM = 5632
N_MAIN = 6656

LANE = 128
VMEM_LIMIT = 56 * 1024 * 1024

TM_IN = 1024
TN_IN = 1664
TS_ROPE = 256
TQ = 256
SEL_BUCKET = 256
WIN_KEYS = WINDOW + TQ
ML_CHUNK = 256
TM_OUT = 512
BM = 256
TC = 256

NT_DIMS = (((1,), (1,)), ((), ()))
TN_DIMS = (((0,), (0,)), ((), ()))


def _cparams(sem):
    return pltpu.CompilerParams(dimension_semantics=sem, vmem_limit_bytes=VMEM_LIMIT)


def _sigmoid(x):
    return 0.5 * jnp.tanh(0.5 * x) + 0.5


def _silu(x):
    h = 0.5 * x
    return h + h * jnp.tanh(h)


W_GATES = 2560
W_QKM = 2584
W_IGATE = 6680
W_END = 6688
TR_PREP = 512
assert W_GATES % TR_PREP == 0 and N_MAIN % TR_PREP == 0 and W_IGATE - W_QKM == N_MAIN - W_GATES


def _wprep_kernel(wt_hbm, wm_ref, ws_ref, wbuf_ref, sbuf_ref, sem_ref, ssem_ref):
    i = pl.program_id(0)
    n_gate = W_QKM - W_GATES
    n_if = W_END - W_IGATE

    def load(blk, slot):
        start = pl.multiple_of(blk * TR_PREP + jnp.where(blk * TR_PREP >= W_GATES, n_gate, 0), 8)
        return pltpu.make_async_copy(wt_hbm.at[pl.ds(start, TR_PREP), :], wbuf_ref.at[slot], sem_ref.at[slot])

    def small_loads():
        return (pltpu.make_async_copy(wt_hbm.at[pl.ds(W_GATES, n_gate), :], sbuf_ref.at[pl.ds(0, n_gate), :],
                                      ssem_ref.at[0]),
                pltpu.make_async_copy(wt_hbm.at[pl.ds(W_IGATE, n_if), :], sbuf_ref.at[pl.ds(n_gate, n_if), :],
                                      ssem_ref.at[1]))

    @pl.when(i == 0)
    def _():
        load(0, 0).start()
        for c in small_loads():
            c.start()
        for c in small_loads():
            c.wait()
        ws_ref[...] = jnp.zeros_like(ws_ref)
        ws_ref[0:n_gate + n_if, :] = sbuf_ref[...].astype(BF16)

    @pl.when(i + 1 < pl.num_programs(0))
    def _():
        load(i + 1, (i + 1) % 2).start()

    load(i, i % 2).wait()
    wm_ref[...] = wbuf_ref[i % 2].astype(BF16)


def _weight_prep(wt):
    assert wt.shape == (W_END, D_MODEL)
    n_small = (W_QKM - W_GATES) + (W_END - W_IGATE)
    return pl.pallas_call(
        _wprep_kernel,
        grid=(N_MAIN // TR_PREP,),
        in_specs=[pl.BlockSpec(memory_space=pl.ANY)],
        out_specs=[pl.BlockSpec((TR_PREP, D_MODEL), lambda i: (i, 0)),
                   pl.BlockSpec((LANE, D_MODEL), lambda i: (0, 0))],
        out_shape=[jax.ShapeDtypeStruct((N_MAIN, D_MODEL), BF16),
                   jax.ShapeDtypeStruct((LANE, D_MODEL), BF16)],
        scratch_shapes=[pltpu.VMEM((2, TR_PREP, D_MODEL), F32), pltpu.VMEM((n_small, D_MODEL), F32),
                        pltpu.SemaphoreType.DMA((2,)), pltpu.SemaphoreType.DMA((2,))],
        compiler_params=_cparams(("arbitrary",)),
        name="weight_prep",
    )(wt)


def _inproj_kernel(x_ref, g_ref, w_ref, ws_ref, o_ref, os_ref, h_ref):
    @pl.when(pl.program_id(1) == 0)
    def _():
        x = x_ref[...]
        ms = jnp.mean(x * x, axis=-1, keepdims=True)
        h_ref[...] = (x * lax.rsqrt(ms + NORM_EPS) * g_ref[...]).astype(BF16)
        os_ref[...] = lax.dot_general(h_ref[...], ws_ref[...], NT_DIMS, preferred_element_type=F32)

    o_ref[...] = lax.dot_general(h_ref[...], w_ref[...], NT_DIMS,
                                 preferred_element_type=F32).astype(o_ref.dtype)


def _in_proj(x2, g1, w_main, w_small):
    T = x2.shape[0]
    return pl.pallas_call(
        _inproj_kernel,
        grid=(T // TM_IN, N_MAIN // TN_IN),
        in_specs=[
            pl.BlockSpec((TM_IN, D_MODEL), lambda m, n: (m, 0)),
            pl.BlockSpec((1, D_MODEL), lambda m, n: (0, 0)),
            pl.BlockSpec((TN_IN, D_MODEL), lambda m, n: (n, 0)),
            pl.BlockSpec((LANE, D_MODEL), lambda m, n: (0, 0)),
        ],
        out_specs=[
            pl.BlockSpec((TM_IN, TN_IN), lambda m, n: (m, n)),
            pl.BlockSpec((TM_IN, LANE), lambda m, n: (m, 0)),
        ],
        out_shape=[
            jax.ShapeDtypeStruct((T, N_MAIN), BF16),
            jax.ShapeDtypeStruct((T, LANE), F32),
        ],
        scratch_shapes=[pltpu.VMEM((TM_IN, D_MODEL), BF16)],
        compiler_params=_cparams(("parallel", "arbitrary")),
        name="in_proj",
    )(x2, g1, w_main, w_small)


ROPE_SLOTS = LANE // ROPE_HALF


def _rope_kernel(pos_ref, invf_ref, q_ref, kc_ref, ks_ref, kw_ref, qo_ref, ko_ref):
    ang = pos_ref[...] * invf_ref[...]
    c = jnp.cos(ang)
    s = jnp.sin(ang)
    rows = ang.shape[0]
    lane = lax.broadcasted_iota(jnp.int32, ang.shape, 1)
    lo = lane < ROPE_HALF
    mid = lane < ROPE_DIM
    scale = HD ** -0.5

    def lanes_from(x, src):
        shift = (-src) % LANE
        return x if shift == 0 else pltpu.roll(x, shift, 1)

    for slot in range(ROPE_SLOTS):
        src = slot * ROPE_HALF
        cf = jnp.where(lo, lanes_from(c, src), jnp.where(mid, lanes_from(c, src - ROPE_HALF), 1.0))
        sa = jnp.where(lo, -lanes_from(s, src), 0.0)
        sb = jnp.where(lo, 0.0, jnp.where(mid, lanes_from(s, src - ROPE_HALF), 0.0))
        rs = slice(slot * rows, (slot + 1) * rows)

        def rope(x):
            return x * cf + pltpu.roll(x, LANE - ROPE_HALF, 1) * sa + pltpu.roll(x, ROPE_HALF, 1) * sb

        for h in range(NSA_HEADS):
            sl = slice(h * HD, (h + 1) * HD)
            qo_ref[rs, sl] = (rope(q_ref[rs, sl].astype(F32)) * scale).astype(BF16)
        for i, r in enumerate((kc_ref, ks_ref, kw_ref)):
            for g in range(NSA_GROUPS):
                sl = slice(g * HD, (g + 1) * HD)
                so = slice(i * 2 * HD + g * HD, i * 2 * HD + (g + 1) * HD)
                ko_ref[rs, so] = rope(r[rs, sl].astype(F32)).astype(BF16)


def _rope(posb, invf, proj):
    T = proj.shape[0]
    kvb = COL_KV // 256
    return pl.pallas_call(
        _rope_kernel,
        grid=(T // TS_ROPE,),
        in_specs=[
            pl.BlockSpec((TS_ROPE // ROPE_SLOTS, LANE), lambda i: (i, 0)),
            pl.BlockSpec((1, LANE), lambda i: (0, 0)),
            pl.BlockSpec((TS_ROPE, 1024), lambda i: (i, 0)),
            pl.BlockSpec((TS_ROPE, 256), lambda i: (i, kvb + 0)),
            pl.BlockSpec((TS_ROPE, 256), lambda i: (i, kvb + 2)),
            pl.BlockSpec((TS_ROPE, 256), lambda i: (i, kvb + 4)),
        ],
        out_specs=[
            pl.BlockSpec((TS_ROPE, 1024), lambda i: (i, 0)),
            pl.BlockSpec((TS_ROPE, 768), lambda i: (i, 0)),
        ],
        out_shape=[
            jax.ShapeDtypeStruct((T, 1024), BF16),
            jax.ShapeDtypeStruct((T, 768), BF16),
        ],
        compiler_params=_cparams(("parallel",)),
        name="rope",
    )(posb, invf, proj, proj, proj, proj)


def _compress_kernel(k_ref, v_ref, pek_ref, pev_ref, w1k_ref, w2k_ref, w1v_ref, w2v_ref,
                     kc_ref, vc_ref, xs_ref):
    S = k_ref.shape[0]
    n_blk = S // CMP_STRIDE
    for src, pe, w1, w2, dst in ((k_ref, pek_ref, w1k_ref, w2k_ref, kc_ref),
                                 (v_ref, pev_ref, w1v_ref, w2v_ref, vc_ref)):
        xs_ref[0:S, :] = src[...].astype(F32)
        xs_ref[S:S + CMP_LEN, :] = jnp.zeros((CMP_LEN, HD), F32)
        acc = jnp.zeros((n_blk, HD), F32)
        for l in range(CMP_LEN):
            a = xs_ref[pl.ds(l, n_blk, stride=CMP_STRIDE), :] + pe[l:l + 1, :]
            acc = acc + jnp.dot(a.astype(BF16), w1[l * HD:(l + 1) * HD, :],
                                preferred_element_type=F32)
        hid = _silu(acc)
        out = jnp.dot(hid.astype(BF16), w2[...], preferred_element_type=F32)
        dst[0, 0] = out.astype(BF16)


def _compress(k_r, proj, B, S, pek, pev, w1k, w2k, w1v, w2v):
    n_blk = S // CMP_STRIDE
    vcol = (COL_KV + 256) // HD
    full = lambda shape: pl.BlockSpec(shape, lambda b, g: tuple(0 for _ in shape))
    return pl.pallas_call(
        _compress_kernel,
        grid=(B, NSA_GROUPS),
        in_specs=[
            pl.BlockSpec((S, HD), lambda b, g: (b, g)),
            pl.BlockSpec((S, HD), lambda b, g: (b, vcol + g)),
            full((CMP_LEN, HD)), full((CMP_LEN, HD)),
            full((CMP_LEN * HD, HD)), full((HD, HD)),
            full((CMP_LEN * HD, HD)), full((HD, HD)),
        ],
        out_specs=[
            pl.BlockSpec((1, 1, n_blk, HD), lambda b, g: (b, g, 0, 0)),
            pl.BlockSpec((1, 1, n_blk, HD), lambda b, g: (b, g, 0, 0)),
        ],
        out_shape=[
            jax.ShapeDtypeStruct((B, NSA_GROUPS, n_blk, HD), BF16),
            jax.ShapeDtypeStruct((B, NSA_GROUPS, n_blk, HD), BF16),
        ],
        scratch_shapes=[pltpu.VMEM((S + CMP_LEN, HD), F32)],
        compiler_params=_cparams(("parallel", "parallel")),
        name="compress",
    )(k_r, proj, pek, pev, w1k, w2k, w1v, w2v)


def _nsa_kernel(q_ref, kc_ref, vc_ref, ks_ref, vs_ref, kw_ref, vw_ref, sm_ref, covt_ref, eneg_ref, eye_ref,
                ng_ref, o_ref, obuf_ref, owin_ref, osel_ref):
    qi = pl.program_id(1)
    q0 = qi * TQ
    R = NSA_J * TQ
    n_cmp = kc_ref.shape[2]
    n_sel = ks_ref.shape[0] // SEL_BLOCK
    n_win = WIN_KEYS // TQ

    def row_t(shape):
        r = lax.broadcasted_iota(jnp.int32, shape, 0)
        return q0 + (r & (TQ - 1))

    qgs = [jnp.concatenate([q_ref[:, (g * NSA_J + j) * HD:(g * NSA_J + j + 1) * HD]
                            for j in range(NSA_J)], axis=0) for g in range(NSA_GROUPS)]

    def with_ones(v):
        lane = lax.broadcasted_iota(jnp.int32, v.shape, 1)
        return jnp.concatenate([v, jnp.where(lane == 0, 1.0, 0.0).astype(BF16)], axis=1)

    def window(masks):
        w0 = pl.multiple_of(jnp.maximum(q0 - WINDOW, 0), TQ)
        for g in range(NSA_GROUPS):
            kt = kw_ref[pl.ds(w0, WIN_KEYS), g * HD:(g + 1) * HD]
            vt = with_ones(vw_ref[pl.ds(w0, WIN_KEYS), g * HD:(g + 1) * HD])
            sc = masks(lax.dot_general(qgs[g], kt, NT_DIMS, preferred_element_type=F32), w0)
            mw = jnp.max(sc, axis=1, keepdims=True)
            pw = jnp.exp((sc - mw).astype(BF16))
            acc = jnp.dot(pw, vt, preferred_element_type=F32)
            owin_ref[g] = acc[:, 0:HD] / acc[:, HD:HD + 1]

    def band_masks(sc, w0):
        d = (lax.broadcasted_iota(jnp.int32, (R, TQ), 1)
             - (lax.broadcasted_iota(jnp.int32, (R, TQ), 0) & (TQ - 1)))
        first = jnp.where(d > 0, sc[:, 0:TQ], NEG_INF)
        last = jnp.where(d <= 0, sc[:, (n_win - 1) * TQ:], NEG_INF)
        return jnp.concatenate([first, sc[:, TQ:(n_win - 1) * TQ], last], axis=1)

    def general_masks(sc, w0):
        diff = row_t((R, WIN_KEYS)) - (w0 + lax.broadcasted_iota(jnp.int32, (R, WIN_KEYS), 1))
        keep = jnp.where(diff >= 0, diff, WINDOW) < WINDOW
        return jnp.where(keep, sc, NEG_INF)

    window(general_masks)

    gates = _sigmoid(sm_ref[...])
    o_cmps = []
    qps = []
    for g in range(NSA_GROUPS):
        qg = qgs[g]

        s = lax.dot_general(qg, kc_ref[0, g], NT_DIMS, preferred_element_type=F32)
        n_lane = lax.broadcasted_iota(jnp.int32, (R, n_cmp), 1)
        cmask = (n_lane * CMP_STRIDE + (CMP_LEN - 1)) <= row_t((R, n_cmp))
        s = jnp.where(cmask, s, NEG_INF)
        m = jnp.max(s, axis=1, keepdims=True)
        e = jnp.where(cmask, jnp.exp(s - m), 0.0)
        l = jnp.sum(e, axis=1, keepdims=True)
        p = (e / jnp.where(l > 0.0, l, 1.0)).astype(BF16)
        o_cmps.append(jnp.dot(p, vc_ref[0, g], preferred_element_type=F32))
        impr = lax.dot_general(covt_ref[...], p, NT_DIMS, preferred_element_type=F32)
        imp = impr[:, 0:TQ]
        for j in range(1, NSA_J):
            imp = imp + impr[:, j * TQ:(j + 1) * TQ]

        m_sub = lax.broadcasted_iota(jnp.int32, (n_sel, TQ), 0)
        jt = (q0 + lax.broadcasted_iota(jnp.int32, (n_sel, TQ), 1)) >> SEL_BLOCK_LOG2
        forced = jnp.where(m_sub == 0, FORCE_SCORE,
                           jnp.where(m_sub == jt, FORCE_SCORE,
                                     jnp.where(m_sub == jt - 1, FORCE_SCORE, 0.0)))
        score = jnp.where(m_sub <= jt, imp + forced, -1.0)
        rank = jnp.zeros((n_sel, TQ), F32)
        for mp in range(n_sel):
            row = score[mp:mp + 1, :]
            ge = jnp.where(row >= score, 1.0, 0.0)
            gt = jnp.where(row > score, 1.0, 0.0)
            rank = rank + jnp.where(m_sub > mp, ge, gt)
        notsel_t = jnp.where(rank < float(min(SEL_TOPK, n_sel)), 0.0, 1.0).astype(BF16)
        notsel = lax.dot_general(notsel_t, eye_ref[...], TN_DIMS,
                                 preferred_element_type=F32).astype(BF16)
        qps.append(jnp.concatenate([qg, jnp.concatenate([notsel] * NSA_J, axis=0)], axis=1))

    def sel_oneshot(nk):
        for g in range(NSA_GROUPS):
            kt = jnp.concatenate([ks_ref[0:nk, g * HD:(g + 1) * HD], eneg_ref[0:nk, :]], axis=1)
            vt = with_ones(vs_ref[0:nk, g * HD:(g + 1) * HD])
            sc = lax.dot_general(qps[g], kt, NT_DIMS, preferred_element_type=F32)
            klane = lax.broadcasted_iota(jnp.int32, (R, SEL_BUCKET), 1) + (nk - SEL_BUCKET)
            tail = jnp.where(klane <= row_t((R, SEL_BUCKET)), sc[:, nk - SEL_BUCKET:], NEG_INF)
            sc = tail if nk == SEL_BUCKET else jnp.concatenate([sc[:, 0:nk - SEL_BUCKET], tail], axis=1)
            ms = jnp.max(sc, axis=1, keepdims=True)
            acc = jnp.dot(jnp.exp((sc - ms).astype(BF16)), vt, preferred_element_type=F32)
            osel_ref[g] = acc[:, 0:HD] / acc[:, HD:HD + 1]

    for b in range(ks_ref.shape[0] // SEL_BUCKET):
        @pl.when(q0 // SEL_BUCKET == b)
        def _(b=b):
            sel_oneshot((b + 1) * SEL_BUCKET)

    ssq = jnp.zeros((TQ, 1), F32)
    for g in range(NSA_GROUPS):
        o_sel = osel_ref[g]
        o_cmp = o_cmps[g]
        o_win = owin_ref[g]
        for j in range(NSA_J):
            h = g * NSA_J + j
            rs = slice(j * TQ, (j + 1) * TQ)
            o = (gates[:, 3 * h:3 * h + 1] * o_cmp[rs] + gates[:, 3 * h + 1:3 * h + 2] * o_sel[rs]
                 + gates[:, 3 * h + 2:3 * h + 3] * o_win[rs])
            ssq = ssq + jnp.sum(o * o, axis=1, keepdims=True)
            obuf_ref[:, h * HD:(h + 1) * HD] = o

    inv = lax.rsqrt(ssq / float(NSA_HEADS * HD) + NORM_EPS)
    o_ref[...] = (obuf_ref[...] * inv * ng_ref[...]).astype(BF16)


def _nsa(q_r, k_r, proj, small, kc, vc, covt, eneg, eye, ng, B, S):
    T = B * S
    nq = S // TQ
    n_blk = S // CMP_STRIDE
    n_sel = S // SEL_BLOCK
    kvb = COL_KV // 256
    return pl.pallas_call(
        _nsa_kernel,
        grid=(B, nq),
        in_specs=[
            pl.BlockSpec((TQ, 1024), lambda b, i: (b * nq + i, 0)),
            pl.BlockSpec((1, NSA_GROUPS, n_blk, HD), lambda b, i: (b, 0, 0, 0)),
            pl.BlockSpec((1, NSA_GROUPS, n_blk, HD), lambda b, i: (b, 0, 0, 0)),
            pl.BlockSpec((S, 256), lambda b, i: (b, 1)),
            pl.BlockSpec((S, 256), lambda b, i: (b, kvb + 3)),
            pl.BlockSpec((S, 256), lambda b, i: (b, 2)),
            pl.BlockSpec((S, 256), lambda b, i: (b, kvb + 5)),
            pl.BlockSpec((TQ, LANE), lambda b, i: (b * nq + i, 0)),
            pl.BlockSpec((n_sel, n_blk), lambda b, i: (0, 0)),
            pl.BlockSpec((S, LANE), lambda b, i: (0, 0)),
            pl.BlockSpec((n_sel, LANE), lambda b, i: (0, 0)),
            pl.BlockSpec((1, 1024), lambda b, i: (0, 0)),
        ],
        out_specs=pl.BlockSpec((TQ, 1024), lambda b, i: (b * nq + i, 0)),
        out_shape=jax.ShapeDtypeStruct((T, 1024), BF16),
        scratch_shapes=[pltpu.VMEM((TQ, 1024), F32), pltpu.VMEM((NSA_GROUPS, NSA_J * TQ, HD), F32),
                        pltpu.VMEM((NSA_GROUPS, NSA_J * TQ, HD), F32)],
        compiler_params=_cparams(("parallel", "parallel")),
        name="nsa",
    )(q_r, kc, vc, k_r, proj, k_r, proj, small, covt, eneg, eye, ng)


ML_HPS = 2


def _mlstm_kernel(bias_ref, q_ref, k_ref, v_ref, om_ref, if_ref, cwq_ref, cwk_ref, cbq_ref, cbk_ref,
                  ng_ref, o_ref, c_ref, n_ref, m_ref, xq_ref, xk_ref):
    hp = pl.program_id(1)
    L = ML_CHUNK

    @pl.when(pl.program_id(2) == 0)
    def _():
        c_ref[...] = jnp.zeros_like(c_ref)
        n_ref[...] = jnp.zeros_like(n_ref)
        m_ref[...] = jnp.zeros_like(m_ref)
        xq_ref[:, 0:8, :] = jnp.zeros((ML_HPS, 8, ML_HD), F32)
        xk_ref[:, 0:8, :] = jnp.zeros((ML_HPS, 8, ML_HD), F32)

    def conv_silu(x_ref, hh, xb_ref, w_ref, b_ref):
        cs = slice(hh * ML_HD, (hh + 1) * ML_HD)
        xb_ref[hh, 8:8 + L, :] = x_ref[:, cs].astype(F32)
        y = b_ref[:, cs] + xb_ref[hh, 8:8 + L, :] * w_ref[CONV_W - 1:CONV_W, cs]
        for k in range(1, CONV_W):
            y = y + xb_ref[hh, 8 - k:8 - k + L, :] * w_ref[CONV_W - 1 - k:CONV_W - k, cs]
        xb_ref[hh, 0:8, :] = xb_ref[hh, L:L + 8, :]
        return _silu(y)

    r = lax.broadcasted_iota(jnp.int32, (L, L), 0)
    cidx = lax.broadcasted_iota(jnp.int32, (L, L), 1)
    tril = cidx <= r
    eye = cidx == r

    for hh in range(ML_HPS):
        h = hp * ML_HPS + hh
        cs = slice(hh * ML_HD, (hh + 1) * ML_HD)
        qf = conv_silu(q_ref, hh, xq_ref, cwq_ref, cbq_ref) * (ML_HD ** -0.5)
        kf = conv_silu(k_ref, hh, xk_ref, cwk_ref, cbk_ref)
        qb = qf.astype(BF16)
        kb = kf.astype(BF16)
        vb = v_ref[:, cs]
        vf = vb.astype(F32)

        ic = if_ref[0, hh, 0:1, :] + bias_ref[0, h]
        fp = if_ref[0, hh, 1:2, :] + bias_ref[1, h]
        fc = jnp.minimum(fp, 0.0) - jnp.log(1.0 + jnp.exp(-jnp.abs(fp)))

        fc_b = jnp.broadcast_to(fc, (L, L))
        ic_b = jnp.broadcast_to(ic, (L, L))
        b_col = jnp.sum(jnp.where(tril, fc_b, 0.0), axis=1, keepdims=True)
        fc_col = jnp.sum(jnp.where(eye, fc_b, 0.0), axis=1, keepdims=True)
        ic_col = jnp.sum(jnp.where(eye, ic_b, 0.0), axis=1, keepdims=True)
        b_row = jnp.sum(jnp.where(r <= cidx, jnp.broadcast_to(fc_col, (L, L)), 0.0),
                        axis=0, keepdims=True)
        b_last = b_col[L - 1:L, :]
        m_prev = m_ref[hh]

        d_log = jnp.where(tril, b_col - b_row + ic, NEG_INF)
        inter = b_col + m_prev
        m_t = jnp.maximum(inter, jnp.max(d_log, axis=1, keepdims=True))
        w_intra = jnp.exp(d_log - m_t)
        w_inter = jnp.exp(inter - m_t)
        qk = lax.dot_general(qb, kb, NT_DIMS, preferred_element_type=F32) * w_intra
        num = (jnp.dot(qk.astype(BF16), vb, preferred_element_type=F32)
               + w_inter * lax.dot_general(qb, c_ref[hh].astype(BF16), NT_DIMS, preferred_element_type=F32))
        den = jnp.sum(qk, axis=1, keepdims=True) + w_inter * jnp.sum(qf * n_ref[hh], axis=1, keepdims=True)
        hm = num / jnp.maximum(jnp.abs(den), jnp.exp(-m_t))

        w_log = b_last - b_col + ic_col
        m_new = jnp.maximum(b_last + m_prev, jnp.max(w_log, axis=0, keepdims=True))
        w_state = jnp.exp(w_log - m_new)
        decay = jnp.exp(b_last + m_prev - m_new)
        c_ref[hh] = decay * c_ref[hh] + lax.dot_general((w_state * vf).astype(BF16), kb, TN_DIMS,
                                                        preferred_element_type=F32)
        n_ref[hh] = decay * n_ref[hh] + jnp.sum(w_state * kf, axis=0, keepdims=True)
        m_ref[hh] = m_new

        hn = hm * lax.rsqrt(jnp.mean(hm * hm, axis=1, keepdims=True) + NORM_EPS) * ng_ref[:, cs]
        o_ref[:, cs] = (hn * _sigmoid(om_ref[:, cs].astype(F32))).astype(BF16)


def _mlstm(proj, if_arr, gate_bias, conv_w, conv_b, ng, B, S):
    T = B * S
    nc = S // ML_CHUNK
    W = ML_HPS * ML_HD
    n_hp = ML_HEADS // ML_HPS
    assert COL_QKM % W == 0 and COL_VM % W == 0 and COL_OM % W == 0 and ML_HEADS % ML_HPS == 0
    cq, ck, cv, co = COL_QKM // W, COL_QKM // W + n_hp, COL_VM // W, COL_OM // W
    rows = lambda col0: pl.BlockSpec((ML_CHUNK, W), lambda b, h, c: (b * nc + c, col0 + h))
    return pl.pallas_call(
        _mlstm_kernel,
        grid=(B, n_hp, nc),
        in_specs=[
            pl.BlockSpec(memory_space=pltpu.SMEM),
            rows(cq), rows(ck), rows(cv), rows(co),
            pl.BlockSpec((1, ML_HPS, 2, ML_CHUNK), lambda b, h, c: (b, h, 0, c)),
            pl.BlockSpec((CONV_W, W), lambda b, h, c: (0, h)),
            pl.BlockSpec((CONV_W, W), lambda b, h, c: (0, n_hp + h)),
            pl.BlockSpec((1, W), lambda b, h, c: (0, h)),
            pl.BlockSpec((1, W), lambda b, h, c: (0, n_hp + h)),
            pl.BlockSpec((1, W), lambda b, h, c: (0, h)),
        ],
        out_specs=pl.BlockSpec((ML_CHUNK, W), lambda b, h, c: (b * nc + c, h)),
        out_shape=jax.ShapeDtypeStruct((T, ML_HEADS * ML_HD), BF16),
        scratch_shapes=[
            pltpu.VMEM((ML_HPS, ML_HD, ML_HD), F32), pltpu.VMEM((ML_HPS, 1, ML_HD), F32),
            pltpu.VMEM((ML_HPS, 1, 1), F32),
            pltpu.VMEM((ML_HPS, ML_CHUNK + 8, ML_HD), F32), pltpu.VMEM((ML_HPS, ML_CHUNK + 8, ML_HD), F32),
        ],
        compiler_params=_cparams(("parallel", "parallel", "arbitrary")),
        name="mlstm",
    )(gate_bias, proj, proj, proj, proj, if_arr, conv_w, conv_w, conv_b, conv_b, ng)


def _pack_bf16_pair(lo, hi):
    lo_b = pltpu.bitcast(lo.astype(BF16).astype(F32), U32)
    hi_b = pltpu.bitcast(hi.astype(BF16).astype(F32), U32)
    return (lo_b >> 16) | hi_b


def _unpack_bf16_pair(p):
    lo = pltpu.bitcast(p << 16, F32)
    hi = pltpu.bitcast(p & jnp.uint32(0xFFFF0000), F32)
    return lo, hi


def _row(ref, r):
    return ref.at[pl.ds(r, 1), :]


def _outproj_kernel(nsa_ref, ml_ref, w_ref, x_ref, g2_ref, wr_ref, br_ref, tril_ref,
                    x1_ref, xp_ref, rt_ref, cnt_ref, carry_ref):
    @pl.when(pl.program_id(0) == 0)
    def _():
        carry_ref[...] = jnp.zeros_like(carry_ref)

    half = D_MODEL // 2
    acc = jnp.dot(nsa_ref[...], w_ref[0:half, :], preferred_element_type=F32)
    acc = acc + jnp.dot(ml_ref[...], w_ref[half:, :], preferred_element_type=F32)
    x1 = x_ref[...] + acc
    x1_ref[...] = x1
    xn = x1 * lax.rsqrt(jnp.mean(x1 * x1, axis=-1, keepdims=True) + NORM_EPS) * g2_ref[...]
    xp_ref[...] = _pack_bf16_pair(xn[:, :half], xn[:, half:])
    logits = jnp.dot(xn.astype(BF16), wr_ref[...], preferred_element_type=F32) + br_ref[...]

    tm = logits.shape[0]
    lane = lax.broadcasted_iota(jnp.int32, (tm, LANE), 1)
    lane_f = lane.astype(F32)
    big = float(LANE)
    gmask = lane < MOE_GROUPS
    gmax = jnp.max(jnp.where(gmask, logits, NEG_INF), axis=1, keepdims=True)
    ge = jnp.where(gmask, jnp.exp(logits - gmax), 0.0)
    gp = ge / jnp.sum(ge, axis=1, keepdims=True)
    g_w = jnp.max(gp, axis=1, keepdims=True)
    g_idx = jnp.min(jnp.where(gmask, jnp.where(gp == g_w, lane_f, big), big), axis=1, keepdims=True)
    grp_of_lane = ((lane - MOE_GROUPS) >> EPG_LOG2).astype(F32)
    emask = jnp.where(lane >= MOE_GROUPS, grp_of_lane, -1.0) == g_idx
    emax = jnp.max(jnp.where(emask, logits, NEG_INF), axis=1, keepdims=True)
    ee = jnp.where(emask, jnp.exp(logits - emax), 0.0)
    ep = jnp.where(emask, ee / jnp.sum(ee, axis=1, keepdims=True), -1.0)
    v1 = jnp.max(ep, axis=1, keepdims=True)
    i1 = jnp.min(jnp.where(ep == v1, lane_f, big), axis=1, keepdims=True)
    ep2 = jnp.where(lane_f == i1, -1.0, ep)
    v2 = jnp.max(ep2, axis=1, keepdims=True)
    i2 = jnp.min(jnp.where(ep2 == v2, lane_f, big), axis=1, keepdims=True)
    w0 = g_w * v1 / (v1 + v2)
    w1 = g_w * v2 / (v1 + v2)
    e0 = i1 - float(MOE_GROUPS)
    e1 = i2 - float(MOE_GROUPS)

    oh0 = jnp.where(lane_f == e0, 1.0, 0.0)
    oh1 = jnp.where(lane_f == e1, 1.0, 0.0)
    pre0 = jnp.dot(tril_ref[...], oh0.astype(BF16), preferred_element_type=F32)
    pre1 = jnp.dot(tril_ref[...], oh1.astype(BF16), preferred_element_type=F32)
    carry = carry_ref[...]
    tot0 = pre0[tm - 1:tm, :]
    tot1 = pre1[tm - 1:tm, :]
    rank0 = jnp.sum(oh0 * (pre0 - 1.0 + carry), axis=1, keepdims=True)
    rank1 = jnp.sum(oh1 * (pre1 - 1.0 + carry + tot0), axis=1, keepdims=True)
    new_carry = carry + tot0 + tot1
    carry_ref[...] = new_carry
    cnt_ref[...] = jnp.broadcast_to(new_carry, cnt_ref.shape)

    rt = jnp.where(lane == 0, e0, jnp.where(lane == 1, e1, jnp.where(lane == 2, w0, jnp.where(
        lane == 3, w1, jnp.where(lane == 4, rank0, jnp.where(lane == 5, rank1, 0.0))))))
    rt_ref[...] = rt


def _out_proj(nsa_o, ml_o, w_out, x2, g2, w_r, b_r, tril):
    T = x2.shape[0]
    half = D_MODEL // 2
    return pl.pallas_call(
        _outproj_kernel,
        grid=(T // TM_OUT,),
        in_specs=[
            pl.BlockSpec((TM_OUT, half), lambda i: (i, 0)),
            pl.BlockSpec((TM_OUT, half), lambda i: (i, 0)),
            pl.BlockSpec((D_MODEL, D_MODEL), lambda i: (0, 0)),
            pl.BlockSpec((TM_OUT, D_MODEL), lambda i: (i, 0)),
            pl.BlockSpec((1, D_MODEL), lambda i: (0, 0)),
            pl.BlockSpec((D_MODEL, LANE), lambda i: (0, 0)),
            pl.BlockSpec((1, LANE), lambda i: (0, 0)),
            pl.BlockSpec((TM_OUT, TM_OUT), lambda i: (0, 0)),
        ],
        out_specs=[
            pl.BlockSpec((TM_OUT, D_MODEL), lambda i: (i, 0)),
            pl.BlockSpec((TM_OUT, half), lambda i: (i, 0)),
            pl.BlockSpec((TM_OUT, LANE), lambda i: (i, 0)),
            pl.BlockSpec((8, LANE), lambda i: (0, 0)),
        ],
        out_shape=[
            jax.ShapeDtypeStruct((T, D_MODEL), F32),
            jax.ShapeDtypeStruct((T, half), U32),
            jax.ShapeDtypeStruct((T, LANE), F32),
            jax.ShapeDtypeStruct((8, LANE), F32),
        ],
        scratch_shapes=[pltpu.VMEM((1, LANE), F32)],
        compiler_params=_cparams(("arbitrary",)),
        name="out_proj",
    )(nsa_o, ml_o, w_out, x2, g2, w_r, b_r, tril)


TD = 256
DISPATCH_BUFS = 3
DISPATCH_UNROLL = 8


def _dispatch_kernel(dest_ref, zblk_ref, meta_ref, xp_hbm, xs_hbm, zbuf_ref, zsem_ref, tbuf_ref, lsem_ref,
                     rsem_ref):
    n_assign = dest_ref.shape[0]
    nb = xs_hbm.shape[0] // BM
    n_used = meta_ref[0]
    zbuf_ref[...] = jnp.zeros_like(zbuf_ref)

    def zero_block(blk):
        return pltpu.make_async_copy(zbuf_ref, xs_hbm.at[pl.ds(blk * BM, BM), :], zsem_ref.at[0])

    def for_zero_blocks(fn):
        for e in range(N_EXPERTS):
            @pl.when(zblk_ref[e] >= 0)
            def _(e=e):
                fn(zero_block(zblk_ref[e]))

        def body(blk, _):
            fn(zero_block(blk))
            return 0
        lax.fori_loop(n_used, nb, body, 0)

    for_zero_blocks(lambda c: c.start())
    for_zero_blocks(lambda c: c.wait())

    n_tiles = n_assign // (2 * TD)

    def load(tile, slot):
        return pltpu.make_async_copy(xp_hbm.at[pl.ds(tile * TD, TD), :], tbuf_ref.at[slot], lsem_ref.at[slot])

    def wait_scatter(slot):
        for _ in range(2):
            pltpu.make_async_copy(tbuf_ref.at[slot], tbuf_ref.at[slot], rsem_ref.at[slot]).wait()

    load(0, 0).start()

    def tile_step(i, _):
        slot = i % DISPATCH_BUFS
        nslot = (i + 1) % DISPATCH_BUFS

        @pl.when(i + 1 < n_tiles)
        def _():
            @pl.when(i + 1 >= DISPATCH_BUFS)
            def _():
                wait_scatter(nslot)
            load(i + 1, nslot).start()

        load(i, slot).wait()

        def body(j, _):
            for u in range(DISPATCH_UNROLL):
                r = j * DISPATCH_UNROLL + u
                for k in range(2):
                    d = dest_ref[(i * TD + r) * 2 + k]
                    pltpu.make_async_copy(_row(tbuf_ref.at[slot], r), _row(xs_hbm, d),
                                          rsem_ref.at[slot]).start(priority=k)
            return 0
        lax.fori_loop(0, TD // DISPATCH_UNROLL, body, 0)
        return 0

    lax.fori_loop(0, n_tiles, tile_step, 0)
    for back in range(min(DISPATCH_BUFS, n_tiles)):
        wait_scatter((n_tiles - 1 - back) % DISPATCH_BUFS)


def _dispatch(dest, zblk, meta, xp, n_rows):
    half = D_MODEL // 2
    assert dest.shape[0] % (2 * TD) == 0 and dest.shape[0] // (2 * TD) >= DISPATCH_BUFS
    grid_spec = pltpu.PrefetchScalarGridSpec(
        num_scalar_prefetch=3,
        grid=(1,),
        in_specs=[pl.BlockSpec(memory_space=pl.ANY)],
        out_specs=pl.BlockSpec(memory_space=pl.ANY),
        scratch_shapes=[
            pltpu.VMEM((BM, half), U32),
            pltpu.SemaphoreType.DMA((1,)),
            pltpu.VMEM((DISPATCH_BUFS, TD, half), U32),
            pltpu.SemaphoreType.DMA((DISPATCH_BUFS,)),
            pltpu.SemaphoreType.DMA((DISPATCH_BUFS,)),
        ],
    )
    return pl.pallas_call(
        _dispatch_kernel,
        grid_spec=grid_spec,
        out_shape=jax.ShapeDtypeStruct((n_rows, half), U32),
        compiler_params=_cparams(("arbitrary",)),
        name="dispatch",
    )(dest, zblk, meta, xp)


def _expert_kernel(be_ref, nxt_ref, par_ref, meta_ref, x_ref, wg_hbm, wu_hbm, wd_hbm, y_ref,
                   wsg_ref, wsu_ref, wsd_ref, wsem_ref, wgb_ref, wub_ref, wdb_ref):
    i = pl.program_id(0)
    n_used = meta_ref[0]
    half = D_MODEL // 2

    def weight_copies(e, p):
        return (pltpu.make_async_copy(wg_hbm.at[e], wsg_ref.at[p], wsem_ref.at[p, 0]),
                pltpu.make_async_copy(wu_hbm.at[e], wsu_ref.at[p], wsem_ref.at[p, 1]),
                pltpu.make_async_copy(wd_hbm.at[e], wsd_ref.at[p], wsem_ref.at[p, 2]))

    def succ(e):
        return jnp.where(e >= 0, nxt_ref[jnp.maximum(e, 0)], -1)

    def start_weights(e, p):
        @pl.when(e >= 0)
        def _():
            for c in weight_copies(e, p):
                c.start(priority=1)

    def wait_staged(p):
        for c in weight_copies(0, p):
            c.wait()

    def cast_staged(p):
        wgb_ref[p] = wsg_ref[p].astype(BF16)
        wub_ref[p] = wsu_ref[p].astype(BF16)
        wdb_ref[p] = wsd_ref[p].astype(BF16)

    @pl.when(i == 0)
    def _():
        e0 = be_ref[0]
        start_weights(e0, 0)
        start_weights(succ(e0), 1)
        wait_staged(0)
        cast_staged(0)
        start_weights(succ(succ(e0)), 0)

    def step(p, cast_next):
        if cast_next:
            wait_staged(1 - p)
        lo, hi = _unpack_bf16_pair(x_ref[...])
        xl = lo.astype(BF16)
        xh = hi.astype(BF16)
        gt = (jnp.dot(xl, wgb_ref[p, 0:half, :], preferred_element_type=F32)
              + jnp.dot(xh, wgb_ref[p, half:, :], preferred_element_type=F32))
        up = (jnp.dot(xl, wub_ref[p, 0:half, :], preferred_element_type=F32)
              + jnp.dot(xh, wub_ref[p, half:, :], preferred_element_type=F32))
        hb = (_silu(gt) * up).astype(BF16)
        y = jnp.dot(hb, wdb_ref[p], preferred_element_type=F32)
        y_ref[...] = _pack_bf16_pair(y[:, :half], y[:, half:])
        if cast_next:
            cast_staged(1 - p)
            start_weights(succ(succ(succ(be_ref[i]))), 1 - p)

    e = be_ref[jnp.minimum(i, n_used - 1)]
    ends_expert = (i + 1 < n_used) & (be_ref[jnp.minimum(i + 1, n_used - 1)] != e)
    for p in range(2):
        for cast_next in (False, True):
            @pl.when((i < n_used) & (par_ref[i] == p)
                     & (ends_expert if cast_next else jnp.logical_not(ends_expert)))
            def _(p=p, cast_next=cast_next):
                step(p, cast_next)

    @pl.when(i >= n_used)
    def _():
        y_ref[...] = jnp.zeros_like(y_ref)


def _experts(block_expert, next_expert, block_parity, meta, xs, w_gate, w_up, w_down):
    half = D_MODEL // 2
    n_rows = xs.shape[0]
    nb = n_rows // BM
    grid_spec = pltpu.PrefetchScalarGridSpec(
        num_scalar_prefetch=4,
        grid=(nb,),
        in_specs=[
            pl.BlockSpec((BM, half), lambda i, be, nxt, par, meta: (jnp.minimum(i, meta[0] - 1), 0)),
            pl.BlockSpec(memory_space=pl.ANY), pl.BlockSpec(memory_space=pl.ANY),
            pl.BlockSpec(memory_space=pl.ANY),
        ],
        out_specs=pl.BlockSpec((BM, half), lambda i, be, nxt, par, meta: (i, 0)),
        scratch_shapes=[
            pltpu.VMEM((2, D_MODEL, D_EXPERT), F32),
            pltpu.VMEM((2, D_MODEL, D_EXPERT), F32),
            pltpu.VMEM((2, D_EXPERT, D_MODEL), F32),
            pltpu.SemaphoreType.DMA((2, 3)),
            pltpu.VMEM((2, D_MODEL, D_EXPERT), BF16),
            pltpu.VMEM((2, D_MODEL, D_EXPERT), BF16),
            pltpu.VMEM((2, D_EXPERT, D_MODEL), BF16),
        ],
    )
    return pl.pallas_call(
        _expert_kernel,
        grid_spec=grid_spec,
        out_shape=jax.ShapeDtypeStruct((n_rows, half), U32),
        compiler_params=_cparams(("arbitrary",)),
        name="experts",
    )(block_expert, next_expert, block_parity, meta, xs, w_gate, w_up, w_down)


def _combine_kernel(dest_ref, y_hbm, x1_ref, rt_ref, fg_ref, o_ref, ya_ref, yb_ref, sem_ref):
    i = pl.program_id(0)
    nt = pl.num_programs(0)
    half = D_MODEL // 2
    bufs = (ya_ref, yb_ref)

    def row_copy(tile, slot, r_tile, s, k):
        r = r_tile * 8 + s
        d = dest_ref[(tile * TC + r) * 2 + k]
        return pltpu.make_async_copy(_row(y_hbm, d), _row(bufs[slot].at[k], r), sem_ref.at[slot])

    def wait_rows(slot):
        pltpu.make_async_copy(bufs[slot], bufs[slot], sem_ref.at[slot]).wait()

    def step(slot):
        wait_rows(slot)
        for r in range(TC):
            for k in range(2):
                row_copy(i + 1, 1 - slot, r // 8, r % 8, k).start(priority=k)
        rt = rt_ref[...]
        w0 = rt[:, 2:3]
        w1 = rt[:, 3:4]
        lo0, hi0 = _unpack_bf16_pair(bufs[slot][0])
        lo1, hi1 = _unpack_bf16_pair(bufs[slot][1])
        xl = x1_ref[:, :half] + (w0 * lo0 + w1 * lo1)
        xh = x1_ref[:, half:] + (w0 * hi0 + w1 * hi1)
        ms = (jnp.sum(xl * xl, axis=1, keepdims=True) + jnp.sum(xh * xh, axis=1, keepdims=True)) / float(D_MODEL)
        inv = lax.rsqrt(ms + NORM_EPS)
        o_ref[:, :half] = xl * inv * fg_ref[:, :half]
        o_ref[:, half:] = xh * inv * fg_ref[:, half:]

        @pl.when(i == nt - 1)
        def _():
            wait_rows(1 - slot)

    @pl.when(i == 0)
    def _():
        def body(rt, _):
            for s in range(8):
                for k in range(2):
                    row_copy(0, 0, rt, s, k).start(priority=k)
            return 0
        lax.fori_loop(0, TC // 8, body, 0)

    @pl.when(i % 2 == 0)
    def _():
        step(0)

    @pl.when(i % 2 == 1)
    def _():
        step(1)


def _combine(dest, ys, x1, route, fg):
    T = x1.shape[0]
    half = D_MODEL // 2
    grid_spec = pltpu.PrefetchScalarGridSpec(
        num_scalar_prefetch=1,
        grid=(T // TC,),
        in_specs=[
            pl.BlockSpec(memory_space=pl.ANY),
            pl.BlockSpec((TC, D_MODEL), lambda i, d: (i, 0)),
            pl.BlockSpec((TC, LANE), lambda i, d: (i, 0)),
            pl.BlockSpec((1, D_MODEL), lambda i, d: (0, 0)),
        ],
        out_specs=pl.BlockSpec((TC, D_MODEL), lambda i, d: (i, 0)),
        scratch_shapes=[
            pltpu.VMEM((2, TC, half), U32),
            pltpu.VMEM((2, TC, half), U32),
            pltpu.SemaphoreType.DMA((2,)),
        ],
    )
    return pl.pallas_call(
        _combine_kernel,
        grid_spec=grid_spec,
        out_shape=jax.ShapeDtypeStruct((T, D_MODEL), F32),
        compiler_params=_cparams(("arbitrary",)),
        name="combine",
    )(dest, ys, x1, route, fg)


def _cover_matrix(S):
    n_blk = S // CMP_STRIDE
    n_sel = S // SEL_BLOCK
    cs = np.arange(n_blk) * CMP_STRIDE
    ss = np.arange(n_sel) * SEL_BLOCK
    shared = np.minimum(cs[:, None] + CMP_LEN, ss[None, :] + SEL_BLOCK) - np.maximum(cs[:, None], ss[None, :])
    return (np.clip(shared, 0, None) / CMP_LEN).T.astype(np.float32)


def _block_mask_matrix(S):
    n_sel = S // SEL_BLOCK
    assert n_sel <= LANE
    e = np.zeros((S, LANE), np.float32)
    e[np.arange(S), np.arange(S) // SEL_BLOCK] = NEG_INF
    return e


def _inv_freq_row():
    inv = np.power(np.float32(ROPE_THETA), -np.arange(ROPE_HALF, dtype=np.float32) * 2.0 / ROPE_DIM)
    return np.tile(inv, ROPE_SLOTS).reshape(1, LANE).astype(np.float32)


def _packed_positions(positions, T):
    rows = TS_ROPE // ROPE_SLOTS
    p = positions.reshape(T // TS_ROPE, ROPE_SLOTS, rows).transpose(0, 2, 1).astype(F32)
    return jnp.repeat(p, ROPE_HALF, axis=2).reshape(T // ROPE_SLOTS, LANE)


def _layer(x2, positions, B, S, norm1_g, w_in, cmp_pe_k, cmp_pe_v, cmp_wk1, cmp_wk2, cmp_wv1, cmp_wv2,
           nsa_norm_g, conv_w, conv_b, b_igate, b_fgate, mlstm_norm_g, w_out, norm2_g,
           w_group, b_group, w_router, b_router, w_exp_gate, w_exp_up, w_exp_down, out_norm_g):
    T = B * S
    w_main, w_small = _weight_prep(jnp.transpose(w_in[0]))
    posb = _packed_positions(positions, T)
    invf = jnp.asarray(_inv_freq_row())

    proj, small = _in_proj(x2, norm1_g.reshape(1, -1), w_main, w_small)
    q_r, k_r = _rope(posb, invf, proj)
    kc, vc = _compress(k_r, proj, B, S, cmp_pe_k, cmp_pe_v, cmp_wk1.astype(BF16), cmp_wk2.astype(BF16),
                       cmp_wv1.astype(BF16), cmp_wv2.astype(BF16))
    eye = np.eye(S // SEL_BLOCK, LANE, dtype=np.float32)
    nsa_o = _nsa(q_r, k_r, proj, small, kc, vc, jnp.asarray(_cover_matrix(S), BF16),
                 jnp.asarray(_block_mask_matrix(S), BF16), jnp.asarray(eye, BF16),
                 nsa_norm_g.reshape(1, -1), B, S)

    if_arr = small[:, 24:32].reshape(B, S, 2, ML_HEADS).transpose(0, 3, 2, 1)
    gate_bias = jnp.stack([b_igate, b_fgate]).astype(F32)
    ml_o = _mlstm(proj, if_arr, gate_bias, conv_w, conv_b.reshape(1, -1), mlstm_norm_g.reshape(1, -1), B, S)

    w_r = jnp.concatenate([w_group, w_router, jnp.zeros((D_MODEL, LANE - MOE_GROUPS - N_EXPERTS), F32)],
                          axis=1).astype(BF16)
    b_r = jnp.concatenate([b_group, b_router, jnp.zeros((LANE - MOE_GROUPS - N_EXPERTS,), F32)]).reshape(1, LANE)
    tril = jnp.asarray(np.tril(np.ones((TM_OUT, TM_OUT), np.float32)), BF16)
    x1, xp, route, cnt = _out_proj(nsa_o, ml_o, w_out.astype(BF16), x2, norm2_g.reshape(1, -1), w_r, b_r, tril)

    n_rows = T * 2 + N_EXPERTS * BM
    counts = cnt[0, :N_EXPERTS].astype(jnp.int32)
    pcounts = (counts + BM - 1) // BM * BM
    pends = jnp.cumsum(pcounts)
    pstarts = pends - pcounts
    eid = route[:, 0:2].astype(jnp.int32)
    onehot = (eid[..., None] == jnp.arange(N_EXPERTS, dtype=jnp.int32)).astype(F32)
    start_blk = jnp.einsum('tke,e->tk', onehot, (pstarts // BM).astype(F32)).astype(jnp.int32)
    dest = (start_blk * BM + route[:, 4:6].astype(jnp.int32)).reshape(T * 2)
    block_expert = jnp.minimum(jnp.searchsorted(pends, jnp.arange(n_rows // BM) * BM, side='right'),
                               N_EXPERTS - 1).astype(jnp.int32)
    present = jnp.where(counts > 0, jnp.arange(N_EXPERTS, dtype=jnp.int32), N_EXPERTS)
    later = jnp.concatenate([lax.cummin(present[::-1])[::-1][1:], jnp.full((1,), N_EXPERTS, jnp.int32)])
    next_expert = jnp.where(later < N_EXPERTS, later, -1).astype(jnp.int32)
    meta = (pends[-1:] // BM).astype(jnp.int32)

    last_block = jnp.where(counts > 0, pends // BM - 1, -1).astype(jnp.int32)

    xs = _dispatch(dest, last_block, meta, xp, n_rows)
    ordinal = jnp.cumsum((counts > 0).astype(jnp.int32)) - 1
    block_parity = (ordinal[block_expert] % 2).astype(jnp.int32)
    ys = _experts(block_expert, next_expert, block_parity, meta, xs, w_exp_gate, w_exp_up, w_exp_down)
    dest_pad = jnp.concatenate([dest, jnp.zeros((2 * TC,), jnp.int32)])
    return _combine(dest_pad, ys, x1, route, out_norm_g.reshape(1, -1))


def kernel(x, positions, norm1_g, w_in, cmp_pe_k, cmp_pe_v, cmp_wk1, cmp_wk2, cmp_wv1, cmp_wv2, nsa_norm_g,
           conv_w, conv_b, b_igate, b_fgate, mlstm_norm_g, w_out, norm2_g, w_group, b_group, w_router,
           b_router, w_exp_gate, w_exp_up, w_exp_down, final_norm_g):
    B, S, D = x.shape
    assert D == D_MODEL and norm1_g.shape[0] == 1, "single-layer, D_MODEL-wide configuration only"
    assert S % ML_CHUNK == 0 and S % SEL_BUCKET == 0 and S >= WIN_KEYS and (B * S) % TM_IN == 0
    out = _layer(x.reshape(B * S, D), positions, B, S, norm1_g[0], w_in, cmp_pe_k[0], cmp_pe_v[0],
                 cmp_wk1[0], cmp_wk2[0], cmp_wv1[0], cmp_wv2[0], nsa_norm_g[0], conv_w[0], conv_b[0],
                 b_igate[0], b_fgate[0], mlstm_norm_g[0], w_out[0], norm2_g[0], w_group[0], b_group[0],
                 w_router[0], b_router[0], w_exp_gate[0], w_exp_up[0], w_exp_down[0], final_norm_g)
    return out.reshape(B, S, D)
```

```python
import functools

import numpy as np
import jax
import jax.numpy as jnp
from jax import lax
from jax.experimental import pallas as pl
from jax.experimental.pallas import tpu as pltpu

F32 = jnp.float32
BF16 = jnp.bfloat16
U32 = jnp.uint32

D_MODEL = 2048
NSA_HEADS = 8
NSA_GROUPS = 2
NSA_J = NSA_HEADS // NSA_GROUPS
HD = 128
CMP_LEN = 32
CMP_STRIDE = 16
SEL_BLOCK = 64
SEL_TOPK = 8
WINDOW = 512
ROPE_THETA = 500000.0
ROPE_DIM = 32
ROPE_HALF = 16
ML_HEADS = 4
ML_HD = 256
CONV_W = 4
MOE_GROUPS = 8
EPG = 8
N_EXPERTS = 64
D_EXPERT = 512
SEL_BLOCK_LOG2 = 6
EPG_LOG2 = 3
assert (1 << SEL_BLOCK_LOG2) == SEL_BLOCK and (1 << EPG_LOG2) == EPG
NORM_EPS = 1e-6
NEG_INF = -1e30
FORCE_SCORE = 1000.0

COL_Q = 0
COL_KV = 1024
COL_QKM = 2560
COL_VM = 4608
COL_OM = 5632
N_MAIN = 6656

LANE = 128
VMEM_LIMIT = 56 * 1024 * 1024

TM_IN = 1024
TN_IN = 1664
TS_ROPE = 256
TQ = 256
SEL_BUCKET = 256
WIN_KEYS = WINDOW + TQ
ML_CHUNK = 256
TM_OUT = 512
OUT_SUBS = 2
OUT_COLS = 512
BM = 256
TC = 256

NT_DIMS = (((1,), (1,)), ((), ()))
TN_DIMS = (((0,), (0,)), ((), ()))


def _cparams(sem):
    return pltpu.CompilerParams(dimension_semantics=sem, vmem_limit_bytes=VMEM_LIMIT)


def _sigmoid(x):
    return 0.5 * jnp.tanh(0.5 * x) + 0.5


def _silu(x):
    h = 0.5 * x
    return h + h * jnp.tanh(h)


W_GATES = 2560
W_QKM = 2584
W_IGATE = 6680
W_END = 6688
TR_PREP = 512
assert W_GATES % TR_PREP == 0 and N_MAIN % TR_PREP == 0 and W_IGATE - W_QKM == N_MAIN - W_GATES


def _wprep_kernel(wt_hbm, wm_ref, ws_ref, wbuf_ref, sbuf_ref, sem_ref, ssem_ref):
    i = pl.program_id(0)
    n_gate = W_QKM - W_GATES
    n_if = W_END - W_IGATE

    def load(blk, slot):
        start = pl.multiple_of(blk * TR_PREP + jnp.where(blk * TR_PREP >= W_GATES, n_gate, 0), 8)
        return pltpu.make_async_copy(wt_hbm.at[pl.ds(start, TR_PREP), :], wbuf_ref.at[slot], sem_ref.at[slot])

    def small_loads():
        return (pltpu.make_async_copy(wt_hbm.at[pl.ds(W_GATES, n_gate), :], sbuf_ref.at[pl.ds(0, n_gate), :],
                                      ssem_ref.at[0]),
                pltpu.make_async_copy(wt_hbm.at[pl.ds(W_IGATE, n_if), :], sbuf_ref.at[pl.ds(n_gate, n_if), :],
                                      ssem_ref.at[1]))

    @pl.when(i == 0)
    def _():
        load(0, 0).start()
        for c in small_loads():
            c.start()
        for c in small_loads():
            c.wait()
        ws_ref[...] = jnp.zeros_like(ws_ref)
        ws_ref[0:n_gate + n_if, :] = sbuf_ref[...].astype(BF16)

    @pl.when(i + 1 < pl.num_programs(0))
    def _():
        load(i + 1, (i + 1) % 2).start()

    load(i, i % 2).wait()
    wm_ref[...] = wbuf_ref[i % 2].astype(BF16)


def _weight_prep(wt):
    assert wt.shape == (W_END, D_MODEL)
    n_small = (W_QKM - W_GATES) + (W_END - W_IGATE)
    return pl.pallas_call(
        _wprep_kernel,
        grid=(N_MAIN // TR_PREP,),
        in_specs=[pl.BlockSpec(memory_space=pl.ANY)],
        out_specs=[pl.BlockSpec((TR_PREP, D_MODEL), lambda i: (i, 0)),
                   pl.BlockSpec((LANE, D_MODEL), lambda i: (0, 0))],
        out_shape=[jax.ShapeDtypeStruct((N_MAIN, D_MODEL), BF16),
                   jax.ShapeDtypeStruct((LANE, D_MODEL), BF16)],
        scratch_shapes=[pltpu.VMEM((2, TR_PREP, D_MODEL), F32), pltpu.VMEM((n_small, D_MODEL), F32),
                        pltpu.SemaphoreType.DMA((2,)), pltpu.SemaphoreType.DMA((2,))],
        compiler_params=_cparams(("arbitrary",)),
        name="weight_prep",
    )(wt)


def _inproj_kernel(x_ref, g_ref, w_ref, ws_ref, o_ref, os_ref, h_ref):
    @pl.when(pl.program_id(1) == 0)
    def _():
        x = x_ref[...]
        ms = jnp.mean(x * x, axis=-1, keepdims=True)
        h_ref[...] = (x * lax.rsqrt(ms + NORM_EPS) * g_ref[...]).astype(BF16)
        os_ref[...] = lax.dot_general(h_ref[...], ws_ref[...], NT_DIMS, preferred_element_type=F32)

    o_ref[...] = lax.dot_general(h_ref[...], w_ref[...], NT_DIMS,
                                 preferred_element_type=F32).astype(o_ref.dtype)


def _in_proj(x2, g1, w_main, w_small):
    T = x2.shape[0]
    return pl.pallas_call(
        _inproj_kernel,
        grid=(T // TM_IN, N_MAIN // TN_IN),
        in_specs=[
            pl.BlockSpec((TM_IN, D_MODEL), lambda m, n: (m, 0)),
            pl.BlockSpec((1, D_MODEL), lambda m, n: (0, 0)),
            pl.BlockSpec((TN_IN, D_MODEL), lambda m, n: (n, 0)),
            pl.BlockSpec((LANE, D_MODEL), lambda m, n: (0, 0)),
        ],
        out_specs=[
            pl.BlockSpec((TM_IN, TN_IN), lambda m, n: (m, n)),
            pl.BlockSpec((TM_IN, LANE), lambda m, n: (m, 0)),
        ],
        out_shape=[
            jax.ShapeDtypeStruct((T, N_MAIN), BF16),
            jax.ShapeDtypeStruct((T, LANE), F32),
        ],
        scratch_shapes=[pltpu.VMEM((TM_IN, D_MODEL), BF16)],
        compiler_params=_cparams(("parallel", "arbitrary")),
        name="in_proj",
    )(x2, g1, w_main, w_small)


ROPE_SLOTS = LANE // ROPE_HALF


def _rope_kernel(pos_ref, invf_ref, q_ref, kc_ref, ks_ref, kw_ref, qo_ref, ko_ref):
    ang = pos_ref[...] * invf_ref[...]
    c = jnp.cos(ang)
    s = jnp.sin(ang)
    rows = ang.shape[0]
    lane = lax.broadcasted_iota(jnp.int32, ang.shape, 1)
    lo = lane < ROPE_HALF
    mid = lane < ROPE_DIM
    scale = HD ** -0.5

    def lanes_from(x, src):
        shift = (-src) % LANE
        return x if shift == 0 else pltpu.roll(x, shift, 1)

    for slot in range(ROPE_SLOTS):
        src = slot * ROPE_HALF
        cf = jnp.where(lo, lanes_from(c, src), jnp.where(mid, lanes_from(c, src - ROPE_HALF), 1.0))
        sa = jnp.where(lo, -lanes_from(s, src), 0.0)
        sb = jnp.where(lo, 0.0, jnp.where(mid, lanes_from(s, src - ROPE_HALF), 0.0))
        rs = slice(slot * rows, (slot + 1) * rows)

        def rope(x):
            return x * cf + pltpu.roll(x, LANE - ROPE_HALF, 1) * sa + pltpu.roll(x, ROPE_HALF, 1) * sb

        for h in range(NSA_HEADS):
            sl = slice(h * HD, (h + 1) * HD)
            qo_ref[rs, sl] = (rope(q_ref[rs, sl].astype(F32)) * scale).astype(BF16)
        for i, r in enumerate((kc_ref, ks_ref, kw_ref)):
            for g in range(NSA_GROUPS):
                sl = slice(g * HD, (g + 1) * HD)
                so = slice(i * 2 * HD + g * HD, i * 2 * HD + (g + 1) * HD)
                ko_ref[rs, so] = rope(r[rs, sl].astype(F32)).astype(BF16)


def _rope(posb, invf, proj):
    T = proj.shape[0]
    kvb = COL_KV // 256
    return pl.pallas_call(
        _rope_kernel,
        grid=(T // TS_ROPE,),
        in_specs=[
            pl.BlockSpec((TS_ROPE // ROPE_SLOTS, LANE), lambda i: (i, 0)),
            pl.BlockSpec((1, LANE), lambda i: (0, 0)),
            pl.BlockSpec((TS_ROPE, 1024), lambda i: (i, 0)),
            pl.BlockSpec((TS_ROPE, 256), lambda i: (i, kvb + 0)),
            pl.BlockSpec((TS_ROPE, 256), lambda i: (i, kvb + 2)),
            pl.BlockSpec((TS_ROPE, 256), lambda i: (i, kvb + 4)),
        ],
        out_specs=[
            pl.BlockSpec((TS_ROPE, 1024), lambda i: (i, 0)),
            pl.BlockSpec((TS_ROPE, 768), lambda i: (i, 0)),
        ],
        out_shape=[
            jax.ShapeDtypeStruct((T, 1024), BF16),
            jax.ShapeDtypeStruct((T, 768), BF16),
        ],
        compiler_params=_cparams(("parallel",)),
        name="rope",
    )(posb, invf, proj, proj, proj, proj)


def _compress_kernel(k_ref, v_ref, pek_ref, pev_ref, w1k_ref, w2k_ref, w1v_ref, w2v_ref,
                     kc_ref, vc_ref, xs_ref):
    S = k_ref.shape[0]
    n_blk = S // CMP_STRIDE
    for src, pe, w1, w2, dst in ((k_ref, pek_ref, w1k_ref, w2k_ref, kc_ref),
                                 (v_ref, pev_ref, w1v_ref, w2v_ref, vc_ref)):
        xs_ref[0:S, :] = src[...].astype(F32)
        xs_ref[S:S + CMP_LEN, :] = jnp.zeros((CMP_LEN, HD), F32)
        acc = jnp.zeros((n_blk, HD), F32)
        for l in range(CMP_LEN):
            a = xs_ref[pl.ds(l, n_blk, stride=CMP_STRIDE), :] + pe[l:l + 1, :]
            acc = acc + jnp.dot(a.astype(BF16), w1[l * HD:(l + 1) * HD, :],
                                preferred_element_type=F32)
        hid = _silu(acc)
        out = jnp.dot(hid.astype(BF16), w2[...], preferred_element_type=F32)
        dst[0, 0] = out.astype(BF16)


def _compress(k_r, proj, B, S, pek, pev, w1k, w2k, w1v, w2v):
    n_blk = S // CMP_STRIDE
    vcol = (COL_KV + 256) // HD
    full = lambda shape: pl.BlockSpec(shape, lambda b, g: tuple(0 for _ in shape))
    return pl.pallas_call(
        _compress_kernel,
        grid=(B, NSA_GROUPS),
        in_specs=[
            pl.BlockSpec((S, HD), lambda b, g: (b, g)),
            pl.BlockSpec((S, HD), lambda b, g: (b, vcol + g)),
            full((CMP_LEN, HD)), full((CMP_LEN, HD)),
            full((CMP_LEN * HD, HD)), full((HD, HD)),
            full((CMP_LEN * HD, HD)), full((HD, HD)),
        ],
        out_specs=[
            pl.BlockSpec((1, 1, n_blk, HD), lambda b, g: (b, g, 0, 0)),
            pl.BlockSpec((1, 1, n_blk, HD), lambda b, g: (b, g, 0, 0)),
        ],
        out_shape=[
            jax.ShapeDtypeStruct((B, NSA_GROUPS, n_blk, HD), BF16),
            jax.ShapeDtypeStruct((B, NSA_GROUPS, n_blk, HD), BF16),
        ],
        scratch_shapes=[pltpu.VMEM((S + CMP_LEN, HD), F32)],
        compiler_params=_cparams(("parallel", "parallel")),
        name="compress",
    )(k_r, proj, pek, pev, w1k, w2k, w1v, w2v)


def _nsa_kernel(q_ref, kc_ref, vc_ref, ks_ref, vs_ref, kw_ref, vw_ref, sm_ref, covt_ref, eneg_ref, eye_ref,
                ng_ref, o_ref, obuf_ref, owin_ref, osel_ref):
    qi = pl.program_id(1)
    q0 = qi * TQ
    R = NSA_J * TQ
    n_cmp = kc_ref.shape[2]
    n_sel = ks_ref.shape[0] // SEL_BLOCK
    n_win = WIN_KEYS // TQ

    def row_t(shape):
        r = lax.broadcasted_iota(jnp.int32, shape, 0)
        return q0 + (r & (TQ - 1))

    qgs = [jnp.concatenate([q_ref[:, (g * NSA_J + j) * HD:(g * NSA_J + j + 1) * HD]
                            for j in range(NSA_J)], axis=0) for g in range(NSA_GROUPS)]

    def with_ones(v):
        lane = lax.broadcasted_iota(jnp.int32, v.shape, 1)
        return jnp.concatenate([v, jnp.where(lane == 0, 1.0, 0.0).astype(BF16)], axis=1)

    def window(masks):
        w0 = pl.multiple_of(jnp.maximum(q0 - WINDOW, 0), TQ)
        for g in range(NSA_GROUPS):
            kt = kw_ref[pl.ds(w0, WIN_KEYS), g * HD:(g + 1) * HD]
            vt = with_ones(vw_ref[pl.ds(w0, WIN_KEYS), g * HD:(g + 1) * HD])
            sc = masks(lax.dot_general(qgs[g], kt, NT_DIMS, preferred_element_type=F32), w0)
            mw = jnp.max(sc, axis=1, keepdims=True)
            pw = jnp.exp((sc - mw).astype(BF16))
            acc = jnp.dot(pw, vt, preferred_element_type=F32)
            owin_ref[g] = acc[:, 0:HD] / acc[:, HD:HD + 1]

    def band_masks(sc, w0):
        d = (lax.broadcasted_iota(jnp.int32, (R, TQ), 1)
             - (lax.broadcasted_iota(jnp.int32, (R, TQ), 0) & (TQ - 1)))
        first = jnp.where(d > 0, sc[:, 0:TQ], NEG_INF)
        last = jnp.where(d <= 0, sc[:, (n_win - 1) * TQ:], NEG_INF)
        return jnp.concatenate([first, sc[:, TQ:(n_win - 1) * TQ], last], axis=1)

    def general_masks(sc, w0):
        diff = row_t((R, WIN_KEYS)) - (w0 + lax.broadcasted_iota(jnp.int32, (R, WIN_KEYS), 1))
        keep = jnp.where(diff >= 0, diff, WINDOW) < WINDOW
        return jnp.where(keep, sc, NEG_INF)

    window(general_masks)

    gates = _sigmoid(sm_ref[...])
    o_cmps = []
    qps = []
    for g in range(NSA_GROUPS):
        qg = qgs[g]

        s = lax.dot_general(qg, kc_ref[0, g], NT_DIMS, preferred_element_type=F32)
        n_lane = lax.broadcasted_iota(jnp.int32, (R, n_cmp), 1)
        cmask = (n_lane * CMP_STRIDE + (CMP_LEN - 1)) <= row_t((R, n_cmp))
        s = jnp.where(cmask, s, NEG_INF)
        m = jnp.max(s, axis=1, keepdims=True)
        e = jnp.where(cmask, jnp.exp(s - m), 0.0)
        l = jnp.sum(e, axis=1, keepdims=True)
        p = (e / jnp.where(l > 0.0, l, 1.0)).astype(BF16)
        o_cmps.append(jnp.dot(p, vc_ref[0, g], preferred_element_type=F32))
        impr = lax.dot_general(covt_ref[...], p, NT_DIMS, preferred_element_type=F32)
        imp = impr[:, 0:TQ]
        for j in range(1, NSA_J):
            imp = imp + impr[:, j * TQ:(j + 1) * TQ]

        m_sub = lax.broadcasted_iota(jnp.int32, (n_sel, TQ), 0)
        jt = (q0 + lax.broadcasted_iota(jnp.int32, (n_sel, TQ), 1)) >> SEL_BLOCK_LOG2
        forced = jnp.where(m_sub == 0, FORCE_SCORE,
                           jnp.where(m_sub == jt, FORCE_SCORE,
                                     jnp.where(m_sub == jt - 1, FORCE_SCORE, 0.0)))
        score = jnp.where(m_sub <= jt, imp + forced, -1.0)
        rank = jnp.zeros((n_sel, TQ), F32)
        for mp in range(n_sel):
            row = score[mp:mp + 1, :]
            ge = jnp.where(row >= score, 1.0, 0.0)
            gt = jnp.where(row > score, 1.0, 0.0)
            rank = rank + jnp.where(m_sub > mp, ge, gt)
        notsel_t = jnp.where(rank < float(min(SEL_TOPK, n_sel)), 0.0, 1.0).astype(BF16)
        notsel = lax.dot_general(notsel_t, eye_ref[...], TN_DIMS,
                                 preferred_element_type=F32).astype(BF16)
        qps.append(jnp.concatenate([qg, jnp.concatenate([notsel] * NSA_J, axis=0)], axis=1))

    def sel_oneshot(nk):
        for g in range(NSA_GROUPS):
            kt = jnp.concatenate([ks_ref[0:nk, g * HD:(g + 1) * HD], eneg_ref[0:nk, :]], axis=1)
            vt = with_ones(vs_ref[0:nk, g * HD:(g + 1) * HD])
            sc = lax.dot_general(qps[g], kt, NT_DIMS, preferred_element_type=F32)
            klane = lax.broadcasted_iota(jnp.int32, (R, SEL_BUCKET), 1) + (nk - SEL_BUCKET)
            tail = jnp.where(klane <= row_t((R, SEL_BUCKET)), sc[:, nk - SEL_BUCKET:], NEG_INF)
            sc = tail if nk == SEL_BUCKET else jnp.concatenate([sc[:, 0:nk - SEL_BUCKET], tail], axis=1)
            ms = jnp.max(sc, axis=1, keepdims=True)
            acc = jnp.dot(jnp.exp((sc - ms).astype(BF16)), vt, preferred_element_type=F32)
            osel_ref[g] = acc[:, 0:HD] / acc[:, HD:HD + 1]

    for b in range(ks_ref.shape[0] // SEL_BUCKET):
        @pl.when(q0 // SEL_BUCKET == b)
        def _(b=b):
            sel_oneshot((b + 1) * SEL_BUCKET)

    ssq = jnp.zeros((TQ, 1), F32)
    for g in range(NSA_GROUPS):
        o_sel = osel_ref[g]
        o_cmp = o_cmps[g]
        o_win = owin_ref[g]
        for j in range(NSA_J):
            h = g * NSA_J + j
            rs = slice(j * TQ, (j + 1) * TQ)
            o = (gates[:, 3 * h:3 * h + 1] * o_cmp[rs] + gates[:, 3 * h + 1:3 * h + 2] * o_sel[rs]
                 + gates[:, 3 * h + 2:3 * h + 3] * o_win[rs])
            ssq = ssq + jnp.sum(o * o, axis=1, keepdims=True)
            obuf_ref[:, h * HD:(h + 1) * HD] = o

    inv = lax.rsqrt(ssq / float(NSA_HEADS * HD) + NORM_EPS)
    o_ref[...] = (obuf_ref[...] * inv * ng_ref[...]).astype(BF16)


def _nsa(q_r, k_r, proj, small, kc, vc, covt, eneg, eye, ng, B, S):
    T = B * S
    nq = S // TQ
    n_blk = S // CMP_STRIDE
    n_sel = S // SEL_BLOCK
    kvb = COL_KV // 256
    return pl.pallas_call(
        _nsa_kernel,
        grid=(B, nq),
        in_specs=[
            pl.BlockSpec((TQ, 1024), lambda b, i: (b * nq + i, 0)),
            pl.BlockSpec((1, NSA_GROUPS, n_blk, HD), lambda b, i: (b, 0, 0, 0)),
            pl.BlockSpec((1, NSA_GROUPS, n_blk, HD), lambda b, i: (b, 0, 0, 0)),
            pl.BlockSpec((S, 256), lambda b, i: (b, 1)),
            pl.BlockSpec((S, 256), lambda b, i: (b, kvb + 3)),
            pl.BlockSpec((S, 256), lambda b, i: (b, 2)),
            pl.BlockSpec((S, 256), lambda b, i: (b, kvb + 5)),
            pl.BlockSpec((TQ, LANE), lambda b, i: (b * nq + i, 0)),
            pl.BlockSpec((n_sel, n_blk), lambda b, i: (0, 0)),
            pl.BlockSpec((S, LANE), lambda b, i: (0, 0)),
            pl.BlockSpec((n_sel, LANE), lambda b, i: (0, 0)),
            pl.BlockSpec((1, 1024), lambda b, i: (0, 0)),
        ],
        out_specs=pl.BlockSpec((TQ, 1024), lambda b, i: (b * nq + i, 0)),
        out_shape=jax.ShapeDtypeStruct((T, 1024), BF16),
        scratch_shapes=[pltpu.VMEM((TQ, 1024), F32), pltpu.VMEM((NSA_GROUPS, NSA_J * TQ, HD), F32),
                        pltpu.VMEM((NSA_GROUPS, NSA_J * TQ, HD), F32)],
        compiler_params=_cparams(("parallel", "parallel")),
        name="nsa",
    )(q_r, kc, vc, k_r, proj, k_r, proj, small, covt, eneg, eye, ng)


ML_HPS = 2


def _mlstm_kernel(bias_ref, q_ref, k_ref, v_ref, om_ref, if_ref, cwq_ref, cwk_ref, cbq_ref, cbk_ref,
                  ng_ref, o_ref, c_ref, n_ref, m_ref, xq_ref, xk_ref):
    hp = pl.program_id(1)
    L = ML_CHUNK

    @pl.when(pl.program_id(2) == 0)
    def _():
        c_ref[...] = jnp.zeros_like(c_ref)
        n_ref[...] = jnp.zeros_like(n_ref)
        m_ref[...] = jnp.zeros_like(m_ref)
        xq_ref[:, 0:8, :] = jnp.zeros((ML_HPS, 8, ML_HD), F32)
        xk_ref[:, 0:8, :] = jnp.zeros((ML_HPS, 8, ML_HD), F32)

    def conv_silu(x_ref, hh, xb_ref, w_ref, b_ref):
        cs = slice(hh * ML_HD, (hh + 1) * ML_HD)
        xb_ref[hh, 8:8 + L, :] = x_ref[:, cs].astype(F32)
        y = b_ref[:, cs] + xb_ref[hh, 8:8 + L, :] * w_ref[CONV_W - 1:CONV_W, cs]
        for k in range(1, CONV_W):
            y = y + xb_ref[hh, 8 - k:8 - k + L, :] * w_ref[CONV_W - 1 - k:CONV_W - k, cs]
        xb_ref[hh, 0:8, :] = xb_ref[hh, L:L + 8, :]
        return _silu(y)

    r = lax.broadcasted_iota(jnp.int32, (L, L), 0)
    cidx = lax.broadcasted_iota(jnp.int32, (L, L), 1)
    tril = cidx <= r
    eye = cidx == r

    for hh in range(ML_HPS):
        h = hp * ML_HPS + hh
        cs = slice(hh * ML_HD, (hh + 1) * ML_HD)
        qf = conv_silu(q_ref, hh, xq_ref, cwq_ref, cbq_ref) * (ML_HD ** -0.5)
        kf = conv_silu(k_ref, hh, xk_ref, cwk_ref, cbk_ref)
        qb = qf.astype(BF16)
        kb = kf.astype(BF16)
        vb = v_ref[:, cs]
        vf = vb.astype(F32)

        ic = if_ref[0, hh, 0:1, :] + bias_ref[0, h]
        fp = if_ref[0, hh, 1:2, :] + bias_ref[1, h]
        fc = jnp.minimum(fp, 0.0) - jnp.log(1.0 + jnp.exp(-jnp.abs(fp)))

        fc_b = jnp.broadcast_to(fc, (L, L))
        ic_b = jnp.broadcast_to(ic, (L, L))
        b_col = jnp.sum(jnp.where(tril, fc_b, 0.0), axis=1, keepdims=True)
        fc_col = jnp.sum(jnp.where(eye, fc_b, 0.0), axis=1, keepdims=True)
        ic_col = jnp.sum(jnp.where(eye, ic_b, 0.0), axis=1, keepdims=True)
        b_row = jnp.sum(jnp.where(r <= cidx, jnp.broadcast_to(fc_col, (L, L)), 0.0),
                        axis=0, keepdims=True)
        b_last = b_col[L - 1:L, :]
        m_prev = m_ref[hh]

        d_log = jnp.where(tril, b_col - b_row + ic, NEG_INF)
        inter = b_col + m_prev
        m_t = jnp.maximum(inter, jnp.max(d_log, axis=1, keepdims=True))
        w_intra = jnp.exp(d_log - m_t)
        w_inter = jnp.exp(inter - m_t)
        qk = lax.dot_general(qb, kb, NT_DIMS, preferred_element_type=F32) * w_intra
        num = (jnp.dot(qk.astype(BF16), vb, preferred_element_type=F32)
               + w_inter * lax.dot_general(qb, c_ref[hh].astype(BF16), NT_DIMS, preferred_element_type=F32))
        den = jnp.sum(qk, axis=1, keepdims=True) + w_inter * jnp.sum(qf * n_ref[hh], axis=1, keepdims=True)
        hm = num / jnp.maximum(jnp.abs(den), jnp.exp(-m_t))

        w_log = b_last - b_col + ic_col
        m_new = jnp.maximum(b_last + m_prev, jnp.max(w_log, axis=0, keepdims=True))
        w_state = jnp.exp(w_log - m_new)
        decay = jnp.exp(b_last + m_prev - m_new)
        c_ref[hh] = decay * c_ref[hh] + lax.dot_general((w_state * vf).astype(BF16), kb, TN_DIMS,
                                                        preferred_element_type=F32)
        n_ref[hh] = decay * n_ref[hh] + jnp.sum(w_state * kf, axis=0, keepdims=True)
        m_ref[hh] = m_new

        hn = hm * lax.rsqrt(jnp.mean(hm * hm, axis=1, keepdims=True) + NORM_EPS) * ng_ref[:, cs]
        o_ref[:, cs] = (hn * _sigmoid(om_ref[:, cs].astype(F32))).astype(BF16)


def _mlstm(proj, if_arr, gate_bias, conv_w, conv_b, ng, B, S):
    T = B * S
    nc = S // ML_CHUNK
    W = ML_HPS * ML_HD
    n_hp = ML_HEADS // ML_HPS
    assert COL_QKM % W == 0 and COL_VM % W == 0 and COL_OM % W == 0 and ML_HEADS % ML_HPS == 0
    cq, ck, cv, co = COL_QKM // W, COL_QKM // W + n_hp, COL_VM // W, COL_OM // W
    rows = lambda col0: pl.BlockSpec((ML_CHUNK, W), lambda b, h, c: (b * nc + c, col0 + h))
    return pl.pallas_call(
        _mlstm_kernel,
        grid=(B, n_hp, nc),
        in_specs=[
            pl.BlockSpec(memory_space=pltpu.SMEM),
            rows(cq), rows(ck), rows(cv), rows(co),
            pl.BlockSpec((1, ML_HPS, 2, ML_CHUNK), lambda b, h, c: (b, h, 0, c)),
            pl.BlockSpec((CONV_W, W), lambda b, h, c: (0, h)),
            pl.BlockSpec((CONV_W, W), lambda b, h, c: (0, n_hp + h)),
            pl.BlockSpec((1, W), lambda b, h, c: (0, h)),
            pl.BlockSpec((1, W), lambda b, h, c: (0, n_hp + h)),
            pl.BlockSpec((1, W), lambda b, h, c: (0, h)),
        ],
        out_specs=pl.BlockSpec((ML_CHUNK, W), lambda b, h, c: (b * nc + c, h)),
        out_shape=jax.ShapeDtypeStruct((T, ML_HEADS * ML_HD), BF16),
        scratch_shapes=[
            pltpu.VMEM((ML_HPS, ML_HD, ML_HD), F32), pltpu.VMEM((ML_HPS, 1, ML_HD), F32),
            pltpu.VMEM((ML_HPS, 1, 1), F32),
            pltpu.VMEM((ML_HPS, ML_CHUNK + 8, ML_HD), F32), pltpu.VMEM((ML_HPS, ML_CHUNK + 8, ML_HD), F32),
        ],
        compiler_params=_cparams(("parallel", "parallel", "arbitrary")),
        name="mlstm",
    )(gate_bias, proj, proj, proj, proj, if_arr, conv_w, conv_w, conv_b, conv_b, ng)


def _pack_bf16_pair(lo, hi):
    lo_b = pltpu.bitcast(lo.astype(BF16).astype(F32), U32)
    hi_b = pltpu.bitcast(hi.astype(BF16).astype(F32), U32)
    return (lo_b >> 16) | hi_b


def _unpack_bf16_pair(p):
    lo = pltpu.bitcast(p << 16, F32)
    hi = pltpu.bitcast(p & jnp.uint32(0xFFFF0000), F32)
    return lo, hi


def _row(ref, r):
    return ref.at[pl.ds(r, 1), :]


def _outproj_kernel(nsa_ref, ml_ref, w_ref, x_ref, g2_ref, wr_ref, br_ref, tril_ref,
                    x1_ref, xp_ref, rt_ref, cnt_ref, carry_ref, xa_ref, xb_ref):
    i = pl.program_id(0)

    @pl.when(i == 0)
    def _():
        carry_ref[...] = jnp.zeros_like(carry_ref)
        xb_ref[...] = jnp.zeros_like(xb_ref)

    half = D_MODEL // 2
    subs = [slice(sub * (TM_OUT // OUT_SUBS), (sub + 1) * (TM_OUT // OUT_SUBS)) for sub in range(OUT_SUBS)]

    def step(cur_ref, prev_ref):
        carry = carry_ref[...]
        carry_box = [carry]

        def chain():
            for rs in subs:
                yield from _outproj_rows(rs, prev_ref, carry_box, g2_ref, wr_ref, br_ref, tril_ref,
                                         x1_ref, xp_ref, rt_ref)
                yield

        pending = chain()
        for rs in subs:
            for c in range(D_MODEL // OUT_COLS):
                cs = slice(c * OUT_COLS, (c + 1) * OUT_COLS)
                cur_ref[rs, cs] = (x_ref[rs, cs]
                                   + jnp.dot(nsa_ref[rs, :], w_ref[0:half, cs], preferred_element_type=F32)
                                   + jnp.dot(ml_ref[rs, :], w_ref[half:, cs], preferred_element_type=F32))
                next(pending, None)
        for _ in pending:
            pass
        carry = jnp.where(i > 0, carry_box[0], carry)
        carry_ref[...] = carry
        cnt_ref[...] = jnp.broadcast_to(carry, cnt_ref.shape)

    @pl.when(i % 2 == 0)
    def _():
        step(xa_ref, xb_ref)

    @pl.when(i % 2 == 1)
    def _():
        step(xb_ref, xa_ref)


def _outproj_rows(rs, src_ref, carry_box, g2_ref, wr_ref, br_ref, tril_ref, x1_ref, xp_ref, rt_ref):
    half = D_MODEL // 2
    x1 = src_ref[rs, :]
    x1_ref[rs, :] = x1
    xn = x1 * lax.rsqrt(jnp.mean(x1 * x1, axis=-1, keepdims=True) + NORM_EPS) * g2_ref[...]
    xp_ref[rs, :] = _pack_bf16_pair(xn[:, :half], xn[:, half:])
    logits = jnp.dot(xn.astype(BF16), wr_ref[...], preferred_element_type=F32) + br_ref[...]
    yield

    tm = logits.shape[0]
    lane = lax.broadcasted_iota(jnp.int32, (tm, LANE), 1)
    lane_f = lane.astype(F32)
    big = float(LANE)
    gmask = lane < MOE_GROUPS
    gmax = jnp.max(jnp.where(gmask, logits, NEG_INF), axis=1, keepdims=True)
    ge = jnp.where(gmask, jnp.exp(logits - gmax), 0.0)
    gp = ge / jnp.sum(ge, axis=1, keepdims=True)
    g_w = jnp.max(gp, axis=1, keepdims=True)
    g_idx = jnp.min(jnp.where(gmask, jnp.where(gp == g_w, lane_f, big), big), axis=1, keepdims=True)
    yield
    grp_of_lane = ((lane - MOE_GROUPS) >> EPG_LOG2).astype(F32)
    emask = jnp.where(lane >= MOE_GROUPS, grp_of_lane, -1.0) == g_idx
    emax = jnp.max(jnp.where(emask, logits, NEG_INF), axis=1, keepdims=True)
    ee = jnp.where(emask, jnp.exp(logits - emax), 0.0)
    ep = jnp.where(emask, ee / jnp.sum(ee, axis=1, keepdims=True), -1.0)
    v1 = jnp.max(ep, axis=1, keepdims=True)
    i1 = jnp.min(jnp.where(ep == v1, lane_f, big), axis=1, keepdims=True)
    ep2 = jnp.where(lane_f == i1, -1.0, ep)
    v2 = jnp.max(ep2, axis=1, keepdims=True)
    i2 = jnp.min(jnp.where(ep2 == v2, lane_f, big), axis=1, keepdims=True)
    w0 = g_w * v1 / (v1 + v2)
    w1 = g_w * v2 / (v1 + v2)
    e0 = i1 - float(MOE_GROUPS)
    e1 = i2 - float(MOE_GROUPS)

    yield
    oh0 = jnp.where(lane_f == e0, 1.0, 0.0)
    oh1 = jnp.where(lane_f == e1, 1.0, 0.0)
    tril = tril_ref[0:tm, 0:tm]
    pre0 = jnp.dot(tril, oh0.astype(BF16), preferred_element_type=F32)
    pre1 = jnp.dot(tril, oh1.astype(BF16), preferred_element_type=F32)
    yield
    carry = carry_box[0]
    tot0 = pre0[tm - 1:tm, :]
    tot1 = pre1[tm - 1:tm, :]
    rank0 = jnp.sum(oh0 * (pre0 - 1.0 + carry), axis=1, keepdims=True)
    rank1 = jnp.sum(oh1 * (pre1 - 1.0 + carry + tot0), axis=1, keepdims=True)

    rt = jnp.where(lane == 0, e0, jnp.where(lane == 1, e1, jnp.where(lane == 2, w0, jnp.where(
        lane == 3, w1, jnp.where(lane == 4, rank0, jnp.where(lane == 5, rank1, 0.0))))))
    rt_ref[rs, :] = rt
    carry_box[0] = carry + tot0 + tot1


def _out_proj(nsa_o, ml_o, w_out, x2, g2, w_r, b_r, tril):
    T = x2.shape[0]
    half = D_MODEL // 2
    n_tiles = T // TM_OUT
    cur = lambda i: (jnp.minimum(i, n_tiles - 1), 0)
    prev = lambda i: (jnp.maximum(i - 1, 0), 0)
    return pl.pallas_call(
        _outproj_kernel,
        grid=(n_tiles + 1,),
        in_specs=[
            pl.BlockSpec((TM_OUT, half), cur),
            pl.BlockSpec((TM_OUT, half), cur),
            pl.BlockSpec((D_MODEL, D_MODEL), lambda i: (0, 0)),
            pl.BlockSpec((TM_OUT, D_MODEL), cur),
            pl.BlockSpec((1, D_MODEL), lambda i: (0, 0)),
            pl.BlockSpec((D_MODEL, LANE), lambda i: (0, 0)),
            pl.BlockSpec((1, LANE), lambda i: (0, 0)),
            pl.BlockSpec((TM_OUT, TM_OUT), lambda i: (0, 0)),
        ],
        out_specs=[
            pl.BlockSpec((TM_OUT, D_MODEL), prev),
            pl.BlockSpec((TM_OUT, half), prev),
            pl.BlockSpec((TM_OUT, LANE), prev),
            pl.BlockSpec((8, LANE), lambda i: (0, 0)),
        ],
        out_shape=[
            jax.ShapeDtypeStruct((T, D_MODEL), F32),
            jax.ShapeDtypeStruct((T, half), U32),
            jax.ShapeDtypeStruct((T, LANE), F32),
            jax.ShapeDtypeStruct((8, LANE), F32),
        ],
        scratch_shapes=[pltpu.VMEM((1, LANE), F32), pltpu.VMEM((TM_OUT, D_MODEL), F32),
                        pltpu.VMEM((TM_OUT, D_MODEL), F32)],
        compiler_params=_cparams(("arbitrary",)),
        name="out_proj",
    )(nsa_o, ml_o, w_out, x2, g2, w_r, b_r, tril)


TD = 256
DISPATCH_BUFS = 3
DISPATCH_UNROLL = 8


def _dispatch_kernel(dest_ref, zblk_ref, meta_ref, xp_hbm, xs_hbm, zbuf_ref, zsem_ref, tbuf_ref, lsem_ref,
                     rsem_ref):
    n_assign = dest_ref.shape[0]
    nb = xs_hbm.shape[0] // BM
    n_used = meta_ref[0]
    zbuf_ref[...] = jnp.zeros_like(zbuf_ref)

    def zero_block(blk):
        return pltpu.make_async_copy(zbuf_ref, xs_hbm.at[pl.ds(blk * BM, BM), :], zsem_ref.at[0])

    def for_zero_blocks(fn):
        for e in range(N_EXPERTS):
            @pl.when(zblk_ref[e] >= 0)
            def _(e=e):
                fn(zero_block(zblk_ref[e]))

        def body(blk, _):
            fn(zero_block(blk))
            return 0
        lax.fori_loop(n_used, nb, body, 0)

    for_zero_blocks(lambda c: c.start())
    for_zero_blocks(lambda c: c.wait())

    n_tiles = n_assign // (2 * TD)

    def load(tile, slot):
        return pltpu.make_async_copy(xp_hbm.at[pl.ds(tile * TD, TD), :], tbuf_ref.at[slot], lsem_ref.at[slot])

    def wait_scatter(slot):
        for _ in range(2):
            pltpu.make_async_copy(tbuf_ref.at[slot], tbuf_ref.at[slot], rsem_ref.at[slot]).wait()

    load(0, 0).start()

    def tile_step(i, _):
        slot = i % DISPATCH_BUFS
        nslot = (i + 1) % DISPATCH_BUFS

        @pl.when(i + 1 < n_tiles)
        def _():
            @pl.when(i + 1 >= DISPATCH_BUFS)
            def _():
                wait_scatter(nslot)
            load(i + 1, nslot).start()

        load(i, slot).wait()

        def body(j, _):
            for u in range(DISPATCH_UNROLL):
                r = j * DISPATCH_UNROLL + u
                for k in range(2):
                    d = dest_ref[(i * TD + r) * 2 + k]
                    pltpu.make_async_copy(_row(tbuf_ref.at[slot], r), _row(xs_hbm, d),
                                          rsem_ref.at[slot]).start(priority=k)
            return 0
        lax.fori_loop(0, TD // DISPATCH_UNROLL, body, 0)
        return 0

    lax.fori_loop(0, n_tiles, tile_step, 0)
    for back in range(min(DISPATCH_BUFS, n_tiles)):
        wait_scatter((n_tiles - 1 - back) % DISPATCH_BUFS)


def _dispatch(dest, zblk, meta, xp, n_rows):
    half = D_MODEL // 2
    assert dest.shape[0] % (2 * TD) == 0 and dest.shape[0] // (2 * TD) >= DISPATCH_BUFS
    grid_spec = pltpu.PrefetchScalarGridSpec(
        num_scalar_prefetch=3,
        grid=(1,),
        in_specs=[pl.BlockSpec(memory_space=pl.ANY)],
        out_specs=pl.BlockSpec(memory_space=pl.ANY),
        scratch_shapes=[
            pltpu.VMEM((BM, half), U32),
            pltpu.SemaphoreType.DMA((1,)),
            pltpu.VMEM((DISPATCH_BUFS, TD, half), U32),
            pltpu.SemaphoreType.DMA((DISPATCH_BUFS,)),
            pltpu.SemaphoreType.DMA((DISPATCH_BUFS,)),
        ],
    )
    return pl.pallas_call(
        _dispatch_kernel,
        grid_spec=grid_spec,
        out_shape=jax.ShapeDtypeStruct((n_rows, half), U32),
        compiler_params=_cparams(("arbitrary",)),
        name="dispatch",
    )(dest, zblk, meta, xp)


def _expert_kernel(be_ref, nxt_ref, par_ref, meta_ref, x_ref, wg_hbm, wu_hbm, wd_hbm, y_ref,
                   wsg_ref, wsu_ref, wsd_ref, wsem_ref, wgb_ref, wub_ref, wdb_ref):
    i = pl.program_id(0)
    n_used = meta_ref[0]
    half = D_MODEL // 2

    def weight_copies(e, p):
        return (pltpu.make_async_copy(wg_hbm.at[e], wsg_ref.at[p], wsem_ref.at[p, 0]),
                pltpu.make_async_copy(wu_hbm.at[e], wsu_ref.at[p], wsem_ref.at[p, 1]),
                pltpu.make_async_copy(wd_hbm.at[e], wsd_ref.at[p], wsem_ref.at[p, 2]))

    def succ(e):
        return jnp.where(e >= 0, nxt_ref[jnp.maximum(e, 0)], -1)

    def start_weights(e, p):
        @pl.when(e >= 0)
        def _():
            for c in weight_copies(e, p):
                c.start(priority=1)

    def wait_staged(p):
        for c in weight_copies(0, p):
            c.wait()

    def cast_staged(p):
        wgb_ref[p] = wsg_ref[p].astype(BF16)
        wub_ref[p] = wsu_ref[p].astype(BF16)
        wdb_ref[p] = wsd_ref[p].astype(BF16)

    @pl.when(i == 0)
    def _():
        e0 = be_ref[0]
        start_weights(e0, 0)
        start_weights(succ(e0), 1)
        wait_staged(0)
        cast_staged(0)
        start_weights(succ(succ(e0)), 0)

    def step(p, cast_next):
        if cast_next:
            wait_staged(1 - p)
        lo, hi = _unpack_bf16_pair(x_ref[...])
        xl = lo.astype(BF16)
        xh = hi.astype(BF16)
        gt = (jnp.dot(xl, wgb_ref[p, 0:half, :], preferred_element_type=F32)
              + jnp.dot(xh, wgb_ref[p, half:, :], preferred_element_type=F32))
        up = (jnp.dot(xl, wub_ref[p, 0:half, :], preferred_element_type=F32)
              + jnp.dot(xh, wub_ref[p, half:, :], preferred_element_type=F32))
        hb = (_silu(gt) * up).astype(BF16)
        y = jnp.dot(hb, wdb_ref[p], preferred_element_type=F32)
        y_ref[...] = _pack_bf16_pair(y[:, :half], y[:, half:])
        if cast_next:
            cast_staged(1 - p)
            start_weights(succ(succ(succ(be_ref[i]))), 1 - p)

    e = be_ref[jnp.minimum(i, n_used - 1)]
    ends_expert = (i + 1 < n_used) & (be_ref[jnp.minimum(i + 1, n_used - 1)] != e)
    for p in range(2):
        for cast_next in (False, True):
            @pl.when((i < n_used) & (par_ref[i] == p)
                     & (ends_expert if cast_next else jnp.logical_not(ends_expert)))
            def _(p=p, cast_next=cast_next):
                step(p, cast_next)

    @pl.when(i >= n_used)
    def _():
        y_ref[...] = jnp.zeros_like(y_ref)


def _experts(block_expert, next_expert, block_parity, meta, xs, w_gate, w_up, w_down):
    half = D_MODEL // 2
    n_rows = xs.shape[0]
    nb = n_rows // BM
    grid_spec = pltpu.PrefetchScalarGridSpec(
        num_scalar_prefetch=4,
        grid=(nb,),
        in_specs=[
            pl.BlockSpec((BM, half), lambda i, be, nxt, par, meta: (jnp.minimum(i, meta[0] - 1), 0)),
            pl.BlockSpec(memory_space=pl.ANY), pl.BlockSpec(memory_space=pl.ANY),
            pl.BlockSpec(memory_space=pl.ANY),
        ],
        out_specs=pl.BlockSpec((BM, half), lambda i, be, nxt, par, meta: (i, 0)),
        scratch_shapes=[
            pltpu.VMEM((2, D_MODEL, D_EXPERT), F32),
            pltpu.VMEM((2, D_MODEL, D_EXPERT), F32),
            pltpu.VMEM((2, D_EXPERT, D_MODEL), F32),
            pltpu.SemaphoreType.DMA((2, 3)),
            pltpu.VMEM((2, D_MODEL, D_EXPERT), BF16),
            pltpu.VMEM((2, D_MODEL, D_EXPERT), BF16),
            pltpu.VMEM((2, D_EXPERT, D_MODEL), BF16),
        ],
    )
    return pl.pallas_call(
        _expert_kernel,
        grid_spec=grid_spec,
        out_shape=jax.ShapeDtypeStruct((n_rows, half), U32),
        compiler_params=_cparams(("arbitrary",)),
        name="experts",
    )(block_expert, next_expert, block_parity, meta, xs, w_gate, w_up, w_down)


def _combine_kernel(dest_ref, y_hbm, x1_ref, rt_ref, fg_ref, o_ref, ya_ref, yb_ref, sem_ref):
    i = pl.program_id(0)
    nt = pl.num_programs(0)
    half = D_MODEL // 2
    bufs = (ya_ref, yb_ref)

    def row_copy(tile, slot, r_tile, s, k):
        r = r_tile * 8 + s
        d = dest_ref[(tile * TC + r) * 2 + k]
        return pltpu.make_async_copy(_row(y_hbm, d), _row(bufs[slot].at[k], r), sem_ref.at[slot])

    def wait_rows(slot):
        pltpu.make_async_copy(bufs[slot], bufs[slot], sem_ref.at[slot]).wait()

    def step(slot):
        wait_rows(slot)
        for r in range(TC):
            for k in range(2):
                row_copy(i + 1, 1 - slot, r // 8, r % 8, k).start(priority=k)
        rt = rt_ref[...]
        w0 = rt[:, 2:3]
        w1 = rt[:, 3:4]
        lo0, hi0 = _unpack_bf16_pair(bufs[slot][0])
        lo1, hi1 = _unpack_bf16_pair(bufs[slot][1])
        xl = x1_ref[:, :half] + (w0 * lo0 + w1 * lo1)
        xh = x1_ref[:, half:] + (w0 * hi0 + w1 * hi1)
        ms = (jnp.sum(xl * xl, axis=1, keepdims=True) + jnp.sum(xh * xh, axis=1, keepdims=True)) / float(D_MODEL)
        inv = lax.rsqrt(ms + NORM_EPS)
        o_ref[:, :half] = xl * inv * fg_ref[:, :half]
        o_ref[:, half:] = xh * inv * fg_ref[:, half:]

        @pl.when(i == nt - 1)
        def _():
            wait_rows(1 - slot)

    @pl.when(i == 0)
    def _():
        def body(rt, _):
            for s in range(8):
                for k in range(2):
                    row_copy(0, 0, rt, s, k).start(priority=k)
            return 0
        lax.fori_loop(0, TC // 8, body, 0)

    @pl.when(i % 2 == 0)
    def _():
        step(0)

    @pl.when(i % 2 == 1)
    def _():
        step(1)


def _combine(dest, ys, x1, route, fg):
    T = x1.shape[0]
    half = D_MODEL // 2
    grid_spec = pltpu.PrefetchScalarGridSpec(
        num_scalar_prefetch=1,
        grid=(T // TC,),
        in_specs=[
            pl.BlockSpec(memory_space=pl.ANY),
            pl.BlockSpec((TC, D_MODEL), lambda i, d: (i, 0)),
            pl.BlockSpec((TC, LANE), lambda i, d: (i, 0)),
            pl.BlockSpec((1, D_MODEL), lambda i, d: (0, 0)),
        ],
        out_specs=pl.BlockSpec((TC, D_MODEL), lambda i, d: (i, 0)),
        scratch_shapes=[
            pltpu.VMEM((2, TC, half), U32),
            pltpu.VMEM((2, TC, half), U32),
            pltpu.SemaphoreType.DMA((2,)),
        ],
    )
    return pl.pallas_call(
        _combine_kernel,
        grid_spec=grid_spec,
        out_shape=jax.ShapeDtypeStruct((T, D_MODEL), F32),
        compiler_params=_cparams(("arbitrary",)),
        name="combine",
    )(dest, ys, x1, route, fg)


def _cover_matrix(S):
    n_blk = S // CMP_STRIDE
    n_sel = S // SEL_BLOCK
    cs = np.arange(n_blk) * CMP_STRIDE
    ss = np.arange(n_sel) * SEL_BLOCK
    shared = np.minimum(cs[:, None] + CMP_LEN, ss[None, :] + SEL_BLOCK) - np.maximum(cs[:, None], ss[None, :])
    return (np.clip(shared, 0, None) / CMP_LEN).T.astype(np.float32)


def _block_mask_matrix(S):
    n_sel = S // SEL_BLOCK
    assert n_sel <= LANE
    e = np.zeros((S, LANE), np.float32)
    e[np.arange(S), np.arange(S) // SEL_BLOCK] = NEG_INF
    return e


def _inv_freq_row():
    inv = np.power(np.float32(ROPE_THETA), -np.arange(ROPE_HALF, dtype=np.float32) * 2.0 / ROPE_DIM)
    return np.tile(inv, ROPE_SLOTS).reshape(1, LANE).astype(np.float32)


def _packed_positions(positions, T):
    rows = TS_ROPE // ROPE_SLOTS
    p = positions.reshape(T // TS_ROPE, ROPE_SLOTS, rows).transpose(0, 2, 1).astype(F32)
    return jnp.repeat(p, ROPE_HALF, axis=2).reshape(T // ROPE_SLOTS, LANE)


def _layer(x2, positions, B, S, norm1_g, w_in, cmp_pe_k, cmp_pe_v, cmp_wk1, cmp_wk2, cmp_wv1, cmp_wv2,
           nsa_norm_g, conv_w, conv_b, b_igate, b_fgate, mlstm_norm_g, w_out, norm2_g,
           w_group, b_group, w_router, b_router, w_exp_gate, w_exp_up, w_exp_down, out_norm_g):
    T = B * S
    w_main, w_small = _weight_prep(jnp.transpose(w_in[0]))
    posb = _packed_positions(positions, T)
    invf = jnp.asarray(_inv_freq_row())

    proj, small = _in_proj(x2, norm1_g.reshape(1, -1), w_main, w_small)
    q_r, k_r = _rope(posb, invf, proj)
    kc, vc = _compress(k_r, proj, B, S, cmp_pe_k, cmp_pe_v, cmp_wk1.astype(BF16), cmp_wk2.astype(BF16),
                       cmp_wv1.astype(BF16), cmp_wv2.astype(BF16))
    eye = np.eye(S // SEL_BLOCK, LANE, dtype=np.float32)
    nsa_o = _nsa(q_r, k_r, proj, small, kc, vc, jnp.asarray(_cover_matrix(S), BF16),
                 jnp.asarray(_block_mask_matrix(S), BF16), jnp.asarray(eye, BF16),
                 nsa_norm_g.reshape(1, -1), B, S)

    if_arr = small[:, 24:32].reshape(B, S, 2, ML_HEADS).transpose(0, 3, 2, 1)
    gate_bias = jnp.stack([b_igate, b_fgate]).astype(F32)
    ml_o = _mlstm(proj, if_arr, gate_bias, conv_w, conv_b.reshape(1, -1), mlstm_norm_g.reshape(1, -1), B, S)

    w_r = jnp.concatenate([w_group, w_router, jnp.zeros((D_MODEL, LANE - MOE_GROUPS - N_EXPERTS), F32)],
                          axis=1).astype(BF16)
    b_r = jnp.concatenate([b_group, b_router, jnp.zeros((LANE - MOE_GROUPS - N_EXPERTS,), F32)]).reshape(1, LANE)
    tril = jnp.asarray(np.tril(np.ones((TM_OUT, TM_OUT), np.float32)), BF16)
    x1, xp, route, cnt = _out_proj(nsa_o, ml_o, w_out.astype(BF16), x2, norm2_g.reshape(1, -1), w_r, b_r, tril)

    n_rows = T * 2 + N_EXPERTS * BM
    counts = cnt[0, :N_EXPERTS].astype(jnp.int32)
    pcounts = (counts + BM - 1) // BM * BM
    pends = jnp.cumsum(pcounts)
    pstarts = pends - pcounts
    eid = route[:, 0:2].astype(jnp.int32)
    onehot = (eid[..., None] == jnp.arange(N_EXPERTS, dtype=jnp.int32)).astype(F32)
    start_blk = jnp.einsum('tke,e->tk', onehot, (pstarts // BM).astype(F32)).astype(jnp.int32)
    dest = (start_blk * BM + route[:, 4:6].astype(jnp.int32)).reshape(T * 2)
    block_expert = jnp.minimum(jnp.searchsorted(pends, jnp.arange(n_rows // BM) * BM, side='right'),
                               N_EXPERTS - 1).astype(jnp.int32)
    present = jnp.where(counts > 0, jnp.arange(N_EXPERTS, dtype=jnp.int32), N_EXPERTS)
    later = jnp.concatenate([lax.cummin(present[::-1])[::-1][1:], jnp.full((1,), N_EXPERTS, jnp.int32)])
    next_expert = jnp.where(later < N_EXPERTS, later, -1).astype(jnp.int32)
    meta = (pends[-1:] // BM).astype(jnp.int32)

    last_block = jnp.where(counts > 0, pends // BM - 1, -1).astype(jnp.int32)

    xs = _dispatch(dest, last_block, meta, xp, n_rows)
    ordinal = jnp.cumsum((counts > 0).astype(jnp.int32)) - 1
    block_parity = (ordinal[block_expert] % 2).astype(jnp.int32)
    ys = _experts(block_expert, next_expert, block_parity, meta, xs, w_exp_gate, w_exp_up, w_exp_down)
    dest_pad = jnp.concatenate([dest, jnp.zeros((2 * TC,), jnp.int32)])
    return _combine(dest_pad, ys, x1, route, out_norm_g.reshape(1, -1))


def kernel(x, positions, norm1_g, w_in, cmp_pe_k, cmp_pe_v, cmp_wk1, cmp_wk2, cmp_wv1, cmp_wv2, nsa_norm_g,
           conv_w, conv_b, b_igate, b_fgate, mlstm_norm_g, w_out, norm2_g, w_group, b_group, w_router,
           b_router, w_exp_gate, w_exp_up, w_exp_down, final_norm_g):
    B, S, D = x.shape
    assert D == D_MODEL and norm1_g.shape[0] == 1, "single-layer, D_MODEL-wide configuration only"
    assert S % ML_CHUNK == 0 and S % SEL_BUCKET == 0 and S >= WIN_KEYS and (B * S) % TM_IN == 0
    out = _layer(x.reshape(B * S, D), positions, B, S, norm1_g[0], w_in, cmp_pe_k[0], cmp_pe_v[0],
                 cmp_wk1[0], cmp_wk2[0], cmp_wv1[0], cmp_wv2[0], nsa_norm_g[0], conv_w[0], conv_b[0],
                 b_igate[0], b_fgate[0], mlstm_norm_g[0], w_out[0], norm2_g[0], w_group[0], b_group[0],
                 w_router[0], b_router[0], w_exp_gate[0], w_exp_up[0], w_exp_down[0], final_norm_g)
    return out.reshape(B, S, D)
```

```python
import functools

import numpy as np
import jax
import jax.numpy as jnp
from jax import lax
from jax.experimental import pallas as pl
from jax.experimental.pallas import tpu as pltpu

F32 = jnp.float32
BF16 = jnp.bfloat16
U32 = jnp.uint32

D_MODEL = 2048
NSA_HEADS = 8
NSA_GROUPS = 2
NSA_J = NSA_HEADS // NSA_GROUPS
HD = 128
CMP_LEN = 32
CMP_STRIDE = 16
SEL_BLOCK = 64
SEL_TOPK = 8
WINDOW = 512
ROPE_THETA = 500000.0
ROPE_DIM = 32
ROPE_HALF = 16
ML_HEADS = 4
ML_HD = 256
CONV_W = 4
MOE_GROUPS = 8
EPG = 8
N_EXPERTS = 64
D_EXPERT = 512
SEL_BLOCK_LOG2 = 6
EPG_LOG2 = 3
assert (1 << SEL_BLOCK_LOG2) == SEL_BLOCK and (1 << EPG_LOG2) == EPG
NORM_EPS = 1e-6
NEG_INF = -1e30
FORCE_SCORE = 1000.0

COL_Q = 0
COL_KV = 1024
COL_QKM = 2560
COL_VM = 4608
COL_OM = 5632
N_MAIN = 6656

LANE = 128
VMEM_LIMIT = 56 * 1024 * 1024

TM_IN = 1024
TN_IN = 1664
TS_ROPE = 256
TQ = 256
SEL_BUCKET = 256
WIN_KEYS = WINDOW + TQ
ML_CHUNK = 256
TM_OUT = 512
OUT_SUBS = 2
OUT_COLS = 512
BM = 256
TC = 256

NT_DIMS = (((1,), (1,)), ((), ()))
TN_DIMS = (((0,), (0,)), ((), ()))


def _cparams(sem):
    return pltpu.CompilerParams(dimension_semantics=sem, vmem_limit_bytes=VMEM_LIMIT)


def _sigmoid(x):
    return 0.5 * jnp.tanh(0.5 * x) + 0.5


def _interleave(*chains):
    live = list(chains)
    while live:
        for g in list(live):
            try:
                next(g)
            except StopIteration:
                live.remove(g)


def _silu(x):
    h = 0.5 * x
    return h + h * jnp.tanh(h)


W_GATES = 2560
W_QKM = 2584
W_IGATE = 6680
W_END = 6688
TR_PREP = 512
assert W_GATES % TR_PREP == 0 and N_MAIN % TR_PREP == 0 and W_IGATE - W_QKM == N_MAIN - W_GATES


def _wprep_kernel(wt_hbm, wm_ref, ws_ref, wbuf_ref, sbuf_ref, sem_ref, ssem_ref):
    i = pl.program_id(0)
    n_gate = W_QKM - W_GATES
    n_if = W_END - W_IGATE

    def load(blk, slot):
        start = pl.multiple_of(blk * TR_PREP + jnp.where(blk * TR_PREP >= W_GATES, n_gate, 0), 8)
        return pltpu.make_async_copy(wt_hbm.at[pl.ds(start, TR_PREP), :], wbuf_ref.at[slot], sem_ref.at[slot])

    def small_loads():
        return (pltpu.make_async_copy(wt_hbm.at[pl.ds(W_GATES, n_gate), :], sbuf_ref.at[pl.ds(0, n_gate), :],
                                      ssem_ref.at[0]),
                pltpu.make_async_copy(wt_hbm.at[pl.ds(W_IGATE, n_if), :], sbuf_ref.at[pl.ds(n_gate, n_if), :],
                                      ssem_ref.at[1]))

    @pl.when(i == 0)
    def _():
        load(0, 0).start()
        for c in small_loads():
            c.start()
        for c in small_loads():
            c.wait()
        ws_ref[...] = jnp.zeros_like(ws_ref)
        ws_ref[0:n_gate + n_if, :] = sbuf_ref[...].astype(BF16)

    @pl.when(i + 1 < pl.num_programs(0))
    def _():
        load(i + 1, (i + 1) % 2).start()

    load(i, i % 2).wait()
    wm_ref[...] = wbuf_ref[i % 2].astype(BF16)


def _weight_prep(wt):
    assert wt.shape == (W_END, D_MODEL)
    n_small = (W_QKM - W_GATES) + (W_END - W_IGATE)
    return pl.pallas_call(
        _wprep_kernel,
        grid=(N_MAIN // TR_PREP,),
        in_specs=[pl.BlockSpec(memory_space=pl.ANY)],
        out_specs=[pl.BlockSpec((TR_PREP, D_MODEL), lambda i: (i, 0)),
                   pl.BlockSpec((LANE, D_MODEL), lambda i: (0, 0))],
        out_shape=[jax.ShapeDtypeStruct((N_MAIN, D_MODEL), BF16),
                   jax.ShapeDtypeStruct((LANE, D_MODEL), BF16)],
        scratch_shapes=[pltpu.VMEM((2, TR_PREP, D_MODEL), F32), pltpu.VMEM((n_small, D_MODEL), F32),
                        pltpu.SemaphoreType.DMA((2,)), pltpu.SemaphoreType.DMA((2,))],
        compiler_params=_cparams(("arbitrary",)),
        name="weight_prep",
    )(wt)


def _inproj_kernel(x_ref, g_ref, w_ref, ws_ref, o_ref, os_ref, h_ref):
    @pl.when(pl.program_id(1) == 0)
    def _():
        x = x_ref[...]
        ms = jnp.mean(x * x, axis=-1, keepdims=True)
        h_ref[...] = (x * lax.rsqrt(ms + NORM_EPS) * g_ref[...]).astype(BF16)
        os_ref[...] = lax.dot_general(h_ref[...], ws_ref[...], NT_DIMS, preferred_element_type=F32)

    o_ref[...] = lax.dot_general(h_ref[...], w_ref[...], NT_DIMS,
                                 preferred_element_type=F32).astype(o_ref.dtype)


def _in_proj(x2, g1, w_main, w_small):
    T = x2.shape[0]
    return pl.pallas_call(
        _inproj_kernel,
        grid=(T // TM_IN, N_MAIN // TN_IN),
        in_specs=[
            pl.BlockSpec((TM_IN, D_MODEL), lambda m, n: (m, 0)),
            pl.BlockSpec((1, D_MODEL), lambda m, n: (0, 0)),
            pl.BlockSpec((TN_IN, D_MODEL), lambda m, n: (n, 0)),
            pl.BlockSpec((LANE, D_MODEL), lambda m, n: (0, 0)),
        ],
        out_specs=[
            pl.BlockSpec((TM_IN, TN_IN), lambda m, n: (m, n)),
            pl.BlockSpec((TM_IN, LANE), lambda m, n: (m, 0)),
        ],
        out_shape=[
            jax.ShapeDtypeStruct((T, N_MAIN), BF16),
            jax.ShapeDtypeStruct((T, LANE), F32),
        ],
        scratch_shapes=[pltpu.VMEM((TM_IN, D_MODEL), BF16)],
        compiler_params=_cparams(("parallel", "arbitrary")),
        name="in_proj",
    )(x2, g1, w_main, w_small)


ROPE_SLOTS = LANE // ROPE_HALF


def _rope_kernel(pos_ref, invf_ref, q_ref, kc_ref, ks_ref, kw_ref, qo_ref, ko_ref):
    ang = pos_ref[...] * invf_ref[...]
    c = jnp.cos(ang)
    s = jnp.sin(ang)
    rows = ang.shape[0]
    lane = lax.broadcasted_iota(jnp.int32, ang.shape, 1)
    lo = lane < ROPE_HALF
    mid = lane < ROPE_DIM
    scale = HD ** -0.5

    def lanes_from(x, src):
        shift = (-src) % LANE
        return x if shift == 0 else pltpu.roll(x, shift, 1)

    for slot in range(ROPE_SLOTS):
        src = slot * ROPE_HALF
        cf = jnp.where(lo, lanes_from(c, src), jnp.where(mid, lanes_from(c, src - ROPE_HALF), 1.0))
        sa = jnp.where(lo, -lanes_from(s, src), 0.0)
        sb = jnp.where(lo, 0.0, jnp.where(mid, lanes_from(s, src - ROPE_HALF), 0.0))
        rs = slice(slot * rows, (slot + 1) * rows)

        def rope(x):
            return x * cf + pltpu.roll(x, LANE - ROPE_HALF, 1) * sa + pltpu.roll(x, ROPE_HALF, 1) * sb

        for h in range(NSA_HEADS):
            sl = slice(h * HD, (h + 1) * HD)
            qo_ref[rs, sl] = (rope(q_ref[rs, sl].astype(F32)) * scale).astype(BF16)
        for i, r in enumerate((kc_ref, ks_ref, kw_ref)):
            for g in range(NSA_GROUPS):
                sl = slice(g * HD, (g + 1) * HD)
                so = slice(i * 2 * HD + g * HD, i * 2 * HD + (g + 1) * HD)
                ko_ref[rs, so] = rope(r[rs, sl].astype(F32)).astype(BF16)


def _rope(posb, invf, proj):
    T = proj.shape[0]
    kvb = COL_KV // 256
    return pl.pallas_call(
        _rope_kernel,
        grid=(T // TS_ROPE,),
        in_specs=[
            pl.BlockSpec((TS_ROPE // ROPE_SLOTS, LANE), lambda i: (i, 0)),
            pl.BlockSpec((1, LANE), lambda i: (0, 0)),
            pl.BlockSpec((TS_ROPE, 1024), lambda i: (i, 0)),
            pl.BlockSpec((TS_ROPE, 256), lambda i: (i, kvb + 0)),
            pl.BlockSpec((TS_ROPE, 256), lambda i: (i, kvb + 2)),
            pl.BlockSpec((TS_ROPE, 256), lambda i: (i, kvb + 4)),
        ],
        out_specs=[
            pl.BlockSpec((TS_ROPE, 1024), lambda i: (i, 0)),
            pl.BlockSpec((TS_ROPE, 768), lambda i: (i, 0)),
        ],
        out_shape=[
            jax.ShapeDtypeStruct((T, 1024), BF16),
            jax.ShapeDtypeStruct((T, 768), BF16),
        ],
        compiler_params=_cparams(("parallel",)),
        name="rope",
    )(posb, invf, proj, proj, proj, proj)


def _compress_kernel(k_ref, v_ref, pek_ref, pev_ref, w1k_ref, w2k_ref, w1v_ref, w2v_ref,
                     kc_ref, vc_ref, xs_ref):
    S = k_ref.shape[0]
    n_blk = S // CMP_STRIDE
    for src, pe, w1, w2, dst in ((k_ref, pek_ref, w1k_ref, w2k_ref, kc_ref),
                                 (v_ref, pev_ref, w1v_ref, w2v_ref, vc_ref)):
        xs_ref[0:S, :] = src[...].astype(F32)
        xs_ref[S:S + CMP_LEN, :] = jnp.zeros((CMP_LEN, HD), F32)
        acc = jnp.zeros((n_blk, HD), F32)
        for l in range(CMP_LEN):
            a = xs_ref[pl.ds(l, n_blk, stride=CMP_STRIDE), :] + pe[l:l + 1, :]
            acc = acc + jnp.dot(a.astype(BF16), w1[l * HD:(l + 1) * HD, :],
                                preferred_element_type=F32)
        hid = _silu(acc)
        out = jnp.dot(hid.astype(BF16), w2[...], preferred_element_type=F32)
        dst[0, 0] = out.astype(BF16)


def _compress(k_r, proj, B, S, pek, pev, w1k, w2k, w1v, w2v):
    n_blk = S // CMP_STRIDE
    vcol = (COL_KV + 256) // HD
    full = lambda shape: pl.BlockSpec(shape, lambda b, g: tuple(0 for _ in shape))
    return pl.pallas_call(
        _compress_kernel,
        grid=(B, NSA_GROUPS),
        in_specs=[
            pl.BlockSpec((S, HD), lambda b, g: (b, g)),
            pl.BlockSpec((S, HD), lambda b, g: (b, vcol + g)),
            full((CMP_LEN, HD)), full((CMP_LEN, HD)),
            full((CMP_LEN * HD, HD)), full((HD, HD)),
            full((CMP_LEN * HD, HD)), full((HD, HD)),
        ],
        out_specs=[
            pl.BlockSpec((1, 1, n_blk, HD), lambda b, g: (b, g, 0, 0)),
            pl.BlockSpec((1, 1, n_blk, HD), lambda b, g: (b, g, 0, 0)),
        ],
        out_shape=[
            jax.ShapeDtypeStruct((B, NSA_GROUPS, n_blk, HD), BF16),
            jax.ShapeDtypeStruct((B, NSA_GROUPS, n_blk, HD), BF16),
        ],
        scratch_shapes=[pltpu.VMEM((S + CMP_LEN, HD), F32)],
        compiler_params=_cparams(("parallel", "parallel")),
        name="compress",
    )(k_r, proj, pek, pev, w1k, w2k, w1v, w2v)


def _nsa_kernel(q_ref, kc_ref, vc_ref, ks_ref, vs_ref, kw_ref, vw_ref, sm_ref, covt_ref, eneg_ref, eye_ref,
                ng_ref, o_ref, obuf_ref, owin_ref, osel_ref):
    qi = pl.program_id(1)
    q0 = qi * TQ
    R = NSA_J * TQ
    n_cmp = kc_ref.shape[2]
    n_sel = ks_ref.shape[0] // SEL_BLOCK
    n_win = WIN_KEYS // TQ

    def row_t(shape):
        r = lax.broadcasted_iota(jnp.int32, shape, 0)
        return q0 + (r & (TQ - 1))

    qgs = [jnp.concatenate([q_ref[:, (g * NSA_J + j) * HD:(g * NSA_J + j + 1) * HD]
                            for j in range(NSA_J)], axis=0) for g in range(NSA_GROUPS)]

    def with_ones(v):
        lane = lax.broadcasted_iota(jnp.int32, v.shape, 1)
        return jnp.concatenate([v, jnp.where(lane == 0, 1.0, 0.0).astype(BF16)], axis=1)

    def window_chain(g):
        w0 = pl.multiple_of(jnp.maximum(q0 - WINDOW, 0), TQ)
        kt = kw_ref[pl.ds(w0, WIN_KEYS), g * HD:(g + 1) * HD]
        vt = with_ones(vw_ref[pl.ds(w0, WIN_KEYS), g * HD:(g + 1) * HD])
        sc = lax.dot_general(qgs[g], kt, NT_DIMS, preferred_element_type=F32)
        yield
        diff = row_t((R, WIN_KEYS)) - (w0 + lax.broadcasted_iota(jnp.int32, (R, WIN_KEYS), 1))
        keep = jnp.where(diff >= 0, diff, WINDOW) < WINDOW
        sc = jnp.where(keep, sc, NEG_INF)
        mw = jnp.max(sc, axis=1, keepdims=True)
        yield
        pw = jnp.exp((sc - mw).astype(BF16))
        acc = jnp.dot(pw, vt, preferred_element_type=F32)
        yield
        owin_ref[g] = acc[:, 0:HD] / acc[:, HD:HD + 1]

    gates = _sigmoid(sm_ref[...])
    o_cmps = [None] * NSA_GROUPS
    qps = [None] * NSA_GROUPS

    def cmp_chain(g):
        qg = qgs[g]

        s = lax.dot_general(qg, kc_ref[0, g], NT_DIMS, preferred_element_type=F32)
        yield
        n_lane = lax.broadcasted_iota(jnp.int32, (R, n_cmp), 1)
        cmask = (n_lane * CMP_STRIDE + (CMP_LEN - 1)) <= row_t((R, n_cmp))
        s = jnp.where(cmask, s, NEG_INF)
        m = jnp.max(s, axis=1, keepdims=True)
        e = jnp.where(cmask, jnp.exp(s - m), 0.0)
        l = jnp.sum(e, axis=1, keepdims=True)
        p = (e / jnp.where(l > 0.0, l, 1.0)).astype(BF16)
        yield
        o_cmps[g] = jnp.dot(p, vc_ref[0, g], preferred_element_type=F32)
        impr = lax.dot_general(covt_ref[...], p, NT_DIMS, preferred_element_type=F32)
        imp = impr[:, 0:TQ]
        for j in range(1, NSA_J):
            imp = imp + impr[:, j * TQ:(j + 1) * TQ]
        yield

        m_sub = lax.broadcasted_iota(jnp.int32, (n_sel, TQ), 0)
        jt = (q0 + lax.broadcasted_iota(jnp.int32, (n_sel, TQ), 1)) >> SEL_BLOCK_LOG2
        forced = jnp.where(m_sub == 0, FORCE_SCORE,
                           jnp.where(m_sub == jt, FORCE_SCORE,
                                     jnp.where(m_sub == jt - 1, FORCE_SCORE, 0.0)))
        score = jnp.where(m_sub <= jt, imp + forced, -1.0)
        rank = jnp.zeros((n_sel, TQ), F32)
        for mp in range(n_sel):
            row = score[mp:mp + 1, :]
            ge = jnp.where(row >= score, 1.0, 0.0)
            gt = jnp.where(row > score, 1.0, 0.0)
            rank = rank + jnp.where(m_sub > mp, ge, gt)
        notsel_t = jnp.where(rank < float(min(SEL_TOPK, n_sel)), 0.0, 1.0).astype(BF16)
        notsel = lax.dot_general(notsel_t, eye_ref[...], TN_DIMS,
                                 preferred_element_type=F32).astype(BF16)
        qps[g] = jnp.concatenate([qg, jnp.concatenate([notsel] * NSA_J, axis=0)], axis=1)

    _interleave(*[c for g in range(NSA_GROUPS) for c in (window_chain(g), cmp_chain(g))])

    def sel_chain(g, nk):
        kt = jnp.concatenate([ks_ref[0:nk, g * HD:(g + 1) * HD], eneg_ref[0:nk, :]], axis=1)
        vt = with_ones(vs_ref[0:nk, g * HD:(g + 1) * HD])
        sc = lax.dot_general(qps[g], kt, NT_DIMS, preferred_element_type=F32)
        yield
        klane = lax.broadcasted_iota(jnp.int32, (R, SEL_BUCKET), 1) + (nk - SEL_BUCKET)
        tail = jnp.where(klane <= row_t((R, SEL_BUCKET)), sc[:, nk - SEL_BUCKET:], NEG_INF)
        sc = tail if nk == SEL_BUCKET else jnp.concatenate([sc[:, 0:nk - SEL_BUCKET], tail], axis=1)
        ms = jnp.max(sc, axis=1, keepdims=True)
        yield
        acc = jnp.dot(jnp.exp((sc - ms).astype(BF16)), vt, preferred_element_type=F32)
        yield
        osel_ref[g] = acc[:, 0:HD] / acc[:, HD:HD + 1]

    for b in range(ks_ref.shape[0] // SEL_BUCKET):
        @pl.when(q0 // SEL_BUCKET == b)
        def _(b=b):
            _interleave(*[sel_chain(g, (b + 1) * SEL_BUCKET) for g in range(NSA_GROUPS)])

    ssq = jnp.zeros((TQ, 1), F32)
    for g in range(NSA_GROUPS):
        o_sel = osel_ref[g]
        o_cmp = o_cmps[g]
        o_win = owin_ref[g]
        for j in range(NSA_J):
            h = g * NSA_J + j
            rs = slice(j * TQ, (j + 1) * TQ)
            o = (gates[:, 3 * h:3 * h + 1] * o_cmp[rs] + gates[:, 3 * h + 1:3 * h + 2] * o_sel[rs]
                 + gates[:, 3 * h + 2:3 * h + 3] * o_win[rs])
            ssq = ssq + jnp.sum(o * o, axis=1, keepdims=True)
            obuf_ref[:, h * HD:(h + 1) * HD] = o

    inv = lax.rsqrt(ssq / float(NSA_HEADS * HD) + NORM_EPS)
    o_ref[...] = (obuf_ref[...] * inv * ng_ref[...]).astype(BF16)


def _nsa(q_r, k_r, proj, small, kc, vc, covt, eneg, eye, ng, B, S):
    T = B * S
    nq = S // TQ
    n_blk = S // CMP_STRIDE
    n_sel = S // SEL_BLOCK
    kvb = COL_KV // 256
    return pl.pallas_call(
        _nsa_kernel,
        grid=(B, nq),
        in_specs=[
            pl.BlockSpec((TQ, 1024), lambda b, i: (b * nq + i, 0)),
            pl.BlockSpec((1, NSA_GROUPS, n_blk, HD), lambda b, i: (b, 0, 0, 0)),
            pl.BlockSpec((1, NSA_GROUPS, n_blk, HD), lambda b, i: (b, 0, 0, 0)),
            pl.BlockSpec((S, 256), lambda b, i: (b, 1)),
            pl.BlockSpec((S, 256), lambda b, i: (b, kvb + 3)),
            pl.BlockSpec((S, 256), lambda b, i: (b, 2)),
            pl.BlockSpec((S, 256), lambda b, i: (b, kvb + 5)),
            pl.BlockSpec((TQ, LANE), lambda b, i: (b * nq + i, 0)),
            pl.BlockSpec((n_sel, n_blk), lambda b, i: (0, 0)),
            pl.BlockSpec((S, LANE), lambda b, i: (0, 0)),
            pl.BlockSpec((n_sel, LANE), lambda b, i: (0, 0)),
            pl.BlockSpec((1, 1024), lambda b, i: (0, 0)),
        ],
        out_specs=pl.BlockSpec((TQ, 1024), lambda b, i: (b * nq + i, 0)),
        out_shape=jax.ShapeDtypeStruct((T, 1024), BF16),
        scratch_shapes=[pltpu.VMEM((TQ, 1024), F32), pltpu.VMEM((NSA_GROUPS, NSA_J * TQ, HD), F32),
                        pltpu.VMEM((NSA_GROUPS, NSA_J * TQ, HD), F32)],
        compiler_params=_cparams(("parallel", "parallel")),
        name="nsa",
    )(q_r, kc, vc, k_r, proj, k_r, proj, small, covt, eneg, eye, ng)


ML_HPS = 2


def _mlstm_kernel(bias_ref, q_ref, k_ref, v_ref, om_ref, if_ref, cwq_ref, cwk_ref, cbq_ref, cbk_ref,
                  ng_ref, o_ref, c_ref, n_ref, m_ref, xq_ref, xk_ref):
    hp = pl.program_id(1)
    L = ML_CHUNK

    @pl.when(pl.program_id(2) == 0)
    def _():
        c_ref[...] = jnp.zeros_like(c_ref)
        n_ref[...] = jnp.zeros_like(n_ref)
        m_ref[...] = jnp.zeros_like(m_ref)
        xq_ref[:, 0:8, :] = jnp.zeros((ML_HPS, 8, ML_HD), F32)
        xk_ref[:, 0:8, :] = jnp.zeros((ML_HPS, 8, ML_HD), F32)

    def conv_silu(x_ref, hh, xb_ref, w_ref, b_ref):
        cs = slice(hh * ML_HD, (hh + 1) * ML_HD)
        xb_ref[hh, 8:8 + L, :] = x_ref[:, cs].astype(F32)
        y = b_ref[:, cs] + xb_ref[hh, 8:8 + L, :] * w_ref[CONV_W - 1:CONV_W, cs]
        for k in range(1, CONV_W):
            y = y + xb_ref[hh, 8 - k:8 - k + L, :] * w_ref[CONV_W - 1 - k:CONV_W - k, cs]
        xb_ref[hh, 0:8, :] = xb_ref[hh, L:L + 8, :]
        return _silu(y)

    r = lax.broadcasted_iota(jnp.int32, (L, L), 0)
    cidx = lax.broadcasted_iota(jnp.int32, (L, L), 1)
    tril = cidx <= r
    eye = cidx == r

    def head_chain(hh):
        h = hp * ML_HPS + hh
        cs = slice(hh * ML_HD, (hh + 1) * ML_HD)
        qf = conv_silu(q_ref, hh, xq_ref, cwq_ref, cbq_ref) * (ML_HD ** -0.5)
        yield
        kf = conv_silu(k_ref, hh, xk_ref, cwk_ref, cbk_ref)
        yield
        qb = qf.astype(BF16)
        kb = kf.astype(BF16)
        vb = v_ref[:, cs]
        vf = vb.astype(F32)

        ic = if_ref[0, hh, 0:1, :] + bias_ref[0, h]
        fp = if_ref[0, hh, 1:2, :] + bias_ref[1, h]
        fc = jnp.minimum(fp, 0.0) - jnp.log(1.0 + jnp.exp(-jnp.abs(fp)))

        fc_b = jnp.broadcast_to(fc, (L, L))
        ic_b = jnp.broadcast_to(ic, (L, L))
        b_col = jnp.sum(jnp.where(tril, fc_b, 0.0), axis=1, keepdims=True)
        fc_col = jnp.sum(jnp.where(eye, fc_b, 0.0), axis=1, keepdims=True)
        ic_col = jnp.sum(jnp.where(eye, ic_b, 0.0), axis=1, keepdims=True)
        b_row = jnp.sum(jnp.where(r <= cidx, jnp.broadcast_to(fc_col, (L, L)), 0.0),
                        axis=0, keepdims=True)
        b_last = b_col[L - 1:L, :]
        m_prev = m_ref[hh]
        yield

        d_log = jnp.where(tril, b_col - b_row + ic, NEG_INF)
        inter = b_col + m_prev
        m_t = jnp.maximum(inter, jnp.max(d_log, axis=1, keepdims=True))
        w_intra = jnp.exp(d_log - m_t)
        w_inter = jnp.exp(inter - m_t)
        qk = lax.dot_general(qb, kb, NT_DIMS, preferred_element_type=F32) * w_intra
        yield
        num = (jnp.dot(qk.astype(BF16), vb, preferred_element_type=F32)
               + w_inter * lax.dot_general(qb, c_ref[hh].astype(BF16), NT_DIMS, preferred_element_type=F32))
        den = jnp.sum(qk, axis=1, keepdims=True) + w_inter * jnp.sum(qf * n_ref[hh], axis=1, keepdims=True)
        hm = num / jnp.maximum(jnp.abs(den), jnp.exp(-m_t))
        yield

        w_log = b_last - b_col + ic_col
        m_new = jnp.maximum(b_last + m_prev, jnp.max(w_log, axis=0, keepdims=True))
        w_state = jnp.exp(w_log - m_new)
        decay = jnp.exp(b_last + m_prev - m_new)
        c_ref[hh] = decay * c_ref[hh] + lax.dot_general((w_state * vf).astype(BF16), kb, TN_DIMS,
                                                        preferred_element_type=F32)
        n_ref[hh] = decay * n_ref[hh] + jnp.sum(w_state * kf, axis=0, keepdims=True)
        m_ref[hh] = m_new
        yield

        hn = hm * lax.rsqrt(jnp.mean(hm * hm, axis=1, keepdims=True) + NORM_EPS) * ng_ref[:, cs]
        o_ref[:, cs] = (hn * _sigmoid(om_ref[:, cs].astype(F32))).astype(BF16)

    _interleave(*[head_chain(hh) for hh in range(ML_HPS)])


def _mlstm(proj, if_arr, gate_bias, conv_w, conv_b, ng, B, S):
    T = B * S
    nc = S // ML_CHUNK
    W = ML_HPS * ML_HD
    n_hp = ML_HEADS // ML_HPS
    assert COL_QKM % W == 0 and COL_VM % W == 0 and COL_OM % W == 0 and ML_HEADS % ML_HPS == 0
    cq, ck, cv, co = COL_QKM // W, COL_QKM // W + n_hp, COL_VM // W, COL_OM // W
    rows = lambda col0: pl.BlockSpec((ML_CHUNK, W), lambda b, h, c: (b * nc + c, col0 + h))
    return pl.pallas_call(
        _mlstm_kernel,
        grid=(B, n_hp, nc),
        in_specs=[
            pl.BlockSpec(memory_space=pltpu.SMEM),
            rows(cq), rows(ck), rows(cv), rows(co),
            pl.BlockSpec((1, ML_HPS, 2, ML_CHUNK), lambda b, h, c: (b, h, 0, c)),
            pl.BlockSpec((CONV_W, W), lambda b, h, c: (0, h)),
            pl.BlockSpec((CONV_W, W), lambda b, h, c: (0, n_hp + h)),
            pl.BlockSpec((1, W), lambda b, h, c: (0, h)),
            pl.BlockSpec((1, W), lambda b, h, c: (0, n_hp + h)),
            pl.BlockSpec((1, W), lambda b, h, c: (0, h)),
        ],
        out_specs=pl.BlockSpec((ML_CHUNK, W), lambda b, h, c: (b * nc + c, h)),
        out_shape=jax.ShapeDtypeStruct((T, ML_HEADS * ML_HD), BF16),
        scratch_shapes=[
            pltpu.VMEM((ML_HPS, ML_HD, ML_HD), F32), pltpu.VMEM((ML_HPS, 1, ML_HD), F32),
            pltpu.VMEM((ML_HPS, 1, 1), F32),
            pltpu.VMEM((ML_HPS, ML_CHUNK + 8, ML_HD), F32), pltpu.VMEM((ML_HPS, ML_CHUNK + 8, ML_HD), F32),
        ],
        compiler_params=_cparams(("parallel", "parallel", "arbitrary")),
        name="mlstm",
    )(gate_bias, proj, proj, proj, proj, if_arr, conv_w, conv_w, conv_b, conv_b, ng)


def _pack_bf16_pair(lo, hi):
    lo_b = pltpu.bitcast(lo.astype(BF16).astype(F32), U32)
    hi_b = pltpu.bitcast(hi.astype(BF16).astype(F32), U32)
    return (lo_b >> 16) | hi_b


def _unpack_bf16_pair(p):
    lo = pltpu.bitcast(p << 16, F32)
    hi = pltpu.bitcast(p & jnp.uint32(0xFFFF0000), F32)
    return lo, hi


def _row(ref, r):
    return ref.at[pl.ds(r, 1), :]


def _outproj_kernel(nsa_ref, ml_ref, w_ref, x_ref, g2_ref, wr_ref, br_ref, tril_ref,
                    x1_ref, xp_ref, rt_ref, cnt_ref, carry_ref, xa_ref, xb_ref):
    i = pl.program_id(0)

    @pl.when(i == 0)
    def _():
        carry_ref[...] = jnp.zeros_like(carry_ref)
        xb_ref[...] = jnp.zeros_like(xb_ref)

    half = D_MODEL // 2
    subs = [slice(sub * (TM_OUT // OUT_SUBS), (sub + 1) * (TM_OUT // OUT_SUBS)) for sub in range(OUT_SUBS)]

    def step(cur_ref, prev_ref):
        carry = carry_ref[...]
        carry_box = [carry]

        def chain():
            for rs in subs:
                yield from _outproj_rows(rs, prev_ref, carry_box, g2_ref, wr_ref, br_ref, tril_ref,
                                         x1_ref, xp_ref, rt_ref)
                yield

        pending = chain()
        for rs in subs:
            for c in range(D_MODEL // OUT_COLS):
                cs = slice(c * OUT_COLS, (c + 1) * OUT_COLS)
                cur_ref[rs, cs] = (x_ref[rs, cs]
                                   + jnp.dot(nsa_ref[rs, :], w_ref[0:half, cs], preferred_element_type=F32)
                                   + jnp.dot(ml_ref[rs, :], w_ref[half:, cs], preferred_element_type=F32))
                next(pending, None)
        for _ in pending:
            pass
        carry = jnp.where(i > 0, carry_box[0], carry)
        carry_ref[...] = carry
        cnt_ref[...] = jnp.broadcast_to(carry, cnt_ref.shape)

    @pl.when(i % 2 == 0)
    def _():
        step(xa_ref, xb_ref)

    @pl.when(i % 2 == 1)
    def _():
        step(xb_ref, xa_ref)


def _outproj_rows(rs, src_ref, carry_box, g2_ref, wr_ref, br_ref, tril_ref, x1_ref, xp_ref, rt_ref):
    half = D_MODEL // 2
    x1 = src_ref[rs, :]
    x1_ref[rs, :] = x1
    xn = x1 * lax.rsqrt(jnp.mean(x1 * x1, axis=-1, keepdims=True) + NORM_EPS) * g2_ref[...]
    xp_ref[rs, :] = _pack_bf16_pair(xn[:, :half], xn[:, half:])
    logits = jnp.dot(xn.astype(BF16), wr_ref[...], preferred_element_type=F32) + br_ref[...]
    yield

    tm = logits.shape[0]
    lane = lax.broadcasted_iota(jnp.int32, (tm, LANE), 1)
    lane_f = lane.astype(F32)
    big = float(LANE)
    gmask = lane < MOE_GROUPS
    gmax = jnp.max(jnp.where(gmask, logits, NEG_INF), axis=1, keepdims=True)
    ge = jnp.where(gmask, jnp.exp(logits - gmax), 0.0)
    gp = ge / jnp.sum(ge, axis=1, keepdims=True)
    g_w = jnp.max(gp, axis=1, keepdims=True)
    g_idx = jnp.min(jnp.where(gmask, jnp.where(gp == g_w, lane_f, big), big), axis=1, keepdims=True)
    yield
    grp_of_lane = ((lane - MOE_GROUPS) >> EPG_LOG2).astype(F32)
    emask = jnp.where(lane >= MOE_GROUPS, grp_of_lane, -1.0) == g_idx
    emax = jnp.max(jnp.where(emask, logits, NEG_INF), axis=1, keepdims=True)
    ee = jnp.where(emask, jnp.exp(logits - emax), 0.0)
    ep = jnp.where(emask, ee / jnp.sum(ee, axis=1, keepdims=True), -1.0)
    v1 = jnp.max(ep, axis=1, keepdims=True)
    i1 = jnp.min(jnp.where(ep == v1, lane_f, big), axis=1, keepdims=True)
    ep2 = jnp.where(lane_f == i1, -1.0, ep)
    v2 = jnp.max(ep2, axis=1, keepdims=True)
    i2 = jnp.min(jnp.where(ep2 == v2, lane_f, big), axis=1, keepdims=True)
    w0 = g_w * v1 / (v1 + v2)
    w1 = g_w * v2 / (v1 + v2)
    e0 = i1 - float(MOE_GROUPS)
    e1 = i2 - float(MOE_GROUPS)

    yield
    oh0 = jnp.where(lane_f == e0, 1.0, 0.0)
    oh1 = jnp.where(lane_f == e1, 1.0, 0.0)
    tril = tril_ref[0:tm, 0:tm]
    pre0 = jnp.dot(tril, oh0.astype(BF16), preferred_element_type=F32)
    pre1 = jnp.dot(tril, oh1.astype(BF16), preferred_element_type=F32)
    yield
    carry = carry_box[0]
    tot0 = pre0[tm - 1:tm, :]
    tot1 = pre1[tm - 1:tm, :]
    rank0 = jnp.sum(oh0 * (pre0 - 1.0 + carry), axis=1, keepdims=True)
    rank1 = jnp.sum(oh1 * (pre1 - 1.0 + carry + tot0), axis=1, keepdims=True)

    rt = jnp.where(lane == 0, e0, jnp.where(lane == 1, e1, jnp.where(lane == 2, w0, jnp.where(
        lane == 3, w1, jnp.where(lane == 4, rank0, jnp.where(lane == 5, rank1, 0.0))))))
    rt_ref[rs, :] = rt
    carry_box[0] = carry + tot0 + tot1


def _out_proj(nsa_o, ml_o, w_out, x2, g2, w_r, b_r, tril):
    T = x2.shape[0]
    half = D_MODEL // 2
    n_tiles = T // TM_OUT
    cur = lambda i: (jnp.minimum(i, n_tiles - 1), 0)
    prev = lambda i: (jnp.maximum(i - 1, 0), 0)
    return pl.pallas_call(
        _outproj_kernel,
        grid=(n_tiles + 1,),
        in_specs=[
            pl.BlockSpec((TM_OUT, half), cur),
            pl.BlockSpec((TM_OUT, half), cur),
            pl.BlockSpec((D_MODEL, D_MODEL), lambda i: (0, 0)),
            pl.BlockSpec((TM_OUT, D_MODEL), cur),
            pl.BlockSpec((1, D_MODEL), lambda i: (0, 0)),
            pl.BlockSpec((D_MODEL, LANE), lambda i: (0, 0)),
            pl.BlockSpec((1, LANE), lambda i: (0, 0)),
            pl.BlockSpec((TM_OUT, TM_OUT), lambda i: (0, 0)),
        ],
        out_specs=[
            pl.BlockSpec((TM_OUT, D_MODEL), prev),
            pl.BlockSpec((TM_OUT, half), prev),
            pl.BlockSpec((TM_OUT, LANE), prev),
            pl.BlockSpec((8, LANE), lambda i: (0, 0)),
        ],
        out_shape=[
            jax.ShapeDtypeStruct((T, D_MODEL), F32),
            jax.ShapeDtypeStruct((T, half), U32),
            jax.ShapeDtypeStruct((T, LANE), F32),
            jax.ShapeDtypeStruct((8, LANE), F32),
        ],
        scratch_shapes=[pltpu.VMEM((1, LANE), F32), pltpu.VMEM((TM_OUT, D_MODEL), F32),
                        pltpu.VMEM((TM_OUT, D_MODEL), F32)],
        compiler_params=_cparams(("arbitrary",)),
        name="out_proj",
    )(nsa_o, ml_o, w_out, x2, g2, w_r, b_r, tril)


TD = 256
DISPATCH_BUFS = 3
DISPATCH_UNROLL = 8


def _dispatch_kernel(dest_ref, zblk_ref, meta_ref, xp_hbm, xs_hbm, zbuf_ref, zsem_ref, tbuf_ref, lsem_ref,
                     rsem_ref):
    n_assign = dest_ref.shape[0]
    nb = xs_hbm.shape[0] // BM
    n_used = meta_ref[0]
    zbuf_ref[...] = jnp.zeros_like(zbuf_ref)

    def zero_block(blk):
        return pltpu.make_async_copy(zbuf_ref, xs_hbm.at[pl.ds(blk * BM, BM), :], zsem_ref.at[0])

    def for_zero_blocks(fn):
        for e in range(N_EXPERTS):
            @pl.when(zblk_ref[e] >= 0)
            def _(e=e):
                fn(zero_block(zblk_ref[e]))

        def body(blk, _):
            fn(zero_block(blk))
            return 0
        lax.fori_loop(n_used, nb, body, 0)

    for_zero_blocks(lambda c: c.start())
    for_zero_blocks(lambda c: c.wait())

    n_tiles = n_assign // (2 * TD)

    def load(tile, slot):
        return pltpu.make_async_copy(xp_hbm.at[pl.ds(tile * TD, TD), :], tbuf_ref.at[slot], lsem_ref.at[slot])

    def wait_scatter(slot):
        for _ in range(2):
            pltpu.make_async_copy(tbuf_ref.at[slot], tbuf_ref.at[slot], rsem_ref.at[slot]).wait()

    load(0, 0).start()

    def tile_step(i, _):
        slot = i % DISPATCH_BUFS
        nslot = (i + 1) % DISPATCH_BUFS

        @pl.when(i + 1 < n_tiles)
        def _():
            @pl.when(i + 1 >= DISPATCH_BUFS)
            def _():
                wait_scatter(nslot)
            load(i + 1, nslot).start()

        load(i, slot).wait()

        def body(j, _):
            for u in range(DISPATCH_UNROLL):
                r = j * DISPATCH_UNROLL + u
                for k in range(2):
                    d = dest_ref[(i * TD + r) * 2 + k]
                    pltpu.make_async_copy(_row(tbuf_ref.at[slot], r), _row(xs_hbm, d),
                                          rsem_ref.at[slot]).start(priority=k)
            return 0
        lax.fori_loop(0, TD // DISPATCH_UNROLL, body, 0)
        return 0

    lax.fori_loop(0, n_tiles, tile_step, 0)
    for back in range(min(DISPATCH_BUFS, n_tiles)):
        wait_scatter((n_tiles - 1 - back) % DISPATCH_BUFS)


def _dispatch(dest, zblk, meta, xp, n_rows):
    half = D_MODEL // 2
    assert dest.shape[0] % (2 * TD) == 0 and dest.shape[0] // (2 * TD) >= DISPATCH_BUFS
    grid_spec = pltpu.PrefetchScalarGridSpec(
        num_scalar_prefetch=3,
        grid=(1,),
        in_specs=[pl.BlockSpec(memory_space=pl.ANY)],
        out_specs=pl.BlockSpec(memory_space=pl.ANY),
        scratch_shapes=[
            pltpu.VMEM((BM, half), U32),
            pltpu.SemaphoreType.DMA((1,)),
            pltpu.VMEM((DISPATCH_BUFS, TD, half), U32),
            pltpu.SemaphoreType.DMA((DISPATCH_BUFS,)),
            pltpu.SemaphoreType.DMA((DISPATCH_BUFS,)),
        ],
    )
    return pl.pallas_call(
        _dispatch_kernel,
        grid_spec=grid_spec,
        out_shape=jax.ShapeDtypeStruct((n_rows, half), U32),
        compiler_params=_cparams(("arbitrary",)),
        name="dispatch",
    )(dest, zblk, meta, xp)


def _expert_kernel(be_ref, nxt_ref, par_ref, meta_ref, x_ref, wg_hbm, wu_hbm, wd_hbm, y_ref,
                   wsg_ref, wsu_ref, wsd_ref, wsem_ref, wgb_ref, wub_ref, wdb_ref):
    i = pl.program_id(0)
    n_used = meta_ref[0]
    half = D_MODEL // 2

    def weight_copies(e, p):
        return (pltpu.make_async_copy(wg_hbm.at[e], wsg_ref.at[p], wsem_ref.at[p, 0]),
                pltpu.make_async_copy(wu_hbm.at[e], wsu_ref.at[p], wsem_ref.at[p, 1]),
                pltpu.make_async_copy(wd_hbm.at[e], wsd_ref.at[p], wsem_ref.at[p, 2]))

    def succ(e):
        return jnp.where(e >= 0, nxt_ref[jnp.maximum(e, 0)], -1)

    def start_weights(e, p):
        @pl.when(e >= 0)
        def _():
            for c in weight_copies(e, p):
                c.start(priority=1)

    def wait_staged(p):
        for c in weight_copies(0, p):
            c.wait()

    def cast_staged(p):
        wgb_ref[p] = wsg_ref[p].astype(BF16)
        wub_ref[p] = wsu_ref[p].astype(BF16)
        wdb_ref[p] = wsd_ref[p].astype(BF16)

    @pl.when(i == 0)
    def _():
        e0 = be_ref[0]
        start_weights(e0, 0)
        start_weights(succ(e0), 1)
        wait_staged(0)
        cast_staged(0)
        start_weights(succ(succ(e0)), 0)

    def step(p, cast_next):
        if cast_next:
            wait_staged(1 - p)
        lo, hi = _unpack_bf16_pair(x_ref[...])
        xl = lo.astype(BF16)
        xh = hi.astype(BF16)
        gt = (jnp.dot(xl, wgb_ref[p, 0:half, :], preferred_element_type=F32)
              + jnp.dot(xh, wgb_ref[p, half:, :], preferred_element_type=F32))
        up = (jnp.dot(xl, wub_ref[p, 0:half, :], preferred_element_type=F32)
              + jnp.dot(xh, wub_ref[p, half:, :], preferred_element_type=F32))
        hb = (_silu(gt) * up).astype(BF16)
        y = jnp.dot(hb, wdb_ref[p], preferred_element_type=F32)
        y_ref[...] = _pack_bf16_pair(y[:, :half], y[:, half:])
        if cast_next:
            cast_staged(1 - p)
            start_weights(succ(succ(succ(be_ref[i]))), 1 - p)

    e = be_ref[jnp.minimum(i, n_used - 1)]
    ends_expert = (i + 1 < n_used) & (be_ref[jnp.minimum(i + 1, n_used - 1)] != e)
    for p in range(2):
        for cast_next in (False, True):
            @pl.when((i < n_used) & (par_ref[i] == p)
                     & (ends_expert if cast_next else jnp.logical_not(ends_expert)))
            def _(p=p, cast_next=cast_next):
                step(p, cast_next)

    @pl.when(i >= n_used)
    def _():
        y_ref[...] = jnp.zeros_like(y_ref)


def _experts(block_expert, next_expert, block_parity, meta, xs, w_gate, w_up, w_down):
    half = D_MODEL // 2
    n_rows = xs.shape[0]
    nb = n_rows // BM
    grid_spec = pltpu.PrefetchScalarGridSpec(
        num_scalar_prefetch=4,
        grid=(nb,),
        in_specs=[
            pl.BlockSpec((BM, half), lambda i, be, nxt, par, meta: (jnp.minimum(i, meta[0] - 1), 0)),
            pl.BlockSpec(memory_space=pl.ANY), pl.BlockSpec(memory_space=pl.ANY),
            pl.BlockSpec(memory_space=pl.ANY),
        ],
        out_specs=pl.BlockSpec((BM, half), lambda i, be, nxt, par, meta: (i, 0)),
        scratch_shapes=[
            pltpu.VMEM((2, D_MODEL, D_EXPERT), F32),
            pltpu.VMEM((2, D_MODEL, D_EXPERT), F32),
            pltpu.VMEM((2, D_EXPERT, D_MODEL), F32),
            pltpu.SemaphoreType.DMA((2, 3)),
            pltpu.VMEM((2, D_MODEL, D_EXPERT), BF16),
            pltpu.VMEM((2, D_MODEL, D_EXPERT), BF16),
            pltpu.VMEM((2, D_EXPERT, D_MODEL), BF16),
        ],
    )
    return pl.pallas_call(
        _expert_kernel,
        grid_spec=grid_spec,
        out_shape=jax.ShapeDtypeStruct((n_rows, half), U32),
        compiler_params=_cparams(("arbitrary",)),
        name="experts",
    )(block_expert, next_expert, block_parity, meta, xs, w_gate, w_up, w_down)


def _combine_kernel(dest_ref, y_hbm, x1_ref, rt_ref, fg_ref, o_ref, ya_ref, yb_ref, sem_ref):
    i = pl.program_id(0)
    nt = pl.num_programs(0)
    half = D_MODEL // 2
    bufs = (ya_ref, yb_ref)

    def row_copy(tile, slot, r_tile, s, k):
        r = r_tile * 8 + s
        d = dest_ref[(tile * TC + r) * 2 + k]
        return pltpu.make_async_copy(_row(y_hbm, d), _row(bufs[slot].at[k], r), sem_ref.at[slot])

    def wait_rows(slot):
        pltpu.make_async_copy(bufs[slot], bufs[slot], sem_ref.at[slot]).wait()

    def step(slot):
        wait_rows(slot)
        for r in range(TC):
            for k in range(2):
                row_copy(i + 1, 1 - slot, r // 8, r % 8, k).start(priority=k)
        rt = rt_ref[...]
        w0 = rt[:, 2:3]
        w1 = rt[:, 3:4]
        lo0, hi0 = _unpack_bf16_pair(bufs[slot][0])
        lo1, hi1 = _unpack_bf16_pair(bufs[slot][1])
        xl = x1_ref[:, :half] + (w0 * lo0 + w1 * lo1)
        xh = x1_ref[:, half:] + (w0 * hi0 + w1 * hi1)
        ms = (jnp.sum(xl * xl, axis=1, keepdims=True) + jnp.sum(xh * xh, axis=1, keepdims=True)) / float(D_MODEL)
        inv = lax.rsqrt(ms + NORM_EPS)
        o_ref[:, :half] = xl * inv * fg_ref[:, :half]
        o_ref[:, half:] = xh * inv * fg_ref[:, half:]

        @pl.when(i == nt - 1)
        def _():
            wait_rows(1 - slot)

    @pl.when(i == 0)
    def _():
        def body(rt, _):
            for s in range(8):
                for k in range(2):
                    row_copy(0, 0, rt, s, k).start(priority=k)
            return 0
        lax.fori_loop(0, TC // 8, body, 0)

    @pl.when(i % 2 == 0)
    def _():
        step(0)

    @pl.when(i % 2 == 1)
    def _():
        step(1)


def _combine(dest, ys, x1, route, fg):
    T = x1.shape[0]
    half = D_MODEL // 2
    grid_spec = pltpu.PrefetchScalarGridSpec(
        num_scalar_prefetch=1,
        grid=(T // TC,),
        in_specs=[
            pl.BlockSpec(memory_space=pl.ANY),
            pl.BlockSpec((TC, D_MODEL), lambda i, d: (i, 0)),
            pl.BlockSpec((TC, LANE), lambda i, d: (i, 0)),
            pl.BlockSpec((1, D_MODEL), lambda i, d: (0, 0)),
        ],
        out_specs=pl.BlockSpec((TC, D_MODEL), lambda i, d: (i, 0)),
        scratch_shapes=[
            pltpu.VMEM((2, TC, half), U32),
            pltpu.VMEM((2, TC, half), U32),
            pltpu.SemaphoreType.DMA((2,)),
        ],
    )
    return pl.pallas_call(
        _combine_kernel,
        grid_spec=grid_spec,
        out_shape=jax.ShapeDtypeStruct((T, D_MODEL), F32),
        compiler_params=_cparams(("arbitrary",)),
        name="combine",
    )(dest, ys, x1, route, fg)


def _cover_matrix(S):
    n_blk = S // CMP_STRIDE
    n_sel = S // SEL_BLOCK
    cs = np.arange(n_blk) * CMP_STRIDE
    ss = np.arange(n_sel) * SEL_BLOCK
    shared = np.minimum(cs[:, None] + CMP_LEN, ss[None, :] + SEL_BLOCK) - np.maximum(cs[:, None], ss[None, :])
    return (np.clip(shared, 0, None) / CMP_LEN).T.astype(np.float32)


def _block_mask_matrix(S):
    n_sel = S // SEL_BLOCK
    assert n_sel <= LANE
    e = np.zeros((S, LANE), np.float32)
    e[np.arange(S), np.arange(S) // SEL_BLOCK] = NEG_INF
    return e


def _inv_freq_row():
    inv = np.power(np.float32(ROPE_THETA), -np.arange(ROPE_HALF, dtype=np.float32) * 2.0 / ROPE_DIM)
    return np.tile(inv, ROPE_SLOTS).reshape(1, LANE).astype(np.float32)


def _packed_positions(positions, T):
    rows = TS_ROPE // ROPE_SLOTS
    p = positions.reshape(T // TS_ROPE, ROPE_SLOTS, rows).transpose(0, 2, 1).astype(F32)
    return jnp.repeat(p, ROPE_HALF, axis=2).reshape(T // ROPE_SLOTS, LANE)


def _layer(x2, positions, B, S, norm1_g, w_in, cmp_pe_k, cmp_pe_v, cmp_wk1, cmp_wk2, cmp_wv1, cmp_wv2,
           nsa_norm_g, conv_w, conv_b, b_igate, b_fgate, mlstm_norm_g, w_out, norm2_g,
           w_group, b_group, w_router, b_router, w_exp_gate, w_exp_up, w_exp_down, out_norm_g):
    T = B * S
    w_main, w_small = _weight_prep(jnp.transpose(w_in[0]))
    posb = _packed_positions(positions, T)
    invf = jnp.asarray(_inv_freq_row())

    proj, small = _in_proj(x2, norm1_g.reshape(1, -1), w_main, w_small)
    q_r, k_r = _rope(posb, invf, proj)
    kc, vc = _compress(k_r, proj, B, S, cmp_pe_k, cmp_pe_v, cmp_wk1.astype(BF16), cmp_wk2.astype(BF16),
                       cmp_wv1.astype(BF16), cmp_wv2.astype(BF16))
    eye = np.eye(S // SEL_BLOCK, LANE, dtype=np.float32)
    nsa_o = _nsa(q_r, k_r, proj, small, kc, vc, jnp.asarray(_cover_matrix(S), BF16),
                 jnp.asarray(_block_mask_matrix(S), BF16), jnp.asarray(eye, BF16),
                 nsa_norm_g.reshape(1, -1), B, S)

    if_arr = small[:, 24:32].reshape(B, S, 2, ML_HEADS).transpose(0, 3, 2, 1)
    gate_bias = jnp.stack([b_igate, b_fgate]).astype(F32)
    ml_o = _mlstm(proj, if_arr, gate_bias, conv_w, conv_b.reshape(1, -1), mlstm_norm_g.reshape(1, -1), B, S)

    w_r = jnp.concatenate([w_group, w_router, jnp.zeros((D_MODEL, LANE - MOE_GROUPS - N_EXPERTS), F32)],
                          axis=1).astype(BF16)
    b_r = jnp.concatenate([b_group, b_router, jnp.zeros((LANE - MOE_GROUPS - N_EXPERTS,), F32)]).reshape(1, LANE)
    tril = jnp.asarray(np.tril(np.ones((TM_OUT, TM_OUT), np.float32)), BF16)
    x1, xp, route, cnt = _out_proj(nsa_o, ml_o, w_out.astype(BF16), x2, norm2_g.reshape(1, -1), w_r, b_r, tril)

    n_rows = T * 2 + N_EXPERTS * BM
    counts = cnt[0, :N_EXPERTS].astype(jnp.int32)
    pcounts = (counts + BM - 1) // BM * BM
    pends = jnp.cumsum(pcounts)
    pstarts = pends - pcounts
    eid = route[:, 0:2].astype(jnp.int32)
    onehot = (eid[..., None] == jnp.arange(N_EXPERTS, dtype=jnp.int32)).astype(F32)
    start_blk = jnp.einsum('tke,e->tk', onehot, (pstarts // BM).astype(F32)).astype(jnp.int32)
    dest = (start_blk * BM + route[:, 4:6].astype(jnp.int32)).reshape(T * 2)
    block_expert = jnp.minimum(jnp.searchsorted(pends, jnp.arange(n_rows // BM) * BM, side='right'),
                               N_EXPERTS - 1).astype(jnp.int32)
    present = jnp.where(counts > 0, jnp.arange(N_EXPERTS, dtype=jnp.int32), N_EXPERTS)
    later = jnp.concatenate([lax.cummin(present[::-1])[::-1][1:], jnp.full((1,), N_EXPERTS, jnp.int32)])
    next_expert = jnp.where(later < N_EXPERTS, later, -1).astype(jnp.int32)
    meta = (pends[-1:] // BM).astype(jnp.int32)

    last_block = jnp.where(counts > 0, pends // BM - 1, -1).astype(jnp.int32)

    xs = _dispatch(dest, last_block, meta, xp, n_rows)
    ordinal = jnp.cumsum((counts > 0).astype(jnp.int32)) - 1
    block_parity = (ordinal[block_expert] % 2).astype(jnp.int32)
    ys = _experts(block_expert, next_expert, block_parity, meta, xs, w_exp_gate, w_exp_up, w_exp_down)
    dest_pad = jnp.concatenate([dest, jnp.zeros((2 * TC,), jnp.int32)])
    return _combine(dest_pad, ys, x1, route, out_norm_g.reshape(1, -1))


def kernel(x, positions, norm1_g, w_in, cmp_pe_k, cmp_pe_v, cmp_wk1, cmp_wk2, cmp_wv1, cmp_wv2, nsa_norm_g,
           conv_w, conv_b, b_igate, b_fgate, mlstm_norm_g, w_out, norm2_g, w_group, b_group, w_router,
           b_router, w_exp_gate, w_exp_up, w_exp_down, final_norm_g):
    B, S, D = x.shape
    assert D == D_MODEL and norm1_g.shape[0] == 1, "single-layer, D_MODEL-wide configuration only"
    assert S % ML_CHUNK == 0 and S % SEL_BUCKET == 0 and S >= WIN_KEYS and (B * S) % TM_IN == 0
    out = _layer(x.reshape(B * S, D), positions, B, S, norm1_g[0], w_in, cmp_pe_k[0], cmp_pe_v[0],
                 cmp_wk1[0], cmp_wk2[0], cmp_wv1[0], cmp_wv2[0], nsa_norm_g[0], conv_w[0], conv_b[0],
                 b_igate[0], b_fgate[0], mlstm_norm_g[0], w_out[0], norm2_g[0], w_group[0], b_group[0],
                 w_router[0], b_router[0], w_exp_gate[0], w_exp_up[0], w_exp_down[0], final_norm_g)
    return out.reshape(B, S, D)
```

```python
import functools

import numpy as np
import jax
import jax.numpy as jnp
from jax import lax
from jax.experimental import pallas as pl
from jax.experimental.pallas import tpu as pltpu

F32 = jnp.float32
BF16 = jnp.bfloat16
U32 = jnp.uint32

D_MODEL = 2048
NSA_HEADS = 8
NSA_GROUPS = 2
NSA_J = NSA_HEADS // NSA_GROUPS
HD = 128
CMP_LEN = 32
CMP_STRIDE = 16
SEL_BLOCK = 64
SEL_TOPK = 8
WINDOW = 512
ROPE_THETA = 500000.0
ROPE_DIM = 32
ROPE_HALF = 16
ML_HEADS = 4
ML_HD = 256
CONV_W = 4
MOE_GROUPS = 8
EPG = 8
N_EXPERTS = 64
D_EXPERT = 512
SEL_BLOCK_LOG2 = 6
EPG_LOG2 = 3
assert (1 << SEL_BLOCK_LOG2) == SEL_BLOCK and (1 << EPG_LOG2) == EPG
NORM_EPS = 1e-6
NEG_INF = -1e30
FORCE_SCORE = 1000.0

COL_Q = 0
COL_KV = 1024
COL_QKM = 2560
COL_VM = 4608
COL_OM = 5632
N_MAIN = 6656

LANE = 128
VMEM_LIMIT = 56 * 1024 * 1024

TM_IN = 1024
TN_IN = 1664
TS_ROPE = 256
TQ = 256
SEL_BUCKET = 256
WIN_KEYS = WINDOW + TQ
ML_CHUNK = 256
TM_OUT = 512
OUT_SUBS = 2
OUT_COLS = 512
BM = 256
TC = 256

NT_DIMS = (((1,), (1,)), ((), ()))
TN_DIMS = (((0,), (0,)), ((), ()))


def _cparams(sem):
    return pltpu.CompilerParams(dimension_semantics=sem, vmem_limit_bytes=VMEM_LIMIT)


def _sigmoid(x):
    return 0.5 * jnp.tanh(0.5 * x) + 0.5


def _interleave(*chains):
    live = list(chains)
    while live:
        for g in list(live):
            try:
                next(g)
            except StopIteration:
                live.remove(g)


def _silu(x):
    h = 0.5 * x
    return h + h * jnp.tanh(h)


W_GATES = 2560
W_QKM = 2584
W_IGATE = 6680
W_END = 6688
TR_PREP = 512
assert W_GATES % TR_PREP == 0 and N_MAIN % TR_PREP == 0 and W_IGATE - W_QKM == N_MAIN - W_GATES


def _wprep_kernel(wt_hbm, wm_ref, ws_ref, wbuf_ref, sbuf_ref, sem_ref, ssem_ref):
    i = pl.program_id(0)
    n_gate = W_QKM - W_GATES
    n_if = W_END - W_IGATE

    def load(blk, slot):
        start = pl.multiple_of(blk * TR_PREP + jnp.where(blk * TR_PREP >= W_GATES, n_gate, 0), 8)
        return pltpu.make_async_copy(wt_hbm.at[pl.ds(start, TR_PREP), :], wbuf_ref.at[slot], sem_ref.at[slot])

    def small_loads():
        return (pltpu.make_async_copy(wt_hbm.at[pl.ds(W_GATES, n_gate), :], sbuf_ref.at[pl.ds(0, n_gate), :],
                                      ssem_ref.at[0]),
                pltpu.make_async_copy(wt_hbm.at[pl.ds(W_IGATE, n_if), :], sbuf_ref.at[pl.ds(n_gate, n_if), :],
                                      ssem_ref.at[1]))

    @pl.when(i == 0)
    def _():
        load(0, 0).start()
        for c in small_loads():
            c.start()
        for c in small_loads():
            c.wait()
        ws_ref[...] = jnp.zeros_like(ws_ref)
        ws_ref[0:n_gate + n_if, :] = sbuf_ref[...].astype(BF16)

    @pl.when(i + 1 < pl.num_programs(0))
    def _():
        load(i + 1, (i + 1) % 2).start()

    load(i, i % 2).wait()
    wm_ref[...] = wbuf_ref[i % 2].astype(BF16)


def _weight_prep(wt):
    assert wt.shape == (W_END, D_MODEL)
    n_small = (W_QKM - W_GATES) + (W_END - W_IGATE)
    return pl.pallas_call(
        _wprep_kernel,
        grid=(N_MAIN // TR_PREP,),
        in_specs=[pl.BlockSpec(memory_space=pl.ANY)],
        out_specs=[pl.BlockSpec((TR_PREP, D_MODEL), lambda i: (i, 0)),
                   pl.BlockSpec((LANE, D_MODEL), lambda i: (0, 0))],
        out_shape=[jax.ShapeDtypeStruct((N_MAIN, D_MODEL), BF16),
                   jax.ShapeDtypeStruct((LANE, D_MODEL), BF16)],
        scratch_shapes=[pltpu.VMEM((2, TR_PREP, D_MODEL), F32), pltpu.VMEM((n_small, D_MODEL), F32),
                        pltpu.SemaphoreType.DMA((2,)), pltpu.SemaphoreType.DMA((2,))],
        compiler_params=_cparams(("arbitrary",)),
        name="weight_prep",
    )(wt)


def _inproj_kernel(x_ref, g_ref, w_ref, ws_ref, o_ref, os_ref, h_ref):
    @pl.when(pl.program_id(1) == 0)
    def _():
        x = x_ref[...]
        ms = jnp.mean(x * x, axis=-1, keepdims=True)
        h_ref[...] = (x * lax.rsqrt(ms + NORM_EPS) * g_ref[...]).astype(BF16)
        os_ref[...] = lax.dot_general(h_ref[...], ws_ref[...], NT_DIMS, preferred_element_type=F32)

    o_ref[...] = lax.dot_general(h_ref[...], w_ref[...], NT_DIMS,
                                 preferred_element_type=F32).astype(o_ref.dtype)


def _in_proj(x2, g1, w_main, w_small):
    T = x2.shape[0]
    return pl.pallas_call(
        _inproj_kernel,
        grid=(T // TM_IN, N_MAIN // TN_IN),
        in_specs=[
            pl.BlockSpec((TM_IN, D_MODEL), lambda m, n: (m, 0)),
            pl.BlockSpec((1, D_MODEL), lambda m, n: (0, 0)),
            pl.BlockSpec((TN_IN, D_MODEL), lambda m, n: (n, 0)),
            pl.BlockSpec((LANE, D_MODEL), lambda m, n: (0, 0)),
        ],
        out_specs=[
            pl.BlockSpec((TM_IN, TN_IN), lambda m, n: (m, n)),
            pl.BlockSpec((TM_IN, LANE), lambda m, n: (m, 0)),
        ],
        out_shape=[
            jax.ShapeDtypeStruct((T, N_MAIN), BF16),
            jax.ShapeDtypeStruct((T, LANE), F32),
        ],
        scratch_shapes=[pltpu.VMEM((TM_IN, D_MODEL), BF16)],
        compiler_params=_cparams(("parallel", "arbitrary")),
        name="in_proj",
    )(x2, g1, w_main, w_small)


ROPE_SLOTS = LANE // ROPE_HALF


def _rope_slots(pos_ref, invf_ref):
    ang = pos_ref[...] * invf_ref[...]
    c = jnp.cos(ang)
    s = jnp.sin(ang)
    rows = ang.shape[0]
    lane = lax.broadcasted_iota(jnp.int32, ang.shape, 1)
    lo = lane < ROPE_HALF
    mid = lane < ROPE_DIM

    def lanes_from(x, src):
        shift = (-src) % LANE
        return x if shift == 0 else pltpu.roll(x, shift, 1)

    for slot in range(ROPE_SLOTS):
        src = slot * ROPE_HALF
        cf = jnp.where(lo, lanes_from(c, src), jnp.where(mid, lanes_from(c, src - ROPE_HALF), 1.0))
        sa = jnp.where(lo, -lanes_from(s, src), 0.0)
        sb = jnp.where(lo, 0.0, jnp.where(mid, lanes_from(s, src - ROPE_HALF), 0.0))

        def rope(x, cf=cf, sa=sa, sb=sb):
            return x * cf + pltpu.roll(x, LANE - ROPE_HALF, 1) * sa + pltpu.roll(x, ROPE_HALF, 1) * sb

        yield slice(slot * rows, (slot + 1) * rows), rope


def _rope_kernel(pos_ref, invf_ref, kc_ref, ks_ref, kw_ref, ko_ref):
    for rs, rope in _rope_slots(pos_ref, invf_ref):
        for i, r in enumerate((kc_ref, ks_ref, kw_ref)):
            for g in range(NSA_GROUPS):
                sl = slice(g * HD, (g + 1) * HD)
                so = slice(i * 2 * HD + g * HD, i * 2 * HD + (g + 1) * HD)
                ko_ref[rs, so] = rope(r[rs, sl].astype(F32)).astype(BF16)


def _rope(posb, invf, proj):
    T = proj.shape[0]
    kvb = COL_KV // 256
    return pl.pallas_call(
        _rope_kernel,
        grid=(T // TS_ROPE,),
        in_specs=[
            pl.BlockSpec((TS_ROPE // ROPE_SLOTS, LANE), lambda i: (i, 0)),
            pl.BlockSpec((1, LANE), lambda i: (0, 0)),
            pl.BlockSpec((TS_ROPE, 256), lambda i: (i, kvb + 0)),
            pl.BlockSpec((TS_ROPE, 256), lambda i: (i, kvb + 2)),
            pl.BlockSpec((TS_ROPE, 256), lambda i: (i, kvb + 4)),
        ],
        out_specs=pl.BlockSpec((TS_ROPE, 768), lambda i: (i, 0)),
        out_shape=jax.ShapeDtypeStruct((T, 768), BF16),
        compiler_params=_cparams(("parallel",)),
        name="rope",
    )(posb, invf, proj, proj, proj)


def _compress_kernel(k_ref, v_ref, pek_ref, pev_ref, w1k_ref, w2k_ref, w1v_ref, w2v_ref,
                     kc_ref, vc_ref, xs_ref):
    S = k_ref.shape[0]
    n_blk = S // CMP_STRIDE
    for src, pe, w1, w2, dst in ((k_ref, pek_ref, w1k_ref, w2k_ref, kc_ref),
                                 (v_ref, pev_ref, w1v_ref, w2v_ref, vc_ref)):
        xs_ref[0:S, :] = src[...].astype(F32)
        xs_ref[S:S + CMP_LEN, :] = jnp.zeros((CMP_LEN, HD), F32)
        acc = jnp.zeros((n_blk, HD), F32)
        for l in range(CMP_LEN):
            a = xs_ref[pl.ds(l, n_blk, stride=CMP_STRIDE), :] + pe[l:l + 1, :]
            acc = acc + jnp.dot(a.astype(BF16), w1[l * HD:(l + 1) * HD, :],
                                preferred_element_type=F32)
        hid = _silu(acc)
        out = jnp.dot(hid.astype(BF16), w2[...], preferred_element_type=F32)
        dst[0, 0] = out.astype(BF16)


def _compress(k_r, proj, B, S, pek, pev, w1k, w2k, w1v, w2v):
    n_blk = S // CMP_STRIDE
    vcol = (COL_KV + 256) // HD
    full = lambda shape: pl.BlockSpec(shape, lambda b, g: tuple(0 for _ in shape))
    return pl.pallas_call(
        _compress_kernel,
        grid=(B, NSA_GROUPS),
        in_specs=[
            pl.BlockSpec((S, HD), lambda b, g: (b, g)),
            pl.BlockSpec((S, HD), lambda b, g: (b, vcol + g)),
            full((CMP_LEN, HD)), full((CMP_LEN, HD)),
            full((CMP_LEN * HD, HD)), full((HD, HD)),
            full((CMP_LEN * HD, HD)), full((HD, HD)),
        ],
        out_specs=[
            pl.BlockSpec((1, 1, n_blk, HD), lambda b, g: (b, g, 0, 0)),
            pl.BlockSpec((1, 1, n_blk, HD), lambda b, g: (b, g, 0, 0)),
        ],
        out_shape=[
            jax.ShapeDtypeStruct((B, NSA_GROUPS, n_blk, HD), BF16),
            jax.ShapeDtypeStruct((B, NSA_GROUPS, n_blk, HD), BF16),
        ],
        scratch_shapes=[pltpu.VMEM((S + CMP_LEN, HD), F32)],
        compiler_params=_cparams(("parallel", "parallel")),
        name="compress",
    )(k_r, proj, pek, pev, w1k, w2k, w1v, w2v)


def _nsa_kernel(pos_ref, invf_ref, q_ref, kc_ref, vc_ref, ks_ref, vs_ref, kw_ref, vw_ref, sm_ref, covt_ref,
                eneg_ref, eye_ref, ng_ref, o_ref, obuf_ref, owin_ref, osel_ref, qs_ref):
    qi = pl.program_id(1)
    q0 = qi * TQ
    R = NSA_J * TQ
    n_cmp = kc_ref.shape[2]
    n_sel = ks_ref.shape[0] // SEL_BLOCK
    n_win = WIN_KEYS // TQ

    def row_t(shape):
        r = lax.broadcasted_iota(jnp.int32, shape, 0)
        return q0 + (r & (TQ - 1))

    scale = HD ** -0.5
    for rs, rope in _rope_slots(pos_ref, invf_ref):
        for h in range(NSA_HEADS):
            g, j = divmod(h, NSA_J)
            dst = slice(j * TQ + rs.start, j * TQ + rs.stop)
            qs_ref[g, dst, :] = (rope(q_ref[rs, h * HD:(h + 1) * HD].astype(F32)) * scale).astype(BF16)
    qgs = [qs_ref[g] for g in range(NSA_GROUPS)]

    def with_ones(v):
        lane = lax.broadcasted_iota(jnp.int32, v.shape, 1)
        return jnp.concatenate([v, jnp.where(lane == 0, 1.0, 0.0).astype(BF16)], axis=1)

    def window_chain(g):
        w0 = pl.multiple_of(jnp.maximum(q0 - WINDOW, 0), TQ)
        kt = kw_ref[pl.ds(w0, WIN_KEYS), g * HD:(g + 1) * HD]
        vt = with_ones(vw_ref[pl.ds(w0, WIN_KEYS), g * HD:(g + 1) * HD])
        sc = lax.dot_general(qgs[g], kt, NT_DIMS, preferred_element_type=F32)
        yield
        diff = row_t((R, WIN_KEYS)) - (w0 + lax.broadcasted_iota(jnp.int32, (R, WIN_KEYS), 1))
        keep = jnp.where(diff >= 0, diff, WINDOW) < WINDOW
        sc = jnp.where(keep, sc, NEG_INF)
        mw = jnp.max(sc, axis=1, keepdims=True)
        yield
        pw = jnp.exp((sc - mw).astype(BF16))
        acc = jnp.dot(pw, vt, preferred_element_type=F32)
        yield
        owin_ref[g] = acc[:, 0:HD] / acc[:, HD:HD + 1]

    gates = _sigmoid(sm_ref[...])
    o_cmps = [None] * NSA_GROUPS
    qps = [None] * NSA_GROUPS

    def cmp_chain(g):
        qg = qgs[g]

        s = lax.dot_general(qg, kc_ref[0, g], NT_DIMS, preferred_element_type=F32)
        yield
        n_lane = lax.broadcasted_iota(jnp.int32, (R, n_cmp), 1)
        cmask = (n_lane * CMP_STRIDE + (CMP_LEN - 1)) <= row_t((R, n_cmp))
        s = jnp.where(cmask, s, NEG_INF)
        m = jnp.max(s, axis=1, keepdims=True)
        e = jnp.where(cmask, jnp.exp(s - m), 0.0)
        l = jnp.sum(e, axis=1, keepdims=True)
        p = (e / jnp.where(l > 0.0, l, 1.0)).astype(BF16)
        yield
        o_cmps[g] = jnp.dot(p, vc_ref[0, g], preferred_element_type=F32)
        impr = lax.dot_general(covt_ref[...], p, NT_DIMS, preferred_element_type=F32)
        imp = impr[:, 0:TQ]
        for j in range(1, NSA_J):
            imp = imp + impr[:, j * TQ:(j + 1) * TQ]
        yield

        m_sub = lax.broadcasted_iota(jnp.int32, (n_sel, TQ), 0)
        jt = (q0 + lax.broadcasted_iota(jnp.int32, (n_sel, TQ), 1)) >> SEL_BLOCK_LOG2
        forced = jnp.where(m_sub == 0, FORCE_SCORE,
                           jnp.where(m_sub == jt, FORCE_SCORE,
                                     jnp.where(m_sub == jt - 1, FORCE_SCORE, 0.0)))
        score = jnp.where(m_sub <= jt, imp + forced, -1.0)
        rank = jnp.zeros((n_sel, TQ), F32)
        for mp in range(n_sel):
            row = score[mp:mp + 1, :]
            ge = jnp.where(row >= score, 1.0, 0.0)
            gt = jnp.where(row > score, 1.0, 0.0)
            rank = rank + jnp.where(m_sub > mp, ge, gt)
        notsel_t = jnp.where(rank < float(min(SEL_TOPK, n_sel)), 0.0, 1.0).astype(BF16)
        notsel = lax.dot_general(notsel_t, eye_ref[...], TN_DIMS,
                                 preferred_element_type=F32).astype(BF16)
        qps[g] = jnp.concatenate([qg, jnp.concatenate([notsel] * NSA_J, axis=0)], axis=1)

    _interleave(*[c for g in range(NSA_GROUPS) for c in (window_chain(g), cmp_chain(g))])

    def sel_chain(g, nk):
        kt = jnp.concatenate([ks_ref[0:nk, g * HD:(g + 1) * HD], eneg_ref[0:nk, :]], axis=1)
        vt = with_ones(vs_ref[0:nk, g * HD:(g + 1) * HD])
        sc = lax.dot_general(qps[g], kt, NT_DIMS, preferred_element_type=F32)
        yield
        klane = lax.broadcasted_iota(jnp.int32, (R, SEL_BUCKET), 1) + (nk - SEL_BUCKET)
        tail = jnp.where(klane <= row_t((R, SEL_BUCKET)), sc[:, nk - SEL_BUCKET:], NEG_INF)
        sc = tail if nk == SEL_BUCKET else jnp.concatenate([sc[:, 0:nk - SEL_BUCKET], tail], axis=1)
        ms = jnp.max(sc, axis=1, keepdims=True)
        yield
        acc = jnp.dot(jnp.exp((sc - ms).astype(BF16)), vt, preferred_element_type=F32)
        yield
        osel_ref[g] = acc[:, 0:HD] / acc[:, HD:HD + 1]

    for b in range(ks_ref.shape[0] // SEL_BUCKET):
        @pl.when(q0 // SEL_BUCKET == b)
        def _(b=b):
            _interleave(*[sel_chain(g, (b + 1) * SEL_BUCKET) for g in range(NSA_GROUPS)])

    ssq = jnp.zeros((TQ, 1), F32)
    for g in range(NSA_GROUPS):
        o_sel = osel_ref[g]
        o_cmp = o_cmps[g]
        o_win = owin_ref[g]
        for j in range(NSA_J):
            h = g * NSA_J + j
            rs = slice(j * TQ, (j + 1) * TQ)
            o = (gates[:, 3 * h:3 * h + 1] * o_cmp[rs] + gates[:, 3 * h + 1:3 * h + 2] * o_sel[rs]
                 + gates[:, 3 * h + 2:3 * h + 3] * o_win[rs])
            ssq = ssq + jnp.sum(o * o, axis=1, keepdims=True)
            obuf_ref[:, h * HD:(h + 1) * HD] = o

    inv = lax.rsqrt(ssq / float(NSA_HEADS * HD) + NORM_EPS)
    o_ref[...] = (obuf_ref[...] * inv * ng_ref[...]).astype(BF16)


def _nsa(posb, invf, k_r, proj, small, kc, vc, covt, eneg, eye, ng, B, S):
    T = B * S
    nq = S // TQ
    n_blk = S // CMP_STRIDE
    n_sel = S // SEL_BLOCK
    kvb = COL_KV // 256
    assert TQ == TS_ROPE
    return pl.pallas_call(
        _nsa_kernel,
        grid=(B, nq),
        in_specs=[
            pl.BlockSpec((TQ // ROPE_SLOTS, LANE), lambda b, i: (b * nq + i, 0)),
            pl.BlockSpec((1, LANE), lambda b, i: (0, 0)),
            pl.BlockSpec((TQ, 1024), lambda b, i: (b * nq + i, 0)),
            pl.BlockSpec((1, NSA_GROUPS, n_blk, HD), lambda b, i: (b, 0, 0, 0)),
            pl.BlockSpec((1, NSA_GROUPS, n_blk, HD), lambda b, i: (b, 0, 0, 0)),
            pl.BlockSpec((S, 256), lambda b, i: (b, 1)),
            pl.BlockSpec((S, 256), lambda b, i: (b, kvb + 3)),
            pl.BlockSpec((S, 256), lambda b, i: (b, 2)),
            pl.BlockSpec((S, 256), lambda b, i: (b, kvb + 5)),
            pl.BlockSpec((TQ, LANE), lambda b, i: (b * nq + i, 0)),
            pl.BlockSpec((n_sel, n_blk), lambda b, i: (0, 0)),
            pl.BlockSpec((S, LANE), lambda b, i: (0, 0)),
            pl.BlockSpec((n_sel, LANE), lambda b, i: (0, 0)),
            pl.BlockSpec((1, 1024), lambda b, i: (0, 0)),
        ],
        out_specs=pl.BlockSpec((TQ, 1024), lambda b, i: (b * nq + i, 0)),
        out_shape=jax.ShapeDtypeStruct((T, 1024), BF16),
        scratch_shapes=[pltpu.VMEM((TQ, 1024), F32), pltpu.VMEM((NSA_GROUPS, NSA_J * TQ, HD), F32),
                        pltpu.VMEM((NSA_GROUPS, NSA_J * TQ, HD), F32),
                        pltpu.VMEM((NSA_GROUPS, NSA_J * TQ, HD), BF16)],
        compiler_params=_cparams(("parallel", "parallel")),
        name="nsa",
    )(posb, invf, proj, kc, vc, k_r, proj, k_r, proj, small, covt, eneg, eye, ng)


ML_HPS = 2


def _mlstm_kernel(bias_ref, q_ref, k_ref, v_ref, om_ref, if_ref, cwq_ref, cwk_ref, cbq_ref, cbk_ref,
                  ng_ref, o_ref, c_ref, n_ref, m_ref, xq_ref, xk_ref):
    hp = pl.program_id(1)
    L = ML_CHUNK

    @pl.when(pl.program_id(2) == 0)
    def _():
        c_ref[...] = jnp.zeros_like(c_ref)
        n_ref[...] = jnp.zeros_like(n_ref)
        m_ref[...] = jnp.zeros_like(m_ref)
        xq_ref[:, 0:8, :] = jnp.zeros((ML_HPS, 8, ML_HD), F32)
        xk_ref[:, 0:8, :] = jnp.zeros((ML_HPS, 8, ML_HD), F32)

    def conv_silu(x_ref, hh, xb_ref, w_ref, b_ref):
        cs = slice(hh * ML_HD, (hh + 1) * ML_HD)
        xb_ref[hh, 8:8 + L, :] = x_ref[:, cs].astype(F32)
        y = b_ref[:, cs] + xb_ref[hh, 8:8 + L, :] * w_ref[CONV_W - 1:CONV_W, cs]
        for k in range(1, CONV_W):
            y = y + xb_ref[hh, 8 - k:8 - k + L, :] * w_ref[CONV_W - 1 - k:CONV_W - k, cs]
        xb_ref[hh, 0:8, :] = xb_ref[hh, L:L + 8, :]
        return _silu(y)

    r = lax.broadcasted_iota(jnp.int32, (L, L), 0)
    cidx = lax.broadcasted_iota(jnp.int32, (L, L), 1)
    tril = cidx <= r
    eye = cidx == r

    def head_chain(hh):
        h = hp * ML_HPS + hh
        cs = slice(hh * ML_HD, (hh + 1) * ML_HD)
        qf = conv_silu(q_ref, hh, xq_ref, cwq_ref, cbq_ref) * (ML_HD ** -0.5)
        yield
        kf = conv_silu(k_ref, hh, xk_ref, cwk_ref, cbk_ref)
        yield
        qb = qf.astype(BF16)
        kb = kf.astype(BF16)
        vb = v_ref[:, cs]
        vf = vb.astype(F32)

        ic = if_ref[0, hh, 0:1, :] + bias_ref[0, h]
        fp = if_ref[0, hh, 1:2, :] + bias_ref[1, h]
        fc = jnp.minimum(fp, 0.0) - jnp.log(1.0 + jnp.exp(-jnp.abs(fp)))

        fc_b = jnp.broadcast_to(fc, (L, L))
        ic_b = jnp.broadcast_to(ic, (L, L))
        b_col = jnp.sum(jnp.where(tril, fc_b, 0.0), axis=1, keepdims=True)
        fc_col = jnp.sum(jnp.where(eye, fc_b, 0.0), axis=1, keepdims=True)
        ic_col = jnp.sum(jnp.where(eye, ic_b, 0.0), axis=1, keepdims=True)
        b_row = jnp.sum(jnp.where(r <= cidx, jnp.broadcast_to(fc_col, (L, L)), 0.0),
                        axis=0, keepdims=True)
        b_last = b_col[L - 1:L, :]
        m_prev = m_ref[hh]
        yield

        d_log = jnp.where(tril, b_col - b_row + ic, NEG_INF)
        inter = b_col + m_prev
        m_t = jnp.maximum(inter, jnp.max(d_log, axis=1, keepdims=True))
        w_intra = jnp.exp(d_log - m_t)
        w_inter = jnp.exp(inter - m_t)
        qk = lax.dot_general(qb, kb, NT_DIMS, preferred_element_type=F32) * w_intra
        yield
        num = (jnp.dot(qk.astype(BF16), vb, preferred_element_type=F32)
               + w_inter * lax.dot_general(qb, c_ref[hh].astype(BF16), NT_DIMS, preferred_element_type=F32))
        den = jnp.sum(qk, axis=1, keepdims=True) + w_inter * jnp.sum(qf * n_ref[hh], axis=1, keepdims=True)
        hm = num / jnp.maximum(jnp.abs(den), jnp.exp(-m_t))
        yield

        w_log = b_last - b_col + ic_col
        m_new = jnp.maximum(b_last + m_prev, jnp.max(w_log, axis=0, keepdims=True))
        w_state = jnp.exp(w_log - m_new)
        decay = jnp.exp(b_last + m_prev - m_new)
        c_ref[hh] = decay * c_ref[hh] + lax.dot_general((w_state * vf).astype(BF16), kb, TN_DIMS,
                                                        preferred_element_type=F32)
        n_ref[hh] = decay * n_ref[hh] + jnp.sum(w_state * kf, axis=0, keepdims=True)
        m_ref[hh] = m_new
        yield

        hn = hm * lax.rsqrt(jnp.mean(hm * hm, axis=1, keepdims=True) + NORM_EPS) * ng_ref[:, cs]
        o_ref[:, cs] = (hn * _sigmoid(om_ref[:, cs].astype(F32))).astype(BF16)

    _interleave(*[head_chain(hh) for hh in range(ML_HPS)])


def _mlstm(proj, if_arr, gate_bias, conv_w, conv_b, ng, B, S):
    T = B * S
    nc = S // ML_CHUNK
    W = ML_HPS * ML_HD
    n_hp = ML_HEADS // ML_HPS
    assert COL_QKM % W == 0 and COL_VM % W == 0 and COL_OM % W == 0 and ML_HEADS % ML_HPS == 0
    cq, ck, cv, co = COL_QKM // W, COL_QKM // W + n_hp, COL_VM // W, COL_OM // W
    rows = lambda col0: pl.BlockSpec((ML_CHUNK, W), lambda b, h, c: (b * nc + c, col0 + h))
    return pl.pallas_call(
        _mlstm_kernel,
        grid=(B, n_hp, nc),
        in_specs=[
            pl.BlockSpec(memory_space=pltpu.SMEM),
            rows(cq), rows(ck), rows(cv), rows(co),
            pl.BlockSpec((1, ML_HPS, 2, ML_CHUNK), lambda b, h, c: (b, h, 0, c)),
            pl.BlockSpec((CONV_W, W), lambda b, h, c: (0, h)),
            pl.BlockSpec((CONV_W, W), lambda b, h, c: (0, n_hp + h)),
            pl.BlockSpec((1, W), lambda b, h, c: (0, h)),
            pl.BlockSpec((1, W), lambda b, h, c: (0, n_hp + h)),
            pl.BlockSpec((1, W), lambda b, h, c: (0, h)),
        ],
        out_specs=pl.BlockSpec((ML_CHUNK, W), lambda b, h, c: (b * nc + c, h)),
        out_shape=jax.ShapeDtypeStruct((T, ML_HEADS * ML_HD), BF16),
        scratch_shapes=[
            pltpu.VMEM((ML_HPS, ML_HD, ML_HD), F32), pltpu.VMEM((ML_HPS, 1, ML_HD), F32),
            pltpu.VMEM((ML_HPS, 1, 1), F32),
            pltpu.VMEM((ML_HPS, ML_CHUNK + 8, ML_HD), F32), pltpu.VMEM((ML_HPS, ML_CHUNK + 8, ML_HD), F32),
        ],
        compiler_params=_cparams(("parallel", "parallel", "arbitrary")),
        name="mlstm",
    )(gate_bias, proj, proj, proj, proj, if_arr, conv_w, conv_w, conv_b, conv_b, ng)


def _pack_bf16_pair(lo, hi):
    lo_b = pltpu.bitcast(lo.astype(BF16).astype(F32), U32)
    hi_b = pltpu.bitcast(hi.astype(BF16).astype(F32), U32)
    return (lo_b >> 16) | hi_b


def _unpack_bf16_pair(p):
    lo = pltpu.bitcast(p << 16, F32)
    hi = pltpu.bitcast(p & jnp.uint32(0xFFFF0000), F32)
    return lo, hi


def _row(ref, r):
    return ref.at[pl.ds(r, 1), :]


def _outproj_kernel(nsa_ref, ml_ref, w_ref, x_ref, g2_ref, wr_ref, br_ref, tril_ref,
                    x1_ref, xp_ref, rt_ref, cnt_ref, carry_ref, xa_ref, xb_ref):
    i = pl.program_id(0)

    @pl.when(i == 0)
    def _():
        carry_ref[...] = jnp.zeros_like(carry_ref)
        xb_ref[...] = jnp.zeros_like(xb_ref)

    half = D_MODEL // 2
    subs = [slice(sub * (TM_OUT // OUT_SUBS), (sub + 1) * (TM_OUT // OUT_SUBS)) for sub in range(OUT_SUBS)]

    def step(cur_ref, prev_ref):
        carry = carry_ref[...]
        carry_box = [carry]

        def chain():
            for rs in subs:
                yield from _outproj_rows(rs, prev_ref, carry_box, g2_ref, wr_ref, br_ref, tril_ref,
                                         x1_ref, xp_ref, rt_ref)
                yield

        pending = chain()
        for rs in subs:
            for c in range(D_MODEL // OUT_COLS):
                cs = slice(c * OUT_COLS, (c + 1) * OUT_COLS)
                cur_ref[rs, cs] = (x_ref[rs, cs]
                                   + jnp.dot(nsa_ref[rs, :], w_ref[0:half, cs], preferred_element_type=F32)
                                   + jnp.dot(ml_ref[rs, :], w_ref[half:, cs], preferred_element_type=F32))
                next(pending, None)
        for _ in pending:
            pass
        carry = jnp.where(i > 0, carry_box[0], carry)
        carry_ref[...] = carry
        cnt_ref[...] = jnp.broadcast_to(carry, cnt_ref.shape)

    @pl.when(i % 2 == 0)
    def _():
        step(xa_ref, xb_ref)

    @pl.when(i % 2 == 1)
    def _():
        step(xb_ref, xa_ref)


def _outproj_rows(rs, src_ref, carry_box, g2_ref, wr_ref, br_ref, tril_ref, x1_ref, xp_ref, rt_ref):
    half = D_MODEL // 2
    x1 = src_ref[rs, :]
    x1_ref[rs, :] = x1
    xn = x1 * lax.rsqrt(jnp.mean(x1 * x1, axis=-1, keepdims=True) + NORM_EPS) * g2_ref[...]
    xp_ref[rs, :] = _pack_bf16_pair(xn[:, :half], xn[:, half:])
    logits = jnp.dot(xn.astype(BF16), wr_ref[...], preferred_element_type=F32) + br_ref[...]
    yield

    tm = logits.shape[0]
    lane = lax.broadcasted_iota(jnp.int32, (tm, LANE), 1)
    lane_f = lane.astype(F32)
    big = float(LANE)
    gmask = lane < MOE_GROUPS
    gmax = jnp.max(jnp.where(gmask, logits, NEG_INF), axis=1, keepdims=True)
    ge = jnp.where(gmask, jnp.exp(logits - gmax), 0.0)
    gp = ge / jnp.sum(ge, axis=1, keepdims=True)
    g_w = jnp.max(gp, axis=1, keepdims=True)
    g_idx = jnp.min(jnp.where(gmask, jnp.where(gp == g_w, lane_f, big), big), axis=1, keepdims=True)
    yield
    grp_of_lane = ((lane - MOE_GROUPS) >> EPG_LOG2).astype(F32)
    emask = jnp.where(lane >= MOE_GROUPS, grp_of_lane, -1.0) == g_idx
    emax = jnp.max(jnp.where(emask, logits, NEG_INF), axis=1, keepdims=True)
    ee = jnp.where(emask, jnp.exp(logits - emax), 0.0)
    ep = jnp.where(emask, ee / jnp.sum(ee, axis=1, keepdims=True), -1.0)
    v1 = jnp.max(ep, axis=1, keepdims=True)
    i1 = jnp.min(jnp.where(ep == v1, lane_f, big), axis=1, keepdims=True)
    ep2 = jnp.where(lane_f == i1, -1.0, ep)
    v2 = jnp.max(ep2, axis=1, keepdims=True)
    i2 = jnp.min(jnp.where(ep2 == v2, lane_f, big), axis=1, keepdims=True)
    w0 = g_w * v1 / (v1 + v2)
    w1 = g_w * v2 / (v1 + v2)
    e0 = i1 - float(MOE_GROUPS)
    e1 = i2 - float(MOE_GROUPS)

    yield
    oh0 = jnp.where(lane_f == e0, 1.0, 0.0)
    oh1 = jnp.where(lane_f == e1, 1.0, 0.0)
    tril = tril_ref[0:tm, 0:tm]
    pre0 = jnp.dot(tril, oh0.astype(BF16), preferred_element_type=F32)
    pre1 = jnp.dot(tril, oh1.astype(BF16), preferred_element_type=F32)
    yield
    carry = carry_box[0]
    tot0 = pre0[tm - 1:tm, :]
    tot1 = pre1[tm - 1:tm, :]
    rank0 = jnp.sum(oh0 * (pre0 - 1.0 + carry), axis=1, keepdims=True)
    rank1 = jnp.sum(oh1 * (pre1 - 1.0 + carry + tot0), axis=1, keepdims=True)

    rt = jnp.where(lane == 0, e0, jnp.where(lane == 1, e1, jnp.where(lane == 2, w0, jnp.where(
        lane == 3, w1, jnp.where(lane == 4, rank0, jnp.where(lane == 5, rank1, 0.0))))))
    rt_ref[rs, :] = rt
    carry_box[0] = carry + tot0 + tot1


def _out_proj(nsa_o, ml_o, w_out, x2, g2, w_r, b_r, tril):
    T = x2.shape[0]
    half = D_MODEL // 2
    n_tiles = T // TM_OUT
    cur = lambda i: (jnp.minimum(i, n_tiles - 1), 0)
    prev = lambda i: (jnp.maximum(i - 1, 0), 0)
    return pl.pallas_call(
        _outproj_kernel,
        grid=(n_tiles + 1,),
        in_specs=[
            pl.BlockSpec((TM_OUT, half), cur),
            pl.BlockSpec((TM_OUT, half), cur),
            pl.BlockSpec((D_MODEL, D_MODEL), lambda i: (0, 0)),
            pl.BlockSpec((TM_OUT, D_MODEL), cur),
            pl.BlockSpec((1, D_MODEL), lambda i: (0, 0)),
            pl.BlockSpec((D_MODEL, LANE), lambda i: (0, 0)),
            pl.BlockSpec((1, LANE), lambda i: (0, 0)),
            pl.BlockSpec((TM_OUT, TM_OUT), lambda i: (0, 0)),
        ],
        out_specs=[
            pl.BlockSpec((TM_OUT, D_MODEL), prev),
            pl.BlockSpec((TM_OUT, half), prev),
            pl.BlockSpec((TM_OUT, LANE), prev),
            pl.BlockSpec((8, LANE), lambda i: (0, 0)),
        ],
        out_shape=[
            jax.ShapeDtypeStruct((T, D_MODEL), F32),
            jax.ShapeDtypeStruct((T, half), U32),
            jax.ShapeDtypeStruct((T, LANE), F32),
            jax.ShapeDtypeStruct((8, LANE), F32),
        ],
        scratch_shapes=[pltpu.VMEM((1, LANE), F32), pltpu.VMEM((TM_OUT, D_MODEL), F32),
                        pltpu.VMEM((TM_OUT, D_MODEL), F32)],
        compiler_params=_cparams(("arbitrary",)),
        name="out_proj",
    )(nsa_o, ml_o, w_out, x2, g2, w_r, b_r, tril)


TD = 256
DISPATCH_BUFS = 3
DISPATCH_UNROLL = 8


def _dispatch_kernel(dest_ref, zblk_ref, meta_ref, xp_hbm, xs_hbm, zbuf_ref, zsem_ref, tbuf_ref, lsem_ref,
                     rsem_ref):
    n_assign = dest_ref.shape[0]
    nb = xs_hbm.shape[0] // BM
    n_used = meta_ref[0]
    zbuf_ref[...] = jnp.zeros_like(zbuf_ref)

    def zero_block(blk):
        return pltpu.make_async_copy(zbuf_ref, xs_hbm.at[pl.ds(blk * BM, BM), :], zsem_ref.at[0])

    def for_zero_blocks(fn):
        for e in range(N_EXPERTS):
            @pl.when(zblk_ref[e] >= 0)
            def _(e=e):
                fn(zero_block(zblk_ref[e]))

        def body(blk, _):
            fn(zero_block(blk))
            return 0
        lax.fori_loop(n_used, nb, body, 0)

    for_zero_blocks(lambda c: c.start())
    for_zero_blocks(lambda c: c.wait())

    n_tiles = n_assign // (2 * TD)

    def load(tile, slot):
        return pltpu.make_async_copy(xp_hbm.at[pl.ds(tile * TD, TD), :], tbuf_ref.at[slot], lsem_ref.at[slot])

    def wait_scatter(slot):
        for _ in range(2):
            pltpu.make_async_copy(tbuf_ref.at[slot], tbuf_ref.at[slot], rsem_ref.at[slot]).wait()

    load(0, 0).start()

    def tile_step(i, _):
        slot = i % DISPATCH_BUFS
        nslot = (i + 1) % DISPATCH_BUFS

        @pl.when(i + 1 < n_tiles)
        def _():
            @pl.when(i + 1 >= DISPATCH_BUFS)
            def _():
                wait_scatter(nslot)
            load(i + 1, nslot).start()

        load(i, slot).wait()

        def body(j, _):
            for u in range(DISPATCH_UNROLL):
                r = j * DISPATCH_UNROLL + u
                for k in range(2):
                    d = dest_ref[(i * TD + r) * 2 + k]
                    pltpu.make_async_copy(_row(tbuf_ref.at[slot], r), _row(xs_hbm, d),
                                          rsem_ref.at[slot]).start(priority=k)
            return 0
        lax.fori_loop(0, TD // DISPATCH_UNROLL, body, 0)
        return 0

    lax.fori_loop(0, n_tiles, tile_step, 0)
    for back in range(min(DISPATCH_BUFS, n_tiles)):
        wait_scatter((n_tiles - 1 - back) % DISPATCH_BUFS)


def _dispatch(dest, zblk, meta, xp, n_rows):
    half = D_MODEL // 2
    assert dest.shape[0] % (2 * TD) == 0 and dest.shape[0] // (2 * TD) >= DISPATCH_BUFS
    grid_spec = pltpu.PrefetchScalarGridSpec(
        num_scalar_prefetch=3,
        grid=(1,),
        in_specs=[pl.BlockSpec(memory_space=pl.ANY)],
        out_specs=pl.BlockSpec(memory_space=pl.ANY),
        scratch_shapes=[
            pltpu.VMEM((BM, half), U32),
            pltpu.SemaphoreType.DMA((1,)),
            pltpu.VMEM((DISPATCH_BUFS, TD, half), U32),
            pltpu.SemaphoreType.DMA((DISPATCH_BUFS,)),
            pltpu.SemaphoreType.DMA((DISPATCH_BUFS,)),
        ],
    )
    return pl.pallas_call(
        _dispatch_kernel,
        grid_spec=grid_spec,
        out_shape=jax.ShapeDtypeStruct((n_rows, half), U32),
        compiler_params=_cparams(("arbitrary",)),
        name="dispatch",
    )(dest, zblk, meta, xp)


def _expert_kernel(be_ref, nxt_ref, par_ref, meta_ref, x_ref, wg_hbm, wu_hbm, wd_hbm, y_ref,
                   wsg_ref, wsu_ref, wsd_ref, wsem_ref, wgb_ref, wub_ref, wdb_ref):
    i = pl.program_id(0)
    n_used = meta_ref[0]
    half = D_MODEL // 2

    def weight_copies(e, p):
        return (pltpu.make_async_copy(wg_hbm.at[e], wsg_ref.at[p], wsem_ref.at[p, 0]),
                pltpu.make_async_copy(wu_hbm.at[e], wsu_ref.at[p], wsem_ref.at[p, 1]),
                pltpu.make_async_copy(wd_hbm.at[e], wsd_ref.at[p], wsem_ref.at[p, 2]))

    def succ(e):
        return jnp.where(e >= 0, nxt_ref[jnp.maximum(e, 0)], -1)

    def start_weights(e, p):
        @pl.when(e >= 0)
        def _():
            for c in weight_copies(e, p):
                c.start(priority=1)

    def wait_staged(p):
        for c in weight_copies(0, p):
            c.wait()

    def cast_staged(p):
        wgb_ref[p] = wsg_ref[p].astype(BF16)
        wub_ref[p] = wsu_ref[p].astype(BF16)
        wdb_ref[p] = wsd_ref[p].astype(BF16)

    @pl.when(i == 0)
    def _():
        e0 = be_ref[0]
        start_weights(e0, 0)
        start_weights(succ(e0), 1)
        wait_staged(0)
        cast_staged(0)
        start_weights(succ(succ(e0)), 0)

    def step(p, cast_next):
        if cast_next:
            wait_staged(1 - p)
        lo, hi = _unpack_bf16_pair(x_ref[...])
        xl = lo.astype(BF16)
        xh = hi.astype(BF16)
        gt = (jnp.dot(xl, wgb_ref[p, 0:half, :], preferred_element_type=F32)
              + jnp.dot(xh, wgb_ref[p, half:, :], preferred_element_type=F32))
        up = (jnp.dot(xl, wub_ref[p, 0:half, :], preferred_element_type=F32)
              + jnp.dot(xh, wub_ref[p, half:, :], preferred_element_type=F32))
        hb = (_silu(gt) * up).astype(BF16)
        y = jnp.dot(hb, wdb_ref[p], preferred_element_type=F32)
        y_ref[...] = _pack_bf16_pair(y[:, :half], y[:, half:])
        if cast_next:
            cast_staged(1 - p)
            start_weights(succ(succ(succ(be_ref[i]))), 1 - p)

    e = be_ref[jnp.minimum(i, n_used - 1)]
    ends_expert = (i + 1 < n_used) & (be_ref[jnp.minimum(i + 1, n_used - 1)] != e)
    for p in range(2):
        for cast_next in (False, True):
            @pl.when((i < n_used) & (par_ref[i] == p)
                     & (ends_expert if cast_next else jnp.logical_not(ends_expert)))
            def _(p=p, cast_next=cast_next):
                step(p, cast_next)

    @pl.when(i >= n_used)
    def _():
        y_ref[...] = jnp.zeros_like(y_ref)


def _experts(block_expert, next_expert, block_parity, meta, xs, w_gate, w_up, w_down):
    half = D_MODEL // 2
    n_rows = xs.shape[0]
    nb = n_rows // BM
    grid_spec = pltpu.PrefetchScalarGridSpec(
        num_scalar_prefetch=4,
        grid=(nb,),
        in_specs=[
            pl.BlockSpec((BM, half), lambda i, be, nxt, par, meta: (jnp.minimum(i, meta[0] - 1), 0)),
            pl.BlockSpec(memory_space=pl.ANY), pl.BlockSpec(memory_space=pl.ANY),
            pl.BlockSpec(memory_space=pl.ANY),
        ],
        out_specs=pl.BlockSpec((BM, half), lambda i, be, nxt, par, meta: (i, 0)),
        scratch_shapes=[
            pltpu.VMEM((2, D_MODEL, D_EXPERT), F32),
            pltpu.VMEM((2, D_MODEL, D_EXPERT), F32),
            pltpu.VMEM((2, D_EXPERT, D_MODEL), F32),
            pltpu.SemaphoreType.DMA((2, 3)),
            pltpu.VMEM((2, D_MODEL, D_EXPERT), BF16),
            pltpu.VMEM((2, D_MODEL, D_EXPERT), BF16),
            pltpu.VMEM((2, D_EXPERT, D_MODEL), BF16),
        ],
    )
    return pl.pallas_call(
        _expert_kernel,
        grid_spec=grid_spec,
        out_shape=jax.ShapeDtypeStruct((n_rows, half), U32),
        compiler_params=_cparams(("arbitrary",)),
        name="experts",
    )(block_expert, next_expert, block_parity, meta, xs, w_gate, w_up, w_down)


def _combine_kernel(dest_ref, y_hbm, x1_ref, rt_ref, fg_ref, o_ref, ya_ref, yb_ref, sem_ref):
    i = pl.program_id(0)
    nt = pl.num_programs(0)
    half = D_MODEL // 2
    bufs = (ya_ref, yb_ref)

    def row_copy(tile, slot, r_tile, s, k):
        r = r_tile * 8 + s
        d = dest_ref[(tile * TC + r) * 2 + k]
        return pltpu.make_async_copy(_row(y_hbm, d), _row(bufs[slot].at[k], r), sem_ref.at[slot])

    def wait_rows(slot):
        pltpu.make_async_copy(bufs[slot], bufs[slot], sem_ref.at[slot]).wait()

    def step(slot):
        wait_rows(slot)
        for r in range(TC):
            for k in range(2):
                row_copy(i + 1, 1 - slot, r // 8, r % 8, k).start(priority=k)
        rt = rt_ref[...]
        w0 = rt[:, 2:3]
        w1 = rt[:, 3:4]
        lo0, hi0 = _unpack_bf16_pair(bufs[slot][0])
        lo1, hi1 = _unpack_bf16_pair(bufs[slot][1])
        xl = x1_ref[:, :half] + (w0 * lo0 + w1 * lo1)
        xh = x1_ref[:, half:] + (w0 * hi0 + w1 * hi1)
        ms = (jnp.sum(xl * xl, axis=1, keepdims=True) + jnp.sum(xh * xh, axis=1, keepdims=True)) / float(D_MODEL)
        inv = lax.rsqrt(ms + NORM_EPS)
        o_ref[:, :half] = xl * inv * fg_ref[:, :half]
        o_ref[:, half:] = xh * inv * fg_ref[:, half:]

        @pl.when(i == nt - 1)
        def _():
            wait_rows(1 - slot)

    @pl.when(i == 0)
    def _():
        def body(rt, _):
            for s in range(8):
                for k in range(2):
                    row_copy(0, 0, rt, s, k).start(priority=k)
            return 0
        lax.fori_loop(0, TC // 8, body, 0)

    @pl.when(i % 2 == 0)
    def _():
        step(0)

    @pl.when(i % 2 == 1)
    def _():
        step(1)


def _combine(dest, ys, x1, route, fg):
    T = x1.shape[0]
    half = D_MODEL // 2
    grid_spec = pltpu.PrefetchScalarGridSpec(
        num_scalar_prefetch=1,
        grid=(T // TC,),
        in_specs=[
            pl.BlockSpec(memory_space=pl.ANY),
            pl.BlockSpec((TC, D_MODEL), lambda i, d: (i, 0)),
            pl.BlockSpec((TC, LANE), lambda i, d: (i, 0)),
            pl.BlockSpec((1, D_MODEL), lambda i, d: (0, 0)),
        ],
        out_specs=pl.BlockSpec((TC, D_MODEL), lambda i, d: (i, 0)),
        scratch_shapes=[
            pltpu.VMEM((2, TC, half), U32),
            pltpu.VMEM((2, TC, half), U32),
            pltpu.SemaphoreType.DMA((2,)),
        ],
    )
    return pl.pallas_call(
        _combine_kernel,
        grid_spec=grid_spec,
        out_shape=jax.ShapeDtypeStruct((T, D_MODEL), F32),
        compiler_params=_cparams(("arbitrary",)),
        name="combine",
    )(dest, ys, x1, route, fg)


def _cover_matrix(S):
    n_blk = S // CMP_STRIDE
    n_sel = S // SEL_BLOCK
    cs = np.arange(n_blk) * CMP_STRIDE
    ss = np.arange(n_sel) * SEL_BLOCK
    shared = np.minimum(cs[:, None] + CMP_LEN, ss[None, :] + SEL_BLOCK) - np.maximum(cs[:, None], ss[None, :])
    return (np.clip(shared, 0, None) / CMP_LEN).T.astype(np.float32)


def _block_mask_matrix(S):
    n_sel = S // SEL_BLOCK
    assert n_sel <= LANE
    e = np.zeros((S, LANE), np.float32)
    e[np.arange(S), np.arange(S) // SEL_BLOCK] = NEG_INF
    return e


def _inv_freq_row():
    inv = np.power(np.float32(ROPE_THETA), -np.arange(ROPE_HALF, dtype=np.float32) * 2.0 / ROPE_DIM)
    return np.tile(inv, ROPE_SLOTS).reshape(1, LANE).astype(np.float32)


def _packed_positions(positions, T):
    rows = TS_ROPE // ROPE_SLOTS
    p = positions.reshape(T // TS_ROPE, ROPE_SLOTS, rows).transpose(0, 2, 1).astype(F32)
    return jnp.repeat(p, ROPE_HALF, axis=2).reshape(T // ROPE_SLOTS, LANE)


def _layer(x2, positions, B, S, norm1_g, w_in, cmp_pe_k, cmp_pe_v, cmp_wk1, cmp_wk2, cmp_wv1, cmp_wv2,
           nsa_norm_g, conv_w, conv_b, b_igate, b_fgate, mlstm_norm_g, w_out, norm2_g,
           w_group, b_group, w_router, b_router, w_exp_gate, w_exp_up, w_exp_down, out_norm_g):
    T = B * S
    w_main, w_small = _weight_prep(jnp.transpose(w_in[0]))
    posb = _packed_positions(positions, T)
    invf = jnp.asarray(_inv_freq_row())

    proj, small = _in_proj(x2, norm1_g.reshape(1, -1), w_main, w_small)
    k_r = _rope(posb, invf, proj)
    kc, vc = _compress(k_r, proj, B, S, cmp_pe_k, cmp_pe_v, cmp_wk1.astype(BF16), cmp_wk2.astype(BF16),
                       cmp_wv1.astype(BF16), cmp_wv2.astype(BF16))
    eye = np.eye(S // SEL_BLOCK, LANE, dtype=np.float32)
    nsa_o = _nsa(posb, invf, k_r, proj, small, kc, vc, jnp.asarray(_cover_matrix(S), BF16),
                 jnp.asarray(_block_mask_matrix(S), BF16), jnp.asarray(eye, BF16),
                 nsa_norm_g.reshape(1, -1), B, S)

    if_arr = small[:, 24:32].reshape(B, S, 2, ML_HEADS).transpose(0, 3, 2, 1)
    gate_bias = jnp.stack([b_igate, b_fgate]).astype(F32)
    ml_o = _mlstm(proj, if_arr, gate_bias, conv_w, conv_b.reshape(1, -1), mlstm_norm_g.reshape(1, -1), B, S)

    w_r = jnp.concatenate([w_group, w_router, jnp.zeros((D_MODEL, LANE - MOE_GROUPS - N_EXPERTS), F32)],
                          axis=1).astype(BF16)
    b_r = jnp.concatenate([b_group, b_router, jnp.zeros((LANE - MOE_GROUPS - N_EXPERTS,), F32)]).reshape(1, LANE)
    tril = jnp.asarray(np.tril(np.ones((TM_OUT, TM_OUT), np.float32)), BF16)
    x1, xp, route, cnt = _out_proj(nsa_o, ml_o, w_out.astype(BF16), x2, norm2_g.reshape(1, -1), w_r, b_r, tril)

    n_rows = T * 2 + N_EXPERTS * BM
    counts = cnt[0, :N_EXPERTS].astype(jnp.int32)
    pcounts = (counts + BM - 1) // BM * BM
    pends = jnp.cumsum(pcounts)
    pstarts = pends - pcounts
    eid = route[:, 0:2].astype(jnp.int32)
    onehot = (eid[..., None] == jnp.arange(N_EXPERTS, dtype=jnp.int32)).astype(F32)
    start_blk = jnp.einsum('tke,e->tk', onehot, (pstarts // BM).astype(F32)).astype(jnp.int32)
    dest = (start_blk * BM + route[:, 4:6].astype(jnp.int32)).reshape(T * 2)
    block_row0 = jnp.arange(n_rows // BM, dtype=jnp.int32) * BM
    block_expert = jnp.minimum(jnp.sum((pends[None, :] <= block_row0[:, None]).astype(jnp.int32), axis=1),
                               N_EXPERTS - 1).astype(jnp.int32)
    present = jnp.where(counts > 0, jnp.arange(N_EXPERTS, dtype=jnp.int32), N_EXPERTS)
    later = jnp.concatenate([lax.cummin(present[::-1])[::-1][1:], jnp.full((1,), N_EXPERTS, jnp.int32)])
    next_expert = jnp.where(later < N_EXPERTS, later, -1).astype(jnp.int32)
    meta = (pends[-1:] // BM).astype(jnp.int32)

    last_block = jnp.where(counts > 0, pends // BM - 1, -1).astype(jnp.int32)

    xs = _dispatch(dest, last_block, meta, xp, n_rows)
    ordinal = jnp.cumsum((counts > 0).astype(jnp.int32)) - 1
    block_parity = (ordinal[block_expert] % 2).astype(jnp.int32)
    ys = _experts(block_expert, next_expert, block_parity, meta, xs, w_exp_gate, w_exp_up, w_exp_down)
    dest_pad = jnp.concatenate([dest, jnp.zeros((2 * TC,), jnp.int32)])
    return _combine(dest_pad, ys, x1, route, out_norm_g.reshape(1, -1))


def kernel(x, positions, norm1_g, w_in, cmp_pe_k, cmp_pe_v, cmp_wk1, cmp_wk2, cmp_wv1, cmp_wv2, nsa_norm_g,
           conv_w, conv_b, b_igate, b_fgate, mlstm_norm_g, w_out, norm2_g, w_group, b_group, w_router,
           b_router, w_exp_gate, w_exp_up, w_exp_down, final_norm_g):
    B, S, D = x.shape
    assert D == D_MODEL and norm1_g.shape[0] == 1, "single-layer, D_MODEL-wide configuration only"
    assert S % ML_CHUNK == 0 and S % SEL_BUCKET == 0 and S >= WIN_KEYS and (B * S) % TM_IN == 0
    out = _layer(x.reshape(B * S, D), positions, B, S, norm1_g[0], w_in, cmp_pe_k[0], cmp_pe_v[0],
                 cmp_wk1[0], cmp_wk2[0], cmp_wv1[0], cmp_wv2[0], nsa_norm_g[0], conv_w[0], conv_b[0],
                 b_igate[0], b_fgate[0], mlstm_norm_g[0], w_out[0], norm2_g[0], w_group[0], b_group[0],
                 w_router[0], b_router[0], w_exp_gate[0], w_exp_up[0], w_exp_down[0], final_norm_g)
    return out.reshape(B, S, D)
```

```python
import functools

import numpy as np
import jax
import jax.numpy as jnp
from jax import lax
from jax.experimental import pallas as pl
from jax.experimental.pallas import tpu as pltpu

F32 = jnp.float32
BF16 = jnp.bfloat16
U32 = jnp.uint32

D_MODEL = 2048
NSA_HEADS = 8
NSA_GROUPS = 2
NSA_J = NSA_HEADS // NSA_GROUPS
HD = 128
CMP_LEN = 32
CMP_STRIDE = 16
SEL_BLOCK = 64
SEL_TOPK = 8
WINDOW = 512
ROPE_THETA = 500000.0
ROPE_DIM = 32
ROPE_HALF = 16
ML_HEADS = 4
ML_HD = 256
CONV_W = 4
MOE_GROUPS = 8
EPG = 8
N_EXPERTS = 64
D_EXPERT = 512
SEL_BLOCK_LOG2 = 6
EPG_LOG2 = 3
assert (1 << SEL_BLOCK_LOG2) == SEL_BLOCK and (1 << EPG_LOG2) == EPG
NORM_EPS = 1e-6
NEG_INF = -1e30
FORCE_SCORE = 1000.0

COL_Q = 0
COL_KV = 1024
COL_QKM = 2560
COL_VM = 4608
COL_OM = 5632
N_MAIN = 6656

LANE = 128
VMEM_LIMIT = 56 * 1024 * 1024

TM_IN = 1024
TN_IN = 1664
TS_ROPE = 256
TQ = 256
SEL_BUCKET = 256
WIN_KEYS = WINDOW + TQ
ML_CHUNK = 256
TM_OUT = 512
OUT_SUBS = 2
OUT_COLS = 512
BM = 256
TC = 256

NT_DIMS = (((1,), (1,)), ((), ()))
TN_DIMS = (((0,), (0,)), ((), ()))


def _cparams(sem):
    return pltpu.CompilerParams(dimension_semantics=sem, vmem_limit_bytes=VMEM_LIMIT)


def _sigmoid(x):
    return 0.5 * jnp.tanh(0.5 * x) + 0.5


def _interleave(*chains):
    live = list(chains)
    while live:
        for g in list(live):
            try:
                next(g)
            except StopIteration:
                live.remove(g)


def _silu(x):
    h = 0.5 * x
    return h + h * jnp.tanh(h)


W_GATES = 2560
W_QKM = 2584
W_IGATE = 6680
W_END = 6688
TR_PREP = 512
assert W_GATES % TR_PREP == 0 and N_MAIN % TR_PREP == 0 and W_IGATE - W_QKM == N_MAIN - W_GATES


def _wprep_kernel(wt_hbm, wm_ref, ws_ref, wbuf_ref, sbuf_ref, sem_ref, ssem_ref):
    i = pl.program_id(0)
    n_gate = W_QKM - W_GATES
    n_if = W_END - W_IGATE

    def load(blk, slot):
        start = pl.multiple_of(blk * TR_PREP + jnp.where(blk * TR_PREP >= W_GATES, n_gate, 0), 8)
        return pltpu.make_async_copy(wt_hbm.at[pl.ds(start, TR_PREP), :], wbuf_ref.at[slot], sem_ref.at[slot])

    def small_loads():
        return (pltpu.make_async_copy(wt_hbm.at[pl.ds(W_GATES, n_gate), :], sbuf_ref.at[pl.ds(0, n_gate), :],
                                      ssem_ref.at[0]),
                pltpu.make_async_copy(wt_hbm.at[pl.ds(W_IGATE, n_if), :], sbuf_ref.at[pl.ds(n_gate, n_if), :],
                                      ssem_ref.at[1]))

    @pl.when(i == 0)
    def _():
        load(0, 0).start()
        for c in small_loads():
            c.start()
        for c in small_loads():
            c.wait()
        ws_ref[...] = jnp.zeros_like(ws_ref)
        ws_ref[0:n_gate + n_if, :] = sbuf_ref[...].astype(BF16)

    @pl.when(i + 1 < pl.num_programs(0))
    def _():
        load(i + 1, (i + 1) % 2).start()

    load(i, i % 2).wait()
    wm_ref[...] = wbuf_ref[i % 2].astype(BF16)


def _weight_prep(wt):
    assert wt.shape == (W_END, D_MODEL)
    n_small = (W_QKM - W_GATES) + (W_END - W_IGATE)
    return pl.pallas_call(
        _wprep_kernel,
        grid=(N_MAIN // TR_PREP,),
        in_specs=[pl.BlockSpec(memory_space=pl.ANY)],
        out_specs=[pl.BlockSpec((TR_PREP, D_MODEL), lambda i: (i, 0)),
                   pl.BlockSpec((LANE, D_MODEL), lambda i: (0, 0))],
        out_shape=[jax.ShapeDtypeStruct((N_MAIN, D_MODEL), BF16),
                   jax.ShapeDtypeStruct((LANE, D_MODEL), BF16)],
        scratch_shapes=[pltpu.VMEM((2, TR_PREP, D_MODEL), F32), pltpu.VMEM((n_small, D_MODEL), F32),
                        pltpu.SemaphoreType.DMA((2,)), pltpu.SemaphoreType.DMA((2,))],
        compiler_params=_cparams(("arbitrary",)),
        name="weight_prep",
    )(wt)


def _inproj_kernel(x_ref, g_ref, w_ref, ws_ref, o_ref, os_ref, h_ref):
    @pl.when(pl.program_id(1) == 0)
    def _():
        x = x_ref[...]
        ms = jnp.mean(x * x, axis=-1, keepdims=True)
        h_ref[...] = (x * lax.rsqrt(ms + NORM_EPS) * g_ref[...]).astype(BF16)
        os_ref[...] = lax.dot_general(h_ref[...], ws_ref[...], NT_DIMS, preferred_element_type=F32)

    o_ref[...] = lax.dot_general(h_ref[...], w_ref[...], NT_DIMS,
                                 preferred_element_type=F32).astype(o_ref.dtype)


def _in_proj(x2, g1, w_main, w_small):
    T = x2.shape[0]
    return pl.pallas_call(
        _inproj_kernel,
        grid=(T // TM_IN, N_MAIN // TN_IN),
        in_specs=[
            pl.BlockSpec((TM_IN, D_MODEL), lambda m, n: (m, 0)),
            pl.BlockSpec((1, D_MODEL), lambda m, n: (0, 0)),
            pl.BlockSpec((TN_IN, D_MODEL), lambda m, n: (n, 0)),
            pl.BlockSpec((LANE, D_MODEL), lambda m, n: (0, 0)),
        ],
        out_specs=[
            pl.BlockSpec((TM_IN, TN_IN), lambda m, n: (m, n)),
            pl.BlockSpec((TM_IN, LANE), lambda m, n: (m, 0)),
        ],
        out_shape=[
            jax.ShapeDtypeStruct((T, N_MAIN), BF16),
            jax.ShapeDtypeStruct((T, LANE), F32),
        ],
        scratch_shapes=[pltpu.VMEM((TM_IN, D_MODEL), BF16)],
        compiler_params=_cparams(("parallel", "arbitrary")),
        name="in_proj",
    )(x2, g1, w_main, w_small)


ROPE_SLOTS = LANE // ROPE_HALF


def _rope_slots(pos_ref, invf_ref):
    ang = pos_ref[...] * invf_ref[...]
    c = jnp.cos(ang)
    s = jnp.sin(ang)
    rows = ang.shape[0]
    lane = lax.broadcasted_iota(jnp.int32, ang.shape, 1)
    lo = lane < ROPE_HALF
    mid = lane < ROPE_DIM

    def lanes_from(x, src):
        shift = (-src) % LANE
        return x if shift == 0 else pltpu.roll(x, shift, 1)

    for slot in range(ROPE_SLOTS):
        src = slot * ROPE_HALF
        cf = jnp.where(lo, lanes_from(c, src), jnp.where(mid, lanes_from(c, src - ROPE_HALF), 1.0))
        sa = jnp.where(lo, -lanes_from(s, src), 0.0)
        sb = jnp.where(lo, 0.0, jnp.where(mid, lanes_from(s, src - ROPE_HALF), 0.0))

        def rope(x, cf=cf, sa=sa, sb=sb):
            return x * cf + pltpu.roll(x, LANE - ROPE_HALF, 1) * sa + pltpu.roll(x, ROPE_HALF, 1) * sb

        yield slice(slot * rows, (slot + 1) * rows), rope


def _rope_kernel(pos_ref, invf_ref, kc_ref, ks_ref, kw_ref, ko_ref):
    for rs, rope in _rope_slots(pos_ref, invf_ref):
        for i, r in enumerate((kc_ref, ks_ref, kw_ref)):
            for g in range(NSA_GROUPS):
                sl = slice(g * HD, (g + 1) * HD)
                so = slice(i * 2 * HD + g * HD, i * 2 * HD + (g + 1) * HD)
                ko_ref[rs, so] = rope(r[rs, sl].astype(F32)).astype(BF16)


def _rope(posb, invf, proj):
    T = proj.shape[0]
    kvb = COL_KV // 256
    return pl.pallas_call(
        _rope_kernel,
        grid=(T // TS_ROPE,),
        in_specs=[
            pl.BlockSpec((TS_ROPE // ROPE_SLOTS, LANE), lambda i: (i, 0)),
            pl.BlockSpec((1, LANE), lambda i: (0, 0)),
            pl.BlockSpec((TS_ROPE, 256), lambda i: (i, kvb + 0)),
            pl.BlockSpec((TS_ROPE, 256), lambda i: (i, kvb + 2)),
            pl.BlockSpec((TS_ROPE, 256), lambda i: (i, kvb + 4)),
        ],
        out_specs=pl.BlockSpec((TS_ROPE, 768), lambda i: (i, 0)),
        out_shape=jax.ShapeDtypeStruct((T, 768), BF16),
        compiler_params=_cparams(("parallel",)),
        name="rope",
    )(posb, invf, proj, proj, proj)


def _compress_kernel(k_ref, v_ref, pek_ref, pev_ref, w1k_ref, w2k_ref, w1v_ref, w2v_ref,
                     kc_ref, vc_ref, xs_ref):
    S = k_ref.shape[0]
    n_blk = S // CMP_STRIDE
    for src, pe, w1, w2, dst in ((k_ref, pek_ref, w1k_ref, w2k_ref, kc_ref),
                                 (v_ref, pev_ref, w1v_ref, w2v_ref, vc_ref)):
        xs_ref[0:S, :] = src[...].astype(F32)
        xs_ref[S:S + CMP_LEN, :] = jnp.zeros((CMP_LEN, HD), F32)
        acc = jnp.zeros((n_blk, HD), F32)
        for l in range(CMP_LEN):
            a = xs_ref[pl.ds(l, n_blk, stride=CMP_STRIDE), :] + pe[l:l + 1, :]
            acc = acc + jnp.dot(a.astype(BF16), w1[l * HD:(l + 1) * HD, :],
                                preferred_element_type=F32)
        hid = _silu(acc)
        out = jnp.dot(hid.astype(BF16), w2[...], preferred_element_type=F32)
        dst[0, 0] = out.astype(BF16)


def _compress(k_r, proj, B, S, pek, pev, w1k, w2k, w1v, w2v):
    n_blk = S // CMP_STRIDE
    vcol = (COL_KV + 256) // HD
    full = lambda shape: pl.BlockSpec(shape, lambda b, g: tuple(0 for _ in shape))
    return pl.pallas_call(
        _compress_kernel,
        grid=(B, NSA_GROUPS),
        in_specs=[
            pl.BlockSpec((S, HD), lambda b, g: (b, g)),
            pl.BlockSpec((S, HD), lambda b, g: (b, vcol + g)),
            full((CMP_LEN, HD)), full((CMP_LEN, HD)),
            full((CMP_LEN * HD, HD)), full((HD, HD)),
            full((CMP_LEN * HD, HD)), full((HD, HD)),
        ],
        out_specs=[
            pl.BlockSpec((1, 1, n_blk, HD), lambda b, g: (b, g, 0, 0)),
            pl.BlockSpec((1, 1, n_blk, HD), lambda b, g: (b, g, 0, 0)),
        ],
        out_shape=[
            jax.ShapeDtypeStruct((B, NSA_GROUPS, n_blk, HD), BF16),
            jax.ShapeDtypeStruct((B, NSA_GROUPS, n_blk, HD), BF16),
        ],
        scratch_shapes=[pltpu.VMEM((S + CMP_LEN, HD), F32)],
        compiler_params=_cparams(("parallel", "parallel")),
        name="compress",
    )(k_r, proj, pek, pev, w1k, w2k, w1v, w2v)


def _nsa_kernel(pos_ref, invf_ref, q_ref, kc_ref, vc_ref, ks_ref, vs_ref, kw_ref, vw_ref, sm_ref, covt_ref,
                eneg_ref, eye_ref, ng_ref, o_ref, obuf_ref, owin_ref, osel_ref, qs_ref):
    qi = pl.program_id(1)
    q0 = qi * TQ
    R = NSA_J * TQ
    n_cmp = kc_ref.shape[2]
    n_sel = ks_ref.shape[0] // SEL_BLOCK
    n_win = WIN_KEYS // TQ

    def row_t(shape):
        r = lax.broadcasted_iota(jnp.int32, shape, 0)
        return q0 + (r & (TQ - 1))

    scale = HD ** -0.5
    for rs, rope in _rope_slots(pos_ref, invf_ref):
        for h in range(NSA_HEADS):
            g, j = divmod(h, NSA_J)
            dst = slice(j * TQ + rs.start, j * TQ + rs.stop)
            qs_ref[g, dst, :] = (rope(q_ref[rs, h * HD:(h + 1) * HD].astype(F32)) * scale).astype(BF16)
    qgs = [qs_ref[g] for g in range(NSA_GROUPS)]

    def with_ones(v):
        lane = lax.broadcasted_iota(jnp.int32, v.shape, 1)
        return jnp.concatenate([v, jnp.where(lane == 0, 1.0, 0.0).astype(BF16)], axis=1)

    def window_chain(g):
        w0 = pl.multiple_of(jnp.maximum(q0 - WINDOW, 0), TQ)
        kt = kw_ref[pl.ds(w0, WIN_KEYS), g * HD:(g + 1) * HD]
        vt = with_ones(vw_ref[pl.ds(w0, WIN_KEYS), g * HD:(g + 1) * HD])
        sc = lax.dot_general(qgs[g], kt, NT_DIMS, preferred_element_type=F32)
        yield
        diff = row_t((R, WIN_KEYS)) - (w0 + lax.broadcasted_iota(jnp.int32, (R, WIN_KEYS), 1))
        keep = jnp.where(diff >= 0, diff, WINDOW) < WINDOW
        sc = jnp.where(keep, sc, NEG_INF)
        mw = jnp.max(sc, axis=1, keepdims=True)
        yield
        pw = jnp.exp((sc - mw).astype(BF16))
        acc = jnp.dot(pw, vt, preferred_element_type=F32)
        yield
        owin_ref[g] = acc[:, 0:HD] / acc[:, HD:HD + 1]

    gates = _sigmoid(sm_ref[...])
    o_cmps = [None] * NSA_GROUPS
    qps = [None] * NSA_GROUPS

    def cmp_chain(g):
        qg = qgs[g]

        s = lax.dot_general(qg, kc_ref[0, g], NT_DIMS, preferred_element_type=F32)
        yield
        n_lane = lax.broadcasted_iota(jnp.int32, (R, n_cmp), 1)
        cmask = (n_lane * CMP_STRIDE + (CMP_LEN - 1)) <= row_t((R, n_cmp))
        s = jnp.where(cmask, s, NEG_INF)
        m = jnp.max(s, axis=1, keepdims=True)
        e = jnp.where(cmask, jnp.exp(s - m), 0.0)
        l = jnp.sum(e, axis=1, keepdims=True)
        p = (e / jnp.where(l > 0.0, l, 1.0)).astype(BF16)
        yield
        o_cmps[g] = jnp.dot(p, vc_ref[0, g], preferred_element_type=F32)
        impr = lax.dot_general(covt_ref[...], p, NT_DIMS, preferred_element_type=F32)
        imp = impr[:, 0:TQ]
        for j in range(1, NSA_J):
            imp = imp + impr[:, j * TQ:(j + 1) * TQ]
        yield

        m_sub = lax.broadcasted_iota(jnp.int32, (n_sel, TQ), 0)
        jt = (q0 + lax.broadcasted_iota(jnp.int32, (n_sel, TQ), 1)) >> SEL_BLOCK_LOG2
        forced = jnp.where(m_sub == 0, FORCE_SCORE,
                           jnp.where(m_sub == jt, FORCE_SCORE,
                                     jnp.where(m_sub == jt - 1, FORCE_SCORE, 0.0)))
        score = jnp.where(m_sub <= jt, imp + forced, -1.0)
        rank = jnp.zeros((n_sel, TQ), F32)
        for mp in range(n_sel):
            row = score[mp:mp + 1, :]
            ge = jnp.where(row >= score, 1.0, 0.0)
            gt = jnp.where(row > score, 1.0, 0.0)
            rank = rank + jnp.where(m_sub > mp, ge, gt)
        notsel_t = jnp.where(rank < float(min(SEL_TOPK, n_sel)), 0.0, 1.0).astype(BF16)
        notsel = lax.dot_general(notsel_t, eye_ref[...], TN_DIMS,
                                 preferred_element_type=F32).astype(BF16)
        qps[g] = jnp.concatenate([qg, jnp.concatenate([notsel] * NSA_J, axis=0)], axis=1)

    _interleave(*[c for g in range(NSA_GROUPS) for c in (window_chain(g), cmp_chain(g))])

    def sel_chain(g, nk):
        kt = jnp.concatenate([ks_ref[0:nk, g * HD:(g + 1) * HD], eneg_ref[0:nk, :]], axis=1)
        vt = with_ones(vs_ref[0:nk, g * HD:(g + 1) * HD])
        sc = lax.dot_general(qps[g], kt, NT_DIMS, preferred_element_type=F32)
        yield
        klane = lax.broadcasted_iota(jnp.int32, (R, SEL_BUCKET), 1) + (nk - SEL_BUCKET)
        tail = jnp.where(klane <= row_t((R, SEL_BUCKET)), sc[:, nk - SEL_BUCKET:], NEG_INF)
        sc = tail if nk == SEL_BUCKET else jnp.concatenate([sc[:, 0:nk - SEL_BUCKET], tail], axis=1)
        ms = jnp.max(sc, axis=1, keepdims=True)
        yield
        acc = jnp.dot(jnp.exp((sc - ms).astype(BF16)), vt, preferred_element_type=F32)
        yield
        osel_ref[g] = acc[:, 0:HD] / acc[:, HD:HD + 1]

    for b in range(ks_ref.shape[0] // SEL_BUCKET):
        @pl.when(q0 // SEL_BUCKET == b)
        def _(b=b):
            _interleave(*[sel_chain(g, (b + 1) * SEL_BUCKET) for g in range(NSA_GROUPS)])

    ssq = jnp.zeros((TQ, 1), F32)
    for g in range(NSA_GROUPS):
        o_sel = osel_ref[g]
        o_cmp = o_cmps[g]
        o_win = owin_ref[g]
        for j in range(NSA_J):
            h = g * NSA_J + j
            rs = slice(j * TQ, (j + 1) * TQ)
            o = (gates[:, 3 * h:3 * h + 1] * o_cmp[rs] + gates[:, 3 * h + 1:3 * h + 2] * o_sel[rs]
                 + gates[:, 3 * h + 2:3 * h + 3] * o_win[rs])
            ssq = ssq + jnp.sum(o * o, axis=1, keepdims=True)
            obuf_ref[:, h * HD:(h + 1) * HD] = o

    inv = lax.rsqrt(ssq / float(NSA_HEADS * HD) + NORM_EPS)
    o_ref[...] = (obuf_ref[...] * inv * ng_ref[...]).astype(BF16)


def _nsa(posb, invf, k_r, proj, small, kc, vc, covt, eneg, eye, ng, B, S):
    T = B * S
    nq = S // TQ
    n_blk = S // CMP_STRIDE
    n_sel = S // SEL_BLOCK
    kvb = COL_KV // 256
    assert TQ == TS_ROPE
    return pl.pallas_call(
        _nsa_kernel,
        grid=(B, nq),
        in_specs=[
            pl.BlockSpec((TQ // ROPE_SLOTS, LANE), lambda b, i: (b * nq + i, 0)),
            pl.BlockSpec((1, LANE), lambda b, i: (0, 0)),
            pl.BlockSpec((TQ, 1024), lambda b, i: (b * nq + i, 0)),
            pl.BlockSpec((1, NSA_GROUPS, n_blk, HD), lambda b, i: (b, 0, 0, 0)),
            pl.BlockSpec((1, NSA_GROUPS, n_blk, HD), lambda b, i: (b, 0, 0, 0)),
            pl.BlockSpec((S, 256), lambda b, i: (b, 1)),
            pl.BlockSpec((S, 256), lambda b, i: (b, kvb + 3)),
            pl.BlockSpec((S, 256), lambda b, i: (b, 2)),
            pl.BlockSpec((S, 256), lambda b, i: (b, kvb + 5)),
            pl.BlockSpec((TQ, LANE), lambda b, i: (b * nq + i, 0)),
            pl.BlockSpec((n_sel, n_blk), lambda b, i: (0, 0)),
            pl.BlockSpec((S, LANE), lambda b, i: (0, 0)),
            pl.BlockSpec((n_sel, LANE), lambda b, i: (0, 0)),
            pl.BlockSpec((1, 1024), lambda b, i: (0, 0)),
        ],
        out_specs=pl.BlockSpec((TQ, 1024), lambda b, i: (b * nq + i, 0)),
        out_shape=jax.ShapeDtypeStruct((T, 1024), BF16),
        scratch_shapes=[pltpu.VMEM((TQ, 1024), F32), pltpu.VMEM((NSA_GROUPS, NSA_J * TQ, HD), F32),
                        pltpu.VMEM((NSA_GROUPS, NSA_J * TQ, HD), F32),
                        pltpu.VMEM((NSA_GROUPS, NSA_J * TQ, HD), BF16)],
        compiler_params=_cparams(("parallel", "parallel")),
        name="nsa",
    )(posb, invf, proj, kc, vc, k_r, proj, k_r, proj, small, covt, eneg, eye, ng)


ML_HPS = 2
ML_ROWS = 512


def _mlstm_kernel(bias_ref, q_ref, k_ref, v_ref, om_ref, if_ref, cwq_ref, cwk_ref, cbq_ref, cbk_ref,
                  ng_ref, o_ref, c_ref, n_ref, m_ref, xq_ref, xk_ref):
    hp = pl.program_id(1)
    L = ML_CHUNK

    @pl.when(pl.program_id(2) == 0)
    def _():
        c_ref[...] = jnp.zeros_like(c_ref)
        n_ref[...] = jnp.zeros_like(n_ref)
        m_ref[...] = jnp.zeros_like(m_ref)
        xq_ref[:, 0:8, :] = jnp.zeros((ML_HPS, 8, ML_HD), F32)
        xk_ref[:, 0:8, :] = jnp.zeros((ML_HPS, 8, ML_HD), F32)

    def conv_silu(x_ref, hh, rows, xb_ref, w_ref, b_ref):
        cs = slice(hh * ML_HD, (hh + 1) * ML_HD)
        xb_ref[hh, 8:8 + L, :] = x_ref[rows, cs].astype(F32)
        y = b_ref[:, cs] + xb_ref[hh, 8:8 + L, :] * w_ref[CONV_W - 1:CONV_W, cs]
        for k in range(1, CONV_W):
            y = y + xb_ref[hh, 8 - k:8 - k + L, :] * w_ref[CONV_W - 1 - k:CONV_W - k, cs]
        xb_ref[hh, 0:8, :] = xb_ref[hh, L:L + 8, :]
        return _silu(y)

    r = lax.broadcasted_iota(jnp.int32, (L, L), 0)
    cidx = lax.broadcasted_iota(jnp.int32, (L, L), 1)
    tril = cidx <= r
    eye = cidx == r

    def head_chain(hh, rows):
        h = hp * ML_HPS + hh
        cs = slice(hh * ML_HD, (hh + 1) * ML_HD)
        qf = conv_silu(q_ref, hh, rows, xq_ref, cwq_ref, cbq_ref) * (ML_HD ** -0.5)
        yield
        kf = conv_silu(k_ref, hh, rows, xk_ref, cwk_ref, cbk_ref)
        yield
        qb = qf.astype(BF16)
        kb = kf.astype(BF16)
        vb = v_ref[rows, cs]
        vf = vb.astype(F32)

        ic = if_ref[0, hh, 0:1, rows] + bias_ref[0, h]
        fp = if_ref[0, hh, 1:2, rows] + bias_ref[1, h]
        fc = jnp.minimum(fp, 0.0) - jnp.log(1.0 + jnp.exp(-jnp.abs(fp)))

        fc_b = jnp.broadcast_to(fc, (L, L))
        ic_b = jnp.broadcast_to(ic, (L, L))
        b_col = jnp.sum(jnp.where(tril, fc_b, 0.0), axis=1, keepdims=True)
        fc_col = jnp.sum(jnp.where(eye, fc_b, 0.0), axis=1, keepdims=True)
        ic_col = jnp.sum(jnp.where(eye, ic_b, 0.0), axis=1, keepdims=True)
        b_row = jnp.sum(jnp.where(r <= cidx, jnp.broadcast_to(fc_col, (L, L)), 0.0),
                        axis=0, keepdims=True)
        b_last = b_col[L - 1:L, :]
        m_prev = m_ref[hh]
        yield

        d_log = jnp.where(tril, b_col - b_row + ic, NEG_INF)
        inter = b_col + m_prev
        m_t = jnp.maximum(inter, jnp.max(d_log, axis=1, keepdims=True))
        w_intra = jnp.exp(d_log - m_t)
        w_inter = jnp.exp(inter - m_t)
        qk = lax.dot_general(qb, kb, NT_DIMS, preferred_element_type=F32) * w_intra
        yield
        num = (jnp.dot(qk.astype(BF16), vb, preferred_element_type=F32)
               + w_inter * lax.dot_general(qb, c_ref[hh].astype(BF16), NT_DIMS, preferred_element_type=F32))
        den = jnp.sum(qk, axis=1, keepdims=True) + w_inter * jnp.sum(qf * n_ref[hh], axis=1, keepdims=True)
        hm = num / jnp.maximum(jnp.abs(den), jnp.exp(-m_t))
        yield

        w_log = b_last - b_col + ic_col
        m_new = jnp.maximum(b_last + m_prev, jnp.max(w_log, axis=0, keepdims=True))
        w_state = jnp.exp(w_log - m_new)
        decay = jnp.exp(b_last + m_prev - m_new)
        c_ref[hh] = decay * c_ref[hh] + lax.dot_general((w_state * vf).astype(BF16), kb, TN_DIMS,
                                                        preferred_element_type=F32)
        n_ref[hh] = decay * n_ref[hh] + jnp.sum(w_state * kf, axis=0, keepdims=True)
        m_ref[hh] = m_new
        yield

        hn = hm * lax.rsqrt(jnp.mean(hm * hm, axis=1, keepdims=True) + NORM_EPS) * ng_ref[:, cs]
        o_ref[rows, cs] = (hn * _sigmoid(om_ref[rows, cs].astype(F32))).astype(BF16)

    for cc in range(ML_ROWS // L):
        _interleave(*[head_chain(hh, slice(cc * L, (cc + 1) * L)) for hh in range(ML_HPS)])


def _mlstm(proj, if_arr, gate_bias, conv_w, conv_b, ng, B, S):
    T = B * S
    nc = S // ML_ROWS
    W = ML_HPS * ML_HD
    n_hp = ML_HEADS // ML_HPS
    assert COL_QKM % W == 0 and COL_VM % W == 0 and COL_OM % W == 0 and ML_HEADS % ML_HPS == 0
    cq, ck, cv, co = COL_QKM // W, COL_QKM // W + n_hp, COL_VM // W, COL_OM // W
    rows = lambda col0: pl.BlockSpec((ML_ROWS, W), lambda b, h, c: (b * nc + c, col0 + h))
    return pl.pallas_call(
        _mlstm_kernel,
        grid=(B, n_hp, nc),
        in_specs=[
            pl.BlockSpec(memory_space=pltpu.SMEM),
            rows(cq), rows(ck), rows(cv), rows(co),
            pl.BlockSpec((1, ML_HPS, 2, ML_ROWS), lambda b, h, c: (b, h, 0, c)),
            pl.BlockSpec((CONV_W, W), lambda b, h, c: (0, h)),
            pl.BlockSpec((CONV_W, W), lambda b, h, c: (0, n_hp + h)),
            pl.BlockSpec((1, W), lambda b, h, c: (0, h)),
            pl.BlockSpec((1, W), lambda b, h, c: (0, n_hp + h)),
            pl.BlockSpec((1, W), lambda b, h, c: (0, h)),
        ],
        out_specs=pl.BlockSpec((ML_ROWS, W), lambda b, h, c: (b * nc + c, h)),
        out_shape=jax.ShapeDtypeStruct((T, ML_HEADS * ML_HD), BF16),
        scratch_shapes=[
            pltpu.VMEM((ML_HPS, ML_HD, ML_HD), F32), pltpu.VMEM((ML_HPS, 1, ML_HD), F32),
            pltpu.VMEM((ML_HPS, 1, 1), F32),
            pltpu.VMEM((ML_HPS, ML_CHUNK + 8, ML_HD), F32), pltpu.VMEM((ML_HPS, ML_CHUNK + 8, ML_HD), F32),
        ],
        compiler_params=_cparams(("parallel", "parallel", "arbitrary")),
        name="mlstm",
    )(gate_bias, proj, proj, proj, proj, if_arr, conv_w, conv_w, conv_b, conv_b, ng)


def _pack_bf16_pair(lo, hi):
    lo_b = pltpu.bitcast(lo.astype(BF16).astype(F32), U32)
    hi_b = pltpu.bitcast(hi.astype(BF16).astype(F32), U32)
    return (lo_b >> 16) | hi_b


def _unpack_bf16_pair(p):
    lo = pltpu.bitcast(p << 16, F32)
    hi = pltpu.bitcast(p & jnp.uint32(0xFFFF0000), F32)
    return lo, hi


def _row(ref, r):
    return ref.at[pl.ds(r, 1), :]


def _outproj_kernel(nsa_ref, ml_ref, w_ref, x_ref, g2_ref, wr_ref, br_ref, tril_ref,
                    x1_ref, xp_ref, rt_ref, cnt_ref, carry_ref, xa_ref, xb_ref):
    i = pl.program_id(0)

    @pl.when(i == 0)
    def _():
        carry_ref[...] = jnp.zeros_like(carry_ref)
        xb_ref[...] = jnp.zeros_like(xb_ref)

    half = D_MODEL // 2
    subs = [slice(sub * (TM_OUT // OUT_SUBS), (sub + 1) * (TM_OUT // OUT_SUBS)) for sub in range(OUT_SUBS)]

    def step(cur_ref, prev_ref):
        carry = carry_ref[...]
        carry_box = [carry]

        def chain():
            for rs in subs:
                yield from _outproj_rows(rs, prev_ref, carry_box, g2_ref, wr_ref, br_ref, tril_ref,
                                         x1_ref, xp_ref, rt_ref)
                yield

        pending = chain()
        for rs in subs:
            for c in range(D_MODEL // OUT_COLS):
                cs = slice(c * OUT_COLS, (c + 1) * OUT_COLS)
                cur_ref[rs, cs] = (x_ref[rs, cs]
                                   + jnp.dot(nsa_ref[rs, :], w_ref[0:half, cs], preferred_element_type=F32)
                                   + jnp.dot(ml_ref[rs, :], w_ref[half:, cs], preferred_element_type=F32))
                next(pending, None)
        for _ in pending:
            pass
        carry = jnp.where(i > 0, carry_box[0], carry)
        carry_ref[...] = carry
        cnt_ref[...] = jnp.broadcast_to(carry, cnt_ref.shape)

    @pl.when(i % 2 == 0)
    def _():
        step(xa_ref, xb_ref)

    @pl.when(i % 2 == 1)
    def _():
        step(xb_ref, xa_ref)


def _outproj_rows(rs, src_ref, carry_box, g2_ref, wr_ref, br_ref, tril_ref, x1_ref, xp_ref, rt_ref):
    half = D_MODEL // 2
    x1 = src_ref[rs, :]
    x1_ref[rs, :] = x1
    xn = x1 * lax.rsqrt(jnp.mean(x1 * x1, axis=-1, keepdims=True) + NORM_EPS) * g2_ref[...]
    xp_ref[rs, :] = _pack_bf16_pair(xn[:, :half], xn[:, half:])
    logits = jnp.dot(xn.astype(BF16), wr_ref[...], preferred_element_type=F32) + br_ref[...]
    yield

    tm = logits.shape[0]
    lane = lax.broadcasted_iota(jnp.int32, (tm, LANE), 1)
    lane_f = lane.astype(F32)
    big = float(LANE)
    gmask = lane < MOE_GROUPS
    gmax = jnp.max(jnp.where(gmask, logits, NEG_INF), axis=1, keepdims=True)
    ge = jnp.where(gmask, jnp.exp(logits - gmax), 0.0)
    gp = ge / jnp.sum(ge, axis=1, keepdims=True)
    g_w = jnp.max(gp, axis=1, keepdims=True)
    g_idx = jnp.min(jnp.where(gmask, jnp.where(gp == g_w, lane_f, big), big), axis=1, keepdims=True)
    yield
    grp_of_lane = ((lane - MOE_GROUPS) >> EPG_LOG2).astype(F32)
    emask = jnp.where(lane >= MOE_GROUPS, grp_of_lane, -1.0) == g_idx
    emax = jnp.max(jnp.where(emask, logits, NEG_INF), axis=1, keepdims=True)
    ee = jnp.where(emask, jnp.exp(logits - emax), 0.0)
    ep = jnp.where(emask, ee / jnp.sum(ee, axis=1, keepdims=True), -1.0)
    v1 = jnp.max(ep, axis=1, keepdims=True)
    i1 = jnp.min(jnp.where(ep == v1, lane_f, big), axis=1, keepdims=True)
    ep2 = jnp.where(lane_f == i1, -1.0, ep)
    v2 = jnp.max(ep2, axis=1, keepdims=True)
    i2 = jnp.min(jnp.where(ep2 == v2, lane_f, big), axis=1, keepdims=True)
    w0 = g_w * v1 / (v1 + v2)
    w1 = g_w * v2 / (v1 + v2)
    e0 = i1 - float(MOE_GROUPS)
    e1 = i2 - float(MOE_GROUPS)

    yield
    oh0 = jnp.where(lane_f == e0, 1.0, 0.0)
    oh1 = jnp.where(lane_f == e1, 1.0, 0.0)
    tril = tril_ref[0:tm, 0:tm]
    pre0 = jnp.dot(tril, oh0.astype(BF16), preferred_element_type=F32)
    pre1 = jnp.dot(tril, oh1.astype(BF16), preferred_element_type=F32)
    yield
    carry = carry_box[0]
    tot0 = pre0[tm - 1:tm, :]
    tot1 = pre1[tm - 1:tm, :]
    rank0 = jnp.sum(oh0 * (pre0 - 1.0 + carry), axis=1, keepdims=True)
    rank1 = jnp.sum(oh1 * (pre1 - 1.0 + carry + tot0), axis=1, keepdims=True)

    rt = jnp.where(lane == 0, e0, jnp.where(lane == 1, e1, jnp.where(lane == 2, w0, jnp.where(
        lane == 3, w1, jnp.where(lane == 4, rank0, jnp.where(lane == 5, rank1, 0.0))))))
    rt_ref[rs, :] = rt
    carry_box[0] = carry + tot0 + tot1


def _out_proj(nsa_o, ml_o, w_out, x2, g2, w_r, b_r, tril):
    T = x2.shape[0]
    half = D_MODEL // 2
    n_tiles = T // TM_OUT
    cur = lambda i: (jnp.minimum(i, n_tiles - 1), 0)
    prev = lambda i: (jnp.maximum(i - 1, 0), 0)
    return pl.pallas_call(
        _outproj_kernel,
        grid=(n_tiles + 1,),
        in_specs=[
            pl.BlockSpec((TM_OUT, half), cur),
            pl.BlockSpec((TM_OUT, half), cur),
            pl.BlockSpec((D_MODEL, D_MODEL), lambda i: (0, 0)),
            pl.BlockSpec((TM_OUT, D_MODEL), cur),
            pl.BlockSpec((1, D_MODEL), lambda i: (0, 0)),
            pl.BlockSpec((D_MODEL, LANE), lambda i: (0, 0)),
            pl.BlockSpec((1, LANE), lambda i: (0, 0)),
            pl.BlockSpec((TM_OUT, TM_OUT), lambda i: (0, 0)),
        ],
        out_specs=[
            pl.BlockSpec((TM_OUT, D_MODEL), prev),
            pl.BlockSpec((TM_OUT, half), prev),
            pl.BlockSpec((TM_OUT, LANE), prev),
            pl.BlockSpec((8, LANE), lambda i: (0, 0)),
        ],
        out_shape=[
            jax.ShapeDtypeStruct((T, D_MODEL), F32),
            jax.ShapeDtypeStruct((T, half), U32),
            jax.ShapeDtypeStruct((T, LANE), F32),
            jax.ShapeDtypeStruct((8, LANE), F32),
        ],
        scratch_shapes=[pltpu.VMEM((1, LANE), F32), pltpu.VMEM((TM_OUT, D_MODEL), F32),
                        pltpu.VMEM((TM_OUT, D_MODEL), F32)],
        compiler_params=_cparams(("arbitrary",)),
        name="out_proj",
    )(nsa_o, ml_o, w_out, x2, g2, w_r, b_r, tril)


TD = 256
DISPATCH_BUFS = 3
DISPATCH_UNROLL = 8


def _dispatch_kernel(dest_ref, zblk_ref, meta_ref, xp_hbm, xs_hbm, zbuf_ref, zsem_ref, tbuf_ref, lsem_ref,
                     rsem_ref):
    n_assign = dest_ref.shape[0]
    nb = xs_hbm.shape[0] // BM
    n_used = meta_ref[0]
    zbuf_ref[...] = jnp.zeros_like(zbuf_ref)

    def zero_block(blk):
        return pltpu.make_async_copy(zbuf_ref, xs_hbm.at[pl.ds(blk * BM, BM), :], zsem_ref.at[0])

    def for_zero_blocks(fn):
        for e in range(N_EXPERTS):
            @pl.when(zblk_ref[e] >= 0)
            def _(e=e):
                fn(zero_block(zblk_ref[e]))

        def body(blk, _):
            fn(zero_block(blk))
            return 0
        lax.fori_loop(n_used, nb, body, 0)

    for_zero_blocks(lambda c: c.start())
    for_zero_blocks(lambda c: c.wait())

    n_tiles = n_assign // (2 * TD)

    def load(tile, slot):
        return pltpu.make_async_copy(xp_hbm.at[pl.ds(tile * TD, TD), :], tbuf_ref.at[slot], lsem_ref.at[slot])

    def wait_scatter(slot):
        for _ in range(2):
            pltpu.make_async_copy(tbuf_ref.at[slot], tbuf_ref.at[slot], rsem_ref.at[slot]).wait()

    load(0, 0).start()

    def tile_step(i, _):
        slot = i % DISPATCH_BUFS
        nslot = (i + 1) % DISPATCH_BUFS

        @pl.when(i + 1 < n_tiles)
        def _():
            @pl.when(i + 1 >= DISPATCH_BUFS)
            def _():
                wait_scatter(nslot)
            load(i + 1, nslot).start()

        load(i, slot).wait()

        def body(j, _):
            for u in range(DISPATCH_UNROLL):
                r = j * DISPATCH_UNROLL + u
                for k in range(2):
                    d = dest_ref[(i * TD + r) * 2 + k]
                    pltpu.make_async_copy(_row(tbuf_ref.at[slot], r), _row(xs_hbm, d),
                                          rsem_ref.at[slot]).start(priority=k)
            return 0
        lax.fori_loop(0, TD // DISPATCH_UNROLL, body, 0)
        return 0

    lax.fori_loop(0, n_tiles, tile_step, 0)
    for back in range(min(DISPATCH_BUFS, n_tiles)):
        wait_scatter((n_tiles - 1 - back) % DISPATCH_BUFS)


def _dispatch(dest, zblk, meta, xp, n_rows):
    half = D_MODEL // 2
    assert dest.shape[0] % (2 * TD) == 0 and dest.shape[0] // (2 * TD) >= DISPATCH_BUFS
    grid_spec = pltpu.PrefetchScalarGridSpec(
        num_scalar_prefetch=3,
        grid=(1,),
        in_specs=[pl.BlockSpec(memory_space=pl.ANY)],
        out_specs=pl.BlockSpec(memory_space=pl.ANY),
        scratch_shapes=[
            pltpu.VMEM((BM, half), U32),
            pltpu.SemaphoreType.DMA((1,)),
            pltpu.VMEM((DISPATCH_BUFS, TD, half), U32),
            pltpu.SemaphoreType.DMA((DISPATCH_BUFS,)),
            pltpu.SemaphoreType.DMA((DISPATCH_BUFS,)),
        ],
    )
    return pl.pallas_call(
        _dispatch_kernel,
        grid_spec=grid_spec,
        out_shape=jax.ShapeDtypeStruct((n_rows, half), U32),
        compiler_params=_cparams(("arbitrary",)),
        name="dispatch",
    )(dest, zblk, meta, xp)


def _expert_kernel(be_ref, nxt_ref, par_ref, meta_ref, x_ref, wg_hbm, wu_hbm, wd_hbm, y_ref,
                   wsg_ref, wsu_ref, wsd_ref, wsem_ref, wgb_ref, wub_ref, wdb_ref):
    i = pl.program_id(0)
    n_used = meta_ref[0]
    half = D_MODEL // 2

    def weight_copies(e, p):
        return (pltpu.make_async_copy(wg_hbm.at[e], wsg_ref.at[p], wsem_ref.at[p, 0]),
                pltpu.make_async_copy(wu_hbm.at[e], wsu_ref.at[p], wsem_ref.at[p, 1]),
                pltpu.make_async_copy(wd_hbm.at[e], wsd_ref.at[p], wsem_ref.at[p, 2]))

    def succ(e):
        return jnp.where(e >= 0, nxt_ref[jnp.maximum(e, 0)], -1)

    def start_weights(e, p):
        @pl.when(e >= 0)
        def _():
            for c in weight_copies(e, p):
                c.start(priority=1)

    def wait_staged(p):
        for c in weight_copies(0, p):
            c.wait()

    def cast_staged(p):
        wgb_ref[p] = wsg_ref[p].astype(BF16)
        wub_ref[p] = wsu_ref[p].astype(BF16)
        wdb_ref[p] = wsd_ref[p].astype(BF16)

    @pl.when(i == 0)
    def _():
        e0 = be_ref[0]
        start_weights(e0, 0)
        start_weights(succ(e0), 1)
        wait_staged(0)
        cast_staged(0)
        start_weights(succ(succ(e0)), 0)

    def step(p, cast_next):
        if cast_next:
            wait_staged(1 - p)
        lo, hi = _unpack_bf16_pair(x_ref[...])
        xl = lo.astype(BF16)
        xh = hi.astype(BF16)
        gt = (jnp.dot(xl, wgb_ref[p, 0:half, :], preferred_element_type=F32)
              + jnp.dot(xh, wgb_ref[p, half:, :], preferred_element_type=F32))
        up = (jnp.dot(xl, wub_ref[p, 0:half, :], preferred_element_type=F32)
              + jnp.dot(xh, wub_ref[p, half:, :], preferred_element_type=F32))
        hb = (_silu(gt) * up).astype(BF16)
        y = jnp.dot(hb, wdb_ref[p], preferred_element_type=F32)
        y_ref[...] = _pack_bf16_pair(y[:, :half], y[:, half:])
        if cast_next:
            cast_staged(1 - p)
            start_weights(succ(succ(succ(be_ref[i]))), 1 - p)

    e = be_ref[jnp.minimum(i, n_used - 1)]
    ends_expert = (i + 1 < n_used) & (be_ref[jnp.minimum(i + 1, n_used - 1)] != e)
    for p in range(2):
        for cast_next in (False, True):
            @pl.when((i < n_used) & (par_ref[i] == p)
                     & (ends_expert if cast_next else jnp.logical_not(ends_expert)))
            def _(p=p, cast_next=cast_next):
                step(p, cast_next)

    @pl.when(i >= n_used)
    def _():
        y_ref[...] = jnp.zeros_like(y_ref)


def _experts(block_expert, next_expert, block_parity, meta, xs, w_gate, w_up, w_down):
    half = D_MODEL // 2
    n_rows = xs.shape[0]
    nb = n_rows // BM
    grid_spec = pltpu.PrefetchScalarGridSpec(
        num_scalar_prefetch=4,
        grid=(nb,),
        in_specs=[
            pl.BlockSpec((BM, half), lambda i, be, nxt, par, meta: (jnp.minimum(i, meta[0] - 1), 0)),
            pl.BlockSpec(memory_space=pl.ANY), pl.BlockSpec(memory_space=pl.ANY),
            pl.BlockSpec(memory_space=pl.ANY),
        ],
        out_specs=pl.BlockSpec((BM, half), lambda i, be, nxt, par, meta: (i, 0)),
        scratch_shapes=[
            pltpu.VMEM((2, D_MODEL, D_EXPERT), F32),
            pltpu.VMEM((2, D_MODEL, D_EXPERT), F32),
            pltpu.VMEM((2, D_EXPERT, D_MODEL), F32),
            pltpu.SemaphoreType.DMA((2, 3)),
            pltpu.VMEM((2, D_MODEL, D_EXPERT), BF16),
            pltpu.VMEM((2, D_MODEL, D_EXPERT), BF16),
            pltpu.VMEM((2, D_EXPERT, D_MODEL), BF16),
        ],
    )
    return pl.pallas_call(
        _expert_kernel,
        grid_spec=grid_spec,
        out_shape=jax.ShapeDtypeStruct((n_rows, half), U32),
        compiler_params=_cparams(("arbitrary",)),
        name="experts",
    )(block_expert, next_expert, block_parity, meta, xs, w_gate, w_up, w_down)


def _combine_kernel(dest_ref, y_hbm, x1_ref, rt_ref, fg_ref, o_ref, ya_ref, yb_ref, sem_ref):
    i = pl.program_id(0)
    nt = pl.num_programs(0)
    half = D_MODEL // 2
    bufs = (ya_ref, yb_ref)

    def row_copy(tile, slot, r_tile, s, k):
        r = r_tile * 8 + s
        d = dest_ref[(tile * TC + r) * 2 + k]
        return pltpu.make_async_copy(_row(y_hbm, d), _row(bufs[slot].at[k], r), sem_ref.at[slot])

    def wait_rows(slot):
        pltpu.make_async_copy(bufs[slot], bufs[slot], sem_ref.at[slot]).wait()

    def step(slot):
        wait_rows(slot)
        for r in range(TC):
            for k in range(2):
                row_copy(i + 1, 1 - slot, r // 8, r % 8, k).start(priority=k)
        rt = rt_ref[...]
        w0 = rt[:, 2:3]
        w1 = rt[:, 3:4]
        lo0, hi0 = _unpack_bf16_pair(bufs[slot][0])
        lo1, hi1 = _unpack_bf16_pair(bufs[slot][1])
        xl = x1_ref[:, :half] + (w0 * lo0 + w1 * lo1)
        xh = x1_ref[:, half:] + (w0 * hi0 + w1 * hi1)
        ms = (jnp.sum(xl * xl, axis=1, keepdims=True) + jnp.sum(xh * xh, axis=1, keepdims=True)) / float(D_MODEL)
        inv = lax.rsqrt(ms + NORM_EPS)
        o_ref[:, :half] = xl * inv * fg_ref[:, :half]
        o_ref[:, half:] = xh * inv * fg_ref[:, half:]

        @pl.when(i == nt - 1)
        def _():
            wait_rows(1 - slot)

    @pl.when(i == 0)
    def _():
        def body(rt, _):
            for s in range(8):
                for k in range(2):
                    row_copy(0, 0, rt, s, k).start(priority=k)
            return 0
        lax.fori_loop(0, TC // 8, body, 0)

    @pl.when(i % 2 == 0)
    def _():
        step(0)

    @pl.when(i % 2 == 1)
    def _():
        step(1)


def _combine(dest, ys, x1, route, fg):
    T = x1.shape[0]
    half = D_MODEL // 2
    grid_spec = pltpu.PrefetchScalarGridSpec(
        num_scalar_prefetch=1,
        grid=(T // TC,),
        in_specs=[
            pl.BlockSpec(memory_space=pl.ANY),
            pl.BlockSpec((TC, D_MODEL), lambda i, d: (i, 0)),
            pl.BlockSpec((TC, LANE), lambda i, d: (i, 0)),
            pl.BlockSpec((1, D_MODEL), lambda i, d: (0, 0)),
        ],
        out_specs=pl.BlockSpec((TC, D_MODEL), lambda i, d: (i, 0)),
        scratch_shapes=[
            pltpu.VMEM((2, TC, half), U32),
            pltpu.VMEM((2, TC, half), U32),
            pltpu.SemaphoreType.DMA((2,)),
        ],
    )
    return pl.pallas_call(
        _combine_kernel,
        grid_spec=grid_spec,
        out_shape=jax.ShapeDtypeStruct((T, D_MODEL), F32),
        compiler_params=_cparams(("arbitrary",)),
        name="combine",
    )(dest, ys, x1, route, fg)


def _cover_matrix(S):
    n_blk = S // CMP_STRIDE
    n_sel = S // SEL_BLOCK
    cs = np.arange(n_blk) * CMP_STRIDE
    ss = np.arange(n_sel) * SEL_BLOCK
    shared = np.minimum(cs[:, None] + CMP_LEN, ss[None, :] + SEL_BLOCK) - np.maximum(cs[:, None], ss[None, :])
    return (np.clip(shared, 0, None) / CMP_LEN).T.astype(np.float32)


def _block_mask_matrix(S):
    n_sel = S // SEL_BLOCK
    assert n_sel <= LANE
    e = np.zeros((S, LANE), np.float32)
    e[np.arange(S), np.arange(S) // SEL_BLOCK] = NEG_INF
    return e


def _inv_freq_row():
    inv = np.power(np.float32(ROPE_THETA), -np.arange(ROPE_HALF, dtype=np.float32) * 2.0 / ROPE_DIM)
    return np.tile(inv, ROPE_SLOTS).reshape(1, LANE).astype(np.float32)


def _packed_positions(positions, T):
    rows = TS_ROPE // ROPE_SLOTS
    p = positions.reshape(T // TS_ROPE, ROPE_SLOTS, rows).transpose(0, 2, 1).astype(F32)
    return jnp.repeat(p, ROPE_HALF, axis=2).reshape(T // ROPE_SLOTS, LANE)


def _layer(x2, positions, B, S, norm1_g, w_in, cmp_pe_k, cmp_pe_v, cmp_wk1, cmp_wk2, cmp_wv1, cmp_wv2,
           nsa_norm_g, conv_w, conv_b, b_igate, b_fgate, mlstm_norm_g, w_out, norm2_g,
           w_group, b_group, w_router, b_router, w_exp_gate, w_exp_up, w_exp_down, out_norm_g):
    T = B * S
    w_main, w_small = _weight_prep(jnp.transpose(w_in[0]))
    posb = _packed_positions(positions, T)
    invf = jnp.asarray(_inv_freq_row())

    proj, small = _in_proj(x2, norm1_g.reshape(1, -1), w_main, w_small)
    k_r = _rope(posb, invf, proj)
    kc, vc = _compress(k_r, proj, B, S, cmp_pe_k, cmp_pe_v, cmp_wk1.astype(BF16), cmp_wk2.astype(BF16),
                       cmp_wv1.astype(BF16), cmp_wv2.astype(BF16))
    eye = np.eye(S // SEL_BLOCK, LANE, dtype=np.float32)
    nsa_o = _nsa(posb, invf, k_r, proj, small, kc, vc, jnp.asarray(_cover_matrix(S), BF16),
                 jnp.asarray(_block_mask_matrix(S), BF16), jnp.asarray(eye, BF16),
                 nsa_norm_g.reshape(1, -1), B, S)

    if_arr = small[:, 24:32].reshape(B, S, 2, ML_HEADS).transpose(0, 3, 2, 1)
    gate_bias = jnp.stack([b_igate, b_fgate]).astype(F32)
    ml_o = _mlstm(proj, if_arr, gate_bias, conv_w, conv_b.reshape(1, -1), mlstm_norm_g.reshape(1, -1), B, S)

    w_r = jnp.concatenate([w_group, w_router, jnp.zeros((D_MODEL, LANE - MOE_GROUPS - N_EXPERTS), F32)],
                          axis=1).astype(BF16)
    b_r = jnp.concatenate([b_group, b_router, jnp.zeros((LANE - MOE_GROUPS - N_EXPERTS,), F32)]).reshape(1, LANE)
    tril = jnp.asarray(np.tril(np.ones((TM_OUT, TM_OUT), np.float32)), BF16)
    x1, xp, route, cnt = _out_proj(nsa_o, ml_o, w_out.astype(BF16), x2, norm2_g.reshape(1, -1), w_r, b_r, tril)

    n_rows = T * 2 + N_EXPERTS * BM
    counts = cnt[0, :N_EXPERTS].astype(jnp.int32)
    pcounts = (counts + BM - 1) // BM * BM
    pends = jnp.cumsum(pcounts)
    pstarts = pends - pcounts
    eid = route[:, 0:2].astype(jnp.int32)
    onehot = (eid[..., None] == jnp.arange(N_EXPERTS, dtype=jnp.int32)).astype(F32)
    start_blk = jnp.einsum('tke,e->tk', onehot, (pstarts // BM).astype(F32)).astype(jnp.int32)
    dest = (start_blk * BM + route[:, 4:6].astype(jnp.int32)).reshape(T * 2)
    block_row0 = jnp.arange(n_rows // BM, dtype=jnp.int32) * BM
    block_expert = jnp.minimum(jnp.sum((pends[None, :] <= block_row0[:, None]).astype(jnp.int32), axis=1),
                               N_EXPERTS - 1).astype(jnp.int32)
    present = jnp.where(counts > 0, jnp.arange(N_EXPERTS, dtype=jnp.int32), N_EXPERTS)
    later = jnp.concatenate([lax.cummin(present[::-1])[::-1][1:], jnp.full((1,), N_EXPERTS, jnp.int32)])
    next_expert = jnp.where(later < N_EXPERTS, later, -1).astype(jnp.int32)
    meta = (pends[-1:] // BM).astype(jnp.int32)

    last_block = jnp.where(counts > 0, pends // BM - 1, -1).astype(jnp.int32)

    xs = _dispatch(dest, last_block, meta, xp, n_rows)
    ordinal = jnp.cumsum((counts > 0).astype(jnp.int32)) - 1
    block_parity = (ordinal[block_expert] % 2).astype(jnp.int32)
    ys = _experts(block_expert, next_expert, block_parity, meta, xs, w_exp_gate, w_exp_up, w_exp_down)
    dest_pad = jnp.concatenate([dest, jnp.zeros((2 * TC,), jnp.int32)])
    return _combine(dest_pad, ys, x1, route, out_norm_g.reshape(1, -1))


def kernel(x, positions, norm1_g, w_in, cmp_pe_k, cmp_pe_v, cmp_wk1, cmp_wk2, cmp_wv1, cmp_wv2, nsa_norm_g,
           conv_w, conv_b, b_igate, b_fgate, mlstm_norm_g, w_out, norm2_g, w_group, b_group, w_router,
           b_router, w_exp_gate, w_exp_up, w_exp_down, final_norm_g):
    B, S, D = x.shape
    assert D == D_MODEL and norm1_g.shape[0] == 1, "single-layer, D_MODEL-wide configuration only"
    assert S % ML_ROWS == 0 and ML_ROWS % ML_CHUNK == 0 and S % SEL_BUCKET == 0 and S >= WIN_KEYS and (B * S) % TM_IN == 0
    out = _layer(x.reshape(B * S, D), positions, B, S, norm1_g[0], w_in, cmp_pe_k[0], cmp_pe_v[0],
                 cmp_wk1[0], cmp_wk2[0], cmp_wv1[0], cmp_wv2[0], nsa_norm_g[0], conv_w[0], conv_b[0],
                 b_igate[0], b_fgate[0], mlstm_norm_g[0], w_out[0], norm2_g[0], w_group[0], b_group[0],
                 w_router[0], b_router[0], w_exp_gate[0], w_exp_up[0], w_exp_down[0], final_norm_g)
    return out.reshape(B, S, D)
```

```python
import functools

import numpy as np
import jax
import jax.numpy as jnp
from jax import lax
from jax.experimental import pallas as pl
from jax.experimental.pallas import tpu as pltpu

F32 = jnp.float32
BF16 = jnp.bfloat16
U32 = jnp.uint32

D_MODEL = 2048
NSA_HEADS = 8
NSA_GROUPS = 2
NSA_J = NSA_HEADS // NSA_GROUPS
HD = 128
CMP_LEN = 32
CMP_STRIDE = 16
SEL_BLOCK = 64
SEL_TOPK = 8
WINDOW = 512
ROPE_THETA = 500000.0
ROPE_DIM = 32
ROPE_HALF = 16
ML_HEADS = 4
ML_HD = 256
CONV_W = 4
MOE_GROUPS = 8
EPG = 8
N_EXPERTS = 64
D_EXPERT = 512
SEL_BLOCK_LOG2 = 6
EPG_LOG2 = 3
assert (1 << SEL_BLOCK_LOG2) == SEL_BLOCK and (1 << EPG_LOG2) == EPG
NORM_EPS = 1e-6
NEG_INF = -1e30
FORCE_SCORE = 1000.0

COL_Q = 0
COL_KV = 1024
COL_QKM = 2560
COL_VM = 4608
COL_OM = 5632
N_MAIN = 6656

LANE = 128
VMEM_LIMIT = 56 * 1024 * 1024

TM_IN = 1024
TN_IN = 1664
TS_ROPE = 256
TQ = 256
SEL_BUCKET = 256
WIN_KEYS = WINDOW + TQ
ML_CHUNK = 256
TM_OUT = 512
OUT_SUBS = 2
OUT_COLS = 512
BM = 256
TC = 256

NT_DIMS = (((1,), (1,)), ((), ()))
TN_DIMS = (((0,), (0,)), ((), ()))


def _cparams(sem):
    return pltpu.CompilerParams(dimension_semantics=sem, vmem_limit_bytes=VMEM_LIMIT)


def _sigmoid(x):
    return 0.5 * jnp.tanh(0.5 * x) + 0.5


def _interleave(*chains):
    live = list(chains)
    while live:
        for g in list(live):
            try:
                next(g)
            except StopIteration:
                live.remove(g)


def _silu(x):
    h = 0.5 * x
    return h + h * jnp.tanh(h)


W_GATES = 2560
W_QKM = 2584
W_IGATE = 6680
W_END = 6688
TR_PREP = 512
assert W_GATES % TR_PREP == 0 and N_MAIN % TR_PREP == 0 and W_IGATE - W_QKM == N_MAIN - W_GATES


def _wprep_kernel(wt_hbm, wm_ref, ws_ref, wbuf_ref, sbuf_ref, sem_ref, ssem_ref):
    i = pl.program_id(0)
    n_gate = W_QKM - W_GATES
    n_if = W_END - W_IGATE

    def load(blk, slot):
        start = pl.multiple_of(blk * TR_PREP + jnp.where(blk * TR_PREP >= W_GATES, n_gate, 0), 8)
        return pltpu.make_async_copy(wt_hbm.at[pl.ds(start, TR_PREP), :], wbuf_ref.at[slot], sem_ref.at[slot])

    def small_loads():
        return (pltpu.make_async_copy(wt_hbm.at[pl.ds(W_GATES, n_gate), :], sbuf_ref.at[pl.ds(0, n_gate), :],
                                      ssem_ref.at[0]),
                pltpu.make_async_copy(wt_hbm.at[pl.ds(W_IGATE, n_if), :], sbuf_ref.at[pl.ds(n_gate, n_if), :],
                                      ssem_ref.at[1]))

    @pl.when(i == 0)
    def _():
        load(0, 0).start()
        for c in small_loads():
            c.start()
        for c in small_loads():
            c.wait()
        ws_ref[...] = jnp.zeros_like(ws_ref)
        ws_ref[0:n_gate + n_if, :] = sbuf_ref[...].astype(BF16)

    @pl.when(i + 1 < pl.num_programs(0))
    def _():
        load(i + 1, (i + 1) % 2).start()

    load(i, i % 2).wait()
    wm_ref[...] = wbuf_ref[i % 2].astype(BF16)


def _weight_prep(wt):
    assert wt.shape == (W_END, D_MODEL)
    n_small = (W_QKM - W_GATES) + (W_END - W_IGATE)
    return pl.pallas_call(
        _wprep_kernel,
        grid=(N_MAIN // TR_PREP,),
        in_specs=[pl.BlockSpec(memory_space=pl.ANY)],
        out_specs=[pl.BlockSpec((TR_PREP, D_MODEL), lambda i: (i, 0)),
                   pl.BlockSpec((LANE, D_MODEL), lambda i: (0, 0))],
        out_shape=[jax.ShapeDtypeStruct((N_MAIN, D_MODEL), BF16),
                   jax.ShapeDtypeStruct((LANE, D_MODEL), BF16)],
        scratch_shapes=[pltpu.VMEM((2, TR_PREP, D_MODEL), F32), pltpu.VMEM((n_small, D_MODEL), F32),
                        pltpu.SemaphoreType.DMA((2,)), pltpu.SemaphoreType.DMA((2,))],
        compiler_params=_cparams(("arbitrary",)),
        name="weight_prep",
    )(wt)


def _inproj_kernel(x_ref, g_ref, w_ref, ws_ref, o_ref, os_ref, h_ref):
    @pl.when(pl.program_id(1) == 0)
    def _():
        x = x_ref[...]
        ms = jnp.mean(x * x, axis=-1, keepdims=True)
        h_ref[...] = (x * lax.rsqrt(ms + NORM_EPS) * g_ref[...]).astype(BF16)
        os_ref[...] = lax.dot_general(h_ref[...], ws_ref[...], NT_DIMS, preferred_element_type=F32)

    o_ref[...] = lax.dot_general(h_ref[...], w_ref[...], NT_DIMS,
                                 preferred_element_type=F32).astype(o_ref.dtype)


def _in_proj(x2, g1, w_main, w_small):
    T = x2.shape[0]
    return pl.pallas_call(
        _inproj_kernel,
        grid=(T // TM_IN, N_MAIN // TN_IN),
        in_specs=[
            pl.BlockSpec((TM_IN, D_MODEL), lambda m, n: (m, 0)),
            pl.BlockSpec((1, D_MODEL), lambda m, n: (0, 0)),
            pl.BlockSpec((TN_IN, D_MODEL), lambda m, n: (n, 0)),
            pl.BlockSpec((LANE, D_MODEL), lambda m, n: (0, 0)),
        ],
        out_specs=[
            pl.BlockSpec((TM_IN, TN_IN), lambda m, n: (m, n)),
            pl.BlockSpec((TM_IN, LANE), lambda m, n: (m, 0)),
        ],
        out_shape=[
            jax.ShapeDtypeStruct((T, N_MAIN), BF16),
            jax.ShapeDtypeStruct((T, LANE), F32),
        ],
        scratch_shapes=[pltpu.VMEM((TM_IN, D_MODEL), BF16)],
        compiler_params=_cparams(("parallel", "arbitrary")),
        name="in_proj",
    )(x2, g1, w_main, w_small)


ROPE_SLOTS = LANE // ROPE_HALF


def _rope_slots(pos_ref, invf_ref):
    ang = pos_ref[...] * invf_ref[...]
    c = jnp.cos(ang)
    s = jnp.sin(ang)
    rows = ang.shape[0]
    lane = lax.broadcasted_iota(jnp.int32, ang.shape, 1)
    lo = lane < ROPE_HALF
    mid = lane < ROPE_DIM

    def lanes_from(x, src):
        shift = (-src) % LANE
        return x if shift == 0 else pltpu.roll(x, shift, 1)

    for slot in range(ROPE_SLOTS):
        src = slot * ROPE_HALF
        cf = jnp.where(lo, lanes_from(c, src), jnp.where(mid, lanes_from(c, src - ROPE_HALF), 1.0))
        sa = jnp.where(lo, -lanes_from(s, src), 0.0)
        sb = jnp.where(lo, 0.0, jnp.where(mid, lanes_from(s, src - ROPE_HALF), 0.0))

        def rope(x, cf=cf, sa=sa, sb=sb):
            return x * cf + pltpu.roll(x, LANE - ROPE_HALF, 1) * sa + pltpu.roll(x, ROPE_HALF, 1) * sb

        yield slice(slot * rows, (slot + 1) * rows), rope


def _rope_kernel(pos_ref, invf_ref, kc_ref, ks_ref, kw_ref, ko_ref):
    for rs, rope in _rope_slots(pos_ref, invf_ref):
        for i, r in enumerate((kc_ref, ks_ref, kw_ref)):
            for g in range(NSA_GROUPS):
                sl = slice(g * HD, (g + 1) * HD)
                so = slice(i * 2 * HD + g * HD, i * 2 * HD + (g + 1) * HD)
                ko_ref[rs, so] = rope(r[rs, sl].astype(F32)).astype(BF16)


def _rope(posb, invf, proj):
    T = proj.shape[0]
    kvb = COL_KV // 256
    return pl.pallas_call(
        _rope_kernel,
        grid=(T // TS_ROPE,),
        in_specs=[
            pl.BlockSpec((TS_ROPE // ROPE_SLOTS, LANE), lambda i: (i, 0)),
            pl.BlockSpec((1, LANE), lambda i: (0, 0)),
            pl.BlockSpec((TS_ROPE, 256), lambda i: (i, kvb + 0)),
            pl.BlockSpec((TS_ROPE, 256), lambda i: (i, kvb + 2)),
            pl.BlockSpec((TS_ROPE, 256), lambda i: (i, kvb + 4)),
        ],
        out_specs=pl.BlockSpec((TS_ROPE, 768), lambda i: (i, 0)),
        out_shape=jax.ShapeDtypeStruct((T, 768), BF16),
        compiler_params=_cparams(("parallel",)),
        name="rope",
    )(posb, invf, proj, proj, proj)


def _compress_kernel(k_ref, v_ref, pek_ref, pev_ref, w1k_ref, w2k_ref, w1v_ref, w2v_ref,
                     kc_ref, vc_ref, xs_ref):
    S = k_ref.shape[0]
    n_blk = S // CMP_STRIDE
    for src, pe, w1, w2, dst in ((k_ref, pek_ref, w1k_ref, w2k_ref, kc_ref),
                                 (v_ref, pev_ref, w1v_ref, w2v_ref, vc_ref)):
        xs_ref[0:S, :] = src[...].astype(F32)
        xs_ref[S:S + CMP_LEN, :] = jnp.zeros((CMP_LEN, HD), F32)
        acc = jnp.zeros((n_blk, HD), F32)
        for l in range(CMP_LEN):
            a = xs_ref[pl.ds(l, n_blk, stride=CMP_STRIDE), :] + pe[l:l + 1, :]
            acc = acc + jnp.dot(a.astype(BF16), w1[l * HD:(l + 1) * HD, :],
                                preferred_element_type=F32)
        hid = _silu(acc)
        out = jnp.dot(hid.astype(BF16), w2[...], preferred_element_type=F32)
        dst[0, 0] = out.astype(BF16)


def _compress(k_r, proj, B, S, pek, pev, w1k, w2k, w1v, w2v):
    n_blk = S // CMP_STRIDE
    vcol = (COL_KV + 256) // HD
    full = lambda shape: pl.BlockSpec(shape, lambda b, g: tuple(0 for _ in shape))
    return pl.pallas_call(
        _compress_kernel,
        grid=(B, NSA_GROUPS),
        in_specs=[
            pl.BlockSpec((S, HD), lambda b, g: (b, g)),
            pl.BlockSpec((S, HD), lambda b, g: (b, vcol + g)),
            full((CMP_LEN, HD)), full((CMP_LEN, HD)),
            full((CMP_LEN * HD, HD)), full((HD, HD)),
            full((CMP_LEN * HD, HD)), full((HD, HD)),
        ],
        out_specs=[
            pl.BlockSpec((1, 1, n_blk, HD), lambda b, g: (b, g, 0, 0)),
            pl.BlockSpec((1, 1, n_blk, HD), lambda b, g: (b, g, 0, 0)),
        ],
        out_shape=[
            jax.ShapeDtypeStruct((B, NSA_GROUPS, n_blk, HD), BF16),
            jax.ShapeDtypeStruct((B, NSA_GROUPS, n_blk, HD), BF16),
        ],
        scratch_shapes=[pltpu.VMEM((S + CMP_LEN, HD), F32)],
        compiler_params=_cparams(("parallel", "parallel")),
        name="compress",
    )(k_r, proj, pek, pev, w1k, w2k, w1v, w2v)


def _nsa_kernel(pos_ref, invf_ref, q_ref, kc_ref, vc_ref, ks_ref, vs_ref, kw_ref, vw_ref, sm_ref, covt_ref,
                eneg_ref, eye_ref, ng_ref, o_ref, obuf_ref, owin_ref, osel_ref, qs_ref):
    qi = pl.program_id(1)
    q0 = qi * TQ
    R = NSA_J * TQ
    n_cmp = kc_ref.shape[2]
    n_sel = ks_ref.shape[0] // SEL_BLOCK
    n_win = WIN_KEYS // TQ

    def row_t(shape):
        r = lax.broadcasted_iota(jnp.int32, shape, 0)
        return q0 + (r & (TQ - 1))

    scale = HD ** -0.5
    for rs, rope in _rope_slots(pos_ref, invf_ref):
        for h in range(NSA_HEADS):
            g, j = divmod(h, NSA_J)
            dst = slice(j * TQ + rs.start, j * TQ + rs.stop)
            qs_ref[g, dst, :] = (rope(q_ref[rs, h * HD:(h + 1) * HD].astype(F32)) * scale).astype(BF16)
    qgs = [qs_ref[g] for g in range(NSA_GROUPS)]

    def with_ones(v):
        lane = lax.broadcasted_iota(jnp.int32, v.shape, 1)
        return jnp.concatenate([v, jnp.where(lane == 0, 1.0, 0.0).astype(BF16)], axis=1)

    def window_chain(g):
        w0 = pl.multiple_of(jnp.maximum(q0 - WINDOW, 0), TQ)
        kt = kw_ref[pl.ds(w0, WIN_KEYS), g * HD:(g + 1) * HD]
        vt = with_ones(vw_ref[pl.ds(w0, WIN_KEYS), g * HD:(g + 1) * HD])
        sc = lax.dot_general(qgs[g], kt, NT_DIMS, preferred_element_type=F32)
        yield
        diff = row_t((R, WIN_KEYS)) - (w0 + lax.broadcasted_iota(jnp.int32, (R, WIN_KEYS), 1))
        keep = jnp.where(diff >= 0, diff, WINDOW) < WINDOW
        sc = jnp.where(keep, sc, NEG_INF)
        mw = jnp.max(sc, axis=1, keepdims=True)
        yield
        pw = jnp.exp((sc - mw).astype(BF16))
        acc = jnp.dot(pw, vt, preferred_element_type=F32)
        yield
        owin_ref[g] = acc[:, 0:HD] / acc[:, HD:HD + 1]

    gates = _sigmoid(sm_ref[...])
    o_cmps = [None] * NSA_GROUPS
    qps = [None] * NSA_GROUPS

    def cmp_chain(g):
        qg = qgs[g]

        s = lax.dot_general(qg, kc_ref[0, g], NT_DIMS, preferred_element_type=F32)
        yield
        n_lane = lax.broadcasted_iota(jnp.int32, (R, n_cmp), 1)
        cmask = (n_lane * CMP_STRIDE + (CMP_LEN - 1)) <= row_t((R, n_cmp))
        s = jnp.where(cmask, s, NEG_INF)
        m = jnp.max(s, axis=1, keepdims=True)
        e = jnp.where(cmask, jnp.exp(s - m), 0.0)
        l = jnp.sum(e, axis=1, keepdims=True)
        p = (e / jnp.where(l > 0.0, l, 1.0)).astype(BF16)
        yield
        o_cmps[g] = jnp.dot(p, vc_ref[0, g], preferred_element_type=F32)
        impr = lax.dot_general(covt_ref[...], p, NT_DIMS, preferred_element_type=F32)
        imp = impr[:, 0:TQ]
        for j in range(1, NSA_J):
            imp = imp + impr[:, j * TQ:(j + 1) * TQ]
        yield

        m_sub = lax.broadcasted_iota(jnp.int32, (n_sel, TQ), 0)
        jt = (q0 + lax.broadcasted_iota(jnp.int32, (n_sel, TQ), 1)) >> SEL_BLOCK_LOG2
        forced = jnp.where(m_sub == 0, FORCE_SCORE,
                           jnp.where(m_sub == jt, FORCE_SCORE,
                                     jnp.where(m_sub == jt - 1, FORCE_SCORE, 0.0)))
        score = jnp.where(m_sub <= jt, imp + forced, -1.0)
        rank = jnp.zeros((n_sel, TQ), F32)
        for mp in range(n_sel):
            row = score[mp:mp + 1, :]
            ge = jnp.where(row >= score, 1.0, 0.0)
            gt = jnp.where(row > score, 1.0, 0.0)
            rank = rank + jnp.where(m_sub > mp, ge, gt)
        notsel_t = jnp.where(rank < float(min(SEL_TOPK, n_sel)), 0.0, 1.0).astype(BF16)
        notsel = lax.dot_general(notsel_t, eye_ref[...], TN_DIMS,
                                 preferred_element_type=F32).astype(BF16)
        qps[g] = jnp.concatenate([qg, jnp.concatenate([notsel] * NSA_J, axis=0)], axis=1)

    _interleave(*[c for g in range(NSA_GROUPS) for c in (window_chain(g), cmp_chain(g))])

    def sel_chain(g, nk):
        kt = jnp.concatenate([ks_ref[0:nk, g * HD:(g + 1) * HD], eneg_ref[0:nk, :]], axis=1)
        vt = with_ones(vs_ref[0:nk, g * HD:(g + 1) * HD])
        sc = lax.dot_general(qps[g], kt, NT_DIMS, preferred_element_type=F32)
        yield
        klane = lax.broadcasted_iota(jnp.int32, (R, SEL_BUCKET), 1) + (nk - SEL_BUCKET)
        tail = jnp.where(klane <= row_t((R, SEL_BUCKET)), sc[:, nk - SEL_BUCKET:], NEG_INF)
        sc = tail if nk == SEL_BUCKET else jnp.concatenate([sc[:, 0:nk - SEL_BUCKET], tail], axis=1)
        ms = jnp.max(sc, axis=1, keepdims=True)
        yield
        acc = jnp.dot(jnp.exp((sc - ms).astype(BF16)), vt, preferred_element_type=F32)
        yield
        osel_ref[g] = acc[:, 0:HD] / acc[:, HD:HD + 1]

    for b in range(ks_ref.shape[0] // SEL_BUCKET):
        @pl.when(q0 // SEL_BUCKET == b)
        def _(b=b):
            _interleave(*[sel_chain(g, (b + 1) * SEL_BUCKET) for g in range(NSA_GROUPS)])

    ssq = jnp.zeros((TQ, 1), F32)
    for g in range(NSA_GROUPS):
        o_sel = osel_ref[g]
        o_cmp = o_cmps[g]
        o_win = owin_ref[g]
        for j in range(NSA_J):
            h = g * NSA_J + j
            rs = slice(j * TQ, (j + 1) * TQ)
            o = (gates[:, 3 * h:3 * h + 1] * o_cmp[rs] + gates[:, 3 * h + 1:3 * h + 2] * o_sel[rs]
                 + gates[:, 3 * h + 2:3 * h + 3] * o_win[rs])
            ssq = ssq + jnp.sum(o * o, axis=1, keepdims=True)
            obuf_ref[:, h * HD:(h + 1) * HD] = o

    inv = lax.rsqrt(ssq / float(NSA_HEADS * HD) + NORM_EPS)
    o_ref[...] = (obuf_ref[...] * inv * ng_ref[...]).astype(BF16)


def _nsa(posb, invf, k_r, proj, small, kc, vc, covt, eneg, eye, ng, B, S):
    T = B * S
    nq = S // TQ
    n_blk = S // CMP_STRIDE
    n_sel = S // SEL_BLOCK
    kvb = COL_KV // 256
    assert TQ == TS_ROPE
    return pl.pallas_call(
        _nsa_kernel,
        grid=(B, nq),
        in_specs=[
            pl.BlockSpec((TQ // ROPE_SLOTS, LANE), lambda b, i: (b * nq + i, 0)),
            pl.BlockSpec((1, LANE), lambda b, i: (0, 0)),
            pl.BlockSpec((TQ, 1024), lambda b, i: (b * nq + i, 0)),
            pl.BlockSpec((1, NSA_GROUPS, n_blk, HD), lambda b, i: (b, 0, 0, 0)),
            pl.BlockSpec((1, NSA_GROUPS, n_blk, HD), lambda b, i: (b, 0, 0, 0)),
            pl.BlockSpec((S, 256), lambda b, i: (b, 1)),
            pl.BlockSpec((S, 256), lambda b, i: (b, kvb + 3)),
            pl.BlockSpec((S, 256), lambda b, i: (b, 2)),
            pl.BlockSpec((S, 256), lambda b, i: (b, kvb + 5)),
            pl.BlockSpec((TQ, LANE), lambda b, i: (b * nq + i, 0)),
            pl.BlockSpec((n_sel, n_blk), lambda b, i: (0, 0)),
            pl.BlockSpec((S, LANE), lambda b, i: (0, 0)),
            pl.BlockSpec((n_sel, LANE), lambda b, i: (0, 0)),
            pl.BlockSpec((1, 1024), lambda b, i: (0, 0)),
        ],
        out_specs=pl.BlockSpec((TQ, 1024), lambda b, i: (b * nq + i, 0)),
        out_shape=jax.ShapeDtypeStruct((T, 1024), BF16),
        scratch_shapes=[pltpu.VMEM((TQ, 1024), F32), pltpu.VMEM((NSA_GROUPS, NSA_J * TQ, HD), F32),
                        pltpu.VMEM((NSA_GROUPS, NSA_J * TQ, HD), F32),
                        pltpu.VMEM((NSA_GROUPS, NSA_J * TQ, HD), BF16)],
        compiler_params=_cparams(("parallel", "parallel")),
        name="nsa",
    )(posb, invf, proj, kc, vc, k_r, proj, k_r, proj, small, covt, eneg, eye, ng)


ML_HPS = 2
ML_ROWS = 512


def _mlstm_kernel(bias_ref, q_ref, k_ref, v_ref, om_ref, if_ref, cwq_ref, cwk_ref, cbq_ref, cbk_ref,
                  ng_ref, o_ref, c_ref, n_ref, m_ref, xq_ref, xk_ref):
    hp = pl.program_id(1)
    L = ML_CHUNK

    @pl.when(pl.program_id(2) == 0)
    def _():
        c_ref[...] = jnp.zeros_like(c_ref)
        n_ref[...] = jnp.zeros_like(n_ref)
        m_ref[...] = jnp.zeros_like(m_ref)
        xq_ref[:, 0:8, :] = jnp.zeros((ML_HPS, 8, ML_HD), F32)
        xk_ref[:, 0:8, :] = jnp.zeros((ML_HPS, 8, ML_HD), F32)

    def conv_silu(x_ref, hh, rows, xb_ref, w_ref, b_ref):
        cs = slice(hh * ML_HD, (hh + 1) * ML_HD)
        xb_ref[hh, 8:8 + L, :] = x_ref[rows, cs].astype(F32)
        y = b_ref[:, cs] + xb_ref[hh, 8:8 + L, :] * w_ref[CONV_W - 1:CONV_W, cs]
        for k in range(1, CONV_W):
            y = y + xb_ref[hh, 8 - k:8 - k + L, :] * w_ref[CONV_W - 1 - k:CONV_W - k, cs]
        xb_ref[hh, 0:8, :] = xb_ref[hh, L:L + 8, :]
        return _silu(y)

    r = lax.broadcasted_iota(jnp.int32, (L, L), 0)
    cidx = lax.broadcasted_iota(jnp.int32, (L, L), 1)
    tril = cidx <= r
    eye = cidx == r

    def head_chain(hh, rows):
        h = hp * ML_HPS + hh
        cs = slice(hh * ML_HD, (hh + 1) * ML_HD)
        qf = conv_silu(q_ref, hh, rows, xq_ref, cwq_ref, cbq_ref) * (ML_HD ** -0.5)
        yield
        kf = conv_silu(k_ref, hh, rows, xk_ref, cwk_ref, cbk_ref)
        yield
        qb = qf.astype(BF16)
        kb = kf.astype(BF16)
        vb = v_ref[rows, cs]
        vf = vb.astype(F32)

        ic = if_ref[0, hh, 0:1, rows] + bias_ref[0, h]
        fp = if_ref[0, hh, 1:2, rows] + bias_ref[1, h]
        fc = jnp.minimum(fp, 0.0) - jnp.log(1.0 + jnp.exp(-jnp.abs(fp)))

        fc_b = jnp.broadcast_to(fc, (L, L))
        ic_b = jnp.broadcast_to(ic, (L, L))
        b_col = jnp.sum(jnp.where(tril, fc_b, 0.0), axis=1, keepdims=True)
        fc_col = jnp.sum(jnp.where(eye, fc_b, 0.0), axis=1, keepdims=True)
        ic_col = jnp.sum(jnp.where(eye, ic_b, 0.0), axis=1, keepdims=True)
        b_row = jnp.sum(jnp.where(r <= cidx, jnp.broadcast_to(fc_col, (L, L)), 0.0),
                        axis=0, keepdims=True)
        b_last = b_col[L - 1:L, :]
        m_prev = m_ref[hh]
        yield

        d_log = jnp.where(tril, b_col - b_row + ic, NEG_INF)
        inter = b_col + m_prev
        m_t = jnp.maximum(inter, jnp.max(d_log, axis=1, keepdims=True))
        w_intra = jnp.exp(d_log - m_t)
        w_inter = jnp.exp(inter - m_t)
        qk = lax.dot_general(qb, kb, NT_DIMS, preferred_element_type=F32) * w_intra
        yield
        num = (jnp.dot(qk.astype(BF16), vb, preferred_element_type=F32)
               + w_inter * lax.dot_general(qb, c_ref[hh].astype(BF16), NT_DIMS, preferred_element_type=F32))
        den = jnp.sum(qk, axis=1, keepdims=True) + w_inter * jnp.sum(qf * n_ref[hh], axis=1, keepdims=True)
        hm = num / jnp.maximum(jnp.abs(den), jnp.exp(-m_t))
        yield

        w_log = b_last - b_col + ic_col
        m_new = jnp.maximum(b_last + m_prev, jnp.max(w_log, axis=0, keepdims=True))
        w_state = jnp.exp(w_log - m_new)
        decay = jnp.exp(b_last + m_prev - m_new)
        c_ref[hh] = decay * c_ref[hh] + lax.dot_general((w_state * vf).astype(BF16), kb, TN_DIMS,
                                                        preferred_element_type=F32)
        n_ref[hh] = decay * n_ref[hh] + jnp.sum(w_state * kf, axis=0, keepdims=True)
        m_ref[hh] = m_new
        yield

        hn = hm * lax.rsqrt(jnp.mean(hm * hm, axis=1, keepdims=True) + NORM_EPS) * ng_ref[:, cs]
        o_ref[rows, cs] = (hn * _sigmoid(om_ref[rows, cs].astype(F32))).astype(BF16)

    for cc in range(ML_ROWS // L):
        _interleave(*[head_chain(hh, slice(cc * L, (cc + 1) * L)) for hh in range(ML_HPS)])


def _mlstm(proj, if_arr, gate_bias, conv_w, conv_b, ng, B, S):
    T = B * S
    nc = S // ML_ROWS
    W = ML_HPS * ML_HD
    n_hp = ML_HEADS // ML_HPS
    assert COL_QKM % W == 0 and COL_VM % W == 0 and COL_OM % W == 0 and ML_HEADS % ML_HPS == 0
    cq, ck, cv, co = COL_QKM // W, COL_QKM // W + n_hp, COL_VM // W, COL_OM // W
    rows = lambda col0: pl.BlockSpec((ML_ROWS, W), lambda b, h, c: (b * nc + c, col0 + h))
    return pl.pallas_call(
        _mlstm_kernel,
        grid=(B, n_hp, nc),
        in_specs=[
            pl.BlockSpec(memory_space=pltpu.SMEM),
            rows(cq), rows(ck), rows(cv), rows(co),
            pl.BlockSpec((1, ML_HPS, 2, ML_ROWS), lambda b, h, c: (b, h, 0, c)),
            pl.BlockSpec((CONV_W, W), lambda b, h, c: (0, h)),
            pl.BlockSpec((CONV_W, W), lambda b, h, c: (0, n_hp + h)),
            pl.BlockSpec((1, W), lambda b, h, c: (0, h)),
            pl.BlockSpec((1, W), lambda b, h, c: (0, n_hp + h)),
            pl.BlockSpec((1, W), lambda b, h, c: (0, h)),
        ],
        out_specs=pl.BlockSpec((ML_ROWS, W), lambda b, h, c: (b * nc + c, h)),
        out_shape=jax.ShapeDtypeStruct((T, ML_HEADS * ML_HD), BF16),
        scratch_shapes=[
            pltpu.VMEM((ML_HPS, ML_HD, ML_HD), F32), pltpu.VMEM((ML_HPS, 1, ML_HD), F32),
            pltpu.VMEM((ML_HPS, 1, 1), F32),
            pltpu.VMEM((ML_HPS, ML_CHUNK + 8, ML_HD), F32), pltpu.VMEM((ML_HPS, ML_CHUNK + 8, ML_HD), F32),
        ],
        compiler_params=_cparams(("parallel", "parallel", "arbitrary")),
        name="mlstm",
    )(gate_bias, proj, proj, proj, proj, if_arr, conv_w, conv_w, conv_b, conv_b, ng)


def _pack_bf16_pair(lo, hi):
    lo_b = pltpu.bitcast(lo.astype(BF16).astype(F32), U32)
    hi_b = pltpu.bitcast(hi.astype(BF16).astype(F32), U32)
    return (lo_b >> 16) | hi_b


def _unpack_bf16_pair(p):
    lo = pltpu.bitcast(p << 16, F32)
    hi = pltpu.bitcast(p & jnp.uint32(0xFFFF0000), F32)
    return lo, hi


def _row(ref, r):
    return ref.at[pl.ds(r, 1), :]


def _outproj_kernel(nsa_ref, ml_ref, w_ref, x_ref, g2_ref, wr_ref, br_ref, tril_ref,
                    x1_ref, xp_ref, rt_ref, rtt_ref, cnt_ref, carry_ref, xa_ref, xb_ref):
    i = pl.program_id(0)

    @pl.when(i == 0)
    def _():
        carry_ref[...] = jnp.zeros_like(carry_ref)
        xb_ref[...] = jnp.zeros_like(xb_ref)

    half = D_MODEL // 2
    subs = [slice(sub * (TM_OUT // OUT_SUBS), (sub + 1) * (TM_OUT // OUT_SUBS)) for sub in range(OUT_SUBS)]

    def step(cur_ref, prev_ref):
        carry = carry_ref[...]
        carry_box = [carry]

        def chain():
            for rs in subs:
                yield from _outproj_rows(rs, prev_ref, carry_box, g2_ref, wr_ref, br_ref, tril_ref,
                                         x1_ref, xp_ref, rt_ref, rtt_ref)
                yield

        pending = chain()
        for rs in subs:
            for c in range(D_MODEL // OUT_COLS):
                cs = slice(c * OUT_COLS, (c + 1) * OUT_COLS)
                cur_ref[rs, cs] = (x_ref[rs, cs]
                                   + jnp.dot(nsa_ref[rs, :], w_ref[0:half, cs], preferred_element_type=F32)
                                   + jnp.dot(ml_ref[rs, :], w_ref[half:, cs], preferred_element_type=F32))
                next(pending, None)
        for _ in pending:
            pass
        carry = jnp.where(i > 0, carry_box[0], carry)
        carry_ref[...] = carry
        cnt_ref[...] = jnp.broadcast_to(carry, cnt_ref.shape)

    @pl.when(i % 2 == 0)
    def _():
        step(xa_ref, xb_ref)

    @pl.when(i % 2 == 1)
    def _():
        step(xb_ref, xa_ref)


def _outproj_rows(rs, src_ref, carry_box, g2_ref, wr_ref, br_ref, tril_ref, x1_ref, xp_ref, rt_ref, rtt_ref):
    half = D_MODEL // 2
    x1 = src_ref[rs, :]
    x1_ref[rs, :] = x1
    xn = x1 * lax.rsqrt(jnp.mean(x1 * x1, axis=-1, keepdims=True) + NORM_EPS) * g2_ref[...]
    xp_ref[rs, :] = _pack_bf16_pair(xn[:, :half], xn[:, half:])
    logits = jnp.dot(xn.astype(BF16), wr_ref[...], preferred_element_type=F32) + br_ref[...]
    yield

    tm = logits.shape[0]
    lane = lax.broadcasted_iota(jnp.int32, (tm, LANE), 1)
    lane_f = lane.astype(F32)
    big = float(LANE)
    gmask = lane < MOE_GROUPS
    gmax = jnp.max(jnp.where(gmask, logits, NEG_INF), axis=1, keepdims=True)
    ge = jnp.where(gmask, jnp.exp(logits - gmax), 0.0)
    gp = ge / jnp.sum(ge, axis=1, keepdims=True)
    g_w = jnp.max(gp, axis=1, keepdims=True)
    g_idx = jnp.min(jnp.where(gmask, jnp.where(gp == g_w, lane_f, big), big), axis=1, keepdims=True)
    yield
    grp_of_lane = ((lane - MOE_GROUPS) >> EPG_LOG2).astype(F32)
    emask = jnp.where(lane >= MOE_GROUPS, grp_of_lane, -1.0) == g_idx
    emax = jnp.max(jnp.where(emask, logits, NEG_INF), axis=1, keepdims=True)
    ee = jnp.where(emask, jnp.exp(logits - emax), 0.0)
    ep = jnp.where(emask, ee / jnp.sum(ee, axis=1, keepdims=True), -1.0)
    v1 = jnp.max(ep, axis=1, keepdims=True)
    i1 = jnp.min(jnp.where(ep == v1, lane_f, big), axis=1, keepdims=True)
    ep2 = jnp.where(lane_f == i1, -1.0, ep)
    v2 = jnp.max(ep2, axis=1, keepdims=True)
    i2 = jnp.min(jnp.where(ep2 == v2, lane_f, big), axis=1, keepdims=True)
    w0 = g_w * v1 / (v1 + v2)
    w1 = g_w * v2 / (v1 + v2)
    e0 = i1 - float(MOE_GROUPS)
    e1 = i2 - float(MOE_GROUPS)

    yield
    oh0 = jnp.where(lane_f == e0, 1.0, 0.0)
    oh1 = jnp.where(lane_f == e1, 1.0, 0.0)
    tril = tril_ref[0:tm, 0:tm]
    pre0 = jnp.dot(tril, oh0.astype(BF16), preferred_element_type=F32)
    pre1 = jnp.dot(tril, oh1.astype(BF16), preferred_element_type=F32)
    yield
    carry = carry_box[0]
    tot0 = pre0[tm - 1:tm, :]
    tot1 = pre1[tm - 1:tm, :]
    rank0 = jnp.sum(oh0 * (pre0 - 1.0 + carry), axis=1, keepdims=True)
    rank1 = jnp.sum(oh1 * (pre1 - 1.0 + carry + tot0), axis=1, keepdims=True)

    rt = jnp.where(lane == 0, e0, jnp.where(lane == 1, e1, jnp.where(lane == 2, w0, jnp.where(
        lane == 3, w1, jnp.where(lane == 4, rank0, jnp.where(lane == 5, rank1, 0.0))))))
    rt_ref[rs, :] = rt
    rtt_ref[:, rs] = jnp.transpose(rt)
    carry_box[0] = carry + tot0 + tot1


def _out_proj(nsa_o, ml_o, w_out, x2, g2, w_r, b_r, tril):
    T = x2.shape[0]
    half = D_MODEL // 2
    n_tiles = T // TM_OUT
    cur = lambda i: (jnp.minimum(i, n_tiles - 1), 0)
    prev = lambda i: (jnp.maximum(i - 1, 0), 0)
    return pl.pallas_call(
        _outproj_kernel,
        grid=(n_tiles + 1,),
        in_specs=[
            pl.BlockSpec((TM_OUT, half), cur),
            pl.BlockSpec((TM_OUT, half), cur),
            pl.BlockSpec((D_MODEL, D_MODEL), lambda i: (0, 0)),
            pl.BlockSpec((TM_OUT, D_MODEL), cur),
            pl.BlockSpec((1, D_MODEL), lambda i: (0, 0)),
            pl.BlockSpec((D_MODEL, LANE), lambda i: (0, 0)),
            pl.BlockSpec((1, LANE), lambda i: (0, 0)),
            pl.BlockSpec((TM_OUT, TM_OUT), lambda i: (0, 0)),
        ],
        out_specs=[
            pl.BlockSpec((TM_OUT, D_MODEL), prev),
            pl.BlockSpec((TM_OUT, half), prev),
            pl.BlockSpec((TM_OUT, LANE), prev),
            pl.BlockSpec((LANE, TM_OUT), lambda i: (0, jnp.maximum(i - 1, 0))),
            pl.BlockSpec((8, LANE), lambda i: (0, 0)),
        ],
        out_shape=[
            jax.ShapeDtypeStruct((T, D_MODEL), F32),
            jax.ShapeDtypeStruct((T, half), U32),
            jax.ShapeDtypeStruct((T, LANE), F32),
            jax.ShapeDtypeStruct((LANE, T), F32),
            jax.ShapeDtypeStruct((8, LANE), F32),
        ],
        scratch_shapes=[pltpu.VMEM((1, LANE), F32), pltpu.VMEM((TM_OUT, D_MODEL), F32),
                        pltpu.VMEM((TM_OUT, D_MODEL), F32)],
        compiler_params=_cparams(("arbitrary",)),
        name="out_proj",
    )(nsa_o, ml_o, w_out, x2, g2, w_r, b_r, tril)


TD = 256
DISPATCH_BUFS = 3
DISPATCH_UNROLL = 8


def _dispatch_kernel(d0_ref, d1_ref, zblk_ref, meta_ref, xp_hbm, xs_hbm, zbuf_ref, zsem_ref, tbuf_ref,
                     lsem_ref, rsem_ref):
    dests = (d0_ref, d1_ref)
    n_assign = 2 * d0_ref.shape[0]
    nb = xs_hbm.shape[0] // BM
    n_used = meta_ref[0]
    zbuf_ref[...] = jnp.zeros_like(zbuf_ref)

    def zero_block(blk):
        return pltpu.make_async_copy(zbuf_ref, xs_hbm.at[pl.ds(blk * BM, BM), :], zsem_ref.at[0])

    def for_zero_blocks(fn):
        for e in range(N_EXPERTS):
            @pl.when(zblk_ref[e] >= 0)
            def _(e=e):
                fn(zero_block(zblk_ref[e]))

        def body(blk, _):
            fn(zero_block(blk))
            return 0
        lax.fori_loop(n_used, nb, body, 0)

    for_zero_blocks(lambda c: c.start())
    for_zero_blocks(lambda c: c.wait())

    n_tiles = n_assign // (2 * TD)

    def load(tile, slot):
        return pltpu.make_async_copy(xp_hbm.at[pl.ds(tile * TD, TD), :], tbuf_ref.at[slot], lsem_ref.at[slot])

    def wait_scatter(slot):
        for _ in range(2):
            pltpu.make_async_copy(tbuf_ref.at[slot], tbuf_ref.at[slot], rsem_ref.at[slot]).wait()

    load(0, 0).start()

    def tile_step(i, _):
        slot = i % DISPATCH_BUFS
        nslot = (i + 1) % DISPATCH_BUFS

        @pl.when(i + 1 < n_tiles)
        def _():
            @pl.when(i + 1 >= DISPATCH_BUFS)
            def _():
                wait_scatter(nslot)
            load(i + 1, nslot).start()

        load(i, slot).wait()

        def body(j, _):
            for u in range(DISPATCH_UNROLL):
                r = j * DISPATCH_UNROLL + u
                for k in range(2):
                    d = dests[k][i * TD + r]
                    pltpu.make_async_copy(_row(tbuf_ref.at[slot], r), _row(xs_hbm, d),
                                          rsem_ref.at[slot]).start(priority=k)
            return 0
        lax.fori_loop(0, TD // DISPATCH_UNROLL, body, 0)
        return 0

    lax.fori_loop(0, n_tiles, tile_step, 0)
    for back in range(min(DISPATCH_BUFS, n_tiles)):
        wait_scatter((n_tiles - 1 - back) % DISPATCH_BUFS)


def _dispatch(dest0, dest1, zblk, meta, xp, n_rows):
    half = D_MODEL // 2
    assert dest0.shape == dest1.shape and dest0.shape[0] % TD == 0 and dest0.shape[0] // TD >= DISPATCH_BUFS
    grid_spec = pltpu.PrefetchScalarGridSpec(
        num_scalar_prefetch=4,
        grid=(1,),
        in_specs=[pl.BlockSpec(memory_space=pl.ANY)],
        out_specs=pl.BlockSpec(memory_space=pl.ANY),
        scratch_shapes=[
            pltpu.VMEM((BM, half), U32),
            pltpu.SemaphoreType.DMA((1,)),
            pltpu.VMEM((DISPATCH_BUFS, TD, half), U32),
            pltpu.SemaphoreType.DMA((DISPATCH_BUFS,)),
            pltpu.SemaphoreType.DMA((DISPATCH_BUFS,)),
        ],
    )
    return pl.pallas_call(
        _dispatch_kernel,
        grid_spec=grid_spec,
        out_shape=jax.ShapeDtypeStruct((n_rows, half), U32),
        compiler_params=_cparams(("arbitrary",)),
        name="dispatch",
    )(dest0, dest1, zblk, meta, xp)


def _expert_kernel(be_ref, nxt_ref, par_ref, meta_ref, x_ref, wg_hbm, wu_hbm, wd_hbm, y_ref,
                   wsg_ref, wsu_ref, wsd_ref, wsem_ref, wgb_ref, wub_ref, wdb_ref):
    i = pl.program_id(0)
    n_used = meta_ref[0]
    half = D_MODEL // 2

    def weight_copies(e, p):
        return (pltpu.make_async_copy(wg_hbm.at[e], wsg_ref.at[p], wsem_ref.at[p, 0]),
                pltpu.make_async_copy(wu_hbm.at[e], wsu_ref.at[p], wsem_ref.at[p, 1]),
                pltpu.make_async_copy(wd_hbm.at[e], wsd_ref.at[p], wsem_ref.at[p, 2]))

    def succ(e):
        return jnp.where(e >= 0, nxt_ref[jnp.maximum(e, 0)], -1)

    def start_weights(e, p):
        @pl.when(e >= 0)
        def _():
            for c in weight_copies(e, p):
                c.start(priority=1)

    def wait_staged(p):
        for c in weight_copies(0, p):
            c.wait()

    def cast_staged(p):
        wgb_ref[p] = wsg_ref[p].astype(BF16)
        wub_ref[p] = wsu_ref[p].astype(BF16)
        wdb_ref[p] = wsd_ref[p].astype(BF16)

    @pl.when(i == 0)
    def _():
        e0 = be_ref[0]
        start_weights(e0, 0)
        start_weights(succ(e0), 1)
        wait_staged(0)
        cast_staged(0)
        start_weights(succ(succ(e0)), 0)

    def step(p, cast_next):
        if cast_next:
            wait_staged(1 - p)
        lo, hi = _unpack_bf16_pair(x_ref[...])
        xl = lo.astype(BF16)
        xh = hi.astype(BF16)
        gt = (jnp.dot(xl, wgb_ref[p, 0:half, :], preferred_element_type=F32)
              + jnp.dot(xh, wgb_ref[p, half:, :], preferred_element_type=F32))
        up = (jnp.dot(xl, wub_ref[p, 0:half, :], preferred_element_type=F32)
              + jnp.dot(xh, wub_ref[p, half:, :], preferred_element_type=F32))
        hb = (_silu(gt) * up).astype(BF16)
        y = jnp.dot(hb, wdb_ref[p], preferred_element_type=F32)
        y_ref[...] = _pack_bf16_pair(y[:, :half], y[:, half:])
        if cast_next:
            cast_staged(1 - p)
            start_weights(succ(succ(succ(be_ref[i]))), 1 - p)

    e = be_ref[jnp.minimum(i, n_used - 1)]
    ends_expert = (i + 1 < n_used) & (be_ref[jnp.minimum(i + 1, n_used - 1)] != e)
    for p in range(2):
        for cast_next in (False, True):
            @pl.when((i < n_used) & (par_ref[i] == p)
                     & (ends_expert if cast_next else jnp.logical_not(ends_expert)))
            def _(p=p, cast_next=cast_next):
                step(p, cast_next)

    @pl.when(i >= n_used)
    def _():
        y_ref[...] = jnp.zeros_like(y_ref)


def _experts(block_expert, next_expert, block_parity, meta, xs, w_gate, w_up, w_down):
    half = D_MODEL // 2
    n_rows = xs.shape[0]
    nb = n_rows // BM
    grid_spec = pltpu.PrefetchScalarGridSpec(
        num_scalar_prefetch=4,
        grid=(nb,),
        in_specs=[
            pl.BlockSpec((BM, half), lambda i, be, nxt, par, meta: (jnp.minimum(i, meta[0] - 1), 0)),
            pl.BlockSpec(memory_space=pl.ANY), pl.BlockSpec(memory_space=pl.ANY),
            pl.BlockSpec(memory_space=pl.ANY),
        ],
        out_specs=pl.BlockSpec((BM, half), lambda i, be, nxt, par, meta: (i, 0)),
        scratch_shapes=[
            pltpu.VMEM((2, D_MODEL, D_EXPERT), F32),
            pltpu.VMEM((2, D_MODEL, D_EXPERT), F32),
            pltpu.VMEM((2, D_EXPERT, D_MODEL), F32),
            pltpu.SemaphoreType.DMA((2, 3)),
            pltpu.VMEM((2, D_MODEL, D_EXPERT), BF16),
            pltpu.VMEM((2, D_MODEL, D_EXPERT), BF16),
            pltpu.VMEM((2, D_EXPERT, D_MODEL), BF16),
        ],
    )
    return pl.pallas_call(
        _expert_kernel,
        grid_spec=grid_spec,
        out_shape=jax.ShapeDtypeStruct((n_rows, half), U32),
        compiler_params=_cparams(("arbitrary",)),
        name="experts",
    )(block_expert, next_expert, block_parity, meta, xs, w_gate, w_up, w_down)


def _combine_kernel(d0_ref, d1_ref, y_hbm, x1_ref, rt_ref, fg_ref, o_ref, ya_ref, yb_ref, sem_ref):
    i = pl.program_id(0)
    nt = pl.num_programs(0)
    half = D_MODEL // 2
    bufs = (ya_ref, yb_ref)
    dests = (d0_ref, d1_ref)

    def row_copy(tile, slot, r_tile, s, k):
        r = r_tile * 8 + s
        d = dests[k][tile * TC + r]
        return pltpu.make_async_copy(_row(y_hbm, d), _row(bufs[slot].at[k], r), sem_ref.at[slot])

    def wait_rows(slot):
        pltpu.make_async_copy(bufs[slot], bufs[slot], sem_ref.at[slot]).wait()

    def step(slot):
        wait_rows(slot)
        for r in range(TC):
            for k in range(2):
                row_copy(i + 1, 1 - slot, r // 8, r % 8, k).start(priority=k)
        rt = rt_ref[...]
        w0 = rt[:, 2:3]
        w1 = rt[:, 3:4]
        lo0, hi0 = _unpack_bf16_pair(bufs[slot][0])
        lo1, hi1 = _unpack_bf16_pair(bufs[slot][1])
        xl = x1_ref[:, :half] + (w0 * lo0 + w1 * lo1)
        xh = x1_ref[:, half:] + (w0 * hi0 + w1 * hi1)
        ms = (jnp.sum(xl * xl, axis=1, keepdims=True) + jnp.sum(xh * xh, axis=1, keepdims=True)) / float(D_MODEL)
        inv = lax.rsqrt(ms + NORM_EPS)
        o_ref[:, :half] = xl * inv * fg_ref[:, :half]
        o_ref[:, half:] = xh * inv * fg_ref[:, half:]

        @pl.when(i == nt - 1)
        def _():
            wait_rows(1 - slot)

    @pl.when(i == 0)
    def _():
        def body(rt, _):
            for s in range(8):
                for k in range(2):
                    row_copy(0, 0, rt, s, k).start(priority=k)
            return 0
        lax.fori_loop(0, TC // 8, body, 0)

    @pl.when(i % 2 == 0)
    def _():
        step(0)

    @pl.when(i % 2 == 1)
    def _():
        step(1)


def _combine(dest0, dest1, ys, x1, route, fg):
    T = x1.shape[0]
    half = D_MODEL // 2
    grid_spec = pltpu.PrefetchScalarGridSpec(
        num_scalar_prefetch=2,
        grid=(T // TC,),
        in_specs=[
            pl.BlockSpec(memory_space=pl.ANY),
            pl.BlockSpec((TC, D_MODEL), lambda i, d0, d1: (i, 0)),
            pl.BlockSpec((TC, LANE), lambda i, d0, d1: (i, 0)),
            pl.BlockSpec((1, D_MODEL), lambda i, d0, d1: (0, 0)),
        ],
        out_specs=pl.BlockSpec((TC, D_MODEL), lambda i, d0, d1: (i, 0)),
        scratch_shapes=[
            pltpu.VMEM((2, TC, half), U32),
            pltpu.VMEM((2, TC, half), U32),
            pltpu.SemaphoreType.DMA((2,)),
        ],
    )
    return pl.pallas_call(
        _combine_kernel,
        grid_spec=grid_spec,
        out_shape=jax.ShapeDtypeStruct((T, D_MODEL), F32),
        compiler_params=_cparams(("arbitrary",)),
        name="combine",
    )(dest0, dest1, ys, x1, route, fg)


def _cover_matrix(S):
    n_blk = S // CMP_STRIDE
    n_sel = S // SEL_BLOCK
    cs = np.arange(n_blk) * CMP_STRIDE
    ss = np.arange(n_sel) * SEL_BLOCK
    shared = np.minimum(cs[:, None] + CMP_LEN, ss[None, :] + SEL_BLOCK) - np.maximum(cs[:, None], ss[None, :])
    return (np.clip(shared, 0, None) / CMP_LEN).T.astype(np.float32)


def _block_mask_matrix(S):
    n_sel = S // SEL_BLOCK
    assert n_sel <= LANE
    e = np.zeros((S, LANE), np.float32)
    e[np.arange(S), np.arange(S) // SEL_BLOCK] = NEG_INF
    return e


def _inv_freq_row():
    inv = np.power(np.float32(ROPE_THETA), -np.arange(ROPE_HALF, dtype=np.float32) * 2.0 / ROPE_DIM)
    return np.tile(inv, ROPE_SLOTS).reshape(1, LANE).astype(np.float32)


def _packed_positions(positions, T):
    rows = TS_ROPE // ROPE_SLOTS
    p = positions.reshape(T // TS_ROPE, ROPE_SLOTS, rows).transpose(0, 2, 1).astype(F32)
    return jnp.repeat(p, ROPE_HALF, axis=2).reshape(T // ROPE_SLOTS, LANE)


def _layer(x2, positions, B, S, norm1_g, w_in, cmp_pe_k, cmp_pe_v, cmp_wk1, cmp_wk2, cmp_wv1, cmp_wv2,
           nsa_norm_g, conv_w, conv_b, b_igate, b_fgate, mlstm_norm_g, w_out, norm2_g,
           w_group, b_group, w_router, b_router, w_exp_gate, w_exp_up, w_exp_down, out_norm_g):
    T = B * S
    w_main, w_small = _weight_prep(jnp.transpose(w_in[0]))
    posb = _packed_positions(positions, T)
    invf = jnp.asarray(_inv_freq_row())

    proj, small = _in_proj(x2, norm1_g.reshape(1, -1), w_main, w_small)
    k_r = _rope(posb, invf, proj)
    kc, vc = _compress(k_r, proj, B, S, cmp_pe_k, cmp_pe_v, cmp_wk1.astype(BF16), cmp_wk2.astype(BF16),
                       cmp_wv1.astype(BF16), cmp_wv2.astype(BF16))
    eye = np.eye(S // SEL_BLOCK, LANE, dtype=np.float32)
    nsa_o = _nsa(posb, invf, k_r, proj, small, kc, vc, jnp.asarray(_cover_matrix(S), BF16),
                 jnp.asarray(_block_mask_matrix(S), BF16), jnp.asarray(eye, BF16),
                 nsa_norm_g.reshape(1, -1), B, S)

    if_arr = small[:, 24:32].reshape(B, S, 2, ML_HEADS).transpose(0, 3, 2, 1)
    gate_bias = jnp.stack([b_igate, b_fgate]).astype(F32)
    ml_o = _mlstm(proj, if_arr, gate_bias, conv_w, conv_b.reshape(1, -1), mlstm_norm_g.reshape(1, -1), B, S)

    w_r = jnp.concatenate([w_group, w_router, jnp.zeros((D_MODEL, LANE - MOE_GROUPS - N_EXPERTS), F32)],
                          axis=1).astype(BF16)
    b_r = jnp.concatenate([b_group, b_router, jnp.zeros((LANE - MOE_GROUPS - N_EXPERTS,), F32)]).reshape(1, LANE)
    tril = jnp.asarray(np.tril(np.ones((TM_OUT, TM_OUT), np.float32)), BF16)
    x1, xp, route, route_t, cnt = _out_proj(nsa_o, ml_o, w_out.astype(BF16), x2, norm2_g.reshape(1, -1),
                                            w_r, b_r, tril)

    n_rows = T * 2 + N_EXPERTS * BM
    counts = cnt[0, :N_EXPERTS].astype(jnp.int32)
    pcounts = (counts + BM - 1) // BM * BM
    pends = jnp.cumsum(pcounts)
    pstarts = pends - pcounts
    start_blk = (pstarts // BM).astype(F32)

    def sorted_row(k):
        onehot = (route_t[k].astype(jnp.int32)[:, None] == jnp.arange(N_EXPERTS, dtype=jnp.int32)).astype(F32)
        return (jnp.einsum('te,e->t', onehot, start_blk).astype(jnp.int32) * BM
                + route_t[4 + k].astype(jnp.int32))

    dests = (sorted_row(0), sorted_row(1))
    block_row0 = jnp.arange(n_rows // BM, dtype=jnp.int32) * BM
    block_expert = jnp.minimum(jnp.sum((pends[None, :] <= block_row0[:, None]).astype(jnp.int32), axis=1),
                               N_EXPERTS - 1).astype(jnp.int32)
    present = jnp.where(counts > 0, jnp.arange(N_EXPERTS, dtype=jnp.int32), N_EXPERTS)
    later = jnp.concatenate([lax.cummin(present[::-1])[::-1][1:], jnp.full((1,), N_EXPERTS, jnp.int32)])
    next_expert = jnp.where(later < N_EXPERTS, later, -1).astype(jnp.int32)
    meta = (pends[-1:] // BM).astype(jnp.int32)

    last_block = jnp.where(counts > 0, pends // BM - 1, -1).astype(jnp.int32)

    xs = _dispatch(dests[0], dests[1], last_block, meta, xp, n_rows)
    ordinal = jnp.cumsum((counts > 0).astype(jnp.int32)) - 1
    block_parity = (ordinal[block_expert] % 2).astype(jnp.int32)
    ys = _experts(block_expert, next_expert, block_parity, meta, xs, w_exp_gate, w_exp_up, w_exp_down)
    pad = jnp.zeros((TC,), jnp.int32)
    return _combine(jnp.concatenate([dests[0], pad]), jnp.concatenate([dests[1], pad]), ys, x1, route,
                    out_norm_g.reshape(1, -1))


def kernel(x, positions, norm1_g, w_in, cmp_pe_k, cmp_pe_v, cmp_wk1, cmp_wk2, cmp_wv1, cmp_wv2, nsa_norm_g,
           conv_w, conv_b, b_igate, b_fgate, mlstm_norm_g, w_out, norm2_g, w_group, b_group, w_router,
           b_router, w_exp_gate, w_exp_up, w_exp_down, final_norm_g):
    B, S, D = x.shape
    assert D == D_MODEL and norm1_g.shape[0] == 1, "single-layer, D_MODEL-wide configuration only"
    assert S % ML_ROWS == 0 and ML_ROWS % ML_CHUNK == 0 and S % SEL_BUCKET == 0 and S >= WIN_KEYS and (B * S) % TM_IN == 0
    out = _layer(x.reshape(B * S, D), positions, B, S, norm1_g[0], w_in, cmp_pe_k[0], cmp_pe_v[0],
                 cmp_wk1[0], cmp_wk2[0], cmp_wv1[0], cmp_wv2[0], nsa_norm_g[0], conv_w[0], conv_b[0],
                 b_igate[0], b_fgate[0], mlstm_norm_g[0], w_out[0], norm2_g[0], w_group[0], b_group[0],
                 w_router[0], b_router[0], w_exp_gate[0], w_exp_up[0], w_exp_down[0], final_norm_g)
    return out.reshape(B, S, D)
```

```python
import functools

import numpy as np
import jax
import jax.numpy as jnp
from jax import lax
from jax.experimental import pallas as pl
from jax.experimental.pallas import tpu as pltpu

F32 = jnp.float32
BF16 = jnp.bfloat16
U32 = jnp.uint32

D_MODEL = 2048
NSA_HEADS = 8
NSA_GROUPS = 2
NSA_J = NSA_HEADS // NSA_GROUPS
HD = 128
CMP_LEN = 32
CMP_STRIDE = 16
SEL_BLOCK = 64
SEL_TOPK = 8
WINDOW = 512
ROPE_THETA = 500000.0
ROPE_DIM = 32
ROPE_HALF = 16
ML_HEADS = 4
ML_HD = 256
CONV_W = 4
MOE_GROUPS = 8
EPG = 8
N_EXPERTS = 64
D_EXPERT = 512
SEL_BLOCK_LOG2 = 6
EPG_LOG2 = 3
assert (1 << SEL_BLOCK_LOG2) == SEL_BLOCK and (1 << EPG_LOG2) == EPG
NORM_EPS = 1e-6
NEG_INF = -1e30
FORCE_SCORE = 1000.0

COL_Q = 0
COL_KV = 1024
COL_QKM = 2560
COL_VM = 4608
COL_OM = 5632
N_MAIN = 6656

LANE = 128
VMEM_LIMIT = 56 * 1024 * 1024

TM_IN = 1024
TN_IN = 1664
TS_ROPE = 256
TQ = 256
SEL_BUCKET = 256
WIN_KEYS = WINDOW + TQ
ML_CHUNK = 256
TM_OUT = 512
OUT_SUBS = 2
OUT_COLS = 512
BM = 256
TC = 256

NT_DIMS = (((1,), (1,)), ((), ()))
TN_DIMS = (((0,), (0,)), ((), ()))


def _cparams(sem):
    return pltpu.CompilerParams(dimension_semantics=sem, vmem_limit_bytes=VMEM_LIMIT)


def _sigmoid(x):
    return 0.5 * jnp.tanh(0.5 * x) + 0.5


def _interleave(*chains):
    live = list(chains)
    while live:
        for g in list(live):
            try:
                next(g)
            except StopIteration:
                live.remove(g)


def _silu(x):
    h = 0.5 * x
    return h + h * jnp.tanh(h)


W_GATES = 2560
W_QKM = 2584
W_IGATE = 6680
W_END = 6688
TR_PREP = 512
assert W_GATES % TR_PREP == 0 and N_MAIN % TR_PREP == 0 and W_IGATE - W_QKM == N_MAIN - W_GATES


def _wprep_kernel(wt_hbm, wm_ref, ws_ref, wbuf_ref, sbuf_ref, sem_ref, ssem_ref):
    i = pl.program_id(0)
    n_gate = W_QKM - W_GATES
    n_if = W_END - W_IGATE

    def load(blk, slot):
        start = pl.multiple_of(blk * TR_PREP + jnp.where(blk * TR_PREP >= W_GATES, n_gate, 0), 8)
        return pltpu.make_async_copy(wt_hbm.at[pl.ds(start, TR_PREP), :], wbuf_ref.at[slot], sem_ref.at[slot])

    def small_loads():
        return (pltpu.make_async_copy(wt_hbm.at[pl.ds(W_GATES, n_gate), :], sbuf_ref.at[pl.ds(0, n_gate), :],
                                      ssem_ref.at[0]),
                pltpu.make_async_copy(wt_hbm.at[pl.ds(W_IGATE, n_if), :], sbuf_ref.at[pl.ds(n_gate, n_if), :],
                                      ssem_ref.at[1]))

    @pl.when(i == 0)
    def _():
        load(0, 0).start()
        for c in small_loads():
            c.start()
        for c in small_loads():
            c.wait()
        ws_ref[...] = jnp.zeros_like(ws_ref)
        ws_ref[0:n_gate + n_if, :] = sbuf_ref[...].astype(BF16)

    @pl.when(i + 1 < pl.num_programs(0))
    def _():
        load(i + 1, (i + 1) % 2).start()

    load(i, i % 2).wait()
    wm_ref[...] = wbuf_ref[i % 2].astype(BF16)


def _weight_prep(wt):
    assert wt.shape == (W_END, D_MODEL)
    n_small = (W_QKM - W_GATES) + (W_END - W_IGATE)
    return pl.pallas_call(
        _wprep_kernel,
        grid=(N_MAIN // TR_PREP,),
        in_specs=[pl.BlockSpec(memory_space=pl.ANY)],
        out_specs=[pl.BlockSpec((TR_PREP, D_MODEL), lambda i: (i, 0)),
                   pl.BlockSpec((LANE, D_MODEL), lambda i: (0, 0))],
        out_shape=[jax.ShapeDtypeStruct((N_MAIN, D_MODEL), BF16),
                   jax.ShapeDtypeStruct((LANE, D_MODEL), BF16)],
        scratch_shapes=[pltpu.VMEM((2, TR_PREP, D_MODEL), F32), pltpu.VMEM((n_small, D_MODEL), F32),
                        pltpu.SemaphoreType.DMA((2,)), pltpu.SemaphoreType.DMA((2,))],
        compiler_params=_cparams(("arbitrary",)),
        name="weight_prep",
    )(wt)


def _inproj_kernel(x_ref, g_ref, w_ref, ws_ref, o_ref, os_ref, ost_ref, h_ref):
    @pl.when(pl.program_id(1) == 0)
    def _():
        x = x_ref[...]
        ms = jnp.mean(x * x, axis=-1, keepdims=True)
        h_ref[...] = (x * lax.rsqrt(ms + NORM_EPS) * g_ref[...]).astype(BF16)
        small = lax.dot_general(h_ref[...], ws_ref[...], NT_DIMS, preferred_element_type=F32)
        os_ref[...] = small
        ost_ref[...] = jnp.transpose(small)

    o_ref[...] = lax.dot_general(h_ref[...], w_ref[...], NT_DIMS,
                                 preferred_element_type=F32).astype(o_ref.dtype)


def _in_proj(x2, g1, w_main, w_small):
    T = x2.shape[0]
    return pl.pallas_call(
        _inproj_kernel,
        grid=(T // TM_IN, N_MAIN // TN_IN),
        in_specs=[
            pl.BlockSpec((TM_IN, D_MODEL), lambda m, n: (m, 0)),
            pl.BlockSpec((1, D_MODEL), lambda m, n: (0, 0)),
            pl.BlockSpec((TN_IN, D_MODEL), lambda m, n: (n, 0)),
            pl.BlockSpec((LANE, D_MODEL), lambda m, n: (0, 0)),
        ],
        out_specs=[
            pl.BlockSpec((TM_IN, TN_IN), lambda m, n: (m, n)),
            pl.BlockSpec((TM_IN, LANE), lambda m, n: (m, 0)),
            pl.BlockSpec((LANE, TM_IN), lambda m, n: (0, m)),
        ],
        out_shape=[
            jax.ShapeDtypeStruct((T, N_MAIN), BF16),
            jax.ShapeDtypeStruct((T, LANE), F32),
            jax.ShapeDtypeStruct((LANE, T), F32),
        ],
        scratch_shapes=[pltpu.VMEM((TM_IN, D_MODEL), BF16)],
        compiler_params=_cparams(("parallel", "arbitrary")),
        name="in_proj",
    )(x2, g1, w_main, w_small)


ROPE_SLOTS = LANE // ROPE_HALF


def _rope_slots(pos_ref, invf_ref):
    ang = pos_ref[...] * invf_ref[...]
    c = jnp.cos(ang)
    s = jnp.sin(ang)
    rows = ang.shape[0]
    lane = lax.broadcasted_iota(jnp.int32, ang.shape, 1)
    lo = lane < ROPE_HALF
    mid = lane < ROPE_DIM

    def lanes_from(x, src):
        shift = (-src) % LANE
        return x if shift == 0 else pltpu.roll(x, shift, 1)

    for slot in range(ROPE_SLOTS):
        src = slot * ROPE_HALF
        cf = jnp.where(lo, lanes_from(c, src), jnp.where(mid, lanes_from(c, src - ROPE_HALF), 1.0))
        sa = jnp.where(lo, -lanes_from(s, src), 0.0)
        sb = jnp.where(lo, 0.0, jnp.where(mid, lanes_from(s, src - ROPE_HALF), 0.0))

        def rope(x, cf=cf, sa=sa, sb=sb):
            return x * cf + pltpu.roll(x, LANE - ROPE_HALF, 1) * sa + pltpu.roll(x, ROPE_HALF, 1) * sb

        yield slice(slot * rows, (slot + 1) * rows), rope


ROPE_TILES = 4


def _rope_kernel(pos_ref, invf_ref, kc_ref, ks_ref, kw_ref, ko_ref):
    prow = TS_ROPE // ROPE_SLOTS
    for t in range(ROPE_TILES):
        for rs, rope in _rope_slots(pos_ref.at[t * prow:(t + 1) * prow, :], invf_ref):
            rs = slice(t * TS_ROPE + rs.start, t * TS_ROPE + rs.stop)
            for i, r in enumerate((kc_ref, ks_ref, kw_ref)):
                for g in range(NSA_GROUPS):
                    sl = slice(g * HD, (g + 1) * HD)
                    so = slice(i * 2 * HD + g * HD, i * 2 * HD + (g + 1) * HD)
                    ko_ref[rs, so] = rope(r[rs, sl].astype(F32)).astype(BF16)


def _rope(posb, invf, proj):
    T = proj.shape[0]
    kvb = COL_KV // 256
    rows = ROPE_TILES * TS_ROPE
    assert T % rows == 0
    return pl.pallas_call(
        _rope_kernel,
        grid=(T // rows,),
        in_specs=[
            pl.BlockSpec((rows // ROPE_SLOTS, LANE), lambda i: (i, 0)),
            pl.BlockSpec((1, LANE), lambda i: (0, 0)),
            pl.BlockSpec((rows, 256), lambda i: (i, kvb + 0)),
            pl.BlockSpec((rows, 256), lambda i: (i, kvb + 2)),
            pl.BlockSpec((rows, 256), lambda i: (i, kvb + 4)),
        ],
        out_specs=pl.BlockSpec((rows, 768), lambda i: (i, 0)),
        out_shape=jax.ShapeDtypeStruct((T, 768), BF16),
        compiler_params=_cparams(("parallel",)),
        name="rope",
    )(posb, invf, proj, proj, proj)


def _compress_kernel(k_ref, v_ref, pek_ref, pev_ref, w1k_ref, w2k_ref, w1v_ref, w2v_ref,
                     kc_ref, vc_ref, xs_ref):
    S = k_ref.shape[0]
    n_blk = S // CMP_STRIDE
    for src, pe, w1, w2, dst in ((k_ref, pek_ref, w1k_ref, w2k_ref, kc_ref),
                                 (v_ref, pev_ref, w1v_ref, w2v_ref, vc_ref)):
        xs_ref[0:S, :] = src[...].astype(F32)
        xs_ref[S:S + CMP_LEN, :] = jnp.zeros((CMP_LEN, HD), F32)
        acc = jnp.zeros((n_blk, HD), F32)
        for l in range(CMP_LEN):
            a = xs_ref[pl.ds(l, n_blk, stride=CMP_STRIDE), :] + pe[l:l + 1, :]
            acc = acc + jnp.dot(a.astype(BF16), w1[l * HD:(l + 1) * HD, :],
                                preferred_element_type=F32)
        hid = _silu(acc)
        out = jnp.dot(hid.astype(BF16), w2[...], preferred_element_type=F32)
        dst[0, 0] = out.astype(BF16)


def _compress(k_r, proj, B, S, pek, pev, w1k, w2k, w1v, w2v):
    n_blk = S // CMP_STRIDE
    vcol = (COL_KV + 256) // HD
    full = lambda shape: pl.BlockSpec(shape, lambda b, g: tuple(0 for _ in shape))
    return pl.pallas_call(
        _compress_kernel,
        grid=(B, NSA_GROUPS),
        in_specs=[
            pl.BlockSpec((S, HD), lambda b, g: (b, g)),
            pl.BlockSpec((S, HD), lambda b, g: (b, vcol + g)),
            full((CMP_LEN, HD)), full((CMP_LEN, HD)),
            full((CMP_LEN * HD, HD)), full((HD, HD)),
            full((CMP_LEN * HD, HD)), full((HD, HD)),
        ],
        out_specs=[
            pl.BlockSpec((1, 1, n_blk, HD), lambda b, g: (b, g, 0, 0)),
            pl.BlockSpec((1, 1, n_blk, HD), lambda b, g: (b, g, 0, 0)),
        ],
        out_shape=[
            jax.ShapeDtypeStruct((B, NSA_GROUPS, n_blk, HD), BF16),
            jax.ShapeDtypeStruct((B, NSA_GROUPS, n_blk, HD), BF16),
        ],
        scratch_shapes=[pltpu.VMEM((S + CMP_LEN, HD), F32)],
        compiler_params=_cparams(("parallel", "parallel")),
        name="compress",
    )(k_r, proj, pek, pev, w1k, w2k, w1v, w2v)


def _nsa_kernel(pos_ref, invf_ref, q_ref, kc_ref, vc_ref, ks_ref, vs_ref, kw_ref, vw_ref, sm_ref, covt_ref,
                eneg_ref, eye_ref, ng_ref, o_ref, obuf_ref, owin_ref, osel_ref, qs_ref):
    qi = pl.program_id(1)
    q0 = qi * TQ
    R = NSA_J * TQ
    n_cmp = kc_ref.shape[2]
    n_sel = ks_ref.shape[0] // SEL_BLOCK
    n_win = WIN_KEYS // TQ

    def row_t(shape):
        r = lax.broadcasted_iota(jnp.int32, shape, 0)
        return q0 + (r & (TQ - 1))

    scale = HD ** -0.5
    for rs, rope in _rope_slots(pos_ref, invf_ref):
        for h in range(NSA_HEADS):
            g, j = divmod(h, NSA_J)
            dst = slice(j * TQ + rs.start, j * TQ + rs.stop)
            qs_ref[g, dst, :] = (rope(q_ref[rs, h * HD:(h + 1) * HD].astype(F32)) * scale).astype(BF16)
    qgs = [qs_ref[g] for g in range(NSA_GROUPS)]

    def with_ones(v):
        lane = lax.broadcasted_iota(jnp.int32, v.shape, 1)
        return jnp.concatenate([v, jnp.where(lane == 0, 1.0, 0.0).astype(BF16)], axis=1)

    def window_chain(g):
        w0 = pl.multiple_of(jnp.maximum(q0 - WINDOW, 0), TQ)
        kt = kw_ref[pl.ds(w0, WIN_KEYS), g * HD:(g + 1) * HD]
        vt = with_ones(vw_ref[pl.ds(w0, WIN_KEYS), g * HD:(g + 1) * HD])
        sc = lax.dot_general(qgs[g], kt, NT_DIMS, preferred_element_type=F32)
        yield
        diff = row_t((R, WIN_KEYS)) - (w0 + lax.broadcasted_iota(jnp.int32, (R, WIN_KEYS), 1))
        keep = jnp.where(diff >= 0, diff, WINDOW) < WINDOW
        sc = jnp.where(keep, sc, NEG_INF)
        mw = jnp.max(sc, axis=1, keepdims=True)
        yield
        pw = jnp.exp((sc - mw).astype(BF16))
        acc = jnp.dot(pw, vt, preferred_element_type=F32)
        yield
        owin_ref[g] = acc[:, 0:HD] / acc[:, HD:HD + 1]

    gates = _sigmoid(sm_ref[...])
    o_cmps = [None] * NSA_GROUPS
    qps = [None] * NSA_GROUPS

    def cmp_chain(g):
        qg = qgs[g]

        s = lax.dot_general(qg, kc_ref[0, g], NT_DIMS, preferred_element_type=F32)
        yield
        n_lane = lax.broadcasted_iota(jnp.int32, (R, n_cmp), 1)
        cmask = (n_lane * CMP_STRIDE + (CMP_LEN - 1)) <= row_t((R, n_cmp))
        s = jnp.where(cmask, s, NEG_INF)
        m = jnp.max(s, axis=1, keepdims=True)
        e = jnp.where(cmask, jnp.exp(s - m), 0.0)
        l = jnp.sum(e, axis=1, keepdims=True)
        p = (e / jnp.where(l > 0.0, l, 1.0)).astype(BF16)
        yield
        o_cmps[g] = jnp.dot(p, vc_ref[0, g], preferred_element_type=F32)
        impr = lax.dot_general(covt_ref[...], p, NT_DIMS, preferred_element_type=F32)
        imp = impr[:, 0:TQ]
        for j in range(1, NSA_J):
            imp = imp + impr[:, j * TQ:(j + 1) * TQ]
        yield

        m_sub = lax.broadcasted_iota(jnp.int32, (n_sel, TQ), 0)
        jt = (q0 + lax.broadcasted_iota(jnp.int32, (n_sel, TQ), 1)) >> SEL_BLOCK_LOG2
        forced = jnp.where(m_sub == 0, FORCE_SCORE,
                           jnp.where(m_sub == jt, FORCE_SCORE,
                                     jnp.where(m_sub == jt - 1, FORCE_SCORE, 0.0)))
        score = jnp.where(m_sub <= jt, imp + forced, -1.0)
        rank = jnp.zeros((n_sel, TQ), F32)
        for mp in range(n_sel):
            row = score[mp:mp + 1, :]
            ge = jnp.where(row >= score, 1.0, 0.0)
            gt = jnp.where(row > score, 1.0, 0.0)
            rank = rank + jnp.where(m_sub > mp, ge, gt)
        notsel_t = jnp.where(rank < float(min(SEL_TOPK, n_sel)), 0.0, 1.0).astype(BF16)
        notsel = lax.dot_general(notsel_t, eye_ref[...], TN_DIMS,
                                 preferred_element_type=F32).astype(BF16)
        qps[g] = jnp.concatenate([qg, jnp.concatenate([notsel] * NSA_J, axis=0)], axis=1)

    _interleave(*[c for g in range(NSA_GROUPS) for c in (window_chain(g), cmp_chain(g))])

    def sel_chain(g, nk):
        kt = jnp.concatenate([ks_ref[0:nk, g * HD:(g + 1) * HD], eneg_ref[0:nk, :]], axis=1)
        vt = with_ones(vs_ref[0:nk, g * HD:(g + 1) * HD])
        sc = lax.dot_general(qps[g], kt, NT_DIMS, preferred_element_type=F32)
        yield
        klane = lax.broadcasted_iota(jnp.int32, (R, SEL_BUCKET), 1) + (nk - SEL_BUCKET)
        tail = jnp.where(klane <= row_t((R, SEL_BUCKET)), sc[:, nk - SEL_BUCKET:], NEG_INF)
        sc = tail if nk == SEL_BUCKET else jnp.concatenate([sc[:, 0:nk - SEL_BUCKET], tail], axis=1)
        ms = jnp.max(sc, axis=1, keepdims=True)
        yield
        acc = jnp.dot(jnp.exp((sc - ms).astype(BF16)), vt, preferred_element_type=F32)
        yield
        osel_ref[g] = acc[:, 0:HD] / acc[:, HD:HD + 1]

    for b in range(ks_ref.shape[0] // SEL_BUCKET):
        @pl.when(q0 // SEL_BUCKET == b)
        def _(b=b):
            _interleave(*[sel_chain(g, (b + 1) * SEL_BUCKET) for g in range(NSA_GROUPS)])

    ssq = jnp.zeros((TQ, 1), F32)
    for g in range(NSA_GROUPS):
        o_sel = osel_ref[g]
        o_cmp = o_cmps[g]
        o_win = owin_ref[g]
        for j in range(NSA_J):
            h = g * NSA_J + j
            rs = slice(j * TQ, (j + 1) * TQ)
            o = (gates[:, 3 * h:3 * h + 1] * o_cmp[rs] + gates[:, 3 * h + 1:3 * h + 2] * o_sel[rs]
                 + gates[:, 3 * h + 2:3 * h + 3] * o_win[rs])
            ssq = ssq + jnp.sum(o * o, axis=1, keepdims=True)
            obuf_ref[:, h * HD:(h + 1) * HD] = o

    inv = lax.rsqrt(ssq / float(NSA_HEADS * HD) + NORM_EPS)
    o_ref[...] = (obuf_ref[...] * inv * ng_ref[...]).astype(BF16)


def _nsa(posb, invf, k_r, proj, small, kc, vc, covt, eneg, eye, ng, B, S):
    T = B * S
    nq = S // TQ
    n_blk = S // CMP_STRIDE
    n_sel = S // SEL_BLOCK
    kvb = COL_KV // 256
    assert TQ == TS_ROPE
    return pl.pallas_call(
        _nsa_kernel,
        grid=(B, nq),
        in_specs=[
            pl.BlockSpec((TQ // ROPE_SLOTS, LANE), lambda b, i: (b * nq + i, 0)),
            pl.BlockSpec((1, LANE), lambda b, i: (0, 0)),
            pl.BlockSpec((TQ, 1024), lambda b, i: (b * nq + i, 0)),
            pl.BlockSpec((1, NSA_GROUPS, n_blk, HD), lambda b, i: (b, 0, 0, 0)),
            pl.BlockSpec((1, NSA_GROUPS, n_blk, HD), lambda b, i: (b, 0, 0, 0)),
            pl.BlockSpec((S, 256), lambda b, i: (b, 1)),
            pl.BlockSpec((S, 256), lambda b, i: (b, kvb + 3)),
            pl.BlockSpec((S, 256), lambda b, i: (b, 2)),
            pl.BlockSpec((S, 256), lambda b, i: (b, kvb + 5)),
            pl.BlockSpec((TQ, LANE), lambda b, i: (b * nq + i, 0)),
            pl.BlockSpec((n_sel, n_blk), lambda b, i: (0, 0)),
            pl.BlockSpec((S, LANE), lambda b, i: (0, 0)),
            pl.BlockSpec((n_sel, LANE), lambda b, i: (0, 0)),
            pl.BlockSpec((1, 1024), lambda b, i: (0, 0)),
        ],
        out_specs=pl.BlockSpec((TQ, 1024), lambda b, i: (b * nq + i, 0)),
        out_shape=jax.ShapeDtypeStruct((T, 1024), BF16),
        scratch_shapes=[pltpu.VMEM((TQ, 1024), F32), pltpu.VMEM((NSA_GROUPS, NSA_J * TQ, HD), F32),
                        pltpu.VMEM((NSA_GROUPS, NSA_J * TQ, HD), F32),
                        pltpu.VMEM((NSA_GROUPS, NSA_J * TQ, HD), BF16)],
        compiler_params=_cparams(("parallel", "parallel")),
        name="nsa",
    )(posb, invf, proj, kc, vc, k_r, proj, k_r, proj, small, covt, eneg, eye, ng)


ML_HPS = 2
ML_ROWS = 1024


def _mlstm_kernel(bias_ref, q_ref, k_ref, v_ref, om_ref, if_ref, cwq_ref, cwk_ref, cbq_ref, cbk_ref,
                  ng_ref, o_ref, c_ref, n_ref, m_ref, xq_ref, xk_ref):
    hp = pl.program_id(1)
    L = ML_CHUNK

    @pl.when(pl.program_id(2) == 0)
    def _():
        c_ref[...] = jnp.zeros_like(c_ref)
        n_ref[...] = jnp.zeros_like(n_ref)
        m_ref[...] = jnp.zeros_like(m_ref)
        xq_ref[:, 0:8, :] = jnp.zeros((ML_HPS, 8, ML_HD), F32)
        xk_ref[:, 0:8, :] = jnp.zeros((ML_HPS, 8, ML_HD), F32)

    def conv_silu(x_ref, hh, rows, xb_ref, w_ref, b_ref):
        cs = slice(hh * ML_HD, (hh + 1) * ML_HD)
        xb_ref[hh, 8:8 + L, :] = x_ref[rows, cs].astype(F32)
        y = b_ref[:, cs] + xb_ref[hh, 8:8 + L, :] * w_ref[CONV_W - 1:CONV_W, cs]
        for k in range(1, CONV_W):
            y = y + xb_ref[hh, 8 - k:8 - k + L, :] * w_ref[CONV_W - 1 - k:CONV_W - k, cs]
        xb_ref[hh, 0:8, :] = xb_ref[hh, L:L + 8, :]
        return _silu(y)

    r = lax.broadcasted_iota(jnp.int32, (L, L), 0)
    cidx = lax.broadcasted_iota(jnp.int32, (L, L), 1)
    tril = cidx <= r
    eye = cidx == r

    def head_chain(hh, rows):
        h = hp * ML_HPS + hh
        cs = slice(hh * ML_HD, (hh + 1) * ML_HD)
        qf = conv_silu(q_ref, hh, rows, xq_ref, cwq_ref, cbq_ref) * (ML_HD ** -0.5)
        yield
        kf = conv_silu(k_ref, hh, rows, xk_ref, cwk_ref, cbk_ref)
        yield
        qb = qf.astype(BF16)
        kb = kf.astype(BF16)
        vb = v_ref[rows, cs]
        vf = vb.astype(F32)

        ic = if_ref[0, hh, 0:1, rows] + bias_ref[0, h]
        fp = if_ref[0, hh, 1:2, rows] + bias_ref[1, h]
        fc = jnp.minimum(fp, 0.0) - jnp.log(1.0 + jnp.exp(-jnp.abs(fp)))

        fc_b = jnp.broadcast_to(fc, (L, L))
        ic_b = jnp.broadcast_to(ic, (L, L))
        b_col = jnp.sum(jnp.where(tril, fc_b, 0.0), axis=1, keepdims=True)
        fc_col = jnp.sum(jnp.where(eye, fc_b, 0.0), axis=1, keepdims=True)
        ic_col = jnp.sum(jnp.where(eye, ic_b, 0.0), axis=1, keepdims=True)
        b_row = jnp.sum(jnp.where(r <= cidx, jnp.broadcast_to(fc_col, (L, L)), 0.0),
                        axis=0, keepdims=True)
        b_last = b_col[L - 1:L, :]
        m_prev = m_ref[hh]
        yield

        d_log = jnp.where(tril, b_col - b_row + ic, NEG_INF)
        inter = b_col + m_prev
        m_t = jnp.maximum(inter, jnp.max(d_log, axis=1, keepdims=True))
        w_intra = jnp.exp(d_log - m_t)
        w_inter = jnp.exp(inter - m_t)
        qk = lax.dot_general(qb, kb, NT_DIMS, preferred_element_type=F32) * w_intra
        yield
        num = (jnp.dot(qk.astype(BF16), vb, preferred_element_type=F32)
               + w_inter * lax.dot_general(qb, c_ref[hh].astype(BF16), NT_DIMS, preferred_element_type=F32))
        den = jnp.sum(qk, axis=1, keepdims=True) + w_inter * jnp.sum(qf * n_ref[hh], axis=1, keepdims=True)
        hm = num / jnp.maximum(jnp.abs(den), jnp.exp(-m_t))
        yield

        w_log = b_last - b_col + ic_col
        m_new = jnp.maximum(b_last + m_prev, jnp.max(w_log, axis=0, keepdims=True))
        w_state = jnp.exp(w_log - m_new)
        decay = jnp.exp(b_last + m_prev - m_new)
        c_ref[hh] = decay * c_ref[hh] + lax.dot_general((w_state * vf).astype(BF16), kb, TN_DIMS,
                                                        preferred_element_type=F32)
        n_ref[hh] = decay * n_ref[hh] + jnp.sum(w_state * kf, axis=0, keepdims=True)
        m_ref[hh] = m_new
        yield

        hn = hm * lax.rsqrt(jnp.mean(hm * hm, axis=1, keepdims=True) + NORM_EPS) * ng_ref[:, cs]
        o_ref[rows, cs] = (hn * _sigmoid(om_ref[rows, cs].astype(F32))).astype(BF16)

    for cc in range(ML_ROWS // L):
        _interleave(*[head_chain(hh, slice(cc * L, (cc + 1) * L)) for hh in range(ML_HPS)])


def _mlstm(proj, if_arr, gate_bias, conv_w, conv_b, ng, B, S):
    T = B * S
    nc = S // ML_ROWS
    W = ML_HPS * ML_HD
    n_hp = ML_HEADS // ML_HPS
    assert COL_QKM % W == 0 and COL_VM % W == 0 and COL_OM % W == 0 and ML_HEADS % ML_HPS == 0
    cq, ck, cv, co = COL_QKM // W, COL_QKM // W + n_hp, COL_VM // W, COL_OM // W
    rows = lambda col0: pl.BlockSpec((ML_ROWS, W), lambda b, h, c: (b * nc + c, col0 + h))
    return pl.pallas_call(
        _mlstm_kernel,
        grid=(B, n_hp, nc),
        in_specs=[
            pl.BlockSpec(memory_space=pltpu.SMEM),
            rows(cq), rows(ck), rows(cv), rows(co),
            pl.BlockSpec((1, ML_HPS, 2, ML_ROWS), lambda b, h, c: (b, h, 0, c)),
            pl.BlockSpec((CONV_W, W), lambda b, h, c: (0, h)),
            pl.BlockSpec((CONV_W, W), lambda b, h, c: (0, n_hp + h)),
            pl.BlockSpec((1, W), lambda b, h, c: (0, h)),
            pl.BlockSpec((1, W), lambda b, h, c: (0, n_hp + h)),
            pl.BlockSpec((1, W), lambda b, h, c: (0, h)),
        ],
        out_specs=pl.BlockSpec((ML_ROWS, W), lambda b, h, c: (b * nc + c, h)),
        out_shape=jax.ShapeDtypeStruct((T, ML_HEADS * ML_HD), BF16),
        scratch_shapes=[
            pltpu.VMEM((ML_HPS, ML_HD, ML_HD), F32), pltpu.VMEM((ML_HPS, 1, ML_HD), F32),
            pltpu.VMEM((ML_HPS, 1, 1), F32),
            pltpu.VMEM((ML_HPS, ML_CHUNK + 8, ML_HD), F32), pltpu.VMEM((ML_HPS, ML_CHUNK + 8, ML_HD), F32),
        ],
        compiler_params=_cparams(("parallel", "parallel", "arbitrary")),
        name="mlstm",
    )(gate_bias, proj, proj, proj, proj, if_arr, conv_w, conv_w, conv_b, conv_b, ng)


def _pack_bf16_pair(lo, hi):
    lo_b = pltpu.bitcast(lo.astype(BF16).astype(F32), U32)
    hi_b = pltpu.bitcast(hi.astype(BF16).astype(F32), U32)
    return (lo_b >> 16) | hi_b


def _unpack_bf16_pair(p):
    lo = pltpu.bitcast(p << 16, F32)
    hi = pltpu.bitcast(p & jnp.uint32(0xFFFF0000), F32)
    return lo, hi


def _row(ref, r):
    return ref.at[pl.ds(r, 1), :]


def _outproj_kernel(nsa_ref, ml_ref, w_ref, x_ref, g2_ref, wr_ref, br_ref, tril_ref,
                    x1_ref, xp_ref, rt_ref, rtt_ref, cnt_ref, carry_ref, xa_ref, xb_ref):
    i = pl.program_id(0)

    @pl.when(i == 0)
    def _():
        carry_ref[...] = jnp.zeros_like(carry_ref)
        xb_ref[...] = jnp.zeros_like(xb_ref)

    half = D_MODEL // 2
    subs = [slice(sub * (TM_OUT // OUT_SUBS), (sub + 1) * (TM_OUT // OUT_SUBS)) for sub in range(OUT_SUBS)]

    def step(cur_ref, prev_ref):
        carry = carry_ref[...]
        carry_box = [carry]

        def chain():
            for rs in subs:
                yield from _outproj_rows(rs, prev_ref, carry_box, g2_ref, wr_ref, br_ref, tril_ref,
                                         x1_ref, xp_ref, rt_ref, rtt_ref)
                yield

        pending = chain()
        for rs in subs:
            for c in range(D_MODEL // OUT_COLS):
                cs = slice(c * OUT_COLS, (c + 1) * OUT_COLS)
                cur_ref[rs, cs] = (x_ref[rs, cs]
                                   + jnp.dot(nsa_ref[rs, :], w_ref[0:half, cs], preferred_element_type=F32)
                                   + jnp.dot(ml_ref[rs, :], w_ref[half:, cs], preferred_element_type=F32))
                next(pending, None)
        for _ in pending:
            pass
        carry = jnp.where(i > 0, carry_box[0], carry)
        carry_ref[...] = carry
        cnt_ref[...] = jnp.broadcast_to(carry, cnt_ref.shape)

    @pl.when(i % 2 == 0)
    def _():
        step(xa_ref, xb_ref)

    @pl.when(i % 2 == 1)
    def _():
        step(xb_ref, xa_ref)


def _outproj_rows(rs, src_ref, carry_box, g2_ref, wr_ref, br_ref, tril_ref, x1_ref, xp_ref, rt_ref, rtt_ref):
    half = D_MODEL // 2
    x1 = src_ref[rs, :]
    x1_ref[rs, :] = x1
    xn = x1 * lax.rsqrt(jnp.mean(x1 * x1, axis=-1, keepdims=True) + NORM_EPS) * g2_ref[...]
    xp_ref[rs, :] = _pack_bf16_pair(xn[:, :half], xn[:, half:])
    logits = jnp.dot(xn.astype(BF16), wr_ref[...], preferred_element_type=F32) + br_ref[...]
    yield

    tm = logits.shape[0]
    lane = lax.broadcasted_iota(jnp.int32, (tm, LANE), 1)
    lane_f = lane.astype(F32)
    big = float(LANE)
    gmask = lane < MOE_GROUPS
    gmax = jnp.max(jnp.where(gmask, logits, NEG_INF), axis=1, keepdims=True)
    ge = jnp.where(gmask, jnp.exp(logits - gmax), 0.0)
    gp = ge / jnp.sum(ge, axis=1, keepdims=True)
    g_w = jnp.max(gp, axis=1, keepdims=True)
    g_idx = jnp.min(jnp.where(gmask, jnp.where(gp == g_w, lane_f, big), big), axis=1, keepdims=True)
    yield
    grp_of_lane = ((lane - MOE_GROUPS) >> EPG_LOG2).astype(F32)
    emask = jnp.where(lane >= MOE_GROUPS, grp_of_lane, -1.0) == g_idx
    emax = jnp.max(jnp.where(emask, logits, NEG_INF), axis=1, keepdims=True)
    ee = jnp.where(emask, jnp.exp(logits - emax), 0.0)
    ep = jnp.where(emask, ee / jnp.sum(ee, axis=1, keepdims=True), -1.0)
    v1 = jnp.max(ep, axis=1, keepdims=True)
    i1 = jnp.min(jnp.where(ep == v1, lane_f, big), axis=1, keepdims=True)
    ep2 = jnp.where(lane_f == i1, -1.0, ep)
    v2 = jnp.max(ep2, axis=1, keepdims=True)
    i2 = jnp.min(jnp.where(ep2 == v2, lane_f, big), axis=1, keepdims=True)
    w0 = g_w * v1 / (v1 + v2)
    w1 = g_w * v2 / (v1 + v2)
    e0 = i1 - float(MOE_GROUPS)
    e1 = i2 - float(MOE_GROUPS)

    yield
    oh0 = jnp.where(lane_f == e0, 1.0, 0.0)
    oh1 = jnp.where(lane_f == e1, 1.0, 0.0)
    tril = tril_ref[0:tm, 0:tm]
    pre0 = jnp.dot(tril, oh0.astype(BF16), preferred_element_type=F32)
    pre1 = jnp.dot(tril, oh1.astype(BF16), preferred_element_type=F32)
    yield
    carry = carry_box[0]
    tot0 = pre0[tm - 1:tm, :]
    tot1 = pre1[tm - 1:tm, :]
    rank0 = jnp.sum(oh0 * (pre0 - 1.0 + carry), axis=1, keepdims=True)
    rank1 = jnp.sum(oh1 * (pre1 - 1.0 + carry + tot0), axis=1, keepdims=True)

    rt = jnp.where(lane == 0, e0, jnp.where(lane == 1, e1, jnp.where(lane == 2, w0, jnp.where(
        lane == 3, w1, jnp.where(lane == 4, rank0, jnp.where(lane == 5, rank1, 0.0))))))
    rt_ref[rs, :] = rt
    rtt_ref[:, rs] = jnp.transpose(rt)
    carry_box[0] = carry + tot0 + tot1


def _out_proj(nsa_o, ml_o, w_out, x2, g2, w_r, b_r, tril):
    T = x2.shape[0]
    half = D_MODEL // 2
    n_tiles = T // TM_OUT
    cur = lambda i: (jnp.minimum(i, n_tiles - 1), 0)
    prev = lambda i: (jnp.maximum(i - 1, 0), 0)
    return pl.pallas_call(
        _outproj_kernel,
        grid=(n_tiles + 1,),
        in_specs=[
            pl.BlockSpec((TM_OUT, half), cur),
            pl.BlockSpec((TM_OUT, half), cur),
            pl.BlockSpec((D_MODEL, D_MODEL), lambda i: (0, 0)),
            pl.BlockSpec((TM_OUT, D_MODEL), cur),
            pl.BlockSpec((1, D_MODEL), lambda i: (0, 0)),
            pl.BlockSpec((D_MODEL, LANE), lambda i: (0, 0)),
            pl.BlockSpec((1, LANE), lambda i: (0, 0)),
            pl.BlockSpec((TM_OUT, TM_OUT), lambda i: (0, 0)),
        ],
        out_specs=[
            pl.BlockSpec((TM_OUT, D_MODEL), prev),
            pl.BlockSpec((TM_OUT, half), prev),
            pl.BlockSpec((TM_OUT, LANE), prev),
            pl.BlockSpec((LANE, TM_OUT), lambda i: (0, jnp.maximum(i - 1, 0))),
            pl.BlockSpec((8, LANE), lambda i: (0, 0)),
        ],
        out_shape=[
            jax.ShapeDtypeStruct((T, D_MODEL), F32),
            jax.ShapeDtypeStruct((T, half), U32),
            jax.ShapeDtypeStruct((T, LANE), F32),
            jax.ShapeDtypeStruct((LANE, T), F32),
            jax.ShapeDtypeStruct((8, LANE), F32),
        ],
        scratch_shapes=[pltpu.VMEM((1, LANE), F32), pltpu.VMEM((TM_OUT, D_MODEL), F32),
                        pltpu.VMEM((TM_OUT, D_MODEL), F32)],
        compiler_params=_cparams(("arbitrary",)),
        name="out_proj",
    )(nsa_o, ml_o, w_out, x2, g2, w_r, b_r, tril)


TD = 256
DISPATCH_BUFS = 3
DISPATCH_UNROLL = 8


def _dispatch_kernel(d0_ref, d1_ref, zblk_ref, meta_ref, xp_hbm, xs_hbm, zbuf_ref, zsem_ref, tbuf_ref,
                     lsem_ref, rsem_ref):
    dests = (d0_ref, d1_ref)
    n_assign = 2 * d0_ref.shape[0]
    nb = xs_hbm.shape[0] // BM
    n_used = meta_ref[0]
    zbuf_ref[...] = jnp.zeros_like(zbuf_ref)

    def zero_block(blk):
        return pltpu.make_async_copy(zbuf_ref, xs_hbm.at[pl.ds(blk * BM, BM), :], zsem_ref.at[0])

    def for_zero_blocks(fn):
        for e in range(N_EXPERTS):
            @pl.when(zblk_ref[e] >= 0)
            def _(e=e):
                fn(zero_block(zblk_ref[e]))

        def body(blk, _):
            fn(zero_block(blk))
            return 0
        lax.fori_loop(n_used, nb, body, 0)

    for_zero_blocks(lambda c: c.start())
    for_zero_blocks(lambda c: c.wait())

    n_tiles = n_assign // (2 * TD)

    def load(tile, slot):
        return pltpu.make_async_copy(xp_hbm.at[pl.ds(tile * TD, TD), :], tbuf_ref.at[slot], lsem_ref.at[slot])

    def wait_scatter(slot):
        for _ in range(2):
            pltpu.make_async_copy(tbuf_ref.at[slot], tbuf_ref.at[slot], rsem_ref.at[slot]).wait()

    load(0, 0).start()

    def tile_step(i, _):
        slot = i % DISPATCH_BUFS
        nslot = (i + 1) % DISPATCH_BUFS

        @pl.when(i + 1 < n_tiles)
        def _():
            @pl.when(i + 1 >= DISPATCH_BUFS)
            def _():
                wait_scatter(nslot)
            load(i + 1, nslot).start()

        load(i, slot).wait()

        def body(j, _):
            for u in range(DISPATCH_UNROLL):
                r = j * DISPATCH_UNROLL + u
                for k in range(2):
                    d = dests[k][i * TD + r]
                    pltpu.make_async_copy(_row(tbuf_ref.at[slot], r), _row(xs_hbm, d),
                                          rsem_ref.at[slot]).start(priority=k)
            return 0
        lax.fori_loop(0, TD // DISPATCH_UNROLL, body, 0)
        return 0

    lax.fori_loop(0, n_tiles, tile_step, 0)
    for back in range(min(DISPATCH_BUFS, n_tiles)):
        wait_scatter((n_tiles - 1 - back) % DISPATCH_BUFS)


def _dispatch(dest0, dest1, zblk, meta, xp, n_rows):
    half = D_MODEL // 2
    assert dest0.shape == dest1.shape and dest0.shape[0] % TD == 0 and dest0.shape[0] // TD >= DISPATCH_BUFS
    grid_spec = pltpu.PrefetchScalarGridSpec(
        num_scalar_prefetch=4,
        grid=(1,),
        in_specs=[pl.BlockSpec(memory_space=pl.ANY)],
        out_specs=pl.BlockSpec(memory_space=pl.ANY),
        scratch_shapes=[
            pltpu.VMEM((BM, half), U32),
            pltpu.SemaphoreType.DMA((1,)),
            pltpu.VMEM((DISPATCH_BUFS, TD, half), U32),
            pltpu.SemaphoreType.DMA((DISPATCH_BUFS,)),
            pltpu.SemaphoreType.DMA((DISPATCH_BUFS,)),
        ],
    )
    return pl.pallas_call(
        _dispatch_kernel,
        grid_spec=grid_spec,
        out_shape=jax.ShapeDtypeStruct((n_rows, half), U32),
        compiler_params=_cparams(("arbitrary",)),
        name="dispatch",
    )(dest0, dest1, zblk, meta, xp)


def _expert_kernel(be_ref, nxt_ref, par_ref, meta_ref, x_ref, wg_hbm, wu_hbm, wd_hbm, y_ref,
                   wsg_ref, wsu_ref, wsd_ref, wsem_ref, wgb_ref, wub_ref, wdb_ref):
    i = pl.program_id(0)
    n_used = meta_ref[0]
    half = D_MODEL // 2

    def weight_copies(e, p):
        return (pltpu.make_async_copy(wg_hbm.at[e], wsg_ref.at[p], wsem_ref.at[p, 0]),
                pltpu.make_async_copy(wu_hbm.at[e], wsu_ref.at[p], wsem_ref.at[p, 1]),
                pltpu.make_async_copy(wd_hbm.at[e], wsd_ref.at[p], wsem_ref.at[p, 2]))

    def succ(e):
        return jnp.where(e >= 0, nxt_ref[jnp.maximum(e, 0)], -1)

    def start_weights(e, p):
        @pl.when(e >= 0)
        def _():
            for c in weight_copies(e, p):
                c.start(priority=1)

    def wait_staged(p):
        for c in weight_copies(0, p):
            c.wait()

    def cast_staged(p):
        wgb_ref[p] = wsg_ref[p].astype(BF16)
        wub_ref[p] = wsu_ref[p].astype(BF16)
        wdb_ref[p] = wsd_ref[p].astype(BF16)

    @pl.when(i == 0)
    def _():
        e0 = be_ref[0]
        start_weights(e0, 0)
        start_weights(succ(e0), 1)
        wait_staged(0)
        cast_staged(0)
        start_weights(succ(succ(e0)), 0)

    def step(p, cast_next):
        if cast_next:
            wait_staged(1 - p)
        lo, hi = _unpack_bf16_pair(x_ref[...])
        xl = lo.astype(BF16)
        xh = hi.astype(BF16)
        gt = (jnp.dot(xl, wgb_ref[p, 0:half, :], preferred_element_type=F32)
              + jnp.dot(xh, wgb_ref[p, half:, :], preferred_element_type=F32))
        up = (jnp.dot(xl, wub_ref[p, 0:half, :], preferred_element_type=F32)
              + jnp.dot(xh, wub_ref[p, half:, :], preferred_element_type=F32))
        hb = (_silu(gt) * up).astype(BF16)
        y = jnp.dot(hb, wdb_ref[p], preferred_element_type=F32)
        y_ref[...] = _pack_bf16_pair(y[:, :half], y[:, half:])
        if cast_next:
            cast_staged(1 - p)
            start_weights(succ(succ(succ(be_ref[i]))), 1 - p)

    e = be_ref[jnp.minimum(i, n_used - 1)]
    ends_expert = (i + 1 < n_used) & (be_ref[jnp.minimum(i + 1, n_used - 1)] != e)
    for p in range(2):
        for cast_next in (False, True):
            @pl.when((i < n_used) & (par_ref[i] == p)
                     & (ends_expert if cast_next else jnp.logical_not(ends_expert)))
            def _(p=p, cast_next=cast_next):
                step(p, cast_next)

    @pl.when(i >= n_used)
    def _():
        y_ref[...] = jnp.zeros_like(y_ref)


def _experts(block_expert, next_expert, block_parity, meta, xs, w_gate, w_up, w_down):
    half = D_MODEL // 2
    n_rows = xs.shape[0]
    nb = n_rows // BM
    grid_spec = pltpu.PrefetchScalarGridSpec(
        num_scalar_prefetch=4,
        grid=(nb,),
        in_specs=[
            pl.BlockSpec((BM, half), lambda i, be, nxt, par, meta: (jnp.minimum(i, meta[0] - 1), 0)),
            pl.BlockSpec(memory_space=pl.ANY), pl.BlockSpec(memory_space=pl.ANY),
            pl.BlockSpec(memory_space=pl.ANY),
        ],
        out_specs=pl.BlockSpec((BM, half), lambda i, be, nxt, par, meta: (i, 0)),
        scratch_shapes=[
            pltpu.VMEM((2, D_MODEL, D_EXPERT), F32),
            pltpu.VMEM((2, D_MODEL, D_EXPERT), F32),
            pltpu.VMEM((2, D_EXPERT, D_MODEL), F32),
            pltpu.SemaphoreType.DMA((2, 3)),
            pltpu.VMEM((2, D_MODEL, D_EXPERT), BF16),
            pltpu.VMEM((2, D_MODEL, D_EXPERT), BF16),
            pltpu.VMEM((2, D_EXPERT, D_MODEL), BF16),
        ],
    )
    return pl.pallas_call(
        _expert_kernel,
        grid_spec=grid_spec,
        out_shape=jax.ShapeDtypeStruct((n_rows, half), U32),
        compiler_params=_cparams(("arbitrary",)),
        name="experts",
    )(block_expert, next_expert, block_parity, meta, xs, w_gate, w_up, w_down)


def _combine_kernel(d0_ref, d1_ref, y_hbm, x1_ref, rt_ref, fg_ref, o_ref, ya_ref, yb_ref, sem_ref):
    i = pl.program_id(0)
    nt = pl.num_programs(0)
    half = D_MODEL // 2
    bufs = (ya_ref, yb_ref)
    dests = (d0_ref, d1_ref)

    def row_copy(tile, slot, r_tile, s, k):
        r = r_tile * 8 + s
        d = dests[k][tile * TC + r]
        return pltpu.make_async_copy(_row(y_hbm, d), _row(bufs[slot].at[k], r), sem_ref.at[slot])

    def wait_rows(slot):
        pltpu.make_async_copy(bufs[slot], bufs[slot], sem_ref.at[slot]).wait()

    def step(slot):
        wait_rows(slot)
        for r in range(TC):
            for k in range(2):
                row_copy(i + 1, 1 - slot, r // 8, r % 8, k).start(priority=k)
        rt = rt_ref[...]
        w0 = rt[:, 2:3]
        w1 = rt[:, 3:4]
        lo0, hi0 = _unpack_bf16_pair(bufs[slot][0])
        lo1, hi1 = _unpack_bf16_pair(bufs[slot][1])
        xl = x1_ref[:, :half] + (w0 * lo0 + w1 * lo1)
        xh = x1_ref[:, half:] + (w0 * hi0 + w1 * hi1)
        ms = (jnp.sum(xl * xl, axis=1, keepdims=True) + jnp.sum(xh * xh, axis=1, keepdims=True)) / float(D_MODEL)
        inv = lax.rsqrt(ms + NORM_EPS)
        o_ref[:, :half] = xl * inv * fg_ref[:, :half]
        o_ref[:, half:] = xh * inv * fg_ref[:, half:]

        @pl.when(i == nt - 1)
        def _():
            wait_rows(1 - slot)

    @pl.when(i == 0)
    def _():
        def body(rt, _):
            for s in range(8):
                for k in range(2):
                    row_copy(0, 0, rt, s, k).start(priority=k)
            return 0
        lax.fori_loop(0, TC // 8, body, 0)

    @pl.when(i % 2 == 0)
    def _():
        step(0)

    @pl.when(i % 2 == 1)
    def _():
        step(1)


def _combine(dest0, dest1, ys, x1, route, fg):
    T = x1.shape[0]
    half = D_MODEL // 2
    grid_spec = pltpu.PrefetchScalarGridSpec(
        num_scalar_prefetch=2,
        grid=(T // TC,),
        in_specs=[
            pl.BlockSpec(memory_space=pl.ANY),
            pl.BlockSpec((TC, D_MODEL), lambda i, d0, d1: (i, 0)),
            pl.BlockSpec((TC, LANE), lambda i, d0, d1: (i, 0)),
            pl.BlockSpec((1, D_MODEL), lambda i, d0, d1: (0, 0)),
        ],
        out_specs=pl.BlockSpec((TC, D_MODEL), lambda i, d0, d1: (i, 0)),
        scratch_shapes=[
            pltpu.VMEM((2, TC, half), U32),
            pltpu.VMEM((2, TC, half), U32),
            pltpu.SemaphoreType.DMA((2,)),
        ],
    )
    return pl.pallas_call(
        _combine_kernel,
        grid_spec=grid_spec,
        out_shape=jax.ShapeDtypeStruct((T, D_MODEL), F32),
        compiler_params=_cparams(("arbitrary",)),
        name="combine",
    )(dest0, dest1, ys, x1, route, fg)


def _cover_matrix(S):
    n_blk = S // CMP_STRIDE
    n_sel = S // SEL_BLOCK
    cs = np.arange(n_blk) * CMP_STRIDE
    ss = np.arange(n_sel) * SEL_BLOCK
    shared = np.minimum(cs[:, None] + CMP_LEN, ss[None, :] + SEL_BLOCK) - np.maximum(cs[:, None], ss[None, :])
    return (np.clip(shared, 0, None) / CMP_LEN).T.astype(np.float32)


def _block_mask_matrix(S):
    n_sel = S // SEL_BLOCK
    assert n_sel <= LANE
    e = np.zeros((S, LANE), np.float32)
    e[np.arange(S), np.arange(S) // SEL_BLOCK] = NEG_INF
    return e


def _inv_freq_row():
    inv = np.power(np.float32(ROPE_THETA), -np.arange(ROPE_HALF, dtype=np.float32) * 2.0 / ROPE_DIM)
    return np.tile(inv, ROPE_SLOTS).reshape(1, LANE).astype(np.float32)


def _packed_positions(positions, T):
    rows = TS_ROPE // ROPE_SLOTS
    p = positions.reshape(T // TS_ROPE, ROPE_SLOTS, rows).transpose(0, 2, 1).astype(F32)
    return jnp.repeat(p, ROPE_HALF, axis=2).reshape(T // ROPE_SLOTS, LANE)


def _layer(x2, positions, B, S, norm1_g, w_in, cmp_pe_k, cmp_pe_v, cmp_wk1, cmp_wk2, cmp_wv1, cmp_wv2,
           nsa_norm_g, conv_w, conv_b, b_igate, b_fgate, mlstm_norm_g, w_out, norm2_g,
           w_group, b_group, w_router, b_router, w_exp_gate, w_exp_up, w_exp_down, out_norm_g):
    T = B * S
    w_main, w_small = _weight_prep(jnp.transpose(w_in[0]))
    posb = _packed_positions(positions, T)
    invf = jnp.asarray(_inv_freq_row())

    proj, small, small_t = _in_proj(x2, norm1_g.reshape(1, -1), w_main, w_small)
    k_r = _rope(posb, invf, proj)
    kc, vc = _compress(k_r, proj, B, S, cmp_pe_k, cmp_pe_v, cmp_wk1.astype(BF16), cmp_wk2.astype(BF16),
                       cmp_wv1.astype(BF16), cmp_wv2.astype(BF16))
    eye = np.eye(S // SEL_BLOCK, LANE, dtype=np.float32)
    nsa_o = _nsa(posb, invf, k_r, proj, small, kc, vc, jnp.asarray(_cover_matrix(S), BF16),
                 jnp.asarray(_block_mask_matrix(S), BF16), jnp.asarray(eye, BF16),
                 nsa_norm_g.reshape(1, -1), B, S)

    if_arr = small_t[24:32].reshape(2, ML_HEADS, B, S).transpose(2, 1, 0, 3)
    gate_bias = jnp.stack([b_igate, b_fgate]).astype(F32)
    ml_o = _mlstm(proj, if_arr, gate_bias, conv_w, conv_b.reshape(1, -1), mlstm_norm_g.reshape(1, -1), B, S)

    w_r = jnp.concatenate([w_group, w_router, jnp.zeros((D_MODEL, LANE - MOE_GROUPS - N_EXPERTS), F32)],
                          axis=1).astype(BF16)
    b_r = jnp.concatenate([b_group, b_router, jnp.zeros((LANE - MOE_GROUPS - N_EXPERTS,), F32)]).reshape(1, LANE)
    tril = jnp.asarray(np.tril(np.ones((TM_OUT, TM_OUT), np.float32)), BF16)
    x1, xp, route, route_t, cnt = _out_proj(nsa_o, ml_o, w_out.astype(BF16), x2, norm2_g.reshape(1, -1),
                                            w_r, b_r, tril)

    n_rows = T * 2 + N_EXPERTS * BM
    counts = cnt[0, :N_EXPERTS].astype(jnp.int32)
    pcounts = (counts + BM - 1) // BM * BM
    pends = jnp.cumsum(pcounts)
    pstarts = pends - pcounts
    start_blk = (pstarts // BM).astype(F32)

    def sorted_row(k):
        onehot = (route_t[k].astype(jnp.int32)[:, None] == jnp.arange(N_EXPERTS, dtype=jnp.int32)).astype(F32)
        return (jnp.einsum('te,e->t', onehot, start_blk).astype(jnp.int32) * BM
                + route_t[4 + k].astype(jnp.int32))

    dests = (sorted_row(0), sorted_row(1))
    block_row0 = jnp.arange(n_rows // BM, dtype=jnp.int32) * BM
    block_expert = jnp.minimum(jnp.sum((pends[None, :] <= block_row0[:, None]).astype(jnp.int32), axis=1),
                               N_EXPERTS - 1).astype(jnp.int32)
    present = jnp.where(counts > 0, jnp.arange(N_EXPERTS, dtype=jnp.int32), N_EXPERTS)
    later = jnp.concatenate([lax.cummin(present[::-1])[::-1][1:], jnp.full((1,), N_EXPERTS, jnp.int32)])
    next_expert = jnp.where(later < N_EXPERTS, later, -1).astype(jnp.int32)
    meta = (pends[-1:] // BM).astype(jnp.int32)

    last_block = jnp.where(counts > 0, pends // BM - 1, -1).astype(jnp.int32)

    xs = _dispatch(dests[0], dests[1], last_block, meta, xp, n_rows)
    ordinal = jnp.cumsum((counts > 0).astype(jnp.int32)) - 1
    block_parity = (ordinal[block_expert] % 2).astype(jnp.int32)
    ys = _experts(block_expert, next_expert, block_parity, meta, xs, w_exp_gate, w_exp_up, w_exp_down)
    pad = jnp.zeros((TC,), jnp.int32)
    return _combine(jnp.concatenate([dests[0], pad]), jnp.concatenate([dests[1], pad]), ys, x1, route,
                    out_norm_g.reshape(1, -1))


def kernel(x, positions, norm1_g, w_in, cmp_pe_k, cmp_pe_v, cmp_wk1, cmp_wk2, cmp_wv1, cmp_wv2, nsa_norm_g,
           conv_w, conv_b, b_igate, b_fgate, mlstm_norm_g, w_out, norm2_g, w_group, b_group, w_router,
           b_router, w_exp_gate, w_exp_up, w_exp_down, final_norm_g):
    B, S, D = x.shape
    assert D == D_MODEL and norm1_g.shape[0] == 1, "single-layer, D_MODEL-wide configuration only"
    assert S % ML_ROWS == 0 and ML_ROWS % ML_CHUNK == 0 and S % SEL_BUCKET == 0 and S >= WIN_KEYS and (B * S) % TM_IN == 0
    out = _layer(x.reshape(B * S, D), positions, B, S, norm1_g[0], w_in, cmp_pe_k[0], cmp_pe_v[0],
                 cmp_wk1[0], cmp_wk2[0], cmp_wv1[0], cmp_wv2[0], nsa_norm_g[0], conv_w[0], conv_b[0],
                 b_igate[0], b_fgate[0], mlstm_norm_g[0], w_out[0], norm2_g[0], w_group[0], b_group[0],
                 w_router[0], b_router[0], w_exp_gate[0], w_exp_up[0], w_exp_down[0], final_norm_g)
    return out.reshape(B, S, D)
```

```python
import numpy as np
import jax
import jax.numpy as jnp
from jax import lax
from jax.experimental import pallas as pl
from jax.experimental.pallas import tpu as pltpu

F32 = jnp.float32
BF16 = jnp.bfloat16
U32 = jnp.uint32

D_MODEL = 2048
NSA_HEADS = 8
NSA_GROUPS = 2
NSA_J = NSA_HEADS // NSA_GROUPS
HD = 128
CMP_LEN = 32
CMP_STRIDE = 16
SEL_BLOCK = 64
SEL_TOPK = 8
WINDOW = 512
ROPE_THETA = 500000.0
ROPE_DIM = 32
ROPE_HALF = 16
ML_HEADS = 4
ML_HD = 256
CONV_W = 4
MOE_GROUPS = 8
EPG = 8
N_EXPERTS = 64
D_EXPERT = 512
SEL_BLOCK_LOG2 = 6
EPG_LOG2 = 3
assert (1 << SEL_BLOCK_LOG2) == SEL_BLOCK and (1 << EPG_LOG2) == EPG
NORM_EPS = 1e-6
NEG_INF = -1e30
FORCE_SCORE = 1000.0

COL_Q = 0
COL_KV = 1024
COL_QKM = 2560
COL_VM = 4608
COL_OM = 5632
N_MAIN = 6656

LANE = 128
VMEM_LIMIT = 56 * 1024 * 1024

TM_IN = 1024
TN_IN = 1664
TS_ROPE = 256
TQ = 256
SEL_BUCKET = 256
WIN_KEYS = WINDOW + TQ
ML_CHUNK = 256
TM_OUT = 512
OUT_SUBS = 2
OUT_COLS = 512
BM = 256
TC = 256

NT_DIMS = (((1,), (1,)), ((), ()))
TN_DIMS = (((0,), (0,)), ((), ()))


def _cparams(sem):
    return pltpu.CompilerParams(dimension_semantics=sem, vmem_limit_bytes=VMEM_LIMIT)


def _sigmoid(x):
    return 0.5 * jnp.tanh(0.5 * x) + 0.5


def _interleave(*chains):
    live = list(chains)
    while live:
        for g in list(live):
            try:
                next(g)
            except StopIteration:
                live.remove(g)


def _silu(x):
    h = 0.5 * x
    return h + h * jnp.tanh(h)


W_GATES = 2560
W_QKM = 2584
W_IGATE = 6680
W_END = 6688
TR_PREP = 512
assert W_GATES % TR_PREP == 0 and N_MAIN % TR_PREP == 0 and W_IGATE - W_QKM == N_MAIN - W_GATES


def _wprep_kernel(wt_hbm, wm_ref, ws_ref, wbuf_ref, sbuf_ref, sem_ref, ssem_ref):
    i = pl.program_id(0)
    n_gate = W_QKM - W_GATES
    n_if = W_END - W_IGATE

    def load(blk, slot):
        start = pl.multiple_of(blk * TR_PREP + jnp.where(blk * TR_PREP >= W_GATES, n_gate, 0), 8)
        return pltpu.make_async_copy(wt_hbm.at[pl.ds(start, TR_PREP), :], wbuf_ref.at[slot], sem_ref.at[slot])

    def small_loads():
        return (pltpu.make_async_copy(wt_hbm.at[pl.ds(W_GATES, n_gate), :], sbuf_ref.at[pl.ds(0, n_gate), :],
                                      ssem_ref.at[0]),
                pltpu.make_async_copy(wt_hbm.at[pl.ds(W_IGATE, n_if), :], sbuf_ref.at[pl.ds(n_gate, n_if), :],
                                      ssem_ref.at[1]))

    @pl.when(i == 0)
    def _():
        load(0, 0).start()
        for c in small_loads():
            c.start()
        for c in small_loads():
            c.wait()
        ws_ref[...] = jnp.zeros_like(ws_ref)
        ws_ref[0:n_gate + n_if, :] = sbuf_ref[...].astype(BF16)

    @pl.when(i + 1 < pl.num_programs(0))
    def _():
        load(i + 1, (i + 1) % 2).start()

    load(i, i % 2).wait()
    wm_ref[...] = wbuf_ref[i % 2].astype(BF16)


def _weight_prep(wt):
    assert wt.shape == (W_END, D_MODEL)
    n_small = (W_QKM - W_GATES) + (W_END - W_IGATE)
    return pl.pallas_call(
        _wprep_kernel,
        grid=(N_MAIN // TR_PREP,),
        in_specs=[pl.BlockSpec(memory_space=pl.ANY)],
        out_specs=[pl.BlockSpec((TR_PREP, D_MODEL), lambda i: (i, 0)),
                   pl.BlockSpec((LANE, D_MODEL), lambda i: (0, 0))],
        out_shape=[jax.ShapeDtypeStruct((N_MAIN, D_MODEL), BF16),
                   jax.ShapeDtypeStruct((LANE, D_MODEL), BF16)],
        scratch_shapes=[pltpu.VMEM((2, TR_PREP, D_MODEL), F32), pltpu.VMEM((n_small, D_MODEL), F32),
                        pltpu.SemaphoreType.DMA((2,)), pltpu.SemaphoreType.DMA((2,))],
        compiler_params=_cparams(("arbitrary",)),
        name="weight_prep",
    )(wt)


def _inproj_kernel(x_ref, g_ref, w_ref, ws_ref, o_ref, os_ref, ost_ref, h_ref):
    @pl.when(pl.program_id(1) == 0)
    def _():
        x = x_ref[...]
        ms = jnp.mean(x * x, axis=-1, keepdims=True)
        h_ref[...] = (x * lax.rsqrt(ms + NORM_EPS) * g_ref[...]).astype(BF16)
        small = lax.dot_general(h_ref[...], ws_ref[...], NT_DIMS, preferred_element_type=F32)
        os_ref[...] = small
        ost_ref[...] = jnp.transpose(small)

    o_ref[...] = lax.dot_general(h_ref[...], w_ref[...], NT_DIMS,
                                 preferred_element_type=F32).astype(o_ref.dtype)


def _in_proj(x2, g1, w_main, w_small):
    T = x2.shape[0]
    return pl.pallas_call(
        _inproj_kernel,
        grid=(T // TM_IN, N_MAIN // TN_IN),
        in_specs=[
            pl.BlockSpec((TM_IN, D_MODEL), lambda m, n: (m, 0)),
            pl.BlockSpec((1, D_MODEL), lambda m, n: (0, 0)),
            pl.BlockSpec((TN_IN, D_MODEL), lambda m, n: (n, 0)),
            pl.BlockSpec((LANE, D_MODEL), lambda m, n: (0, 0)),
        ],
        out_specs=[
            pl.BlockSpec((TM_IN, TN_IN), lambda m, n: (m, n)),
            pl.BlockSpec((TM_IN, LANE), lambda m, n: (m, 0)),
            pl.BlockSpec((LANE, TM_IN), lambda m, n: (0, m)),
        ],
        out_shape=[
            jax.ShapeDtypeStruct((T, N_MAIN), BF16),
            jax.ShapeDtypeStruct((T, LANE), F32),
            jax.ShapeDtypeStruct((LANE, T), F32),
        ],
        scratch_shapes=[pltpu.VMEM((TM_IN, D_MODEL), BF16)],
        compiler_params=_cparams(("parallel", "arbitrary")),
        name="in_proj",
    )(x2, g1, w_main, w_small)


ROPE_SLOTS = LANE // ROPE_HALF


def _rope_slots(pos_ref, invf_ref):
    ang = pos_ref[...] * invf_ref[...]
    c = jnp.cos(ang)
    s = jnp.sin(ang)
    rows = ang.shape[0]
    lane = lax.broadcasted_iota(jnp.int32, ang.shape, 1)
    lo = lane < ROPE_HALF
    mid = lane < ROPE_DIM

    def lanes_from(x, src):
        shift = (-src) % LANE
        return x if shift == 0 else pltpu.roll(x, shift, 1)

    for slot in range(ROPE_SLOTS):
        src = slot * ROPE_HALF
        cf = jnp.where(lo, lanes_from(c, src), jnp.where(mid, lanes_from(c, src - ROPE_HALF), 1.0))
        sa = jnp.where(lo, -lanes_from(s, src), 0.0)
        sb = jnp.where(lo, 0.0, jnp.where(mid, lanes_from(s, src - ROPE_HALF), 0.0))

        def rope(x, cf=cf, sa=sa, sb=sb):
            return x * cf + pltpu.roll(x, LANE - ROPE_HALF, 1) * sa + pltpu.roll(x, ROPE_HALF, 1) * sb

        yield slice(slot * rows, (slot + 1) * rows), rope


ROPE_TILES = 4


def _rope_kernel(pos_ref, invf_ref, kc_ref, ks_ref, kw_ref, ko_ref):
    prow = TS_ROPE // ROPE_SLOTS
    for t in range(ROPE_TILES):
        for rs, rope in _rope_slots(pos_ref.at[t * prow:(t + 1) * prow, :], invf_ref):
            rs = slice(t * TS_ROPE + rs.start, t * TS_ROPE + rs.stop)
            for i, r in enumerate((kc_ref, ks_ref, kw_ref)):
                for g in range(NSA_GROUPS):
                    sl = slice(g * HD, (g + 1) * HD)
                    so = slice(i * 2 * HD + g * HD, i * 2 * HD + (g + 1) * HD)
                    ko_ref[rs, so] = rope(r[rs, sl].astype(F32)).astype(BF16)


def _rope(posb, invf, proj):
    T = proj.shape[0]
    kvb = COL_KV // 256
    rows = ROPE_TILES * TS_ROPE
    assert T % rows == 0
    return pl.pallas_call(
        _rope_kernel,
        grid=(T // rows,),
        in_specs=[
            pl.BlockSpec((rows // ROPE_SLOTS, LANE), lambda i: (i, 0)),
            pl.BlockSpec((1, LANE), lambda i: (0, 0)),
            pl.BlockSpec((rows, 256), lambda i: (i, kvb + 0)),
            pl.BlockSpec((rows, 256), lambda i: (i, kvb + 2)),
            pl.BlockSpec((rows, 256), lambda i: (i, kvb + 4)),
        ],
        out_specs=pl.BlockSpec((rows, 768), lambda i: (i, 0)),
        out_shape=jax.ShapeDtypeStruct((T, 768), BF16),
        compiler_params=_cparams(("parallel",)),
        name="rope",
    )(posb, invf, proj, proj, proj)


def _compress_kernel(k_ref, v_ref, pek_ref, pev_ref, w1k_ref, w2k_ref, w1v_ref, w2v_ref,
                     kc_ref, vc_ref, xs_ref):
    S = k_ref.shape[0]
    n_blk = S // CMP_STRIDE

    for src, pe, w1, w2, dst in ((k_ref, pek_ref, w1k_ref, w2k_ref, kc_ref),
                                 (v_ref, pev_ref, w1v_ref, w2v_ref, vc_ref)):
        xs_ref[0:S, :] = src[...].astype(F32)
        xs_ref[S:S + CMP_LEN, :] = jnp.zeros((CMP_LEN, HD), F32)
        acc = jnp.zeros((n_blk, HD), F32)
        for l in range(CMP_LEN):
            a = xs_ref[pl.ds(l, n_blk, stride=CMP_STRIDE), :] + pe[l:l + 1, :]
            acc = acc + jnp.dot(a.astype(BF16), w1[l * HD:(l + 1) * HD, :],
                                preferred_element_type=F32)
        hid = _silu(acc)
        out = jnp.dot(hid.astype(BF16), w2[...], preferred_element_type=F32)
        dst[0, 0] = out.astype(BF16)


def _compress(k_r, proj, B, S, pek, pev, w1k, w2k, w1v, w2v):
    n_blk = S // CMP_STRIDE
    vcol = (COL_KV + 256) // HD
    full = lambda shape: pl.BlockSpec(shape, lambda b, g: tuple(0 for _ in shape))
    return pl.pallas_call(
        _compress_kernel,
        grid=(B, NSA_GROUPS),
        in_specs=[
            pl.BlockSpec((S, HD), lambda b, g: (b, g)),
            pl.BlockSpec((S, HD), lambda b, g: (b, vcol + g)),
            full((CMP_LEN, HD)), full((CMP_LEN, HD)),
            full((CMP_LEN * HD, HD)), full((HD, HD)),
            full((CMP_LEN * HD, HD)), full((HD, HD)),
        ],
        out_specs=[
            pl.BlockSpec((1, 1, n_blk, HD), lambda b, g: (b, g, 0, 0)),
            pl.BlockSpec((1, 1, n_blk, HD), lambda b, g: (b, g, 0, 0)),
        ],
        out_shape=[
            jax.ShapeDtypeStruct((B, NSA_GROUPS, n_blk, HD), BF16),
            jax.ShapeDtypeStruct((B, NSA_GROUPS, n_blk, HD), BF16),
        ],
        scratch_shapes=[pltpu.VMEM((S + CMP_LEN, HD), F32)],
        compiler_params=_cparams(("parallel", "parallel")),
        name="compress",
    )(k_r, proj, pek, pev, w1k, w2k, w1v, w2v)


def _nsa_kernel(pos_ref, invf_ref, q_ref, kc_ref, vc_ref, ks_ref, vs_ref, kw_ref, vw_ref, sm_ref, covt_ref,
                eneg_ref, eye_ref, ng_ref, o_ref, obuf_ref, owin_ref, osel_ref, qs_ref):
    qi = pl.program_id(1)
    q0 = qi * TQ
    R = NSA_J * TQ
    n_cmp = kc_ref.shape[2]
    n_sel = ks_ref.shape[0] // SEL_BLOCK

    def row_t(shape):
        r = lax.broadcasted_iota(jnp.int32, shape, 0)
        return q0 + (r & (TQ - 1))

    scale = HD ** -0.5
    for rs, rope in _rope_slots(pos_ref, invf_ref):
        for h in range(NSA_HEADS):
            g, j = divmod(h, NSA_J)
            dst = slice(j * TQ + rs.start, j * TQ + rs.stop)
            qs_ref[g, dst, :] = (rope(q_ref[rs, h * HD:(h + 1) * HD].astype(F32)) * scale).astype(BF16)
    qgs = [qs_ref[g] for g in range(NSA_GROUPS)]

    def with_ones(v):
        lane = lax.broadcasted_iota(jnp.int32, v.shape, 1)
        return jnp.concatenate([v, jnp.where(lane == 0, 1.0, 0.0).astype(BF16)], axis=1)

    def window_chain(g):
        w0 = pl.multiple_of(jnp.maximum(q0 - WINDOW, 0), TQ)
        kt = kw_ref[pl.ds(w0, WIN_KEYS), g * HD:(g + 1) * HD]
        vt = with_ones(vw_ref[pl.ds(w0, WIN_KEYS), g * HD:(g + 1) * HD])
        sc = lax.dot_general(qgs[g], kt, NT_DIMS, preferred_element_type=F32)
        yield
        diff = row_t((R, WIN_KEYS)) - (w0 + lax.broadcasted_iota(jnp.int32, (R, WIN_KEYS), 1))
        keep = jnp.where(diff >= 0, diff, WINDOW) < WINDOW
        sc = jnp.where(keep, sc, NEG_INF)
        mw = jnp.max(sc, axis=1, keepdims=True)
        yield
        pw = jnp.exp((sc - mw).astype(BF16))
        acc = jnp.dot(pw, vt, preferred_element_type=F32)
        yield
        owin_ref[g] = acc[:, 0:HD] / acc[:, HD:HD + 1]

    gates = _sigmoid(sm_ref[...])
    o_cmps = [None] * NSA_GROUPS
    qps = [None] * NSA_GROUPS

    def cmp_chain(g):
        qg = qgs[g]

        s = lax.dot_general(qg, kc_ref[0, g], NT_DIMS, preferred_element_type=F32)
        yield
        n_lane = lax.broadcasted_iota(jnp.int32, (R, n_cmp), 1)
        cmask = (n_lane * CMP_STRIDE + (CMP_LEN - 1)) <= row_t((R, n_cmp))
        s = jnp.where(cmask, s, NEG_INF)
        m = jnp.max(s, axis=1, keepdims=True)
        e = jnp.where(cmask, jnp.exp(s - m), 0.0)
        l = jnp.sum(e, axis=1, keepdims=True)
        p = (e / jnp.where(l > 0.0, l, 1.0)).astype(BF16)
        yield
        o_cmps[g] = jnp.dot(p, vc_ref[0, g], preferred_element_type=F32)
        impr = lax.dot_general(covt_ref[...], p, NT_DIMS, preferred_element_type=F32)
        imp = impr[:, 0:TQ]
        for j in range(1, NSA_J):
            imp = imp + impr[:, j * TQ:(j + 1) * TQ]
        yield

        m_sub = lax.broadcasted_iota(jnp.int32, (n_sel, TQ), 0)
        jt = (q0 + lax.broadcasted_iota(jnp.int32, (n_sel, TQ), 1)) >> SEL_BLOCK_LOG2
        forced = jnp.where(m_sub == 0, FORCE_SCORE,
                           jnp.where(m_sub == jt, FORCE_SCORE,
                                     jnp.where(m_sub == jt - 1, FORCE_SCORE, 0.0)))
        score = jnp.where(m_sub <= jt, imp + forced, -1.0)
        rank = jnp.zeros((n_sel, TQ), F32)
        for mp in range(n_sel):
            row = score[mp:mp + 1, :]
            ge = jnp.where(row >= score, 1.0, 0.0)
            gt = jnp.where(row > score, 1.0, 0.0)
            rank = rank + jnp.where(m_sub > mp, ge, gt)
        notsel_t = jnp.where(rank < float(min(SEL_TOPK, n_sel)), 0.0, 1.0).astype(BF16)
        notsel = lax.dot_general(notsel_t, eye_ref[...], TN_DIMS,
                                 preferred_element_type=F32).astype(BF16)
        qps[g] = jnp.concatenate([qg, jnp.concatenate([notsel] * NSA_J, axis=0)], axis=1)

    _interleave(*[c for g in range(NSA_GROUPS) for c in (window_chain(g), cmp_chain(g))])

    def sel_chain(g, nk):
        kt = jnp.concatenate([ks_ref[0:nk, g * HD:(g + 1) * HD], eneg_ref[0:nk, :]], axis=1)
        vt = with_ones(vs_ref[0:nk, g * HD:(g + 1) * HD])
        sc = lax.dot_general(qps[g], kt, NT_DIMS, preferred_element_type=F32)
        yield
        klane = lax.broadcasted_iota(jnp.int32, (R, SEL_BUCKET), 1) + (nk - SEL_BUCKET)
        tail = jnp.where(klane <= row_t((R, SEL_BUCKET)), sc[:, nk - SEL_BUCKET:], NEG_INF)
        sc = tail if nk == SEL_BUCKET else jnp.concatenate([sc[:, 0:nk - SEL_BUCKET], tail], axis=1)
        ms = jnp.max(sc, axis=1, keepdims=True)
        yield
        acc = jnp.dot(jnp.exp((sc - ms).astype(BF16)), vt, preferred_element_type=F32)
        yield
        osel_ref[g] = acc[:, 0:HD] / acc[:, HD:HD + 1]

    for b in range(ks_ref.shape[0] // SEL_BUCKET):
        @pl.when(q0 // SEL_BUCKET == b)
        def _(b=b):
            _interleave(*[sel_chain(g, (b + 1) * SEL_BUCKET) for g in range(NSA_GROUPS)])

    ssq = jnp.zeros((TQ, 1), F32)
    for g in range(NSA_GROUPS):
        o_sel = osel_ref[g]
        o_cmp = o_cmps[g]
        o_win = owin_ref[g]
        for j in range(NSA_J):
            h = g * NSA_J + j
            rs = slice(j * TQ, (j + 1) * TQ)
            o = (gates[:, 3 * h:3 * h + 1] * o_cmp[rs] + gates[:, 3 * h + 1:3 * h + 2] * o_sel[rs]
                 + gates[:, 3 * h + 2:3 * h + 3] * o_win[rs])
            ssq = ssq + jnp.sum(o * o, axis=1, keepdims=True)
            obuf_ref[:, h * HD:(h + 1) * HD] = o

    inv = lax.rsqrt(ssq / float(NSA_HEADS * HD) + NORM_EPS)
    o_ref[...] = (obuf_ref[...] * inv * ng_ref[...]).astype(BF16)


def _nsa(posb, invf, k_r, proj, small, kc, vc, covt, eneg, eye, ng, B, S):
    T = B * S
    nq = S // TQ
    n_blk = S // CMP_STRIDE
    n_sel = S // SEL_BLOCK
    kvb = COL_KV // 256
    assert TQ == TS_ROPE
    return pl.pallas_call(
        _nsa_kernel,
        grid=(B, nq),
        in_specs=[
            pl.BlockSpec((TQ // ROPE_SLOTS, LANE), lambda b, i: (b * nq + i, 0)),
            pl.BlockSpec((1, LANE), lambda b, i: (0, 0)),
            pl.BlockSpec((TQ, 1024), lambda b, i: (b * nq + i, 0)),
            pl.BlockSpec((1, NSA_GROUPS, n_blk, HD), lambda b, i: (b, 0, 0, 0)),
            pl.BlockSpec((1, NSA_GROUPS, n_blk, HD), lambda b, i: (b, 0, 0, 0)),
            pl.BlockSpec((S, 256), lambda b, i: (b, 1)),
            pl.BlockSpec((S, 256), lambda b, i: (b, kvb + 3)),
            pl.BlockSpec((S, 256), lambda b, i: (b, 2)),
            pl.BlockSpec((S, 256), lambda b, i: (b, kvb + 5)),
            pl.BlockSpec((TQ, LANE), lambda b, i: (b * nq + i, 0)),
            pl.BlockSpec((n_sel, n_blk), lambda b, i: (0, 0)),
            pl.BlockSpec((S, LANE), lambda b, i: (0, 0)),
            pl.BlockSpec((n_sel, LANE), lambda b, i: (0, 0)),
            pl.BlockSpec((1, 1024), lambda b, i: (0, 0)),
        ],
        out_specs=pl.BlockSpec((TQ, 1024), lambda b, i: (b * nq + i, 0)),
        out_shape=jax.ShapeDtypeStruct((T, 1024), BF16),
        scratch_shapes=[pltpu.VMEM((TQ, 1024), F32), pltpu.VMEM((NSA_GROUPS, NSA_J * TQ, HD), F32),
                        pltpu.VMEM((NSA_GROUPS, NSA_J * TQ, HD), F32),
                        pltpu.VMEM((NSA_GROUPS, NSA_J * TQ, HD), BF16)],
        compiler_params=_cparams(("parallel", "parallel")),
        name="nsa",
    )(posb, invf, proj, kc, vc, k_r, proj, k_r, proj, small, covt, eneg, eye, ng)


ML_HPS = 2
ML_ROWS = 1024


def _mlstm_kernel(bias_ref, q_ref, k_ref, v_ref, om_ref, if_ref, cwq_ref, cwk_ref, cbq_ref, cbk_ref,
                  ng_ref, o_ref, c_ref, n_ref, m_ref, xq_ref, xk_ref):
    hp = pl.program_id(1)
    L = ML_CHUNK

    @pl.when(pl.program_id(2) == 0)
    def _():
        c_ref[...] = jnp.zeros_like(c_ref)
        n_ref[...] = jnp.zeros_like(n_ref)
        m_ref[...] = jnp.zeros_like(m_ref)
        xq_ref[:, 0:8, :] = jnp.zeros((ML_HPS, 8, ML_HD), F32)
        xk_ref[:, 0:8, :] = jnp.zeros((ML_HPS, 8, ML_HD), F32)

    def conv_silu(x_ref, hh, rows, xb_ref, w_ref, b_ref):
        cs = slice(hh * ML_HD, (hh + 1) * ML_HD)
        xb_ref[hh, 8:8 + L, :] = x_ref[rows, cs].astype(F32)
        y = b_ref[:, cs] + xb_ref[hh, 8:8 + L, :] * w_ref[CONV_W - 1:CONV_W, cs]
        for k in range(1, CONV_W):
            y = y + xb_ref[hh, 8 - k:8 - k + L, :] * w_ref[CONV_W - 1 - k:CONV_W - k, cs]
        xb_ref[hh, 0:8, :] = xb_ref[hh, L:L + 8, :]
        return _silu(y)

    r = lax.broadcasted_iota(jnp.int32, (L, L), 0)
    cidx = lax.broadcasted_iota(jnp.int32, (L, L), 1)
    tril = cidx <= r
    eye = cidx == r

    def head_chain(hh, rows):
        h = hp * ML_HPS + hh
        cs = slice(hh * ML_HD, (hh + 1) * ML_HD)
        qf = conv_silu(q_ref, hh, rows, xq_ref, cwq_ref, cbq_ref) * (ML_HD ** -0.5)
        yield
        kf = conv_silu(k_ref, hh, rows, xk_ref, cwk_ref, cbk_ref)
        yield
        qb = qf.astype(BF16)
        kb = kf.astype(BF16)
        vb = v_ref[rows, cs]
        vf = vb.astype(F32)

        ic = if_ref[0, hh, 0:1, rows] + bias_ref[0, h]
        fp = if_ref[0, hh, 1:2, rows] + bias_ref[1, h]
        fc = jnp.minimum(fp, 0.0) - jnp.log(1.0 + jnp.exp(-jnp.abs(fp)))

        fc_b = jnp.broadcast_to(fc, (L, L))
        ic_b = jnp.broadcast_to(ic, (L, L))
        b_col = jnp.sum(jnp.where(tril, fc_b, 0.0), axis=1, keepdims=True)
        fc_col = jnp.sum(jnp.where(eye, fc_b, 0.0), axis=1, keepdims=True)
        ic_col = jnp.sum(jnp.where(eye, ic_b, 0.0), axis=1, keepdims=True)
        b_row = jnp.sum(jnp.where(r <= cidx, jnp.broadcast_to(fc_col, (L, L)), 0.0),
                        axis=0, keepdims=True)
        b_last = b_col[L - 1:L, :]
        m_prev = m_ref[hh]
        yield

        d_log = jnp.where(tril, b_col - b_row + ic, NEG_INF)
        inter = b_col + m_prev
        m_t = jnp.maximum(inter, jnp.max(d_log, axis=1, keepdims=True))
        w_intra = jnp.exp(d_log - m_t)
        w_inter = jnp.exp(inter - m_t)
        qk = lax.dot_general(qb, kb, NT_DIMS, preferred_element_type=F32) * w_intra
        yield
        num = (jnp.dot(qk.astype(BF16), vb, preferred_element_type=F32)
               + w_inter * lax.dot_general(qb, c_ref[hh].astype(BF16), NT_DIMS, preferred_element_type=F32))
        den = jnp.sum(qk, axis=1, keepdims=True) + w_inter * jnp.sum(qf * n_ref[hh], axis=1, keepdims=True)
        hm = num / jnp.maximum(jnp.abs(den), jnp.exp(-m_t))
        yield

        w_log = b_last - b_col + ic_col
        m_new = jnp.maximum(b_last + m_prev, jnp.max(w_log, axis=0, keepdims=True))
        w_state = jnp.exp(w_log - m_new)
        decay = jnp.exp(b_last + m_prev - m_new)
        c_ref[hh] = decay * c_ref[hh] + lax.dot_general((w_state * vf).astype(BF16), kb, TN_DIMS,
                                                        preferred_element_type=F32)
        n_ref[hh] = decay * n_ref[hh] + jnp.sum(w_state * kf, axis=0, keepdims=True)
        m_ref[hh] = m_new
        yield

        hn = hm * lax.rsqrt(jnp.mean(hm * hm, axis=1, keepdims=True) + NORM_EPS) * ng_ref[:, cs]
        o_ref[rows, cs] = (hn * _sigmoid(om_ref[rows, cs].astype(F32))).astype(BF16)

    for cc in range(ML_ROWS // L):
        _interleave(*[head_chain(hh, slice(cc * L, (cc + 1) * L)) for hh in range(ML_HPS)])


def _mlstm(proj, if_arr, gate_bias, conv_w, conv_b, ng, B, S):
    T = B * S
    nc = S // ML_ROWS
    W = ML_HPS * ML_HD
    n_hp = ML_HEADS // ML_HPS
    assert COL_QKM % W == 0 and COL_VM % W == 0 and COL_OM % W == 0 and ML_HEADS % ML_HPS == 0
    cq, ck, cv, co = COL_QKM // W, COL_QKM // W + n_hp, COL_VM // W, COL_OM // W
    rows = lambda col0: pl.BlockSpec((ML_ROWS, W), lambda b, h, c: (b * nc + c, col0 + h))
    return pl.pallas_call(
        _mlstm_kernel,
        grid=(B, n_hp, nc),
        in_specs=[
            pl.BlockSpec(memory_space=pltpu.SMEM),
            rows(cq), rows(ck), rows(cv), rows(co),
            pl.BlockSpec((1, ML_HPS, 2, ML_ROWS), lambda b, h, c: (b, h, 0, c)),
            pl.BlockSpec((CONV_W, W), lambda b, h, c: (0, h)),
            pl.BlockSpec((CONV_W, W), lambda b, h, c: (0, n_hp + h)),
            pl.BlockSpec((1, W), lambda b, h, c: (0, h)),
            pl.BlockSpec((1, W), lambda b, h, c: (0, n_hp + h)),
            pl.BlockSpec((1, W), lambda b, h, c: (0, h)),
        ],
        out_specs=pl.BlockSpec((ML_ROWS, W), lambda b, h, c: (b * nc + c, h)),
        out_shape=jax.ShapeDtypeStruct((T, ML_HEADS * ML_HD), BF16),
        scratch_shapes=[
            pltpu.VMEM((ML_HPS, ML_HD, ML_HD), F32), pltpu.VMEM((ML_HPS, 1, ML_HD), F32),
            pltpu.VMEM((ML_HPS, 1, 1), F32),
            pltpu.VMEM((ML_HPS, ML_CHUNK + 8, ML_HD), F32), pltpu.VMEM((ML_HPS, ML_CHUNK + 8, ML_HD), F32),
        ],
        compiler_params=_cparams(("parallel", "parallel", "arbitrary")),
        name="mlstm",
    )(gate_bias, proj, proj, proj, proj, if_arr, conv_w, conv_w, conv_b, conv_b, ng)


def _pack_bf16_pair(lo, hi):
    lo_b = pltpu.bitcast(lo.astype(BF16).astype(F32), U32)
    hi_b = pltpu.bitcast(hi.astype(BF16).astype(F32), U32)
    return (lo_b >> 16) | hi_b


def _unpack_bf16_pair(p):
    lo = pltpu.bitcast(p << 16, F32)
    hi = pltpu.bitcast(p & jnp.uint32(0xFFFF0000), F32)
    return lo, hi


def _row(ref, r):
    return ref.at[pl.ds(r, 1), :]


def _outproj_kernel(nsa_ref, ml_ref, w_ref, x_ref, g2_ref, wr_ref, br_ref, tril_ref,
                    x1_ref, xp_ref, rt_ref, rtt_ref, cnt_ref, carry_ref, xa_ref, xb_ref):
    i = pl.program_id(0)

    @pl.when(i == 0)
    def _():
        carry_ref[...] = jnp.zeros_like(carry_ref)
        xb_ref[...] = jnp.zeros_like(xb_ref)

    half = D_MODEL // 2
    subs = [slice(sub * (TM_OUT // OUT_SUBS), (sub + 1) * (TM_OUT // OUT_SUBS)) for sub in range(OUT_SUBS)]

    def step(cur_ref, prev_ref):
        carry = carry_ref[...]
        carry_box = [carry]

        def chain():
            for rs in subs:
                yield from _outproj_rows(rs, prev_ref, carry_box, g2_ref, wr_ref, br_ref, tril_ref,
                                         x1_ref, xp_ref, rt_ref, rtt_ref)
                yield

        pending = chain()
        for rs in subs:
            for c in range(D_MODEL // OUT_COLS):
                cs = slice(c * OUT_COLS, (c + 1) * OUT_COLS)
                cur_ref[rs, cs] = (x_ref[rs, cs]
                                   + jnp.dot(nsa_ref[rs, :], w_ref[0:half, cs], preferred_element_type=F32)
                                   + jnp.dot(ml_ref[rs, :], w_ref[half:, cs], preferred_element_type=F32))
                next(pending, None)
        for _ in pending:
            pass
        carry = jnp.where(i > 0, carry_box[0], carry)
        carry_ref[...] = carry
        cnt_ref[...] = jnp.broadcast_to(carry, cnt_ref.shape)

    @pl.when(i % 2 == 0)
    def _():
        step(xa_ref, xb_ref)

    @pl.when(i % 2 == 1)
    def _():
        step(xb_ref, xa_ref)


def _outproj_rows(rs, src_ref, carry_box, g2_ref, wr_ref, br_ref, tril_ref, x1_ref, xp_ref, rt_ref, rtt_ref):
    half = D_MODEL // 2
    x1 = src_ref[rs, :]
    x1_ref[rs, :] = x1
    xn = x1 * lax.rsqrt(jnp.mean(x1 * x1, axis=-1, keepdims=True) + NORM_EPS) * g2_ref[...]
    xp_ref[rs, :] = _pack_bf16_pair(xn[:, :half], xn[:, half:])
    logits = jnp.dot(xn.astype(BF16), wr_ref[...], preferred_element_type=F32) + br_ref[...]
    yield

    tm = logits.shape[0]
    lane = lax.broadcasted_iota(jnp.int32, (tm, LANE), 1)
    lane_f = lane.astype(F32)
    big = float(LANE)
    gmask = lane < MOE_GROUPS
    gmax = jnp.max(jnp.where(gmask, logits, NEG_INF), axis=1, keepdims=True)
    ge = jnp.where(gmask, jnp.exp(logits - gmax), 0.0)
    gp = ge / jnp.sum(ge, axis=1, keepdims=True)
    g_w = jnp.max(gp, axis=1, keepdims=True)
    g_idx = jnp.min(jnp.where(gmask, jnp.where(gp == g_w, lane_f, big), big), axis=1, keepdims=True)
    yield
    grp_of_lane = ((lane - MOE_GROUPS) >> EPG_LOG2).astype(F32)
    emask = jnp.where(lane >= MOE_GROUPS, grp_of_lane, -1.0) == g_idx
    emax = jnp.max(jnp.where(emask, logits, NEG_INF), axis=1, keepdims=True)
    ee = jnp.where(emask, jnp.exp(logits - emax), 0.0)
    ep = jnp.where(emask, ee / jnp.sum(ee, axis=1, keepdims=True), -1.0)
    v1 = jnp.max(ep, axis=1, keepdims=True)
    i1 = jnp.min(jnp.where(ep == v1, lane_f, big), axis=1, keepdims=True)
    ep2 = jnp.where(lane_f == i1, -1.0, ep)
    v2 = jnp.max(ep2, axis=1, keepdims=True)
    i2 = jnp.min(jnp.where(ep2 == v2, lane_f, big), axis=1, keepdims=True)
    w0 = g_w * v1 / (v1 + v2)
    w1 = g_w * v2 / (v1 + v2)
    e0 = i1 - float(MOE_GROUPS)
    e1 = i2 - float(MOE_GROUPS)

    yield
    oh0 = jnp.where(lane_f == e0, 1.0, 0.0)
    oh1 = jnp.where(lane_f == e1, 1.0, 0.0)
    tril = tril_ref[0:tm, 0:tm]
    pre0 = jnp.dot(tril, oh0.astype(BF16), preferred_element_type=F32)
    pre1 = jnp.dot(tril, oh1.astype(BF16), preferred_element_type=F32)
    yield
    carry = carry_box[0]
    tot0 = pre0[tm - 1:tm, :]
    tot1 = pre1[tm - 1:tm, :]
    rank0 = jnp.sum(oh0 * (pre0 - 1.0 + carry), axis=1, keepdims=True)
    rank1 = jnp.sum(oh1 * (pre1 - 1.0 + carry + tot0), axis=1, keepdims=True)

    rt = jnp.where(lane == 0, e0, jnp.where(lane == 1, e1, jnp.where(lane == 2, w0, jnp.where(
        lane == 3, w1, jnp.where(lane == 4, rank0, jnp.where(lane == 5, rank1, 0.0))))))
    rt_ref[rs, :] = rt
    rtt_ref[:, rs] = jnp.transpose(rt)
    carry_box[0] = carry + tot0 + tot1


def _out_proj(nsa_o, ml_o, w_out, x2, g2, w_r, b_r, tril):
    T = x2.shape[0]
    half = D_MODEL // 2
    n_tiles = T // TM_OUT
    cur = lambda i: (jnp.minimum(i, n_tiles - 1), 0)
    prev = lambda i: (jnp.maximum(i - 1, 0), 0)
    return pl.pallas_call(
        _outproj_kernel,
        grid=(n_tiles + 1,),
        in_specs=[
            pl.BlockSpec((TM_OUT, half), cur),
            pl.BlockSpec((TM_OUT, half), cur),
            pl.BlockSpec((D_MODEL, D_MODEL), lambda i: (0, 0)),
            pl.BlockSpec((TM_OUT, D_MODEL), cur),
            pl.BlockSpec((1, D_MODEL), lambda i: (0, 0)),
            pl.BlockSpec((D_MODEL, LANE), lambda i: (0, 0)),
            pl.BlockSpec((1, LANE), lambda i: (0, 0)),
            pl.BlockSpec((TM_OUT, TM_OUT), lambda i: (0, 0)),
        ],
        out_specs=[
            pl.BlockSpec((TM_OUT, D_MODEL), prev),
            pl.BlockSpec((TM_OUT, half), prev),
            pl.BlockSpec((TM_OUT, LANE), prev),
            pl.BlockSpec((LANE, TM_OUT), lambda i: (0, jnp.maximum(i - 1, 0))),
            pl.BlockSpec((8, LANE), lambda i: (0, 0)),
        ],
        out_shape=[
            jax.ShapeDtypeStruct((T, D_MODEL), F32),
            jax.ShapeDtypeStruct((T, half), U32),
            jax.ShapeDtypeStruct((T, LANE), F32),
            jax.ShapeDtypeStruct((LANE, T), F32),
            jax.ShapeDtypeStruct((8, LANE), F32),
        ],
        scratch_shapes=[pltpu.VMEM((1, LANE), F32), pltpu.VMEM((TM_OUT, D_MODEL), F32),
                        pltpu.VMEM((TM_OUT, D_MODEL), F32)],
        compiler_params=_cparams(("arbitrary",)),
        name="out_proj",
    )(nsa_o, ml_o, w_out, x2, g2, w_r, b_r, tril)


TD = 256
DISPATCH_BUFS = 3
DISPATCH_UNROLL = 8


def _dispatch_kernel(d0_ref, d1_ref, zblk_ref, meta_ref, xp_hbm, xs_hbm, zbuf_ref, zsem_ref, tbuf_ref,
                     lsem_ref, rsem_ref):
    dests = (d0_ref, d1_ref)
    n_assign = 2 * d0_ref.shape[0]
    nb = xs_hbm.shape[0] // BM
    n_used = meta_ref[0]
    zbuf_ref[...] = jnp.zeros_like(zbuf_ref)

    def zero_block(blk):
        return pltpu.make_async_copy(zbuf_ref, xs_hbm.at[pl.ds(blk * BM, BM), :], zsem_ref.at[0])

    def for_zero_blocks(fn):
        for e in range(N_EXPERTS):
            @pl.when(zblk_ref[e] >= 0)
            def _(e=e):
                fn(zero_block(zblk_ref[e]))

        def body(blk, _):
            fn(zero_block(blk))
            return 0
        lax.fori_loop(n_used, nb, body, 0)

    for_zero_blocks(lambda c: c.start())
    for_zero_blocks(lambda c: c.wait())

    n_tiles = n_assign // (2 * TD)

    def load(tile, slot):
        return pltpu.make_async_copy(xp_hbm.at[pl.ds(tile * TD, TD), :], tbuf_ref.at[slot], lsem_ref.at[slot])

    def wait_scatter(slot):
        for _ in range(2):
            pltpu.make_async_copy(tbuf_ref.at[slot], tbuf_ref.at[slot], rsem_ref.at[slot]).wait()

    load(0, 0).start()

    def tile_step(i, _):
        slot = i % DISPATCH_BUFS
        nslot = (i + 1) % DISPATCH_BUFS

        @pl.when(i + 1 < n_tiles)
        def _():
            @pl.when(i + 1 >= DISPATCH_BUFS)
            def _():
                wait_scatter(nslot)
            load(i + 1, nslot).start()

        load(i, slot).wait()

        def body(j, _):
            for u in range(DISPATCH_UNROLL):
                r = j * DISPATCH_UNROLL + u
                for k in range(2):
                    d = dests[k][i * TD + r]
                    pltpu.make_async_copy(_row(tbuf_ref.at[slot], r), _row(xs_hbm, d),
                                          rsem_ref.at[slot]).start(priority=k)
            return 0
        lax.fori_loop(0, TD // DISPATCH_UNROLL, body, 0)
        return 0

    lax.fori_loop(0, n_tiles, tile_step, 0)
    for back in range(min(DISPATCH_BUFS, n_tiles)):
        wait_scatter((n_tiles - 1 - back) % DISPATCH_BUFS)


def _dispatch(dest0, dest1, zblk, meta, xp, n_rows):
    half = D_MODEL // 2
    assert dest0.shape == dest1.shape and dest0.shape[0] % TD == 0 and dest0.shape[0] // TD >= DISPATCH_BUFS
    grid_spec = pltpu.PrefetchScalarGridSpec(
        num_scalar_prefetch=4,
        grid=(1,),
        in_specs=[pl.BlockSpec(memory_space=pl.ANY)],
        out_specs=pl.BlockSpec(memory_space=pl.ANY),
        scratch_shapes=[
            pltpu.VMEM((BM, half), U32),
            pltpu.SemaphoreType.DMA((1,)),
            pltpu.VMEM((DISPATCH_BUFS, TD, half), U32),
            pltpu.SemaphoreType.DMA((DISPATCH_BUFS,)),
            pltpu.SemaphoreType.DMA((DISPATCH_BUFS,)),
        ],
    )
    return pl.pallas_call(
        _dispatch_kernel,
        grid_spec=grid_spec,
        out_shape=jax.ShapeDtypeStruct((n_rows, half), U32),
        compiler_params=_cparams(("arbitrary",)),
        name="dispatch",
    )(dest0, dest1, zblk, meta, xp)


def _expert_kernel(be_ref, nxt_ref, par_ref, meta_ref, x_ref, wg_hbm, wu_hbm, wd_hbm, y_ref,
                   wsg_ref, wsu_ref, wsd_ref, wsem_ref, wgb_ref, wub_ref, wdb_ref):
    i = pl.program_id(0)
    n_used = meta_ref[0]
    half = D_MODEL // 2

    def weight_copies(e, p):
        return (pltpu.make_async_copy(wg_hbm.at[e], wsg_ref.at[p], wsem_ref.at[p, 0]),
                pltpu.make_async_copy(wu_hbm.at[e], wsu_ref.at[p], wsem_ref.at[p, 1]),
                pltpu.make_async_copy(wd_hbm.at[e], wsd_ref.at[p], wsem_ref.at[p, 2]))

    def succ(e):
        return jnp.where(e >= 0, nxt_ref[jnp.maximum(e, 0)], -1)

    def start_weights(e, p):
        @pl.when(e >= 0)
        def _():
            for c in weight_copies(e, p):
                c.start(priority=1)

    def wait_staged(p):
        for c in weight_copies(0, p):
            c.wait()

    def cast_staged(p):
        wgb_ref[p] = wsg_ref[p].astype(BF16)
        wub_ref[p] = wsu_ref[p].astype(BF16)
        wdb_ref[p] = wsd_ref[p].astype(BF16)

    @pl.when(i == 0)
    def _():
        e0 = be_ref[0]
        start_weights(e0, 0)
        start_weights(succ(e0), 1)
        wait_staged(0)
        cast_staged(0)
        start_weights(succ(succ(e0)), 0)

    def step(p, cast_next):
        if cast_next:
            wait_staged(1 - p)
        lo, hi = _unpack_bf16_pair(x_ref[...])
        xl = lo.astype(BF16)
        xh = hi.astype(BF16)
        gt = (jnp.dot(xl, wgb_ref[p, 0:half, :], preferred_element_type=F32)
              + jnp.dot(xh, wgb_ref[p, half:, :], preferred_element_type=F32))
        up = (jnp.dot(xl, wub_ref[p, 0:half, :], preferred_element_type=F32)
              + jnp.dot(xh, wub_ref[p, half:, :], preferred_element_type=F32))
        hb = (_silu(gt) * up).astype(BF16)
        y = jnp.dot(hb, wdb_ref[p], preferred_element_type=F32)
        y_ref[...] = _pack_bf16_pair(y[:, :half], y[:, half:])
        if cast_next:
            cast_staged(1 - p)
            start_weights(succ(succ(succ(be_ref[i]))), 1 - p)

    e = be_ref[jnp.minimum(i, n_used - 1)]
    ends_expert = (i + 1 < n_used) & (be_ref[jnp.minimum(i + 1, n_used - 1)] != e)
    for p in range(2):
        for cast_next in (False, True):
            @pl.when((i < n_used) & (par_ref[i] == p)
                     & (ends_expert if cast_next else jnp.logical_not(ends_expert)))
            def _(p=p, cast_next=cast_next):
                step(p, cast_next)

    @pl.when(i >= n_used)
    def _():
        y_ref[...] = jnp.zeros_like(y_ref)


def _experts(block_expert, next_expert, block_parity, meta, xs, w_gate, w_up, w_down):
    half = D_MODEL // 2
    n_rows = xs.shape[0]
    nb = n_rows // BM
    grid_spec = pltpu.PrefetchScalarGridSpec(
        num_scalar_prefetch=4,
        grid=(nb,),
        in_specs=[
            pl.BlockSpec((BM, half), lambda i, be, nxt, par, meta: (jnp.minimum(i, meta[0] - 1), 0)),
            pl.BlockSpec(memory_space=pl.ANY), pl.BlockSpec(memory_space=pl.ANY),
            pl.BlockSpec(memory_space=pl.ANY),
        ],
        out_specs=pl.BlockSpec((BM, half), lambda i, be, nxt, par, meta: (i, 0)),
        scratch_shapes=[
            pltpu.VMEM((2, D_MODEL, D_EXPERT), F32),
            pltpu.VMEM((2, D_MODEL, D_EXPERT), F32),
            pltpu.VMEM((2, D_EXPERT, D_MODEL), F32),
            pltpu.SemaphoreType.DMA((2, 3)),
            pltpu.VMEM((2, D_MODEL, D_EXPERT), BF16),
            pltpu.VMEM((2, D_MODEL, D_EXPERT), BF16),
            pltpu.VMEM((2, D_EXPERT, D_MODEL), BF16),
        ],
    )
    return pl.pallas_call(
        _expert_kernel,
        grid_spec=grid_spec,
        out_shape=jax.ShapeDtypeStruct((n_rows, half), U32),
        compiler_params=_cparams(("arbitrary",)),
        name="experts",
    )(block_expert, next_expert, block_parity, meta, xs, w_gate, w_up, w_down)


def _combine_kernel(d0_ref, d1_ref, y_hbm, x1_ref, rt_ref, fg_ref, o_ref, ya_ref, yb_ref, sem_ref):
    i = pl.program_id(0)
    nt = pl.num_programs(0)
    half = D_MODEL // 2
    bufs = (ya_ref, yb_ref)
    dests = (d0_ref, d1_ref)

    def row_copy(tile, slot, r_tile, s, k):
        r = r_tile * 8 + s
        d = dests[k][tile * TC + r]
        return pltpu.make_async_copy(_row(y_hbm, d), _row(bufs[slot].at[k], r), sem_ref.at[slot])

    def wait_rows(slot):
        pltpu.make_async_copy(bufs[slot], bufs[slot], sem_ref.at[slot]).wait()

    def step(slot):
        wait_rows(slot)
        for r in range(TC):
            for k in range(2):
                row_copy(i + 1, 1 - slot, r // 8, r % 8, k).start(priority=k)
        rt = rt_ref[...]
        w0 = rt[:, 2:3]
        w1 = rt[:, 3:4]
        lo0, hi0 = _unpack_bf16_pair(bufs[slot][0])
        lo1, hi1 = _unpack_bf16_pair(bufs[slot][1])
        xl = x1_ref[:, :half] + (w0 * lo0 + w1 * lo1)
        xh = x1_ref[:, half:] + (w0 * hi0 + w1 * hi1)
        ms = (jnp.sum(xl * xl, axis=1, keepdims=True) + jnp.sum(xh * xh, axis=1, keepdims=True)) / float(D_MODEL)
        inv = lax.rsqrt(ms + NORM_EPS)
        o_ref[:, :half] = xl * inv * fg_ref[:, :half]
        o_ref[:, half:] = xh * inv * fg_ref[:, half:]

        @pl.when(i == nt - 1)
        def _():
            wait_rows(1 - slot)

    @pl.when(i == 0)
    def _():
        def body(rt, _):
            for s in range(8):
                for k in range(2):
                    row_copy(0, 0, rt, s, k).start(priority=k)
            return 0
        lax.fori_loop(0, TC // 8, body, 0)

    @pl.when(i % 2 == 0)
    def _():
        step(0)

    @pl.when(i % 2 == 1)
    def _():
        step(1)


def _combine(dest0, dest1, ys, x1, route, fg):
    T = x1.shape[0]
    half = D_MODEL // 2
    grid_spec = pltpu.PrefetchScalarGridSpec(
        num_scalar_prefetch=2,
        grid=(T // TC,),
        in_specs=[
            pl.BlockSpec(memory_space=pl.ANY),
            pl.BlockSpec((TC, D_MODEL), lambda i, d0, d1: (i, 0)),
            pl.BlockSpec((TC, LANE), lambda i, d0, d1: (i, 0)),
            pl.BlockSpec((1, D_MODEL), lambda i, d0, d1: (0, 0)),
        ],
        out_specs=pl.BlockSpec((TC, D_MODEL), lambda i, d0, d1: (i, 0)),
        scratch_shapes=[
            pltpu.VMEM((2, TC, half), U32),
            pltpu.VMEM((2, TC, half), U32),
            pltpu.SemaphoreType.DMA((2,)),
        ],
    )
    return pl.pallas_call(
        _combine_kernel,
        grid_spec=grid_spec,
        out_shape=jax.ShapeDtypeStruct((T, D_MODEL), F32),
        compiler_params=_cparams(("arbitrary",)),
        name="combine",
    )(dest0, dest1, ys, x1, route, fg)


def _cover_matrix(S):
    n_blk = S // CMP_STRIDE
    n_sel = S // SEL_BLOCK
    cs = np.arange(n_blk) * CMP_STRIDE
    ss = np.arange(n_sel) * SEL_BLOCK
    shared = np.minimum(cs[:, None] + CMP_LEN, ss[None, :] + SEL_BLOCK) - np.maximum(cs[:, None], ss[None, :])
    return (np.clip(shared, 0, None) / CMP_LEN).T.astype(np.float32)


def _block_mask_matrix(S):
    n_sel = S // SEL_BLOCK
    assert n_sel <= LANE
    e = np.zeros((S, LANE), np.float32)
    e[np.arange(S), np.arange(S) // SEL_BLOCK] = NEG_INF
    return e


def _inv_freq_row():
    inv = np.power(np.float32(ROPE_THETA), -np.arange(ROPE_HALF, dtype=np.float32) * 2.0 / ROPE_DIM)
    return np.tile(inv, ROPE_SLOTS).reshape(1, LANE).astype(np.float32)


def _packed_positions(positions, T):
    rows = TS_ROPE // ROPE_SLOTS
    p = positions.reshape(T // TS_ROPE, ROPE_SLOTS, rows).transpose(0, 2, 1).astype(F32)
    return jnp.repeat(p, ROPE_HALF, axis=2).reshape(T // ROPE_SLOTS, LANE)


def _layer(x2, positions, B, S, norm1_g, w_in, cmp_pe_k, cmp_pe_v, cmp_wk1, cmp_wk2, cmp_wv1, cmp_wv2,
           nsa_norm_g, conv_w, conv_b, b_igate, b_fgate, mlstm_norm_g, w_out, norm2_g,
           w_group, b_group, w_router, b_router, w_exp_gate, w_exp_up, w_exp_down, out_norm_g):
    T = B * S
    w_main, w_small = _weight_prep(jnp.transpose(w_in[0]))
    posb = _packed_positions(positions, T)
    invf = jnp.asarray(_inv_freq_row())

    proj, small, small_t = _in_proj(x2, norm1_g.reshape(1, -1), w_main, w_small)
    k_r = _rope(posb, invf, proj)
    kc, vc = _compress(k_r, proj, B, S, cmp_pe_k, cmp_pe_v, cmp_wk1.astype(BF16), cmp_wk2.astype(BF16),
                       cmp_wv1.astype(BF16), cmp_wv2.astype(BF16))
    eye = np.eye(S // SEL_BLOCK, LANE, dtype=np.float32)
    nsa_o = _nsa(posb, invf, k_r, proj, small, kc, vc, jnp.asarray(_cover_matrix(S), BF16),
                 jnp.asarray(_block_mask_matrix(S), BF16), jnp.asarray(eye, BF16),
                 nsa_norm_g.reshape(1, -1), B, S)

    if_arr = small_t[24:32].reshape(2, ML_HEADS, B, S).transpose(2, 1, 0, 3)
    gate_bias = jnp.stack([b_igate, b_fgate]).astype(F32)
    ml_o = _mlstm(proj, if_arr, gate_bias, conv_w, conv_b.reshape(1, -1), mlstm_norm_g.reshape(1, -1), B, S)

    w_r = jnp.concatenate([w_group, w_router, jnp.zeros((D_MODEL, LANE - MOE_GROUPS - N_EXPERTS), F32)],
                          axis=1).astype(BF16)
    b_r = jnp.concatenate([b_group, b_router, jnp.zeros((LANE - MOE_GROUPS - N_EXPERTS,), F32)]).reshape(1, LANE)
    tril = jnp.asarray(np.tril(np.ones((TM_OUT, TM_OUT), np.float32)), BF16)
    x1, xp, route, route_t, cnt = _out_proj(nsa_o, ml_o, w_out.astype(BF16), x2, norm2_g.reshape(1, -1),
                                            w_r, b_r, tril)

    n_rows = T * 2 + N_EXPERTS * BM
    counts = cnt[0, :N_EXPERTS].astype(jnp.int32)
    pcounts = (counts + BM - 1) // BM * BM
    pends = jnp.cumsum(pcounts)
    pstarts = pends - pcounts
    start_blk = (pstarts // BM).astype(F32)

    def sorted_row(k):
        onehot = (route_t[k].astype(jnp.int32)[:, None] == jnp.arange(N_EXPERTS, dtype=jnp.int32)).astype(F32)
        return (jnp.einsum('te,e->t', onehot, start_blk).astype(jnp.int32) * BM
                + route_t[4 + k].astype(jnp.int32))

    dests = (sorted_row(0), sorted_row(1))
    block_row0 = jnp.arange(n_rows // BM, dtype=jnp.int32) * BM
    block_expert = jnp.minimum(jnp.sum((pends[None, :] <= block_row0[:, None]).astype(jnp.int32), axis=1),
                               N_EXPERTS - 1).astype(jnp.int32)
    present = jnp.where(counts > 0, jnp.arange(N_EXPERTS, dtype=jnp.int32), N_EXPERTS)
    later = jnp.concatenate([lax.cummin(present[::-1])[::-1][1:], jnp.full((1,), N_EXPERTS, jnp.int32)])
    next_expert = jnp.where(later < N_EXPERTS, later, -1).astype(jnp.int32)
    meta = (pends[-1:] // BM).astype(jnp.int32)

    last_block = jnp.where(counts > 0, pends // BM - 1, -1).astype(jnp.int32)

    xs = _dispatch(dests[0], dests[1], last_block, meta, xp, n_rows)
    ordinal = jnp.cumsum((counts > 0).astype(jnp.int32)) - 1
    block_parity = (ordinal[block_expert] % 2).astype(jnp.int32)
    ys = _experts(block_expert, next_expert, block_parity, meta, xs, w_exp_gate, w_exp_up, w_exp_down)
    pad = jnp.zeros((TC,), jnp.int32)
    return _combine(jnp.concatenate([dests[0], pad]), jnp.concatenate([dests[1], pad]), ys, x1, route,
                    out_norm_g.reshape(1, -1))


def kernel(x, positions, norm1_g, w_in, cmp_pe_k, cmp_pe_v, cmp_wk1, cmp_wk2, cmp_wv1, cmp_wv2, nsa_norm_g,
           conv_w, conv_b, b_igate, b_fgate, mlstm_norm_g, w_out, norm2_g, w_group, b_group, w_router,
           b_router, w_exp_gate, w_exp_up, w_exp_down, final_norm_g):
    B, S, D = x.shape
    assert D == D_MODEL and norm1_g.shape[0] == 1, "single-layer, D_MODEL-wide configuration only"
    assert S % ML_ROWS == 0 and ML_ROWS % ML_CHUNK == 0 and S % SEL_BUCKET == 0 and S >= WIN_KEYS and (B * S) % TM_IN == 0
    out = _layer(x.reshape(B * S, D), positions, B, S, norm1_g[0], w_in, cmp_pe_k[0], cmp_pe_v[0],
                 cmp_wk1[0], cmp_wk2[0], cmp_wv1[0], cmp_wv2[0], nsa_norm_g[0], conv_w[0], conv_b[0],
                 b_igate[0], b_fgate[0], mlstm_norm_g[0], w_out[0], norm2_g[0], w_group[0], b_group[0],
                 w_router[0], b_router[0], w_exp_gate[0], w_exp_up[0], w_exp_down[0], final_norm_g)
    return out.reshape(B, S, D)
```
